```python
import math
import jax, jax.numpy as jnp
from jax import lax
import numpy as np

D_MODEL = 1024
BATCH = 8
SEQ = 4096
DEPTH = 4

CHUNK = 64
Q_BLOCK = 128
N_MIXERS = 3

MLA_HEADS = 16
QK_NOPE = 64
QK_ROPE = 32
V_HEAD = 64
Q_LORA = 384
KV_LORA = 256
ROPE_THETA = 10000.0

CONV_WIDTH = 31

POOL_WINDOWS = (2, 4, 8, 16)
POOL_GROUPS = len(POOL_WINDOWS)

D_FF = 4 * D_MODEL

NORM_EPS = 1e-6
NEG_INF = -1e30

kernel_name = "hybrid_mla_conformer_pool_trunk"


def _rmsnorm(x, g):
    x32 = x.astype(jnp.float32)
    y = x32 * lax.rsqrt(jnp.mean(x32 * x32, axis=-1, keepdims=True) + NORM_EPS)
    return (y * g.astype(jnp.float32)).astype(x.dtype)


def _layernorm(x, g, b):
    x32 = x.astype(jnp.float32)
    mu = jnp.mean(x32, axis=-1, keepdims=True)
    xc = x32 - mu
    y = xc * lax.rsqrt(jnp.mean(xc * xc, axis=-1, keepdims=True) + NORM_EPS)
    return (y * g.astype(jnp.float32) + b.astype(jnp.float32)).astype(x.dtype)


def _rope_tables(positions):
    inv_freq = ROPE_THETA ** (-jnp.arange(0, QK_ROPE, 2, dtype=jnp.float32) / QK_ROPE)
    ang = positions.astype(jnp.float32)[..., None] * inv_freq
    return jnp.cos(ang), jnp.sin(ang)


def _apply_rope(x, cos, sin):
    half = x.shape[-1] // 2
    x1 = x[..., :half].astype(jnp.float32)
    x2 = x[..., half:].astype(jnp.float32)
    out = jnp.concatenate([x1 * cos - x2 * sin, x1 * sin + x2 * cos], axis=-1)
    return out.astype(x.dtype)


def _chunk_causal_attention(q, k, v):
    B, S, H, Dk = q.shape
    nb = S // Q_BLOCK
    qb = q.reshape(B, nb, Q_BLOCK, H, Dk).swapaxes(0, 1)
    k_chunk = jnp.arange(S) // CHUNK
    scale = 1.0 / math.sqrt(Dk)

    def one_block(args):
        qblk, bi = args
        q_chunk = (bi * Q_BLOCK + jnp.arange(Q_BLOCK)) // CHUNK
        mask = k_chunk[None, :] <= q_chunk[:, None]
        s = jnp.einsum('bqhd,bkhd->bhqk', qblk, k).astype(jnp.float32) * scale
        s = jnp.where(mask[None, None], s, NEG_INF)
        p = jax.nn.softmax(s, axis=-1).astype(v.dtype)
        return jnp.einsum('bhqk,bkhd->bqhd', p, v)

    o = lax.map(one_block, (qb, jnp.arange(nb)))
    return o.swapaxes(0, 1).reshape(B, S, H, v.shape[-1])


def _mla(h, positions, w_dq, q_norm_g, w_uq, w_dkv, kv_norm_g, w_ukv, w_o):
    B, S, _ = h.shape
    cq = _rmsnorm(h @ w_dq, q_norm_g)
    q = (cq @ w_uq).reshape(B, S, MLA_HEADS, QK_NOPE + QK_ROPE)
    q_nope, q_rope = q[..., :QK_NOPE], q[..., QK_NOPE:]
    ckv_all = h @ w_dkv
    ckv = _rmsnorm(ckv_all[..., :KV_LORA], kv_norm_g)
    k_rope = ckv_all[..., KV_LORA:]
    kv = (ckv @ w_ukv).reshape(B, S, MLA_HEADS, QK_NOPE + V_HEAD)
    k_nope, v = kv[..., :QK_NOPE], kv[..., QK_NOPE:]
    cos, sin = _rope_tables(positions)
    q_rope = _apply_rope(q_rope, cos[:, :, None], sin[:, :, None])
    k_rope = _apply_rope(k_rope, cos, sin)
    qf = jnp.concatenate([q_nope, q_rope], axis=-1)
    kf = jnp.concatenate(
        [k_nope, jnp.broadcast_to(k_rope[:, :, None], (B, S, MLA_HEADS, QK_ROPE))], axis=-1)
    o = _chunk_causal_attention(qf, kf, v)
    return o.reshape(B, S, MLA_HEADS * V_HEAD) @ w_o


def _conformer_conv(h, w_pw1, b_pw1, w_dw, b_dw, ln_g, ln_b, w_pw2, b_pw2):
    D = h.shape[-1]
    a = h @ w_pw1 + b_pw1
    u = a[..., :D] * jax.nn.sigmoid(a[..., D:])
    u = lax.conv_general_dilated(
        u, w_dw[:, None, :].astype(u.dtype), window_strides=(1,),
        padding=[(CONV_WIDTH - 1, 0)], dimension_numbers=('NWC', 'WIO', 'NWC'),
        feature_group_count=D) + b_dw
    u = jax.nn.silu(_layernorm(u, ln_g, ln_b))
    return u @ w_pw2 + b_pw2


def _pool_mixer(h, w, b, scale):
    B, S, D = h.shape
    C = D // POOL_GROUPS
    csum = jnp.cumsum(h.astype(jnp.float32), axis=1)
    t = jnp.arange(S)
    pooled = []
    for g, win in enumerate(POOL_WINDOWS):
        cs = csum[..., g * C:(g + 1) * C]
        lag = jnp.pad(cs, ((0, 0), (win, 0), (0, 0)))[:, :S]
        cnt = jnp.minimum(t + 1, win).astype(jnp.float32)
        pooled.append((cs - lag) / cnt[None, :, None])
    p = jnp.concatenate(pooled, axis=-1).astype(h.dtype) - h
    y = jnp.einsum('bsgc,gcd->bsgd', p.reshape(B, S, POOL_GROUPS, C), w) + b
    return y.reshape(B, S, D) * scale


def _sq_relu_mlp(h, w1, w2):
    return jnp.square(jax.nn.relu(h @ w1)) @ w2


def _fwd_setup_inputs(seed: int = 0) -> dict:
    key = jax.random.key(seed)
    ks = jax.random.split(key, 32)
    D = D_MODEL
    n_mla = (DEPTH + 2) // 3
    n_conv = (DEPTH + 1) // 3
    n_pool = DEPTH // 3
    C = D // POOL_GROUPS

    def nrm(k, shape, fan_in, mult=1.0):
        return jax.random.normal(k, shape, jnp.float32) * (mult * fan_in ** -0.5)

    def gain(k, shape):
        return 1.0 + 0.05 * jax.random.normal(k, shape, jnp.float32)

    def bias(k, shape):
        return 0.02 * jax.random.normal(k, shape, jnp.float32)

    x = jax.random.normal(ks[0], (BATCH, SEQ, D), jnp.float32)
    c = jax.random.normal(ks[1], (BATCH, D), jnp.float32)
    offsets = jax.random.randint(ks[2], (BATCH,), 0, 64, dtype=jnp.int32) * CHUNK
    positions = offsets[:, None] + jnp.arange(SEQ, dtype=jnp.int32)[None, :]
    return {
        "x": x,
        "c": c,
        "positions": positions,
        "ada_w": nrm(ks[3], (DEPTH, D, 6 * D), D, 0.5),
        "ada_b": bias(ks[4], (DEPTH, 6 * D)),
        "norm_g": gain(ks[5], (DEPTH, 4, D)),
        "mla_w_dq": nrm(ks[6], (n_mla, D, Q_LORA), D),
        "mla_q_norm_g": gain(ks[7], (n_mla, Q_LORA)),
        "mla_w_uq": nrm(ks[8], (n_mla, Q_LORA, MLA_HEADS * (QK_NOPE + QK_ROPE)), Q_LORA),
        "mla_w_dkv": nrm(ks[9], (n_mla, D, KV_LORA + QK_ROPE), D),
        "mla_kv_norm_g": gain(ks[10], (n_mla, KV_LORA)),
        "mla_w_ukv": nrm(ks[11], (n_mla, KV_LORA, MLA_HEADS * (QK_NOPE + V_HEAD)), KV_LORA),
        "mla_w_o": nrm(ks[12], (n_mla, MLA_HEADS * V_HEAD, D), MLA_HEADS * V_HEAD),
        "conv_w_pw1": nrm(ks[13], (n_conv, D, 2 * D), D),
        "conv_b_pw1": bias(ks[14], (n_conv, 2 * D)),
        "conv_w_dw": nrm(ks[15], (n_conv, CONV_WIDTH, D), CONV_WIDTH),
        "conv_b_dw": bias(ks[16], (n_conv, D)),
        "conv_ln_g": gain(ks[17], (n_conv, D)),
        "conv_ln_b": bias(ks[18], (n_conv, D)),
        "conv_w_pw2": nrm(ks[19], (n_conv, D, D), D),
        "conv_b_pw2": bias(ks[20], (n_conv, D)),
        "pool_w": nrm(ks[21], (n_pool, POOL_GROUPS, C, C), C),
        "pool_b": bias(ks[22], (n_pool, POOL_GROUPS, C)),
        "pool_scale": gain(ks[23], (n_pool, D)),
        "ffn_w1": nrm(ks[24], (DEPTH, D, D_FF), D),
        "ffn_w2": nrm(ks[25], (DEPTH, D_FF, D), D_FF),
    }


def _fwd_reference(x, c, positions, ada_w, ada_b, norm_g,
              mla_w_dq, mla_q_norm_g, mla_w_uq, mla_w_dkv, mla_kv_norm_g, mla_w_ukv, mla_w_o,
              conv_w_pw1, conv_b_pw1, conv_w_dw, conv_b_dw, conv_ln_g, conv_ln_b,
              conv_w_pw2, conv_b_pw2,
              pool_w, pool_b, pool_scale,
              ffn_w1, ffn_w2):
    c_act = jax.nn.silu(c)
    for i in range(DEPTH):
        kind = i % N_MIXERS
        j = i // N_MIXERS
        mod = c_act @ ada_w[i] + ada_b[i]
        sh_m, sc_m, gt_m, sh_f, sc_f, gt_f = jnp.split(mod, 6, axis=-1)

        h = _rmsnorm(x, norm_g[i, 0]) * (1.0 + sc_m[:, None]) + sh_m[:, None]
        if kind == 0:
            y = _mla(h, positions, mla_w_dq[j], mla_q_norm_g[j], mla_w_uq[j],
                     mla_w_dkv[j], mla_kv_norm_g[j], mla_w_ukv[j], mla_w_o[j])
        elif kind == 1:
            y = _conformer_conv(h, conv_w_pw1[j], conv_b_pw1[j], conv_w_dw[j], conv_b_dw[j],
                                conv_ln_g[j], conv_ln_b[j], conv_w_pw2[j], conv_b_pw2[j])
        else:
            y = _pool_mixer(h, pool_w[j], pool_b[j], pool_scale[j])
        x = x + gt_m[:, None] * _rmsnorm(y, norm_g[i, 1])

        h = _rmsnorm(x, norm_g[i, 2]) * (1.0 + sc_f[:, None]) + sh_f[:, None]
        y = _sq_relu_mlp(h, ffn_w1[i], ffn_w2[i])
        x = x + gt_f[:, None] * _rmsnorm(y, norm_g[i, 3])
    return x


import jax as _jax
import jax.numpy as _jnp

TWIN_FORMAT = 'train_step'
FWD_PARAMS = ['x', 'c', 'positions', 'ada_w', 'ada_b', 'norm_g', 'mla_w_dq', 'mla_q_norm_g', 'mla_w_uq', 'mla_w_dkv', 'mla_kv_norm_g', 'mla_w_ukv', 'mla_w_o', 'conv_w_pw1', 'conv_b_pw1', 'conv_w_dw', 'conv_b_dw', 'conv_ln_g', 'conv_ln_b', 'conv_w_pw2', 'conv_b_pw2', 'pool_w', 'pool_b', 'pool_scale', 'ffn_w1', 'ffn_w2']
TWIN_WEIGHTS = ['ada_w', 'ada_b', 'norm_g', 'mla_w_dq', 'mla_q_norm_g', 'mla_w_uq', 'mla_w_dkv', 'mla_kv_norm_g', 'mla_w_ukv', 'mla_w_o', 'conv_w_pw1', 'conv_b_pw1', 'conv_w_dw', 'conv_b_dw', 'conv_ln_g', 'conv_ln_b', 'conv_w_pw2', 'conv_b_pw2', 'pool_w', 'pool_b', 'pool_scale', 'ffn_w1', 'ffn_w2']
TWIN_DIFF_INPUT = 'x'
TWIN_INPUTS = ['x', 'c', 'positions', 'ada_w', 'ada_b', 'norm_g', 'mla_w_dq', 'mla_q_norm_g', 'mla_w_uq', 'mla_w_dkv', 'mla_kv_norm_g', 'mla_w_ukv', 'mla_w_o', 'conv_w_pw1', 'conv_b_pw1', 'conv_w_dw', 'conv_b_dw', 'conv_ln_g', 'conv_ln_b', 'conv_w_pw2', 'conv_b_pw2', 'pool_w', 'pool_b', 'pool_scale', 'ffn_w1', 'ffn_w2', 'loss_target', 'm_ada_w', 'm_ada_b', 'm_norm_g', 'm_mla_w_dq', 'm_mla_q_norm_g', 'm_mla_w_uq', 'm_mla_w_dkv', 'm_mla_kv_norm_g', 'm_mla_w_ukv', 'm_mla_w_o', 'm_conv_w_pw1', 'm_conv_b_pw1', 'm_conv_w_dw', 'm_conv_b_dw', 'm_conv_ln_g', 'm_conv_ln_b', 'm_conv_w_pw2', 'm_conv_b_pw2', 'm_pool_w', 'm_pool_b', 'm_pool_scale', 'm_ffn_w1', 'm_ffn_w2', 'v_ada_w', 'v_ada_b', 'v_norm_g', 'v_mla_w_dq', 'v_mla_q_norm_g', 'v_mla_w_uq', 'v_mla_w_dkv', 'v_mla_kv_norm_g', 'v_mla_w_ukv', 'v_mla_w_o', 'v_conv_w_pw1', 'v_conv_b_pw1', 'v_conv_w_dw', 'v_conv_b_dw', 'v_conv_ln_g', 'v_conv_ln_b', 'v_conv_w_pw2', 'v_conv_b_pw2', 'v_pool_w', 'v_pool_b', 'v_pool_scale', 'v_ffn_w1', 'v_ffn_w2']
TWIN_OUTPUTS = ['loss', 'grad_x', 'grad_ada_w', 'grad_ada_b', 'grad_norm_g', 'grad_mla_w_dq', 'grad_mla_q_norm_g', 'grad_mla_w_uq', 'grad_mla_w_dkv', 'grad_mla_kv_norm_g', 'grad_mla_w_ukv', 'grad_mla_w_o', 'grad_conv_w_pw1', 'grad_conv_b_pw1', 'grad_conv_w_dw', 'grad_conv_b_dw', 'grad_conv_ln_g', 'grad_conv_ln_b', 'grad_conv_w_pw2', 'grad_conv_b_pw2', 'grad_pool_w', 'grad_pool_b', 'grad_pool_scale', 'grad_ffn_w1', 'grad_ffn_w2', 'delta_ada_w', 'delta_ada_b', 'delta_norm_g', 'delta_mla_w_dq', 'delta_mla_q_norm_g', 'delta_mla_w_uq', 'delta_mla_w_dkv', 'delta_mla_kv_norm_g', 'delta_mla_w_ukv', 'delta_mla_w_o', 'delta_conv_w_pw1', 'delta_conv_b_pw1', 'delta_conv_w_dw', 'delta_conv_b_dw', 'delta_conv_ln_g', 'delta_conv_ln_b', 'delta_conv_w_pw2', 'delta_conv_b_pw2', 'delta_pool_w', 'delta_pool_b', 'delta_pool_scale', 'delta_ffn_w1', 'delta_ffn_w2', 'new_m_ada_w', 'new_m_ada_b', 'new_m_norm_g', 'new_m_mla_w_dq', 'new_m_mla_q_norm_g', 'new_m_mla_w_uq', 'new_m_mla_w_dkv', 'new_m_mla_kv_norm_g', 'new_m_mla_w_ukv', 'new_m_mla_w_o', 'new_m_conv_w_pw1', 'new_m_conv_b_pw1', 'new_m_conv_w_dw', 'new_m_conv_b_dw', 'new_m_conv_ln_g', 'new_m_conv_ln_b', 'new_m_conv_w_pw2', 'new_m_conv_b_pw2', 'new_m_pool_w', 'new_m_pool_b', 'new_m_pool_scale', 'new_m_ffn_w1', 'new_m_ffn_w2', 'new_v_ada_w', 'new_v_ada_b', 'new_v_norm_g', 'new_v_mla_w_dq', 'new_v_mla_q_norm_g', 'new_v_mla_w_uq', 'new_v_mla_w_dkv', 'new_v_mla_kv_norm_g', 'new_v_mla_w_ukv', 'new_v_mla_w_o', 'new_v_conv_w_pw1', 'new_v_conv_b_pw1', 'new_v_conv_w_dw', 'new_v_conv_b_dw', 'new_v_conv_ln_g', 'new_v_conv_ln_b', 'new_v_conv_w_pw2', 'new_v_conv_b_pw2', 'new_v_pool_w', 'new_v_pool_b', 'new_v_pool_scale', 'new_v_ffn_w1', 'new_v_ffn_w2']
TWIN_LEAF_KINDS = {'loss': 'loss', 'grad_x': 'grad_x', 'grad_ada_w': 'grad_w', 'grad_ada_b': 'grad_w', 'grad_norm_g': 'grad_w', 'grad_mla_w_dq': 'grad_w', 'grad_mla_q_norm_g': 'grad_w', 'grad_mla_w_uq': 'grad_w', 'grad_mla_w_dkv': 'grad_w', 'grad_mla_kv_norm_g': 'grad_w', 'grad_mla_w_ukv': 'grad_w', 'grad_mla_w_o': 'grad_w', 'grad_conv_w_pw1': 'grad_w', 'grad_conv_b_pw1': 'grad_w', 'grad_conv_w_dw': 'grad_w', 'grad_conv_b_dw': 'grad_w', 'grad_conv_ln_g': 'grad_w', 'grad_conv_ln_b': 'grad_w', 'grad_conv_w_pw2': 'grad_w', 'grad_conv_b_pw2': 'grad_w', 'grad_pool_w': 'grad_w', 'grad_pool_b': 'grad_w', 'grad_pool_scale': 'grad_w', 'grad_ffn_w1': 'grad_w', 'grad_ffn_w2': 'grad_w', 'delta_ada_w': 'delta_w', 'delta_ada_b': 'delta_w', 'delta_norm_g': 'delta_w', 'delta_mla_w_dq': 'delta_w', 'delta_mla_q_norm_g': 'delta_w', 'delta_mla_w_uq': 'delta_w', 'delta_mla_w_dkv': 'delta_w', 'delta_mla_kv_norm_g': 'delta_w', 'delta_mla_w_ukv': 'delta_w', 'delta_mla_w_o': 'delta_w', 'delta_conv_w_pw1': 'delta_w', 'delta_conv_b_pw1': 'delta_w', 'delta_conv_w_dw': 'delta_w', 'delta_conv_b_dw': 'delta_w', 'delta_conv_ln_g': 'delta_w', 'delta_conv_ln_b': 'delta_w', 'delta_conv_w_pw2': 'delta_w', 'delta_conv_b_pw2': 'delta_w', 'delta_pool_w': 'delta_w', 'delta_pool_b': 'delta_w', 'delta_pool_scale': 'delta_w', 'delta_ffn_w1': 'delta_w', 'delta_ffn_w2': 'delta_w', 'new_m_ada_w': 'new_m', 'new_m_ada_b': 'new_m', 'new_m_norm_g': 'new_m', 'new_m_mla_w_dq': 'new_m', 'new_m_mla_q_norm_g': 'new_m', 'new_m_mla_w_uq': 'new_m', 'new_m_mla_w_dkv': 'new_m', 'new_m_mla_kv_norm_g': 'new_m', 'new_m_mla_w_ukv': 'new_m', 'new_m_mla_w_o': 'new_m', 'new_m_conv_w_pw1': 'new_m', 'new_m_conv_b_pw1': 'new_m', 'new_m_conv_w_dw': 'new_m', 'new_m_conv_b_dw': 'new_m', 'new_m_conv_ln_g': 'new_m', 'new_m_conv_ln_b': 'new_m', 'new_m_conv_w_pw2': 'new_m', 'new_m_conv_b_pw2': 'new_m', 'new_m_pool_w': 'new_m', 'new_m_pool_b': 'new_m', 'new_m_pool_scale': 'new_m', 'new_m_ffn_w1': 'new_m', 'new_m_ffn_w2': 'new_m', 'new_v_ada_w': 'new_v', 'new_v_ada_b': 'new_v', 'new_v_norm_g': 'new_v', 'new_v_mla_w_dq': 'new_v', 'new_v_mla_q_norm_g': 'new_v', 'new_v_mla_w_uq': 'new_v', 'new_v_mla_w_dkv': 'new_v', 'new_v_mla_kv_norm_g': 'new_v', 'new_v_mla_w_ukv': 'new_v', 'new_v_mla_w_o': 'new_v', 'new_v_conv_w_pw1': 'new_v', 'new_v_conv_b_pw1': 'new_v', 'new_v_conv_w_dw': 'new_v', 'new_v_conv_b_dw': 'new_v', 'new_v_conv_ln_g': 'new_v', 'new_v_conv_ln_b': 'new_v', 'new_v_conv_w_pw2': 'new_v', 'new_v_conv_b_pw2': 'new_v', 'new_v_pool_w': 'new_v', 'new_v_pool_b': 'new_v', 'new_v_pool_scale': 'new_v', 'new_v_ffn_w1': 'new_v', 'new_v_ffn_w2': 'new_v'}


def _forward(args):
    return _fwd_reference(*[args[k] for k in FWD_PARAMS])


def _output_shape():
    def fwd():
        inp = _fwd_setup_inputs(0)
        return _fwd_reference(*[inp[k] for k in FWD_PARAMS])
    out = _jax.eval_shape(fwd)
    return out.shape, out.dtype

N_MICROBATCH = 1
ADAM_LR = 0.001
ADAM_B1 = 0.9
ADAM_B2 = 0.999
ADAM_EPS = 1e-08
ADAM_WD = 0.01
ADAM_STEP = 10
PER_EXAMPLE_BATCH_AXIS = {'x': 0, 'c': 0, 'positions': 0, 'loss_target': 0}
SHARED_INPUTS = []
_WEIGHT_DTYPES = {'ada_w': _jnp.float32, 'ada_b': _jnp.float32, 'norm_g': _jnp.float32, 'mla_w_dq': _jnp.float32, 'mla_q_norm_g': _jnp.float32, 'mla_w_uq': _jnp.float32, 'mla_w_dkv': _jnp.float32, 'mla_kv_norm_g': _jnp.float32, 'mla_w_ukv': _jnp.float32, 'mla_w_o': _jnp.float32, 'conv_w_pw1': _jnp.float32, 'conv_b_pw1': _jnp.float32, 'conv_w_dw': _jnp.float32, 'conv_b_dw': _jnp.float32, 'conv_ln_g': _jnp.float32, 'conv_ln_b': _jnp.float32, 'conv_w_pw2': _jnp.float32, 'conv_b_pw2': _jnp.float32, 'pool_w': _jnp.float32, 'pool_b': _jnp.float32, 'pool_scale': _jnp.float32, 'ffn_w1': _jnp.float32, 'ffn_w2': _jnp.float32}
MOMENT_SCALE = {'ada_w': 2.468460e+00, 'ada_b': 4.514251e+00, 'norm_g': 2.757539e+00, 'mla_w_dq': 1.780378e-01, 'mla_q_norm_g': 1.724530e-01, 'mla_w_uq': 8.844280e-02, 'mla_w_dkv': 4.301994e+00, 'mla_kv_norm_g': 4.435758e+00, 'mla_w_ukv': 1.581013e+00, 'mla_w_o': 2.228338e+00, 'conv_w_pw1': 5.691264e-01, 'conv_b_pw1': 1.306363e+00, 'conv_w_dw': 7.321624e-01, 'conv_b_dw': 3.269166e+00, 'conv_ln_g': 1.519598e+00, 'conv_ln_b': 2.040528e+00, 'conv_w_pw2': 1.159229e+00, 'conv_b_pw2': 3.979137e+00, 'pool_w': 2.668761e-01, 'pool_b': 2.331502e+00, 'pool_scale': 1.587638e+00, 'ffn_w1': 3.346423e-01, 'ffn_w2': 1.128327e+00}


def _to_microbatches(a, axis):
    t = _jnp.moveaxis(a, axis, 0)
    t = t.reshape((N_MICROBATCH, t.shape[0] // N_MICROBATCH) + t.shape[1:])
    return _jnp.moveaxis(t, 1, axis + 1)


def setup_inputs(seed: int = 0) -> dict:
    inp = _fwd_setup_inputs(seed)
    key = _jax.random.fold_in(_jax.random.key(seed), 7919)
    shape, _ = _output_shape()
    out = dict(inp)
    out["loss_target"] = _jax.random.normal(_jax.random.fold_in(key, 0), shape, _jnp.float32)
    for i, name in enumerate(TWIN_WEIGHTS):
        w = inp[name].astype(_jnp.float32)
        if MOMENT_SCALE is None:
            s = _jnp.sqrt(_jnp.mean(_jnp.square(w)) + 1e-30)
        else:
            s = MOMENT_SCALE[name]
        km, kv = _jax.random.split(_jax.random.fold_in(key, i + 1))
        out[name] = w
        out["m_" + name] = s * _jax.random.normal(km, w.shape, _jnp.float32)
        out["v_" + name] = (s * s) * _jax.random.uniform(kv, w.shape, _jnp.float32, 0.5, 1.5)
    if N_MICROBATCH > 1:
        for name, axis in PER_EXAMPLE_BATCH_AXIS.items():
            out[name] = _to_microbatches(out[name], axis)
    return {'x': out['x'], 'c': out['c'], 'positions': out['positions'], 'ada_w': out['ada_w'], 'ada_b': out['ada_b'], 'norm_g': out['norm_g'], 'mla_w_dq': out['mla_w_dq'], 'mla_q_norm_g': out['mla_q_norm_g'], 'mla_w_uq': out['mla_w_uq'], 'mla_w_dkv': out['mla_w_dkv'], 'mla_kv_norm_g': out['mla_kv_norm_g'], 'mla_w_ukv': out['mla_w_ukv'], 'mla_w_o': out['mla_w_o'], 'conv_w_pw1': out['conv_w_pw1'], 'conv_b_pw1': out['conv_b_pw1'], 'conv_w_dw': out['conv_w_dw'], 'conv_b_dw': out['conv_b_dw'], 'conv_ln_g': out['conv_ln_g'], 'conv_ln_b': out['conv_ln_b'], 'conv_w_pw2': out['conv_w_pw2'], 'conv_b_pw2': out['conv_b_pw2'], 'pool_w': out['pool_w'], 'pool_b': out['pool_b'], 'pool_scale': out['pool_scale'], 'ffn_w1': out['ffn_w1'], 'ffn_w2': out['ffn_w2'], 'loss_target': out['loss_target'], 'm_ada_w': out['m_ada_w'], 'm_ada_b': out['m_ada_b'], 'm_norm_g': out['m_norm_g'], 'm_mla_w_dq': out['m_mla_w_dq'], 'm_mla_q_norm_g': out['m_mla_q_norm_g'], 'm_mla_w_uq': out['m_mla_w_uq'], 'm_mla_w_dkv': out['m_mla_w_dkv'], 'm_mla_kv_norm_g': out['m_mla_kv_norm_g'], 'm_mla_w_ukv': out['m_mla_w_ukv'], 'm_mla_w_o': out['m_mla_w_o'], 'm_conv_w_pw1': out['m_conv_w_pw1'], 'm_conv_b_pw1': out['m_conv_b_pw1'], 'm_conv_w_dw': out['m_conv_w_dw'], 'm_conv_b_dw': out['m_conv_b_dw'], 'm_conv_ln_g': out['m_conv_ln_g'], 'm_conv_ln_b': out['m_conv_ln_b'], 'm_conv_w_pw2': out['m_conv_w_pw2'], 'm_conv_b_pw2': out['m_conv_b_pw2'], 'm_pool_w': out['m_pool_w'], 'm_pool_b': out['m_pool_b'], 'm_pool_scale': out['m_pool_scale'], 'm_ffn_w1': out['m_ffn_w1'], 'm_ffn_w2': out['m_ffn_w2'], 'v_ada_w': out['v_ada_w'], 'v_ada_b': out['v_ada_b'], 'v_norm_g': out['v_norm_g'], 'v_mla_w_dq': out['v_mla_w_dq'], 'v_mla_q_norm_g': out['v_mla_q_norm_g'], 'v_mla_w_uq': out['v_mla_w_uq'], 'v_mla_w_dkv': out['v_mla_w_dkv'], 'v_mla_kv_norm_g': out['v_mla_kv_norm_g'], 'v_mla_w_ukv': out['v_mla_w_ukv'], 'v_mla_w_o': out['v_mla_w_o'], 'v_conv_w_pw1': out['v_conv_w_pw1'], 'v_conv_b_pw1': out['v_conv_b_pw1'], 'v_conv_w_dw': out['v_conv_w_dw'], 'v_conv_b_dw': out['v_conv_b_dw'], 'v_conv_ln_g': out['v_conv_ln_g'], 'v_conv_ln_b': out['v_conv_ln_b'], 'v_conv_w_pw2': out['v_conv_w_pw2'], 'v_conv_b_pw2': out['v_conv_b_pw2'], 'v_pool_w': out['v_pool_w'], 'v_pool_b': out['v_pool_b'], 'v_pool_scale': out['v_pool_scale'], 'v_ffn_w1': out['v_ffn_w1'], 'v_ffn_w2': out['v_ffn_w2']}


def _loss(weights, diff, rest, loss_target):
    with _jax.named_scope("forward"):
        args = {**rest, TWIN_DIFF_INPUT: diff, **{k: w.astype(_WEIGHT_DTYPES[k]) for k, w in weights.items()}}
        y = _forward(args)
    with _jax.named_scope("loss_head"):
        err = _jnp.square(y.astype(_jnp.float32) - loss_target)
        return 0.5 * _jnp.sum(_jnp.mean(err, axis=-1)) if err.ndim else 0.5 * err


def _adamw(w, g, m, v):
    m = ADAM_B1 * m + (1.0 - ADAM_B1) * g
    v = ADAM_B2 * v + (1.0 - ADAM_B2) * _jnp.square(g)
    m_hat = m / (1.0 - ADAM_B1 ** ADAM_STEP)
    v_hat = v / (1.0 - ADAM_B2 ** ADAM_STEP)
    delta = -ADAM_LR * (m_hat / (_jnp.sqrt(v_hat) + ADAM_EPS) + ADAM_WD * w)
    return delta, m, v


def reference(x, c, positions, ada_w, ada_b, norm_g, mla_w_dq, mla_q_norm_g, mla_w_uq, mla_w_dkv, mla_kv_norm_g, mla_w_ukv, mla_w_o, conv_w_pw1, conv_b_pw1, conv_w_dw, conv_b_dw, conv_ln_g, conv_ln_b, conv_w_pw2, conv_b_pw2, pool_w, pool_b, pool_scale, ffn_w1, ffn_w2, loss_target, m_ada_w, m_ada_b, m_norm_g, m_mla_w_dq, m_mla_q_norm_g, m_mla_w_uq, m_mla_w_dkv, m_mla_kv_norm_g, m_mla_w_ukv, m_mla_w_o, m_conv_w_pw1, m_conv_b_pw1, m_conv_w_dw, m_conv_b_dw, m_conv_ln_g, m_conv_ln_b, m_conv_w_pw2, m_conv_b_pw2, m_pool_w, m_pool_b, m_pool_scale, m_ffn_w1, m_ffn_w2, v_ada_w, v_ada_b, v_norm_g, v_mla_w_dq, v_mla_q_norm_g, v_mla_w_uq, v_mla_w_dkv, v_mla_kv_norm_g, v_mla_w_ukv, v_mla_w_o, v_conv_w_pw1, v_conv_b_pw1, v_conv_w_dw, v_conv_b_dw, v_conv_ln_g, v_conv_ln_b, v_conv_w_pw2, v_conv_b_pw2, v_pool_w, v_pool_b, v_pool_scale, v_ffn_w1, v_ffn_w2):
    given = dict(x=x, c=c, positions=positions, ada_w=ada_w, ada_b=ada_b, norm_g=norm_g, mla_w_dq=mla_w_dq, mla_q_norm_g=mla_q_norm_g, mla_w_uq=mla_w_uq, mla_w_dkv=mla_w_dkv, mla_kv_norm_g=mla_kv_norm_g, mla_w_ukv=mla_w_ukv, mla_w_o=mla_w_o, conv_w_pw1=conv_w_pw1, conv_b_pw1=conv_b_pw1, conv_w_dw=conv_w_dw, conv_b_dw=conv_b_dw, conv_ln_g=conv_ln_g, conv_ln_b=conv_ln_b, conv_w_pw2=conv_w_pw2, conv_b_pw2=conv_b_pw2, pool_w=pool_w, pool_b=pool_b, pool_scale=pool_scale, ffn_w1=ffn_w1, ffn_w2=ffn_w2, loss_target=loss_target, m_ada_w=m_ada_w, m_ada_b=m_ada_b, m_norm_g=m_norm_g, m_mla_w_dq=m_mla_w_dq, m_mla_q_norm_g=m_mla_q_norm_g, m_mla_w_uq=m_mla_w_uq, m_mla_w_dkv=m_mla_w_dkv, m_mla_kv_norm_g=m_mla_kv_norm_g, m_mla_w_ukv=m_mla_w_ukv, m_mla_w_o=m_mla_w_o, m_conv_w_pw1=m_conv_w_pw1, m_conv_b_pw1=m_conv_b_pw1, m_conv_w_dw=m_conv_w_dw, m_conv_b_dw=m_conv_b_dw, m_conv_ln_g=m_conv_ln_g, m_conv_ln_b=m_conv_ln_b, m_conv_w_pw2=m_conv_w_pw2, m_conv_b_pw2=m_conv_b_pw2, m_pool_w=m_pool_w, m_pool_b=m_pool_b, m_pool_scale=m_pool_scale, m_ffn_w1=m_ffn_w1, m_ffn_w2=m_ffn_w2, v_ada_w=v_ada_w, v_ada_b=v_ada_b, v_norm_g=v_norm_g, v_mla_w_dq=v_mla_w_dq, v_mla_q_norm_g=v_mla_q_norm_g, v_mla_w_uq=v_mla_w_uq, v_mla_w_dkv=v_mla_w_dkv, v_mla_kv_norm_g=v_mla_kv_norm_g, v_mla_w_ukv=v_mla_w_ukv, v_mla_w_o=v_mla_w_o, v_conv_w_pw1=v_conv_w_pw1, v_conv_b_pw1=v_conv_b_pw1, v_conv_w_dw=v_conv_w_dw, v_conv_b_dw=v_conv_b_dw, v_conv_ln_g=v_conv_ln_g, v_conv_ln_b=v_conv_ln_b, v_conv_w_pw2=v_conv_w_pw2, v_conv_b_pw2=v_conv_b_pw2, v_pool_w=v_pool_w, v_pool_b=v_pool_b, v_pool_scale=v_pool_scale, v_ffn_w1=v_ffn_w1, v_ffn_w2=v_ffn_w2)
    weights = {n: given[n] for n in TWIN_WEIGHTS}
    shared = {n: given[n] for n in SHARED_INPUTS}
    per_example = {n: given[n] for n in ['x', 'c', 'positions']}
    grad_fn = _jax.value_and_grad(_loss, argnums=(0, 1))

    def one_microbatch(ex, loss_target):
        ex = dict(ex)
        diff = ex.pop(TWIN_DIFF_INPUT)
        return grad_fn(weights, diff, {**shared, **ex}, loss_target)

    if N_MICROBATCH == 1:
        loss, (grad_w, grad_x) = one_microbatch(per_example, given["loss_target"])
    else:
        def body(carry, xs):
            loss_sum, grad_sum = carry
            l_k, (gw_k, gx_k) = one_microbatch(xs[0], xs[1])
            with _jax.named_scope("update"):
                return (loss_sum + l_k, _jax.tree.map(_jnp.add, grad_sum, gw_k)), gx_k

        init = (_jnp.zeros((), _jnp.float32), _jax.tree.map(_jnp.zeros_like, weights))
        (loss, grad_w), grad_x = _jax.lax.scan(body, init, (per_example, given["loss_target"]))
    with _jax.named_scope("update"):
        delta_w, new_m, new_v = {}, {}, {}
        for n in TWIN_WEIGHTS:
            delta_w[n], new_m[n], new_v[n] = _adamw(weights[n], grad_w[n], given["m_" + n], given["v_" + n])
    return (loss, grad_x, *[grad_w[n] for n in TWIN_WEIGHTS], *[delta_w[n] for n in TWIN_WEIGHTS],
            *[new_m[n] for n in TWIN_WEIGHTS], *[new_v[n] for n in TWIN_WEIGHTS])
```

```python
import functools
import math

import jax
import jax.numpy as jnp
from jax import lax
from jax.experimental import pallas as pl
from jax.experimental.pallas import tpu as pltpu

F32 = jnp.float32
MM = jnp.bfloat16
EPS = 1e-6
NEG = -1e30
N_DEV = 8
VMEM_LIMIT = 48 * 1024 * 1024
MESH = pl.DeviceIdType.MESH

D_MODEL = 1024
N_HEADS = 16
HEAD_PAD = 128
QK_NOPE, QK_ROPE, V_HEAD = 64, 32, 64
Q_LORA, KV_LORA = 384, 256
CHUNK = 64
CONV_W = 31
POOL_WINDOWS = (2, 4, 8, 16)
ROPE_THETA = 10000.0
ATT_SCALE = 1.0 / math.sqrt(QK_NOPE + QK_ROPE)

ADAM_LR, ADAM_B1, ADAM_B2, ADAM_EPS, ADAM_WD, ADAM_STEP = 0.001, 0.9, 0.999, 1e-08, 0.01, 10

WEIGHTS = [('ada_w', 2), ('ada_b', None), ('norm_g', 2), ('mla_w_dq', 1), ('mla_q_norm_g', 1), ('mla_w_uq', 2),
           ('mla_w_dkv', 1), ('mla_kv_norm_g', 1), ('mla_w_ukv', 2), ('mla_w_o', 1), ('conv_w_pw1', 2),
           ('conv_b_pw1', None), ('conv_w_dw', 2), ('conv_b_dw', None), ('conv_ln_g', None), ('conv_ln_b', None),
           ('conv_w_pw2', 1), ('conv_b_pw2', None), ('pool_w', 2), ('pool_b', 2), ('pool_scale', 1),
           ('ffn_w1', 2), ('ffn_w2', 1)]
SHARD_AXIS = dict(WEIGHTS)
BIG = ['mla_w_dq', 'mla_w_uq', 'mla_w_dkv', 'mla_w_ukv', 'mla_w_o', 'conv_w_pw1', 'conv_w_pw2', 'pool_w', 'ffn_w1', 'ffn_w2']
SMALL = ['norm_g', 'mla_q_norm_g', 'mla_kv_norm_g', 'conv_w_dw', 'pool_b', 'pool_scale']
REPL = ['conv_b_pw1', 'conv_b_dw', 'conv_ln_g', 'conv_ln_b', 'conv_b_pw2']


def _dot(a, b):
    return jnp.dot(a.astype(MM), b.astype(MM), preferred_element_type=F32)


def _dot_nt(a, b):
    return lax.dot_general(a.astype(MM), b.astype(MM), (((1,), (1,)), ((), ())), preferred_element_type=F32)


def _dot_tn(a, b):
    return lax.dot_general(a.astype(MM), b.astype(MM), (((0,), (0,)), ((), ())), preferred_element_type=F32)


def _sigmoid(x):
    return 1.0 / (1.0 + jnp.exp(-x))


def _rstd(x):
    return lax.rsqrt(jnp.mean(x * x, axis=-1, keepdims=True) + EPS)


def _rms(x, g):
    return x * _rstd(x) * g


def _rms_bwd(x, g, dout):
    r = _rstd(x)
    xn = x * r
    dg = jnp.sum(dout * xn, axis=0, keepdims=True)
    dxn = dout * g
    dx = r * (dxn - xn * jnp.mean(dxn * xn, axis=-1, keepdims=True))
    return dx, dg


def _prenorm_bwd(x, g0, sc, dh):
    r = _rstd(x)
    xn = x * r
    dsh = jnp.sum(dh, axis=0, keepdims=True)
    dsc = jnp.sum(dh * (xn * g0), axis=0, keepdims=True)
    dn = dh * (1.0 + sc)
    dg0 = jnp.sum(dn * xn, axis=0, keepdims=True)
    dxn = dn * g0
    dx = r * (dxn - xn * jnp.mean(dxn * xn, axis=-1, keepdims=True))
    return dx, dsh, dsc, dg0


def _cparams(sem):
    return pltpu.CompilerParams(dimension_semantics=sem, vmem_limit_bytes=VMEM_LIMIT)


def _rows(name, body, n_rows, tm, rows, consts, outs, accs=(), scratch=()):
    tm = min(tm, n_rows)
    nblk = n_rows // tm
    nr, nc, no, na = len(rows), len(consts), len(outs), len(accs)
    in_specs, args = [], []
    for a, kind in rows:
        if kind == 'cur':
            im = lambda i: (i, 0)
        elif kind == 'prev':
            im = lambda i: (jnp.maximum(i - 1, 0), 0)
        else:
            im = lambda i: (jnp.minimum(i + 1, nblk - 1), 0)
        in_specs.append(pl.BlockSpec((tm, a.shape[1]), im))
        args.append(a)
    for a in consts:
        in_specs.append(pl.BlockSpec(a.shape, lambda i, nd=a.ndim: (0,) * nd))
        args.append(a)
    out_specs = [pl.BlockSpec((tm, c), lambda i: (i, 0)) for c, _ in outs]
    out_specs += [pl.BlockSpec(s, lambda i, nd=len(s): (0,) * nd) for s in accs]
    out_shape = [jax.ShapeDtypeStruct((n_rows, c), dt) for c, dt in outs]
    out_shape += [jax.ShapeDtypeStruct(s, F32) for s in accs]

    def kern(*refs):
        i = pl.program_id(0)
        rr = refs[:nr]
        cc = refs[nr:nr + nc]
        oo = refs[nr + nc:nr + nc + no]
        aa = refs[nr + nc + no:nr + nc + no + na]
        ss = refs[nr + nc + no + na:]

        @pl.when(i == 0)
        def _():
            for a in aa:
                a[...] = jnp.zeros(a.shape, F32)

        body(i, nblk, rr, cc, oo, aa, ss)

    return pl.pallas_call(kern, grid=(nblk,), in_specs=in_specs, out_specs=out_specs, out_shape=out_shape,
                          scratch_shapes=list(scratch), name=name, compiler_params=_cparams(("arbitrary",)))(*args)


def _place():
    return lax.axis_index("x"), lax.axis_index("y"), lax.axis_index("c")


def _ag_small(name, xs):
    R, C = xs.shape

    def body(x_ref, out_ref, send_sems, recv_sems):
        x, y, c = _place()
        me = 4 * x + 2 * y + c
        out_ref[me] = x_ref[...]
        copies = []
        for k in range(1, N_DEV):
            peer = ((1 - x) if k & 4 else x, (1 - y) if k & 2 else y, (1 - c) if k & 1 else c)
            cp = pltpu.make_async_remote_copy(src_ref=x_ref, dst_ref=out_ref.at[me], send_sem=send_sems.at[k - 1],
                                              recv_sem=recv_sems.at[k - 1], device_id=peer, device_id_type=MESH)
            cp.start()
            copies.append(cp)
        for cp in copies:
            cp.wait()

    return pl.pallas_call(
        body, out_shape=jax.ShapeDtypeStruct((N_DEV, R, C), xs.dtype),
        in_specs=[pl.BlockSpec(memory_space=pltpu.VMEM)], out_specs=pl.BlockSpec(memory_space=pltpu.VMEM),
        scratch_shapes=[pltpu.SemaphoreType.DMA((N_DEV - 1,)), pltpu.SemaphoreType.DMA((N_DEV - 1,))], name=name)(xs)


def _ag_big(name, xs):
    R, C = xs.shape

    def body(x_ref, out_ref, send_sems, recv_sems, local_sem):
        x, y, c = _place()
        me, sibling = (x, y, c), (x, y, 1 - c)
        chips = [(1 - x, y), (x, 1 - y), (1 - x, 1 - y)]

        def rows(px, py, pc):
            return out_ref.at[4 * px + 2 * py + pc]

        def copy(k, block, to, src=None):
            return pltpu.make_async_remote_copy(src_ref=rows(*block) if src is None else src, dst_ref=rows(*block),
                                                send_sem=send_sems.at[k], recv_sem=recv_sems.at[k], device_id=to,
                                                device_id_type=MESH)

        mine = pltpu.make_async_copy(x_ref, rows(*me), local_sem)
        mine.start()
        first = [copy(0, me, sibling, src=x_ref)]
        first += [copy(1 + j, me, (*chip, c), src=x_ref) for j, chip in enumerate(chips)]
        for cp in first:
            cp.start()
        passed = [copy(4 + j, (*chip, c), sibling) for j, chip in enumerate(chips)]
        for j, chip in enumerate(chips):
            copy(1 + j, (*chip, c), me).wait_recv()
            passed[j].start()
        copy(0, sibling, me).wait_recv()
        for j, chip in enumerate(chips):
            copy(4 + j, (*chip, 1 - c), me).wait_recv()
        for cp in first + passed:
            cp.wait_send()
        mine.wait()

    return pl.pallas_call(
        body, out_shape=jax.ShapeDtypeStruct((N_DEV, R, C), xs.dtype),
        in_specs=[pl.BlockSpec(memory_space=pl.ANY)], out_specs=pl.BlockSpec(memory_space=pl.ANY),
        scratch_shapes=[pltpu.SemaphoreType.DMA((7,)), pltpu.SemaphoreType.DMA((7,)), pltpu.SemaphoreType.DMA], name=name)(xs)


def _rs_pair(name, p):
    _, _, R, C = p.shape

    def body(p_ref, recv_ref, send_sems, recv_sems):
        x, y, c = _place()
        copies = []
        for j in range(4):
            cp = pltpu.make_async_remote_copy(src_ref=p_ref.at[j, 1 - c], dst_ref=recv_ref.at[j], send_sem=send_sems.at[j],
                                              recv_sem=recv_sems.at[j], device_id=(x, y, 1 - c), device_id_type=MESH)
            cp.start()
            copies.append(cp)
        for cp in copies:
            cp.wait()

    return pl.pallas_call(
        body, out_shape=jax.ShapeDtypeStruct((4, R, C), p.dtype),
        in_specs=[pl.BlockSpec(memory_space=pl.ANY)], out_specs=pl.BlockSpec(memory_space=pl.ANY),
        scratch_shapes=[pltpu.SemaphoreType.DMA((4,)), pltpu.SemaphoreType.DMA((4,))], name=name)(p)


def _rs_chips(name, s):
    _, R, C = s.shape

    def body(s_ref, recv_ref, send_sems, recv_sems, local_sem):
        x, y, c = _place()
        mine = 2 * x + y
        own = pltpu.make_async_copy(s_ref.at[mine], recv_ref.at[mine], local_sem)
        own.start()
        copies = []
        for k in range(1, 4):
            px = (1 - x) if k & 2 else x
            py = (1 - y) if k & 1 else y
            cp = pltpu.make_async_remote_copy(src_ref=s_ref.at[2 * px + py], dst_ref=recv_ref.at[mine], send_sem=send_sems.at[k - 1],
                                              recv_sem=recv_sems.at[k - 1], device_id=(px, py, c), device_id_type=MESH)
            cp.start()
            copies.append(cp)
        for cp in copies:
            cp.wait()
        own.wait()

    return pl.pallas_call(
        body, out_shape=jax.ShapeDtypeStruct((4, R, C), s.dtype),
        in_specs=[pl.BlockSpec(memory_space=pl.ANY)], out_specs=pl.BlockSpec(memory_space=pl.ANY),
        scratch_shapes=[pltpu.SemaphoreType.DMA((3,)), pltpu.SemaphoreType.DMA((3,)), pltpu.SemaphoreType.DMA], name=name)(s)


def _pair_sum(p, recv, my_c):
    _, _, R, C = p.shape
    tr = 256

    def body(c_ref, p_ref, r_ref, o_ref):
        o_ref[...] = p_ref[...] + r_ref[...]

    return pl.pallas_call(
        body, grid_spec=pltpu.PrefetchScalarGridSpec(
            num_scalar_prefetch=1, grid=(4, R // tr),
            in_specs=[pl.BlockSpec((None, None, tr, C), lambda j, r, cr: (j, cr[0], r, 0)),
                      pl.BlockSpec((None, tr, C), lambda j, r, cr: (j, r, 0))],
            out_specs=pl.BlockSpec((None, tr, C), lambda j, r, cr: (j, r, 0))),
        out_shape=jax.ShapeDtypeStruct((4, R, C), F32), name="rs_pair_sum",
        compiler_params=_cparams(("arbitrary", "arbitrary")))(my_c.reshape(1), p, recv)


def _chip_sum(recv):
    _, R, C = recv.shape
    tr = 256

    def body(r_ref, o_ref):
        o_ref[...] = ((r_ref[0] + r_ref[1]) + r_ref[2]) + r_ref[3]

    return pl.pallas_call(
        body, grid=(R // tr,), in_specs=[pl.BlockSpec((4, tr, C), lambda r: (0, r, 0))],
        out_specs=pl.BlockSpec((tr, C), lambda r: (r, 0)), out_shape=jax.ShapeDtypeStruct((R, C), F32), name="rs_chip_sum",
        compiler_params=_cparams(("arbitrary",)))(recv)


def _mod_part(c16, ada_w, ada_b_cols):
    L, D, n = ada_w.shape

    def body(c_ref, w_ref, b_ref, o_ref):
        cv = c_ref[...]
        o_ref[...] = _dot(cv * _sigmoid(cv), w_ref[...]) + b_ref[...]

    return pl.pallas_call(
        body, grid=(L,), in_specs=[pl.BlockSpec((16, D), lambda i: (0, 0)), pl.BlockSpec((None, D, n), lambda i: (i, 0, 0)),
                                   pl.BlockSpec((None, 1, n), lambda i: (i, 0, 0))],
        out_specs=pl.BlockSpec((None, 16, n), lambda i: (i, 0, 0)), out_shape=jax.ShapeDtypeStruct((L, 16, n), F32),
        name="ada_mod", compiler_params=_cparams(("arbitrary",)))(c16, ada_w, ada_b_cols)


def _ada_w_grad(c16, dmod16):
    L, _, n = dmod16.shape
    D = c16.shape[1]

    def body(c_ref, d_ref, o_ref):
        cv = c_ref[...]
        o_ref[...] = _dot_tn(cv * _sigmoid(cv), d_ref[...])

    return pl.pallas_call(
        body, grid=(L,), in_specs=[pl.BlockSpec((16, D), lambda i: (0, 0)), pl.BlockSpec((None, 16, n), lambda i: (i, 0, 0))],
        out_specs=pl.BlockSpec((None, D, n), lambda i: (i, 0, 0)), out_shape=jax.ShapeDtypeStruct((L, D, n), F32),
        name="ada_w_grad", compiler_params=_cparams(("arbitrary",)))(c16, dmod16)


def _sum_devices(name, g):
    _, R, C = g.shape

    def body(g_ref, o_ref):
        acc = g_ref[0]
        for d in range(1, N_DEV):
            acc = acc + g_ref[d]
        o_ref[...] = acc

    return pl.pallas_call(body, out_shape=jax.ShapeDtypeStruct((R, C), F32), name=name)(g)


def _prenorm(name, x, g0, sc, sh, dtype):
    T, D = x.shape

    def body(i, n, rr, cc, oo, aa, ss):
        oo[0][...] = (_rms(rr[0][...], cc[0][...]) * (1.0 + cc[1][...]) + cc[2][...]).astype(dtype)

    return _rows(name, body, T, 512, [(x, 'cur')], [g0, sc, sh], [(D, dtype)])[0]


def _post_bwd(name, dxo, y, g1, gt):
    T, D = y.shape

    def body(i, n, rr, cc, oo, aa, ss):
        d = rr[0][...]
        yv = rr[1][...]
        g1v, gtv = cc[0][...], cc[1][...]
        aa[1][...] += jnp.sum(d * _rms(yv, g1v), axis=0, keepdims=True)
        dy, dg1 = _rms_bwd(yv, g1v, d * gtv)
        aa[0][...] += dg1
        aa[2][...] += jnp.sum(dy, axis=0, keepdims=True)
        oo[0][...] = dy.astype(MM)

    return _rows(name, body, T, 512, [(dxo, 'cur'), (y, 'cur')], [g1, gt], [(D, MM)], accs=[(1, D)] * 3)


def _mm_post(name, a, w, bias, x, g1, gt):
    T, D = x.shape
    consts = [w, g1, gt] + ([bias] if bias is not None else [])

    def body(i, n, rr, cc, oo, aa, ss):
        y = _dot(rr[0][...], cc[0][...])
        if bias is not None:
            y = y + cc[3][...]
        oo[0][...] = y
        oo[1][...] = rr[1][...] + cc[2][...] * _rms(y, cc[1][...])

    return _rows(name, body, T, 512, [(a, 'cur'), (x, 'cur')], consts, [(D, F32), (D, F32)])


def _mm_nt_rows(name, a, w):
    T = a.shape[0]
    K = w.shape[0]

    def body(i, n, rr, cc, oo, aa, ss):
        oo[0][...] = _dot_nt(rr[0][...], cc[0][...]).astype(MM)

    return _rows(name, body, T, 512, [(a, 'cur')], [w], [(K, MM)])[0]


def _mm_tn(name, a, b, sqrelu=False, col_shards=0, diag=0):
    T, M = a.shape
    N = b.shape[1]
    tk = min(512, T)
    nk = T // tk
    if diag:
        bm, bn = M // diag, N // diag
        grid = (diag, 1, nk)
        a_spec = pl.BlockSpec((tk, bm), lambda g, n, k: (k, g))
        b_spec = pl.BlockSpec((tk, bn), lambda g, n, k: (k, g))
        o_spec = pl.BlockSpec((None, bm, bn), lambda g, n, k: (g, 0, 0))
        o_shape = (diag, bm, bn)
    else:
        bm = min(M, 1024)
        bn = N // col_shards if col_shards else min(N, 1024)
        grid = (M // bm, N // bn, nk)
        a_spec = pl.BlockSpec((tk, bm), lambda m, n, k: (k, m))
        b_spec = pl.BlockSpec((tk, bn), lambda m, n, k: (k, n))
        if col_shards:
            o_spec = pl.BlockSpec((None, bm, bn), lambda m, n, k: (n, m, 0))
            o_shape = (col_shards, M, bn)
        else:
            o_spec = pl.BlockSpec((bm, bn), lambda m, n, k: (m, n))
            o_shape = (M, N)

    def body(a_ref, b_ref, o_ref):
        @pl.when(pl.program_id(2) == 0)
        def _():
            o_ref[...] = jnp.zeros(o_ref.shape, F32)

        av = a_ref[...]
        if sqrelu:
            r = jnp.maximum(av.astype(F32), 0.0)
            av = r * r
        o_ref[...] += _dot_tn(av, b_ref[...])

    return pl.pallas_call(body, grid=grid, in_specs=[a_spec, b_spec], out_specs=o_spec,
                          out_shape=jax.ShapeDtypeStruct(o_shape, F32), name=name,
                          compiler_params=_cparams(("arbitrary", "arbitrary", "arbitrary")))(a, b)


def _ffn_fwd(name, h, w1s, w2, x, g1, gt):
    T, D = h.shape
    nf, _, tf = w1s.shape
    F = nf * tf
    tm = min(512, T)

    def body(h_ref, w1_ref, w2_ref, x_ref, g1_ref, gt_ref, a_ref, y_ref, xo_ref, acc):
        f = pl.program_id(1)

        @pl.when(f == 0)
        def _():
            acc[...] = jnp.zeros(acc.shape, F32)

        a = _dot(h_ref[...], w1_ref[...])
        a_ref[...] = a.astype(MM)
        r = jnp.maximum(a, 0.0)
        acc[...] += _dot(r * r, w2_ref[...])

        @pl.when(f == nf - 1)
        def _():
            y = acc[...]
            y_ref[...] = y
            xo_ref[...] = x_ref[...] + gt_ref[...] * _rms(y, g1_ref[...])

    row = lambda t, f: (t, 0)
    one = lambda t, f: (0, 0)
    return pl.pallas_call(
        body, grid=(T // tm, nf),
        in_specs=[pl.BlockSpec((tm, D), row), pl.BlockSpec((None, D, tf), lambda t, f: (f, 0, 0)), pl.BlockSpec((tf, D), lambda t, f: (f, 0)),
                  pl.BlockSpec((tm, D), row), pl.BlockSpec((1, D), one), pl.BlockSpec((1, D), one)],
        out_specs=[pl.BlockSpec((tm, tf), lambda t, f: (t, f)), pl.BlockSpec((tm, D), row), pl.BlockSpec((tm, D), row)],
        out_shape=[jax.ShapeDtypeStruct((T, F), MM), jax.ShapeDtypeStruct((T, D), F32), jax.ShapeDtypeStruct((T, D), F32)],
        scratch_shapes=[pltpu.VMEM((tm, D), F32)], name=name,
        compiler_params=_cparams(("arbitrary", "arbitrary")))(h, w1s, w2, x, g1, gt)


def _ffn_bwd(name, dy, a, w1s, w2, x, dxo, g0, sc):
    T, D = x.shape
    nf, _, tf = w1s.shape
    F = nf * tf
    tm = min(512, T)

    def body(dy_ref, a_ref, w1_ref, w2_ref, x_ref, dxo_ref, g0_ref, sc_ref, da_ref, dx_ref, dsh_ref, dsc_ref, dg0_ref, acc):
        t, f = pl.program_id(0), pl.program_id(1)

        @pl.when((t == 0) & (f == 0))
        def _():
            for r in (dsh_ref, dsc_ref, dg0_ref):
                r[...] = jnp.zeros(r.shape, F32)

        @pl.when(f == 0)
        def _():
            acc[...] = jnp.zeros(acc.shape, F32)

        du = _dot_nt(dy_ref[...], w2_ref[...])
        da = (du * (2.0 * jnp.maximum(a_ref[...].astype(F32), 0.0))).astype(MM)
        da_ref[...] = da
        acc[...] += _dot_nt(da, w1_ref[...])

        @pl.when(f == nf - 1)
        def _():
            dx, dsh, dsc, dg0 = _prenorm_bwd(x_ref[...], g0_ref[...], sc_ref[...], acc[...])
            dx_ref[...] = dxo_ref[...] + dx
            dsh_ref[...] += dsh
            dsc_ref[...] += dsc
            dg0_ref[...] += dg0

    row = lambda t, f: (t, 0)
    one = lambda t, f: (0, 0)
    blk = lambda t, f: (t, f)
    return pl.pallas_call(
        body, grid=(T // tm, nf),
        in_specs=[pl.BlockSpec((tm, D), row), pl.BlockSpec((tm, tf), blk), pl.BlockSpec((None, D, tf), lambda t, f: (f, 0, 0)),
                  pl.BlockSpec((tf, D), lambda t, f: (f, 0)), pl.BlockSpec((tm, D), row), pl.BlockSpec((tm, D), row),
                  pl.BlockSpec((1, D), one), pl.BlockSpec((1, D), one)],
        out_specs=[pl.BlockSpec((tm, tf), blk), pl.BlockSpec((tm, D), row)] + [pl.BlockSpec((1, D), one)] * 3,
        out_shape=[jax.ShapeDtypeStruct((T, F), MM), jax.ShapeDtypeStruct((T, D), F32)] + [jax.ShapeDtypeStruct((1, D), F32)] * 3,
        scratch_shapes=[pltpu.VMEM((tm, D), F32)], name=name,
        compiler_params=_cparams(("arbitrary", "arbitrary")))(dy, a, w1s, w2, x, dxo, g0, sc)


def _rope_tables(pos, invf):
    T = pos.shape[0]

    def body(i, n, rr, cc, oo, aa, ss):
        ang = rr[0][...] * cc[0][...]
        lane = lax.broadcasted_iota(jnp.int32, ang.shape, 1)
        cs, sn = jnp.cos(ang), jnp.sin(ang)
        oo[0][...] = jnp.where((lane >= QK_NOPE) & (lane < QK_NOPE + QK_ROPE), cs, 1.0)
        oo[1][...] = jnp.where((lane >= QK_NOPE) & (lane < QK_NOPE + QK_ROPE // 2), -sn, 0.0)
        oo[2][...] = jnp.where((lane >= QK_NOPE + QK_ROPE // 2) & (lane < QK_NOPE + QK_ROPE), sn, 0.0)

    return _rows("rope_tables", body, T, 512, [(pos, 'cur')], [invf], [(HEAD_PAD, F32)] * 3)


def _rope(v, C, S1, S2):
    n = v.shape[1]
    reps = n // HEAD_PAD
    if reps > 1:
        C, S1, S2 = (jnp.tile(t, (1, reps)) for t in (C, S1, S2))
    return v * C + pltpu.roll(v, n - QK_ROPE // 2, 1) * S1 + pltpu.roll(v, QK_ROPE // 2, 1) * S2


def _unrope(d, C, S1, S2):
    n = d.shape[1]
    reps = n // HEAD_PAD
    if reps > 1:
        C, S1, S2 = (jnp.tile(t, (1, reps)) for t in (C, S1, S2))
    return d * C + pltpu.roll(d * S1, QK_ROPE // 2, 1) + pltpu.roll(d * S2, n - QK_ROPE // 2, 1)


def _mla_proj(name, h, C, S1, S2, w_dq, qg, w_uq, w_dkv, kvg, w_ukv_k, w_ukv_v):
    T = h.shape[0]
    HP = N_HEADS * HEAD_PAD

    def body(i, n, rr, cc, oo, aa, ss):
        hv = rr[0][...]
        Cv, S1v, S2v = rr[1][...], rr[2][...], rr[3][...]
        cq_raw = _dot(hv, cc[0][...])
        cq = _rms(cq_raw, cc[1][...]).astype(MM)
        q = _rope(_dot(cq, cc[2][...]), Cv, S1v, S2v)
        ckv_all = _dot(hv, cc[3][...])
        ckv_raw = ckv_all[:, :KV_LORA]
        ckv = _rms(ckv_raw, cc[4][...]).astype(MM)
        kr = _rope(ckv_all[:, KV_LORA:], Cv, S1v, S2v)
        k = _dot(ckv, cc[5][...]) + jnp.tile(kr, (1, N_HEADS))
        v = _dot(ckv, cc[6][...])
        oo[0][...] = cq_raw
        oo[1][...] = cq
        oo[2][...] = ckv_raw
        oo[3][...] = ckv
        oo[4][...] = q.astype(MM)
        oo[5][...] = k.astype(MM)
        oo[6][...] = v.astype(MM)

    return _rows(name, body, T, 256, [(h, 'cur'), (C, 'cur'), (S1, 'cur'), (S2, 'cur')],
                 [w_dq, qg, w_uq, w_dkv, kvg, w_ukv_k, w_ukv_v],
                 [(Q_LORA, F32), (Q_LORA, MM), (KV_LORA, F32), (KV_LORA, MM), (HP, MM), (HP, MM), (HP, MM)])


def _chunk_mask(tq, tk):
    qi = lax.broadcasted_iota(jnp.int32, (tq, tk), 0) // CHUNK
    ki = lax.broadcasted_iota(jnp.int32, (tq, tk), 1) // CHUNK
    return ki <= qi


def _attn_fwd(name, q, k, v):
    T = q.shape[0]
    tb = min(256, T)
    nb = T // tb

    def body(q_ref, k_ref, v_ref, o_ref, lse_ref):
        def q_block(qb, carry):
            q0 = pl.multiple_of(qb * tb, tb)
            qh = q_ref[pl.ds(q0, tb), :]

            def k_block(k0, masked, st):
                m, l, acc = st
                s = _dot_nt(qh, k_ref[pl.ds(k0, tb), :]) * ATT_SCALE
                if masked:
                    s = jnp.where(_chunk_mask(tb, tb), s, NEG)
                m_new = jnp.maximum(m, jnp.max(s, axis=-1, keepdims=True))
                alpha = jnp.exp(m - m_new)
                p = jnp.exp(s - m_new)
                l = alpha * l + jnp.sum(p, axis=-1, keepdims=True)
                acc = alpha * acc + _dot(p, v_ref[pl.ds(k0, tb), :])
                return m_new, l, acc

            st = (jnp.full((tb, 1), NEG, F32), jnp.zeros((tb, 1), F32), jnp.zeros((tb, HEAD_PAD), F32))
            st = k_block(q0, True, st)
            m, l, acc = lax.fori_loop(0, qb, lambda kb, s_: k_block(pl.multiple_of(kb * tb, tb), False, s_), st)
            o_ref[pl.ds(q0, tb), :] = (acc / l).astype(MM)
            lse_ref[pl.ds(q0, tb), :] = jnp.broadcast_to(m + jnp.log(l), (tb, HEAD_PAD))
            return carry

        lax.fori_loop(0, nb, q_block, 0)

    spec = pl.BlockSpec((T, HEAD_PAD), lambda h: (0, h))
    return pl.pallas_call(
        body, grid=(N_HEADS,), in_specs=[spec] * 3, out_specs=[spec] * 2,
        out_shape=[jax.ShapeDtypeStruct(q.shape, MM), jax.ShapeDtypeStruct(q.shape, F32)], name=name,
        compiler_params=_cparams(("arbitrary",)))(q, k, v)


def _attn_bwd(name, q, k, v, o, do, lse):
    T = q.shape[0]
    tb = min(256, T)
    nb = T // tb

    def body(q_ref, k_ref, v_ref, o_ref, do_ref, lse_ref, dq_ref, dk_ref, dv_ref, dq_acc, dk_acc, dv_acc):
        dq_acc[...] = jnp.zeros(dq_acc.shape, F32)

        def k_block(kb, carry):
            k0 = pl.multiple_of(kb * tb, tb)
            kh = k_ref[pl.ds(k0, tb), :]
            vh = v_ref[pl.ds(k0, tb), :]
            dk_acc[...] = jnp.zeros(dk_acc.shape, F32)
            dv_acc[...] = jnp.zeros(dv_acc.shape, F32)

            def q_block(q0, masked):
                qh = q_ref[pl.ds(q0, tb), :]
                doh = do_ref[pl.ds(q0, tb), :]
                delta = jnp.sum(doh.astype(F32) * o_ref[pl.ds(q0, tb), :].astype(F32), axis=-1, keepdims=True)
                s = _dot_nt(qh, kh) * ATT_SCALE
                if masked:
                    s = jnp.where(_chunk_mask(tb, tb), s, NEG)
                p = jnp.exp(s - jnp.tile(lse_ref[pl.ds(q0, tb), :], (1, tb // HEAD_PAD)))
                ds = (p * (_dot_nt(doh, vh) - delta) * ATT_SCALE).astype(MM)
                dv_acc[...] += _dot_tn(p, doh)
                dk_acc[...] += _dot_tn(ds, qh)
                dq_acc[pl.ds(q0, tb), :] += _dot(ds, kh)

            q_block(k0, True)

            def rest(qb, c_):
                q_block(pl.multiple_of(qb * tb, tb), False)
                return c_

            lax.fori_loop(kb + 1, nb, rest, 0)
            dk_ref[pl.ds(k0, tb), :] = dk_acc[...].astype(MM)
            dv_ref[pl.ds(k0, tb), :] = dv_acc[...].astype(MM)
            return carry

        lax.fori_loop(0, nb, k_block, 0)
        dq_ref[...] = dq_acc[...].astype(MM)

    spec = pl.BlockSpec((T, HEAD_PAD), lambda h: (0, h))
    return pl.pallas_call(
        body, grid=(N_HEADS,), in_specs=[spec] * 6, out_specs=[spec] * 3,
        out_shape=[jax.ShapeDtypeStruct(q.shape, MM)] * 3,
        scratch_shapes=[pltpu.VMEM((T, HEAD_PAD), F32), pltpu.VMEM((tb, HEAD_PAD), F32), pltpu.VMEM((tb, HEAD_PAD), F32)],
        name=name, compiler_params=_cparams(("arbitrary",)))(q, k, v, o, do, lse)


def _mla_proj_bwd(name, dq, dk, dv, C, S1, S2, cq_raw, ckv_raw, x, dxo, w_uq, w_ukv_k, w_ukv_v, w_dq, w_dkv, qg, kvg, g0, sc):
    T, D = x.shape
    HP = N_HEADS * HEAD_PAD

    def body(i, n, rr, cc, oo, aa, ss):
        Cv, S1v, S2v = rr[3][...], rr[4][...], rr[5][...]
        dq_pre = _unrope(rr[0][...].astype(F32), Cv, S1v, S2v).astype(MM)
        oo[0][...] = dq_pre
        dcq = _dot_nt(dq_pre, cc[0][...])
        dcq_raw, dqg = _rms_bwd(rr[6][...], cc[5][...], dcq)
        aa[0][...] += dqg
        dcq_raw = dcq_raw.astype(MM)
        oo[1][...] = dcq_raw
        dkv = rr[1][...]
        dkr = dkv[:, :HEAD_PAD].astype(F32)
        for hh in range(1, N_HEADS):
            dkr = dkr + dkv[:, hh * HEAD_PAD:(hh + 1) * HEAD_PAD].astype(F32)
        lane = lax.broadcasted_iota(jnp.int32, dkr.shape, 1)
        dkr = jnp.where((lane >= QK_NOPE) & (lane < QK_NOPE + QK_ROPE), _unrope(dkr, Cv, S1v, S2v), 0.0)
        dckv = _dot_nt(dkv, cc[1][...]) + _dot_nt(rr[2][...], cc[2][...])
        dckv_raw, dkvg = _rms_bwd(rr[7][...], cc[6][...], dckv)
        aa[1][...] += dkvg
        dckv_all = jnp.concatenate([dckv_raw, dkr], axis=1).astype(MM)
        oo[2][...] = dckv_all
        dh = _dot_nt(dcq_raw, cc[3][...]) + _dot_nt(dckv_all, cc[4][...])
        dx, dsh, dsc, dg0 = _prenorm_bwd(rr[8][...], cc[7][...], cc[8][...], dh)
        oo[3][...] = rr[9][...] + dx
        aa[2][...] += dsh
        aa[3][...] += dsc
        aa[4][...] += dg0

    return _rows(name, body, T, 256,
                 [(dq, 'cur'), (dk, 'cur'), (dv, 'cur'), (C, 'cur'), (S1, 'cur'), (S2, 'cur'), (cq_raw, 'cur'), (ckv_raw, 'cur'),
                  (x, 'cur'), (dxo, 'cur')],
                 [w_uq, w_ukv_k, w_ukv_v, w_dq, w_dkv, qg, kvg, g0, sc],
                 [(HP, MM), (Q_LORA, MM), (KV_LORA + HEAD_PAD, MM), (D, F32)],
                 accs=[(1, Q_LORA), (1, KV_LORA), (1, D), (1, D), (1, D)])


HALO = 32


def _conv_glu(name, h, w_pw1, b_pw1):
    T, D = h.shape

    def body(i, n, rr, cc, oo, aa, ss):
        a = _dot(rr[0][...], cc[0][...]) + cc[1][...]
        oo[0][...] = a
        oo[1][...] = a[:, :D] * _sigmoid(a[:, D:])

    return _rows(name, body, T, 512, [(h, 'cur')], [w_pw1, b_pw1], [(2 * D, F32), (D, F32)])


def _layernorm_parts(uc):
    xc = uc - jnp.mean(uc, axis=-1, keepdims=True)
    r = lax.rsqrt(jnp.mean(xc * xc, axis=-1, keepdims=True) + EPS)
    return xc * r, r


def _conv_dw(name, u, w_dw, b_dw, ln_g, ln_b, w_pw2, b_pw2, x, g1, gt):
    T, D = u.shape
    tm = min(256, T)

    def body(i, n, rr, cc, oo, aa, ss):
        ext = ss[0]
        ext[0:HALO, :] = jnp.where(i > 0, rr[1][tm - HALO:tm, :], 0.0)
        ext[HALO:HALO + tm, :] = rr[0][...]
        uc = jnp.zeros((tm, D), F32) + cc[1][...]
        for kk in range(CONV_W):
            uc = uc + ext[pl.ds(HALO - (CONV_W - 1) + kk, tm), :] * cc[0][kk:kk + 1, :]
        xh, _ = _layernorm_parts(uc)
        ln = xh * cc[2][...] + cc[3][...]
        z = (ln * _sigmoid(ln)).astype(MM)
        y = _dot(z, cc[4][...]) + cc[5][...]
        oo[0][...] = uc
        oo[1][...] = z
        oo[2][...] = y
        oo[3][...] = rr[2][...] + cc[7][...] * _rms(y, cc[6][...])

    return _rows(name, body, T, tm, [(u, 'cur'), (u, 'prev'), (x, 'cur')], [w_dw, b_dw, ln_g, ln_b, w_pw2, b_pw2, g1, gt],
                 [(D, F32), (D, MM), (D, F32), (D, F32)], scratch=[pltpu.VMEM((tm + HALO, D), F32)])


def _conv_bwd1(name, dy, uc, w_pw2, ln_g, ln_b):
    T, D = uc.shape

    def body(i, n, rr, cc, oo, aa, ss):
        dz = _dot_nt(rr[0][...], cc[0][...])
        xh, r = _layernorm_parts(rr[1][...])
        g = cc[1][...]
        ln = xh * g + cc[2][...]
        sg = _sigmoid(ln)
        dln = dz * (sg * (1.0 + ln * (1.0 - sg)))
        aa[0][...] += jnp.sum(dln * xh, axis=0, keepdims=True)
        aa[1][...] += jnp.sum(dln, axis=0, keepdims=True)
        dxh = dln * g
        duc = r * (dxh - jnp.mean(dxh, axis=-1, keepdims=True) - xh * jnp.mean(dxh * xh, axis=-1, keepdims=True))
        aa[2][...] += jnp.sum(duc, axis=0, keepdims=True)
        oo[0][...] = duc

    return _rows(name, body, T, 256, [(dy, 'cur'), (uc, 'cur')], [w_pw2, ln_g, ln_b], [(D, F32)], accs=[(1, D)] * 3)


def _conv_bwd2(name, duc, u, a, x, dxo, w_dw, w_pw1, g0, sc):
    T, D = u.shape
    tm = min(256, T)

    def body(i, n, rr, cc, oo, aa, ss):
        extd, extu = ss[0], ss[1]
        dcur = rr[0][...]
        extd[0:tm, :] = dcur
        extd[tm:tm + HALO, :] = jnp.where(i < n - 1, rr[1][0:HALO, :], 0.0)
        extu[0:HALO, :] = jnp.where(i > 0, rr[3][tm - HALO:tm, :], 0.0)
        extu[HALO:HALO + tm, :] = rr[2][...]
        du = jnp.zeros((tm, D), F32)
        for kk in range(CONV_W):
            du = du + extd[pl.ds(CONV_W - 1 - kk, tm), :] * cc[0][kk:kk + 1, :]
            aa[0][kk:kk + 1, :] += jnp.sum(dcur * extu[pl.ds(HALO - (CONV_W - 1) + kk, tm), :], axis=0, keepdims=True)
        av = rr[4][...]
        a1, sg = av[:, :D], _sigmoid(av[:, D:])
        da = jnp.concatenate([du * sg, du * a1 * (sg * (1.0 - sg))], axis=1)
        aa[1][...] += jnp.sum(da, axis=0, keepdims=True)
        da = da.astype(MM)
        oo[0][...] = da
        dx, dsh, dsc, dg0 = _prenorm_bwd(rr[5][...], cc[2][...], cc[3][...], _dot_nt(da, cc[1][...]))
        oo[1][...] = rr[6][...] + dx
        aa[2][...] += dsh
        aa[3][...] += dsc
        aa[4][...] += dg0

    return _rows(name, body, T, tm,
                 [(duc, 'cur'), (duc, 'next'), (u, 'cur'), (u, 'prev'), (a, 'cur'), (x, 'cur'), (dxo, 'cur')],
                 [w_dw, w_pw1, g0, sc], [(2 * D, MM), (D, F32)],
                 accs=[(32, D), (1, 2 * D), (1, D), (1, D), (1, D)],
                 scratch=[pltpu.VMEM((tm + HALO, D), F32), pltpu.VMEM((tm + HALO, D), F32)])


PHALO = 16


def _pool_fwd(name, h, w, b, scale, x, g1, gt):
    T, D = h.shape
    G = len(POOL_WINDOWS)
    Cg = D // G
    tm = min(256, T)

    def body(i, n, rr, cc, oo, aa, ss):
        ext = ss[0]
        ext[0:PHALO, :] = jnp.where(i > 0, rr[1][tm - PHALO:tm, :], 0.0)
        ext[PHALO:PHALO + tm, :] = rr[0][...]
        t_glob = i * tm + lax.broadcasted_iota(jnp.int32, (tm, 1), 0)
        ps, ys = [], []
        for g, win in enumerate(POOL_WINDOWS):
            cols = slice(g * Cg, (g + 1) * Cg)
            s = ext[pl.ds(PHALO, tm), cols]
            for j in range(1, win):
                s = s + ext[pl.ds(PHALO - j, tm), cols]
            cnt = jnp.minimum(t_glob + 1, win).astype(F32)
            p = (s / cnt - ext[pl.ds(PHALO, tm), cols]).astype(MM)
            ps.append(p)
            ys.append(_dot(p, cc[0][g]) + cc[1][:, cols])
        ypre = jnp.concatenate(ys, axis=1)
        y = ypre * cc[2][...]
        oo[0][...] = jnp.concatenate(ps, axis=1)
        oo[1][...] = ypre
        oo[2][...] = y
        oo[3][...] = rr[2][...] + cc[4][...] * _rms(y, cc[3][...])

    return _rows(name, body, T, tm, [(h, 'cur'), (h, 'prev'), (x, 'cur')], [w, b, scale, g1, gt],
                 [(D, MM), (D, F32), (D, F32), (D, F32)], scratch=[pltpu.VMEM((tm + PHALO, D), F32)])


def _pool_bwd1(name, dy, ypre, scale, w):
    T, D = ypre.shape
    G = len(POOL_WINDOWS)
    Cg = D // G

    def body(i, n, rr, cc, oo, aa, ss):
        dyv = rr[0][...].astype(F32)
        aa[0][...] += jnp.sum(dyv * rr[1][...], axis=0, keepdims=True)
        dypre = dyv * cc[0][...]
        aa[1][...] += jnp.sum(dypre, axis=0, keepdims=True)
        dypre = dypre.astype(MM)
        oo[1][...] = dypre
        oo[0][...] = jnp.concatenate([_dot_nt(dypre[:, g * Cg:(g + 1) * Cg], cc[1][g]) for g in range(G)], axis=1)

    return _rows(name, body, T, 256, [(dy, 'cur'), (ypre, 'cur')], [scale, w], [(D, F32), (D, MM)], accs=[(1, D)] * 2)


def _pool_bwd2(name, dp, x, dxo, g0, sc):
    T, D = x.shape
    G = len(POOL_WINDOWS)
    Cg = D // G
    tm = min(256, T)

    def body(i, n, rr, cc, oo, aa, ss):
        ext = ss[0]
        t_glob = i * tm + lax.broadcasted_iota(jnp.int32, (tm, 1), 0)
        dcur = rr[0][...]
        dhs = []
        for g, win in enumerate(POOL_WINDOWS):
            cols = slice(g * Cg, (g + 1) * Cg)
            cnt = jnp.minimum(t_glob + 1, win).astype(F32)
            ext[0:tm, cols] = dcur[:, cols] / cnt
            ext[tm:tm + PHALO, cols] = jnp.where(i < n - 1, rr[1][0:PHALO, cols] * (1.0 / win), 0.0)
        for g, win in enumerate(POOL_WINDOWS):
            cols = slice(g * Cg, (g + 1) * Cg)
            s = ext[pl.ds(0, tm), cols]
            for j in range(1, win):
                s = s + ext[pl.ds(j, tm), cols]
            dhs.append(s - dcur[:, cols])
        dx, dsh, dsc, dg0 = _prenorm_bwd(rr[2][...], cc[0][...], cc[1][...], jnp.concatenate(dhs, axis=1))
        oo[0][...] = rr[3][...] + dx
        aa[0][...] += dsh
        aa[1][...] += dsc
        aa[2][...] += dg0

    return _rows(name, body, T, tm, [(dp, 'cur'), (dp, 'next'), (x, 'cur'), (dxo, 'cur')], [g0, sc], [(D, F32)],
                 accs=[(1, D)] * 3, scratch=[pltpu.VMEM((tm + PHALO, D), F32)])


def _loss_head(x, tgt):
    T, D = x.shape

    def body(i, n, rr, cc, oo, aa, ss):
        err = rr[0][...] - rr[1][...]
        oo[0][...] = err * (1.0 / D)
        aa[0][...] += jnp.sum(err * err, axis=0, keepdims=True)

        @pl.when(i == n - 1)
        def _():
            aa[1][...] = jnp.broadcast_to(jnp.sum(aa[0][...], axis=1, keepdims=True) * (0.5 / D), (1, 128))

    dx, _, loss_row = _rows("loss_head", body, T, 512, [(x, 'cur'), (tgt, 'cur')], [], [(D, F32)], accs=[(1, D), (1, 128)])
    return dx, loss_row


def _adamw(name, w, g, m, v):
    shape = w.shape
    C = shape[-1]
    R = w.size // C
    w2, g2, m2, v2 = (t.reshape(R, C) for t in (w, g, m, v))
    br = R
    if R * C * 4 > (1 << 20):
        br = 8
        while br * 2 * C * 4 <= (1 << 20) and R % (br * 2) == 0:
            br *= 2
    b1c = 1.0 - ADAM_B1 ** ADAM_STEP
    b2c = 1.0 - ADAM_B2 ** ADAM_STEP

    def body(w_ref, g_ref, m_ref, v_ref, d_ref, mo_ref, vo_ref):
        gv = g_ref[...]
        mn = ADAM_B1 * m_ref[...] + (1.0 - ADAM_B1) * gv
        vn = ADAM_B2 * v_ref[...] + (1.0 - ADAM_B2) * (gv * gv)
        d_ref[...] = -ADAM_LR * ((mn / b1c) / (jnp.sqrt(vn / b2c) + ADAM_EPS) + ADAM_WD * w_ref[...])
        mo_ref[...] = mn
        vo_ref[...] = vn

    spec = pl.BlockSpec((br, C), lambda r: (r, 0))
    outs = pl.pallas_call(body, grid=(R // br,), in_specs=[spec] * 4, out_specs=[spec] * 3,
                          out_shape=[jax.ShapeDtypeStruct((R, C), F32)] * 3, name=name,
                          compiler_params=_cparams(("arbitrary",)))(w2, g2, m2, v2)
    return tuple(t.reshape(shape) for t in outs)


def _to_shards(full, ax):
    s = full.shape
    return jnp.moveaxis(full.reshape(s[:ax] + (N_DEV, s[ax] // N_DEV) + s[ax + 1:]), ax, 0)


def _unshard(g, ax):
    r = jnp.moveaxis(g, 0, ax)
    s = r.shape
    return r.reshape(s[:ax] + (s[ax] * s[ax + 1],) + s[ax + 2:])


def _pack(parts, dtype, row_mult):
    lead = parts[0].shape[:-1]
    flat = jnp.concatenate([p.astype(dtype) for p in parts], axis=-1)
    n = flat.shape[-1]
    per = row_mult * 1024
    tot = -(-n // per) * per
    flat = jnp.pad(flat, [(0, 0)] * len(lead) + [(0, tot - n)])
    return flat.reshape(lead + (tot // 1024, 1024))


def _pad_heads(w, lo, hi):
    K = w.shape[0]
    r = w.reshape(K, N_HEADS, -1)[:, :, lo:hi]
    return jnp.pad(r, ((0, 0), (0, 0), (0, HEAD_PAD - (hi - lo)))).reshape(K, N_HEADS * HEAD_PAD)


def kernel(x, c, positions, ada_w, ada_b, norm_g, mla_w_dq, mla_q_norm_g, mla_w_uq, mla_w_dkv, mla_kv_norm_g, mla_w_ukv, mla_w_o, conv_w_pw1, conv_b_pw1, conv_w_dw, conv_b_dw, conv_ln_g, conv_ln_b, conv_w_pw2, conv_b_pw2, pool_w, pool_b, pool_scale, ffn_w1, ffn_w2, loss_target, m_ada_w, m_ada_b, m_norm_g, m_mla_w_dq, m_mla_q_norm_g, m_mla_w_uq, m_mla_w_dkv, m_mla_kv_norm_g, m_mla_w_ukv, m_mla_w_o, m_conv_w_pw1, m_conv_b_pw1, m_conv_w_dw, m_conv_b_dw, m_conv_ln_g, m_conv_ln_b, m_conv_w_pw2, m_conv_b_pw2, m_pool_w, m_pool_b, m_pool_scale, m_ffn_w1, m_ffn_w2, v_ada_w, v_ada_b, v_norm_g, v_mla_w_dq, v_mla_q_norm_g, v_mla_w_uq, v_mla_w_dkv, v_mla_kv_norm_g, v_mla_w_ukv, v_mla_w_o, v_conv_w_pw1, v_conv_b_pw1, v_conv_w_dw, v_conv_b_dw, v_conv_ln_g, v_conv_ln_b, v_conv_w_pw2, v_conv_b_pw2, v_pool_w, v_pool_b, v_pool_scale, v_ffn_w1, v_ffn_w2):
    args = dict(locals())
    W = {n: args[n] for n, _ in WEIGHTS}
    M1 = {n: args['m_' + n] for n, _ in WEIGHTS}
    V2 = {n: args['v_' + n] for n, _ in WEIGHTS}
    D = D_MODEL
    T = x.shape[1]
    L = ffn_w1.shape[0]
    xi, yi, ci = _place()
    me = 4 * xi + 2 * yi + ci
    n_ada = ada_w.shape[2]

    small_sizes = [W[n].size for n in SMALL]
    small_in = _pack([c.reshape(-1)] + [W[n].reshape(-1) for n in SMALL], F32, 8)
    small_all = _ag_small("ag_small_params", small_in).reshape(N_DEV, -1)
    c_all = small_all[:, :D]
    Ws = {}
    off = D
    for n, sz in zip(SMALL, small_sizes):
        Ws[n] = _unshard(small_all[:, off:off + sz].reshape((N_DEV,) + W[n].shape), SHARD_AXIS[n])
        off += sz
    c16 = jnp.pad(c_all, ((0, 16 - N_DEV), (0, 0)))

    ada_b_cols = lax.dynamic_slice_in_dim(ada_b, me * n_ada, n_ada, axis=1).reshape(L, 1, n_ada)
    mod_part = _mod_part(c16, ada_w, ada_b_cols)[:, :N_DEV]
    mod_all = _ag_small("ag_mod", mod_part.reshape(L * N_DEV, n_ada)).reshape(N_DEV, L, N_DEV, n_ada)
    mod_mine = lax.dynamic_index_in_dim(mod_all, me, axis=2, keepdims=False)
    mod = jnp.transpose(mod_mine, (1, 0, 2)).reshape(L, 6, 1, D)

    big_sizes = [W[n].size for n in BIG]
    big_all = _ag_big("ag_weights", _pack([W[n].reshape(-1) for n in BIG], MM, 32)).reshape(N_DEV, -1)
    Wb = {}
    off = 0
    for n, sz in zip(BIG, big_sizes):
        Wb[n] = big_all[:, off:off + sz].reshape((N_DEV,) + W[n].shape)
        off += sz
    full = lambda n: _unshard(Wb[n], SHARD_AXIS[n])
    w_dq, w_uq, w_dkv, w_ukv, w_o = full('mla_w_dq'), full('mla_w_uq'), full('mla_w_dkv'), full('mla_w_ukv'), full('mla_w_o')
    w_pw1, w_pw2, w_pool = full('conv_w_pw1'), full('conv_w_pw2'), full('pool_w')
    w1s = jnp.moveaxis(Wb['ffn_w1'], 1, 0)
    w2 = full('ffn_w2')
    n_mla = w_dq.shape[0]
    w_uq_p = [_pad_heads(w_uq[j], 0, QK_NOPE + QK_ROPE) for j in range(n_mla)]
    w_ukv_k = [_pad_heads(w_ukv[j], 0, QK_NOPE) for j in range(n_mla)]
    w_ukv_v = [_pad_heads(w_ukv[j], QK_NOPE, QK_NOPE + V_HEAD) for j in range(n_mla)]
    w_dkv_p = [jnp.pad(jnp.concatenate([w_dkv[j][:, :KV_LORA], jnp.zeros((D, QK_NOPE), MM), w_dkv[j][:, KV_LORA:]], axis=1),
                       ((0, 0), (0, HEAD_PAD - QK_NOPE - QK_ROPE))) for j in range(n_mla)]
    w_o_p = [jnp.pad(w_o[j].reshape(N_HEADS, V_HEAD, D), ((0, 0), (0, HEAD_PAD - V_HEAD), (0, 0))).reshape(N_HEADS * HEAD_PAD, D)
             for j in range(n_mla)]
    w_dw32 = jnp.pad(Ws['conv_w_dw'], ((0, 0), (0, 32 - CONV_W), (0, 0)))
    row = lambda t: t.reshape(1, -1)

    half = QK_ROPE // 2
    inv_freq = ROPE_THETA ** (-jnp.arange(0, QK_ROPE, 2, dtype=F32) / QK_ROPE)
    invf = jnp.zeros((1, HEAD_PAD), F32).at[0, QK_NOPE:QK_NOPE + half].set(inv_freq).at[0, QK_NOPE + half:QK_NOPE + QK_ROPE].set(inv_freq)
    rC, rS1, rS2 = _rope_tables(positions.reshape(T, 1).astype(F32), invf)

    xs = x.reshape(T, D)
    saved = []
    for i in range(L):
        kind, j = i % 3, i // 3
        sh_m, sc_m, gt_m, sh_f, sc_f, gt_f = (mod[i, r] for r in range(6))
        g = [row(Ws['norm_g'][i, r]) for r in range(4)]
        st = dict(x0=xs)
        if kind == 0:
            h = _prenorm(f"prenorm_m{i}", xs, g[0], sc_m, sh_m, MM)
            cq_raw, cq, ckv_raw, ckv, q, k, v = _mla_proj(f"mla_proj{i}", h, rC, rS1, rS2, w_dq[j], row(Ws['mla_q_norm_g'][j]), w_uq_p[j],
                                                          w_dkv_p[j], row(Ws['mla_kv_norm_g'][j]), w_ukv_k[j], w_ukv_v[j])
            o, lse = _attn_fwd(f"attn_fwd{i}", q, k, v)
            y, xs = _mm_post(f"mla_out{i}", o, w_o_p[j], None, xs, g[1], gt_m)
            st.update(h=h, cq_raw=cq_raw, cq=cq, ckv_raw=ckv_raw, ckv=ckv, q=q, k=k, v=v, o=o, lse=lse, y=y)
        elif kind == 1:
            h = _prenorm(f"prenorm_m{i}", xs, g[0], sc_m, sh_m, MM)
            a, u = _conv_glu(f"conv_glu{i}", h, w_pw1[j], row(W['conv_b_pw1'][j]))
            uc, z, y, xs = _conv_dw(f"conv_dw{i}", u, w_dw32[j], row(W['conv_b_dw'][j]), row(W['conv_ln_g'][j]), row(W['conv_ln_b'][j]),
                                    w_pw2[j], row(W['conv_b_pw2'][j]), xs, g[1], gt_m)
            st.update(h=h, a=a, u=u, uc=uc, z=z, y=y)
        else:
            h = _prenorm(f"prenorm_m{i}", xs, g[0], sc_m, sh_m, F32)
            p, ypre, y, xs = _pool_fwd(f"pool_fwd{i}", h, w_pool[j], row(Ws['pool_b'][j]), row(Ws['pool_scale'][j]), xs, g[1], gt_m)
            st.update(p=p, ypre=ypre, y=y)
        st['x1'] = xs
        hf = _prenorm(f"prenorm_f{i}", xs, g[2], sc_f, sh_f, MM)
        af, yf, xs = _ffn_fwd(f"ffn_fwd{i}", hf, w1s[i], w2[i], xs, g[3], gt_f)
        st.update(hf=hf, af=af, yf=yf)
        saved.append(st)

    dx, loss_row = _loss_head(xs, loss_target.reshape(T, D))

    G = {}
    dmod = [None] * L
    dnorm = [None] * L
    for i in reversed(range(L)):
        kind, j = i % 3, i // 3
        sh_m, sc_m, gt_m, sh_f, sc_f, gt_f = (mod[i, r] for r in range(6))
        g = [row(Ws['norm_g'][i, r]) for r in range(4)]
        st = saved[i]
        dy, dg3, dgt_f, _ = _post_bwd(f"post_bwd_f{i}", dx, st['yf'], g[3], gt_f)
        da, dx, dsh_f, dsc_f, dg2 = _ffn_bwd(f"ffn_bwd{i}", dy, st['af'], w1s[i], w2[i], st['x1'], dx, g[2], sc_f)
        G.setdefault('ffn_w1', [None] * L)[i] = _mm_tn(f"ffn_dw1_{i}", st['hf'], da, col_shards=N_DEV)
        G.setdefault('ffn_w2', [None] * L)[i] = _mm_tn(f"ffn_dw2_{i}", st['af'], dy, sqrelu=True)
        dy, dg1, dgt_m, dysum = _post_bwd(f"post_bwd_m{i}", dx, st['y'], g[1], gt_m)
        if kind == 0:
            do = _mm_nt_rows(f"mla_do{i}", dy, w_o_p[j])
            dq, dk, dv = _attn_bwd(f"attn_bwd{i}", st['q'], st['k'], st['v'], st['o'], do, st['lse'])
            dq_pre, dcq_raw, dckv_all, dx, dqg, dkvg, dsh_m, dsc_m, dg0 = _mla_proj_bwd(
                f"mla_proj_bwd{i}", dq, dk, dv, rC, rS1, rS2, st['cq_raw'], st['ckv_raw'], st['x0'], dx, w_uq_p[j], w_ukv_k[j], w_ukv_v[j],
                w_dq[j], w_dkv_p[j], row(Ws['mla_q_norm_g'][j]), row(Ws['mla_kv_norm_g'][j]), g[0], sc_m)
            dwo = _mm_tn(f"mla_dwo{i}", st['o'], dy)
            dwuq = _mm_tn(f"mla_dwuq{i}", st['cq'], dq_pre)
            dwk = _mm_tn(f"mla_dwukvk{i}", st['ckv'], dk)
            dwv = _mm_tn(f"mla_dwukvv{i}", st['ckv'], dv)
            dwdq = _mm_tn(f"mla_dwdq{i}", st['h'], dcq_raw)
            dwdkv = _mm_tn(f"mla_dwdkv{i}", st['h'], dckv_all)
            G.setdefault('mla_w_o', [None] * n_mla)[j] = dwo.reshape(N_HEADS, HEAD_PAD, D)[:, :V_HEAD].reshape(N_HEADS * V_HEAD, D)
            G.setdefault('mla_w_uq', [None] * n_mla)[j] = dwuq.reshape(Q_LORA, N_HEADS, HEAD_PAD)[:, :, :QK_NOPE + QK_ROPE].reshape(Q_LORA, -1)
            G.setdefault('mla_w_ukv', [None] * n_mla)[j] = jnp.concatenate(
                [dwk.reshape(KV_LORA, N_HEADS, HEAD_PAD)[:, :, :QK_NOPE], dwv.reshape(KV_LORA, N_HEADS, HEAD_PAD)[:, :, :V_HEAD]], axis=2).reshape(KV_LORA, -1)
            G.setdefault('mla_w_dq', [None] * n_mla)[j] = dwdq
            G.setdefault('mla_w_dkv', [None] * n_mla)[j] = jnp.concatenate([dwdkv[:, :KV_LORA], dwdkv[:, KV_LORA + QK_NOPE:KV_LORA + QK_NOPE + QK_ROPE]], axis=1)
            G.setdefault('mla_q_norm_g', [None] * n_mla)[j] = dqg[0]
            G.setdefault('mla_kv_norm_g', [None] * n_mla)[j] = dkvg[0]
        elif kind == 1:
            duc, dlng, dlnb, dbdw = _conv_bwd1(f"conv_bwd1_{i}", dy, st['uc'], w_pw2[j], row(W['conv_ln_g'][j]), row(W['conv_ln_b'][j]))
            da, dx, dwdw, dbpw1, dsh_m, dsc_m, dg0 = _conv_bwd2(f"conv_bwd2_{i}", duc, st['u'], st['a'], st['x0'], dx, w_dw32[j], w_pw1[j], g[0], sc_m)
            G['conv_w_pw2'] = [_mm_tn(f"conv_dwpw2_{i}", st['z'], dy)]
            G['conv_w_pw1'] = [_mm_tn(f"conv_dwpw1_{i}", st['h'], da)]
            G['conv_w_dw'] = [dwdw[:CONV_W]]
            G['conv_b_pw1'], G['conv_b_dw'], G['conv_ln_g'], G['conv_ln_b'], G['conv_b_pw2'] = [dbpw1[0]], [dbdw[0]], [dlng[0]], [dlnb[0]], [dysum[0]]
        else:
            dp, dypre, dscale, dpb = _pool_bwd1(f"pool_bwd1_{i}", dy, st['ypre'], row(Ws['pool_scale'][j]), w_pool[j])
            dx, dsh_m, dsc_m, dg0 = _pool_bwd2(f"pool_bwd2_{i}", dp, st['x0'], dx, g[0], sc_m)
            G['pool_w'] = [_mm_tn(f"pool_dw{i}", st['p'], dypre, diag=len(POOL_WINDOWS))]
            G['pool_b'] = [dpb.reshape(len(POOL_WINDOWS), -1)]
            G['pool_scale'] = [dscale[0]]
        dmod[i] = jnp.concatenate([dsh_m, dsc_m, dgt_m, dsh_f, dsc_f, dgt_f], axis=1)
        dnorm[i] = jnp.concatenate([dg0, dg1, dg2, dg3], axis=0)
    G['norm_g'] = dnorm
    grad_x = dx.reshape(x.shape)

    rs_names = [n for n, ax in WEIGHTS if ax is not None and n != 'ada_w']
    parts = []
    for n in rs_names:
        if n == 'ffn_w1':
            sh = jnp.stack(G[n], axis=1)
        else:
            sh = _to_shards(jnp.stack(G[n], axis=0), SHARD_AXIS[n])
        parts.append(sh.reshape(N_DEV, -1))
    packed = _pack(parts, F32, 256)
    R = packed.shape[1]
    p4 = packed.reshape(4, 2, R, 1024)
    pair = _pair_sum(p4, _rs_pair("rs_pair", p4), ci)
    red = _chip_sum(_rs_chips("rs_chips", pair)).reshape(-1)
    grads = {}
    off = 0
    for n in rs_names:
        grads[n] = red[off:off + W[n].size].reshape(W[n].shape)
        off += W[n].size

    dmod_mine = jnp.concatenate(dmod, axis=1).reshape(-1)
    fin_in = _pack([dmod_mine] + [G[n][0].reshape(-1) for n in REPL] + [loss_row.reshape(-1)], F32, 8)
    fin_all = _ag_small("ag_final", fin_in)
    fin_sum = _sum_devices("final_sum", fin_all).reshape(-1)
    nm = L * 6 * D
    grads['ada_b'] = fin_sum[:nm].reshape(L, 6 * D)
    off = nm
    for n in REPL:
        grads[n] = fin_sum[off:off + W[n].size].reshape(W[n].shape)
        off += W[n].size
    loss = fin_sum[off]
    dmod_all = fin_all.reshape(N_DEV, -1)[:, :nm].reshape(N_DEV, L, 6 * D)
    dmod_cols = lax.dynamic_slice_in_dim(dmod_all, me * n_ada, n_ada, axis=2)
    dmod16 = jnp.pad(jnp.transpose(dmod_cols, (1, 0, 2)), ((0, 0), (0, 16 - N_DEV), (0, 0)))
    grads['ada_w'] = _ada_w_grad(c16, dmod16)

    deltas, new_m, new_v = {}, {}, {}
    for n, _ in WEIGHTS:
        deltas[n], new_m[n], new_v[n] = _adamw("adamw_" + n, W[n], grads[n], M1[n], V2[n])
    names = [n for n, _ in WEIGHTS]
    return (loss, grad_x, *[grads[n] for n in names], *[deltas[n] for n in names], *[new_m[n] for n in names],
            *[new_v[n] for n in names])
```

```python
import functools
import math

import jax
import jax.numpy as jnp
from jax import lax
from jax.experimental import pallas as pl
from jax.experimental.pallas import tpu as pltpu

F32 = jnp.float32
MM = jnp.bfloat16
EPS = 1e-6
NEG = -1e30
N_DEV = 8
VMEM_LIMIT = 48 * 1024 * 1024
MESH = pl.DeviceIdType.MESH

D_MODEL = 1024
N_HEADS = 16
HEAD_PAD = 128
QK_NOPE, QK_ROPE, V_HEAD = 64, 32, 64
Q_LORA, KV_LORA = 384, 256
CHUNK = 64
CONV_W = 31
POOL_WINDOWS = (2, 4, 8, 16)
ROPE_THETA = 10000.0
ATT_SCALE = 1.0 / math.sqrt(QK_NOPE + QK_ROPE)

ADAM_LR, ADAM_B1, ADAM_B2, ADAM_EPS, ADAM_WD, ADAM_STEP = 0.001, 0.9, 0.999, 1e-08, 0.01, 10

WEIGHTS = [('ada_w', 2), ('ada_b', None), ('norm_g', 2), ('mla_w_dq', 1), ('mla_q_norm_g', 1), ('mla_w_uq', 2),
           ('mla_w_dkv', 1), ('mla_kv_norm_g', 1), ('mla_w_ukv', 2), ('mla_w_o', 1), ('conv_w_pw1', 2),
           ('conv_b_pw1', None), ('conv_w_dw', 2), ('conv_b_dw', None), ('conv_ln_g', None), ('conv_ln_b', None),
           ('conv_w_pw2', 1), ('conv_b_pw2', None), ('pool_w', 2), ('pool_b', 2), ('pool_scale', 1),
           ('ffn_w1', 2), ('ffn_w2', 1)]
SHARD_AXIS = dict(WEIGHTS)
BIG = ['mla_w_dq', 'mla_w_uq', 'mla_w_dkv', 'mla_w_ukv', 'mla_w_o', 'conv_w_pw1', 'conv_w_pw2', 'pool_w', 'ffn_w1', 'ffn_w2']
SMALL = ['norm_g', 'mla_q_norm_g', 'mla_kv_norm_g', 'conv_w_dw', 'pool_b', 'pool_scale']
REPL = ['conv_b_pw1', 'conv_b_dw', 'conv_ln_g', 'conv_ln_b', 'conv_b_pw2']


def _dot(a, b):
    return jnp.dot(a.astype(MM), b.astype(MM), preferred_element_type=F32)


def _dot_nt(a, b):
    return lax.dot_general(a.astype(MM), b.astype(MM), (((1,), (1,)), ((), ())), preferred_element_type=F32)


def _dot_tn(a, b):
    return lax.dot_general(a.astype(MM), b.astype(MM), (((0,), (0,)), ((), ())), preferred_element_type=F32)


def _sigmoid(x):
    return 1.0 / (1.0 + jnp.exp(-x))


def _rstd(x):
    return lax.rsqrt(jnp.mean(x * x, axis=-1, keepdims=True) + EPS)


def _rms(x, g):
    return x * _rstd(x) * g


def _rms_bwd(x, g, dout):
    r = _rstd(x)
    xn = x * r
    dg = jnp.sum(dout * xn, axis=0, keepdims=True)
    dxn = dout * g
    dx = r * (dxn - xn * jnp.mean(dxn * xn, axis=-1, keepdims=True))
    return dx, dg


def _prenorm_bwd(x, g0, sc, dh):
    r = _rstd(x)
    xn = x * r
    dsh = jnp.sum(dh, axis=0, keepdims=True)
    dsc = jnp.sum(dh * (xn * g0), axis=0, keepdims=True)
    dn = dh * (1.0 + sc)
    dg0 = jnp.sum(dn * xn, axis=0, keepdims=True)
    dxn = dn * g0
    dx = r * (dxn - xn * jnp.mean(dxn * xn, axis=-1, keepdims=True))
    return dx, dsh, dsc, dg0


def _cparams(sem):
    return pltpu.CompilerParams(dimension_semantics=sem, vmem_limit_bytes=VMEM_LIMIT)


def _rows(name, body, n_rows, tm, rows, consts, outs, accs=(), scratch=()):
    tm = min(tm, n_rows)
    nblk = n_rows // tm
    nr, nc, no, na = len(rows), len(consts), len(outs), len(accs)
    in_specs, args = [], []
    for a, kind in rows:
        if kind == 'cur':
            im = lambda i: (i, 0)
        elif kind == 'prev':
            im = lambda i: (jnp.maximum(i - 1, 0), 0)
        else:
            im = lambda i: (jnp.minimum(i + 1, nblk - 1), 0)
        in_specs.append(pl.BlockSpec((tm, a.shape[1]), im))
        args.append(a)
    for a in consts:
        in_specs.append(pl.BlockSpec(a.shape, lambda i, nd=a.ndim: (0,) * nd))
        args.append(a)
    out_specs = [pl.BlockSpec((tm, c), lambda i: (i, 0)) for c, _ in outs]
    out_specs += [pl.BlockSpec(s, lambda i, nd=len(s): (0,) * nd) for s in accs]
    out_shape = [jax.ShapeDtypeStruct((n_rows, c), dt) for c, dt in outs]
    out_shape += [jax.ShapeDtypeStruct(s, F32) for s in accs]

    def kern(*refs):
        i = pl.program_id(0)
        rr = refs[:nr]
        cc = refs[nr:nr + nc]
        oo = refs[nr + nc:nr + nc + no]
        aa = refs[nr + nc + no:nr + nc + no + na]
        ss = refs[nr + nc + no + na:]

        @pl.when(i == 0)
        def _():
            for a in aa:
                a[...] = jnp.zeros(a.shape, F32)

        body(i, nblk, rr, cc, oo, aa, ss)

    return pl.pallas_call(kern, grid=(nblk,), in_specs=in_specs, out_specs=out_specs, out_shape=out_shape,
                          scratch_shapes=list(scratch), name=name, compiler_params=_cparams(("arbitrary",)))(*args)


def _place():
    return lax.axis_index("x"), lax.axis_index("y"), lax.axis_index("c")


def _ag_small(name, xs):
    R, C = xs.shape

    def body(x_ref, out_ref, send_sems, recv_sems):
        x, y, c = _place()
        me = 4 * x + 2 * y + c
        out_ref[me] = x_ref[...]
        copies = []
        for k in range(1, N_DEV):
            peer = ((1 - x) if k & 4 else x, (1 - y) if k & 2 else y, (1 - c) if k & 1 else c)
            cp = pltpu.make_async_remote_copy(src_ref=x_ref, dst_ref=out_ref.at[me], send_sem=send_sems.at[k - 1],
                                              recv_sem=recv_sems.at[k - 1], device_id=peer, device_id_type=MESH)
            cp.start()
            copies.append(cp)
        for cp in copies:
            cp.wait()

    return pl.pallas_call(
        body, out_shape=jax.ShapeDtypeStruct((N_DEV, R, C), xs.dtype),
        in_specs=[pl.BlockSpec(memory_space=pltpu.VMEM)], out_specs=pl.BlockSpec(memory_space=pltpu.VMEM),
        scratch_shapes=[pltpu.SemaphoreType.DMA((N_DEV - 1,)), pltpu.SemaphoreType.DMA((N_DEV - 1,))], name=name)(xs)


def _ag_big(name, xs):
    nt = len(xs)

    def body(*refs):
        x_refs, out_refs = refs[:nt], refs[nt:2 * nt]
        send_sems, recv_sems, local_sems = refs[2 * nt:]
        x, y, c = _place()
        me, sibling = (x, y, c), (x, y, 1 - c)
        chips = [(1 - x, y), (x, 1 - y), (1 - x, 1 - y)]

        def copy(t, k, block, to, own=False):
            px, py, pc = block
            rows = out_refs[t].at[4 * px + 2 * py + pc]
            return pltpu.make_async_remote_copy(src_ref=x_refs[t] if own else rows, dst_ref=rows, send_sem=send_sems.at[7 * t + k],
                                                recv_sem=recv_sems.at[7 * t + k], device_id=to, device_id_type=MESH)

        mine = [pltpu.make_async_copy(x_refs[t], out_refs[t].at[4 * x + 2 * y + c], local_sems.at[t]) for t in range(nt)]
        for cp in mine:
            cp.start()
        first = []
        for t in range(nt):
            first.append(copy(t, 0, me, sibling, own=True))
            first += [copy(t, 1 + j, me, (*chip, c), own=True) for j, chip in enumerate(chips)]
        for cp in first:
            cp.start()
        passed = []
        for t in range(nt):
            for j, chip in enumerate(chips):
                copy(t, 1 + j, (*chip, c), me).wait_recv()
                cp = copy(t, 4 + j, (*chip, c), sibling)
                cp.start()
                passed.append(cp)
        for t in range(nt):
            copy(t, 0, sibling, me).wait_recv()
            for j, chip in enumerate(chips):
                copy(t, 4 + j, (*chip, 1 - c), me).wait_recv()
        for cp in first + passed:
            cp.wait_send()
        for cp in mine:
            cp.wait()

    hbm = pl.BlockSpec(memory_space=pl.ANY)
    return pl.pallas_call(
        body, out_shape=[jax.ShapeDtypeStruct((N_DEV,) + t.shape, t.dtype) for t in xs],
        in_specs=[hbm] * nt, out_specs=[hbm] * nt,
        scratch_shapes=[pltpu.SemaphoreType.DMA((7 * nt,)), pltpu.SemaphoreType.DMA((7 * nt,)), pltpu.SemaphoreType.DMA((nt,))],
        name=name)(*xs)


def _rs_pair(name, ps):
    nt = len(ps)

    def body(*refs):
        p_refs, recv_refs = refs[:nt], refs[nt:2 * nt]
        send_sems, recv_sems = refs[2 * nt:]
        x, y, c = _place()
        copies = []
        for t in range(nt):
            for j in range(4):
                cp = pltpu.make_async_remote_copy(src_ref=p_refs[t].at[j, 1 - c], dst_ref=recv_refs[t].at[j], send_sem=send_sems.at[4 * t + j],
                                                  recv_sem=recv_sems.at[4 * t + j], device_id=(x, y, 1 - c), device_id_type=MESH)
                cp.start()
                copies.append(cp)
        for cp in copies:
            cp.wait()

    hbm = pl.BlockSpec(memory_space=pl.ANY)
    return pl.pallas_call(
        body, out_shape=[jax.ShapeDtypeStruct((4,) + p.shape[2:], p.dtype) for p in ps], in_specs=[hbm] * nt, out_specs=[hbm] * nt,
        scratch_shapes=[pltpu.SemaphoreType.DMA((4 * nt,)), pltpu.SemaphoreType.DMA((4 * nt,))], name=name)(*ps)


def _rs_chips(name, ss):
    nt = len(ss)

    def body(*refs):
        s_refs, recv_refs = refs[:nt], refs[nt:2 * nt]
        send_sems, recv_sems, local_sems = refs[2 * nt:]
        x, y, c = _place()
        mine = 2 * x + y
        owns = [pltpu.make_async_copy(s_refs[t].at[mine], recv_refs[t].at[mine], local_sems.at[t]) for t in range(nt)]
        for cp in owns:
            cp.start()
        copies = []
        for t in range(nt):
            for k in range(1, 4):
                px = (1 - x) if k & 2 else x
                py = (1 - y) if k & 1 else y
                cp = pltpu.make_async_remote_copy(src_ref=s_refs[t].at[2 * px + py], dst_ref=recv_refs[t].at[mine],
                                                  send_sem=send_sems.at[3 * t + k - 1], recv_sem=recv_sems.at[3 * t + k - 1],
                                                  device_id=(px, py, c), device_id_type=MESH)
                cp.start()
                copies.append(cp)
        for cp in copies:
            cp.wait()
        for cp in owns:
            cp.wait()

    hbm = pl.BlockSpec(memory_space=pl.ANY)
    return pl.pallas_call(
        body, out_shape=[jax.ShapeDtypeStruct(s_.shape, s_.dtype) for s_ in ss], in_specs=[hbm] * nt, out_specs=[hbm] * nt,
        scratch_shapes=[pltpu.SemaphoreType.DMA((3 * nt,)), pltpu.SemaphoreType.DMA((3 * nt,)), pltpu.SemaphoreType.DMA((nt,))],
        name=name)(*ss)


RS_ROWS = 256


def _pair_sum(name, p, recv, my_c, my_chip):
    _, _, r, c = p.shape
    tr = RS_ROWS

    def body(sc_ref, p_ref, r_ref, o_ref, own_ref):
        s = p_ref[...] + r_ref[...]
        o_ref[...] = s.astype(MM)

        @pl.when(pl.program_id(1) == sc_ref[1])
        def _():
            own_ref[...] = s

    return pl.pallas_call(
        body, grid_spec=pltpu.PrefetchScalarGridSpec(
            num_scalar_prefetch=1, grid=(r // tr, 4),
            in_specs=[pl.BlockSpec((None, None, tr, c), lambda i, j, sc: (j, sc[0], i, 0)),
                      pl.BlockSpec((None, tr, c), lambda i, j, sc: (j, i, 0))],
            out_specs=[pl.BlockSpec((None, tr, c), lambda i, j, sc: (j, i, 0)), pl.BlockSpec((tr, c), lambda i, j, sc: (i, 0))]),
        out_shape=[jax.ShapeDtypeStruct((4, r, c), MM), jax.ShapeDtypeStruct((r, c), F32)], name=name,
        compiler_params=_cparams(("arbitrary", "arbitrary")))(jnp.stack([my_c, my_chip]), p, recv)


def _chip_sum(name, own, recv, my_chip):
    _, r, c = recv.shape
    tr = RS_ROWS

    def body(sc_ref, own_ref, r_ref, o_ref):
        acc = jnp.zeros((tr, c), F32)
        for j in range(4):
            acc = acc + jnp.where(sc_ref[0] == j, own_ref[...], r_ref[j].astype(F32))
        o_ref[...] = acc

    return pl.pallas_call(
        body, grid_spec=pltpu.PrefetchScalarGridSpec(
            num_scalar_prefetch=1, grid=(r // tr,),
            in_specs=[pl.BlockSpec((tr, c), lambda i, sc: (i, 0)), pl.BlockSpec((4, tr, c), lambda i, sc: (0, i, 0))],
            out_specs=pl.BlockSpec((tr, c), lambda i, sc: (i, 0))),
        out_shape=jax.ShapeDtypeStruct((r, c), F32), name=name,
        compiler_params=_cparams(("arbitrary",)))(my_chip.reshape(1), own, recv)


def _reduce_scatter(tag, tensors, my_c, my_chip):
    ps = [t.reshape((4, 2) + t.shape[1:]) for t in tensors]
    recv = _rs_pair(tag + "_pair", ps)
    sums = [_pair_sum(f"{tag}_pair_sum{t}", ps[t], recv[t], my_c, my_chip) for t in range(len(ps))]
    recv2 = _rs_chips(tag + "_chips", [s_[0] for s_ in sums])
    return [_chip_sum(f"{tag}_chip_sum{t}", sums[t][1], recv2[t], my_chip) for t in range(len(ps))]


def _mod_part(c16, ada_w, ada_b_cols):
    L, D, n = ada_w.shape

    def body(c_ref, w_ref, b_ref, o_ref):
        cv = c_ref[...]
        o_ref[...] = _dot(cv * _sigmoid(cv), w_ref[...]) + b_ref[...]

    return pl.pallas_call(
        body, grid=(L,), in_specs=[pl.BlockSpec((16, D), lambda i: (0, 0)), pl.BlockSpec((None, D, n), lambda i: (i, 0, 0)),
                                   pl.BlockSpec((None, 1, n), lambda i: (i, 0, 0))],
        out_specs=pl.BlockSpec((None, 16, n), lambda i: (i, 0, 0)), out_shape=jax.ShapeDtypeStruct((L, 16, n), F32),
        name="ada_mod", compiler_params=_cparams(("arbitrary",)))(c16, ada_w, ada_b_cols)


def _ada_w_grad(c16, dmod16):
    L, _, n = dmod16.shape
    D = c16.shape[1]

    def body(c_ref, d_ref, o_ref):
        cv = c_ref[...]
        o_ref[...] = _dot_tn(cv * _sigmoid(cv), d_ref[...])

    return pl.pallas_call(
        body, grid=(L,), in_specs=[pl.BlockSpec((16, D), lambda i: (0, 0)), pl.BlockSpec((None, 16, n), lambda i: (i, 0, 0))],
        out_specs=pl.BlockSpec((None, D, n), lambda i: (i, 0, 0)), out_shape=jax.ShapeDtypeStruct((L, D, n), F32),
        name="ada_w_grad", compiler_params=_cparams(("arbitrary",)))(c16, dmod16)


def _sum_devices(name, g):
    _, R, C = g.shape

    def body(g_ref, o_ref):
        acc = g_ref[0]
        for d in range(1, N_DEV):
            acc = acc + g_ref[d]
        o_ref[...] = acc

    return pl.pallas_call(body, out_shape=jax.ShapeDtypeStruct((R, C), F32), name=name)(g)


def _prenorm(name, x, g0, sc, sh, dtype):
    T, D = x.shape

    def body(i, n, rr, cc, oo, aa, ss):
        oo[0][...] = (_rms(rr[0][...], cc[0][...]) * (1.0 + cc[1][...]) + cc[2][...]).astype(dtype)

    return _rows(name, body, T, 512, [(x, 'cur')], [g0, sc, sh], [(D, dtype)])[0]


def _post_bwd(name, dxo, y, g1, gt):
    T, D = y.shape

    def body(i, n, rr, cc, oo, aa, ss):
        d = rr[0][...]
        yv = rr[1][...]
        g1v, gtv = cc[0][...], cc[1][...]
        aa[1][...] += jnp.sum(d * _rms(yv, g1v), axis=0, keepdims=True)
        dy, dg1 = _rms_bwd(yv, g1v, d * gtv)
        aa[0][...] += dg1
        aa[2][...] += jnp.sum(dy, axis=0, keepdims=True)
        oo[0][...] = dy.astype(MM)

    return _rows(name, body, T, 512, [(dxo, 'cur'), (y, 'cur')], [g1, gt], [(D, MM)], accs=[(1, D)] * 3)


def _mm_post(name, a, w, bias, x, g1, gt):
    T, D = x.shape
    consts = [w, g1, gt] + ([bias] if bias is not None else [])

    def body(i, n, rr, cc, oo, aa, ss):
        y = _dot(rr[0][...], cc[0][...])
        if bias is not None:
            y = y + cc[3][...]
        oo[0][...] = y
        oo[1][...] = rr[1][...] + cc[2][...] * _rms(y, cc[1][...])

    return _rows(name, body, T, 512, [(a, 'cur'), (x, 'cur')], consts, [(D, F32), (D, F32)])


def _mm_nt_rows(name, a, w):
    T = a.shape[0]
    K = w.shape[0]

    def body(i, n, rr, cc, oo, aa, ss):
        oo[0][...] = _dot_nt(rr[0][...], cc[0][...]).astype(MM)

    return _rows(name, body, T, 512, [(a, 'cur')], [w], [(K, MM)])[0]


def _mm_tn(name, a, b, sqrelu=False, col_shards=0, diag=0):
    T, M = a.shape
    N = b.shape[1]
    tk = min(512, T)
    nk = T // tk
    if diag:
        bm, bn = M // diag, N // diag
        grid = (diag, 1, nk)
        a_spec = pl.BlockSpec((tk, bm), lambda g, n, k: (k, g))
        b_spec = pl.BlockSpec((tk, bn), lambda g, n, k: (k, g))
        o_spec = pl.BlockSpec((None, bm, bn), lambda g, n, k: (g, 0, 0))
        o_shape = (diag, bm, bn)
    else:
        bm = min(M, 1024)
        bn = N // col_shards if col_shards else min(N, 1024)
        grid = (M // bm, N // bn, nk)
        a_spec = pl.BlockSpec((tk, bm), lambda m, n, k: (k, m))
        b_spec = pl.BlockSpec((tk, bn), lambda m, n, k: (k, n))
        if col_shards:
            o_spec = pl.BlockSpec((None, bm, bn), lambda m, n, k: (n, m, 0))
            o_shape = (col_shards, M, bn)
        else:
            o_spec = pl.BlockSpec((bm, bn), lambda m, n, k: (m, n))
            o_shape = (M, N)

    def body(a_ref, b_ref, o_ref):
        @pl.when(pl.program_id(2) == 0)
        def _():
            o_ref[...] = jnp.zeros(o_ref.shape, F32)

        av = a_ref[...]
        if sqrelu:
            r = jnp.maximum(av.astype(F32), 0.0)
            av = r * r
        o_ref[...] += _dot_tn(av, b_ref[...])

    return pl.pallas_call(body, grid=grid, in_specs=[a_spec, b_spec], out_specs=o_spec,
                          out_shape=jax.ShapeDtypeStruct(o_shape, F32), name=name,
                          compiler_params=_cparams(("arbitrary", "arbitrary", "arbitrary")))(a, b)


def _ffn_fwd(name, li, h, w1g, w2g, x, g1, gt):
    T, D = h.shape
    nf, _, _, tf = w1g.shape
    F = nf * tf
    tm = min(512, T)

    def body(h_ref, w1_ref, w2_ref, x_ref, g1_ref, gt_ref, a_ref, y_ref, xo_ref, acc):
        f = pl.program_id(1)

        @pl.when(f == 0)
        def _():
            acc[...] = jnp.zeros(acc.shape, F32)

        a = _dot(h_ref[...], w1_ref[...])
        a_ref[...] = a.astype(MM)
        r = jnp.maximum(a, 0.0)
        acc[...] += _dot(r * r, w2_ref[...])

        @pl.when(f == nf - 1)
        def _():
            y = acc[...]
            y_ref[...] = y
            xo_ref[...] = x_ref[...] + gt_ref[...] * _rms(y, g1_ref[...])

    row = lambda t, f: (t, 0)
    one = lambda t, f: (0, 0)
    return pl.pallas_call(
        body, grid=(T // tm, nf),
        in_specs=[pl.BlockSpec((tm, D), row), pl.BlockSpec((None, None, D, tf), lambda t, f: (f, li, 0, 0)),
                  pl.BlockSpec((None, None, tf, D), lambda t, f: (f, li, 0, 0)),
                  pl.BlockSpec((tm, D), row), pl.BlockSpec((1, D), one), pl.BlockSpec((1, D), one)],
        out_specs=[pl.BlockSpec((tm, tf), lambda t, f: (t, f)), pl.BlockSpec((tm, D), row), pl.BlockSpec((tm, D), row)],
        out_shape=[jax.ShapeDtypeStruct((T, F), MM), jax.ShapeDtypeStruct((T, D), F32), jax.ShapeDtypeStruct((T, D), F32)],
        scratch_shapes=[pltpu.VMEM((tm, D), F32)], name=name,
        compiler_params=_cparams(("arbitrary", "arbitrary")))(h, w1g, w2g, x, g1, gt)


def _ffn_bwd(name, li, dy, a, w1g, w2g, x, dxo, g0, sc):
    T, D = x.shape
    nf, _, _, tf = w1g.shape
    F = nf * tf
    tm = min(512, T)

    def body(dy_ref, a_ref, w1_ref, w2_ref, x_ref, dxo_ref, g0_ref, sc_ref, da_ref, dx_ref, dsh_ref, dsc_ref, dg0_ref, acc):
        t, f = pl.program_id(0), pl.program_id(1)

        @pl.when((t == 0) & (f == 0))
        def _():
            for r in (dsh_ref, dsc_ref, dg0_ref):
                r[...] = jnp.zeros(r.shape, F32)

        @pl.when(f == 0)
        def _():
            acc[...] = jnp.zeros(acc.shape, F32)

        du = _dot_nt(dy_ref[...], w2_ref[...])
        da = (du * (2.0 * jnp.maximum(a_ref[...].astype(F32), 0.0))).astype(MM)
        da_ref[...] = da
        acc[...] += _dot_nt(da, w1_ref[...])

        @pl.when(f == nf - 1)
        def _():
            dx, dsh, dsc, dg0 = _prenorm_bwd(x_ref[...], g0_ref[...], sc_ref[...], acc[...])
            dx_ref[...] = dxo_ref[...] + dx
            dsh_ref[...] += dsh
            dsc_ref[...] += dsc
            dg0_ref[...] += dg0

    row = lambda t, f: (t, 0)
    one = lambda t, f: (0, 0)
    blk = lambda t, f: (t, f)
    return pl.pallas_call(
        body, grid=(T // tm, nf),
        in_specs=[pl.BlockSpec((tm, D), row), pl.BlockSpec((tm, tf), blk), pl.BlockSpec((None, None, D, tf), lambda t, f: (f, li, 0, 0)),
                  pl.BlockSpec((None, None, tf, D), lambda t, f: (f, li, 0, 0)), pl.BlockSpec((tm, D), row), pl.BlockSpec((tm, D), row),
                  pl.BlockSpec((1, D), one), pl.BlockSpec((1, D), one)],
        out_specs=[pl.BlockSpec((tm, tf), blk), pl.BlockSpec((tm, D), row)] + [pl.BlockSpec((1, D), one)] * 3,
        out_shape=[jax.ShapeDtypeStruct((T, F), MM), jax.ShapeDtypeStruct((T, D), F32)] + [jax.ShapeDtypeStruct((1, D), F32)] * 3,
        scratch_shapes=[pltpu.VMEM((tm, D), F32)], name=name,
        compiler_params=_cparams(("arbitrary", "arbitrary")))(dy, a, w1g, w2g, x, dxo, g0, sc)


def _rope_tables(pos, invf):
    T = pos.shape[0]

    def body(i, n, rr, cc, oo, aa, ss):
        ang = rr[0][...] * cc[0][...]
        lane = lax.broadcasted_iota(jnp.int32, ang.shape, 1)
        cs, sn = jnp.cos(ang), jnp.sin(ang)
        oo[0][...] = jnp.where((lane >= QK_NOPE) & (lane < QK_NOPE + QK_ROPE), cs, 1.0)
        oo[1][...] = jnp.where((lane >= QK_NOPE) & (lane < QK_NOPE + QK_ROPE // 2), -sn, 0.0)
        oo[2][...] = jnp.where((lane >= QK_NOPE + QK_ROPE // 2) & (lane < QK_NOPE + QK_ROPE), sn, 0.0)

    return _rows("rope_tables", body, T, 512, [(pos, 'cur')], [invf], [(HEAD_PAD, F32)] * 3)


def _rope(v, C, S1, S2):
    n = v.shape[1]
    reps = n // HEAD_PAD
    if reps > 1:
        C, S1, S2 = (jnp.tile(t, (1, reps)) for t in (C, S1, S2))
    return v * C + pltpu.roll(v, n - QK_ROPE // 2, 1) * S1 + pltpu.roll(v, QK_ROPE // 2, 1) * S2


def _unrope(d, C, S1, S2):
    n = d.shape[1]
    reps = n // HEAD_PAD
    if reps > 1:
        C, S1, S2 = (jnp.tile(t, (1, reps)) for t in (C, S1, S2))
    return d * C + pltpu.roll(d * S1, QK_ROPE // 2, 1) + pltpu.roll(d * S2, n - QK_ROPE // 2, 1)


def _mla_proj(name, h, C, S1, S2, w_dq, qg, w_uq, w_dkv, kvg, w_ukv_k, w_ukv_v):
    T = h.shape[0]
    HP = N_HEADS * HEAD_PAD

    def body(i, n, rr, cc, oo, aa, ss):
        hv = rr[0][...]
        Cv, S1v, S2v = rr[1][...], rr[2][...], rr[3][...]
        cq_raw = _dot(hv, cc[0][...])
        cq = _rms(cq_raw, cc[1][...]).astype(MM)
        q = _rope(_dot(cq, cc[2][...]), Cv, S1v, S2v)
        ckv_all = _dot(hv, cc[3][...])
        ckv_raw = ckv_all[:, :KV_LORA]
        ckv = _rms(ckv_raw, cc[4][...]).astype(MM)
        kr = _rope(ckv_all[:, KV_LORA:], Cv, S1v, S2v)
        k = _dot(ckv, cc[5][...]) + jnp.tile(kr, (1, N_HEADS))
        v = _dot(ckv, cc[6][...])
        oo[0][...] = cq_raw
        oo[1][...] = cq
        oo[2][...] = ckv_raw
        oo[3][...] = ckv
        oo[4][...] = q.astype(MM)
        oo[5][...] = k.astype(MM)
        oo[6][...] = v.astype(MM)

    return _rows(name, body, T, 256, [(h, 'cur'), (C, 'cur'), (S1, 'cur'), (S2, 'cur')],
                 [w_dq, qg, w_uq, w_dkv, kvg, w_ukv_k, w_ukv_v],
                 [(Q_LORA, F32), (Q_LORA, MM), (KV_LORA, F32), (KV_LORA, MM), (HP, MM), (HP, MM), (HP, MM)])


def _chunk_mask(tq, tk):
    qi = lax.broadcasted_iota(jnp.int32, (tq, tk), 0) // CHUNK
    ki = lax.broadcasted_iota(jnp.int32, (tq, tk), 1) // CHUNK
    return ki <= qi


ATT_FWD_HEADS = 4
ATT_BWD_HEADS = 2


def _attn_fwd(name, q, k, v):
    T = q.shape[0]
    tb = min(256, T)
    nb = T // tb
    nh = ATT_FWD_HEADS
    hs = [slice(h * HEAD_PAD, (h + 1) * HEAD_PAD) for h in range(nh)]

    def body(q_ref, k_ref, v_ref, o_ref, lse_ref):
        qb = pl.program_id(1)

        def k_block(k0, masked, st):
            new = []
            for h in range(nh):
                m, l, acc = st[h]
                s = _dot_nt(q_ref[:, hs[h]], k_ref[pl.ds(k0, tb), hs[h]])
                if masked:
                    s = jnp.where(_chunk_mask(tb, tb), s, NEG)
                m_new = jnp.maximum(m, jnp.max(s, axis=-1, keepdims=True))
                alpha = jnp.exp((m - m_new) * ATT_SCALE)
                p = jnp.exp((s - m_new) * ATT_SCALE)
                l = alpha * l + jnp.sum(p, axis=-1, keepdims=True)
                acc = alpha * acc + _dot(p, v_ref[pl.ds(k0, tb), hs[h]])
                new.append((m_new, l, acc))
            return tuple(new)

        st = tuple((jnp.full((tb, 1), NEG, F32), jnp.zeros((tb, 1), F32), jnp.zeros((tb, HEAD_PAD), F32)) for _ in range(nh))
        st = k_block(pl.multiple_of(qb * tb, tb), True, st)
        st = lax.fori_loop(0, qb, lambda kb, s_: k_block(pl.multiple_of(kb * tb, tb), False, s_), st)
        for h in range(nh):
            m, l, acc = st[h]
            o_ref[:, hs[h]] = (acc / l).astype(MM)
            lse_ref[:, hs[h]] = jnp.broadcast_to(m * ATT_SCALE + jnp.log(l), (tb, HEAD_PAD))

    blk = pl.BlockSpec((tb, nh * HEAD_PAD), lambda g, i: (i, g))
    res = pl.BlockSpec((T, nh * HEAD_PAD), lambda g, i: (0, g))
    return pl.pallas_call(
        body, grid=(N_HEADS // nh, nb), in_specs=[blk, res, res], out_specs=[blk, blk],
        out_shape=[jax.ShapeDtypeStruct(q.shape, MM), jax.ShapeDtypeStruct(q.shape, F32)], name=name,
        compiler_params=_cparams(("arbitrary", "arbitrary")))(q, k, v)


def _attn_bwd(name, q, k, v, o, do, lse):
    T = q.shape[0]
    tb = min(256, T)
    nb = T // tb
    nh = ATT_BWD_HEADS
    hs = [slice(h * HEAD_PAD, (h + 1) * HEAD_PAD) for h in range(nh)]

    def body(q_ref, k_ref, v_ref, o_ref, do_ref, lse_ref, dq_ref, dk_ref, dv_ref, dq_acc, dk_acc, dv_acc):
        kb = pl.program_id(1)

        @pl.when(kb == 0)
        def _():
            dq_acc[...] = jnp.zeros(dq_acc.shape, F32)

        dk_acc[...] = jnp.zeros(dk_acc.shape, F32)
        dv_acc[...] = jnp.zeros(dv_acc.shape, F32)

        def q_block(q0, masked):
            for h in range(nh):
                qh = q_ref[pl.ds(q0, tb), hs[h]]
                doh = do_ref[pl.ds(q0, tb), hs[h]]
                kh = k_ref[:, hs[h]]
                delta = jnp.sum(doh.astype(F32) * o_ref[pl.ds(q0, tb), hs[h]].astype(F32), axis=-1, keepdims=True)
                s = _dot_nt(qh, kh) * ATT_SCALE
                if masked:
                    s = jnp.where(_chunk_mask(tb, tb), s, NEG)
                p = jnp.exp(s - jnp.tile(lse_ref[pl.ds(q0, tb), hs[h]], (1, tb // HEAD_PAD)))
                ds = (p * (_dot_nt(doh, v_ref[:, hs[h]]) - delta) * ATT_SCALE).astype(MM)
                dv_acc[:, hs[h]] += _dot_tn(p, doh)
                dk_acc[:, hs[h]] += _dot_tn(ds, qh)
                dq_acc[pl.ds(q0, tb), hs[h]] += _dot(ds, kh)

        q_block(pl.multiple_of(kb * tb, tb), True)

        def rest(qb, c_):
            q_block(pl.multiple_of(qb * tb, tb), False)
            return c_

        lax.fori_loop(kb + 1, nb, rest, 0)
        dk_ref[...] = dk_acc[...].astype(MM)
        dv_ref[...] = dv_acc[...].astype(MM)

        @pl.when(kb == nb - 1)
        def _():
            dq_ref[...] = dq_acc[...].astype(MM)

    W = nh * HEAD_PAD
    blk = pl.BlockSpec((tb, W), lambda g, i: (i, g))
    res = pl.BlockSpec((T, W), lambda g, i: (0, g))
    return pl.pallas_call(
        body, grid=(N_HEADS // nh, nb), in_specs=[res, blk, blk, res, res, res], out_specs=[res, blk, blk],
        out_shape=[jax.ShapeDtypeStruct(q.shape, MM)] * 3,
        scratch_shapes=[pltpu.VMEM((T, W), F32), pltpu.VMEM((tb, W), F32), pltpu.VMEM((tb, W), F32)],
        name=name, compiler_params=_cparams(("arbitrary", "arbitrary")))(q, k, v, o, do, lse)


def _mla_proj_bwd(name, dq, dk, dv, C, S1, S2, cq_raw, ckv_raw, x, dxo, w_uq, w_ukv_k, w_ukv_v, w_dq, w_dkv, qg, kvg, g0, sc):
    T, D = x.shape
    HP = N_HEADS * HEAD_PAD

    def body(i, n, rr, cc, oo, aa, ss):
        Cv, S1v, S2v = rr[3][...], rr[4][...], rr[5][...]
        dq_pre = _unrope(rr[0][...].astype(F32), Cv, S1v, S2v).astype(MM)
        oo[0][...] = dq_pre
        dcq = _dot_nt(dq_pre, cc[0][...])
        dcq_raw, dqg = _rms_bwd(rr[6][...], cc[5][...], dcq)
        aa[0][...] += dqg
        dcq_raw = dcq_raw.astype(MM)
        oo[1][...] = dcq_raw
        dkv = rr[1][...]
        dkr = dkv[:, :HEAD_PAD].astype(F32)
        for hh in range(1, N_HEADS):
            dkr = dkr + dkv[:, hh * HEAD_PAD:(hh + 1) * HEAD_PAD].astype(F32)
        lane = lax.broadcasted_iota(jnp.int32, dkr.shape, 1)
        dkr = jnp.where((lane >= QK_NOPE) & (lane < QK_NOPE + QK_ROPE), _unrope(dkr, Cv, S1v, S2v), 0.0)
        dckv = _dot_nt(dkv, cc[1][...]) + _dot_nt(rr[2][...], cc[2][...])
        dckv_raw, dkvg = _rms_bwd(rr[7][...], cc[6][...], dckv)
        aa[1][...] += dkvg
        dckv_all = jnp.concatenate([dckv_raw, dkr], axis=1).astype(MM)
        oo[2][...] = dckv_all
        dh = _dot_nt(dcq_raw, cc[3][...]) + _dot_nt(dckv_all, cc[4][...])
        dx, dsh, dsc, dg0 = _prenorm_bwd(rr[8][...], cc[7][...], cc[8][...], dh)
        oo[3][...] = rr[9][...] + dx
        aa[2][...] += dsh
        aa[3][...] += dsc
        aa[4][...] += dg0

    return _rows(name, body, T, 256,
                 [(dq, 'cur'), (dk, 'cur'), (dv, 'cur'), (C, 'cur'), (S1, 'cur'), (S2, 'cur'), (cq_raw, 'cur'), (ckv_raw, 'cur'),
                  (x, 'cur'), (dxo, 'cur')],
                 [w_uq, w_ukv_k, w_ukv_v, w_dq, w_dkv, qg, kvg, g0, sc],
                 [(HP, MM), (Q_LORA, MM), (KV_LORA + HEAD_PAD, MM), (D, F32)],
                 accs=[(1, Q_LORA), (1, KV_LORA), (1, D), (1, D), (1, D)])


HALO = 32


def _conv_glu(name, h, w_pw1, b_pw1):
    T, D = h.shape

    def body(i, n, rr, cc, oo, aa, ss):
        a = _dot(rr[0][...], cc[0][...]) + cc[1][...]
        oo[0][...] = a
        oo[1][...] = a[:, :D] * _sigmoid(a[:, D:])

    return _rows(name, body, T, 512, [(h, 'cur')], [w_pw1, b_pw1], [(2 * D, F32), (D, F32)])


def _layernorm_parts(uc):
    xc = uc - jnp.mean(uc, axis=-1, keepdims=True)
    r = lax.rsqrt(jnp.mean(xc * xc, axis=-1, keepdims=True) + EPS)
    return xc * r, r


def _conv_dw(name, u, w_dw, b_dw, ln_g, ln_b, w_pw2, b_pw2, x, g1, gt):
    T, D = u.shape
    tm = min(256, T)

    def body(i, n, rr, cc, oo, aa, ss):
        ext = ss[0]
        ext[0:HALO, :] = jnp.where(i > 0, rr[1][tm - HALO:tm, :], 0.0)
        ext[HALO:HALO + tm, :] = rr[0][...]
        uc = jnp.zeros((tm, D), F32) + cc[1][...]
        for kk in range(CONV_W):
            uc = uc + ext[pl.ds(HALO - (CONV_W - 1) + kk, tm), :] * cc[0][kk:kk + 1, :]
        xh, _ = _layernorm_parts(uc)
        ln = xh * cc[2][...] + cc[3][...]
        z = (ln * _sigmoid(ln)).astype(MM)
        y = _dot(z, cc[4][...]) + cc[5][...]
        oo[0][...] = uc
        oo[1][...] = z
        oo[2][...] = y
        oo[3][...] = rr[2][...] + cc[7][...] * _rms(y, cc[6][...])

    return _rows(name, body, T, tm, [(u, 'cur'), (u, 'prev'), (x, 'cur')], [w_dw, b_dw, ln_g, ln_b, w_pw2, b_pw2, g1, gt],
                 [(D, F32), (D, MM), (D, F32), (D, F32)], scratch=[pltpu.VMEM((tm + HALO, D), F32)])


def _conv_bwd1(name, dy, uc, w_pw2, ln_g, ln_b):
    T, D = uc.shape

    def body(i, n, rr, cc, oo, aa, ss):
        dz = _dot_nt(rr[0][...], cc[0][...])
        xh, r = _layernorm_parts(rr[1][...])
        g = cc[1][...]
        ln = xh * g + cc[2][...]
        sg = _sigmoid(ln)
        dln = dz * (sg * (1.0 + ln * (1.0 - sg)))
        aa[0][...] += jnp.sum(dln * xh, axis=0, keepdims=True)
        aa[1][...] += jnp.sum(dln, axis=0, keepdims=True)
        dxh = dln * g
        duc = r * (dxh - jnp.mean(dxh, axis=-1, keepdims=True) - xh * jnp.mean(dxh * xh, axis=-1, keepdims=True))
        aa[2][...] += jnp.sum(duc, axis=0, keepdims=True)
        oo[0][...] = duc

    return _rows(name, body, T, 256, [(dy, 'cur'), (uc, 'cur')], [w_pw2, ln_g, ln_b], [(D, F32)], accs=[(1, D)] * 3)


def _conv_bwd2(name, duc, u, a, x, dxo, w_dw, w_pw1, g0, sc):
    T, D = u.shape
    tm = min(256, T)

    def body(i, n, rr, cc, oo, aa, ss):
        extd, extu = ss[0], ss[1]
        dcur = rr[0][...]
        extd[0:tm, :] = dcur
        extd[tm:tm + HALO, :] = jnp.where(i < n - 1, rr[1][0:HALO, :], 0.0)
        extu[0:HALO, :] = jnp.where(i > 0, rr[3][tm - HALO:tm, :], 0.0)
        extu[HALO:HALO + tm, :] = rr[2][...]
        du = jnp.zeros((tm, D), F32)
        for kk in range(CONV_W):
            du = du + extd[pl.ds(CONV_W - 1 - kk, tm), :] * cc[0][kk:kk + 1, :]
            aa[0][kk:kk + 1, :] += jnp.sum(dcur * extu[pl.ds(HALO - (CONV_W - 1) + kk, tm), :], axis=0, keepdims=True)
        av = rr[4][...]
        a1, sg = av[:, :D], _sigmoid(av[:, D:])
        da = jnp.concatenate([du * sg, du * a1 * (sg * (1.0 - sg))], axis=1)
        aa[1][...] += jnp.sum(da, axis=0, keepdims=True)
        da = da.astype(MM)
        oo[0][...] = da
        dx, dsh, dsc, dg0 = _prenorm_bwd(rr[5][...], cc[2][...], cc[3][...], _dot_nt(da, cc[1][...]))
        oo[1][...] = rr[6][...] + dx
        aa[2][...] += dsh
        aa[3][...] += dsc
        aa[4][...] += dg0

    return _rows(name, body, T, tm,
                 [(duc, 'cur'), (duc, 'next'), (u, 'cur'), (u, 'prev'), (a, 'cur'), (x, 'cur'), (dxo, 'cur')],
                 [w_dw, w_pw1, g0, sc], [(2 * D, MM), (D, F32)],
                 accs=[(32, D), (1, 2 * D), (1, D), (1, D), (1, D)],
                 scratch=[pltpu.VMEM((tm + HALO, D), F32), pltpu.VMEM((tm + HALO, D), F32)])


PHALO = 16


def _pool_fwd(name, h, w, b, scale, x, g1, gt):
    T, D = h.shape
    G = len(POOL_WINDOWS)
    Cg = D // G
    tm = min(256, T)

    def body(i, n, rr, cc, oo, aa, ss):
        ext = ss[0]
        ext[0:PHALO, :] = jnp.where(i > 0, rr[1][tm - PHALO:tm, :], 0.0)
        ext[PHALO:PHALO + tm, :] = rr[0][...]
        t_glob = i * tm + lax.broadcasted_iota(jnp.int32, (tm, 1), 0)
        ps, ys = [], []
        for g, win in enumerate(POOL_WINDOWS):
            cols = slice(g * Cg, (g + 1) * Cg)
            s = ext[pl.ds(PHALO, tm), cols]
            for j in range(1, win):
                s = s + ext[pl.ds(PHALO - j, tm), cols]
            cnt = jnp.minimum(t_glob + 1, win).astype(F32)
            p = (s / cnt - ext[pl.ds(PHALO, tm), cols]).astype(MM)
            ps.append(p)
            ys.append(_dot(p, cc[0][g]) + cc[1][:, cols])
        ypre = jnp.concatenate(ys, axis=1)
        y = ypre * cc[2][...]
        oo[0][...] = jnp.concatenate(ps, axis=1)
        oo[1][...] = ypre
        oo[2][...] = y
        oo[3][...] = rr[2][...] + cc[4][...] * _rms(y, cc[3][...])

    return _rows(name, body, T, tm, [(h, 'cur'), (h, 'prev'), (x, 'cur')], [w, b, scale, g1, gt],
                 [(D, MM), (D, F32), (D, F32), (D, F32)], scratch=[pltpu.VMEM((tm + PHALO, D), F32)])


def _pool_bwd1(name, dy, ypre, scale, w):
    T, D = ypre.shape
    G = len(POOL_WINDOWS)
    Cg = D // G

    def body(i, n, rr, cc, oo, aa, ss):
        dyv = rr[0][...].astype(F32)
        aa[0][...] += jnp.sum(dyv * rr[1][...], axis=0, keepdims=True)
        dypre = dyv * cc[0][...]
        aa[1][...] += jnp.sum(dypre, axis=0, keepdims=True)
        dypre = dypre.astype(MM)
        oo[1][...] = dypre
        oo[0][...] = jnp.concatenate([_dot_nt(dypre[:, g * Cg:(g + 1) * Cg], cc[1][g]) for g in range(G)], axis=1)

    return _rows(name, body, T, 256, [(dy, 'cur'), (ypre, 'cur')], [scale, w], [(D, F32), (D, MM)], accs=[(1, D)] * 2)


def _pool_bwd2(name, dp, x, dxo, g0, sc):
    T, D = x.shape
    G = len(POOL_WINDOWS)
    Cg = D // G
    tm = min(256, T)

    def body(i, n, rr, cc, oo, aa, ss):
        ext = ss[0]
        t_glob = i * tm + lax.broadcasted_iota(jnp.int32, (tm, 1), 0)
        dcur = rr[0][...]
        dhs = []
        for g, win in enumerate(POOL_WINDOWS):
            cols = slice(g * Cg, (g + 1) * Cg)
            cnt = jnp.minimum(t_glob + 1, win).astype(F32)
            ext[0:tm, cols] = dcur[:, cols] / cnt
            ext[tm:tm + PHALO, cols] = jnp.where(i < n - 1, rr[1][0:PHALO, cols] * (1.0 / win), 0.0)
        for g, win in enumerate(POOL_WINDOWS):
            cols = slice(g * Cg, (g + 1) * Cg)
            s = ext[pl.ds(0, tm), cols]
            for j in range(1, win):
                s = s + ext[pl.ds(j, tm), cols]
            dhs.append(s - dcur[:, cols])
        dx, dsh, dsc, dg0 = _prenorm_bwd(rr[2][...], cc[0][...], cc[1][...], jnp.concatenate(dhs, axis=1))
        oo[0][...] = rr[3][...] + dx
        aa[0][...] += dsh
        aa[1][...] += dsc
        aa[2][...] += dg0

    return _rows(name, body, T, tm, [(dp, 'cur'), (dp, 'next'), (x, 'cur'), (dxo, 'cur')], [g0, sc], [(D, F32)],
                 accs=[(1, D)] * 3, scratch=[pltpu.VMEM((tm + PHALO, D), F32)])


def _loss_head(x, tgt):
    T, D = x.shape

    def body(i, n, rr, cc, oo, aa, ss):
        err = rr[0][...] - rr[1][...]
        oo[0][...] = err * (1.0 / D)
        aa[0][...] += jnp.sum(err * err, axis=0, keepdims=True)

        @pl.when(i == n - 1)
        def _():
            aa[1][...] = jnp.broadcast_to(jnp.sum(aa[0][...], axis=1, keepdims=True) * (0.5 / D), (1, 128))

    dx, _, loss_row = _rows("loss_head", body, T, 512, [(x, 'cur'), (tgt, 'cur')], [], [(D, F32)], accs=[(1, D), (1, 128)])
    return dx, loss_row


def _adamw(name, w, g, m, v):
    shape = w.shape
    C = shape[-1]
    R = w.size // C
    w2, g2, m2, v2 = (t.reshape(R, C) for t in (w, g, m, v))
    br = R
    if R * C * 4 > (1 << 20):
        br = 8
        while br * 2 * C * 4 <= (1 << 20) and R % (br * 2) == 0:
            br *= 2
    b1c = 1.0 - ADAM_B1 ** ADAM_STEP
    b2c = 1.0 - ADAM_B2 ** ADAM_STEP

    def body(w_ref, g_ref, m_ref, v_ref, d_ref, mo_ref, vo_ref):
        gv = g_ref[...]
        mn = ADAM_B1 * m_ref[...] + (1.0 - ADAM_B1) * gv
        vn = ADAM_B2 * v_ref[...] + (1.0 - ADAM_B2) * (gv * gv)
        d_ref[...] = -ADAM_LR * ((mn / b1c) / (jnp.sqrt(vn / b2c) + ADAM_EPS) + ADAM_WD * w_ref[...])
        mo_ref[...] = mn
        vo_ref[...] = vn

    spec = pl.BlockSpec((br, C), lambda r: (r, 0))
    outs = pl.pallas_call(body, grid=(R // br,), in_specs=[spec] * 4, out_specs=[spec] * 3,
                          out_shape=[jax.ShapeDtypeStruct((R, C), F32)] * 3, name=name,
                          compiler_params=_cparams(("arbitrary",)))(w2, g2, m2, v2)
    return tuple(t.reshape(shape) for t in outs)


def _to_shards(full, ax):
    s = full.shape
    return jnp.moveaxis(full.reshape(s[:ax] + (N_DEV, s[ax] // N_DEV) + s[ax + 1:]), ax, 0)


def _unshard(g, ax):
    r = jnp.moveaxis(g, 0, ax)
    s = r.shape
    return r.reshape(s[:ax] + (s[ax] * s[ax + 1],) + s[ax + 2:])


def _pack(parts, dtype, row_mult):
    lead = parts[0].shape[:-1]
    flat = jnp.concatenate([p.astype(dtype) for p in parts], axis=-1)
    n = flat.shape[-1]
    per = row_mult * 1024
    tot = -(-n // per) * per
    flat = jnp.pad(flat, [(0, 0)] * len(lead) + [(0, tot - n)])
    return flat.reshape(lead + (tot // 1024, 1024))


def _pad_heads(w, lo, hi):
    K = w.shape[0]
    r = w.reshape(K, N_HEADS, -1)[:, :, lo:hi]
    return jnp.pad(r, ((0, 0), (0, 0), (0, HEAD_PAD - (hi - lo)))).reshape(K, N_HEADS * HEAD_PAD)


def kernel(x, c, positions, ada_w, ada_b, norm_g, mla_w_dq, mla_q_norm_g, mla_w_uq, mla_w_dkv, mla_kv_norm_g, mla_w_ukv, mla_w_o, conv_w_pw1, conv_b_pw1, conv_w_dw, conv_b_dw, conv_ln_g, conv_ln_b, conv_w_pw2, conv_b_pw2, pool_w, pool_b, pool_scale, ffn_w1, ffn_w2, loss_target, m_ada_w, m_ada_b, m_norm_g, m_mla_w_dq, m_mla_q_norm_g, m_mla_w_uq, m_mla_w_dkv, m_mla_kv_norm_g, m_mla_w_ukv, m_mla_w_o, m_conv_w_pw1, m_conv_b_pw1, m_conv_w_dw, m_conv_b_dw, m_conv_ln_g, m_conv_ln_b, m_conv_w_pw2, m_conv_b_pw2, m_pool_w, m_pool_b, m_pool_scale, m_ffn_w1, m_ffn_w2, v_ada_w, v_ada_b, v_norm_g, v_mla_w_dq, v_mla_q_norm_g, v_mla_w_uq, v_mla_w_dkv, v_mla_kv_norm_g, v_mla_w_ukv, v_mla_w_o, v_conv_w_pw1, v_conv_b_pw1, v_conv_w_dw, v_conv_b_dw, v_conv_ln_g, v_conv_ln_b, v_conv_w_pw2, v_conv_b_pw2, v_pool_w, v_pool_b, v_pool_scale, v_ffn_w1, v_ffn_w2):
    args = dict(locals())
    W = {n: args[n] for n, _ in WEIGHTS}
    M1 = {n: args['m_' + n] for n, _ in WEIGHTS}
    V2 = {n: args['v_' + n] for n, _ in WEIGHTS}
    D = D_MODEL
    T = x.shape[1]
    L = ffn_w1.shape[0]
    xi, yi, ci = _place()
    me = 4 * xi + 2 * yi + ci
    n_ada = ada_w.shape[2]

    small_sizes = [W[n].size for n in SMALL]
    small_in = _pack([c.reshape(-1)] + [W[n].reshape(-1) for n in SMALL], F32, 8)
    small_all = _ag_small("ag_small_params", small_in).reshape(N_DEV, -1)
    c_all = small_all[:, :D]
    Ws = {}
    off = D
    for n, sz in zip(SMALL, small_sizes):
        Ws[n] = _unshard(small_all[:, off:off + sz].reshape((N_DEV,) + W[n].shape), SHARD_AXIS[n])
        off += sz
    c16 = jnp.pad(c_all, ((0, 16 - N_DEV), (0, 0)))

    ada_b_cols = lax.dynamic_slice_in_dim(ada_b, me * n_ada, n_ada, axis=1).reshape(L, 1, n_ada)
    mod_part = _mod_part(c16, ada_w, ada_b_cols)[:, :N_DEV]
    mod_all = _ag_small("ag_mod", mod_part.reshape(L * N_DEV, n_ada)).reshape(N_DEV, L, N_DEV, n_ada)
    mod_mine = lax.dynamic_index_in_dim(mod_all, me, axis=2, keepdims=False)
    mod = jnp.transpose(mod_mine, (1, 0, 2)).reshape(L, 6, 1, D)

    rest = [n for n in BIG if not n.startswith('ffn')]
    w1g, w2g, rest_all = _ag_big("ag_weights", [ffn_w1.astype(MM), ffn_w2.astype(MM), _pack([W[n].reshape(-1) for n in rest], MM, 32)])
    rest_all = rest_all.reshape(N_DEV, -1)
    Wb = {}
    off = 0
    for n in rest:
        Wb[n] = rest_all[:, off:off + W[n].size].reshape((N_DEV,) + W[n].shape)
        off += W[n].size
    full = lambda n: _unshard(Wb[n], SHARD_AXIS[n])
    w_dq, w_uq, w_dkv, w_ukv, w_o = full('mla_w_dq'), full('mla_w_uq'), full('mla_w_dkv'), full('mla_w_ukv'), full('mla_w_o')
    w_pw1, w_pw2, w_pool = full('conv_w_pw1'), full('conv_w_pw2'), full('pool_w')
    n_mla = w_dq.shape[0]
    w_uq_p = [_pad_heads(w_uq[j], 0, QK_NOPE + QK_ROPE) for j in range(n_mla)]
    w_ukv_k = [_pad_heads(w_ukv[j], 0, QK_NOPE) for j in range(n_mla)]
    w_ukv_v = [_pad_heads(w_ukv[j], QK_NOPE, QK_NOPE + V_HEAD) for j in range(n_mla)]
    w_dkv_p = [jnp.pad(jnp.concatenate([w_dkv[j][:, :KV_LORA], jnp.zeros((D, QK_NOPE), MM), w_dkv[j][:, KV_LORA:]], axis=1),
                       ((0, 0), (0, HEAD_PAD - QK_NOPE - QK_ROPE))) for j in range(n_mla)]
    w_o_p = [jnp.pad(w_o[j].reshape(N_HEADS, V_HEAD, D), ((0, 0), (0, HEAD_PAD - V_HEAD), (0, 0))).reshape(N_HEADS * HEAD_PAD, D)
             for j in range(n_mla)]
    w_dw32 = jnp.pad(Ws['conv_w_dw'], ((0, 0), (0, 32 - CONV_W), (0, 0)))
    row = lambda t: t.reshape(1, -1)

    half = QK_ROPE // 2
    inv_freq = ROPE_THETA ** (-jnp.arange(0, QK_ROPE, 2, dtype=F32) / QK_ROPE)
    invf = jnp.zeros((1, HEAD_PAD), F32).at[0, QK_NOPE:QK_NOPE + half].set(inv_freq).at[0, QK_NOPE + half:QK_NOPE + QK_ROPE].set(inv_freq)
    rC, rS1, rS2 = _rope_tables(positions.reshape(T, 1).astype(F32), invf)

    xs = x.reshape(T, D)
    saved = []
    for i in range(L):
        kind, j = i % 3, i // 3
        sh_m, sc_m, gt_m, sh_f, sc_f, gt_f = (mod[i, r] for r in range(6))
        g = [row(Ws['norm_g'][i, r]) for r in range(4)]
        st = dict(x0=xs)
        if kind == 0:
            h = _prenorm(f"prenorm_m{i}", xs, g[0], sc_m, sh_m, MM)
            cq_raw, cq, ckv_raw, ckv, q, k, v = _mla_proj(f"mla_proj{i}", h, rC, rS1, rS2, w_dq[j], row(Ws['mla_q_norm_g'][j]), w_uq_p[j],
                                                          w_dkv_p[j], row(Ws['mla_kv_norm_g'][j]), w_ukv_k[j], w_ukv_v[j])
            o, lse = _attn_fwd(f"attn_fwd{i}", q, k, v)
            y, xs = _mm_post(f"mla_out{i}", o, w_o_p[j], None, xs, g[1], gt_m)
            st.update(h=h, cq_raw=cq_raw, cq=cq, ckv_raw=ckv_raw, ckv=ckv, q=q, k=k, v=v, o=o, lse=lse, y=y)
        elif kind == 1:
            h = _prenorm(f"prenorm_m{i}", xs, g[0], sc_m, sh_m, MM)
            a, u = _conv_glu(f"conv_glu{i}", h, w_pw1[j], row(W['conv_b_pw1'][j]))
            uc, z, y, xs = _conv_dw(f"conv_dw{i}", u, w_dw32[j], row(W['conv_b_dw'][j]), row(W['conv_ln_g'][j]), row(W['conv_ln_b'][j]),
                                    w_pw2[j], row(W['conv_b_pw2'][j]), xs, g[1], gt_m)
            st.update(h=h, a=a, u=u, uc=uc, z=z, y=y)
        else:
            h = _prenorm(f"prenorm_m{i}", xs, g[0], sc_m, sh_m, F32)
            p, ypre, y, xs = _pool_fwd(f"pool_fwd{i}", h, w_pool[j], row(Ws['pool_b'][j]), row(Ws['pool_scale'][j]), xs, g[1], gt_m)
            st.update(p=p, ypre=ypre, y=y)
        st['x1'] = xs
        hf = _prenorm(f"prenorm_f{i}", xs, g[2], sc_f, sh_f, MM)
        af, yf, xs = _ffn_fwd(f"ffn_fwd{i}", i, hf, w1g, w2g, xs, g[3], gt_f)
        st.update(hf=hf, af=af, yf=yf)
        saved.append(st)

    dx, loss_row = _loss_head(xs, loss_target.reshape(T, D))

    G = {}
    dmod = [None] * L
    dnorm = [None] * L
    for i in reversed(range(L)):
        kind, j = i % 3, i // 3
        sh_m, sc_m, gt_m, sh_f, sc_f, gt_f = (mod[i, r] for r in range(6))
        g = [row(Ws['norm_g'][i, r]) for r in range(4)]
        st = saved[i]
        dy, dg3, dgt_f, _ = _post_bwd(f"post_bwd_f{i}", dx, st['yf'], g[3], gt_f)
        da, dx, dsh_f, dsc_f, dg2 = _ffn_bwd(f"ffn_bwd{i}", i, dy, st['af'], w1g, w2g, st['x1'], dx, g[2], sc_f)
        G.setdefault('ffn_w1', [None] * L)[i] = _mm_tn(f"ffn_dw1_{i}", st['hf'], da, col_shards=N_DEV)
        G.setdefault('ffn_w2', [None] * L)[i] = _mm_tn(f"ffn_dw2_{i}", st['af'], dy, sqrelu=True)
        dy, dg1, dgt_m, dysum = _post_bwd(f"post_bwd_m{i}", dx, st['y'], g[1], gt_m)
        if kind == 0:
            do = _mm_nt_rows(f"mla_do{i}", dy, w_o_p[j])
            dq, dk, dv = _attn_bwd(f"attn_bwd{i}", st['q'], st['k'], st['v'], st['o'], do, st['lse'])
            dq_pre, dcq_raw, dckv_all, dx, dqg, dkvg, dsh_m, dsc_m, dg0 = _mla_proj_bwd(
                f"mla_proj_bwd{i}", dq, dk, dv, rC, rS1, rS2, st['cq_raw'], st['ckv_raw'], st['x0'], dx, w_uq_p[j], w_ukv_k[j], w_ukv_v[j],
                w_dq[j], w_dkv_p[j], row(Ws['mla_q_norm_g'][j]), row(Ws['mla_kv_norm_g'][j]), g[0], sc_m)
            dwo = _mm_tn(f"mla_dwo{i}", st['o'], dy)
            dwuq = _mm_tn(f"mla_dwuq{i}", st['cq'], dq_pre)
            dwk = _mm_tn(f"mla_dwukvk{i}", st['ckv'], dk)
            dwv = _mm_tn(f"mla_dwukvv{i}", st['ckv'], dv)
            dwdq = _mm_tn(f"mla_dwdq{i}", st['h'], dcq_raw)
            dwdkv = _mm_tn(f"mla_dwdkv{i}", st['h'], dckv_all)
            G.setdefault('mla_w_o', [None] * n_mla)[j] = dwo.reshape(N_HEADS, HEAD_PAD, D)[:, :V_HEAD].reshape(N_HEADS * V_HEAD, D)
            G.setdefault('mla_w_uq', [None] * n_mla)[j] = dwuq.reshape(Q_LORA, N_HEADS, HEAD_PAD)[:, :, :QK_NOPE + QK_ROPE].reshape(Q_LORA, -1)
            G.setdefault('mla_w_ukv', [None] * n_mla)[j] = jnp.concatenate(
                [dwk.reshape(KV_LORA, N_HEADS, HEAD_PAD)[:, :, :QK_NOPE], dwv.reshape(KV_LORA, N_HEADS, HEAD_PAD)[:, :, :V_HEAD]], axis=2).reshape(KV_LORA, -1)
            G.setdefault('mla_w_dq', [None] * n_mla)[j] = dwdq
            G.setdefault('mla_w_dkv', [None] * n_mla)[j] = jnp.concatenate([dwdkv[:, :KV_LORA], dwdkv[:, KV_LORA + QK_NOPE:KV_LORA + QK_NOPE + QK_ROPE]], axis=1)
            G.setdefault('mla_q_norm_g', [None] * n_mla)[j] = dqg[0]
            G.setdefault('mla_kv_norm_g', [None] * n_mla)[j] = dkvg[0]
        elif kind == 1:
            duc, dlng, dlnb, dbdw = _conv_bwd1(f"conv_bwd1_{i}", dy, st['uc'], w_pw2[j], row(W['conv_ln_g'][j]), row(W['conv_ln_b'][j]))
            da, dx, dwdw, dbpw1, dsh_m, dsc_m, dg0 = _conv_bwd2(f"conv_bwd2_{i}", duc, st['u'], st['a'], st['x0'], dx, w_dw32[j], w_pw1[j], g[0], sc_m)
            G['conv_w_pw2'] = [_mm_tn(f"conv_dwpw2_{i}", st['z'], dy)]
            G['conv_w_pw1'] = [_mm_tn(f"conv_dwpw1_{i}", st['h'], da)]
            G['conv_w_dw'] = [dwdw[:CONV_W]]
            G['conv_b_pw1'], G['conv_b_dw'], G['conv_ln_g'], G['conv_ln_b'], G['conv_b_pw2'] = [dbpw1[0]], [dbdw[0]], [dlng[0]], [dlnb[0]], [dysum[0]]
        else:
            dp, dypre, dscale, dpb = _pool_bwd1(f"pool_bwd1_{i}", dy, st['ypre'], row(Ws['pool_scale'][j]), w_pool[j])
            dx, dsh_m, dsc_m, dg0 = _pool_bwd2(f"pool_bwd2_{i}", dp, st['x0'], dx, g[0], sc_m)
            G['pool_w'] = [_mm_tn(f"pool_dw{i}", st['p'], dypre, diag=len(POOL_WINDOWS))]
            G['pool_b'] = [dpb.reshape(len(POOL_WINDOWS), -1)]
            G['pool_scale'] = [dscale[0]]
        dmod[i] = jnp.concatenate([dsh_m, dsc_m, dgt_m, dsh_f, dsc_f, dgt_f], axis=1)
        dnorm[i] = jnp.concatenate([dg0, dg1, dg2, dg3], axis=0)
    G['norm_g'] = dnorm
    grad_x = dx.reshape(x.shape)

    rs_names = [n for n, ax in WEIGHTS if ax is not None and n != 'ada_w' and not n.startswith('ffn')]
    parts = [_to_shards(jnp.stack(G[n], axis=0), SHARD_AXIS[n]).reshape(N_DEV, -1) for n in rs_names]
    tensors = [_pack(parts, F32, RS_ROWS)]
    for i in range(L):
        tensors += [G['ffn_w1'][i], G['ffn_w2'][i].reshape(N_DEV, -1, D)]
    red = _reduce_scatter("rs", tensors, ci, 2 * xi + yi)
    grads = {'ffn_w1': jnp.stack(red[1::2], axis=0), 'ffn_w2': jnp.stack(red[2::2], axis=0)}
    red0 = red[0].reshape(-1)
    off = 0
    for n in rs_names:
        grads[n] = red0[off:off + W[n].size].reshape(W[n].shape)
        off += W[n].size

    dmod_mine = jnp.concatenate(dmod, axis=1).reshape(-1)
    fin_in = _pack([dmod_mine] + [G[n][0].reshape(-1) for n in REPL] + [loss_row.reshape(-1)], F32, 8)
    fin_all = _ag_small("ag_final", fin_in)
    fin_sum = _sum_devices("final_sum", fin_all).reshape(-1)
    nm = L * 6 * D
    grads['ada_b'] = fin_sum[:nm].reshape(L, 6 * D)
    off = nm
    for n in REPL:
        grads[n] = fin_sum[off:off + W[n].size].reshape(W[n].shape)
        off += W[n].size
    loss = fin_sum[off]
    dmod_all = fin_all.reshape(N_DEV, -1)[:, :nm].reshape(N_DEV, L, 6 * D)
    dmod_cols = lax.dynamic_slice_in_dim(dmod_all, me * n_ada, n_ada, axis=2)
    dmod16 = jnp.pad(jnp.transpose(dmod_cols, (1, 0, 2)), ((0, 0), (0, 16 - N_DEV), (0, 0)))
    grads['ada_w'] = _ada_w_grad(c16, dmod16)

    deltas, new_m, new_v = {}, {}, {}
    for n, _ in WEIGHTS:
        deltas[n], new_m[n], new_v[n] = _adamw("adamw_" + n, W[n], grads[n], M1[n], V2[n])
    names = [n for n, _ in WEIGHTS]
    return (loss, grad_x, *[grads[n] for n in names], *[deltas[n] for n in names], *[new_m[n] for n in names],
            *[new_v[n] for n in names])
```

```python
import functools
import math

import jax
import jax.numpy as jnp
from jax import lax
from jax.experimental import pallas as pl
from jax.experimental.pallas import tpu as pltpu

F32 = jnp.float32
MM = jnp.bfloat16
EPS = 1e-6
NEG = -1e30
N_DEV = 8
VMEM_LIMIT = 48 * 1024 * 1024
MESH = pl.DeviceIdType.MESH

D_MODEL = 1024
N_HEADS = 16
HEAD_PAD = 128
QK_NOPE, QK_ROPE, V_HEAD = 64, 32, 64
Q_LORA, KV_LORA = 384, 256
CHUNK = 64
CONV_W = 31
POOL_WINDOWS = (2, 4, 8, 16)
ROPE_THETA = 10000.0
ATT_SCALE = 1.0 / math.sqrt(QK_NOPE + QK_ROPE)

ADAM_LR, ADAM_B1, ADAM_B2, ADAM_EPS, ADAM_WD, ADAM_STEP = 0.001, 0.9, 0.999, 1e-08, 0.01, 10

WEIGHTS = [('ada_w', 2), ('ada_b', None), ('norm_g', 2), ('mla_w_dq', 1), ('mla_q_norm_g', 1), ('mla_w_uq', 2),
           ('mla_w_dkv', 1), ('mla_kv_norm_g', 1), ('mla_w_ukv', 2), ('mla_w_o', 1), ('conv_w_pw1', 2),
           ('conv_b_pw1', None), ('conv_w_dw', 2), ('conv_b_dw', None), ('conv_ln_g', None), ('conv_ln_b', None),
           ('conv_w_pw2', 1), ('conv_b_pw2', None), ('pool_w', 2), ('pool_b', 2), ('pool_scale', 1),
           ('ffn_w1', 2), ('ffn_w2', 1)]
SHARD_AXIS = dict(WEIGHTS)
BIG = ['mla_w_dq', 'mla_w_uq', 'mla_w_dkv', 'mla_w_ukv', 'mla_w_o', 'conv_w_pw1', 'conv_w_pw2', 'pool_w', 'ffn_w1', 'ffn_w2']
SMALL = ['norm_g', 'mla_q_norm_g', 'mla_kv_norm_g', 'conv_w_dw', 'pool_b', 'pool_scale']
REPL = ['conv_b_pw1', 'conv_b_dw', 'conv_ln_g', 'conv_ln_b', 'conv_b_pw2']


def _dot(a, b):
    return jnp.dot(a.astype(MM), b.astype(MM), preferred_element_type=F32)


def _dot_nt(a, b):
    return lax.dot_general(a.astype(MM), b.astype(MM), (((1,), (1,)), ((), ())), preferred_element_type=F32)


def _dot_tn(a, b):
    return lax.dot_general(a.astype(MM), b.astype(MM), (((0,), (0,)), ((), ())), preferred_element_type=F32)


def _sigmoid(x):
    return 1.0 / (1.0 + jnp.exp(-x))


def _rstd(x):
    return lax.rsqrt(jnp.mean(x * x, axis=-1, keepdims=True) + EPS)


def _rms(x, g):
    return x * _rstd(x) * g


def _rms_bwd(x, g, dout):
    r = _rstd(x)
    xn = x * r
    dg = jnp.sum(dout * xn, axis=0, keepdims=True)
    dxn = dout * g
    dx = r * (dxn - xn * jnp.mean(dxn * xn, axis=-1, keepdims=True))
    return dx, dg


def _prenorm_bwd(x, g0, sc, dh):
    r = _rstd(x)
    xn = x * r
    dsh = jnp.sum(dh, axis=0, keepdims=True)
    dsc = jnp.sum(dh * (xn * g0), axis=0, keepdims=True)
    dn = dh * (1.0 + sc)
    dg0 = jnp.sum(dn * xn, axis=0, keepdims=True)
    dxn = dn * g0
    dx = r * (dxn - xn * jnp.mean(dxn * xn, axis=-1, keepdims=True))
    return dx, dsh, dsc, dg0


def _cparams(sem):
    return pltpu.CompilerParams(dimension_semantics=sem, vmem_limit_bytes=VMEM_LIMIT)


def _rows(name, body, n_rows, tm, rows, consts, outs, accs=(), scratch=()):
    tm = min(tm, n_rows)
    nblk = n_rows // tm
    nr, nc, no, na = len(rows), len(consts), len(outs), len(accs)
    in_specs, args = [], []
    for a, kind in rows:
        if kind == 'cur':
            im = lambda i: (i, 0)
        elif kind == 'prev':
            im = lambda i: (jnp.maximum(i - 1, 0), 0)
        else:
            im = lambda i: (jnp.minimum(i + 1, nblk - 1), 0)
        in_specs.append(pl.BlockSpec((tm, a.shape[1]), im))
        args.append(a)
    for a in consts:
        in_specs.append(pl.BlockSpec(a.shape, lambda i, nd=a.ndim: (0,) * nd))
        args.append(a)
    out_specs = [pl.BlockSpec((tm, c), lambda i: (i, 0)) for c, _ in outs]
    out_specs += [pl.BlockSpec(s, lambda i, nd=len(s): (0,) * nd) for s in accs]
    out_shape = [jax.ShapeDtypeStruct((n_rows, c), dt) for c, dt in outs]
    out_shape += [jax.ShapeDtypeStruct(s, F32) for s in accs]

    def kern(*refs):
        i = pl.program_id(0)
        rr = refs[:nr]
        cc = refs[nr:nr + nc]
        oo = refs[nr + nc:nr + nc + no]
        aa = refs[nr + nc + no:nr + nc + no + na]
        ss = refs[nr + nc + no + na:]

        @pl.when(i == 0)
        def _():
            for a in aa:
                a[...] = jnp.zeros(a.shape, F32)

        body(i, nblk, rr, cc, oo, aa, ss)

    return pl.pallas_call(kern, grid=(nblk,), in_specs=in_specs, out_specs=out_specs, out_shape=out_shape,
                          scratch_shapes=list(scratch), name=name, compiler_params=_cparams(("arbitrary",)))(*args)


def _place():
    return lax.axis_index("x"), lax.axis_index("y"), lax.axis_index("c")


def _ag_small(name, xs):
    R, C = xs.shape

    def body(x_ref, out_ref, send_sems, recv_sems):
        x, y, c = _place()
        me = 4 * x + 2 * y + c
        out_ref[me] = x_ref[...]
        copies = []
        for k in range(1, N_DEV):
            peer = ((1 - x) if k & 4 else x, (1 - y) if k & 2 else y, (1 - c) if k & 1 else c)
            cp = pltpu.make_async_remote_copy(src_ref=x_ref, dst_ref=out_ref.at[me], send_sem=send_sems.at[k - 1],
                                              recv_sem=recv_sems.at[k - 1], device_id=peer, device_id_type=MESH)
            cp.start()
            copies.append(cp)
        for cp in copies:
            cp.wait()

    return pl.pallas_call(
        body, out_shape=jax.ShapeDtypeStruct((N_DEV, R, C), xs.dtype),
        in_specs=[pl.BlockSpec(memory_space=pltpu.VMEM)], out_specs=pl.BlockSpec(memory_space=pltpu.VMEM),
        scratch_shapes=[pltpu.SemaphoreType.DMA((N_DEV - 1,)), pltpu.SemaphoreType.DMA((N_DEV - 1,))], name=name)(xs)


def _ag_big(name, xs):
    nt = len(xs)

    def body(*refs):
        x_refs, out_refs = refs[:nt], refs[nt:2 * nt]
        send_sems, recv_sems, local_sems = refs[2 * nt:]
        x, y, c = _place()
        me, sibling = (x, y, c), (x, y, 1 - c)
        chips = [(1 - x, y), (x, 1 - y), (1 - x, 1 - y)]

        def copy(t, k, block, to, own=False):
            px, py, pc = block
            rows = out_refs[t].at[4 * px + 2 * py + pc]
            return pltpu.make_async_remote_copy(src_ref=x_refs[t] if own else rows, dst_ref=rows, send_sem=send_sems.at[7 * t + k],
                                                recv_sem=recv_sems.at[7 * t + k], device_id=to, device_id_type=MESH)

        mine = [pltpu.make_async_copy(x_refs[t], out_refs[t].at[4 * x + 2 * y + c], local_sems.at[t]) for t in range(nt)]
        for cp in mine:
            cp.start()
        first = []
        for t in range(nt):
            first.append(copy(t, 0, me, sibling, own=True))
            first += [copy(t, 1 + j, me, (*chip, c), own=True) for j, chip in enumerate(chips)]
        for cp in first:
            cp.start()
        passed = []
        for t in range(nt):
            for j, chip in enumerate(chips):
                copy(t, 1 + j, (*chip, c), me).wait_recv()
                cp = copy(t, 4 + j, (*chip, c), sibling)
                cp.start()
                passed.append(cp)
        for t in range(nt):
            copy(t, 0, sibling, me).wait_recv()
            for j, chip in enumerate(chips):
                copy(t, 4 + j, (*chip, 1 - c), me).wait_recv()
        for cp in first + passed:
            cp.wait_send()
        for cp in mine:
            cp.wait()

    hbm = pl.BlockSpec(memory_space=pl.ANY)
    return pl.pallas_call(
        body, out_shape=[jax.ShapeDtypeStruct((N_DEV,) + t.shape, t.dtype) for t in xs],
        in_specs=[hbm] * nt, out_specs=[hbm] * nt,
        scratch_shapes=[pltpu.SemaphoreType.DMA((7 * nt,)), pltpu.SemaphoreType.DMA((7 * nt,)), pltpu.SemaphoreType.DMA((nt,))],
        name=name)(*xs)


def _rs_pair(name, ps):
    nt = len(ps)

    def body(*refs):
        p_refs, recv_refs = refs[:nt], refs[nt:2 * nt]
        send_sems, recv_sems = refs[2 * nt:]
        x, y, c = _place()
        copies = []
        for t in range(nt):
            for j in range(4):
                cp = pltpu.make_async_remote_copy(src_ref=p_refs[t].at[j, 1 - c], dst_ref=recv_refs[t].at[j], send_sem=send_sems.at[4 * t + j],
                                                  recv_sem=recv_sems.at[4 * t + j], device_id=(x, y, 1 - c), device_id_type=MESH)
                cp.start()
                copies.append(cp)
        for cp in copies:
            cp.wait()

    hbm = pl.BlockSpec(memory_space=pl.ANY)
    return pl.pallas_call(
        body, out_shape=[jax.ShapeDtypeStruct((4,) + p.shape[2:], p.dtype) for p in ps], in_specs=[hbm] * nt, out_specs=[hbm] * nt,
        scratch_shapes=[pltpu.SemaphoreType.DMA((4 * nt,)), pltpu.SemaphoreType.DMA((4 * nt,))], name=name)(*ps)


def _rs_chips(name, ss):
    nt = len(ss)

    def body(*refs):
        s_refs, recv_refs = refs[:nt], refs[nt:2 * nt]
        send_sems, recv_sems, local_sems = refs[2 * nt:]
        x, y, c = _place()
        mine = 2 * x + y
        owns = [pltpu.make_async_copy(s_refs[t].at[mine], recv_refs[t].at[mine], local_sems.at[t]) for t in range(nt)]
        for cp in owns:
            cp.start()
        copies = []
        for t in range(nt):
            for k in range(1, 4):
                px = (1 - x) if k & 2 else x
                py = (1 - y) if k & 1 else y
                cp = pltpu.make_async_remote_copy(src_ref=s_refs[t].at[2 * px + py], dst_ref=recv_refs[t].at[mine],
                                                  send_sem=send_sems.at[3 * t + k - 1], recv_sem=recv_sems.at[3 * t + k - 1],
                                                  device_id=(px, py, c), device_id_type=MESH)
                cp.start()
                copies.append(cp)
        for cp in copies:
            cp.wait()
        for cp in owns:
            cp.wait()

    hbm = pl.BlockSpec(memory_space=pl.ANY)
    return pl.pallas_call(
        body, out_shape=[jax.ShapeDtypeStruct(s_.shape, s_.dtype) for s_ in ss], in_specs=[hbm] * nt, out_specs=[hbm] * nt,
        scratch_shapes=[pltpu.SemaphoreType.DMA((3 * nt,)), pltpu.SemaphoreType.DMA((3 * nt,)), pltpu.SemaphoreType.DMA((nt,))],
        name=name)(*ss)


RS_ROWS = 256


def _pair_sum(name, p, recv, my_c, my_chip):
    _, _, r, c = p.shape
    tr = RS_ROWS

    def body(sc_ref, p_ref, r_ref, o_ref, own_ref):
        s = p_ref[...] + r_ref[...]
        o_ref[...] = s.astype(MM)

        @pl.when(pl.program_id(1) == sc_ref[1])
        def _():
            own_ref[...] = s

    return pl.pallas_call(
        body, grid_spec=pltpu.PrefetchScalarGridSpec(
            num_scalar_prefetch=1, grid=(r // tr, 4),
            in_specs=[pl.BlockSpec((None, None, tr, c), lambda i, j, sc: (j, sc[0], i, 0)),
                      pl.BlockSpec((None, tr, c), lambda i, j, sc: (j, i, 0))],
            out_specs=[pl.BlockSpec((None, tr, c), lambda i, j, sc: (j, i, 0)), pl.BlockSpec((tr, c), lambda i, j, sc: (i, 0))]),
        out_shape=[jax.ShapeDtypeStruct((4, r, c), MM), jax.ShapeDtypeStruct((r, c), F32)], name=name,
        compiler_params=_cparams(("arbitrary", "arbitrary")))(jnp.stack([my_c, my_chip]), p, recv)


def _chip_sum(name, own, recv, my_chip):
    _, r, c = recv.shape
    tr = RS_ROWS

    def body(sc_ref, own_ref, r_ref, o_ref):
        acc = jnp.zeros((tr, c), F32)
        for j in range(4):
            acc = acc + jnp.where(sc_ref[0] == j, own_ref[...], r_ref[j].astype(F32))
        o_ref[...] = acc

    return pl.pallas_call(
        body, grid_spec=pltpu.PrefetchScalarGridSpec(
            num_scalar_prefetch=1, grid=(r // tr,),
            in_specs=[pl.BlockSpec((tr, c), lambda i, sc: (i, 0)), pl.BlockSpec((4, tr, c), lambda i, sc: (0, i, 0))],
            out_specs=pl.BlockSpec((tr, c), lambda i, sc: (i, 0))),
        out_shape=jax.ShapeDtypeStruct((r, c), F32), name=name,
        compiler_params=_cparams(("arbitrary",)))(my_chip.reshape(1), own, recv)


def _reduce_scatter(tag, tensors, my_c, my_chip):
    ps = [t.reshape((4, 2) + t.shape[1:]) for t in tensors]
    recv = _rs_pair(tag + "_pair", ps)
    sums = [_pair_sum(f"{tag}_pair_sum{t}", ps[t], recv[t], my_c, my_chip) for t in range(len(ps))]
    recv2 = _rs_chips(tag + "_chips", [s_[0] for s_ in sums])
    return [_chip_sum(f"{tag}_chip_sum{t}", sums[t][1], recv2[t], my_chip) for t in range(len(ps))]


def _mod_part(c16, ada_w, ada_b_cols):
    L, D, n = ada_w.shape

    def body(c_ref, w_ref, b_ref, o_ref):
        cv = c_ref[...]
        o_ref[...] = _dot(cv * _sigmoid(cv), w_ref[...]) + b_ref[...]

    return pl.pallas_call(
        body, grid=(L,), in_specs=[pl.BlockSpec((16, D), lambda i: (0, 0)), pl.BlockSpec((None, D, n), lambda i: (i, 0, 0)),
                                   pl.BlockSpec((None, 1, n), lambda i: (i, 0, 0))],
        out_specs=pl.BlockSpec((None, 16, n), lambda i: (i, 0, 0)), out_shape=jax.ShapeDtypeStruct((L, 16, n), F32),
        name="ada_mod", compiler_params=_cparams(("arbitrary",)))(c16, ada_w, ada_b_cols)


def _ada_w_grad(c16, dmod16):
    L, _, n = dmod16.shape
    D = c16.shape[1]

    def body(c_ref, d_ref, o_ref):
        cv = c_ref[...]
        o_ref[...] = _dot_tn(cv * _sigmoid(cv), d_ref[...])

    return pl.pallas_call(
        body, grid=(L,), in_specs=[pl.BlockSpec((16, D), lambda i: (0, 0)), pl.BlockSpec((None, 16, n), lambda i: (i, 0, 0))],
        out_specs=pl.BlockSpec((None, D, n), lambda i: (i, 0, 0)), out_shape=jax.ShapeDtypeStruct((L, D, n), F32),
        name="ada_w_grad", compiler_params=_cparams(("arbitrary",)))(c16, dmod16)


def _sum_devices(name, g):
    _, R, C = g.shape

    def body(g_ref, o_ref):
        acc = g_ref[0]
        for d in range(1, N_DEV):
            acc = acc + g_ref[d]
        o_ref[...] = acc

    return pl.pallas_call(body, out_shape=jax.ShapeDtypeStruct((R, C), F32), name=name)(g)


def _prenorm(name, x, g0, sc, sh, dtype):
    T, D = x.shape

    def body(i, n, rr, cc, oo, aa, ss):
        oo[0][...] = (_rms(rr[0][...], cc[0][...]) * (1.0 + cc[1][...]) + cc[2][...]).astype(dtype)

    return _rows(name, body, T, 512, [(x, 'cur')], [g0, sc, sh], [(D, dtype)])[0]


def _post_bwd(name, dxo, y, g1, gt):
    T, D = y.shape

    def body(i, n, rr, cc, oo, aa, ss):
        d = rr[0][...]
        yv = rr[1][...]
        g1v, gtv = cc[0][...], cc[1][...]
        aa[1][...] += jnp.sum(d * _rms(yv, g1v), axis=0, keepdims=True)
        dy, dg1 = _rms_bwd(yv, g1v, d * gtv)
        aa[0][...] += dg1
        aa[2][...] += jnp.sum(dy, axis=0, keepdims=True)
        oo[0][...] = dy.astype(MM)

    return _rows(name, body, T, 512, [(dxo, 'cur'), (y, 'cur')], [g1, gt], [(D, MM)], accs=[(1, D)] * 3)


def _mm_post(name, a, w, bias, x, g1, gt):
    T, D = x.shape
    consts = [w, g1, gt] + ([bias] if bias is not None else [])

    def body(i, n, rr, cc, oo, aa, ss):
        y = _dot(rr[0][...], cc[0][...])
        if bias is not None:
            y = y + cc[3][...]
        oo[0][...] = y
        oo[1][...] = rr[1][...] + cc[2][...] * _rms(y, cc[1][...])

    return _rows(name, body, T, 512, [(a, 'cur'), (x, 'cur')], consts, [(D, F32), (D, F32)])


def _mm_nt_rows(name, a, w):
    T = a.shape[0]
    K = w.shape[0]

    def body(i, n, rr, cc, oo, aa, ss):
        oo[0][...] = _dot_nt(rr[0][...], cc[0][...]).astype(MM)

    return _rows(name, body, T, 512, [(a, 'cur')], [w], [(K, MM)])[0]


def _mm_tn(name, a, b, sqrelu=False, col_shards=0, diag=0):
    T, M = a.shape
    N = b.shape[1]
    tk = min(512, T)
    nk = T // tk
    if diag:
        bm, bn = M // diag, N // diag
        grid = (diag, 1, nk)
        a_spec = pl.BlockSpec((tk, bm), lambda g, n, k: (k, g))
        b_spec = pl.BlockSpec((tk, bn), lambda g, n, k: (k, g))
        o_spec = pl.BlockSpec((None, bm, bn), lambda g, n, k: (g, 0, 0))
        o_shape = (diag, bm, bn)
    else:
        bm = min(M, 1024)
        bn = N // col_shards if col_shards else min(N, 1024)
        grid = (M // bm, N // bn, nk)
        a_spec = pl.BlockSpec((tk, bm), lambda m, n, k: (k, m))
        b_spec = pl.BlockSpec((tk, bn), lambda m, n, k: (k, n))
        if col_shards:
            o_spec = pl.BlockSpec((None, bm, bn), lambda m, n, k: (n, m, 0))
            o_shape = (col_shards, M, bn)
        else:
            o_spec = pl.BlockSpec((bm, bn), lambda m, n, k: (m, n))
            o_shape = (M, N)

    def body(a_ref, b_ref, o_ref):
        @pl.when(pl.program_id(2) == 0)
        def _():
            o_ref[...] = jnp.zeros(o_ref.shape, F32)

        av = a_ref[...]
        if sqrelu:
            r = jnp.maximum(av.astype(F32), 0.0)
            av = r * r
        o_ref[...] += _dot_tn(av, b_ref[...])

    return pl.pallas_call(body, grid=grid, in_specs=[a_spec, b_spec], out_specs=o_spec,
                          out_shape=jax.ShapeDtypeStruct(o_shape, F32), name=name,
                          compiler_params=_cparams(("arbitrary", "arbitrary", "arbitrary")))(a, b)


def _ffn_fwd(name, li, h, w1g, w2g, x, g1, gt):
    T, D = h.shape
    nf, _, _, tf = w1g.shape
    F = nf * tf
    tm = min(512, T)

    def body(h_ref, w1_ref, w2_ref, x_ref, g1_ref, gt_ref, a_ref, y_ref, xo_ref, acc):
        f = pl.program_id(1)

        @pl.when(f == 0)
        def _():
            acc[...] = jnp.zeros(acc.shape, F32)

        a = _dot(h_ref[...], w1_ref[...])
        a_ref[...] = a.astype(MM)
        r = jnp.maximum(a, 0.0)
        acc[...] += _dot(r * r, w2_ref[...])

        @pl.when(f == nf - 1)
        def _():
            y = acc[...]
            y_ref[...] = y
            xo_ref[...] = x_ref[...] + gt_ref[...] * _rms(y, g1_ref[...])

    row = lambda t, f: (t, 0)
    one = lambda t, f: (0, 0)
    return pl.pallas_call(
        body, grid=(T // tm, nf),
        in_specs=[pl.BlockSpec((tm, D), row), pl.BlockSpec((None, None, D, tf), lambda t, f: (f, li, 0, 0)),
                  pl.BlockSpec((None, None, tf, D), lambda t, f: (f, li, 0, 0)),
                  pl.BlockSpec((tm, D), row), pl.BlockSpec((1, D), one), pl.BlockSpec((1, D), one)],
        out_specs=[pl.BlockSpec((tm, tf), lambda t, f: (t, f)), pl.BlockSpec((tm, D), row), pl.BlockSpec((tm, D), row)],
        out_shape=[jax.ShapeDtypeStruct((T, F), MM), jax.ShapeDtypeStruct((T, D), F32), jax.ShapeDtypeStruct((T, D), F32)],
        scratch_shapes=[pltpu.VMEM((tm, D), F32)], name=name,
        compiler_params=_cparams(("arbitrary", "arbitrary")))(h, w1g, w2g, x, g1, gt)


def _ffn_bwd(name, li, dy, a, w1g, w2g, x, dxo, g0, sc):
    T, D = x.shape
    nf, _, _, tf = w1g.shape
    F = nf * tf
    tm = min(512, T)

    def body(dy_ref, a_ref, w1_ref, w2_ref, x_ref, dxo_ref, g0_ref, sc_ref, da_ref, dx_ref, dsh_ref, dsc_ref, dg0_ref, acc):
        t, f = pl.program_id(0), pl.program_id(1)

        @pl.when((t == 0) & (f == 0))
        def _():
            for r in (dsh_ref, dsc_ref, dg0_ref):
                r[...] = jnp.zeros(r.shape, F32)

        @pl.when(f == 0)
        def _():
            acc[...] = jnp.zeros(acc.shape, F32)

        du = _dot_nt(dy_ref[...], w2_ref[...])
        da = (du * (2.0 * jnp.maximum(a_ref[...].astype(F32), 0.0))).astype(MM)
        da_ref[...] = da
        acc[...] += _dot_nt(da, w1_ref[...])

        @pl.when(f == nf - 1)
        def _():
            dx, dsh, dsc, dg0 = _prenorm_bwd(x_ref[...], g0_ref[...], sc_ref[...], acc[...])
            dx_ref[...] = dxo_ref[...] + dx
            dsh_ref[...] += dsh
            dsc_ref[...] += dsc
            dg0_ref[...] += dg0

    row = lambda t, f: (t, 0)
    one = lambda t, f: (0, 0)
    blk = lambda t, f: (t, f)
    return pl.pallas_call(
        body, grid=(T // tm, nf),
        in_specs=[pl.BlockSpec((tm, D), row), pl.BlockSpec((tm, tf), blk), pl.BlockSpec((None, None, D, tf), lambda t, f: (f, li, 0, 0)),
                  pl.BlockSpec((None, None, tf, D), lambda t, f: (f, li, 0, 0)), pl.BlockSpec((tm, D), row), pl.BlockSpec((tm, D), row),
                  pl.BlockSpec((1, D), one), pl.BlockSpec((1, D), one)],
        out_specs=[pl.BlockSpec((tm, tf), blk), pl.BlockSpec((tm, D), row)] + [pl.BlockSpec((1, D), one)] * 3,
        out_shape=[jax.ShapeDtypeStruct((T, F), MM), jax.ShapeDtypeStruct((T, D), F32)] + [jax.ShapeDtypeStruct((1, D), F32)] * 3,
        scratch_shapes=[pltpu.VMEM((tm, D), F32)], name=name,
        compiler_params=_cparams(("arbitrary", "arbitrary")))(dy, a, w1g, w2g, x, dxo, g0, sc)


def _rope_tables(pos, invf):
    T = pos.shape[0]

    def body(i, n, rr, cc, oo, aa, ss):
        ang = rr[0][...] * cc[0][...]
        lane = lax.broadcasted_iota(jnp.int32, ang.shape, 1)
        cs, sn = jnp.cos(ang), jnp.sin(ang)
        oo[0][...] = jnp.where((lane >= QK_NOPE) & (lane < QK_NOPE + QK_ROPE), cs, 1.0)
        oo[1][...] = jnp.where((lane >= QK_NOPE) & (lane < QK_NOPE + QK_ROPE // 2), -sn, 0.0)
        oo[2][...] = jnp.where((lane >= QK_NOPE + QK_ROPE // 2) & (lane < QK_NOPE + QK_ROPE), sn, 0.0)

    return _rows("rope_tables", body, T, 512, [(pos, 'cur')], [invf], [(HEAD_PAD, F32)] * 3)


def _rope(v, C, S1, S2):
    n = v.shape[1]
    reps = n // HEAD_PAD
    if reps > 1:
        C, S1, S2 = (jnp.tile(t, (1, reps)) for t in (C, S1, S2))
    return v * C + pltpu.roll(v, n - QK_ROPE // 2, 1) * S1 + pltpu.roll(v, QK_ROPE // 2, 1) * S2


def _unrope(d, C, S1, S2):
    n = d.shape[1]
    reps = n // HEAD_PAD
    if reps > 1:
        C, S1, S2 = (jnp.tile(t, (1, reps)) for t in (C, S1, S2))
    return d * C + pltpu.roll(d * S1, QK_ROPE // 2, 1) + pltpu.roll(d * S2, n - QK_ROPE // 2, 1)


def _mla_proj(name, h, C, S1, S2, w_dq, qg, w_uq, w_dkv, kvg, w_ukv_k, w_ukv_v):
    T = h.shape[0]
    HP = N_HEADS * HEAD_PAD

    def body(i, n, rr, cc, oo, aa, ss):
        hv = rr[0][...]
        Cv, S1v, S2v = rr[1][...], rr[2][...], rr[3][...]
        cq_raw = _dot(hv, cc[0][...])
        cq = _rms(cq_raw, cc[1][...]).astype(MM)
        q = _rope(_dot(cq, cc[2][...]), Cv, S1v, S2v)
        ckv_all = _dot(hv, cc[3][...])
        ckv_raw = ckv_all[:, :KV_LORA]
        ckv = _rms(ckv_raw, cc[4][...]).astype(MM)
        kr = _rope(ckv_all[:, KV_LORA:], Cv, S1v, S2v)
        k = _dot(ckv, cc[5][...]) + jnp.tile(kr, (1, N_HEADS))
        v = _dot(ckv, cc[6][...])
        v = jnp.where(lax.broadcasted_iota(jnp.int32, v.shape, 1) % HEAD_PAD == V_HEAD, 1.0, v)
        oo[0][...] = cq_raw
        oo[1][...] = cq
        oo[2][...] = ckv_raw
        oo[3][...] = ckv
        oo[4][...] = q.astype(MM)
        oo[5][...] = k.astype(MM)
        oo[6][...] = v.astype(MM)

    return _rows(name, body, T, 256, [(h, 'cur'), (C, 'cur'), (S1, 'cur'), (S2, 'cur')],
                 [w_dq, qg, w_uq, w_dkv, kvg, w_ukv_k, w_ukv_v],
                 [(Q_LORA, F32), (Q_LORA, MM), (KV_LORA, F32), (KV_LORA, MM), (HP, MM), (HP, MM), (HP, MM)])


ATT_HEADS = 4
ATT_BLOCK = 512


def _chunk_mask_t(tk, tq):
    ki = lax.broadcasted_iota(jnp.int32, (tk, tq), 0) // CHUNK
    qi = lax.broadcasted_iota(jnp.int32, (tk, tq), 1) // CHUNK
    return ki <= qi


def _attn_fwd(name, q, k, v):
    T = q.shape[0]
    tb = min(ATT_BLOCK, T)
    nb = T // tb
    nh = ATT_HEADS
    hs = [slice(h * HEAD_PAD, (h + 1) * HEAD_PAD) for h in range(nh)]

    def body(q_ref, k_ref, v_ref, o_ref, lse_ref):
        qb = pl.program_id(1)

        def k_block(k0, masked, st):
            new = []
            for h in range(nh):
                m, acc = st[h]
                s = _dot_nt(k_ref[pl.ds(k0, tb), hs[h]], q_ref[:, hs[h]])
                if masked:
                    s = jnp.where(_chunk_mask_t(tb, tb), s, NEG)
                m_new = jnp.maximum(m, jnp.max(s, axis=0, keepdims=True))
                alpha = jnp.exp((m - m_new) * ATT_SCALE)
                p = jnp.exp((s - m_new) * ATT_SCALE)
                acc = alpha * acc + _dot_tn(v_ref[pl.ds(k0, tb), hs[h]], p)
                new.append((m_new, acc))
            return tuple(new)

        st = tuple((jnp.full((1, tb), NEG, F32), jnp.zeros((HEAD_PAD, tb), F32)) for _ in range(nh))
        st = k_block(pl.multiple_of(qb * tb, tb), True, st)
        st = lax.fori_loop(0, qb, lambda kb, s_: k_block(pl.multiple_of(kb * tb, tb), False, s_), st)
        for h in range(nh):
            m, acc = st[h]
            l = acc[V_HEAD:V_HEAD + 1, :]
            o_ref[:, hs[h]] = (acc / l).T.astype(MM)
            lse_ref[h] = jnp.broadcast_to(m * ATT_SCALE + jnp.log(l), (8, tb))

    blk = pl.BlockSpec((tb, nh * HEAD_PAD), lambda g, i: (i, g))
    res = pl.BlockSpec((T, nh * HEAD_PAD), lambda g, i: (0, g))
    return pl.pallas_call(
        body, grid=(N_HEADS // nh, nb), in_specs=[blk, res, res],
        out_specs=[blk, pl.BlockSpec((nh, 8, tb), lambda g, i: (g, 0, i))],
        out_shape=[jax.ShapeDtypeStruct(q.shape, MM), jax.ShapeDtypeStruct((N_HEADS, 8, T), F32)], name=name,
        compiler_params=_cparams(("arbitrary", "arbitrary")))(q, k, v)


def _attn_delta(name, do, o):
    T = do.shape[0]
    tb = min(256, T)

    def body(do_ref, o_ref, d_ref):
        lane = lax.broadcasted_iota(jnp.int32, (tb, HEAD_PAD), 1) // 8
        cols = jnp.zeros((tb, HEAD_PAD), F32)
        for h in range(N_HEADS):
            hsl = slice(h * HEAD_PAD, (h + 1) * HEAD_PAD)
            r = jnp.sum(do_ref[:, hsl].astype(F32) * o_ref[:, hsl].astype(F32), axis=1, keepdims=True)
            cols = jnp.where(lane == h, r, cols)
        d_ref[...] = cols.T

    spec = pl.BlockSpec((tb, N_HEADS * HEAD_PAD), lambda i: (i, 0))
    out = pl.pallas_call(body, grid=(T // tb,), in_specs=[spec, spec], out_specs=pl.BlockSpec((HEAD_PAD, tb), lambda i: (0, i)),
                         out_shape=jax.ShapeDtypeStruct((HEAD_PAD, T), F32), name=name, compiler_params=_cparams(("arbitrary",)))(do, o)
    return out.reshape(N_HEADS, 8, T)


def _attn_bwd(name, q, k, v, do, lse, delta):
    T = q.shape[0]
    tb = min(ATT_BLOCK, T)
    nb = T // tb
    nh = ATT_HEADS
    hs = [slice(h * HEAD_PAD, (h + 1) * HEAD_PAD) for h in range(nh)]

    def body(q_ref, k_ref, v_ref, do_ref, lse_ref, dl_ref, dq_ref, dk_ref, dv_ref, dq_acc, dk_acc, dv_acc):
        kb = pl.program_id(1)

        @pl.when(kb == 0)
        def _():
            dq_acc[...] = jnp.zeros(dq_acc.shape, F32)

        dk_acc[...] = jnp.zeros(dk_acc.shape, F32)
        dv_acc[...] = jnp.zeros(dv_acc.shape, F32)

        def q_block(q0, masked):
            for h in range(nh):
                qh = q_ref[pl.ds(q0, tb), hs[h]]
                doh = do_ref[pl.ds(q0, tb), hs[h]]
                kh = k_ref[:, hs[h]]
                s = _dot_nt(kh, qh) * ATT_SCALE
                if masked:
                    s = jnp.where(_chunk_mask_t(tb, tb), s, NEG)
                p = jnp.exp(s - lse_ref[h, 0:1, pl.ds(q0, tb)])
                ds = (p * (_dot_nt(v_ref[:, hs[h]], doh) - dl_ref[h, 0:1, pl.ds(q0, tb)]) * ATT_SCALE).astype(MM)
                dv_acc[:, hs[h]] += _dot(p, doh)
                dk_acc[:, hs[h]] += _dot(ds, qh)
                dq_acc[pl.ds(q0, tb), hs[h]] += _dot_tn(ds, kh)

        q_block(pl.multiple_of(kb * tb, tb), True)

        def rest(qb, c_):
            q_block(pl.multiple_of(qb * tb, tb), False)
            return c_

        lax.fori_loop(kb + 1, nb, rest, 0)
        dk_ref[...] = dk_acc[...].astype(MM)
        dv_ref[...] = dv_acc[...].astype(MM)

        @pl.when(kb == nb - 1)
        def _():
            dq_ref[...] = dq_acc[...].astype(MM)

    W = nh * HEAD_PAD
    blk = pl.BlockSpec((tb, W), lambda g, i: (i, g))
    res = pl.BlockSpec((T, W), lambda g, i: (0, g))
    rows = pl.BlockSpec((nh, 8, T), lambda g, i: (g, 0, 0))
    return pl.pallas_call(
        body, grid=(N_HEADS // nh, nb), in_specs=[res, blk, blk, res, rows, rows], out_specs=[res, blk, blk],
        out_shape=[jax.ShapeDtypeStruct(q.shape, MM)] * 3,
        scratch_shapes=[pltpu.VMEM((T, W), F32), pltpu.VMEM((tb, W), F32), pltpu.VMEM((tb, W), F32)],
        name=name, compiler_params=_cparams(("arbitrary", "arbitrary")))(q, k, v, do, lse, delta)


def _mla_proj_bwd(name, dq, dk, dv, C, S1, S2, cq_raw, ckv_raw, x, dxo, w_uq, w_ukv_k, w_ukv_v, w_dq, w_dkv, qg, kvg, g0, sc):
    T, D = x.shape
    HP = N_HEADS * HEAD_PAD

    def body(i, n, rr, cc, oo, aa, ss):
        Cv, S1v, S2v = rr[3][...], rr[4][...], rr[5][...]
        dq_pre = _unrope(rr[0][...].astype(F32), Cv, S1v, S2v).astype(MM)
        oo[0][...] = dq_pre
        dcq = _dot_nt(dq_pre, cc[0][...])
        dcq_raw, dqg = _rms_bwd(rr[6][...], cc[5][...], dcq)
        aa[0][...] += dqg
        dcq_raw = dcq_raw.astype(MM)
        oo[1][...] = dcq_raw
        dkv = rr[1][...]
        dkr = dkv[:, :HEAD_PAD].astype(F32)
        for hh in range(1, N_HEADS):
            dkr = dkr + dkv[:, hh * HEAD_PAD:(hh + 1) * HEAD_PAD].astype(F32)
        lane = lax.broadcasted_iota(jnp.int32, dkr.shape, 1)
        dkr = jnp.where((lane >= QK_NOPE) & (lane < QK_NOPE + QK_ROPE), _unrope(dkr, Cv, S1v, S2v), 0.0)
        dckv = _dot_nt(dkv, cc[1][...]) + _dot_nt(rr[2][...], cc[2][...])
        dckv_raw, dkvg = _rms_bwd(rr[7][...], cc[6][...], dckv)
        aa[1][...] += dkvg
        dckv_all = jnp.concatenate([dckv_raw, dkr], axis=1).astype(MM)
        oo[2][...] = dckv_all
        dh = _dot_nt(dcq_raw, cc[3][...]) + _dot_nt(dckv_all, cc[4][...])
        dx, dsh, dsc, dg0 = _prenorm_bwd(rr[8][...], cc[7][...], cc[8][...], dh)
        oo[3][...] = rr[9][...] + dx
        aa[2][...] += dsh
        aa[3][...] += dsc
        aa[4][...] += dg0

    return _rows(name, body, T, 256,
                 [(dq, 'cur'), (dk, 'cur'), (dv, 'cur'), (C, 'cur'), (S1, 'cur'), (S2, 'cur'), (cq_raw, 'cur'), (ckv_raw, 'cur'),
                  (x, 'cur'), (dxo, 'cur')],
                 [w_uq, w_ukv_k, w_ukv_v, w_dq, w_dkv, qg, kvg, g0, sc],
                 [(HP, MM), (Q_LORA, MM), (KV_LORA + HEAD_PAD, MM), (D, F32)],
                 accs=[(1, Q_LORA), (1, KV_LORA), (1, D), (1, D), (1, D)])


HALO = 32


def _conv_glu(name, h, w_pw1, b_pw1):
    T, D = h.shape

    def body(i, n, rr, cc, oo, aa, ss):
        a = _dot(rr[0][...], cc[0][...]) + cc[1][...]
        oo[0][...] = a
        oo[1][...] = a[:, :D] * _sigmoid(a[:, D:])

    return _rows(name, body, T, 512, [(h, 'cur')], [w_pw1, b_pw1], [(2 * D, F32), (D, F32)])


def _layernorm_parts(uc):
    xc = uc - jnp.mean(uc, axis=-1, keepdims=True)
    r = lax.rsqrt(jnp.mean(xc * xc, axis=-1, keepdims=True) + EPS)
    return xc * r, r


def _conv_dw(name, u, w_dw, b_dw, ln_g, ln_b, w_pw2, b_pw2, x, g1, gt):
    T, D = u.shape
    tm = min(256, T)

    def body(i, n, rr, cc, oo, aa, ss):
        ext = ss[0]
        ext[0:HALO, :] = jnp.where(i > 0, rr[1][tm - HALO:tm, :], 0.0)
        ext[HALO:HALO + tm, :] = rr[0][...]
        uc = jnp.zeros((tm, D), F32) + cc[1][...]
        for kk in range(CONV_W):
            uc = uc + ext[pl.ds(HALO - (CONV_W - 1) + kk, tm), :] * cc[0][kk:kk + 1, :]
        xh, _ = _layernorm_parts(uc)
        ln = xh * cc[2][...] + cc[3][...]
        z = (ln * _sigmoid(ln)).astype(MM)
        y = _dot(z, cc[4][...]) + cc[5][...]
        oo[0][...] = uc
        oo[1][...] = z
        oo[2][...] = y
        oo[3][...] = rr[2][...] + cc[7][...] * _rms(y, cc[6][...])

    return _rows(name, body, T, tm, [(u, 'cur'), (u, 'prev'), (x, 'cur')], [w_dw, b_dw, ln_g, ln_b, w_pw2, b_pw2, g1, gt],
                 [(D, F32), (D, MM), (D, F32), (D, F32)], scratch=[pltpu.VMEM((tm + HALO, D), F32)])


def _conv_bwd1(name, dy, uc, w_pw2, ln_g, ln_b):
    T, D = uc.shape

    def body(i, n, rr, cc, oo, aa, ss):
        dz = _dot_nt(rr[0][...], cc[0][...])
        xh, r = _layernorm_parts(rr[1][...])
        g = cc[1][...]
        ln = xh * g + cc[2][...]
        sg = _sigmoid(ln)
        dln = dz * (sg * (1.0 + ln * (1.0 - sg)))
        aa[0][...] += jnp.sum(dln * xh, axis=0, keepdims=True)
        aa[1][...] += jnp.sum(dln, axis=0, keepdims=True)
        dxh = dln * g
        duc = r * (dxh - jnp.mean(dxh, axis=-1, keepdims=True) - xh * jnp.mean(dxh * xh, axis=-1, keepdims=True))
        aa[2][...] += jnp.sum(duc, axis=0, keepdims=True)
        oo[0][...] = duc

    return _rows(name, body, T, 256, [(dy, 'cur'), (uc, 'cur')], [w_pw2, ln_g, ln_b], [(D, F32)], accs=[(1, D)] * 3)


def _conv_bwd2(name, duc, u, a, x, dxo, w_dw, w_pw1, g0, sc):
    T, D = u.shape
    tm = min(256, T)

    def body(i, n, rr, cc, oo, aa, ss):
        extd, extu = ss[0], ss[1]
        dcur = rr[0][...]
        extd[0:tm, :] = dcur
        extd[tm:tm + HALO, :] = jnp.where(i < n - 1, rr[1][0:HALO, :], 0.0)
        extu[0:HALO, :] = jnp.where(i > 0, rr[3][tm - HALO:tm, :], 0.0)
        extu[HALO:HALO + tm, :] = rr[2][...]
        du = jnp.zeros((tm, D), F32)
        for kk in range(CONV_W):
            du = du + extd[pl.ds(CONV_W - 1 - kk, tm), :] * cc[0][kk:kk + 1, :]
            aa[0][kk:kk + 1, :] += jnp.sum(dcur * extu[pl.ds(HALO - (CONV_W - 1) + kk, tm), :], axis=0, keepdims=True)
        av = rr[4][...]
        a1, sg = av[:, :D], _sigmoid(av[:, D:])
        da = jnp.concatenate([du * sg, du * a1 * (sg * (1.0 - sg))], axis=1)
        aa[1][...] += jnp.sum(da, axis=0, keepdims=True)
        da = da.astype(MM)
        oo[0][...] = da
        dx, dsh, dsc, dg0 = _prenorm_bwd(rr[5][...], cc[2][...], cc[3][...], _dot_nt(da, cc[1][...]))
        oo[1][...] = rr[6][...] + dx
        aa[2][...] += dsh
        aa[3][...] += dsc
        aa[4][...] += dg0

    return _rows(name, body, T, tm,
                 [(duc, 'cur'), (duc, 'next'), (u, 'cur'), (u, 'prev'), (a, 'cur'), (x, 'cur'), (dxo, 'cur')],
                 [w_dw, w_pw1, g0, sc], [(2 * D, MM), (D, F32)],
                 accs=[(32, D), (1, 2 * D), (1, D), (1, D), (1, D)],
                 scratch=[pltpu.VMEM((tm + HALO, D), F32), pltpu.VMEM((tm + HALO, D), F32)])


PHALO = 16


def _pool_fwd(name, h, w, b, scale, x, g1, gt):
    T, D = h.shape
    G = len(POOL_WINDOWS)
    Cg = D // G
    tm = min(256, T)

    def body(i, n, rr, cc, oo, aa, ss):
        ext = ss[0]
        ext[0:PHALO, :] = jnp.where(i > 0, rr[1][tm - PHALO:tm, :], 0.0)
        ext[PHALO:PHALO + tm, :] = rr[0][...]
        t_glob = i * tm + lax.broadcasted_iota(jnp.int32, (tm, 1), 0)
        ps, ys = [], []
        for g, win in enumerate(POOL_WINDOWS):
            cols = slice(g * Cg, (g + 1) * Cg)
            s = ext[pl.ds(PHALO, tm), cols]
            for j in range(1, win):
                s = s + ext[pl.ds(PHALO - j, tm), cols]
            cnt = jnp.minimum(t_glob + 1, win).astype(F32)
            p = (s / cnt - ext[pl.ds(PHALO, tm), cols]).astype(MM)
            ps.append(p)
            ys.append(_dot(p, cc[0][g]) + cc[1][:, cols])
        ypre = jnp.concatenate(ys, axis=1)
        y = ypre * cc[2][...]
        oo[0][...] = jnp.concatenate(ps, axis=1)
        oo[1][...] = ypre
        oo[2][...] = y
        oo[3][...] = rr[2][...] + cc[4][...] * _rms(y, cc[3][...])

    return _rows(name, body, T, tm, [(h, 'cur'), (h, 'prev'), (x, 'cur')], [w, b, scale, g1, gt],
                 [(D, MM), (D, F32), (D, F32), (D, F32)], scratch=[pltpu.VMEM((tm + PHALO, D), F32)])


def _pool_bwd1(name, dy, ypre, scale, w):
    T, D = ypre.shape
    G = len(POOL_WINDOWS)
    Cg = D // G

    def body(i, n, rr, cc, oo, aa, ss):
        dyv = rr[0][...].astype(F32)
        aa[0][...] += jnp.sum(dyv * rr[1][...], axis=0, keepdims=True)
        dypre = dyv * cc[0][...]
        aa[1][...] += jnp.sum(dypre, axis=0, keepdims=True)
        dypre = dypre.astype(MM)
        oo[1][...] = dypre
        oo[0][...] = jnp.concatenate([_dot_nt(dypre[:, g * Cg:(g + 1) * Cg], cc[1][g]) for g in range(G)], axis=1)

    return _rows(name, body, T, 256, [(dy, 'cur'), (ypre, 'cur')], [scale, w], [(D, F32), (D, MM)], accs=[(1, D)] * 2)


def _pool_bwd2(name, dp, x, dxo, g0, sc):
    T, D = x.shape
    G = len(POOL_WINDOWS)
    Cg = D // G
    tm = min(256, T)

    def body(i, n, rr, cc, oo, aa, ss):
        ext = ss[0]
        t_glob = i * tm + lax.broadcasted_iota(jnp.int32, (tm, 1), 0)
        dcur = rr[0][...]
        dhs = []
        for g, win in enumerate(POOL_WINDOWS):
            cols = slice(g * Cg, (g + 1) * Cg)
            cnt = jnp.minimum(t_glob + 1, win).astype(F32)
            ext[0:tm, cols] = dcur[:, cols] / cnt
            ext[tm:tm + PHALO, cols] = jnp.where(i < n - 1, rr[1][0:PHALO, cols] * (1.0 / win), 0.0)
        for g, win in enumerate(POOL_WINDOWS):
            cols = slice(g * Cg, (g + 1) * Cg)
            s = ext[pl.ds(0, tm), cols]
            for j in range(1, win):
                s = s + ext[pl.ds(j, tm), cols]
            dhs.append(s - dcur[:, cols])
        dx, dsh, dsc, dg0 = _prenorm_bwd(rr[2][...], cc[0][...], cc[1][...], jnp.concatenate(dhs, axis=1))
        oo[0][...] = rr[3][...] + dx
        aa[0][...] += dsh
        aa[1][...] += dsc
        aa[2][...] += dg0

    return _rows(name, body, T, tm, [(dp, 'cur'), (dp, 'next'), (x, 'cur'), (dxo, 'cur')], [g0, sc], [(D, F32)],
                 accs=[(1, D)] * 3, scratch=[pltpu.VMEM((tm + PHALO, D), F32)])


def _loss_head(x, tgt):
    T, D = x.shape

    def body(i, n, rr, cc, oo, aa, ss):
        err = rr[0][...] - rr[1][...]
        oo[0][...] = err * (1.0 / D)
        aa[0][...] += jnp.sum(err * err, axis=0, keepdims=True)

        @pl.when(i == n - 1)
        def _():
            aa[1][...] = jnp.broadcast_to(jnp.sum(aa[0][...], axis=1, keepdims=True) * (0.5 / D), (1, 128))

    dx, _, loss_row = _rows("loss_head", body, T, 512, [(x, 'cur'), (tgt, 'cur')], [], [(D, F32)], accs=[(1, D), (1, 128)])
    return dx, loss_row


def _adamw(name, w, g, m, v):
    shape = w.shape
    C = shape[-1]
    R = w.size // C
    w2, g2, m2, v2 = (t.reshape(R, C) for t in (w, g, m, v))
    br = R
    if R * C * 4 > (1 << 20):
        br = 8
        while br * 2 * C * 4 <= (1 << 20) and R % (br * 2) == 0:
            br *= 2
    b1c = 1.0 - ADAM_B1 ** ADAM_STEP
    b2c = 1.0 - ADAM_B2 ** ADAM_STEP

    def body(w_ref, g_ref, m_ref, v_ref, d_ref, mo_ref, vo_ref):
        gv = g_ref[...]
        mn = ADAM_B1 * m_ref[...] + (1.0 - ADAM_B1) * gv
        vn = ADAM_B2 * v_ref[...] + (1.0 - ADAM_B2) * (gv * gv)
        d_ref[...] = -ADAM_LR * ((mn / b1c) / (jnp.sqrt(vn / b2c) + ADAM_EPS) + ADAM_WD * w_ref[...])
        mo_ref[...] = mn
        vo_ref[...] = vn

    spec = pl.BlockSpec((br, C), lambda r: (r, 0))
    outs = pl.pallas_call(body, grid=(R // br,), in_specs=[spec] * 4, out_specs=[spec] * 3,
                          out_shape=[jax.ShapeDtypeStruct((R, C), F32)] * 3, name=name,
                          compiler_params=_cparams(("arbitrary",)))(w2, g2, m2, v2)
    return tuple(t.reshape(shape) for t in outs)


def _to_shards(full, ax):
    s = full.shape
    return jnp.moveaxis(full.reshape(s[:ax] + (N_DEV, s[ax] // N_DEV) + s[ax + 1:]), ax, 0)


def _unshard(g, ax):
    r = jnp.moveaxis(g, 0, ax)
    s = r.shape
    return r.reshape(s[:ax] + (s[ax] * s[ax + 1],) + s[ax + 2:])


def _pack(parts, dtype, row_mult):
    lead = parts[0].shape[:-1]
    flat = jnp.concatenate([p.astype(dtype) for p in parts], axis=-1)
    n = flat.shape[-1]
    per = row_mult * 1024
    tot = -(-n // per) * per
    flat = jnp.pad(flat, [(0, 0)] * len(lead) + [(0, tot - n)])
    return flat.reshape(lead + (tot // 1024, 1024))


def _pad_heads(w, lo, hi):
    K = w.shape[0]
    r = w.reshape(K, N_HEADS, -1)[:, :, lo:hi]
    return jnp.pad(r, ((0, 0), (0, 0), (0, HEAD_PAD - (hi - lo)))).reshape(K, N_HEADS * HEAD_PAD)


def kernel(x, c, positions, ada_w, ada_b, norm_g, mla_w_dq, mla_q_norm_g, mla_w_uq, mla_w_dkv, mla_kv_norm_g, mla_w_ukv, mla_w_o, conv_w_pw1, conv_b_pw1, conv_w_dw, conv_b_dw, conv_ln_g, conv_ln_b, conv_w_pw2, conv_b_pw2, pool_w, pool_b, pool_scale, ffn_w1, ffn_w2, loss_target, m_ada_w, m_ada_b, m_norm_g, m_mla_w_dq, m_mla_q_norm_g, m_mla_w_uq, m_mla_w_dkv, m_mla_kv_norm_g, m_mla_w_ukv, m_mla_w_o, m_conv_w_pw1, m_conv_b_pw1, m_conv_w_dw, m_conv_b_dw, m_conv_ln_g, m_conv_ln_b, m_conv_w_pw2, m_conv_b_pw2, m_pool_w, m_pool_b, m_pool_scale, m_ffn_w1, m_ffn_w2, v_ada_w, v_ada_b, v_norm_g, v_mla_w_dq, v_mla_q_norm_g, v_mla_w_uq, v_mla_w_dkv, v_mla_kv_norm_g, v_mla_w_ukv, v_mla_w_o, v_conv_w_pw1, v_conv_b_pw1, v_conv_w_dw, v_conv_b_dw, v_conv_ln_g, v_conv_ln_b, v_conv_w_pw2, v_conv_b_pw2, v_pool_w, v_pool_b, v_pool_scale, v_ffn_w1, v_ffn_w2):
    args = dict(locals())
    W = {n: args[n] for n, _ in WEIGHTS}
    M1 = {n: args['m_' + n] for n, _ in WEIGHTS}
    V2 = {n: args['v_' + n] for n, _ in WEIGHTS}
    D = D_MODEL
    T = x.shape[1]
    L = ffn_w1.shape[0]
    xi, yi, ci = _place()
    me = 4 * xi + 2 * yi + ci
    n_ada = ada_w.shape[2]

    small_sizes = [W[n].size for n in SMALL]
    small_in = _pack([c.reshape(-1)] + [W[n].reshape(-1) for n in SMALL], F32, 8)
    small_all = _ag_small("ag_small_params", small_in).reshape(N_DEV, -1)
    c_all = small_all[:, :D]
    Ws = {}
    off = D
    for n, sz in zip(SMALL, small_sizes):
        Ws[n] = _unshard(small_all[:, off:off + sz].reshape((N_DEV,) + W[n].shape), SHARD_AXIS[n])
        off += sz
    c16 = jnp.pad(c_all, ((0, 16 - N_DEV), (0, 0)))

    ada_b_cols = lax.dynamic_slice_in_dim(ada_b, me * n_ada, n_ada, axis=1).reshape(L, 1, n_ada)
    mod_part = _mod_part(c16, ada_w, ada_b_cols)[:, :N_DEV]
    mod_all = _ag_small("ag_mod", mod_part.reshape(L * N_DEV, n_ada)).reshape(N_DEV, L, N_DEV, n_ada)
    mod_mine = lax.dynamic_index_in_dim(mod_all, me, axis=2, keepdims=False)
    mod = jnp.transpose(mod_mine, (1, 0, 2)).reshape(L, 6, 1, D)

    rest = [n for n in BIG if not n.startswith('ffn')]
    w1g, w2g, rest_all = _ag_big("ag_weights", [ffn_w1.astype(MM), ffn_w2.astype(MM), _pack([W[n].reshape(-1) for n in rest], MM, 32)])
    rest_all = rest_all.reshape(N_DEV, -1)
    Wb = {}
    off = 0
    for n in rest:
        Wb[n] = rest_all[:, off:off + W[n].size].reshape((N_DEV,) + W[n].shape)
        off += W[n].size
    full = lambda n: _unshard(Wb[n], SHARD_AXIS[n])
    w_dq, w_uq, w_dkv, w_ukv, w_o = full('mla_w_dq'), full('mla_w_uq'), full('mla_w_dkv'), full('mla_w_ukv'), full('mla_w_o')
    w_pw1, w_pw2, w_pool = full('conv_w_pw1'), full('conv_w_pw2'), full('pool_w')
    n_mla = w_dq.shape[0]
    w_uq_p = [_pad_heads(w_uq[j], 0, QK_NOPE + QK_ROPE) for j in range(n_mla)]
    w_ukv_k = [_pad_heads(w_ukv[j], 0, QK_NOPE) for j in range(n_mla)]
    w_ukv_v = [_pad_heads(w_ukv[j], QK_NOPE, QK_NOPE + V_HEAD) for j in range(n_mla)]
    w_dkv_p = [jnp.pad(jnp.concatenate([w_dkv[j][:, :KV_LORA], jnp.zeros((D, QK_NOPE), MM), w_dkv[j][:, KV_LORA:]], axis=1),
                       ((0, 0), (0, HEAD_PAD - QK_NOPE - QK_ROPE))) for j in range(n_mla)]
    w_o_p = [jnp.pad(w_o[j].reshape(N_HEADS, V_HEAD, D), ((0, 0), (0, HEAD_PAD - V_HEAD), (0, 0))).reshape(N_HEADS * HEAD_PAD, D)
             for j in range(n_mla)]
    w_dw32 = jnp.pad(Ws['conv_w_dw'], ((0, 0), (0, 32 - CONV_W), (0, 0)))
    row = lambda t: t.reshape(1, -1)

    half = QK_ROPE // 2
    inv_freq = ROPE_THETA ** (-jnp.arange(0, QK_ROPE, 2, dtype=F32) / QK_ROPE)
    invf = jnp.zeros((1, HEAD_PAD), F32).at[0, QK_NOPE:QK_NOPE + half].set(inv_freq).at[0, QK_NOPE + half:QK_NOPE + QK_ROPE].set(inv_freq)
    rC, rS1, rS2 = _rope_tables(positions.reshape(T, 1).astype(F32), invf)

    xs = x.reshape(T, D)
    saved = []
    for i in range(L):
        kind, j = i % 3, i // 3
        sh_m, sc_m, gt_m, sh_f, sc_f, gt_f = (mod[i, r] for r in range(6))
        g = [row(Ws['norm_g'][i, r]) for r in range(4)]
        st = dict(x0=xs)
        if kind == 0:
            h = _prenorm(f"prenorm_m{i}", xs, g[0], sc_m, sh_m, MM)
            cq_raw, cq, ckv_raw, ckv, q, k, v = _mla_proj(f"mla_proj{i}", h, rC, rS1, rS2, w_dq[j], row(Ws['mla_q_norm_g'][j]), w_uq_p[j],
                                                          w_dkv_p[j], row(Ws['mla_kv_norm_g'][j]), w_ukv_k[j], w_ukv_v[j])
            o, lse = _attn_fwd(f"attn_fwd{i}", q, k, v)
            y, xs = _mm_post(f"mla_out{i}", o, w_o_p[j], None, xs, g[1], gt_m)
            st.update(h=h, cq_raw=cq_raw, cq=cq, ckv_raw=ckv_raw, ckv=ckv, q=q, k=k, v=v, o=o, lse=lse, y=y)
        elif kind == 1:
            h = _prenorm(f"prenorm_m{i}", xs, g[0], sc_m, sh_m, MM)
            a, u = _conv_glu(f"conv_glu{i}", h, w_pw1[j], row(W['conv_b_pw1'][j]))
            uc, z, y, xs = _conv_dw(f"conv_dw{i}", u, w_dw32[j], row(W['conv_b_dw'][j]), row(W['conv_ln_g'][j]), row(W['conv_ln_b'][j]),
                                    w_pw2[j], row(W['conv_b_pw2'][j]), xs, g[1], gt_m)
            st.update(h=h, a=a, u=u, uc=uc, z=z, y=y)
        else:
            h = _prenorm(f"prenorm_m{i}", xs, g[0], sc_m, sh_m, F32)
            p, ypre, y, xs = _pool_fwd(f"pool_fwd{i}", h, w_pool[j], row(Ws['pool_b'][j]), row(Ws['pool_scale'][j]), xs, g[1], gt_m)
            st.update(p=p, ypre=ypre, y=y)
        st['x1'] = xs
        hf = _prenorm(f"prenorm_f{i}", xs, g[2], sc_f, sh_f, MM)
        af, yf, xs = _ffn_fwd(f"ffn_fwd{i}", i, hf, w1g, w2g, xs, g[3], gt_f)
        st.update(hf=hf, af=af, yf=yf)
        saved.append(st)

    dx, loss_row = _loss_head(xs, loss_target.reshape(T, D))

    G = {}
    dmod = [None] * L
    dnorm = [None] * L
    for i in reversed(range(L)):
        kind, j = i % 3, i // 3
        sh_m, sc_m, gt_m, sh_f, sc_f, gt_f = (mod[i, r] for r in range(6))
        g = [row(Ws['norm_g'][i, r]) for r in range(4)]
        st = saved[i]
        dy, dg3, dgt_f, _ = _post_bwd(f"post_bwd_f{i}", dx, st['yf'], g[3], gt_f)
        da, dx, dsh_f, dsc_f, dg2 = _ffn_bwd(f"ffn_bwd{i}", i, dy, st['af'], w1g, w2g, st['x1'], dx, g[2], sc_f)
        G.setdefault('ffn_w1', [None] * L)[i] = _mm_tn(f"ffn_dw1_{i}", st['hf'], da, col_shards=N_DEV)
        G.setdefault('ffn_w2', [None] * L)[i] = _mm_tn(f"ffn_dw2_{i}", st['af'], dy, sqrelu=True)
        dy, dg1, dgt_m, dysum = _post_bwd(f"post_bwd_m{i}", dx, st['y'], g[1], gt_m)
        if kind == 0:
            do = _mm_nt_rows(f"mla_do{i}", dy, w_o_p[j])
            delta = _attn_delta(f"attn_delta{i}", do, st['o'])
            dq, dk, dv = _attn_bwd(f"attn_bwd{i}", st['q'], st['k'], st['v'], do, st['lse'], delta)
            dq_pre, dcq_raw, dckv_all, dx, dqg, dkvg, dsh_m, dsc_m, dg0 = _mla_proj_bwd(
                f"mla_proj_bwd{i}", dq, dk, dv, rC, rS1, rS2, st['cq_raw'], st['ckv_raw'], st['x0'], dx, w_uq_p[j], w_ukv_k[j], w_ukv_v[j],
                w_dq[j], w_dkv_p[j], row(Ws['mla_q_norm_g'][j]), row(Ws['mla_kv_norm_g'][j]), g[0], sc_m)
            dwo = _mm_tn(f"mla_dwo{i}", st['o'], dy)
            dwuq = _mm_tn(f"mla_dwuq{i}", st['cq'], dq_pre)
            dwk = _mm_tn(f"mla_dwukvk{i}", st['ckv'], dk)
            dwv = _mm_tn(f"mla_dwukvv{i}", st['ckv'], dv)
            dwdq = _mm_tn(f"mla_dwdq{i}", st['h'], dcq_raw)
            dwdkv = _mm_tn(f"mla_dwdkv{i}", st['h'], dckv_all)
            G.setdefault('mla_w_o', [None] * n_mla)[j] = dwo.reshape(N_HEADS, HEAD_PAD, D)[:, :V_HEAD].reshape(N_HEADS * V_HEAD, D)
            G.setdefault('mla_w_uq', [None] * n_mla)[j] = dwuq.reshape(Q_LORA, N_HEADS, HEAD_PAD)[:, :, :QK_NOPE + QK_ROPE].reshape(Q_LORA, -1)
            G.setdefault('mla_w_ukv', [None] * n_mla)[j] = jnp.concatenate(
                [dwk.reshape(KV_LORA, N_HEADS, HEAD_PAD)[:, :, :QK_NOPE], dwv.reshape(KV_LORA, N_HEADS, HEAD_PAD)[:, :, :V_HEAD]], axis=2).reshape(KV_LORA, -1)
            G.setdefault('mla_w_dq', [None] * n_mla)[j] = dwdq
            G.setdefault('mla_w_dkv', [None] * n_mla)[j] = jnp.concatenate([dwdkv[:, :KV_LORA], dwdkv[:, KV_LORA + QK_NOPE:KV_LORA + QK_NOPE + QK_ROPE]], axis=1)
            G.setdefault('mla_q_norm_g', [None] * n_mla)[j] = dqg[0]
            G.setdefault('mla_kv_norm_g', [None] * n_mla)[j] = dkvg[0]
        elif kind == 1:
            duc, dlng, dlnb, dbdw = _conv_bwd1(f"conv_bwd1_{i}", dy, st['uc'], w_pw2[j], row(W['conv_ln_g'][j]), row(W['conv_ln_b'][j]))
            da, dx, dwdw, dbpw1, dsh_m, dsc_m, dg0 = _conv_bwd2(f"conv_bwd2_{i}", duc, st['u'], st['a'], st['x0'], dx, w_dw32[j], w_pw1[j], g[0], sc_m)
            G['conv_w_pw2'] = [_mm_tn(f"conv_dwpw2_{i}", st['z'], dy)]
            G['conv_w_pw1'] = [_mm_tn(f"conv_dwpw1_{i}", st['h'], da)]
            G['conv_w_dw'] = [dwdw[:CONV_W]]
            G['conv_b_pw1'], G['conv_b_dw'], G['conv_ln_g'], G['conv_ln_b'], G['conv_b_pw2'] = [dbpw1[0]], [dbdw[0]], [dlng[0]], [dlnb[0]], [dysum[0]]
        else:
            dp, dypre, dscale, dpb = _pool_bwd1(f"pool_bwd1_{i}", dy, st['ypre'], row(Ws['pool_scale'][j]), w_pool[j])
            dx, dsh_m, dsc_m, dg0 = _pool_bwd2(f"pool_bwd2_{i}", dp, st['x0'], dx, g[0], sc_m)
            G['pool_w'] = [_mm_tn(f"pool_dw{i}", st['p'], dypre, diag=len(POOL_WINDOWS))]
            G['pool_b'] = [dpb.reshape(len(POOL_WINDOWS), -1)]
            G['pool_scale'] = [dscale[0]]
        dmod[i] = jnp.concatenate([dsh_m, dsc_m, dgt_m, dsh_f, dsc_f, dgt_f], axis=1)
        dnorm[i] = jnp.concatenate([dg0, dg1, dg2, dg3], axis=0)
    G['norm_g'] = dnorm
    grad_x = dx.reshape(x.shape)

    rs_names = [n for n, ax in WEIGHTS if ax is not None and n != 'ada_w' and not n.startswith('ffn')]
    parts = [_to_shards(jnp.stack(G[n], axis=0), SHARD_AXIS[n]).reshape(N_DEV, -1) for n in rs_names]
    tensors = [_pack(parts, F32, RS_ROWS)]
    for i in range(L):
        tensors += [G['ffn_w1'][i], G['ffn_w2'][i].reshape(N_DEV, -1, D)]
    red = _reduce_scatter("rs", tensors, ci, 2 * xi + yi)
    grads = {'ffn_w1': jnp.stack(red[1::2], axis=0), 'ffn_w2': jnp.stack(red[2::2], axis=0)}
    red0 = red[0].reshape(-1)
    off = 0
    for n in rs_names:
        grads[n] = red0[off:off + W[n].size].reshape(W[n].shape)
        off += W[n].size

    dmod_mine = jnp.concatenate(dmod, axis=1).reshape(-1)
    fin_in = _pack([dmod_mine] + [G[n][0].reshape(-1) for n in REPL] + [loss_row.reshape(-1)], F32, 8)
    fin_all = _ag_small("ag_final", fin_in)
    fin_sum = _sum_devices("final_sum", fin_all).reshape(-1)
    nm = L * 6 * D
    grads['ada_b'] = fin_sum[:nm].reshape(L, 6 * D)
    off = nm
    for n in REPL:
        grads[n] = fin_sum[off:off + W[n].size].reshape(W[n].shape)
        off += W[n].size
    loss = fin_sum[off]
    dmod_all = fin_all.reshape(N_DEV, -1)[:, :nm].reshape(N_DEV, L, 6 * D)
    dmod_cols = lax.dynamic_slice_in_dim(dmod_all, me * n_ada, n_ada, axis=2)
    dmod16 = jnp.pad(jnp.transpose(dmod_cols, (1, 0, 2)), ((0, 0), (0, 16 - N_DEV), (0, 0)))
    grads['ada_w'] = _ada_w_grad(c16, dmod16)

    deltas, new_m, new_v = {}, {}, {}
    for n, _ in WEIGHTS:
        deltas[n], new_m[n], new_v[n] = _adamw("adamw_" + n, W[n], grads[n], M1[n], V2[n])
    names = [n for n, _ in WEIGHTS]
    return (loss, grad_x, *[grads[n] for n in names], *[deltas[n] for n in names], *[new_m[n] for n in names],
            *[new_v[n] for n in names])
```

```python
import functools
import math

import jax
import jax.numpy as jnp
from jax import lax
from jax.experimental import pallas as pl
from jax.experimental.pallas import tpu as pltpu

F32 = jnp.float32
MM = jnp.bfloat16
EPS = 1e-6
NEG = -1e30
N_DEV = 8
VMEM_LIMIT = 48 * 1024 * 1024
MESH = pl.DeviceIdType.MESH

D_MODEL = 1024
N_HEADS = 16
HEAD_PAD = 128
QK_NOPE, QK_ROPE, V_HEAD = 64, 32, 64
Q_LORA, KV_LORA = 384, 256
CHUNK = 64
CONV_W = 31
POOL_WINDOWS = (2, 4, 8, 16)
ROPE_THETA = 10000.0
ATT_SCALE = 1.0 / math.sqrt(QK_NOPE + QK_ROPE)

ADAM_LR, ADAM_B1, ADAM_B2, ADAM_EPS, ADAM_WD, ADAM_STEP = 0.001, 0.9, 0.999, 1e-08, 0.01, 10

WEIGHTS = [('ada_w', 2), ('ada_b', None), ('norm_g', 2), ('mla_w_dq', 1), ('mla_q_norm_g', 1), ('mla_w_uq', 2),
           ('mla_w_dkv', 1), ('mla_kv_norm_g', 1), ('mla_w_ukv', 2), ('mla_w_o', 1), ('conv_w_pw1', 2),
           ('conv_b_pw1', None), ('conv_w_dw', 2), ('conv_b_dw', None), ('conv_ln_g', None), ('conv_ln_b', None),
           ('conv_w_pw2', 1), ('conv_b_pw2', None), ('pool_w', 2), ('pool_b', 2), ('pool_scale', 1),
           ('ffn_w1', 2), ('ffn_w2', 1)]
SHARD_AXIS = dict(WEIGHTS)
BIG = ['mla_w_dq', 'mla_w_uq', 'mla_w_dkv', 'mla_w_ukv', 'mla_w_o', 'conv_w_pw1', 'conv_w_pw2', 'pool_w', 'ffn_w1', 'ffn_w2']
SMALL = ['norm_g', 'mla_q_norm_g', 'mla_kv_norm_g', 'conv_w_dw', 'pool_b', 'pool_scale']
REPL = ['conv_b_pw1', 'conv_b_dw', 'conv_ln_g', 'conv_ln_b', 'conv_b_pw2']


def _dot(a, b):
    return jnp.dot(a.astype(MM), b.astype(MM), preferred_element_type=F32)


def _dot_nt(a, b):
    return lax.dot_general(a.astype(MM), b.astype(MM), (((1,), (1,)), ((), ())), preferred_element_type=F32)


def _dot_tn(a, b):
    return lax.dot_general(a.astype(MM), b.astype(MM), (((0,), (0,)), ((), ())), preferred_element_type=F32)


def _sigmoid(x):
    return 1.0 / (1.0 + jnp.exp(-x))


def _rstd(x):
    return lax.rsqrt(jnp.mean(x * x, axis=-1, keepdims=True) + EPS)


def _rms(x, g):
    return x * _rstd(x) * g


def _rms_bwd(x, g, dout):
    r = _rstd(x)
    xn = x * r
    dg = jnp.sum(dout * xn, axis=0, keepdims=True)
    dxn = dout * g
    dx = r * (dxn - xn * jnp.mean(dxn * xn, axis=-1, keepdims=True))
    return dx, dg


def _prenorm_bwd(x, g0, sc, dh):
    r = _rstd(x)
    xn = x * r
    dsh = jnp.sum(dh, axis=0, keepdims=True)
    dsc = jnp.sum(dh * (xn * g0), axis=0, keepdims=True)
    dn = dh * (1.0 + sc)
    dg0 = jnp.sum(dn * xn, axis=0, keepdims=True)
    dxn = dn * g0
    dx = r * (dxn - xn * jnp.mean(dxn * xn, axis=-1, keepdims=True))
    return dx, dsh, dsc, dg0


def _cparams(sem):
    return pltpu.CompilerParams(dimension_semantics=sem, vmem_limit_bytes=VMEM_LIMIT)


def _rows(name, body, n_rows, tm, rows, consts, outs, accs=(), scratch=()):
    tm = min(tm, n_rows)
    nblk = n_rows // tm
    nr, nc, no, na = len(rows), len(consts), len(outs), len(accs)
    in_specs, args = [], []
    for a, kind in rows:
        if kind == 'cur':
            im = lambda i: (i, 0)
        elif kind == 'prev':
            im = lambda i: (jnp.maximum(i - 1, 0), 0)
        else:
            im = lambda i: (jnp.minimum(i + 1, nblk - 1), 0)
        in_specs.append(pl.BlockSpec((tm, a.shape[1]), im))
        args.append(a)
    for a in consts:
        in_specs.append(pl.BlockSpec(a.shape, lambda i, nd=a.ndim: (0,) * nd))
        args.append(a)
    out_specs = [pl.BlockSpec((tm, c), lambda i: (i, 0)) for c, _ in outs]
    out_specs += [pl.BlockSpec(s, lambda i, nd=len(s): (0,) * nd) for s in accs]
    out_shape = [jax.ShapeDtypeStruct((n_rows, c), dt) for c, dt in outs]
    out_shape += [jax.ShapeDtypeStruct(s, F32) for s in accs]

    def kern(*refs):
        i = pl.program_id(0)
        rr = refs[:nr]
        cc = refs[nr:nr + nc]
        oo = refs[nr + nc:nr + nc + no]
        aa = refs[nr + nc + no:nr + nc + no + na]
        ss = refs[nr + nc + no + na:]

        @pl.when(i == 0)
        def _():
            for a in aa:
                a[...] = jnp.zeros(a.shape, F32)

        body(i, nblk, rr, cc, oo, aa, ss)

    return pl.pallas_call(kern, grid=(nblk,), in_specs=in_specs, out_specs=out_specs, out_shape=out_shape,
                          scratch_shapes=list(scratch), name=name, compiler_params=_cparams(("arbitrary",)))(*args)


def _place():
    return lax.axis_index("x"), lax.axis_index("y"), lax.axis_index("c")


def _ag_small(name, xs):
    R, C = xs.shape

    def body(x_ref, out_ref, send_sems, recv_sems):
        x, y, c = _place()
        me = 4 * x + 2 * y + c
        out_ref[me] = x_ref[...]
        copies = []
        for k in range(1, N_DEV):
            peer = ((1 - x) if k & 4 else x, (1 - y) if k & 2 else y, (1 - c) if k & 1 else c)
            cp = pltpu.make_async_remote_copy(src_ref=x_ref, dst_ref=out_ref.at[me], send_sem=send_sems.at[k - 1],
                                              recv_sem=recv_sems.at[k - 1], device_id=peer, device_id_type=MESH)
            cp.start()
            copies.append(cp)
        for cp in copies:
            cp.wait()

    return pl.pallas_call(
        body, out_shape=jax.ShapeDtypeStruct((N_DEV, R, C), xs.dtype),
        in_specs=[pl.BlockSpec(memory_space=pltpu.VMEM)], out_specs=pl.BlockSpec(memory_space=pltpu.VMEM),
        scratch_shapes=[pltpu.SemaphoreType.DMA((N_DEV - 1,)), pltpu.SemaphoreType.DMA((N_DEV - 1,))], name=name)(xs)


def _ag_big(name, xs):
    nt = len(xs)

    def body(*refs):
        x_refs, out_refs = refs[:nt], refs[nt:2 * nt]
        send_sems, recv_sems, local_sems = refs[2 * nt:]
        x, y, c = _place()
        me, sibling = (x, y, c), (x, y, 1 - c)
        chips = [(1 - x, y), (x, 1 - y), (1 - x, 1 - y)]

        def copy(t, k, block, to, own=False):
            px, py, pc = block
            rows = out_refs[t].at[4 * px + 2 * py + pc]
            return pltpu.make_async_remote_copy(src_ref=x_refs[t] if own else rows, dst_ref=rows, send_sem=send_sems.at[7 * t + k],
                                                recv_sem=recv_sems.at[7 * t + k], device_id=to, device_id_type=MESH)

        mine = [pltpu.make_async_copy(x_refs[t], out_refs[t].at[4 * x + 2 * y + c], local_sems.at[t]) for t in range(nt)]
        for cp in mine:
            cp.start()
        first = []
        for t in range(nt):
            first.append(copy(t, 0, me, sibling, own=True))
            first += [copy(t, 1 + j, me, (*chip, c), own=True) for j, chip in enumerate(chips)]
        for cp in first:
            cp.start()
        passed = []
        for t in range(nt):
            for j, chip in enumerate(chips):
                copy(t, 1 + j, (*chip, c), me).wait_recv()
                cp = copy(t, 4 + j, (*chip, c), sibling)
                cp.start()
                passed.append(cp)
        for t in range(nt):
            copy(t, 0, sibling, me).wait_recv()
            for j, chip in enumerate(chips):
                copy(t, 4 + j, (*chip, 1 - c), me).wait_recv()
        for cp in first + passed:
            cp.wait_send()
        for cp in mine:
            cp.wait()

    hbm = pl.BlockSpec(memory_space=pl.ANY)
    return pl.pallas_call(
        body, out_shape=[jax.ShapeDtypeStruct((N_DEV,) + t.shape, t.dtype) for t in xs],
        in_specs=[hbm] * nt, out_specs=[hbm] * nt,
        scratch_shapes=[pltpu.SemaphoreType.DMA((7 * nt,)), pltpu.SemaphoreType.DMA((7 * nt,)), pltpu.SemaphoreType.DMA((nt,))],
        name=name)(*xs)


def _rs_pair(name, ps):
    nt = len(ps)

    def body(*refs):
        p_refs, recv_refs = refs[:nt], refs[nt:2 * nt]
        send_sems, recv_sems = refs[2 * nt:]
        x, y, c = _place()
        copies = []
        for t in range(nt):
            for j in range(4):
                cp = pltpu.make_async_remote_copy(src_ref=p_refs[t].at[j, 1 - c], dst_ref=recv_refs[t].at[j], send_sem=send_sems.at[4 * t + j],
                                                  recv_sem=recv_sems.at[4 * t + j], device_id=(x, y, 1 - c), device_id_type=MESH)
                cp.start()
                copies.append(cp)
        for cp in copies:
            cp.wait()

    hbm = pl.BlockSpec(memory_space=pl.ANY)
    return pl.pallas_call(
        body, out_shape=[jax.ShapeDtypeStruct((4,) + p.shape[2:], p.dtype) for p in ps], in_specs=[hbm] * nt, out_specs=[hbm] * nt,
        scratch_shapes=[pltpu.SemaphoreType.DMA((4 * nt,)), pltpu.SemaphoreType.DMA((4 * nt,))], name=name)(*ps)


def _rs_chips(name, ss):
    nt = len(ss)

    def body(*refs):
        s_refs, recv_refs = refs[:nt], refs[nt:2 * nt]
        send_sems, recv_sems, local_sems = refs[2 * nt:]
        x, y, c = _place()
        mine = 2 * x + y
        owns = [pltpu.make_async_copy(s_refs[t].at[mine], recv_refs[t].at[mine], local_sems.at[t]) for t in range(nt)]
        for cp in owns:
            cp.start()
        copies = []
        for t in range(nt):
            for k in range(1, 4):
                px = (1 - x) if k & 2 else x
                py = (1 - y) if k & 1 else y
                cp = pltpu.make_async_remote_copy(src_ref=s_refs[t].at[2 * px + py], dst_ref=recv_refs[t].at[mine],
                                                  send_sem=send_sems.at[3 * t + k - 1], recv_sem=recv_sems.at[3 * t + k - 1],
                                                  device_id=(px, py, c), device_id_type=MESH)
                cp.start()
                copies.append(cp)
        for cp in copies:
            cp.wait()
        for cp in owns:
            cp.wait()

    hbm = pl.BlockSpec(memory_space=pl.ANY)
    return pl.pallas_call(
        body, out_shape=[jax.ShapeDtypeStruct(s_.shape, s_.dtype) for s_ in ss], in_specs=[hbm] * nt, out_specs=[hbm] * nt,
        scratch_shapes=[pltpu.SemaphoreType.DMA((3 * nt,)), pltpu.SemaphoreType.DMA((3 * nt,)), pltpu.SemaphoreType.DMA((nt,))],
        name=name)(*ss)


RS_ROWS = 256


def _pair_sum(name, p, recv, my_c, my_chip):
    _, _, r, c = p.shape
    tr = RS_ROWS

    def body(sc_ref, p_ref, r_ref, o_ref, own_ref):
        s = p_ref[...] + r_ref[...]
        o_ref[...] = s.astype(MM)

        @pl.when(pl.program_id(1) == sc_ref[1])
        def _():
            own_ref[...] = s

    return pl.pallas_call(
        body, grid_spec=pltpu.PrefetchScalarGridSpec(
            num_scalar_prefetch=1, grid=(r // tr, 4),
            in_specs=[pl.BlockSpec((None, None, tr, c), lambda i, j, sc: (j, sc[0], i, 0)),
                      pl.BlockSpec((None, tr, c), lambda i, j, sc: (j, i, 0))],
            out_specs=[pl.BlockSpec((None, tr, c), lambda i, j, sc: (j, i, 0)), pl.BlockSpec((tr, c), lambda i, j, sc: (i, 0))]),
        out_shape=[jax.ShapeDtypeStruct((4, r, c), MM), jax.ShapeDtypeStruct((r, c), F32)], name=name,
        compiler_params=_cparams(("arbitrary", "arbitrary")))(jnp.stack([my_c, my_chip]), p, recv)


def _chip_sum(name, own, recv, my_chip):
    _, r, c = recv.shape
    tr = RS_ROWS

    def body(sc_ref, own_ref, r_ref, o_ref):
        acc = jnp.zeros((tr, c), F32)
        for j in range(4):
            acc = acc + jnp.where(sc_ref[0] == j, own_ref[...], r_ref[j].astype(F32))
        o_ref[...] = acc

    return pl.pallas_call(
        body, grid_spec=pltpu.PrefetchScalarGridSpec(
            num_scalar_prefetch=1, grid=(r // tr,),
            in_specs=[pl.BlockSpec((tr, c), lambda i, sc: (i, 0)), pl.BlockSpec((4, tr, c), lambda i, sc: (0, i, 0))],
            out_specs=pl.BlockSpec((tr, c), lambda i, sc: (i, 0))),
        out_shape=jax.ShapeDtypeStruct((r, c), F32), name=name,
        compiler_params=_cparams(("arbitrary",)))(my_chip.reshape(1), own, recv)


def _reduce_scatter(tag, tensors, my_c, my_chip):
    ps = [t.reshape((4, 2) + t.shape[1:]) for t in tensors]
    recv = _rs_pair(tag + "_pair", ps)
    sums = [_pair_sum(f"{tag}_pair_sum{t}", ps[t], recv[t], my_c, my_chip) for t in range(len(ps))]
    recv2 = _rs_chips(tag + "_chips", [s_[0] for s_ in sums])
    return [_chip_sum(f"{tag}_chip_sum{t}", sums[t][1], recv2[t], my_chip) for t in range(len(ps))]


HBM_SPEC = pl.BlockSpec(memory_space=pltpu.HBM)
SEM_SPEC = pl.BlockSpec(memory_space=pltpu.SEMAPHORE)
SPLIT_EFFECT = pltpu.SideEffectType.DATAFLOW_SIDE_EFFECTING
ALL_PEERS = (1, 2, 3, 4, 5, 6, 7)
FIRST_LEVEL_PEERS = (1, 4, 2, 6)


def _split_copies(src_ref, land_ref, sems, masks, src_per_peer):
    n = len(masks)
    x, y, c = _place()
    me = 4 * x + 2 * y + c
    copies = []
    for k, mask in enumerate(masks):
        px, py, pc = (1 - x) if mask & 4 else x, (1 - y) if mask & 2 else y, (1 - c) if mask & 1 else c
        src = src_ref.at[4 * px + 2 * py + pc] if src_per_peer else src_ref
        copies.append(pltpu.make_async_remote_copy(src_ref=src, dst_ref=land_ref.at[me], send_sem=sems[k], recv_sem=sems[n + k],
                                                   device_id=(px, py, pc), device_id_type=MESH))
    return copies


def _copies_start(name, src, land, masks, src_per_peer):
    n = len(masks)

    def body(src_ref, land_ref, *rest):
        for cp in _split_copies(src_ref, land_ref, rest[:2 * n], masks, src_per_peer):
            cp.start()
        token = rest[2 * n + 2]
        token[...] = jnp.zeros(token.shape, F32)

    outs = pl.pallas_call(
        body, name=name,
        out_shape=(pltpu.SemaphoreType.DMA(()),) * (2 * n) + (pltpu.HBM(src.shape, src.dtype), pltpu.HBM(land.shape, land.dtype),
                                                             jax.ShapeDtypeStruct((8, 128), F32)),
        in_specs=(HBM_SPEC, HBM_SPEC), out_specs=(SEM_SPEC,) * (2 * n) + (HBM_SPEC, HBM_SPEC, pl.BlockSpec(memory_space=pltpu.VMEM)),
        input_output_aliases={0: 2 * n, 1: 2 * n + 1}, compiler_params=pltpu.CompilerParams(has_side_effects=SPLIT_EFFECT))(
            pltpu.with_memory_space_constraint(src, pltpu.HBM), pltpu.with_memory_space_constraint(land, pltpu.HBM))
    return outs[:2 * n], outs[2 * n], outs[2 * n + 1], outs[2 * n + 2]


def _copies_wait(name, sems, src_thru, land_thru, after, masks, src_per_peer):
    n = len(masks)

    def body(src_ref, land_ref, *rest):
        for cp in _split_copies(src_ref, land_ref, rest[:2 * n], masks, src_per_peer):
            cp.wait_send()
            cp.wait_recv()

    return pl.pallas_call(
        body, name=name, out_shape=(pltpu.HBM(src_thru.shape, src_thru.dtype), pltpu.HBM(land_thru.shape, land_thru.dtype)),
        in_specs=(HBM_SPEC, HBM_SPEC) + (SEM_SPEC,) * (2 * n) + (pl.BlockSpec(memory_space=pl.ANY),), out_specs=(HBM_SPEC, HBM_SPEC),
        input_output_aliases={0: 0, 1: 1}, compiler_params=pltpu.CompilerParams(has_side_effects=SPLIT_EFFECT))(
            src_thru, land_thru, *sems, after)[1]


def _ag_forward(name, g):
    def body(g_ref, o_ref, send_sems, recv_sems):
        x, y, c = _place()
        chips = [(1 - x, y), (x, 1 - y), (1 - x, 1 - y)]
        copies = []
        for j, (px, py) in enumerate(chips):
            cp = pltpu.make_async_remote_copy(src_ref=o_ref.at[4 * px + 2 * py + c], dst_ref=o_ref.at[4 * px + 2 * py + c],
                                              send_sem=send_sems.at[j], recv_sem=recv_sems.at[j], device_id=(x, y, 1 - c), device_id_type=MESH)
            cp.start()
            copies.append(cp)
        for j, (px, py) in enumerate(chips):
            copies[j].wait_send()
            pltpu.make_async_remote_copy(src_ref=o_ref.at[4 * px + 2 * py + 1 - c], dst_ref=o_ref.at[4 * px + 2 * py + 1 - c],
                                         send_sem=send_sems.at[j], recv_sem=recv_sems.at[j], device_id=(x, y, 1 - c),
                                         device_id_type=MESH).wait_recv()

    hbm = pl.BlockSpec(memory_space=pl.ANY)
    return pl.pallas_call(body, out_shape=jax.ShapeDtypeStruct(g.shape, g.dtype), in_specs=[hbm], out_specs=hbm,
                          scratch_shapes=[pltpu.SemaphoreType.DMA((3,)), pltpu.SemaphoreType.DMA((3,))],
                          input_output_aliases={0: 0}, name=name)(g)


def _rs_wire(name, d1, d2):
    _, r, c = d1.shape
    tr = RS_ROWS

    def body(a_ref, b_ref, o_ref):
        o_ref[0] = a_ref[...].astype(MM)
        o_ref[1] = b_ref[...].astype(MM)

    spec = pl.BlockSpec((None, tr, c), lambda j, i: (j, i, 0))
    return pl.pallas_call(body, grid=(N_DEV, r // tr), in_specs=[spec, spec], out_specs=pl.BlockSpec((None, 2, tr, c), lambda j, i: (j, 0, i, 0)),
                          out_shape=jax.ShapeDtypeStruct((N_DEV, 2, r, c), MM), name=name,
                          compiler_params=_cparams(("arbitrary", "arbitrary")))(d1, d2)


def _rs_final(name, d1, d2, recv, me):
    _, r, c = d1.shape
    tr = RS_ROWS

    def body(me_ref, a_ref, b_ref, r_ref, oa_ref, ob_ref):
        for own_ref, o_ref, t in ((a_ref, oa_ref, 0), (b_ref, ob_ref, 1)):
            acc = jnp.zeros((tr, c), F32)
            for j in range(N_DEV):
                acc = acc + jnp.where(me_ref[0] == j, own_ref[...], r_ref[j, t].astype(F32))
            o_ref[...] = acc

    own = pl.BlockSpec((None, tr, c), lambda i, m: (m[0], i, 0))
    out = pl.BlockSpec((tr, c), lambda i, m: (i, 0))
    return pl.pallas_call(
        body, grid_spec=pltpu.PrefetchScalarGridSpec(
            num_scalar_prefetch=1, grid=(r // tr,),
            in_specs=[own, own, pl.BlockSpec((N_DEV, 2, tr, c), lambda i, m: (0, 0, i, 0))], out_specs=[out, out]),
        out_shape=[jax.ShapeDtypeStruct((r, c), F32)] * 2, name=name,
        compiler_params=_cparams(("arbitrary",)))(me.reshape(1), d1, d2, recv)


def _rs_finish(pending, after, me):
    i, sems, wire_thru, land, dw1, dw2 = pending
    recv = _copies_wait(f"rs_wait{i}", sems, wire_thru, land, after, ALL_PEERS, True)
    return _rs_final(f"rs_final{i}", dw1, dw2, recv, me)


def _mod_part(c16, ada_w, ada_b_cols):
    L, D, n = ada_w.shape

    def body(c_ref, w_ref, b_ref, o_ref):
        cv = c_ref[...]
        o_ref[...] = _dot(cv * _sigmoid(cv), w_ref[...]) + b_ref[...]

    return pl.pallas_call(
        body, grid=(L,), in_specs=[pl.BlockSpec((16, D), lambda i: (0, 0)), pl.BlockSpec((None, D, n), lambda i: (i, 0, 0)),
                                   pl.BlockSpec((None, 1, n), lambda i: (i, 0, 0))],
        out_specs=pl.BlockSpec((None, 16, n), lambda i: (i, 0, 0)), out_shape=jax.ShapeDtypeStruct((L, 16, n), F32),
        name="ada_mod", compiler_params=_cparams(("arbitrary",)))(c16, ada_w, ada_b_cols)


def _ada_w_grad(c16, dmod16):
    L, _, n = dmod16.shape
    D = c16.shape[1]

    def body(c_ref, d_ref, o_ref):
        cv = c_ref[...]
        o_ref[...] = _dot_tn(cv * _sigmoid(cv), d_ref[...])

    return pl.pallas_call(
        body, grid=(L,), in_specs=[pl.BlockSpec((16, D), lambda i: (0, 0)), pl.BlockSpec((None, 16, n), lambda i: (i, 0, 0))],
        out_specs=pl.BlockSpec((None, D, n), lambda i: (i, 0, 0)), out_shape=jax.ShapeDtypeStruct((L, D, n), F32),
        name="ada_w_grad", compiler_params=_cparams(("arbitrary",)))(c16, dmod16)


def _sum_devices(name, g):
    _, R, C = g.shape

    def body(g_ref, o_ref):
        acc = g_ref[0]
        for d in range(1, N_DEV):
            acc = acc + g_ref[d]
        o_ref[...] = acc

    return pl.pallas_call(body, out_shape=jax.ShapeDtypeStruct((R, C), F32), name=name)(g)


def _prenorm(name, x, g0, sc, sh, dtype):
    T, D = x.shape

    def body(i, n, rr, cc, oo, aa, ss):
        oo[0][...] = (_rms(rr[0][...], cc[0][...]) * (1.0 + cc[1][...]) + cc[2][...]).astype(dtype)

    return _rows(name, body, T, 512, [(x, 'cur')], [g0, sc, sh], [(D, dtype)])[0]


def _post_bwd(name, dxo, y, g1, gt):
    T, D = y.shape

    def body(i, n, rr, cc, oo, aa, ss):
        d = rr[0][...]
        yv = rr[1][...]
        g1v, gtv = cc[0][...], cc[1][...]
        aa[1][...] += jnp.sum(d * _rms(yv, g1v), axis=0, keepdims=True)
        dy, dg1 = _rms_bwd(yv, g1v, d * gtv)
        aa[0][...] += dg1
        aa[2][...] += jnp.sum(dy, axis=0, keepdims=True)
        oo[0][...] = dy.astype(MM)

    return _rows(name, body, T, 512, [(dxo, 'cur'), (y, 'cur')], [g1, gt], [(D, MM)], accs=[(1, D)] * 3)


def _mm_post(name, a, w, bias, x, g1, gt):
    T, D = x.shape
    consts = [w, g1, gt] + ([bias] if bias is not None else [])

    def body(i, n, rr, cc, oo, aa, ss):
        y = _dot(rr[0][...], cc[0][...])
        if bias is not None:
            y = y + cc[3][...]
        oo[0][...] = y
        oo[1][...] = rr[1][...] + cc[2][...] * _rms(y, cc[1][...])

    return _rows(name, body, T, 512, [(a, 'cur'), (x, 'cur')], consts, [(D, F32), (D, F32)])


def _mm_nt_rows(name, a, w):
    T = a.shape[0]
    K = w.shape[0]

    def body(i, n, rr, cc, oo, aa, ss):
        oo[0][...] = _dot_nt(rr[0][...], cc[0][...]).astype(MM)

    return _rows(name, body, T, 512, [(a, 'cur')], [w], [(K, MM)])[0]


def _mm_tn(name, a, b, sqrelu=False, col_shards=0, diag=0):
    T, M = a.shape
    N = b.shape[1]
    tk = min(512, T)
    nk = T // tk
    if diag:
        bm, bn = M // diag, N // diag
        grid = (diag, 1, nk)
        a_spec = pl.BlockSpec((tk, bm), lambda g, n, k: (k, g))
        b_spec = pl.BlockSpec((tk, bn), lambda g, n, k: (k, g))
        o_spec = pl.BlockSpec((None, bm, bn), lambda g, n, k: (g, 0, 0))
        o_shape = (diag, bm, bn)
    else:
        bm = min(M, 1024)
        bn = N // col_shards if col_shards else min(N, 1024)
        grid = (M // bm, N // bn, nk)
        a_spec = pl.BlockSpec((tk, bm), lambda m, n, k: (k, m))
        b_spec = pl.BlockSpec((tk, bn), lambda m, n, k: (k, n))
        if col_shards:
            o_spec = pl.BlockSpec((None, bm, bn), lambda m, n, k: (n, m, 0))
            o_shape = (col_shards, M, bn)
        else:
            o_spec = pl.BlockSpec((bm, bn), lambda m, n, k: (m, n))
            o_shape = (M, N)

    def body(a_ref, b_ref, o_ref):
        @pl.when(pl.program_id(2) == 0)
        def _():
            o_ref[...] = jnp.zeros(o_ref.shape, F32)

        av = a_ref[...]
        if sqrelu:
            r = jnp.maximum(av.astype(F32), 0.0)
            av = r * r
        o_ref[...] += _dot_tn(av, b_ref[...])

    return pl.pallas_call(body, grid=grid, in_specs=[a_spec, b_spec], out_specs=o_spec,
                          out_shape=jax.ShapeDtypeStruct(o_shape, F32), name=name,
                          compiler_params=_cparams(("arbitrary", "arbitrary", "arbitrary")))(a, b)


def _ffn_fwd(name, li, h, w1g, w2g, x, g1, gt):
    T, D = h.shape
    nf, tf = w1g.shape[0], w1g.shape[-1]
    F = nf * tf
    tm = min(512, T)

    def body(h_ref, w1_ref, w2_ref, x_ref, g1_ref, gt_ref, a_ref, y_ref, xo_ref, acc):
        f = pl.program_id(1)

        @pl.when(f == 0)
        def _():
            acc[...] = jnp.zeros(acc.shape, F32)

        a = _dot(h_ref[...], w1_ref[...])
        a_ref[...] = a.astype(MM)
        r = jnp.maximum(a, 0.0)
        acc[...] += _dot(r * r, w2_ref[...])

        @pl.when(f == nf - 1)
        def _():
            y = acc[...]
            y_ref[...] = y
            xo_ref[...] = x_ref[...] + gt_ref[...] * _rms(y, g1_ref[...])

    row = lambda t, f: (t, 0)
    one = lambda t, f: (0, 0)
    return pl.pallas_call(
        body, grid=(T // tm, nf),
        in_specs=[pl.BlockSpec((tm, D), row), pl.BlockSpec((None, None, None, D, tf), lambda t, f: (f, li, 0, 0, 0)),
                  pl.BlockSpec((None, None, None, tf, D), lambda t, f: (f, li, 1, 0, 0)),
                  pl.BlockSpec((tm, D), row), pl.BlockSpec((1, D), one), pl.BlockSpec((1, D), one)],
        out_specs=[pl.BlockSpec((tm, tf), lambda t, f: (t, f)), pl.BlockSpec((tm, D), row), pl.BlockSpec((tm, D), row)],
        out_shape=[jax.ShapeDtypeStruct((T, F), MM), jax.ShapeDtypeStruct((T, D), F32), jax.ShapeDtypeStruct((T, D), F32)],
        scratch_shapes=[pltpu.VMEM((tm, D), F32)], name=name,
        compiler_params=_cparams(("arbitrary", "arbitrary")))(h, w1g, w2g, x, g1, gt)


def _ffn_bwd(name, li, dy, a, w1g, w2g, x, dxo, g0, sc):
    T, D = x.shape
    nf, tf = w1g.shape[0], w1g.shape[-1]
    F = nf * tf
    tm = min(512, T)

    def body(dy_ref, a_ref, w1_ref, w2_ref, x_ref, dxo_ref, g0_ref, sc_ref, da_ref, dx_ref, dsh_ref, dsc_ref, dg0_ref, acc):
        t, f = pl.program_id(0), pl.program_id(1)

        @pl.when((t == 0) & (f == 0))
        def _():
            for r in (dsh_ref, dsc_ref, dg0_ref):
                r[...] = jnp.zeros(r.shape, F32)

        @pl.when(f == 0)
        def _():
            acc[...] = jnp.zeros(acc.shape, F32)

        du = _dot_nt(dy_ref[...], w2_ref[...])
        da = (du * (2.0 * jnp.maximum(a_ref[...].astype(F32), 0.0))).astype(MM)
        da_ref[...] = da
        acc[...] += _dot_nt(da, w1_ref[...])

        @pl.when(f == nf - 1)
        def _():
            dx, dsh, dsc, dg0 = _prenorm_bwd(x_ref[...], g0_ref[...], sc_ref[...], acc[...])
            dx_ref[...] = dxo_ref[...] + dx
            dsh_ref[...] += dsh
            dsc_ref[...] += dsc
            dg0_ref[...] += dg0

    row = lambda t, f: (t, 0)
    one = lambda t, f: (0, 0)
    blk = lambda t, f: (t, f)
    return pl.pallas_call(
        body, grid=(T // tm, nf),
        in_specs=[pl.BlockSpec((tm, D), row), pl.BlockSpec((tm, tf), blk), pl.BlockSpec((None, None, None, D, tf), lambda t, f: (f, li, 0, 0, 0)),
                  pl.BlockSpec((None, None, None, tf, D), lambda t, f: (f, li, 1, 0, 0)), pl.BlockSpec((tm, D), row), pl.BlockSpec((tm, D), row),
                  pl.BlockSpec((1, D), one), pl.BlockSpec((1, D), one)],
        out_specs=[pl.BlockSpec((tm, tf), blk), pl.BlockSpec((tm, D), row)] + [pl.BlockSpec((1, D), one)] * 3,
        out_shape=[jax.ShapeDtypeStruct((T, F), MM), jax.ShapeDtypeStruct((T, D), F32)] + [jax.ShapeDtypeStruct((1, D), F32)] * 3,
        scratch_shapes=[pltpu.VMEM((tm, D), F32)], name=name,
        compiler_params=_cparams(("arbitrary", "arbitrary")))(dy, a, w1g, w2g, x, dxo, g0, sc)


def _rope_tables(pos, invf):
    T = pos.shape[0]

    def body(i, n, rr, cc, oo, aa, ss):
        ang = rr[0][...] * cc[0][...]
        lane = lax.broadcasted_iota(jnp.int32, ang.shape, 1)
        cs, sn = jnp.cos(ang), jnp.sin(ang)
        oo[0][...] = jnp.where((lane >= QK_NOPE) & (lane < QK_NOPE + QK_ROPE), cs, 1.0)
        oo[1][...] = jnp.where((lane >= QK_NOPE) & (lane < QK_NOPE + QK_ROPE // 2), -sn, 0.0)
        oo[2][...] = jnp.where((lane >= QK_NOPE + QK_ROPE // 2) & (lane < QK_NOPE + QK_ROPE), sn, 0.0)

    return _rows("rope_tables", body, T, 512, [(pos, 'cur')], [invf], [(HEAD_PAD, F32)] * 3)


def _rope(v, C, S1, S2):
    n = v.shape[1]
    reps = n // HEAD_PAD
    if reps > 1:
        C, S1, S2 = (jnp.tile(t, (1, reps)) for t in (C, S1, S2))
    return v * C + pltpu.roll(v, n - QK_ROPE // 2, 1) * S1 + pltpu.roll(v, QK_ROPE // 2, 1) * S2


def _unrope(d, C, S1, S2):
    n = d.shape[1]
    reps = n // HEAD_PAD
    if reps > 1:
        C, S1, S2 = (jnp.tile(t, (1, reps)) for t in (C, S1, S2))
    return d * C + pltpu.roll(d * S1, QK_ROPE // 2, 1) + pltpu.roll(d * S2, n - QK_ROPE // 2, 1)


def _mla_proj(name, h, C, S1, S2, w_dq, qg, w_uq, w_dkv, kvg, w_ukv_k, w_ukv_v):
    T = h.shape[0]
    HP = N_HEADS * HEAD_PAD

    def body(i, n, rr, cc, oo, aa, ss):
        hv = rr[0][...]
        Cv, S1v, S2v = rr[1][...], rr[2][...], rr[3][...]
        cq_raw = _dot(hv, cc[0][...])
        cq = _rms(cq_raw, cc[1][...]).astype(MM)
        q = _rope(_dot(cq, cc[2][...]), Cv, S1v, S2v)
        ckv_all = _dot(hv, cc[3][...])
        ckv_raw = ckv_all[:, :KV_LORA]
        ckv = _rms(ckv_raw, cc[4][...]).astype(MM)
        kr = _rope(ckv_all[:, KV_LORA:], Cv, S1v, S2v)
        k = _dot(ckv, cc[5][...]) + jnp.tile(kr, (1, N_HEADS))
        v = _dot(ckv, cc[6][...])
        v = jnp.where(lax.broadcasted_iota(jnp.int32, v.shape, 1) % HEAD_PAD == V_HEAD, 1.0, v)
        oo[0][...] = cq_raw
        oo[1][...] = cq
        oo[2][...] = ckv_raw
        oo[3][...] = ckv
        oo[4][...] = q.astype(MM)
        oo[5][...] = k.astype(MM)
        oo[6][...] = v.astype(MM)

    return _rows(name, body, T, 256, [(h, 'cur'), (C, 'cur'), (S1, 'cur'), (S2, 'cur')],
                 [w_dq, qg, w_uq, w_dkv, kvg, w_ukv_k, w_ukv_v],
                 [(Q_LORA, F32), (Q_LORA, MM), (KV_LORA, F32), (KV_LORA, MM), (HP, MM), (HP, MM), (HP, MM)])


ATT_HEADS = 4
ATT_BLOCK = 512


def _chunk_mask_t(tk, tq):
    ki = lax.broadcasted_iota(jnp.int32, (tk, tq), 0) // CHUNK
    qi = lax.broadcasted_iota(jnp.int32, (tk, tq), 1) // CHUNK
    return ki <= qi


def _attn_fwd(name, q, k, v):
    T = q.shape[0]
    tb = min(ATT_BLOCK, T)
    nb = T // tb
    nh = ATT_HEADS
    hs = [slice(h * HEAD_PAD, (h + 1) * HEAD_PAD) for h in range(nh)]

    def body(q_ref, k_ref, v_ref, o_ref, lse_ref):
        qb = pl.program_id(1)

        def k_block(k0, masked, st):
            new = []
            for h in range(nh):
                m, acc = st[h]
                s = _dot_nt(k_ref[pl.ds(k0, tb), hs[h]], q_ref[:, hs[h]])
                if masked:
                    s = jnp.where(_chunk_mask_t(tb, tb), s, NEG)
                m_new = jnp.maximum(m, jnp.max(s, axis=0, keepdims=True))
                alpha = jnp.exp((m - m_new) * ATT_SCALE)
                p = jnp.exp((s - m_new) * ATT_SCALE)
                acc = alpha * acc + _dot_tn(v_ref[pl.ds(k0, tb), hs[h]], p)
                new.append((m_new, acc))
            return tuple(new)

        st = tuple((jnp.full((1, tb), NEG, F32), jnp.zeros((HEAD_PAD, tb), F32)) for _ in range(nh))
        st = k_block(pl.multiple_of(qb * tb, tb), True, st)
        st = lax.fori_loop(0, qb, lambda kb, s_: k_block(pl.multiple_of(kb * tb, tb), False, s_), st)
        for h in range(nh):
            m, acc = st[h]
            l = acc[V_HEAD:V_HEAD + 1, :]
            o_ref[:, hs[h]] = (acc / l).T.astype(MM)
            lse_ref[h] = jnp.broadcast_to(m * ATT_SCALE + jnp.log(l), (8, tb))

    blk = pl.BlockSpec((tb, nh * HEAD_PAD), lambda g, i: (i, g))
    res = pl.BlockSpec((T, nh * HEAD_PAD), lambda g, i: (0, g))
    return pl.pallas_call(
        body, grid=(N_HEADS // nh, nb), in_specs=[blk, res, res],
        out_specs=[blk, pl.BlockSpec((nh, 8, tb), lambda g, i: (g, 0, i))],
        out_shape=[jax.ShapeDtypeStruct(q.shape, MM), jax.ShapeDtypeStruct((N_HEADS, 8, T), F32)], name=name,
        compiler_params=_cparams(("arbitrary", "arbitrary")))(q, k, v)


def _attn_delta(name, do, o):
    T = do.shape[0]
    tb = min(256, T)

    def body(do_ref, o_ref, d_ref):
        lane = lax.broadcasted_iota(jnp.int32, (tb, HEAD_PAD), 1) // 8
        cols = jnp.zeros((tb, HEAD_PAD), F32)
        for h in range(N_HEADS):
            hsl = slice(h * HEAD_PAD, (h + 1) * HEAD_PAD)
            r = jnp.sum(do_ref[:, hsl].astype(F32) * o_ref[:, hsl].astype(F32), axis=1, keepdims=True)
            cols = jnp.where(lane == h, r, cols)
        d_ref[...] = cols.T

    spec = pl.BlockSpec((tb, N_HEADS * HEAD_PAD), lambda i: (i, 0))
    out = pl.pallas_call(body, grid=(T // tb,), in_specs=[spec, spec], out_specs=pl.BlockSpec((HEAD_PAD, tb), lambda i: (0, i)),
                         out_shape=jax.ShapeDtypeStruct((HEAD_PAD, T), F32), name=name, compiler_params=_cparams(("arbitrary",)))(do, o)
    return out.reshape(N_HEADS, 8, T)


def _attn_bwd(name, q, k, v, do, lse, delta):
    T = q.shape[0]
    tb = min(ATT_BLOCK, T)
    nb = T // tb
    nh = ATT_HEADS
    hs = [slice(h * HEAD_PAD, (h + 1) * HEAD_PAD) for h in range(nh)]

    def body(q_ref, k_ref, v_ref, do_ref, lse_ref, dl_ref, dq_ref, dk_ref, dv_ref, dq_acc, dk_acc, dv_acc):
        kb = pl.program_id(1)

        @pl.when(kb == 0)
        def _():
            dq_acc[...] = jnp.zeros(dq_acc.shape, F32)

        dk_acc[...] = jnp.zeros(dk_acc.shape, F32)
        dv_acc[...] = jnp.zeros(dv_acc.shape, F32)

        def q_block(q0, masked):
            for h in range(nh):
                qh = q_ref[pl.ds(q0, tb), hs[h]]
                doh = do_ref[pl.ds(q0, tb), hs[h]]
                kh = k_ref[:, hs[h]]
                s = _dot_nt(kh, qh) * ATT_SCALE
                if masked:
                    s = jnp.where(_chunk_mask_t(tb, tb), s, NEG)
                p = jnp.exp(s - lse_ref[h, 0:1, pl.ds(q0, tb)])
                ds = (p * (_dot_nt(v_ref[:, hs[h]], doh) - dl_ref[h, 0:1, pl.ds(q0, tb)]) * ATT_SCALE).astype(MM)
                dv_acc[:, hs[h]] += _dot(p, doh)
                dk_acc[:, hs[h]] += _dot(ds, qh)
                dq_acc[pl.ds(q0, tb), hs[h]] += _dot_tn(ds, kh)

        q_block(pl.multiple_of(kb * tb, tb), True)

        def rest(qb, c_):
            q_block(pl.multiple_of(qb * tb, tb), False)
            return c_

        lax.fori_loop(kb + 1, nb, rest, 0)
        dk_ref[...] = dk_acc[...].astype(MM)
        dv_ref[...] = dv_acc[...].astype(MM)

        @pl.when(kb == nb - 1)
        def _():
            dq_ref[...] = dq_acc[...].astype(MM)

    W = nh * HEAD_PAD
    blk = pl.BlockSpec((tb, W), lambda g, i: (i, g))
    res = pl.BlockSpec((T, W), lambda g, i: (0, g))
    rows = pl.BlockSpec((nh, 8, T), lambda g, i: (g, 0, 0))
    return pl.pallas_call(
        body, grid=(N_HEADS // nh, nb), in_specs=[res, blk, blk, res, rows, rows], out_specs=[res, blk, blk],
        out_shape=[jax.ShapeDtypeStruct(q.shape, MM)] * 3,
        scratch_shapes=[pltpu.VMEM((T, W), F32), pltpu.VMEM((tb, W), F32), pltpu.VMEM((tb, W), F32)],
        name=name, compiler_params=_cparams(("arbitrary", "arbitrary")))(q, k, v, do, lse, delta)


def _mla_proj_bwd(name, dq, dk, dv, C, S1, S2, cq_raw, ckv_raw, x, dxo, w_uq, w_ukv_k, w_ukv_v, w_dq, w_dkv, qg, kvg, g0, sc):
    T, D = x.shape
    HP = N_HEADS * HEAD_PAD

    def body(i, n, rr, cc, oo, aa, ss):
        Cv, S1v, S2v = rr[3][...], rr[4][...], rr[5][...]
        dq_pre = _unrope(rr[0][...].astype(F32), Cv, S1v, S2v).astype(MM)
        oo[0][...] = dq_pre
        dcq = _dot_nt(dq_pre, cc[0][...])
        dcq_raw, dqg = _rms_bwd(rr[6][...], cc[5][...], dcq)
        aa[0][...] += dqg
        dcq_raw = dcq_raw.astype(MM)
        oo[1][...] = dcq_raw
        dkv = rr[1][...]
        dkr = dkv[:, :HEAD_PAD].astype(F32)
        for hh in range(1, N_HEADS):
            dkr = dkr + dkv[:, hh * HEAD_PAD:(hh + 1) * HEAD_PAD].astype(F32)
        lane = lax.broadcasted_iota(jnp.int32, dkr.shape, 1)
        dkr = jnp.where((lane >= QK_NOPE) & (lane < QK_NOPE + QK_ROPE), _unrope(dkr, Cv, S1v, S2v), 0.0)
        dckv = _dot_nt(dkv, cc[1][...]) + _dot_nt(rr[2][...], cc[2][...])
        dckv_raw, dkvg = _rms_bwd(rr[7][...], cc[6][...], dckv)
        aa[1][...] += dkvg
        dckv_all = jnp.concatenate([dckv_raw, dkr], axis=1).astype(MM)
        oo[2][...] = dckv_all
        dh = _dot_nt(dcq_raw, cc[3][...]) + _dot_nt(dckv_all, cc[4][...])
        dx, dsh, dsc, dg0 = _prenorm_bwd(rr[8][...], cc[7][...], cc[8][...], dh)
        oo[3][...] = rr[9][...] + dx
        aa[2][...] += dsh
        aa[3][...] += dsc
        aa[4][...] += dg0

    return _rows(name, body, T, 256,
                 [(dq, 'cur'), (dk, 'cur'), (dv, 'cur'), (C, 'cur'), (S1, 'cur'), (S2, 'cur'), (cq_raw, 'cur'), (ckv_raw, 'cur'),
                  (x, 'cur'), (dxo, 'cur')],
                 [w_uq, w_ukv_k, w_ukv_v, w_dq, w_dkv, qg, kvg, g0, sc],
                 [(HP, MM), (Q_LORA, MM), (KV_LORA + HEAD_PAD, MM), (D, F32)],
                 accs=[(1, Q_LORA), (1, KV_LORA), (1, D), (1, D), (1, D)])


HALO = 32


def _conv_glu(name, h, w_pw1, b_pw1):
    T, D = h.shape

    def body(i, n, rr, cc, oo, aa, ss):
        a = _dot(rr[0][...], cc[0][...]) + cc[1][...]
        oo[0][...] = a
        oo[1][...] = a[:, :D] * _sigmoid(a[:, D:])

    return _rows(name, body, T, 512, [(h, 'cur')], [w_pw1, b_pw1], [(2 * D, F32), (D, F32)])


def _layernorm_parts(uc):
    xc = uc - jnp.mean(uc, axis=-1, keepdims=True)
    r = lax.rsqrt(jnp.mean(xc * xc, axis=-1, keepdims=True) + EPS)
    return xc * r, r


def _conv_dw(name, u, w_dw, b_dw, ln_g, ln_b, w_pw2, b_pw2, x, g1, gt):
    T, D = u.shape
    tm = min(256, T)

    def body(i, n, rr, cc, oo, aa, ss):
        ext = ss[0]
        ext[0:HALO, :] = jnp.where(i > 0, rr[1][tm - HALO:tm, :], 0.0)
        ext[HALO:HALO + tm, :] = rr[0][...]
        uc = jnp.zeros((tm, D), F32) + cc[1][...]
        for kk in range(CONV_W):
            uc = uc + ext[pl.ds(HALO - (CONV_W - 1) + kk, tm), :] * cc[0][kk:kk + 1, :]
        xh, _ = _layernorm_parts(uc)
        ln = xh * cc[2][...] + cc[3][...]
        z = (ln * _sigmoid(ln)).astype(MM)
        y = _dot(z, cc[4][...]) + cc[5][...]
        oo[0][...] = uc
        oo[1][...] = z
        oo[2][...] = y
        oo[3][...] = rr[2][...] + cc[7][...] * _rms(y, cc[6][...])

    return _rows(name, body, T, tm, [(u, 'cur'), (u, 'prev'), (x, 'cur')], [w_dw, b_dw, ln_g, ln_b, w_pw2, b_pw2, g1, gt],
                 [(D, F32), (D, MM), (D, F32), (D, F32)], scratch=[pltpu.VMEM((tm + HALO, D), F32)])


def _conv_bwd1(name, dy, uc, w_pw2, ln_g, ln_b):
    T, D = uc.shape

    def body(i, n, rr, cc, oo, aa, ss):
        dz = _dot_nt(rr[0][...], cc[0][...])
        xh, r = _layernorm_parts(rr[1][...])
        g = cc[1][...]
        ln = xh * g + cc[2][...]
        sg = _sigmoid(ln)
        dln = dz * (sg * (1.0 + ln * (1.0 - sg)))
        aa[0][...] += jnp.sum(dln * xh, axis=0, keepdims=True)
        aa[1][...] += jnp.sum(dln, axis=0, keepdims=True)
        dxh = dln * g
        duc = r * (dxh - jnp.mean(dxh, axis=-1, keepdims=True) - xh * jnp.mean(dxh * xh, axis=-1, keepdims=True))
        aa[2][...] += jnp.sum(duc, axis=0, keepdims=True)
        oo[0][...] = duc

    return _rows(name, body, T, 256, [(dy, 'cur'), (uc, 'cur')], [w_pw2, ln_g, ln_b], [(D, F32)], accs=[(1, D)] * 3)


def _conv_bwd2(name, duc, u, a, x, dxo, w_dw, w_pw1, g0, sc):
    T, D = u.shape
    tm = min(256, T)

    def body(i, n, rr, cc, oo, aa, ss):
        extd, extu = ss[0], ss[1]
        dcur = rr[0][...]
        extd[0:tm, :] = dcur
        extd[tm:tm + HALO, :] = jnp.where(i < n - 1, rr[1][0:HALO, :], 0.0)
        extu[0:HALO, :] = jnp.where(i > 0, rr[3][tm - HALO:tm, :], 0.0)
        extu[HALO:HALO + tm, :] = rr[2][...]
        du = jnp.zeros((tm, D), F32)
        for kk in range(CONV_W):
            du = du + extd[pl.ds(CONV_W - 1 - kk, tm), :] * cc[0][kk:kk + 1, :]
            aa[0][kk:kk + 1, :] += jnp.sum(dcur * extu[pl.ds(HALO - (CONV_W - 1) + kk, tm), :], axis=0, keepdims=True)
        av = rr[4][...]
        a1, sg = av[:, :D], _sigmoid(av[:, D:])
        da = jnp.concatenate([du * sg, du * a1 * (sg * (1.0 - sg))], axis=1)
        aa[1][...] += jnp.sum(da, axis=0, keepdims=True)
        da = da.astype(MM)
        oo[0][...] = da
        dx, dsh, dsc, dg0 = _prenorm_bwd(rr[5][...], cc[2][...], cc[3][...], _dot_nt(da, cc[1][...]))
        oo[1][...] = rr[6][...] + dx
        aa[2][...] += dsh
        aa[3][...] += dsc
        aa[4][...] += dg0

    return _rows(name, body, T, tm,
                 [(duc, 'cur'), (duc, 'next'), (u, 'cur'), (u, 'prev'), (a, 'cur'), (x, 'cur'), (dxo, 'cur')],
                 [w_dw, w_pw1, g0, sc], [(2 * D, MM), (D, F32)],
                 accs=[(32, D), (1, 2 * D), (1, D), (1, D), (1, D)],
                 scratch=[pltpu.VMEM((tm + HALO, D), F32), pltpu.VMEM((tm + HALO, D), F32)])


PHALO = 16


def _pool_fwd(name, h, w, b, scale, x, g1, gt):
    T, D = h.shape
    G = len(POOL_WINDOWS)
    Cg = D // G
    tm = min(256, T)

    def body(i, n, rr, cc, oo, aa, ss):
        ext = ss[0]
        ext[0:PHALO, :] = jnp.where(i > 0, rr[1][tm - PHALO:tm, :], 0.0)
        ext[PHALO:PHALO + tm, :] = rr[0][...]
        t_glob = i * tm + lax.broadcasted_iota(jnp.int32, (tm, 1), 0)
        ps, ys = [], []
        for g, win in enumerate(POOL_WINDOWS):
            cols = slice(g * Cg, (g + 1) * Cg)
            s = ext[pl.ds(PHALO, tm), cols]
            for j in range(1, win):
                s = s + ext[pl.ds(PHALO - j, tm), cols]
            cnt = jnp.minimum(t_glob + 1, win).astype(F32)
            p = (s / cnt - ext[pl.ds(PHALO, tm), cols]).astype(MM)
            ps.append(p)
            ys.append(_dot(p, cc[0][g]) + cc[1][:, cols])
        ypre = jnp.concatenate(ys, axis=1)
        y = ypre * cc[2][...]
        oo[0][...] = jnp.concatenate(ps, axis=1)
        oo[1][...] = ypre
        oo[2][...] = y
        oo[3][...] = rr[2][...] + cc[4][...] * _rms(y, cc[3][...])

    return _rows(name, body, T, tm, [(h, 'cur'), (h, 'prev'), (x, 'cur')], [w, b, scale, g1, gt],
                 [(D, MM), (D, F32), (D, F32), (D, F32)], scratch=[pltpu.VMEM((tm + PHALO, D), F32)])


def _pool_bwd1(name, dy, ypre, scale, w):
    T, D = ypre.shape
    G = len(POOL_WINDOWS)
    Cg = D // G

    def body(i, n, rr, cc, oo, aa, ss):
        dyv = rr[0][...].astype(F32)
        aa[0][...] += jnp.sum(dyv * rr[1][...], axis=0, keepdims=True)
        dypre = dyv * cc[0][...]
        aa[1][...] += jnp.sum(dypre, axis=0, keepdims=True)
        dypre = dypre.astype(MM)
        oo[1][...] = dypre
        oo[0][...] = jnp.concatenate([_dot_nt(dypre[:, g * Cg:(g + 1) * Cg], cc[1][g]) for g in range(G)], axis=1)

    return _rows(name, body, T, 256, [(dy, 'cur'), (ypre, 'cur')], [scale, w], [(D, F32), (D, MM)], accs=[(1, D)] * 2)


def _pool_bwd2(name, dp, x, dxo, g0, sc):
    T, D = x.shape
    G = len(POOL_WINDOWS)
    Cg = D // G
    tm = min(256, T)

    def body(i, n, rr, cc, oo, aa, ss):
        ext = ss[0]
        t_glob = i * tm + lax.broadcasted_iota(jnp.int32, (tm, 1), 0)
        dcur = rr[0][...]
        dhs = []
        for g, win in enumerate(POOL_WINDOWS):
            cols = slice(g * Cg, (g + 1) * Cg)
            cnt = jnp.minimum(t_glob + 1, win).astype(F32)
            ext[0:tm, cols] = dcur[:, cols] / cnt
            ext[tm:tm + PHALO, cols] = jnp.where(i < n - 1, rr[1][0:PHALO, cols] * (1.0 / win), 0.0)
        for g, win in enumerate(POOL_WINDOWS):
            cols = slice(g * Cg, (g + 1) * Cg)
            s = ext[pl.ds(0, tm), cols]
            for j in range(1, win):
                s = s + ext[pl.ds(j, tm), cols]
            dhs.append(s - dcur[:, cols])
        dx, dsh, dsc, dg0 = _prenorm_bwd(rr[2][...], cc[0][...], cc[1][...], jnp.concatenate(dhs, axis=1))
        oo[0][...] = rr[3][...] + dx
        aa[0][...] += dsh
        aa[1][...] += dsc
        aa[2][...] += dg0

    return _rows(name, body, T, tm, [(dp, 'cur'), (dp, 'next'), (x, 'cur'), (dxo, 'cur')], [g0, sc], [(D, F32)],
                 accs=[(1, D)] * 3, scratch=[pltpu.VMEM((tm + PHALO, D), F32)])


def _loss_head(x, tgt):
    T, D = x.shape

    def body(i, n, rr, cc, oo, aa, ss):
        err = rr[0][...] - rr[1][...]
        oo[0][...] = err * (1.0 / D)
        aa[0][...] += jnp.sum(err * err, axis=0, keepdims=True)

        @pl.when(i == n - 1)
        def _():
            aa[1][...] = jnp.broadcast_to(jnp.sum(aa[0][...], axis=1, keepdims=True) * (0.5 / D), (1, 128))

    dx, _, loss_row = _rows("loss_head", body, T, 512, [(x, 'cur'), (tgt, 'cur')], [], [(D, F32)], accs=[(1, D), (1, 128)])
    return dx, loss_row


def _adamw(name, w, g, m, v):
    shape = w.shape
    C = shape[-1]
    R = w.size // C
    w2, g2, m2, v2 = (t.reshape(R, C) for t in (w, g, m, v))
    br = R
    if R * C * 4 > (1 << 20):
        br = 8
        while br * 2 * C * 4 <= (1 << 20) and R % (br * 2) == 0:
            br *= 2
    b1c = 1.0 - ADAM_B1 ** ADAM_STEP
    b2c = 1.0 - ADAM_B2 ** ADAM_STEP

    def body(w_ref, g_ref, m_ref, v_ref, d_ref, mo_ref, vo_ref):
        gv = g_ref[...]
        mn = ADAM_B1 * m_ref[...] + (1.0 - ADAM_B1) * gv
        vn = ADAM_B2 * v_ref[...] + (1.0 - ADAM_B2) * (gv * gv)
        d_ref[...] = -ADAM_LR * ((mn / b1c) / (jnp.sqrt(vn / b2c) + ADAM_EPS) + ADAM_WD * w_ref[...])
        mo_ref[...] = mn
        vo_ref[...] = vn

    spec = pl.BlockSpec((br, C), lambda r: (r, 0))
    outs = pl.pallas_call(body, grid=(R // br,), in_specs=[spec] * 4, out_specs=[spec] * 3,
                          out_shape=[jax.ShapeDtypeStruct((R, C), F32)] * 3, name=name,
                          compiler_params=_cparams(("arbitrary",)))(w2, g2, m2, v2)
    return tuple(t.reshape(shape) for t in outs)


def _to_shards(full, ax):
    s = full.shape
    return jnp.moveaxis(full.reshape(s[:ax] + (N_DEV, s[ax] // N_DEV) + s[ax + 1:]), ax, 0)


def _unshard(g, ax):
    r = jnp.moveaxis(g, 0, ax)
    s = r.shape
    return r.reshape(s[:ax] + (s[ax] * s[ax + 1],) + s[ax + 2:])


def _pack(parts, dtype, row_mult):
    lead = parts[0].shape[:-1]
    flat = jnp.concatenate([p.astype(dtype) for p in parts], axis=-1)
    n = flat.shape[-1]
    per = row_mult * 1024
    tot = -(-n // per) * per
    flat = jnp.pad(flat, [(0, 0)] * len(lead) + [(0, tot - n)])
    return flat.reshape(lead + (tot // 1024, 1024))


def _pad_heads(w, lo, hi):
    K = w.shape[0]
    r = w.reshape(K, N_HEADS, -1)[:, :, lo:hi]
    return jnp.pad(r, ((0, 0), (0, 0), (0, HEAD_PAD - (hi - lo)))).reshape(K, N_HEADS * HEAD_PAD)


def kernel(x, c, positions, ada_w, ada_b, norm_g, mla_w_dq, mla_q_norm_g, mla_w_uq, mla_w_dkv, mla_kv_norm_g, mla_w_ukv, mla_w_o, conv_w_pw1, conv_b_pw1, conv_w_dw, conv_b_dw, conv_ln_g, conv_ln_b, conv_w_pw2, conv_b_pw2, pool_w, pool_b, pool_scale, ffn_w1, ffn_w2, loss_target, m_ada_w, m_ada_b, m_norm_g, m_mla_w_dq, m_mla_q_norm_g, m_mla_w_uq, m_mla_w_dkv, m_mla_kv_norm_g, m_mla_w_ukv, m_mla_w_o, m_conv_w_pw1, m_conv_b_pw1, m_conv_w_dw, m_conv_b_dw, m_conv_ln_g, m_conv_ln_b, m_conv_w_pw2, m_conv_b_pw2, m_pool_w, m_pool_b, m_pool_scale, m_ffn_w1, m_ffn_w2, v_ada_w, v_ada_b, v_norm_g, v_mla_w_dq, v_mla_q_norm_g, v_mla_w_uq, v_mla_w_dkv, v_mla_kv_norm_g, v_mla_w_ukv, v_mla_w_o, v_conv_w_pw1, v_conv_b_pw1, v_conv_w_dw, v_conv_b_dw, v_conv_ln_g, v_conv_ln_b, v_conv_w_pw2, v_conv_b_pw2, v_pool_w, v_pool_b, v_pool_scale, v_ffn_w1, v_ffn_w2):
    args = dict(locals())
    W = {n: args[n] for n, _ in WEIGHTS}
    M1 = {n: args['m_' + n] for n, _ in WEIGHTS}
    V2 = {n: args['v_' + n] for n, _ in WEIGHTS}
    D = D_MODEL
    T = x.shape[1]
    L = ffn_w1.shape[0]
    xi, yi, ci = _place()
    me = 4 * xi + 2 * yi + ci
    n_ada = ada_w.shape[2]

    small_sizes = [W[n].size for n in SMALL]
    small_in = _pack([c.reshape(-1)] + [W[n].reshape(-1) for n in SMALL], F32, 8)
    small_all = _ag_small("ag_small_params", small_in).reshape(N_DEV, -1)
    c_all = small_all[:, :D]
    Ws = {}
    off = D
    for n, sz in zip(SMALL, small_sizes):
        Ws[n] = _unshard(small_all[:, off:off + sz].reshape((N_DEV,) + W[n].shape), SHARD_AXIS[n])
        off += sz
    c16 = jnp.pad(c_all, ((0, 16 - N_DEV), (0, 0)))

    ada_b_cols = lax.dynamic_slice_in_dim(ada_b, me * n_ada, n_ada, axis=1).reshape(L, 1, n_ada)
    mod_part = _mod_part(c16, ada_w, ada_b_cols)[:, :N_DEV]
    mod_all = _ag_small("ag_mod", mod_part.reshape(L * N_DEV, n_ada)).reshape(N_DEV, L, N_DEV, n_ada)
    mod_mine = lax.dynamic_index_in_dim(mod_all, me, axis=2, keepdims=False)
    mod = jnp.transpose(mod_mine, (1, 0, 2)).reshape(L, 6, 1, D)

    rest = [n for n in BIG if not n.startswith('ffn')]
    rest_all, = _ag_big("ag_weights", [_pack([W[n].reshape(-1) for n in rest], MM, 32)])
    wf = jnp.stack([ffn_w1.astype(MM), ffn_w2.astype(MM).reshape(ffn_w1.shape)], axis=1)
    wf, rest_all = lax.optimization_barrier((wf, rest_all))
    wf_land = lax.dynamic_update_slice(lax.empty((N_DEV,) + wf.shape, MM), wf[None], (me, 0, 0, 0, 0))
    ag_sems, wf_thru, wf_land, ag_token = _copies_start("ag_ffn_start", wf, wf_land, FIRST_LEVEL_PEERS, False)
    rest_all = rest_all.reshape(N_DEV, -1)
    Wb = {}
    off = 0
    for n in rest:
        Wb[n] = rest_all[:, off:off + W[n].size].reshape((N_DEV,) + W[n].shape)
        off += W[n].size
    full = lambda n: _unshard(Wb[n], SHARD_AXIS[n])
    w_dq, w_uq, w_dkv, w_ukv, w_o = full('mla_w_dq'), full('mla_w_uq'), full('mla_w_dkv'), full('mla_w_ukv'), full('mla_w_o')
    w_pw1, w_pw2, w_pool = full('conv_w_pw1'), full('conv_w_pw2'), full('pool_w')
    n_mla = w_dq.shape[0]
    w_uq_p = [_pad_heads(w_uq[j], 0, QK_NOPE + QK_ROPE) for j in range(n_mla)]
    w_ukv_k = [_pad_heads(w_ukv[j], 0, QK_NOPE) for j in range(n_mla)]
    w_ukv_v = [_pad_heads(w_ukv[j], QK_NOPE, QK_NOPE + V_HEAD) for j in range(n_mla)]
    w_dkv_p = [jnp.pad(jnp.concatenate([w_dkv[j][:, :KV_LORA], jnp.zeros((D, QK_NOPE), MM), w_dkv[j][:, KV_LORA:]], axis=1),
                       ((0, 0), (0, HEAD_PAD - QK_NOPE - QK_ROPE))) for j in range(n_mla)]
    w_o_p = [jnp.pad(w_o[j].reshape(N_HEADS, V_HEAD, D), ((0, 0), (0, HEAD_PAD - V_HEAD), (0, 0))).reshape(N_HEADS * HEAD_PAD, D)
             for j in range(n_mla)]
    w_dw32 = jnp.pad(Ws['conv_w_dw'], ((0, 0), (0, 32 - CONV_W), (0, 0)))
    row = lambda t: t.reshape(1, -1)

    half = QK_ROPE // 2
    inv_freq = ROPE_THETA ** (-jnp.arange(0, QK_ROPE, 2, dtype=F32) / QK_ROPE)
    invf = jnp.zeros((1, HEAD_PAD), F32).at[0, QK_NOPE:QK_NOPE + half].set(inv_freq).at[0, QK_NOPE + half:QK_NOPE + QK_ROPE].set(inv_freq)
    rC, rS1, rS2 = _rope_tables(positions.reshape(T, 1).astype(F32), invf)

    xs = x.reshape(T, D)
    saved = []
    for i in range(L):
        kind, j = i % 3, i // 3
        sh_m, sc_m, gt_m, sh_f, sc_f, gt_f = (mod[i, r] for r in range(6))
        g = [row(Ws['norm_g'][i, r]) for r in range(4)]
        st = dict(x0=xs)
        if i == 0:
            sc_m = sc_m + ag_token[0:1, 0:1]
        if kind == 0:
            h = _prenorm(f"prenorm_m{i}", xs, g[0], sc_m, sh_m, MM)
            cq_raw, cq, ckv_raw, ckv, q, k, v = _mla_proj(f"mla_proj{i}", h, rC, rS1, rS2, w_dq[j], row(Ws['mla_q_norm_g'][j]), w_uq_p[j],
                                                          w_dkv_p[j], row(Ws['mla_kv_norm_g'][j]), w_ukv_k[j], w_ukv_v[j])
            o, lse = _attn_fwd(f"attn_fwd{i}", q, k, v)
            y, xs = _mm_post(f"mla_out{i}", o, w_o_p[j], None, xs, g[1], gt_m)
            st.update(h=h, cq_raw=cq_raw, cq=cq, ckv_raw=ckv_raw, ckv=ckv, q=q, k=k, v=v, o=o, lse=lse, y=y)
        elif kind == 1:
            h = _prenorm(f"prenorm_m{i}", xs, g[0], sc_m, sh_m, MM)
            a, u = _conv_glu(f"conv_glu{i}", h, w_pw1[j], row(W['conv_b_pw1'][j]))
            uc, z, y, xs = _conv_dw(f"conv_dw{i}", u, w_dw32[j], row(W['conv_b_dw'][j]), row(W['conv_ln_g'][j]), row(W['conv_ln_b'][j]),
                                    w_pw2[j], row(W['conv_b_pw2'][j]), xs, g[1], gt_m)
            st.update(h=h, a=a, u=u, uc=uc, z=z, y=y)
        else:
            h = _prenorm(f"prenorm_m{i}", xs, g[0], sc_m, sh_m, F32)
            p, ypre, y, xs = _pool_fwd(f"pool_fwd{i}", h, w_pool[j], row(Ws['pool_b'][j]), row(Ws['pool_scale'][j]), xs, g[1], gt_m)
            st.update(p=p, ypre=ypre, y=y)
        st['x1'] = xs
        if i == 0:
            wg = _copies_wait("ag_ffn_wait", ag_sems, wf_thru, wf_land, xs, FIRST_LEVEL_PEERS, False)
            w1g = _ag_forward("ag_ffn_forward", wg)
            w2g = w1g.reshape(w1g.shape[:3] + (w1g.shape[4], w1g.shape[3]))
        hf = _prenorm(f"prenorm_f{i}", xs, g[2], sc_f, sh_f, MM)
        af, yf, xs = _ffn_fwd(f"ffn_fwd{i}", i, hf, w1g, w2g, xs, g[3], gt_f)
        st.update(hf=hf, af=af, yf=yf)
        saved.append(st)

    dx, loss_row = _loss_head(xs, loss_target.reshape(T, D))

    G = {}
    dmod = [None] * L
    dnorm = [None] * L
    rs_pending = None
    ffn_red = [None] * L
    for i in reversed(range(L)):
        kind, j = i % 3, i // 3
        sh_m, sc_m, gt_m, sh_f, sc_f, gt_f = (mod[i, r] for r in range(6))
        g = [row(Ws['norm_g'][i, r]) for r in range(4)]
        st = saved[i]
        dy, dg3, dgt_f, _ = _post_bwd(f"post_bwd_f{i}", dx, st['yf'], g[3], gt_f)
        da, dx, dsh_f, dsc_f, dg2 = _ffn_bwd(f"ffn_bwd{i}", i, dy, st['af'], w1g, w2g, st['x1'], dx, g[2], sc_f)
        dw1 = _mm_tn(f"ffn_dw1_{i}", st['hf'], da, col_shards=N_DEV)
        dw2 = _mm_tn(f"ffn_dw2_{i}", st['af'], dy, sqrelu=True).reshape(dw1.shape)
        wire = _rs_wire(f"rs_wire{i}", dw1, dw2)
        if rs_pending is not None:
            ffn_red[rs_pending[0]] = _rs_finish(rs_pending, wire, me)
        rs_sems, wire_thru, rs_land, rs_token = _copies_start(f"rs_start{i}", wire, lax.empty(wire.shape, MM), ALL_PEERS, True)
        rs_pending = (i, rs_sems, wire_thru, rs_land, dw1, dw2)
        dy, dg1, dgt_m, dysum = _post_bwd(f"post_bwd_m{i}", dx, st['y'], g[1], gt_m + rs_token[0:1, 0:1])
        if kind == 0:
            do = _mm_nt_rows(f"mla_do{i}", dy, w_o_p[j])
            delta = _attn_delta(f"attn_delta{i}", do, st['o'])
            dq, dk, dv = _attn_bwd(f"attn_bwd{i}", st['q'], st['k'], st['v'], do, st['lse'], delta)
            dq_pre, dcq_raw, dckv_all, dx, dqg, dkvg, dsh_m, dsc_m, dg0 = _mla_proj_bwd(
                f"mla_proj_bwd{i}", dq, dk, dv, rC, rS1, rS2, st['cq_raw'], st['ckv_raw'], st['x0'], dx, w_uq_p[j], w_ukv_k[j], w_ukv_v[j],
                w_dq[j], w_dkv_p[j], row(Ws['mla_q_norm_g'][j]), row(Ws['mla_kv_norm_g'][j]), g[0], sc_m)
            dwo = _mm_tn(f"mla_dwo{i}", st['o'], dy)
            dwuq = _mm_tn(f"mla_dwuq{i}", st['cq'], dq_pre)
            dwk = _mm_tn(f"mla_dwukvk{i}", st['ckv'], dk)
            dwv = _mm_tn(f"mla_dwukvv{i}", st['ckv'], dv)
            dwdq = _mm_tn(f"mla_dwdq{i}", st['h'], dcq_raw)
            dwdkv = _mm_tn(f"mla_dwdkv{i}", st['h'], dckv_all)
            G.setdefault('mla_w_o', [None] * n_mla)[j] = dwo.reshape(N_HEADS, HEAD_PAD, D)[:, :V_HEAD].reshape(N_HEADS * V_HEAD, D)
            G.setdefault('mla_w_uq', [None] * n_mla)[j] = dwuq.reshape(Q_LORA, N_HEADS, HEAD_PAD)[:, :, :QK_NOPE + QK_ROPE].reshape(Q_LORA, -1)
            G.setdefault('mla_w_ukv', [None] * n_mla)[j] = jnp.concatenate(
                [dwk.reshape(KV_LORA, N_HEADS, HEAD_PAD)[:, :, :QK_NOPE], dwv.reshape(KV_LORA, N_HEADS, HEAD_PAD)[:, :, :V_HEAD]], axis=2).reshape(KV_LORA, -1)
            G.setdefault('mla_w_dq', [None] * n_mla)[j] = dwdq
            G.setdefault('mla_w_dkv', [None] * n_mla)[j] = jnp.concatenate([dwdkv[:, :KV_LORA], dwdkv[:, KV_LORA + QK_NOPE:KV_LORA + QK_NOPE + QK_ROPE]], axis=1)
            G.setdefault('mla_q_norm_g', [None] * n_mla)[j] = dqg[0]
            G.setdefault('mla_kv_norm_g', [None] * n_mla)[j] = dkvg[0]
        elif kind == 1:
            duc, dlng, dlnb, dbdw = _conv_bwd1(f"conv_bwd1_{i}", dy, st['uc'], w_pw2[j], row(W['conv_ln_g'][j]), row(W['conv_ln_b'][j]))
            da, dx, dwdw, dbpw1, dsh_m, dsc_m, dg0 = _conv_bwd2(f"conv_bwd2_{i}", duc, st['u'], st['a'], st['x0'], dx, w_dw32[j], w_pw1[j], g[0], sc_m)
            G['conv_w_pw2'] = [_mm_tn(f"conv_dwpw2_{i}", st['z'], dy)]
            G['conv_w_pw1'] = [_mm_tn(f"conv_dwpw1_{i}", st['h'], da)]
            G['conv_w_dw'] = [dwdw[:CONV_W]]
            G['conv_b_pw1'], G['conv_b_dw'], G['conv_ln_g'], G['conv_ln_b'], G['conv_b_pw2'] = [dbpw1[0]], [dbdw[0]], [dlng[0]], [dlnb[0]], [dysum[0]]
        else:
            dp, dypre, dscale, dpb = _pool_bwd1(f"pool_bwd1_{i}", dy, st['ypre'], row(Ws['pool_scale'][j]), w_pool[j])
            dx, dsh_m, dsc_m, dg0 = _pool_bwd2(f"pool_bwd2_{i}", dp, st['x0'], dx, g[0], sc_m)
            G['pool_w'] = [_mm_tn(f"pool_dw{i}", st['p'], dypre, diag=len(POOL_WINDOWS))]
            G['pool_b'] = [dpb.reshape(len(POOL_WINDOWS), -1)]
            G['pool_scale'] = [dscale[0]]
        dmod[i] = jnp.concatenate([dsh_m, dsc_m, dgt_m, dsh_f, dsc_f, dgt_f], axis=1)
        dnorm[i] = jnp.concatenate([dg0, dg1, dg2, dg3], axis=0)
    G['norm_g'] = dnorm
    grad_x = dx.reshape(x.shape)

    rs_names = [n for n, ax in WEIGHTS if ax is not None and n != 'ada_w' and not n.startswith('ffn')]
    parts = [_to_shards(jnp.stack(G[n], axis=0), SHARD_AXIS[n]).reshape(N_DEV, -1) for n in rs_names]
    ffn_red[rs_pending[0]] = _rs_finish(rs_pending, dx, me)
    red = _reduce_scatter("rs", [_pack(parts, F32, RS_ROWS)], ci, 2 * xi + yi)
    grads = {'ffn_w1': jnp.stack([r_[0] for r_ in ffn_red], axis=0),
             'ffn_w2': jnp.stack([r_[1].reshape(ffn_w2.shape[1:]) for r_ in ffn_red], axis=0)}
    red0 = red[0].reshape(-1)
    off = 0
    for n in rs_names:
        grads[n] = red0[off:off + W[n].size].reshape(W[n].shape)
        off += W[n].size

    dmod_mine = jnp.concatenate(dmod, axis=1).reshape(-1)
    fin_in = _pack([dmod_mine] + [G[n][0].reshape(-1) for n in REPL] + [loss_row.reshape(-1)], F32, 8)
    fin_all = _ag_small("ag_final", fin_in)
    fin_sum = _sum_devices("final_sum", fin_all).reshape(-1)
    nm = L * 6 * D
    grads['ada_b'] = fin_sum[:nm].reshape(L, 6 * D)
    off = nm
    for n in REPL:
        grads[n] = fin_sum[off:off + W[n].size].reshape(W[n].shape)
        off += W[n].size
    loss = fin_sum[off]
    dmod_all = fin_all.reshape(N_DEV, -1)[:, :nm].reshape(N_DEV, L, 6 * D)
    dmod_cols = lax.dynamic_slice_in_dim(dmod_all, me * n_ada, n_ada, axis=2)
    dmod16 = jnp.pad(jnp.transpose(dmod_cols, (1, 0, 2)), ((0, 0), (0, 16 - N_DEV), (0, 0)))
    grads['ada_w'] = _ada_w_grad(c16, dmod16)

    deltas, new_m, new_v = {}, {}, {}
    for n, _ in WEIGHTS:
        deltas[n], new_m[n], new_v[n] = _adamw("adamw_" + n, W[n], grads[n], M1[n], V2[n])
    names = [n for n, _ in WEIGHTS]
    return (loss, grad_x, *[grads[n] for n in names], *[deltas[n] for n in names], *[new_m[n] for n in names],
            *[new_v[n] for n in names])
```

```python
import functools
import math

import jax
import jax.numpy as jnp
from jax import lax
from jax.experimental import pallas as pl
from jax.experimental.pallas import tpu as pltpu

F32 = jnp.float32
MM = jnp.bfloat16
EPS = 1e-6
NEG = -1e30
N_DEV = 8
VMEM_LIMIT = 48 * 1024 * 1024
MESH = pl.DeviceIdType.MESH

D_MODEL = 1024
N_HEADS = 16
HEAD_PAD = 128
QK_NOPE, QK_ROPE, V_HEAD = 64, 32, 64
Q_LORA, KV_LORA = 384, 256
CHUNK = 64
CONV_W = 31
POOL_WINDOWS = (2, 4, 8, 16)
ROPE_THETA = 10000.0
ATT_SCALE = 1.0 / math.sqrt(QK_NOPE + QK_ROPE)

ADAM_LR, ADAM_B1, ADAM_B2, ADAM_EPS, ADAM_WD, ADAM_STEP = 0.001, 0.9, 0.999, 1e-08, 0.01, 10

WEIGHTS = [('ada_w', 2), ('ada_b', None), ('norm_g', 2), ('mla_w_dq', 1), ('mla_q_norm_g', 1), ('mla_w_uq', 2),
           ('mla_w_dkv', 1), ('mla_kv_norm_g', 1), ('mla_w_ukv', 2), ('mla_w_o', 1), ('conv_w_pw1', 2),
           ('conv_b_pw1', None), ('conv_w_dw', 2), ('conv_b_dw', None), ('conv_ln_g', None), ('conv_ln_b', None),
           ('conv_w_pw2', 1), ('conv_b_pw2', None), ('pool_w', 2), ('pool_b', 2), ('pool_scale', 1),
           ('ffn_w1', 2), ('ffn_w2', 1)]
SHARD_AXIS = dict(WEIGHTS)
BIG = ['mla_w_dq', 'mla_w_uq', 'mla_w_dkv', 'mla_w_ukv', 'mla_w_o', 'conv_w_pw1', 'conv_w_pw2', 'pool_w', 'ffn_w1', 'ffn_w2']
SMALL = ['norm_g', 'mla_q_norm_g', 'mla_kv_norm_g', 'conv_w_dw', 'pool_b', 'pool_scale']
REPL = ['conv_b_pw1', 'conv_b_dw', 'conv_ln_g', 'conv_ln_b', 'conv_b_pw2']


def _dot(a, b):
    return jnp.dot(a.astype(MM), b.astype(MM), preferred_element_type=F32)


def _dot_nt(a, b):
    return lax.dot_general(a.astype(MM), b.astype(MM), (((1,), (1,)), ((), ())), preferred_element_type=F32)


def _dot_tn(a, b):
    return lax.dot_general(a.astype(MM), b.astype(MM), (((0,), (0,)), ((), ())), preferred_element_type=F32)


def _sigmoid(x):
    return 1.0 / (1.0 + jnp.exp(-x))


def _rstd(x):
    return lax.rsqrt(jnp.mean(x * x, axis=-1, keepdims=True) + EPS)


def _rms(x, g):
    return x * _rstd(x) * g


def _rms_bwd(x, g, dout):
    r = _rstd(x)
    xn = x * r
    dg = jnp.sum(dout * xn, axis=0, keepdims=True)
    dxn = dout * g
    dx = r * (dxn - xn * jnp.mean(dxn * xn, axis=-1, keepdims=True))
    return dx, dg


def _prenorm_bwd(x, g0, sc, dh):
    r = _rstd(x)
    xn = x * r
    dsh = jnp.sum(dh, axis=0, keepdims=True)
    dsc = jnp.sum(dh * (xn * g0), axis=0, keepdims=True)
    dn = dh * (1.0 + sc)
    dg0 = jnp.sum(dn * xn, axis=0, keepdims=True)
    dxn = dn * g0
    dx = r * (dxn - xn * jnp.mean(dxn * xn, axis=-1, keepdims=True))
    return dx, dsh, dsc, dg0


def _cparams(sem):
    return pltpu.CompilerParams(dimension_semantics=sem, vmem_limit_bytes=VMEM_LIMIT)


def _rows(name, body, n_rows, tm, rows, consts, outs, accs=(), scratch=()):
    tm = min(tm, n_rows)
    nblk = n_rows // tm
    nr, nc, no, na = len(rows), len(consts), len(outs), len(accs)
    in_specs, args = [], []
    for a, kind in rows:
        if kind == 'cur':
            im = lambda i: (i, 0)
        elif kind == 'prev':
            im = lambda i: (jnp.maximum(i - 1, 0), 0)
        else:
            im = lambda i: (jnp.minimum(i + 1, nblk - 1), 0)
        in_specs.append(pl.BlockSpec((tm, a.shape[1]), im))
        args.append(a)
    for a in consts:
        in_specs.append(pl.BlockSpec(a.shape, lambda i, nd=a.ndim: (0,) * nd))
        args.append(a)
    out_specs = [pl.BlockSpec((tm, c), lambda i: (i, 0)) for c, _ in outs]
    out_specs += [pl.BlockSpec(s, lambda i, nd=len(s): (0,) * nd) for s in accs]
    out_shape = [jax.ShapeDtypeStruct((n_rows, c), dt) for c, dt in outs]
    out_shape += [jax.ShapeDtypeStruct(s, F32) for s in accs]

    def kern(*refs):
        i = pl.program_id(0)
        rr = refs[:nr]
        cc = refs[nr:nr + nc]
        oo = refs[nr + nc:nr + nc + no]
        aa = refs[nr + nc + no:nr + nc + no + na]
        ss = refs[nr + nc + no + na:]

        @pl.when(i == 0)
        def _():
            for a in aa:
                a[...] = jnp.zeros(a.shape, F32)

        body(i, nblk, rr, cc, oo, aa, ss)

    return pl.pallas_call(kern, grid=(nblk,), in_specs=in_specs, out_specs=out_specs, out_shape=out_shape,
                          scratch_shapes=list(scratch), name=name, compiler_params=_cparams(("arbitrary",)))(*args)


def _place():
    return lax.axis_index("x"), lax.axis_index("y"), lax.axis_index("c")


def _ag_small(name, xs):
    R, C = xs.shape

    def body(x_ref, out_ref, send_sems, recv_sems):
        x, y, c = _place()
        me = 4 * x + 2 * y + c
        out_ref[me] = x_ref[...]
        copies = []
        for k in range(1, N_DEV):
            peer = ((1 - x) if k & 4 else x, (1 - y) if k & 2 else y, (1 - c) if k & 1 else c)
            cp = pltpu.make_async_remote_copy(src_ref=x_ref, dst_ref=out_ref.at[me], send_sem=send_sems.at[k - 1],
                                              recv_sem=recv_sems.at[k - 1], device_id=peer, device_id_type=MESH)
            cp.start()
            copies.append(cp)
        for cp in copies:
            cp.wait()

    return pl.pallas_call(
        body, out_shape=jax.ShapeDtypeStruct((N_DEV, R, C), xs.dtype),
        in_specs=[pl.BlockSpec(memory_space=pltpu.VMEM)], out_specs=pl.BlockSpec(memory_space=pltpu.VMEM),
        scratch_shapes=[pltpu.SemaphoreType.DMA((N_DEV - 1,)), pltpu.SemaphoreType.DMA((N_DEV - 1,))], name=name)(xs)


def _ag_big(name, xs):
    nt = len(xs)

    def body(*refs):
        x_refs, out_refs = refs[:nt], refs[nt:2 * nt]
        send_sems, recv_sems, local_sems = refs[2 * nt:]
        x, y, c = _place()
        me, sibling = (x, y, c), (x, y, 1 - c)
        chips = [(1 - x, y), (x, 1 - y), (1 - x, 1 - y)]

        def copy(t, k, block, to, own=False):
            px, py, pc = block
            rows = out_refs[t].at[4 * px + 2 * py + pc]
            return pltpu.make_async_remote_copy(src_ref=x_refs[t] if own else rows, dst_ref=rows, send_sem=send_sems.at[7 * t + k],
                                                recv_sem=recv_sems.at[7 * t + k], device_id=to, device_id_type=MESH)

        mine = [pltpu.make_async_copy(x_refs[t], out_refs[t].at[4 * x + 2 * y + c], local_sems.at[t]) for t in range(nt)]
        for cp in mine:
            cp.start()
        first = []
        for t in range(nt):
            first.append(copy(t, 0, me, sibling, own=True))
            first += [copy(t, 1 + j, me, (*chip, c), own=True) for j, chip in enumerate(chips)]
        for cp in first:
            cp.start()
        passed = []
        for t in range(nt):
            for j, chip in enumerate(chips):
                copy(t, 1 + j, (*chip, c), me).wait_recv()
                cp = copy(t, 4 + j, (*chip, c), sibling)
                cp.start()
                passed.append(cp)
        for t in range(nt):
            copy(t, 0, sibling, me).wait_recv()
            for j, chip in enumerate(chips):
                copy(t, 4 + j, (*chip, 1 - c), me).wait_recv()
        for cp in first + passed:
            cp.wait_send()
        for cp in mine:
            cp.wait()

    hbm = pl.BlockSpec(memory_space=pl.ANY)
    return pl.pallas_call(
        body, out_shape=[jax.ShapeDtypeStruct((N_DEV,) + t.shape, t.dtype) for t in xs],
        in_specs=[hbm] * nt, out_specs=[hbm] * nt,
        scratch_shapes=[pltpu.SemaphoreType.DMA((7 * nt,)), pltpu.SemaphoreType.DMA((7 * nt,)), pltpu.SemaphoreType.DMA((nt,))],
        name=name)(*xs)


def _rs_pair(name, ps):
    nt = len(ps)

    def body(*refs):
        p_refs, recv_refs = refs[:nt], refs[nt:2 * nt]
        send_sems, recv_sems = refs[2 * nt:]
        x, y, c = _place()
        copies = []
        for t in range(nt):
            for j in range(4):
                cp = pltpu.make_async_remote_copy(src_ref=p_refs[t].at[j, 1 - c], dst_ref=recv_refs[t].at[j], send_sem=send_sems.at[4 * t + j],
                                                  recv_sem=recv_sems.at[4 * t + j], device_id=(x, y, 1 - c), device_id_type=MESH)
                cp.start()
                copies.append(cp)
        for cp in copies:
            cp.wait()

    hbm = pl.BlockSpec(memory_space=pl.ANY)
    return pl.pallas_call(
        body, out_shape=[jax.ShapeDtypeStruct((4,) + p.shape[2:], p.dtype) for p in ps], in_specs=[hbm] * nt, out_specs=[hbm] * nt,
        scratch_shapes=[pltpu.SemaphoreType.DMA((4 * nt,)), pltpu.SemaphoreType.DMA((4 * nt,))], name=name)(*ps)


def _rs_chips(name, ss):
    nt = len(ss)

    def body(*refs):
        s_refs, recv_refs = refs[:nt], refs[nt:2 * nt]
        send_sems, recv_sems, local_sems = refs[2 * nt:]
        x, y, c = _place()
        mine = 2 * x + y
        owns = [pltpu.make_async_copy(s_refs[t].at[mine], recv_refs[t].at[mine], local_sems.at[t]) for t in range(nt)]
        for cp in owns:
            cp.start()
        copies = []
        for t in range(nt):
            for k in range(1, 4):
                px = (1 - x) if k & 2 else x
                py = (1 - y) if k & 1 else y
                cp = pltpu.make_async_remote_copy(src_ref=s_refs[t].at[2 * px + py], dst_ref=recv_refs[t].at[mine],
                                                  send_sem=send_sems.at[3 * t + k - 1], recv_sem=recv_sems.at[3 * t + k - 1],
                                                  device_id=(px, py, c), device_id_type=MESH)
                cp.start()
                copies.append(cp)
        for cp in copies:
            cp.wait()
        for cp in owns:
            cp.wait()

    hbm = pl.BlockSpec(memory_space=pl.ANY)
    return pl.pallas_call(
        body, out_shape=[jax.ShapeDtypeStruct(s_.shape, s_.dtype) for s_ in ss], in_specs=[hbm] * nt, out_specs=[hbm] * nt,
        scratch_shapes=[pltpu.SemaphoreType.DMA((3 * nt,)), pltpu.SemaphoreType.DMA((3 * nt,)), pltpu.SemaphoreType.DMA((nt,))],
        name=name)(*ss)


RS_ROWS = 256


def _pair_sum(name, p, recv, my_c, my_chip):
    _, _, r, c = p.shape
    tr = RS_ROWS

    def body(sc_ref, p_ref, r_ref, o_ref, own_ref):
        s = p_ref[...] + r_ref[...]
        o_ref[...] = s.astype(MM)

        @pl.when(pl.program_id(1) == sc_ref[1])
        def _():
            own_ref[...] = s

    return pl.pallas_call(
        body, grid_spec=pltpu.PrefetchScalarGridSpec(
            num_scalar_prefetch=1, grid=(r // tr, 4),
            in_specs=[pl.BlockSpec((None, None, tr, c), lambda i, j, sc: (j, sc[0], i, 0)),
                      pl.BlockSpec((None, tr, c), lambda i, j, sc: (j, i, 0))],
            out_specs=[pl.BlockSpec((None, tr, c), lambda i, j, sc: (j, i, 0)), pl.BlockSpec((tr, c), lambda i, j, sc: (i, 0))]),
        out_shape=[jax.ShapeDtypeStruct((4, r, c), MM), jax.ShapeDtypeStruct((r, c), F32)], name=name,
        compiler_params=_cparams(("arbitrary", "arbitrary")))(jnp.stack([my_c, my_chip]), p, recv)


def _chip_sum(name, own, recv, my_chip):
    _, r, c = recv.shape
    tr = RS_ROWS

    def body(sc_ref, own_ref, r_ref, o_ref):
        acc = jnp.zeros((tr, c), F32)
        for j in range(4):
            acc = acc + jnp.where(sc_ref[0] == j, own_ref[...], r_ref[j].astype(F32))
        o_ref[...] = acc

    return pl.pallas_call(
        body, grid_spec=pltpu.PrefetchScalarGridSpec(
            num_scalar_prefetch=1, grid=(r // tr,),
            in_specs=[pl.BlockSpec((tr, c), lambda i, sc: (i, 0)), pl.BlockSpec((4, tr, c), lambda i, sc: (0, i, 0))],
            out_specs=pl.BlockSpec((tr, c), lambda i, sc: (i, 0))),
        out_shape=jax.ShapeDtypeStruct((r, c), F32), name=name,
        compiler_params=_cparams(("arbitrary",)))(my_chip.reshape(1), own, recv)


def _reduce_scatter(tag, tensors, my_c, my_chip):
    ps = [t.reshape((4, 2) + t.shape[1:]) for t in tensors]
    recv = _rs_pair(tag + "_pair", ps)
    sums = [_pair_sum(f"{tag}_pair_sum{t}", ps[t], recv[t], my_c, my_chip) for t in range(len(ps))]
    recv2 = _rs_chips(tag + "_chips", [s_[0] for s_ in sums])
    return [_chip_sum(f"{tag}_chip_sum{t}", sums[t][1], recv2[t], my_chip) for t in range(len(ps))]


HBM_SPEC = pl.BlockSpec(memory_space=pltpu.HBM)
SEM_SPEC = pl.BlockSpec(memory_space=pltpu.SEMAPHORE)
SPLIT_EFFECT = pltpu.SideEffectType.DATAFLOW_SIDE_EFFECTING
ALL_PEERS = (1, 2, 3, 4, 5, 6, 7)
FIRST_LEVEL_PEERS = (1, 4, 2, 6)


def _split_copies(src_refs, land_refs, sems, masks, src_per_peer):
    n, nt = len(masks), len(src_refs)
    x, y, c = _place()
    me = 4 * x + 2 * y + c
    copies = []
    for t in range(nt):
        for k, mask in enumerate(masks):
            px, py, pc = (1 - x) if mask & 4 else x, (1 - y) if mask & 2 else y, (1 - c) if mask & 1 else c
            src = src_refs[t].at[4 * px + 2 * py + pc] if src_per_peer else src_refs[t]
            copies.append(pltpu.make_async_remote_copy(src_ref=src, dst_ref=land_refs[t].at[me], send_sem=sems[t * n + k],
                                                       recv_sem=sems[nt * n + t * n + k], device_id=(px, py, pc), device_id_type=MESH))
    return copies


def _copies_start(name, srcs, lands, masks, src_per_peer):
    nt, ns = len(srcs), 2 * len(masks) * len(srcs)

    def body(*refs):
        for cp in _split_copies(refs[:nt], refs[nt:2 * nt], refs[2 * nt:2 * nt + ns], masks, src_per_peer):
            cp.start()
        token = refs[-1]
        token[...] = jnp.zeros(token.shape, F32)

    outs = pl.pallas_call(
        body, name=name,
        out_shape=(pltpu.SemaphoreType.DMA(()),) * ns + tuple(pltpu.HBM(a.shape, a.dtype) for a in list(srcs) + list(lands))
        + (jax.ShapeDtypeStruct((8, 128), F32),),
        in_specs=(HBM_SPEC,) * (2 * nt), out_specs=(SEM_SPEC,) * ns + (HBM_SPEC,) * (2 * nt) + (pl.BlockSpec(memory_space=pltpu.VMEM),),
        input_output_aliases={t: ns + t for t in range(2 * nt)}, compiler_params=pltpu.CompilerParams(has_side_effects=SPLIT_EFFECT))(
            *[pltpu.with_memory_space_constraint(a, pltpu.HBM) for a in list(srcs) + list(lands)])
    return outs[:ns], outs[ns:ns + nt], outs[ns + nt:ns + 2 * nt], outs[-1]


def _copies_wait(name, sems, srcs_thru, lands_thru, after, masks, src_per_peer):
    nt, ns = len(srcs_thru), len(sems)

    def body(*refs):
        for cp in _split_copies(refs[:nt], refs[nt:2 * nt], refs[2 * nt:2 * nt + ns], masks, src_per_peer):
            cp.wait_send()
            cp.wait_recv()

    thru = list(srcs_thru) + list(lands_thru)
    return pl.pallas_call(
        body, name=name, out_shape=tuple(pltpu.HBM(a.shape, a.dtype) for a in thru),
        in_specs=(HBM_SPEC,) * (2 * nt) + (SEM_SPEC,) * ns + (pl.BlockSpec(memory_space=pl.ANY),), out_specs=(HBM_SPEC,) * (2 * nt),
        input_output_aliases={t: t for t in range(2 * nt)}, compiler_params=pltpu.CompilerParams(has_side_effects=SPLIT_EFFECT))(
            *thru, *sems, after)[nt:]


def _ag_forward(name, gs):
    nt = len(gs)

    def body(*refs):
        o_refs, send_sems, recv_sems = refs[nt:2 * nt], refs[2 * nt], refs[2 * nt + 1]
        x, y, c = _place()
        chips = [(1 - x, y), (x, 1 - y), (1 - x, 1 - y)]

        def copy(t, j, pc):
            rows = o_refs[t].at[4 * chips[j][0] + 2 * chips[j][1] + pc]
            return pltpu.make_async_remote_copy(src_ref=rows, dst_ref=rows, send_sem=send_sems.at[3 * t + j], recv_sem=recv_sems.at[3 * t + j],
                                                device_id=(x, y, 1 - c), device_id_type=MESH)

        for t in range(nt):
            for j in range(3):
                copy(t, j, c).start()
        for t in range(nt):
            for j in range(3):
                copy(t, j, c).wait_send()
                copy(t, j, 1 - c).wait_recv()

    hbm = pl.BlockSpec(memory_space=pl.ANY)
    return pl.pallas_call(body, out_shape=[jax.ShapeDtypeStruct(g.shape, g.dtype) for g in gs], in_specs=[hbm] * nt, out_specs=[hbm] * nt,
                          scratch_shapes=[pltpu.SemaphoreType.DMA((3 * nt,)), pltpu.SemaphoreType.DMA((3 * nt,))],
                          input_output_aliases={t: t for t in range(nt)}, name=name)(*gs)


def _mm_tn_wire(name, a, b, me, sqrelu, shard_rows):
    T, M = a.shape
    N = b.shape[1]
    tk = min(512, T)
    nk = T // tk
    if shard_rows:
        bm, bn = M // N_DEV, N
        a_spec = pl.BlockSpec((tk, bm), lambda j, k, m: (k, j))
        b_spec = pl.BlockSpec((tk, bn), lambda j, k, m: (k, 0))
    else:
        bm, bn = M, N // N_DEV
        a_spec = pl.BlockSpec((tk, bm), lambda j, k, m: (k, 0))
        b_spec = pl.BlockSpec((tk, bn), lambda j, k, m: (k, j))

    def body(me_ref, a_ref, b_ref, wire_ref, own_ref, acc):
        j, k = pl.program_id(0), pl.program_id(1)

        @pl.when(k == 0)
        def _():
            acc[...] = jnp.zeros(acc.shape, F32)

        av = a_ref[...]
        if sqrelu:
            r = jnp.maximum(av.astype(F32), 0.0)
            av = r * r
        acc[...] += _dot_tn(av, b_ref[...])

        @pl.when(k == nk - 1)
        def _():
            wire_ref[...] = acc[...].astype(MM)

        @pl.when((k == nk - 1) & (j == me_ref[0]))
        def _():
            own_ref[...] = acc[...]

    return pl.pallas_call(
        body, grid_spec=pltpu.PrefetchScalarGridSpec(
            num_scalar_prefetch=1, grid=(N_DEV, nk), in_specs=[a_spec, b_spec],
            out_specs=[pl.BlockSpec((None, bm, bn), lambda j, k, m: (j, 0, 0)), pl.BlockSpec((bm, bn), lambda j, k, m: (0, 0))],
            scratch_shapes=[pltpu.VMEM((bm, bn), F32)]),
        out_shape=[jax.ShapeDtypeStruct((N_DEV, bm, bn), MM), jax.ShapeDtypeStruct((bm, bn), F32)], name=name,
        compiler_params=_cparams(("arbitrary", "arbitrary")))(me.reshape(1), a, b)


def _rs_final(name, own, recv, me):
    _, r, c = recv.shape
    tr = RS_ROWS

    def body(me_ref, own_ref, r_ref, o_ref):
        acc = jnp.zeros((tr, c), F32)
        for j in range(N_DEV):
            acc = acc + jnp.where(me_ref[0] == j, own_ref[...], r_ref[j].astype(F32))
        o_ref[...] = acc

    blk = pl.BlockSpec((tr, c), lambda i, m: (i, 0))
    return pl.pallas_call(
        body, grid_spec=pltpu.PrefetchScalarGridSpec(
            num_scalar_prefetch=1, grid=(r // tr,), in_specs=[blk, pl.BlockSpec((N_DEV, tr, c), lambda i, m: (0, i, 0))], out_specs=blk),
        out_shape=jax.ShapeDtypeStruct((r, c), F32), name=name, compiler_params=_cparams(("arbitrary",)))(me.reshape(1), own, recv)


def _rs_finish(pending, after, me):
    i, sems, wires_thru, lands, owns = pending
    recvs = _copies_wait(f"rs_wait{i}", sems, wires_thru, lands, after, ALL_PEERS, True)
    return [_rs_final(f"rs_final{i}_{t}", owns[t], recvs[t], me) for t in range(len(owns))]


def _mod_part(c16, ada_w, ada_b_cols):
    L, D, n = ada_w.shape

    def body(c_ref, w_ref, b_ref, o_ref):
        cv = c_ref[...]
        o_ref[...] = _dot(cv * _sigmoid(cv), w_ref[...]) + b_ref[...]

    return pl.pallas_call(
        body, grid=(L,), in_specs=[pl.BlockSpec((16, D), lambda i: (0, 0)), pl.BlockSpec((None, D, n), lambda i: (i, 0, 0)),
                                   pl.BlockSpec((None, 1, n), lambda i: (i, 0, 0))],
        out_specs=pl.BlockSpec((None, 16, n), lambda i: (i, 0, 0)), out_shape=jax.ShapeDtypeStruct((L, 16, n), F32),
        name="ada_mod", compiler_params=_cparams(("arbitrary",)))(c16, ada_w, ada_b_cols)


def _ada_w_grad(c16, dmod16):
    L, _, n = dmod16.shape
    D = c16.shape[1]

    def body(c_ref, d_ref, o_ref):
        cv = c_ref[...]
        o_ref[...] = _dot_tn(cv * _sigmoid(cv), d_ref[...])

    return pl.pallas_call(
        body, grid=(L,), in_specs=[pl.BlockSpec((16, D), lambda i: (0, 0)), pl.BlockSpec((None, 16, n), lambda i: (i, 0, 0))],
        out_specs=pl.BlockSpec((None, D, n), lambda i: (i, 0, 0)), out_shape=jax.ShapeDtypeStruct((L, D, n), F32),
        name="ada_w_grad", compiler_params=_cparams(("arbitrary",)))(c16, dmod16)


def _sum_devices(name, g):
    _, R, C = g.shape

    def body(g_ref, o_ref):
        acc = g_ref[0]
        for d in range(1, N_DEV):
            acc = acc + g_ref[d]
        o_ref[...] = acc

    return pl.pallas_call(body, out_shape=jax.ShapeDtypeStruct((R, C), F32), name=name)(g)


def _prenorm(name, x, g0, sc, sh, dtype):
    T, D = x.shape

    def body(i, n, rr, cc, oo, aa, ss):
        oo[0][...] = (_rms(rr[0][...], cc[0][...]) * (1.0 + cc[1][...]) + cc[2][...]).astype(dtype)

    return _rows(name, body, T, 512, [(x, 'cur')], [g0, sc, sh], [(D, dtype)])[0]


def _post_bwd(name, dxo, y, g1, gt):
    T, D = y.shape

    def body(i, n, rr, cc, oo, aa, ss):
        d = rr[0][...]
        yv = rr[1][...]
        g1v, gtv = cc[0][...], cc[1][...]
        aa[1][...] += jnp.sum(d * _rms(yv, g1v), axis=0, keepdims=True)
        dy, dg1 = _rms_bwd(yv, g1v, d * gtv)
        aa[0][...] += dg1
        aa[2][...] += jnp.sum(dy, axis=0, keepdims=True)
        oo[0][...] = dy.astype(MM)

    return _rows(name, body, T, 512, [(dxo, 'cur'), (y, 'cur')], [g1, gt], [(D, MM)], accs=[(1, D)] * 3)


def _mm_post(name, a, w, bias, x, g1, gt):
    T, D = x.shape
    consts = [w, g1, gt] + ([bias] if bias is not None else [])

    def body(i, n, rr, cc, oo, aa, ss):
        y = _dot(rr[0][...], cc[0][...])
        if bias is not None:
            y = y + cc[3][...]
        oo[0][...] = y
        oo[1][...] = rr[1][...] + cc[2][...] * _rms(y, cc[1][...])

    return _rows(name, body, T, 512, [(a, 'cur'), (x, 'cur')], consts, [(D, F32), (D, F32)])


def _mm_nt_rows(name, a, w):
    T = a.shape[0]
    K = w.shape[0]

    def body(i, n, rr, cc, oo, aa, ss):
        oo[0][...] = _dot_nt(rr[0][...], cc[0][...]).astype(MM)

    return _rows(name, body, T, 512, [(a, 'cur')], [w], [(K, MM)])[0]


def _mm_tn(name, a, b, sqrelu=False, col_shards=0, diag=0):
    T, M = a.shape
    N = b.shape[1]
    tk = min(512, T)
    nk = T // tk
    if diag:
        bm, bn = M // diag, N // diag
        grid = (diag, 1, nk)
        a_spec = pl.BlockSpec((tk, bm), lambda g, n, k: (k, g))
        b_spec = pl.BlockSpec((tk, bn), lambda g, n, k: (k, g))
        o_spec = pl.BlockSpec((None, bm, bn), lambda g, n, k: (g, 0, 0))
        o_shape = (diag, bm, bn)
    else:
        bm = min(M, 1024)
        bn = N // col_shards if col_shards else min(N, 1024)
        grid = (M // bm, N // bn, nk)
        a_spec = pl.BlockSpec((tk, bm), lambda m, n, k: (k, m))
        b_spec = pl.BlockSpec((tk, bn), lambda m, n, k: (k, n))
        if col_shards:
            o_spec = pl.BlockSpec((None, bm, bn), lambda m, n, k: (n, m, 0))
            o_shape = (col_shards, M, bn)
        else:
            o_spec = pl.BlockSpec((bm, bn), lambda m, n, k: (m, n))
            o_shape = (M, N)

    def body(a_ref, b_ref, o_ref):
        @pl.when(pl.program_id(2) == 0)
        def _():
            o_ref[...] = jnp.zeros(o_ref.shape, F32)

        av = a_ref[...]
        if sqrelu:
            r = jnp.maximum(av.astype(F32), 0.0)
            av = r * r
        o_ref[...] += _dot_tn(av, b_ref[...])

    return pl.pallas_call(body, grid=grid, in_specs=[a_spec, b_spec], out_specs=o_spec,
                          out_shape=jax.ShapeDtypeStruct(o_shape, F32), name=name,
                          compiler_params=_cparams(("arbitrary", "arbitrary", "arbitrary")))(a, b)


def _ffn_fwd(name, li, h, w1g, w2g, x, g1, gt):
    T, D = h.shape
    nf, tf = w1g.shape[0], w1g.shape[-1]
    F = nf * tf
    tm = min(512, T)

    def body(h_ref, w1_ref, w2_ref, x_ref, g1_ref, gt_ref, a_ref, y_ref, xo_ref, acc):
        f = pl.program_id(1)

        @pl.when(f == 0)
        def _():
            acc[...] = jnp.zeros(acc.shape, F32)

        a = _dot(h_ref[...], w1_ref[...])
        a_ref[...] = a.astype(MM)
        r = jnp.maximum(a, 0.0)
        acc[...] += _dot(r * r, w2_ref[...])

        @pl.when(f == nf - 1)
        def _():
            y = acc[...]
            y_ref[...] = y
            xo_ref[...] = x_ref[...] + gt_ref[...] * _rms(y, g1_ref[...])

    row = lambda t, f: (t, 0)
    one = lambda t, f: (0, 0)
    return pl.pallas_call(
        body, grid=(T // tm, nf),
        in_specs=[pl.BlockSpec((tm, D), row), pl.BlockSpec((None, None, D, tf), lambda t, f: (f, li, 0, 0)),
                  pl.BlockSpec((None, None, tf, D), lambda t, f: (f, li, 0, 0)),
                  pl.BlockSpec((tm, D), row), pl.BlockSpec((1, D), one), pl.BlockSpec((1, D), one)],
        out_specs=[pl.BlockSpec((tm, tf), lambda t, f: (t, f)), pl.BlockSpec((tm, D), row), pl.BlockSpec((tm, D), row)],
        out_shape=[jax.ShapeDtypeStruct((T, F), MM), jax.ShapeDtypeStruct((T, D), F32), jax.ShapeDtypeStruct((T, D), F32)],
        scratch_shapes=[pltpu.VMEM((tm, D), F32)], name=name,
        compiler_params=_cparams(("arbitrary", "arbitrary")))(h, w1g, w2g, x, g1, gt)


def _ffn_bwd(name, li, dy, a, w1g, w2g, x, dxo, g0, sc):
    T, D = x.shape
    nf, tf = w1g.shape[0], w1g.shape[-1]
    F = nf * tf
    tm = min(512, T)

    def body(dy_ref, a_ref, w1_ref, w2_ref, x_ref, dxo_ref, g0_ref, sc_ref, da_ref, dx_ref, dsh_ref, dsc_ref, dg0_ref, acc):
        t, f = pl.program_id(0), pl.program_id(1)

        @pl.when((t == 0) & (f == 0))
        def _():
            for r in (dsh_ref, dsc_ref, dg0_ref):
                r[...] = jnp.zeros(r.shape, F32)

        @pl.when(f == 0)
        def _():
            acc[...] = jnp.zeros(acc.shape, F32)

        du = _dot_nt(dy_ref[...], w2_ref[...])
        da = (du * (2.0 * jnp.maximum(a_ref[...].astype(F32), 0.0))).astype(MM)
        da_ref[...] = da
        acc[...] += _dot_nt(da, w1_ref[...])

        @pl.when(f == nf - 1)
        def _():
            dx, dsh, dsc, dg0 = _prenorm_bwd(x_ref[...], g0_ref[...], sc_ref[...], acc[...])
            dx_ref[...] = dxo_ref[...] + dx
            dsh_ref[...] += dsh
            dsc_ref[...] += dsc
            dg0_ref[...] += dg0

    row = lambda t, f: (t, 0)
    one = lambda t, f: (0, 0)
    blk = lambda t, f: (t, f)
    return pl.pallas_call(
        body, grid=(T // tm, nf),
        in_specs=[pl.BlockSpec((tm, D), row), pl.BlockSpec((tm, tf), blk), pl.BlockSpec((None, None, D, tf), lambda t, f: (f, li, 0, 0)),
                  pl.BlockSpec((None, None, tf, D), lambda t, f: (f, li, 0, 0)), pl.BlockSpec((tm, D), row), pl.BlockSpec((tm, D), row),
                  pl.BlockSpec((1, D), one), pl.BlockSpec((1, D), one)],
        out_specs=[pl.BlockSpec((tm, tf), blk), pl.BlockSpec((tm, D), row)] + [pl.BlockSpec((1, D), one)] * 3,
        out_shape=[jax.ShapeDtypeStruct((T, F), MM), jax.ShapeDtypeStruct((T, D), F32)] + [jax.ShapeDtypeStruct((1, D), F32)] * 3,
        scratch_shapes=[pltpu.VMEM((tm, D), F32)], name=name,
        compiler_params=_cparams(("arbitrary", "arbitrary")))(dy, a, w1g, w2g, x, dxo, g0, sc)


def _rope_tables(pos, invf):
    T = pos.shape[0]

    def body(i, n, rr, cc, oo, aa, ss):
        ang = rr[0][...] * cc[0][...]
        lane = lax.broadcasted_iota(jnp.int32, ang.shape, 1)
        cs, sn = jnp.cos(ang), jnp.sin(ang)
        oo[0][...] = jnp.where((lane >= QK_NOPE) & (lane < QK_NOPE + QK_ROPE), cs, 1.0)
        oo[1][...] = jnp.where((lane >= QK_NOPE) & (lane < QK_NOPE + QK_ROPE // 2), -sn, 0.0)
        oo[2][...] = jnp.where((lane >= QK_NOPE + QK_ROPE // 2) & (lane < QK_NOPE + QK_ROPE), sn, 0.0)

    return _rows("rope_tables", body, T, 512, [(pos, 'cur')], [invf], [(HEAD_PAD, F32)] * 3)


def _rope(v, C, S1, S2):
    n = v.shape[1]
    reps = n // HEAD_PAD
    if reps > 1:
        C, S1, S2 = (jnp.tile(t, (1, reps)) for t in (C, S1, S2))
    return v * C + pltpu.roll(v, n - QK_ROPE // 2, 1) * S1 + pltpu.roll(v, QK_ROPE // 2, 1) * S2


def _unrope(d, C, S1, S2):
    n = d.shape[1]
    reps = n // HEAD_PAD
    if reps > 1:
        C, S1, S2 = (jnp.tile(t, (1, reps)) for t in (C, S1, S2))
    return d * C + pltpu.roll(d * S1, QK_ROPE // 2, 1) + pltpu.roll(d * S2, n - QK_ROPE // 2, 1)


def _mla_proj(name, h, C, S1, S2, w_dq, qg, w_uq, w_dkv, kvg, w_ukv_k, w_ukv_v):
    T = h.shape[0]
    HP = N_HEADS * HEAD_PAD

    def body(i, n, rr, cc, oo, aa, ss):
        hv = rr[0][...]
        Cv, S1v, S2v = rr[1][...], rr[2][...], rr[3][...]
        cq_raw = _dot(hv, cc[0][...])
        cq = _rms(cq_raw, cc[1][...]).astype(MM)
        q = _rope(_dot(cq, cc[2][...]), Cv, S1v, S2v)
        ckv_all = _dot(hv, cc[3][...])
        ckv_raw = ckv_all[:, :KV_LORA]
        ckv = _rms(ckv_raw, cc[4][...]).astype(MM)
        kr = _rope(ckv_all[:, KV_LORA:], Cv, S1v, S2v)
        k = _dot(ckv, cc[5][...]) + jnp.tile(kr, (1, N_HEADS))
        v = _dot(ckv, cc[6][...])
        v = jnp.where(lax.broadcasted_iota(jnp.int32, v.shape, 1) % HEAD_PAD == V_HEAD, 1.0, v)
        oo[0][...] = cq_raw
        oo[1][...] = cq
        oo[2][...] = ckv_raw
        oo[3][...] = ckv
        oo[4][...] = q.astype(MM)
        oo[5][...] = k.astype(MM)
        oo[6][...] = v.astype(MM)

    return _rows(name, body, T, 256, [(h, 'cur'), (C, 'cur'), (S1, 'cur'), (S2, 'cur')],
                 [w_dq, qg, w_uq, w_dkv, kvg, w_ukv_k, w_ukv_v],
                 [(Q_LORA, F32), (Q_LORA, MM), (KV_LORA, F32), (KV_LORA, MM), (HP, MM), (HP, MM), (HP, MM)])


ATT_HEADS = 4
ATT_BLOCK = 512


def _chunk_mask_t(tk, tq):
    ki = lax.broadcasted_iota(jnp.int32, (tk, tq), 0) // CHUNK
    qi = lax.broadcasted_iota(jnp.int32, (tk, tq), 1) // CHUNK
    return ki <= qi


def _attn_fwd(name, q, k, v):
    T = q.shape[0]
    tb = min(ATT_BLOCK, T)
    nb = T // tb
    nh = ATT_HEADS
    hs = [slice(h * HEAD_PAD, (h + 1) * HEAD_PAD) for h in range(nh)]

    def body(q_ref, k_ref, v_ref, o_ref, lse_ref):
        qb = pl.program_id(1)

        def k_block(k0, masked, st):
            new = []
            for h in range(nh):
                m, acc = st[h]
                s = _dot_nt(k_ref[pl.ds(k0, tb), hs[h]], q_ref[:, hs[h]])
                if masked:
                    s = jnp.where(_chunk_mask_t(tb, tb), s, NEG)
                m_new = jnp.maximum(m, jnp.max(s, axis=0, keepdims=True))
                alpha = jnp.exp((m - m_new) * ATT_SCALE)
                p = jnp.exp((s - m_new) * ATT_SCALE)
                acc = alpha * acc + _dot_tn(v_ref[pl.ds(k0, tb), hs[h]], p)
                new.append((m_new, acc))
            return tuple(new)

        st = tuple((jnp.full((1, tb), NEG, F32), jnp.zeros((HEAD_PAD, tb), F32)) for _ in range(nh))
        st = k_block(pl.multiple_of(qb * tb, tb), True, st)
        st = lax.fori_loop(0, qb, lambda kb, s_: k_block(pl.multiple_of(kb * tb, tb), False, s_), st)
        for h in range(nh):
            m, acc = st[h]
            l = acc[V_HEAD:V_HEAD + 1, :]
            o_ref[:, hs[h]] = (acc / l).T.astype(MM)
            lse_ref[h] = jnp.broadcast_to(m * ATT_SCALE + jnp.log(l), (8, tb))

    blk = pl.BlockSpec((tb, nh * HEAD_PAD), lambda g, i: (i, g))
    res = pl.BlockSpec((T, nh * HEAD_PAD), lambda g, i: (0, g))
    return pl.pallas_call(
        body, grid=(N_HEADS // nh, nb), in_specs=[blk, res, res],
        out_specs=[blk, pl.BlockSpec((nh, 8, tb), lambda g, i: (g, 0, i))],
        out_shape=[jax.ShapeDtypeStruct(q.shape, MM), jax.ShapeDtypeStruct((N_HEADS, 8, T), F32)], name=name,
        compiler_params=_cparams(("arbitrary", "arbitrary")))(q, k, v)


def _attn_delta(name, do, o):
    T = do.shape[0]
    tb = min(256, T)

    def body(do_ref, o_ref, d_ref):
        lane = lax.broadcasted_iota(jnp.int32, (tb, HEAD_PAD), 1) // 8
        cols = jnp.zeros((tb, HEAD_PAD), F32)
        for h in range(N_HEADS):
            hsl = slice(h * HEAD_PAD, (h + 1) * HEAD_PAD)
            r = jnp.sum(do_ref[:, hsl].astype(F32) * o_ref[:, hsl].astype(F32), axis=1, keepdims=True)
            cols = jnp.where(lane == h, r, cols)
        d_ref[...] = cols.T

    spec = pl.BlockSpec((tb, N_HEADS * HEAD_PAD), lambda i: (i, 0))
    out = pl.pallas_call(body, grid=(T // tb,), in_specs=[spec, spec], out_specs=pl.BlockSpec((HEAD_PAD, tb), lambda i: (0, i)),
                         out_shape=jax.ShapeDtypeStruct((HEAD_PAD, T), F32), name=name, compiler_params=_cparams(("arbitrary",)))(do, o)
    return out.reshape(N_HEADS, 8, T)


def _attn_bwd(name, q, k, v, do, lse, delta):
    T = q.shape[0]
    tb = min(ATT_BLOCK, T)
    nb = T // tb
    nh = ATT_HEADS
    hs = [slice(h * HEAD_PAD, (h + 1) * HEAD_PAD) for h in range(nh)]

    def body(q_ref, k_ref, v_ref, do_ref, lse_ref, dl_ref, dq_ref, dk_ref, dv_ref, dq_acc, dk_acc, dv_acc):
        kb = pl.program_id(1)

        @pl.when(kb == 0)
        def _():
            dq_acc[...] = jnp.zeros(dq_acc.shape, F32)

        dk_acc[...] = jnp.zeros(dk_acc.shape, F32)
        dv_acc[...] = jnp.zeros(dv_acc.shape, F32)

        def q_block(q0, masked):
            for h in range(nh):
                qh = q_ref[pl.ds(q0, tb), hs[h]]
                doh = do_ref[pl.ds(q0, tb), hs[h]]
                kh = k_ref[:, hs[h]]
                s = _dot_nt(kh, qh) * ATT_SCALE
                if masked:
                    s = jnp.where(_chunk_mask_t(tb, tb), s, NEG)
                p = jnp.exp(s - lse_ref[h, 0:1, pl.ds(q0, tb)])
                ds = (p * (_dot_nt(v_ref[:, hs[h]], doh) - dl_ref[h, 0:1, pl.ds(q0, tb)]) * ATT_SCALE).astype(MM)
                dv_acc[:, hs[h]] += _dot(p, doh)
                dk_acc[:, hs[h]] += _dot(ds, qh)
                dq_acc[pl.ds(q0, tb), hs[h]] += _dot_tn(ds, kh)

        q_block(pl.multiple_of(kb * tb, tb), True)

        def rest(qb, c_):
            q_block(pl.multiple_of(qb * tb, tb), False)
            return c_

        lax.fori_loop(kb + 1, nb, rest, 0)
        dk_ref[...] = dk_acc[...].astype(MM)
        dv_ref[...] = dv_acc[...].astype(MM)

        @pl.when(kb == nb - 1)
        def _():
            dq_ref[...] = dq_acc[...].astype(MM)

    W = nh * HEAD_PAD
    blk = pl.BlockSpec((tb, W), lambda g, i: (i, g))
    res = pl.BlockSpec((T, W), lambda g, i: (0, g))
    rows = pl.BlockSpec((nh, 8, T), lambda g, i: (g, 0, 0))
    return pl.pallas_call(
        body, grid=(N_HEADS // nh, nb), in_specs=[res, blk, blk, res, rows, rows], out_specs=[res, blk, blk],
        out_shape=[jax.ShapeDtypeStruct(q.shape, MM)] * 3,
        scratch_shapes=[pltpu.VMEM((T, W), F32), pltpu.VMEM((tb, W), F32), pltpu.VMEM((tb, W), F32)],
        name=name, compiler_params=_cparams(("arbitrary", "arbitrary")))(q, k, v, do, lse, delta)


def _mla_proj_bwd(name, dq, dk, dv, C, S1, S2, cq_raw, ckv_raw, x, dxo, w_uq, w_ukv_k, w_ukv_v, w_dq, w_dkv, qg, kvg, g0, sc):
    T, D = x.shape
    HP = N_HEADS * HEAD_PAD

    def body(i, n, rr, cc, oo, aa, ss):
        Cv, S1v, S2v = rr[3][...], rr[4][...], rr[5][...]
        dq_pre = _unrope(rr[0][...].astype(F32), Cv, S1v, S2v).astype(MM)
        oo[0][...] = dq_pre
        dcq = _dot_nt(dq_pre, cc[0][...])
        dcq_raw, dqg = _rms_bwd(rr[6][...], cc[5][...], dcq)
        aa[0][...] += dqg
        dcq_raw = dcq_raw.astype(MM)
        oo[1][...] = dcq_raw
        dkv = rr[1][...]
        dkr = dkv[:, :HEAD_PAD].astype(F32)
        for hh in range(1, N_HEADS):
            dkr = dkr + dkv[:, hh * HEAD_PAD:(hh + 1) * HEAD_PAD].astype(F32)
        lane = lax.broadcasted_iota(jnp.int32, dkr.shape, 1)
        dkr = jnp.where((lane >= QK_NOPE) & (lane < QK_NOPE + QK_ROPE), _unrope(dkr, Cv, S1v, S2v), 0.0)
        dckv = _dot_nt(dkv, cc[1][...]) + _dot_nt(rr[2][...], cc[2][...])
        dckv_raw, dkvg = _rms_bwd(rr[7][...], cc[6][...], dckv)
        aa[1][...] += dkvg
        dckv_all = jnp.concatenate([dckv_raw, dkr], axis=1).astype(MM)
        oo[2][...] = dckv_all
        dh = _dot_nt(dcq_raw, cc[3][...]) + _dot_nt(dckv_all, cc[4][...])
        dx, dsh, dsc, dg0 = _prenorm_bwd(rr[8][...], cc[7][...], cc[8][...], dh)
        oo[3][...] = rr[9][...] + dx
        aa[2][...] += dsh
        aa[3][...] += dsc
        aa[4][...] += dg0

    return _rows(name, body, T, 256,
                 [(dq, 'cur'), (dk, 'cur'), (dv, 'cur'), (C, 'cur'), (S1, 'cur'), (S2, 'cur'), (cq_raw, 'cur'), (ckv_raw, 'cur'),
                  (x, 'cur'), (dxo, 'cur')],
                 [w_uq, w_ukv_k, w_ukv_v, w_dq, w_dkv, qg, kvg, g0, sc],
                 [(HP, MM), (Q_LORA, MM), (KV_LORA + HEAD_PAD, MM), (D, F32)],
                 accs=[(1, Q_LORA), (1, KV_LORA), (1, D), (1, D), (1, D)])


HALO = 32


def _conv_glu(name, h, w_pw1, b_pw1):
    T, D = h.shape

    def body(i, n, rr, cc, oo, aa, ss):
        a = _dot(rr[0][...], cc[0][...]) + cc[1][...]
        oo[0][...] = a
        oo[1][...] = a[:, :D] * _sigmoid(a[:, D:])

    return _rows(name, body, T, 512, [(h, 'cur')], [w_pw1, b_pw1], [(2 * D, F32), (D, F32)])


def _layernorm_parts(uc):
    xc = uc - jnp.mean(uc, axis=-1, keepdims=True)
    r = lax.rsqrt(jnp.mean(xc * xc, axis=-1, keepdims=True) + EPS)
    return xc * r, r


def _conv_dw(name, u, w_dw, b_dw, ln_g, ln_b, w_pw2, b_pw2, x, g1, gt):
    T, D = u.shape
    tm = min(256, T)

    def body(i, n, rr, cc, oo, aa, ss):
        ext = ss[0]
        ext[0:HALO, :] = jnp.where(i > 0, rr[1][tm - HALO:tm, :], 0.0)
        ext[HALO:HALO + tm, :] = rr[0][...]
        uc = jnp.zeros((tm, D), F32) + cc[1][...]
        for kk in range(CONV_W):
            uc = uc + ext[pl.ds(HALO - (CONV_W - 1) + kk, tm), :] * cc[0][kk:kk + 1, :]
        xh, _ = _layernorm_parts(uc)
        ln = xh * cc[2][...] + cc[3][...]
        z = (ln * _sigmoid(ln)).astype(MM)
        y = _dot(z, cc[4][...]) + cc[5][...]
        oo[0][...] = uc
        oo[1][...] = z
        oo[2][...] = y
        oo[3][...] = rr[2][...] + cc[7][...] * _rms(y, cc[6][...])

    return _rows(name, body, T, tm, [(u, 'cur'), (u, 'prev'), (x, 'cur')], [w_dw, b_dw, ln_g, ln_b, w_pw2, b_pw2, g1, gt],
                 [(D, F32), (D, MM), (D, F32), (D, F32)], scratch=[pltpu.VMEM((tm + HALO, D), F32)])


def _conv_bwd1(name, dy, uc, w_pw2, ln_g, ln_b):
    T, D = uc.shape

    def body(i, n, rr, cc, oo, aa, ss):
        dz = _dot_nt(rr[0][...], cc[0][...])
        xh, r = _layernorm_parts(rr[1][...])
        g = cc[1][...]
        ln = xh * g + cc[2][...]
        sg = _sigmoid(ln)
        dln = dz * (sg * (1.0 + ln * (1.0 - sg)))
        aa[0][...] += jnp.sum(dln * xh, axis=0, keepdims=True)
        aa[1][...] += jnp.sum(dln, axis=0, keepdims=True)
        dxh = dln * g
        duc = r * (dxh - jnp.mean(dxh, axis=-1, keepdims=True) - xh * jnp.mean(dxh * xh, axis=-1, keepdims=True))
        aa[2][...] += jnp.sum(duc, axis=0, keepdims=True)
        oo[0][...] = duc

    return _rows(name, body, T, 256, [(dy, 'cur'), (uc, 'cur')], [w_pw2, ln_g, ln_b], [(D, F32)], accs=[(1, D)] * 3)


def _conv_bwd2(name, duc, u, a, x, dxo, w_dw, w_pw1, g0, sc):
    T, D = u.shape
    tm = min(256, T)

    def body(i, n, rr, cc, oo, aa, ss):
        extd, extu = ss[0], ss[1]
        dcur = rr[0][...]
        extd[0:tm, :] = dcur
        extd[tm:tm + HALO, :] = jnp.where(i < n - 1, rr[1][0:HALO, :], 0.0)
        extu[0:HALO, :] = jnp.where(i > 0, rr[3][tm - HALO:tm, :], 0.0)
        extu[HALO:HALO + tm, :] = rr[2][...]
        du = jnp.zeros((tm, D), F32)
        for kk in range(CONV_W):
            du = du + extd[pl.ds(CONV_W - 1 - kk, tm), :] * cc[0][kk:kk + 1, :]
            aa[0][kk:kk + 1, :] += jnp.sum(dcur * extu[pl.ds(HALO - (CONV_W - 1) + kk, tm), :], axis=0, keepdims=True)
        av = rr[4][...]
        a1, sg = av[:, :D], _sigmoid(av[:, D:])
        da = jnp.concatenate([du * sg, du * a1 * (sg * (1.0 - sg))], axis=1)
        aa[1][...] += jnp.sum(da, axis=0, keepdims=True)
        da = da.astype(MM)
        oo[0][...] = da
        dx, dsh, dsc, dg0 = _prenorm_bwd(rr[5][...], cc[2][...], cc[3][...], _dot_nt(da, cc[1][...]))
        oo[1][...] = rr[6][...] + dx
        aa[2][...] += dsh
        aa[3][...] += dsc
        aa[4][...] += dg0

    return _rows(name, body, T, tm,
                 [(duc, 'cur'), (duc, 'next'), (u, 'cur'), (u, 'prev'), (a, 'cur'), (x, 'cur'), (dxo, 'cur')],
                 [w_dw, w_pw1, g0, sc], [(2 * D, MM), (D, F32)],
                 accs=[(32, D), (1, 2 * D), (1, D), (1, D), (1, D)],
                 scratch=[pltpu.VMEM((tm + HALO, D), F32), pltpu.VMEM((tm + HALO, D), F32)])


PHALO = 16


def _pool_fwd(name, h, w, b, scale, x, g1, gt):
    T, D = h.shape
    G = len(POOL_WINDOWS)
    Cg = D // G
    tm = min(256, T)

    def body(i, n, rr, cc, oo, aa, ss):
        ext = ss[0]
        ext[0:PHALO, :] = jnp.where(i > 0, rr[1][tm - PHALO:tm, :], 0.0)
        ext[PHALO:PHALO + tm, :] = rr[0][...]
        t_glob = i * tm + lax.broadcasted_iota(jnp.int32, (tm, 1), 0)
        ps, ys = [], []
        for g, win in enumerate(POOL_WINDOWS):
            cols = slice(g * Cg, (g + 1) * Cg)
            s = ext[pl.ds(PHALO, tm), cols]
            for j in range(1, win):
                s = s + ext[pl.ds(PHALO - j, tm), cols]
            cnt = jnp.minimum(t_glob + 1, win).astype(F32)
            p = (s / cnt - ext[pl.ds(PHALO, tm), cols]).astype(MM)
            ps.append(p)
            ys.append(_dot(p, cc[0][g]) + cc[1][:, cols])
        ypre = jnp.concatenate(ys, axis=1)
        y = ypre * cc[2][...]
        oo[0][...] = jnp.concatenate(ps, axis=1)
        oo[1][...] = ypre
        oo[2][...] = y
        oo[3][...] = rr[2][...] + cc[4][...] * _rms(y, cc[3][...])

    return _rows(name, body, T, tm, [(h, 'cur'), (h, 'prev'), (x, 'cur')], [w, b, scale, g1, gt],
                 [(D, MM), (D, F32), (D, F32), (D, F32)], scratch=[pltpu.VMEM((tm + PHALO, D), F32)])


def _pool_bwd1(name, dy, ypre, scale, w):
    T, D = ypre.shape
    G = len(POOL_WINDOWS)
    Cg = D // G

    def body(i, n, rr, cc, oo, aa, ss):
        dyv = rr[0][...].astype(F32)
        aa[0][...] += jnp.sum(dyv * rr[1][...], axis=0, keepdims=True)
        dypre = dyv * cc[0][...]
        aa[1][...] += jnp.sum(dypre, axis=0, keepdims=True)
        dypre = dypre.astype(MM)
        oo[1][...] = dypre
        oo[0][...] = jnp.concatenate([_dot_nt(dypre[:, g * Cg:(g + 1) * Cg], cc[1][g]) for g in range(G)], axis=1)

    return _rows(name, body, T, 256, [(dy, 'cur'), (ypre, 'cur')], [scale, w], [(D, F32), (D, MM)], accs=[(1, D)] * 2)


def _pool_bwd2(name, dp, x, dxo, g0, sc):
    T, D = x.shape
    G = len(POOL_WINDOWS)
    Cg = D // G
    tm = min(256, T)

    def body(i, n, rr, cc, oo, aa, ss):
        ext = ss[0]
        t_glob = i * tm + lax.broadcasted_iota(jnp.int32, (tm, 1), 0)
        dcur = rr[0][...]
        dhs = []
        for g, win in enumerate(POOL_WINDOWS):
            cols = slice(g * Cg, (g + 1) * Cg)
            cnt = jnp.minimum(t_glob + 1, win).astype(F32)
            ext[0:tm, cols] = dcur[:, cols] / cnt
            ext[tm:tm + PHALO, cols] = jnp.where(i < n - 1, rr[1][0:PHALO, cols] * (1.0 / win), 0.0)
        for g, win in enumerate(POOL_WINDOWS):
            cols = slice(g * Cg, (g + 1) * Cg)
            s = ext[pl.ds(0, tm), cols]
            for j in range(1, win):
                s = s + ext[pl.ds(j, tm), cols]
            dhs.append(s - dcur[:, cols])
        dx, dsh, dsc, dg0 = _prenorm_bwd(rr[2][...], cc[0][...], cc[1][...], jnp.concatenate(dhs, axis=1))
        oo[0][...] = rr[3][...] + dx
        aa[0][...] += dsh
        aa[1][...] += dsc
        aa[2][...] += dg0

    return _rows(name, body, T, tm, [(dp, 'cur'), (dp, 'next'), (x, 'cur'), (dxo, 'cur')], [g0, sc], [(D, F32)],
                 accs=[(1, D)] * 3, scratch=[pltpu.VMEM((tm + PHALO, D), F32)])


def _loss_head(x, tgt):
    T, D = x.shape

    def body(i, n, rr, cc, oo, aa, ss):
        err = rr[0][...] - rr[1][...]
        oo[0][...] = err * (1.0 / D)
        aa[0][...] += jnp.sum(err * err, axis=0, keepdims=True)

        @pl.when(i == n - 1)
        def _():
            aa[1][...] = jnp.broadcast_to(jnp.sum(aa[0][...], axis=1, keepdims=True) * (0.5 / D), (1, 128))

    dx, _, loss_row = _rows("loss_head", body, T, 512, [(x, 'cur'), (tgt, 'cur')], [], [(D, F32)], accs=[(1, D), (1, 128)])
    return dx, loss_row


def _adamw(name, w, g, m, v):
    shape = w.shape
    C = shape[-1]
    R = w.size // C
    w2, g2, m2, v2 = (t.reshape(R, C) for t in (w, g, m, v))
    br = R
    if R * C * 4 > (1 << 20):
        br = 8
        while br * 2 * C * 4 <= (1 << 20) and R % (br * 2) == 0:
            br *= 2
    b1c = 1.0 - ADAM_B1 ** ADAM_STEP
    b2c = 1.0 - ADAM_B2 ** ADAM_STEP

    def body(w_ref, g_ref, m_ref, v_ref, d_ref, mo_ref, vo_ref):
        gv = g_ref[...]
        mn = ADAM_B1 * m_ref[...] + (1.0 - ADAM_B1) * gv
        vn = ADAM_B2 * v_ref[...] + (1.0 - ADAM_B2) * (gv * gv)
        d_ref[...] = -ADAM_LR * ((mn / b1c) / (jnp.sqrt(vn / b2c) + ADAM_EPS) + ADAM_WD * w_ref[...])
        mo_ref[...] = mn
        vo_ref[...] = vn

    spec = pl.BlockSpec((br, C), lambda r: (r, 0))
    outs = pl.pallas_call(body, grid=(R // br,), in_specs=[spec] * 4, out_specs=[spec] * 3,
                          out_shape=[jax.ShapeDtypeStruct((R, C), F32)] * 3, name=name,
                          compiler_params=_cparams(("arbitrary",)))(w2, g2, m2, v2)
    return tuple(t.reshape(shape) for t in outs)


def _layer_shards(g, ax):
    s = g.shape
    r = g.reshape(s[:ax] + (N_DEV, s[ax] // N_DEV) + s[ax + 1:])
    return (jnp.moveaxis(r, ax, 0) if ax else r).reshape(N_DEV, -1)


def _unshard(g, ax):
    r = jnp.moveaxis(g, 0, ax)
    s = r.shape
    return r.reshape(s[:ax] + (s[ax] * s[ax + 1],) + s[ax + 2:])


def _pack(parts, dtype, row_mult):
    lead = parts[0].shape[:-1]
    flat = jnp.concatenate([p.astype(dtype) for p in parts], axis=-1)
    n = flat.shape[-1]
    per = row_mult * 1024
    tot = -(-n // per) * per
    flat = jnp.pad(flat, [(0, 0)] * len(lead) + [(0, tot - n)])
    return flat.reshape(lead + (tot // 1024, 1024))


def _pad_heads(w, lo, hi):
    K = w.shape[0]
    r = w.reshape(K, N_HEADS, -1)[:, :, lo:hi]
    return jnp.pad(r, ((0, 0), (0, 0), (0, HEAD_PAD - (hi - lo)))).reshape(K, N_HEADS * HEAD_PAD)


def kernel(x, c, positions, ada_w, ada_b, norm_g, mla_w_dq, mla_q_norm_g, mla_w_uq, mla_w_dkv, mla_kv_norm_g, mla_w_ukv, mla_w_o, conv_w_pw1, conv_b_pw1, conv_w_dw, conv_b_dw, conv_ln_g, conv_ln_b, conv_w_pw2, conv_b_pw2, pool_w, pool_b, pool_scale, ffn_w1, ffn_w2, loss_target, m_ada_w, m_ada_b, m_norm_g, m_mla_w_dq, m_mla_q_norm_g, m_mla_w_uq, m_mla_w_dkv, m_mla_kv_norm_g, m_mla_w_ukv, m_mla_w_o, m_conv_w_pw1, m_conv_b_pw1, m_conv_w_dw, m_conv_b_dw, m_conv_ln_g, m_conv_ln_b, m_conv_w_pw2, m_conv_b_pw2, m_pool_w, m_pool_b, m_pool_scale, m_ffn_w1, m_ffn_w2, v_ada_w, v_ada_b, v_norm_g, v_mla_w_dq, v_mla_q_norm_g, v_mla_w_uq, v_mla_w_dkv, v_mla_kv_norm_g, v_mla_w_ukv, v_mla_w_o, v_conv_w_pw1, v_conv_b_pw1, v_conv_w_dw, v_conv_b_dw, v_conv_ln_g, v_conv_ln_b, v_conv_w_pw2, v_conv_b_pw2, v_pool_w, v_pool_b, v_pool_scale, v_ffn_w1, v_ffn_w2):
    args = dict(locals())
    W = {n: args[n] for n, _ in WEIGHTS}
    M1 = {n: args['m_' + n] for n, _ in WEIGHTS}
    V2 = {n: args['v_' + n] for n, _ in WEIGHTS}
    D = D_MODEL
    T = x.shape[1]
    L = ffn_w1.shape[0]
    xi, yi, ci = _place()
    me = 4 * xi + 2 * yi + ci
    n_ada = ada_w.shape[2]

    small_sizes = [W[n].size for n in SMALL]
    small_in = _pack([c.reshape(-1)] + [W[n].reshape(-1) for n in SMALL], F32, 8)
    small_all = _ag_small("ag_small_params", small_in).reshape(N_DEV, -1)
    c_all = small_all[:, :D]
    Ws = {}
    off = D
    for n, sz in zip(SMALL, small_sizes):
        Ws[n] = _unshard(small_all[:, off:off + sz].reshape((N_DEV,) + W[n].shape), SHARD_AXIS[n])
        off += sz
    c16 = jnp.pad(c_all, ((0, 16 - N_DEV), (0, 0)))

    ada_b_cols = lax.dynamic_slice_in_dim(ada_b, me * n_ada, n_ada, axis=1).reshape(L, 1, n_ada)
    mod_part = _mod_part(c16, ada_w, ada_b_cols)[:, :N_DEV]
    mod_all = _ag_small("ag_mod", mod_part.reshape(L * N_DEV, n_ada)).reshape(N_DEV, L, N_DEV, n_ada)
    mod_mine = lax.dynamic_index_in_dim(mod_all, me, axis=2, keepdims=False)
    mod = jnp.transpose(mod_mine, (1, 0, 2)).reshape(L, 6, 1, D)

    rest = [n for n in BIG if not n.startswith('ffn')]
    rest_all, = _ag_big("ag_weights", [_pack([W[n].reshape(-1) for n in rest], MM, 32)])
    wf = [ffn_w1.astype(MM), ffn_w2.astype(MM)]
    wf, rest_all, mod = lax.optimization_barrier((wf, rest_all, mod))
    wf_land = [lax.dynamic_update_slice(lax.empty((N_DEV,) + w.shape, MM), w[None], (me, 0, 0, 0)) for w in wf]
    ag_sems, wf_thru, wf_land, ag_token = _copies_start("ag_ffn_start", wf, wf_land, FIRST_LEVEL_PEERS, False)
    rest_all = rest_all.reshape(N_DEV, -1)
    Wb = {}
    off = 0
    for n in rest:
        Wb[n] = rest_all[:, off:off + W[n].size].reshape((N_DEV,) + W[n].shape)
        off += W[n].size
    full = lambda n: _unshard(Wb[n], SHARD_AXIS[n])
    w_dq, w_uq, w_dkv, w_ukv, w_o = full('mla_w_dq'), full('mla_w_uq'), full('mla_w_dkv'), full('mla_w_ukv'), full('mla_w_o')
    w_pw1, w_pw2, w_pool = full('conv_w_pw1'), full('conv_w_pw2'), full('pool_w')
    n_mla = w_dq.shape[0]
    w_uq_p = [_pad_heads(w_uq[j], 0, QK_NOPE + QK_ROPE) for j in range(n_mla)]
    w_ukv_k = [_pad_heads(w_ukv[j], 0, QK_NOPE) for j in range(n_mla)]
    w_ukv_v = [_pad_heads(w_ukv[j], QK_NOPE, QK_NOPE + V_HEAD) for j in range(n_mla)]
    w_dkv_p = [jnp.pad(jnp.concatenate([w_dkv[j][:, :KV_LORA], jnp.zeros((D, QK_NOPE), MM), w_dkv[j][:, KV_LORA:]], axis=1),
                       ((0, 0), (0, HEAD_PAD - QK_NOPE - QK_ROPE))) for j in range(n_mla)]
    w_o_p = [jnp.pad(w_o[j].reshape(N_HEADS, V_HEAD, D), ((0, 0), (0, HEAD_PAD - V_HEAD), (0, 0))).reshape(N_HEADS * HEAD_PAD, D)
             for j in range(n_mla)]
    w_dw32 = jnp.pad(Ws['conv_w_dw'], ((0, 0), (0, 32 - CONV_W), (0, 0)))
    row = lambda t: t.reshape(1, -1)

    half = QK_ROPE // 2
    inv_freq = ROPE_THETA ** (-jnp.arange(0, QK_ROPE, 2, dtype=F32) / QK_ROPE)
    invf = jnp.zeros((1, HEAD_PAD), F32).at[0, QK_NOPE:QK_NOPE + half].set(inv_freq).at[0, QK_NOPE + half:QK_NOPE + QK_ROPE].set(inv_freq)
    rC, rS1, rS2 = _rope_tables(positions.reshape(T, 1).astype(F32), invf)

    xs = x.reshape(T, D)
    saved = []
    for i in range(L):
        kind, j = i % 3, i // 3
        sh_m, sc_m, gt_m, sh_f, sc_f, gt_f = (mod[i, r] for r in range(6))
        g = [row(Ws['norm_g'][i, r]) for r in range(4)]
        st = dict(x0=xs)
        if i == 0:
            sc_m = sc_m + ag_token[0:1, 0:1]
        if kind == 0:
            h = _prenorm(f"prenorm_m{i}", xs, g[0], sc_m, sh_m, MM)
            cq_raw, cq, ckv_raw, ckv, q, k, v = _mla_proj(f"mla_proj{i}", h, rC, rS1, rS2, w_dq[j], row(Ws['mla_q_norm_g'][j]), w_uq_p[j],
                                                          w_dkv_p[j], row(Ws['mla_kv_norm_g'][j]), w_ukv_k[j], w_ukv_v[j])
            o, lse = _attn_fwd(f"attn_fwd{i}", q, k, v)
            y, xs = _mm_post(f"mla_out{i}", o, w_o_p[j], None, xs, g[1], gt_m)
            st.update(h=h, cq_raw=cq_raw, cq=cq, ckv_raw=ckv_raw, ckv=ckv, q=q, k=k, v=v, o=o, lse=lse, y=y)
        elif kind == 1:
            h = _prenorm(f"prenorm_m{i}", xs, g[0], sc_m, sh_m, MM)
            a, u = _conv_glu(f"conv_glu{i}", h, w_pw1[j], row(W['conv_b_pw1'][j]))
            uc, z, y, xs = _conv_dw(f"conv_dw{i}", u, w_dw32[j], row(W['conv_b_dw'][j]), row(W['conv_ln_g'][j]), row(W['conv_ln_b'][j]),
                                    w_pw2[j], row(W['conv_b_pw2'][j]), xs, g[1], gt_m)
            st.update(h=h, a=a, u=u, uc=uc, z=z, y=y)
        else:
            h = _prenorm(f"prenorm_m{i}", xs, g[0], sc_m, sh_m, F32)
            p, ypre, y, xs = _pool_fwd(f"pool_fwd{i}", h, w_pool[j], row(Ws['pool_b'][j]), row(Ws['pool_scale'][j]), xs, g[1], gt_m)
            st.update(p=p, ypre=ypre, y=y)
        st['x1'] = xs
        if i == 0:
            wg = _copies_wait("ag_ffn_wait", ag_sems, wf_thru, wf_land, xs, FIRST_LEVEL_PEERS, False)
            w1g, w2g = _ag_forward("ag_ffn_forward", wg)
        hf = _prenorm(f"prenorm_f{i}", xs, g[2], sc_f, sh_f, MM)
        af, yf, xs = _ffn_fwd(f"ffn_fwd{i}", i, hf, w1g, w2g, xs, g[3], gt_f)
        st.update(hf=hf, af=af, yf=yf)
        saved.append(st)

    dx, loss_row = _loss_head(xs, loss_target.reshape(T, D))

    G = {}
    dmod = [None] * L
    dnorm = [None] * L
    rs_pending = None
    ffn_red = [None] * L
    for i in reversed(range(L)):
        kind, j = i % 3, i // 3
        sh_m, sc_m, gt_m, sh_f, sc_f, gt_f = (mod[i, r] for r in range(6))
        g = [row(Ws['norm_g'][i, r]) for r in range(4)]
        st = saved[i]
        dy, dg3, dgt_f, _ = _post_bwd(f"post_bwd_f{i}", dx, st['yf'], g[3], gt_f)
        da, dx, dsh_f, dsc_f, dg2 = _ffn_bwd(f"ffn_bwd{i}", i, dy, st['af'], w1g, w2g, st['x1'], dx, g[2], sc_f)
        wire1, own1 = _mm_tn_wire(f"ffn_dw1_{i}", st['hf'], da, me, False, False)
        wire2, own2 = _mm_tn_wire(f"ffn_dw2_{i}", st['af'], dy, me, True, True)
        if rs_pending is not None:
            ffn_red[rs_pending[0]] = _rs_finish(rs_pending, wire2, me)
        wires = [wire1, wire2]
        rs_sems, wires_thru, rs_lands, rs_token = _copies_start(f"rs_start{i}", wires, [lax.empty(w.shape, MM) for w in wires], ALL_PEERS, True)
        rs_pending = (i, rs_sems, wires_thru, rs_lands, [own1, own2])
        dy, dg1, dgt_m, dysum = _post_bwd(f"post_bwd_m{i}", dx, st['y'], g[1], gt_m + rs_token[0:1, 0:1])
        if kind == 0:
            do = _mm_nt_rows(f"mla_do{i}", dy, w_o_p[j])
            delta = _attn_delta(f"attn_delta{i}", do, st['o'])
            dq, dk, dv = _attn_bwd(f"attn_bwd{i}", st['q'], st['k'], st['v'], do, st['lse'], delta)
            dq_pre, dcq_raw, dckv_all, dx, dqg, dkvg, dsh_m, dsc_m, dg0 = _mla_proj_bwd(
                f"mla_proj_bwd{i}", dq, dk, dv, rC, rS1, rS2, st['cq_raw'], st['ckv_raw'], st['x0'], dx, w_uq_p[j], w_ukv_k[j], w_ukv_v[j],
                w_dq[j], w_dkv_p[j], row(Ws['mla_q_norm_g'][j]), row(Ws['mla_kv_norm_g'][j]), g[0], sc_m)
            dwo = _mm_tn(f"mla_dwo{i}", st['o'], dy)
            dwuq = _mm_tn(f"mla_dwuq{i}", st['cq'], dq_pre)
            dwk = _mm_tn(f"mla_dwukvk{i}", st['ckv'], dk)
            dwv = _mm_tn(f"mla_dwukvv{i}", st['ckv'], dv)
            dwdq = _mm_tn(f"mla_dwdq{i}", st['h'], dcq_raw)
            dwdkv = _mm_tn(f"mla_dwdkv{i}", st['h'], dckv_all)
            G.setdefault('mla_w_o', [None] * n_mla)[j] = dwo.reshape(N_HEADS, HEAD_PAD, D)[:, :V_HEAD].reshape(N_HEADS * V_HEAD, D)
            G.setdefault('mla_w_uq', [None] * n_mla)[j] = dwuq.reshape(Q_LORA, N_HEADS, HEAD_PAD)[:, :, :QK_NOPE + QK_ROPE].reshape(Q_LORA, -1)
            G.setdefault('mla_w_ukv', [None] * n_mla)[j] = jnp.concatenate(
                [dwk.reshape(KV_LORA, N_HEADS, HEAD_PAD)[:, :, :QK_NOPE], dwv.reshape(KV_LORA, N_HEADS, HEAD_PAD)[:, :, :V_HEAD]], axis=2).reshape(KV_LORA, -1)
            G.setdefault('mla_w_dq', [None] * n_mla)[j] = dwdq
            G.setdefault('mla_w_dkv', [None] * n_mla)[j] = jnp.concatenate([dwdkv[:, :KV_LORA], dwdkv[:, KV_LORA + QK_NOPE:KV_LORA + QK_NOPE + QK_ROPE]], axis=1)
            G.setdefault('mla_q_norm_g', [None] * n_mla)[j] = dqg[0]
            G.setdefault('mla_kv_norm_g', [None] * n_mla)[j] = dkvg[0]
        elif kind == 1:
            duc, dlng, dlnb, dbdw = _conv_bwd1(f"conv_bwd1_{i}", dy, st['uc'], w_pw2[j], row(W['conv_ln_g'][j]), row(W['conv_ln_b'][j]))
            da, dx, dwdw, dbpw1, dsh_m, dsc_m, dg0 = _conv_bwd2(f"conv_bwd2_{i}", duc, st['u'], st['a'], st['x0'], dx, w_dw32[j], w_pw1[j], g[0], sc_m)
            G['conv_w_pw2'] = [_mm_tn(f"conv_dwpw2_{i}", st['z'], dy)]
            G['conv_w_pw1'] = [_mm_tn(f"conv_dwpw1_{i}", st['h'], da, col_shards=N_DEV).reshape(N_DEV, -1)]
            G['conv_w_dw'] = [dwdw[:CONV_W]]
            G['conv_b_pw1'], G['conv_b_dw'], G['conv_ln_g'], G['conv_ln_b'], G['conv_b_pw2'] = [dbpw1[0]], [dbdw[0]], [dlng[0]], [dlnb[0]], [dysum[0]]
        else:
            dp, dypre, dscale, dpb = _pool_bwd1(f"pool_bwd1_{i}", dy, st['ypre'], row(Ws['pool_scale'][j]), w_pool[j])
            dx, dsh_m, dsc_m, dg0 = _pool_bwd2(f"pool_bwd2_{i}", dp, st['x0'], dx, g[0], sc_m)
            G['pool_w'] = [_mm_tn(f"pool_dw{i}", st['p'], dypre, diag=len(POOL_WINDOWS))]
            G['pool_b'] = [dpb.reshape(len(POOL_WINDOWS), -1)]
            G['pool_scale'] = [dscale[0]]
        dmod[i] = jnp.concatenate([dsh_m, dsc_m, dgt_m, dsh_f, dsc_f, dgt_f], axis=1)
        dnorm[i] = jnp.concatenate([dg0, dg1, dg2, dg3], axis=0)
    G['norm_g'] = dnorm
    grad_x = dx.reshape(x.shape)

    rs_names = [n for n, ax in WEIGHTS if ax is not None and n != 'ada_w' and not n.startswith('ffn')]
    parts = [g if n == 'conv_w_pw1' else _layer_shards(g, SHARD_AXIS[n] - 1) for n in rs_names for g in G[n]]
    ffn_red[rs_pending[0]] = _rs_finish(rs_pending, dx, me)
    red = _reduce_scatter("rs", [_pack(parts, F32, RS_ROWS)], ci, 2 * xi + yi)
    grads = {'ffn_w1': jnp.stack([r_[0] for r_ in ffn_red], axis=0), 'ffn_w2': jnp.stack([r_[1] for r_ in ffn_red], axis=0)}
    red0 = red[0].reshape(-1)
    off = 0
    for n in rs_names:
        grads[n] = red0[off:off + W[n].size].reshape(W[n].shape)
        off += W[n].size

    dmod_mine = jnp.concatenate(dmod, axis=1).reshape(-1)
    fin_in = _pack([dmod_mine] + [G[n][0].reshape(-1) for n in REPL] + [loss_row.reshape(-1)], F32, 8)
    fin_all = _ag_small("ag_final", fin_in)
    fin_sum = _sum_devices("final_sum", fin_all).reshape(-1)
    nm = L * 6 * D
    grads['ada_b'] = fin_sum[:nm].reshape(L, 6 * D)
    off = nm
    for n in REPL:
        grads[n] = fin_sum[off:off + W[n].size].reshape(W[n].shape)
        off += W[n].size
    loss = fin_sum[off]
    dmod_all = fin_all.reshape(N_DEV, -1)[:, :nm].reshape(N_DEV, L, 6 * D)
    dmod_cols = lax.dynamic_slice_in_dim(dmod_all, me * n_ada, n_ada, axis=2)
    dmod16 = jnp.pad(jnp.transpose(dmod_cols, (1, 0, 2)), ((0, 0), (0, 16 - N_DEV), (0, 0)))
    grads['ada_w'] = _ada_w_grad(c16, dmod16)

    deltas, new_m, new_v = {}, {}, {}
    for n, _ in WEIGHTS:
        deltas[n], new_m[n], new_v[n] = _adamw("adamw_" + n, W[n], grads[n], M1[n], V2[n])
    names = [n for n, _ in WEIGHTS]
    return (loss, grad_x, *[grads[n] for n in names], *[deltas[n] for n in names], *[new_m[n] for n in names],
            *[new_v[n] for n in names])
```

```python
import functools
import math

import jax
import jax.numpy as jnp
from jax import lax
from jax.experimental import pallas as pl
from jax.experimental.pallas import tpu as pltpu

F32 = jnp.float32
MM = jnp.bfloat16
EPS = 1e-6
NEG = -1e30
N_DEV = 8
VMEM_LIMIT = 48 * 1024 * 1024
MESH = pl.DeviceIdType.MESH

D_MODEL = 1024
N_HEADS = 16
HEAD_PAD = 128
QK_NOPE, QK_ROPE, V_HEAD = 64, 32, 64
Q_LORA, KV_LORA = 384, 256
CHUNK = 64
CONV_W = 31
POOL_WINDOWS = (2, 4, 8, 16)
ROPE_THETA = 10000.0
ATT_SCALE = 1.0 / math.sqrt(QK_NOPE + QK_ROPE)

ADAM_LR, ADAM_B1, ADAM_B2, ADAM_EPS, ADAM_WD, ADAM_STEP = 0.001, 0.9, 0.999, 1e-08, 0.01, 10

WEIGHTS = [('ada_w', 2), ('ada_b', None), ('norm_g', 2), ('mla_w_dq', 1), ('mla_q_norm_g', 1), ('mla_w_uq', 2),
           ('mla_w_dkv', 1), ('mla_kv_norm_g', 1), ('mla_w_ukv', 2), ('mla_w_o', 1), ('conv_w_pw1', 2),
           ('conv_b_pw1', None), ('conv_w_dw', 2), ('conv_b_dw', None), ('conv_ln_g', None), ('conv_ln_b', None),
           ('conv_w_pw2', 1), ('conv_b_pw2', None), ('pool_w', 2), ('pool_b', 2), ('pool_scale', 1),
           ('ffn_w1', 2), ('ffn_w2', 1)]
SHARD_AXIS = dict(WEIGHTS)
BIG = ['mla_w_dq', 'mla_w_uq', 'mla_w_dkv', 'mla_w_ukv', 'mla_w_o', 'conv_w_pw1', 'conv_w_pw2', 'pool_w', 'ffn_w1', 'ffn_w2']
SMALL = ['norm_g', 'mla_q_norm_g', 'mla_kv_norm_g', 'conv_w_dw', 'pool_b', 'pool_scale']
REPL = ['conv_b_pw1', 'conv_b_dw', 'conv_ln_g', 'conv_ln_b', 'conv_b_pw2']


def _dot(a, b):
    return jnp.dot(a.astype(MM), b.astype(MM), preferred_element_type=F32)


def _dot_nt(a, b):
    return lax.dot_general(a.astype(MM), b.astype(MM), (((1,), (1,)), ((), ())), preferred_element_type=F32)


def _dot_tn(a, b):
    return lax.dot_general(a.astype(MM), b.astype(MM), (((0,), (0,)), ((), ())), preferred_element_type=F32)


def _sigmoid(x):
    return 1.0 / (1.0 + jnp.exp(-x))


def _rstd(x):
    return lax.rsqrt(jnp.mean(x * x, axis=-1, keepdims=True) + EPS)


def _rms(x, g):
    return x * _rstd(x) * g


def _rms_bwd(x, g, dout):
    r = _rstd(x)
    xn = x * r
    dg = jnp.sum(dout * xn, axis=0, keepdims=True)
    dxn = dout * g
    dx = r * (dxn - xn * jnp.mean(dxn * xn, axis=-1, keepdims=True))
    return dx, dg


def _prenorm_bwd(x, g0, sc, dh):
    r = _rstd(x)
    xn = x * r
    dsh = jnp.sum(dh, axis=0, keepdims=True)
    dsc = jnp.sum(dh * (xn * g0), axis=0, keepdims=True)
    dn = dh * (1.0 + sc)
    dg0 = jnp.sum(dn * xn, axis=0, keepdims=True)
    dxn = dn * g0
    dx = r * (dxn - xn * jnp.mean(dxn * xn, axis=-1, keepdims=True))
    return dx, dsh, dsc, dg0


def _cparams(sem):
    return pltpu.CompilerParams(dimension_semantics=sem, vmem_limit_bytes=VMEM_LIMIT)


def _rows(name, body, n_rows, tm, rows, consts, outs, accs=(), scratch=()):
    tm = min(tm, n_rows)
    nblk = n_rows // tm
    nr, nc, no, na = len(rows), len(consts), len(outs), len(accs)
    in_specs, args = [], []
    for a, kind in rows:
        if kind == 'cur':
            im = lambda i: (i, 0)
        elif kind == 'prev':
            im = lambda i: (jnp.maximum(i - 1, 0), 0)
        else:
            im = lambda i: (jnp.minimum(i + 1, nblk - 1), 0)
        in_specs.append(pl.BlockSpec((tm, a.shape[1]), im))
        args.append(a)
    for a in consts:
        in_specs.append(pl.BlockSpec(a.shape, lambda i, nd=a.ndim: (0,) * nd))
        args.append(a)
    out_specs = [pl.BlockSpec((tm, c), lambda i: (i, 0)) for c, _ in outs]
    out_specs += [pl.BlockSpec(s, lambda i, nd=len(s): (0,) * nd) for s in accs]
    out_shape = [jax.ShapeDtypeStruct((n_rows, c), dt) for c, dt in outs]
    out_shape += [jax.ShapeDtypeStruct(s, F32) for s in accs]

    def kern(*refs):
        i = pl.program_id(0)
        rr = refs[:nr]
        cc = refs[nr:nr + nc]
        oo = refs[nr + nc:nr + nc + no]
        aa = refs[nr + nc + no:nr + nc + no + na]
        ss = refs[nr + nc + no + na:]

        @pl.when(i == 0)
        def _():
            for a in aa:
                a[...] = jnp.zeros(a.shape, F32)

        body(i, nblk, rr, cc, oo, aa, ss)

    return pl.pallas_call(kern, grid=(nblk,), in_specs=in_specs, out_specs=out_specs, out_shape=out_shape,
                          scratch_shapes=list(scratch), name=name, compiler_params=_cparams(("arbitrary",)))(*args)


def _place():
    return lax.axis_index("x"), lax.axis_index("y"), lax.axis_index("c")


def _ag_small(name, xs):
    R, C = xs.shape

    def body(x_ref, out_ref, send_sems, recv_sems):
        x, y, c = _place()
        me = 4 * x + 2 * y + c
        out_ref[me] = x_ref[...]
        copies = []
        for k in range(1, N_DEV):
            peer = ((1 - x) if k & 4 else x, (1 - y) if k & 2 else y, (1 - c) if k & 1 else c)
            cp = pltpu.make_async_remote_copy(src_ref=x_ref, dst_ref=out_ref.at[me], send_sem=send_sems.at[k - 1],
                                              recv_sem=recv_sems.at[k - 1], device_id=peer, device_id_type=MESH)
            cp.start()
            copies.append(cp)
        for cp in copies:
            cp.wait()

    return pl.pallas_call(
        body, out_shape=jax.ShapeDtypeStruct((N_DEV, R, C), xs.dtype),
        in_specs=[pl.BlockSpec(memory_space=pltpu.VMEM)], out_specs=pl.BlockSpec(memory_space=pltpu.VMEM),
        scratch_shapes=[pltpu.SemaphoreType.DMA((N_DEV - 1,)), pltpu.SemaphoreType.DMA((N_DEV - 1,))], name=name)(xs)


def _ag_big(name, xs):
    nt = len(xs)

    def body(*refs):
        x_refs, out_refs = refs[:nt], refs[nt:2 * nt]
        send_sems, recv_sems, local_sems = refs[2 * nt:]
        x, y, c = _place()
        me, sibling = (x, y, c), (x, y, 1 - c)
        chips = [(1 - x, y), (x, 1 - y), (1 - x, 1 - y)]

        def copy(t, k, block, to, own=False):
            px, py, pc = block
            rows = out_refs[t].at[4 * px + 2 * py + pc]
            return pltpu.make_async_remote_copy(src_ref=x_refs[t] if own else rows, dst_ref=rows, send_sem=send_sems.at[7 * t + k],
                                                recv_sem=recv_sems.at[7 * t + k], device_id=to, device_id_type=MESH)

        mine = [pltpu.make_async_copy(x_refs[t], out_refs[t].at[4 * x + 2 * y + c], local_sems.at[t]) for t in range(nt)]
        for cp in mine:
            cp.start()
        first = []
        for t in range(nt):
            first.append(copy(t, 0, me, sibling, own=True))
            first += [copy(t, 1 + j, me, (*chip, c), own=True) for j, chip in enumerate(chips)]
        for cp in first:
            cp.start()
        passed = []
        for t in range(nt):
            for j, chip in enumerate(chips):
                copy(t, 1 + j, (*chip, c), me).wait_recv()
                cp = copy(t, 4 + j, (*chip, c), sibling)
                cp.start()
                passed.append(cp)
        for t in range(nt):
            copy(t, 0, sibling, me).wait_recv()
            for j, chip in enumerate(chips):
                copy(t, 4 + j, (*chip, 1 - c), me).wait_recv()
        for cp in first + passed:
            cp.wait_send()
        for cp in mine:
            cp.wait()

    hbm = pl.BlockSpec(memory_space=pl.ANY)
    return pl.pallas_call(
        body, out_shape=[jax.ShapeDtypeStruct((N_DEV,) + t.shape, t.dtype) for t in xs],
        in_specs=[hbm] * nt, out_specs=[hbm] * nt,
        scratch_shapes=[pltpu.SemaphoreType.DMA((7 * nt,)), pltpu.SemaphoreType.DMA((7 * nt,)), pltpu.SemaphoreType.DMA((nt,))],
        name=name)(*xs)


def _rs_pair(name, ps):
    nt = len(ps)

    def body(*refs):
        p_refs, recv_refs = refs[:nt], refs[nt:2 * nt]
        send_sems, recv_sems = refs[2 * nt:]
        x, y, c = _place()
        copies = []
        for t in range(nt):
            for j in range(4):
                cp = pltpu.make_async_remote_copy(src_ref=p_refs[t].at[j, 1 - c], dst_ref=recv_refs[t].at[j], send_sem=send_sems.at[4 * t + j],
                                                  recv_sem=recv_sems.at[4 * t + j], device_id=(x, y, 1 - c), device_id_type=MESH)
                cp.start()
                copies.append(cp)
        for cp in copies:
            cp.wait()

    hbm = pl.BlockSpec(memory_space=pl.ANY)
    return pl.pallas_call(
        body, out_shape=[jax.ShapeDtypeStruct((4,) + p.shape[2:], p.dtype) for p in ps], in_specs=[hbm] * nt, out_specs=[hbm] * nt,
        scratch_shapes=[pltpu.SemaphoreType.DMA((4 * nt,)), pltpu.SemaphoreType.DMA((4 * nt,))], name=name)(*ps)


def _rs_chips(name, ss):
    nt = len(ss)

    def body(*refs):
        s_refs, recv_refs = refs[:nt], refs[nt:2 * nt]
        send_sems, recv_sems, local_sems = refs[2 * nt:]
        x, y, c = _place()
        mine = 2 * x + y
        owns = [pltpu.make_async_copy(s_refs[t].at[mine], recv_refs[t].at[mine], local_sems.at[t]) for t in range(nt)]
        for cp in owns:
            cp.start()
        copies = []
        for t in range(nt):
            for k in range(1, 4):
                px = (1 - x) if k & 2 else x
                py = (1 - y) if k & 1 else y
                cp = pltpu.make_async_remote_copy(src_ref=s_refs[t].at[2 * px + py], dst_ref=recv_refs[t].at[mine],
                                                  send_sem=send_sems.at[3 * t + k - 1], recv_sem=recv_sems.at[3 * t + k - 1],
                                                  device_id=(px, py, c), device_id_type=MESH)
                cp.start()
                copies.append(cp)
        for cp in copies:
            cp.wait()
        for cp in owns:
            cp.wait()

    hbm = pl.BlockSpec(memory_space=pl.ANY)
    return pl.pallas_call(
        body, out_shape=[jax.ShapeDtypeStruct(s_.shape, s_.dtype) for s_ in ss], in_specs=[hbm] * nt, out_specs=[hbm] * nt,
        scratch_shapes=[pltpu.SemaphoreType.DMA((3 * nt,)), pltpu.SemaphoreType.DMA((3 * nt,)), pltpu.SemaphoreType.DMA((nt,))],
        name=name)(*ss)


RS_ROWS = 256


def _pair_sum(name, p, recv, my_c, my_chip):
    _, _, r, c = p.shape
    tr = RS_ROWS

    def body(sc_ref, p_ref, r_ref, o_ref, own_ref):
        s = p_ref[...] + r_ref[...]
        o_ref[...] = s.astype(MM)

        @pl.when(pl.program_id(1) == sc_ref[1])
        def _():
            own_ref[...] = s

    return pl.pallas_call(
        body, grid_spec=pltpu.PrefetchScalarGridSpec(
            num_scalar_prefetch=1, grid=(r // tr, 4),
            in_specs=[pl.BlockSpec((None, None, tr, c), lambda i, j, sc: (j, sc[0], i, 0)),
                      pl.BlockSpec((None, tr, c), lambda i, j, sc: (j, i, 0))],
            out_specs=[pl.BlockSpec((None, tr, c), lambda i, j, sc: (j, i, 0)), pl.BlockSpec((tr, c), lambda i, j, sc: (i, 0))]),
        out_shape=[jax.ShapeDtypeStruct((4, r, c), MM), jax.ShapeDtypeStruct((r, c), F32)], name=name,
        compiler_params=_cparams(("arbitrary", "arbitrary")))(jnp.stack([my_c, my_chip]), p, recv)


def _chip_sum(name, own, recv, my_chip):
    _, r, c = recv.shape
    tr = RS_ROWS

    def body(sc_ref, own_ref, r_ref, o_ref):
        acc = jnp.zeros((tr, c), F32)
        for j in range(4):
            acc = acc + jnp.where(sc_ref[0] == j, own_ref[...], r_ref[j].astype(F32))
        o_ref[...] = acc

    return pl.pallas_call(
        body, grid_spec=pltpu.PrefetchScalarGridSpec(
            num_scalar_prefetch=1, grid=(r // tr,),
            in_specs=[pl.BlockSpec((tr, c), lambda i, sc: (i, 0)), pl.BlockSpec((4, tr, c), lambda i, sc: (0, i, 0))],
            out_specs=pl.BlockSpec((tr, c), lambda i, sc: (i, 0))),
        out_shape=jax.ShapeDtypeStruct((r, c), F32), name=name,
        compiler_params=_cparams(("arbitrary",)))(my_chip.reshape(1), own, recv)


def _reduce_scatter(tag, tensors, my_c, my_chip):
    ps = [t.reshape((4, 2) + t.shape[1:]) for t in tensors]
    recv = _rs_pair(tag + "_pair", ps)
    sums = [_pair_sum(f"{tag}_pair_sum{t}", ps[t], recv[t], my_c, my_chip) for t in range(len(ps))]
    recv2 = _rs_chips(tag + "_chips", [s_[0] for s_ in sums])
    return [_chip_sum(f"{tag}_chip_sum{t}", sums[t][1], recv2[t], my_chip) for t in range(len(ps))]


HBM_SPEC = pl.BlockSpec(memory_space=pltpu.HBM)
SEM_SPEC = pl.BlockSpec(memory_space=pltpu.SEMAPHORE)
SPLIT_EFFECT = pltpu.SideEffectType.DATAFLOW_SIDE_EFFECTING
ALL_PEERS = (1, 2, 3, 4, 5, 6, 7)
FIRST_LEVEL_PEERS = (1, 4, 2, 6)


def _split_copies(src_refs, land_refs, sems, masks, src_per_peer):
    n, nt = len(masks), len(src_refs)
    x, y, c = _place()
    me = 4 * x + 2 * y + c
    copies = []
    for t in range(nt):
        for k, mask in enumerate(masks):
            px, py, pc = (1 - x) if mask & 4 else x, (1 - y) if mask & 2 else y, (1 - c) if mask & 1 else c
            src = src_refs[t].at[4 * px + 2 * py + pc] if src_per_peer else src_refs[t]
            copies.append(pltpu.make_async_remote_copy(src_ref=src, dst_ref=land_refs[t].at[me], send_sem=sems[t * n + k],
                                                       recv_sem=sems[nt * n + t * n + k], device_id=(px, py, pc), device_id_type=MESH))
    return copies


def _copies_start(name, srcs, lands, masks, src_per_peer):
    nt, ns = len(srcs), 2 * len(masks) * len(srcs)

    def body(*refs):
        for cp in _split_copies(refs[:nt], refs[nt:2 * nt], refs[2 * nt:2 * nt + ns], masks, src_per_peer):
            cp.start()
        token = refs[-1]
        token[...] = jnp.zeros(token.shape, F32)

    outs = pl.pallas_call(
        body, name=name,
        out_shape=(pltpu.SemaphoreType.DMA(()),) * ns + tuple(pltpu.HBM(a.shape, a.dtype) for a in list(srcs) + list(lands))
        + (jax.ShapeDtypeStruct((8, 128), F32),),
        in_specs=(HBM_SPEC,) * (2 * nt), out_specs=(SEM_SPEC,) * ns + (HBM_SPEC,) * (2 * nt) + (pl.BlockSpec(memory_space=pltpu.VMEM),),
        input_output_aliases={t: ns + t for t in range(2 * nt)}, compiler_params=pltpu.CompilerParams(has_side_effects=SPLIT_EFFECT))(
            *[pltpu.with_memory_space_constraint(a, pltpu.HBM) for a in list(srcs) + list(lands)])
    return outs[:ns], outs[ns:ns + nt], outs[ns + nt:ns + 2 * nt], outs[-1]


def _copies_wait(name, sems, srcs_thru, lands_thru, after, masks, src_per_peer):
    nt, ns = len(srcs_thru), len(sems)

    def body(*refs):
        for cp in _split_copies(refs[:nt], refs[nt:2 * nt], refs[2 * nt:2 * nt + ns], masks, src_per_peer):
            cp.wait_send()
            cp.wait_recv()

    thru = list(srcs_thru) + list(lands_thru)
    return pl.pallas_call(
        body, name=name, out_shape=tuple(pltpu.HBM(a.shape, a.dtype) for a in thru),
        in_specs=(HBM_SPEC,) * (2 * nt) + (SEM_SPEC,) * ns + (pl.BlockSpec(memory_space=pl.ANY),), out_specs=(HBM_SPEC,) * (2 * nt),
        input_output_aliases={t: t for t in range(2 * nt)}, compiler_params=pltpu.CompilerParams(has_side_effects=SPLIT_EFFECT))(
            *thru, *sems, after)[nt:]


def _ag_forward(name, gs):
    nt = len(gs)

    def body(*refs):
        o_refs, send_sems, recv_sems = refs[nt:2 * nt], refs[2 * nt], refs[2 * nt + 1]
        x, y, c = _place()
        chips = [(1 - x, y), (x, 1 - y), (1 - x, 1 - y)]

        def copy(t, j, pc):
            rows = o_refs[t].at[4 * chips[j][0] + 2 * chips[j][1] + pc]
            return pltpu.make_async_remote_copy(src_ref=rows, dst_ref=rows, send_sem=send_sems.at[3 * t + j], recv_sem=recv_sems.at[3 * t + j],
                                                device_id=(x, y, 1 - c), device_id_type=MESH)

        for t in range(nt):
            for j in range(3):
                copy(t, j, c).start()
        for t in range(nt):
            for j in range(3):
                copy(t, j, c).wait_send()
                copy(t, j, 1 - c).wait_recv()

    hbm = pl.BlockSpec(memory_space=pl.ANY)
    return pl.pallas_call(body, out_shape=[jax.ShapeDtypeStruct(g.shape, g.dtype) for g in gs], in_specs=[hbm] * nt, out_specs=[hbm] * nt,
                          scratch_shapes=[pltpu.SemaphoreType.DMA((3 * nt,)), pltpu.SemaphoreType.DMA((3 * nt,))],
                          input_output_aliases={t: t for t in range(nt)}, name=name)(*gs)


def _mm_tn_wire(name, a, b, me, sqrelu, shard_rows):
    T, M = a.shape
    N = b.shape[1]
    tk = min(512, T)
    nk = T // tk
    if shard_rows:
        bm, bn = M // N_DEV, N
        a_spec = pl.BlockSpec((tk, 2 * bm), lambda j, k, m: (k, j))
        b_spec = pl.BlockSpec((tk, bn), lambda j, k, m: (k, 0))
        halves = (slice(0, bm), slice(None)), (slice(bm, 2 * bm), slice(None))
        acc_shape = (2 * bm, bn)
    else:
        bm, bn = M, N // N_DEV
        a_spec = pl.BlockSpec((tk, bm), lambda j, k, m: (k, 0))
        b_spec = pl.BlockSpec((tk, 2 * bn), lambda j, k, m: (k, j))
        halves = (slice(None), slice(0, bn)), (slice(None), slice(bn, 2 * bn))
        acc_shape = (bm, 2 * bn)

    def body(me_ref, a_ref, b_ref, wire_ref, own_ref, acc):
        j, k = pl.program_id(0), pl.program_id(1)

        @pl.when(k == 0)
        def _():
            acc[...] = jnp.zeros(acc.shape, F32)

        av = a_ref[...]
        if sqrelu:
            r = jnp.maximum(av.astype(F32), 0.0)
            av = r * r
        acc[...] += _dot_tn(av, b_ref[...])

        for hh in range(2):
            @pl.when(k == nk - 1)
            def _():
                wire_ref[hh] = acc[halves[hh]].astype(MM)

            @pl.when((k == nk - 1) & (2 * j + hh == me_ref[0]))
            def _():
                own_ref[...] = acc[halves[hh]]

    return pl.pallas_call(
        body, grid_spec=pltpu.PrefetchScalarGridSpec(
            num_scalar_prefetch=1, grid=(N_DEV // 2, nk), in_specs=[a_spec, b_spec],
            out_specs=[pl.BlockSpec((2, bm, bn), lambda j, k, m: (j, 0, 0)), pl.BlockSpec((bm, bn), lambda j, k, m: (0, 0))],
            scratch_shapes=[pltpu.VMEM(acc_shape, F32)]),
        out_shape=[jax.ShapeDtypeStruct((N_DEV, bm, bn), MM), jax.ShapeDtypeStruct((bm, bn), F32)], name=name,
        compiler_params=_cparams(("arbitrary", "arbitrary")))(me.reshape(1), a, b)


def _rs_final(name, own, recv, me, stack, li):
    _, r, c = recv.shape
    tr = RS_ROWS

    def body(me_ref, own_ref, r_ref, s_ref, o_ref):
        acc = jnp.zeros((tr, c), F32)
        for j in range(N_DEV):
            acc = acc + jnp.where(me_ref[0] == j, own_ref[...], r_ref[j].astype(F32))
        o_ref[...] = acc

    return pl.pallas_call(
        body, grid_spec=pltpu.PrefetchScalarGridSpec(
            num_scalar_prefetch=1, grid=(r // tr,),
            in_specs=[pl.BlockSpec((tr, c), lambda i, m: (i, 0)), pl.BlockSpec((N_DEV, tr, c), lambda i, m: (0, i, 0)),
                      pl.BlockSpec(memory_space=pl.ANY)],
            out_specs=pl.BlockSpec((None, tr, c), lambda i, m: (li, i, 0))),
        out_shape=jax.ShapeDtypeStruct(stack.shape, F32), input_output_aliases={3: 0}, name=name,
        compiler_params=_cparams(("arbitrary",)))(me.reshape(1), own, recv, stack)


def _rs_finish(pending, after, me, stacks):
    i, sems, wires_thru, lands, owns = pending
    recvs = _copies_wait(f"rs_wait{i}", sems, wires_thru, lands, after, ALL_PEERS, True)
    return [_rs_final(f"rs_final{i}_{t}", owns[t], recvs[t], me, stacks[t], i) for t in range(len(owns))]


def _mod_part(c16, ada_w, ada_b_cols):
    L, D, n = ada_w.shape

    def body(c_ref, w_ref, b_ref, o_ref):
        cv = c_ref[...]
        o_ref[...] = _dot(cv * _sigmoid(cv), w_ref[...]) + b_ref[...]

    return pl.pallas_call(
        body, grid=(L,), in_specs=[pl.BlockSpec((16, D), lambda i: (0, 0)), pl.BlockSpec((None, D, n), lambda i: (i, 0, 0)),
                                   pl.BlockSpec((None, 1, n), lambda i: (i, 0, 0))],
        out_specs=pl.BlockSpec((None, 16, n), lambda i: (i, 0, 0)), out_shape=jax.ShapeDtypeStruct((L, 16, n), F32),
        name="ada_mod", compiler_params=_cparams(("arbitrary",)))(c16, ada_w, ada_b_cols)


def _ada_w_grad(c16, dmod16):
    L, _, n = dmod16.shape
    D = c16.shape[1]

    def body(c_ref, d_ref, o_ref):
        cv = c_ref[...]
        o_ref[...] = _dot_tn(cv * _sigmoid(cv), d_ref[...])

    return pl.pallas_call(
        body, grid=(L,), in_specs=[pl.BlockSpec((16, D), lambda i: (0, 0)), pl.BlockSpec((None, 16, n), lambda i: (i, 0, 0))],
        out_specs=pl.BlockSpec((None, D, n), lambda i: (i, 0, 0)), out_shape=jax.ShapeDtypeStruct((L, D, n), F32),
        name="ada_w_grad", compiler_params=_cparams(("arbitrary",)))(c16, dmod16)


def _sum_devices(name, g):
    _, R, C = g.shape

    def body(g_ref, o_ref):
        acc = g_ref[0]
        for d in range(1, N_DEV):
            acc = acc + g_ref[d]
        o_ref[...] = acc

    return pl.pallas_call(body, out_shape=jax.ShapeDtypeStruct((R, C), F32), name=name)(g)


def _prenorm(name, x, g0, sc, sh, dtype):
    T, D = x.shape

    def body(i, n, rr, cc, oo, aa, ss):
        oo[0][...] = (_rms(rr[0][...], cc[0][...]) * (1.0 + cc[1][...]) + cc[2][...]).astype(dtype)

    return _rows(name, body, T, 512, [(x, 'cur')], [g0, sc, sh], [(D, dtype)])[0]


def _post_bwd(name, dxo, y, g1, gt):
    T, D = y.shape

    def body(i, n, rr, cc, oo, aa, ss):
        d = rr[0][...]
        yv = rr[1][...]
        g1v, gtv = cc[0][...], cc[1][...]
        aa[1][...] += jnp.sum(d * _rms(yv, g1v), axis=0, keepdims=True)
        dy, dg1 = _rms_bwd(yv, g1v, d * gtv)
        aa[0][...] += dg1
        aa[2][...] += jnp.sum(dy, axis=0, keepdims=True)
        oo[0][...] = dy.astype(MM)

    return _rows(name, body, T, 512, [(dxo, 'cur'), (y, 'cur')], [g1, gt], [(D, MM)], accs=[(1, D)] * 3)


def _mm_post(name, a, w, bias, x, g1, gt):
    T, D = x.shape
    consts = [w, g1, gt] + ([bias] if bias is not None else [])

    def body(i, n, rr, cc, oo, aa, ss):
        y = _dot(rr[0][...], cc[0][...])
        if bias is not None:
            y = y + cc[3][...]
        oo[0][...] = y
        oo[1][...] = rr[1][...] + cc[2][...] * _rms(y, cc[1][...])

    return _rows(name, body, T, 512, [(a, 'cur'), (x, 'cur')], consts, [(D, F32), (D, F32)])


def _mm_nt_rows(name, a, w):
    T = a.shape[0]
    K = w.shape[0]

    def body(i, n, rr, cc, oo, aa, ss):
        oo[0][...] = _dot_nt(rr[0][...], cc[0][...]).astype(MM)

    return _rows(name, body, T, 512, [(a, 'cur')], [w], [(K, MM)])[0]


def _mm_tn(name, a, b, sqrelu=False, col_shards=0, diag=0):
    T, M = a.shape
    N = b.shape[1]
    tk = min(512, T)
    nk = T // tk
    if diag:
        bm, bn = M // diag, N // diag
        grid = (diag, 1, nk)
        a_spec = pl.BlockSpec((tk, bm), lambda g, n, k: (k, g))
        b_spec = pl.BlockSpec((tk, bn), lambda g, n, k: (k, g))
        o_spec = pl.BlockSpec((None, bm, bn), lambda g, n, k: (g, 0, 0))
        o_shape = (diag, bm, bn)
    else:
        bm = min(M, 1024)
        bn = N // col_shards if col_shards else min(N, 1024)
        grid = (M // bm, N // bn, nk)
        a_spec = pl.BlockSpec((tk, bm), lambda m, n, k: (k, m))
        b_spec = pl.BlockSpec((tk, bn), lambda m, n, k: (k, n))
        if col_shards:
            o_spec = pl.BlockSpec((None, bm, bn), lambda m, n, k: (n, m, 0))
            o_shape = (col_shards, M, bn)
        else:
            o_spec = pl.BlockSpec((bm, bn), lambda m, n, k: (m, n))
            o_shape = (M, N)

    def body(a_ref, b_ref, o_ref):
        @pl.when(pl.program_id(2) == 0)
        def _():
            o_ref[...] = jnp.zeros(o_ref.shape, F32)

        av = a_ref[...]
        if sqrelu:
            r = jnp.maximum(av.astype(F32), 0.0)
            av = r * r
        o_ref[...] += _dot_tn(av, b_ref[...])

    return pl.pallas_call(body, grid=grid, in_specs=[a_spec, b_spec], out_specs=o_spec,
                          out_shape=jax.ShapeDtypeStruct(o_shape, F32), name=name,
                          compiler_params=_cparams(("arbitrary", "arbitrary", "arbitrary")))(a, b)


FFN_SHARDS = 2

def _ffn_fwd(name, li, h, w1g, w2g, x, g1, gt):
    T, D = h.shape
    nf, tf = w1g.shape[0], w1g.shape[-1]
    F = nf * tf
    tm = min(512, T)

    def body(h_ref, w1_ref, w2_ref, x_ref, g1_ref, gt_ref, a_ref, y_ref, xo_ref, acc):
        f = pl.program_id(1)

        @pl.when(f == 0)
        def _():
            acc[...] = jnp.zeros(acc.shape, F32)

        hv = h_ref[...]
        part = None
        for hh in range(FFN_SHARDS):
            a = _dot(hv, w1_ref[hh])
            a_ref[:, hh * tf:(hh + 1) * tf] = a.astype(MM)
            r = jnp.maximum(a, 0.0)
            p = _dot(r * r, w2_ref[hh])
            part = p if part is None else part + p
        acc[...] += part

        @pl.when(f == nf // FFN_SHARDS - 1)
        def _():
            y = acc[...]
            y_ref[...] = y
            xo_ref[...] = x_ref[...] + gt_ref[...] * _rms(y, g1_ref[...])

    row = lambda t, f: (t, 0)
    one = lambda t, f: (0, 0)
    return pl.pallas_call(
        body, grid=(T // tm, nf // FFN_SHARDS),
        in_specs=[pl.BlockSpec((tm, D), row), pl.BlockSpec((FFN_SHARDS, None, D, tf), lambda t, f: (f, li, 0, 0)),
                  pl.BlockSpec((FFN_SHARDS, None, tf, D), lambda t, f: (f, li, 0, 0)),
                  pl.BlockSpec((tm, D), row), pl.BlockSpec((1, D), one), pl.BlockSpec((1, D), one)],
        out_specs=[pl.BlockSpec((tm, FFN_SHARDS * tf), lambda t, f: (t, f)), pl.BlockSpec((tm, D), row), pl.BlockSpec((tm, D), row)],
        out_shape=[jax.ShapeDtypeStruct((T, F), MM), jax.ShapeDtypeStruct((T, D), F32), jax.ShapeDtypeStruct((T, D), F32)],
        scratch_shapes=[pltpu.VMEM((tm, D), F32)], name=name,
        compiler_params=_cparams(("arbitrary", "arbitrary")))(h, w1g, w2g, x, g1, gt)


def _ffn_bwd(name, li, dy, a, w1g, w2g, x, dxo, g0, sc):
    T, D = x.shape
    nf, tf = w1g.shape[0], w1g.shape[-1]
    F = nf * tf
    tm = min(512, T)

    def body(dy_ref, a_ref, w1_ref, w2_ref, x_ref, dxo_ref, g0_ref, sc_ref, da_ref, dx_ref, dsh_ref, dsc_ref, dg0_ref, acc):
        t, f = pl.program_id(0), pl.program_id(1)

        @pl.when((t == 0) & (f == 0))
        def _():
            for r in (dsh_ref, dsc_ref, dg0_ref):
                r[...] = jnp.zeros(r.shape, F32)

        @pl.when(f == 0)
        def _():
            acc[...] = jnp.zeros(acc.shape, F32)

        du = _dot_nt(dy_ref[...], w2_ref[...])
        da = (du * (2.0 * jnp.maximum(a_ref[...].astype(F32), 0.0))).astype(MM)
        da_ref[...] = da
        acc[...] += _dot_nt(da, w1_ref[...])

        @pl.when(f == nf - 1)
        def _():
            dx, dsh, dsc, dg0 = _prenorm_bwd(x_ref[...], g0_ref[...], sc_ref[...], acc[...])
            dx_ref[...] = dxo_ref[...] + dx
            dsh_ref[...] += dsh
            dsc_ref[...] += dsc
            dg0_ref[...] += dg0

    row = lambda t, f: (t, 0)
    one = lambda t, f: (0, 0)
    blk = lambda t, f: (t, f)
    return pl.pallas_call(
        body, grid=(T // tm, nf),
        in_specs=[pl.BlockSpec((tm, D), row), pl.BlockSpec((tm, tf), blk), pl.BlockSpec((None, None, D, tf), lambda t, f: (f, li, 0, 0)),
                  pl.BlockSpec((None, None, tf, D), lambda t, f: (f, li, 0, 0)), pl.BlockSpec((tm, D), row), pl.BlockSpec((tm, D), row),
                  pl.BlockSpec((1, D), one), pl.BlockSpec((1, D), one)],
        out_specs=[pl.BlockSpec((tm, tf), blk), pl.BlockSpec((tm, D), row)] + [pl.BlockSpec((1, D), one)] * 3,
        out_shape=[jax.ShapeDtypeStruct((T, F), MM), jax.ShapeDtypeStruct((T, D), F32)] + [jax.ShapeDtypeStruct((1, D), F32)] * 3,
        scratch_shapes=[pltpu.VMEM((tm, D), F32)], name=name,
        compiler_params=_cparams(("arbitrary", "arbitrary")))(dy, a, w1g, w2g, x, dxo, g0, sc)


def _rope_tables(pos, invf):
    T = pos.shape[0]

    def body(i, n, rr, cc, oo, aa, ss):
        ang = rr[0][...] * cc[0][...]
        lane = lax.broadcasted_iota(jnp.int32, ang.shape, 1)
        cs, sn = jnp.cos(ang), jnp.sin(ang)
        oo[0][...] = jnp.where((lane >= QK_NOPE) & (lane < QK_NOPE + QK_ROPE), cs, 1.0)
        oo[1][...] = jnp.where((lane >= QK_NOPE) & (lane < QK_NOPE + QK_ROPE // 2), -sn, 0.0)
        oo[2][...] = jnp.where((lane >= QK_NOPE + QK_ROPE // 2) & (lane < QK_NOPE + QK_ROPE), sn, 0.0)

    return _rows("rope_tables", body, T, 512, [(pos, 'cur')], [invf], [(HEAD_PAD, F32)] * 3)


def _rope(v, C, S1, S2):
    n = v.shape[1]
    reps = n // HEAD_PAD
    if reps > 1:
        C, S1, S2 = (jnp.tile(t, (1, reps)) for t in (C, S1, S2))
    return v * C + pltpu.roll(v, n - QK_ROPE // 2, 1) * S1 + pltpu.roll(v, QK_ROPE // 2, 1) * S2


def _unrope(d, C, S1, S2):
    n = d.shape[1]
    reps = n // HEAD_PAD
    if reps > 1:
        C, S1, S2 = (jnp.tile(t, (1, reps)) for t in (C, S1, S2))
    return d * C + pltpu.roll(d * S1, QK_ROPE // 2, 1) + pltpu.roll(d * S2, n - QK_ROPE // 2, 1)


def _mla_proj(name, h, C, S1, S2, w_dq, qg, w_uq, w_dkv, kvg, w_ukv_k, w_ukv_v):
    T = h.shape[0]
    HP = N_HEADS * HEAD_PAD

    def body(i, n, rr, cc, oo, aa, ss):
        hv = rr[0][...]
        Cv, S1v, S2v = rr[1][...], rr[2][...], rr[3][...]
        cq_raw = _dot(hv, cc[0][...])
        cq = _rms(cq_raw, cc[1][...]).astype(MM)
        q = _rope(_dot(cq, cc[2][...]), Cv, S1v, S2v)
        ckv_all = _dot(hv, cc[3][...])
        ckv_raw = ckv_all[:, :KV_LORA]
        ckv = _rms(ckv_raw, cc[4][...]).astype(MM)
        kr = _rope(ckv_all[:, KV_LORA:], Cv, S1v, S2v)
        k = _dot(ckv, cc[5][...]) + jnp.tile(kr, (1, N_HEADS))
        v = _dot(ckv, cc[6][...])
        v = jnp.where(lax.broadcasted_iota(jnp.int32, v.shape, 1) % HEAD_PAD == V_HEAD, 1.0, v)
        oo[0][...] = cq_raw
        oo[1][...] = cq
        oo[2][...] = ckv_raw
        oo[3][...] = ckv
        oo[4][...] = q.astype(MM)
        oo[5][...] = k.astype(MM)
        oo[6][...] = v.astype(MM)

    return _rows(name, body, T, 256, [(h, 'cur'), (C, 'cur'), (S1, 'cur'), (S2, 'cur')],
                 [w_dq, qg, w_uq, w_dkv, kvg, w_ukv_k, w_ukv_v],
                 [(Q_LORA, F32), (Q_LORA, MM), (KV_LORA, F32), (KV_LORA, MM), (HP, MM), (HP, MM), (HP, MM)])


ATT_HEADS = 4
ATT_BLOCK = 512


def _chunk_mask_t(tk, tq):
    ki = lax.broadcasted_iota(jnp.int32, (tk, tq), 0) // CHUNK
    qi = lax.broadcasted_iota(jnp.int32, (tk, tq), 1) // CHUNK
    return ki <= qi


def _attn_fwd(name, q, k, v):
    T = q.shape[0]
    tb = min(ATT_BLOCK, T)
    nb = T // tb
    nh = ATT_HEADS
    hs = [slice(h * HEAD_PAD, (h + 1) * HEAD_PAD) for h in range(nh)]

    def body(q_ref, k_ref, v_ref, o_ref, lse_ref):
        qb = pl.program_id(1)

        def k_block(k0, masked, st):
            new = []
            for h in range(nh):
                m, acc = st[h]
                s = _dot_nt(k_ref[pl.ds(k0, tb), hs[h]], q_ref[:, hs[h]])
                if masked:
                    s = jnp.where(_chunk_mask_t(tb, tb), s, NEG)
                m_new = jnp.maximum(m, jnp.max(s, axis=0, keepdims=True))
                alpha = jnp.exp((m - m_new) * ATT_SCALE)
                p = jnp.exp((s - m_new) * ATT_SCALE)
                acc = alpha * acc + _dot_tn(v_ref[pl.ds(k0, tb), hs[h]], p)
                new.append((m_new, acc))
            return tuple(new)

        st = tuple((jnp.full((1, tb), NEG, F32), jnp.zeros((HEAD_PAD, tb), F32)) for _ in range(nh))
        st = k_block(pl.multiple_of(qb * tb, tb), True, st)
        st = lax.fori_loop(0, qb, lambda kb, s_: k_block(pl.multiple_of(kb * tb, tb), False, s_), st)
        for h in range(nh):
            m, acc = st[h]
            l = acc[V_HEAD:V_HEAD + 1, :]
            o_ref[:, hs[h]] = (acc / l).T.astype(MM)
            lse_ref[h] = jnp.broadcast_to(m * ATT_SCALE + jnp.log(l), (8, tb))

    blk = pl.BlockSpec((tb, nh * HEAD_PAD), lambda g, i: (i, g))
    res = pl.BlockSpec((T, nh * HEAD_PAD), lambda g, i: (0, g))
    return pl.pallas_call(
        body, grid=(N_HEADS // nh, nb), in_specs=[blk, res, res],
        out_specs=[blk, pl.BlockSpec((nh, 8, tb), lambda g, i: (g, 0, i))],
        out_shape=[jax.ShapeDtypeStruct(q.shape, MM), jax.ShapeDtypeStruct((N_HEADS, 8, T), F32)], name=name,
        compiler_params=_cparams(("arbitrary", "arbitrary")))(q, k, v)


def _attn_delta(name, do, o):
    T = do.shape[0]
    tb = min(256, T)

    def body(do_ref, o_ref, d_ref):
        lane = lax.broadcasted_iota(jnp.int32, (tb, HEAD_PAD), 1) // 8
        cols = jnp.zeros((tb, HEAD_PAD), F32)
        for h in range(N_HEADS):
            hsl = slice(h * HEAD_PAD, (h + 1) * HEAD_PAD)
            r = jnp.sum(do_ref[:, hsl].astype(F32) * o_ref[:, hsl].astype(F32), axis=1, keepdims=True)
            cols = jnp.where(lane == h, r, cols)
        d_ref[...] = cols.T

    spec = pl.BlockSpec((tb, N_HEADS * HEAD_PAD), lambda i: (i, 0))
    out = pl.pallas_call(body, grid=(T // tb,), in_specs=[spec, spec], out_specs=pl.BlockSpec((HEAD_PAD, tb), lambda i: (0, i)),
                         out_shape=jax.ShapeDtypeStruct((HEAD_PAD, T), F32), name=name, compiler_params=_cparams(("arbitrary",)))(do, o)
    return out.reshape(N_HEADS, 8, T)


def _attn_bwd(name, q, k, v, do, lse, delta):
    T = q.shape[0]
    tb = min(ATT_BLOCK, T)
    nb = T // tb
    nh = ATT_HEADS
    hs = [slice(h * HEAD_PAD, (h + 1) * HEAD_PAD) for h in range(nh)]

    def body(q_ref, k_ref, v_ref, do_ref, lse_ref, dl_ref, dq_ref, dk_ref, dv_ref, dq_acc, dk_acc, dv_acc):
        kb = pl.program_id(1)

        @pl.when(kb == 0)
        def _():
            dq_acc[...] = jnp.zeros(dq_acc.shape, F32)

        dk_acc[...] = jnp.zeros(dk_acc.shape, F32)
        dv_acc[...] = jnp.zeros(dv_acc.shape, F32)

        def q_block(q0, masked):
            for h in range(nh):
                qh = q_ref[pl.ds(q0, tb), hs[h]]
                doh = do_ref[pl.ds(q0, tb), hs[h]]
                kh = k_ref[:, hs[h]]
                s = _dot_nt(kh, qh) * ATT_SCALE
                if masked:
                    s = jnp.where(_chunk_mask_t(tb, tb), s, NEG)
                p = jnp.exp(s - lse_ref[h, 0:1, pl.ds(q0, tb)])
                ds = (p * (_dot_nt(v_ref[:, hs[h]], doh) - dl_ref[h, 0:1, pl.ds(q0, tb)]) * ATT_SCALE).astype(MM)
                dv_acc[:, hs[h]] += _dot(p, doh)
                dk_acc[:, hs[h]] += _dot(ds, qh)
                dq_acc[pl.ds(q0, tb), hs[h]] += _dot_tn(ds, kh)

        q_block(pl.multiple_of(kb * tb, tb), True)

        def rest(qb, c_):
            q_block(pl.multiple_of(qb * tb, tb), False)
            return c_

        lax.fori_loop(kb + 1, nb, rest, 0)
        dk_ref[...] = dk_acc[...].astype(MM)
        dv_ref[...] = dv_acc[...].astype(MM)

        @pl.when(kb == nb - 1)
        def _():
            dq_ref[...] = dq_acc[...].astype(MM)

    W = nh * HEAD_PAD
    blk = pl.BlockSpec((tb, W), lambda g, i: (i, g))
    res = pl.BlockSpec((T, W), lambda g, i: (0, g))
    rows = pl.BlockSpec((nh, 8, T), lambda g, i: (g, 0, 0))
    return pl.pallas_call(
        body, grid=(N_HEADS // nh, nb), in_specs=[res, blk, blk, res, rows, rows], out_specs=[res, blk, blk],
        out_shape=[jax.ShapeDtypeStruct(q.shape, MM)] * 3,
        scratch_shapes=[pltpu.VMEM((T, W), F32), pltpu.VMEM((tb, W), F32), pltpu.VMEM((tb, W), F32)],
        name=name, compiler_params=_cparams(("arbitrary", "arbitrary")))(q, k, v, do, lse, delta)


def _mla_proj_bwd(name, dq, dk, dv, C, S1, S2, cq_raw, ckv_raw, x, dxo, w_uq, w_ukv_k, w_ukv_v, w_dq, w_dkv, qg, kvg, g0, sc):
    T, D = x.shape
    HP = N_HEADS * HEAD_PAD

    def body(i, n, rr, cc, oo, aa, ss):
        Cv, S1v, S2v = rr[3][...], rr[4][...], rr[5][...]
        dq_pre = _unrope(rr[0][...].astype(F32), Cv, S1v, S2v).astype(MM)
        oo[0][...] = dq_pre
        dcq = _dot_nt(dq_pre, cc[0][...])
        dcq_raw, dqg = _rms_bwd(rr[6][...], cc[5][...], dcq)
        aa[0][...] += dqg
        dcq_raw = dcq_raw.astype(MM)
        oo[1][...] = dcq_raw
        dkv = rr[1][...]
        dkr = dkv[:, :HEAD_PAD].astype(F32)
        for hh in range(1, N_HEADS):
            dkr = dkr + dkv[:, hh * HEAD_PAD:(hh + 1) * HEAD_PAD].astype(F32)
        lane = lax.broadcasted_iota(jnp.int32, dkr.shape, 1)
        dkr = jnp.where((lane >= QK_NOPE) & (lane < QK_NOPE + QK_ROPE), _unrope(dkr, Cv, S1v, S2v), 0.0)
        dckv = _dot_nt(dkv, cc[1][...]) + _dot_nt(rr[2][...], cc[2][...])
        dckv_raw, dkvg = _rms_bwd(rr[7][...], cc[6][...], dckv)
        aa[1][...] += dkvg
        dckv_all = jnp.concatenate([dckv_raw, dkr], axis=1).astype(MM)
        oo[2][...] = dckv_all
        dh = _dot_nt(dcq_raw, cc[3][...]) + _dot_nt(dckv_all, cc[4][...])
        dx, dsh, dsc, dg0 = _prenorm_bwd(rr[8][...], cc[7][...], cc[8][...], dh)
        oo[3][...] = rr[9][...] + dx
        aa[2][...] += dsh
        aa[3][...] += dsc
        aa[4][...] += dg0

    return _rows(name, body, T, 256,
                 [(dq, 'cur'), (dk, 'cur'), (dv, 'cur'), (C, 'cur'), (S1, 'cur'), (S2, 'cur'), (cq_raw, 'cur'), (ckv_raw, 'cur'),
                  (x, 'cur'), (dxo, 'cur')],
                 [w_uq, w_ukv_k, w_ukv_v, w_dq, w_dkv, qg, kvg, g0, sc],
                 [(HP, MM), (Q_LORA, MM), (KV_LORA + HEAD_PAD, MM), (D, F32)],
                 accs=[(1, Q_LORA), (1, KV_LORA), (1, D), (1, D), (1, D)])


HALO = 32


def _conv_glu(name, h, w_pw1, b_pw1):
    T, D = h.shape

    def body(i, n, rr, cc, oo, aa, ss):
        a = _dot(rr[0][...], cc[0][...]) + cc[1][...]
        oo[0][...] = a
        oo[1][...] = a[:, :D] * _sigmoid(a[:, D:])

    return _rows(name, body, T, 512, [(h, 'cur')], [w_pw1, b_pw1], [(2 * D, F32), (D, F32)])


def _layernorm_parts(uc):
    xc = uc - jnp.mean(uc, axis=-1, keepdims=True)
    r = lax.rsqrt(jnp.mean(xc * xc, axis=-1, keepdims=True) + EPS)
    return xc * r, r


def _conv_dw(name, u, w_dw, b_dw, ln_g, ln_b, w_pw2, b_pw2, x, g1, gt):
    T, D = u.shape
    tm = min(256, T)

    def body(i, n, rr, cc, oo, aa, ss):
        ext = ss[0]
        ext[0:HALO, :] = jnp.where(i > 0, rr[1][tm - HALO:tm, :], 0.0)
        ext[HALO:HALO + tm, :] = rr[0][...]
        uc = jnp.zeros((tm, D), F32) + cc[1][...]
        for kk in range(CONV_W):
            uc = uc + ext[pl.ds(HALO - (CONV_W - 1) + kk, tm), :] * cc[0][kk:kk + 1, :]
        xh, _ = _layernorm_parts(uc)
        ln = xh * cc[2][...] + cc[3][...]
        z = (ln * _sigmoid(ln)).astype(MM)
        y = _dot(z, cc[4][...]) + cc[5][...]
        oo[0][...] = uc
        oo[1][...] = z
        oo[2][...] = y
        oo[3][...] = rr[2][...] + cc[7][...] * _rms(y, cc[6][...])

    return _rows(name, body, T, tm, [(u, 'cur'), (u, 'prev'), (x, 'cur')], [w_dw, b_dw, ln_g, ln_b, w_pw2, b_pw2, g1, gt],
                 [(D, F32), (D, MM), (D, F32), (D, F32)], scratch=[pltpu.VMEM((tm + HALO, D), F32)])


def _conv_bwd1(name, dy, uc, w_pw2, ln_g, ln_b):
    T, D = uc.shape

    def body(i, n, rr, cc, oo, aa, ss):
        dz = _dot_nt(rr[0][...], cc[0][...])
        xh, r = _layernorm_parts(rr[1][...])
        g = cc[1][...]
        ln = xh * g + cc[2][...]
        sg = _sigmoid(ln)
        dln = dz * (sg * (1.0 + ln * (1.0 - sg)))
        aa[0][...] += jnp.sum(dln * xh, axis=0, keepdims=True)
        aa[1][...] += jnp.sum(dln, axis=0, keepdims=True)
        dxh = dln * g
        duc = r * (dxh - jnp.mean(dxh, axis=-1, keepdims=True) - xh * jnp.mean(dxh * xh, axis=-1, keepdims=True))
        aa[2][...] += jnp.sum(duc, axis=0, keepdims=True)
        oo[0][...] = duc

    return _rows(name, body, T, 256, [(dy, 'cur'), (uc, 'cur')], [w_pw2, ln_g, ln_b], [(D, F32)], accs=[(1, D)] * 3)


def _conv_bwd2(name, duc, u, a, x, dxo, w_dw, w_pw1, g0, sc):
    T, D = u.shape
    tm = min(256, T)

    def body(i, n, rr, cc, oo, aa, ss):
        extd, extu = ss[0], ss[1]
        dcur = rr[0][...]
        extd[0:tm, :] = dcur
        extd[tm:tm + HALO, :] = jnp.where(i < n - 1, rr[1][0:HALO, :], 0.0)
        extu[0:HALO, :] = jnp.where(i > 0, rr[3][tm - HALO:tm, :], 0.0)
        extu[HALO:HALO + tm, :] = rr[2][...]
        du = jnp.zeros((tm, D), F32)
        for kk in range(CONV_W):
            du = du + extd[pl.ds(CONV_W - 1 - kk, tm), :] * cc[0][kk:kk + 1, :]
            aa[0][kk:kk + 1, :] += jnp.sum(dcur * extu[pl.ds(HALO - (CONV_W - 1) + kk, tm), :], axis=0, keepdims=True)
        av = rr[4][...]
        a1, sg = av[:, :D], _sigmoid(av[:, D:])
        da = jnp.concatenate([du * sg, du * a1 * (sg * (1.0 - sg))], axis=1)
        aa[1][...] += jnp.sum(da, axis=0, keepdims=True)
        da = da.astype(MM)
        oo[0][...] = da
        dx, dsh, dsc, dg0 = _prenorm_bwd(rr[5][...], cc[2][...], cc[3][...], _dot_nt(da, cc[1][...]))
        oo[1][...] = rr[6][...] + dx
        aa[2][...] += dsh
        aa[3][...] += dsc
        aa[4][...] += dg0

    return _rows(name, body, T, tm,
                 [(duc, 'cur'), (duc, 'next'), (u, 'cur'), (u, 'prev'), (a, 'cur'), (x, 'cur'), (dxo, 'cur')],
                 [w_dw, w_pw1, g0, sc], [(2 * D, MM), (D, F32)],
                 accs=[(32, D), (1, 2 * D), (1, D), (1, D), (1, D)],
                 scratch=[pltpu.VMEM((tm + HALO, D), F32), pltpu.VMEM((tm + HALO, D), F32)])


PHALO = 16


def _pool_fwd(name, h, w, b, scale, x, g1, gt):
    T, D = h.shape
    G = len(POOL_WINDOWS)
    Cg = D // G
    tm = min(256, T)

    def body(i, n, rr, cc, oo, aa, ss):
        ext = ss[0]
        ext[0:PHALO, :] = jnp.where(i > 0, rr[1][tm - PHALO:tm, :], 0.0)
        ext[PHALO:PHALO + tm, :] = rr[0][...]
        t_glob = i * tm + lax.broadcasted_iota(jnp.int32, (tm, 1), 0)
        ps, ys = [], []
        for g, win in enumerate(POOL_WINDOWS):
            cols = slice(g * Cg, (g + 1) * Cg)
            s = ext[pl.ds(PHALO, tm), cols]
            for j in range(1, win):
                s = s + ext[pl.ds(PHALO - j, tm), cols]
            cnt = jnp.minimum(t_glob + 1, win).astype(F32)
            p = (s / cnt - ext[pl.ds(PHALO, tm), cols]).astype(MM)
            ps.append(p)
            ys.append(_dot(p, cc[0][g]) + cc[1][:, cols])
        ypre = jnp.concatenate(ys, axis=1)
        y = ypre * cc[2][...]
        oo[0][...] = jnp.concatenate(ps, axis=1)
        oo[1][...] = ypre
        oo[2][...] = y
        oo[3][...] = rr[2][...] + cc[4][...] * _rms(y, cc[3][...])

    return _rows(name, body, T, tm, [(h, 'cur'), (h, 'prev'), (x, 'cur')], [w, b, scale, g1, gt],
                 [(D, MM), (D, F32), (D, F32), (D, F32)], scratch=[pltpu.VMEM((tm + PHALO, D), F32)])


def _pool_bwd1(name, dy, ypre, scale, w):
    T, D = ypre.shape
    G = len(POOL_WINDOWS)
    Cg = D // G

    def body(i, n, rr, cc, oo, aa, ss):
        dyv = rr[0][...].astype(F32)
        aa[0][...] += jnp.sum(dyv * rr[1][...], axis=0, keepdims=True)
        dypre = dyv * cc[0][...]
        aa[1][...] += jnp.sum(dypre, axis=0, keepdims=True)
        dypre = dypre.astype(MM)
        oo[1][...] = dypre
        oo[0][...] = jnp.concatenate([_dot_nt(dypre[:, g * Cg:(g + 1) * Cg], cc[1][g]) for g in range(G)], axis=1)

    return _rows(name, body, T, 256, [(dy, 'cur'), (ypre, 'cur')], [scale, w], [(D, F32), (D, MM)], accs=[(1, D)] * 2)


def _pool_bwd2(name, dp, x, dxo, g0, sc):
    T, D = x.shape
    G = len(POOL_WINDOWS)
    Cg = D // G
    tm = min(256, T)

    def body(i, n, rr, cc, oo, aa, ss):
        ext = ss[0]
        t_glob = i * tm + lax.broadcasted_iota(jnp.int32, (tm, 1), 0)
        dcur = rr[0][...]
        dhs = []
        for g, win in enumerate(POOL_WINDOWS):
            cols = slice(g * Cg, (g + 1) * Cg)
            cnt = jnp.minimum(t_glob + 1, win).astype(F32)
            ext[0:tm, cols] = dcur[:, cols] / cnt
            ext[tm:tm + PHALO, cols] = jnp.where(i < n - 1, rr[1][0:PHALO, cols] * (1.0 / win), 0.0)
        for g, win in enumerate(POOL_WINDOWS):
            cols = slice(g * Cg, (g + 1) * Cg)
            s = ext[pl.ds(0, tm), cols]
            for j in range(1, win):
                s = s + ext[pl.ds(j, tm), cols]
            dhs.append(s - dcur[:, cols])
        dx, dsh, dsc, dg0 = _prenorm_bwd(rr[2][...], cc[0][...], cc[1][...], jnp.concatenate(dhs, axis=1))
        oo[0][...] = rr[3][...] + dx
        aa[0][...] += dsh
        aa[1][...] += dsc
        aa[2][...] += dg0

    return _rows(name, body, T, tm, [(dp, 'cur'), (dp, 'next'), (x, 'cur'), (dxo, 'cur')], [g0, sc], [(D, F32)],
                 accs=[(1, D)] * 3, scratch=[pltpu.VMEM((tm + PHALO, D), F32)])


def _loss_head(x, tgt):
    T, D = x.shape

    def body(i, n, rr, cc, oo, aa, ss):
        err = rr[0][...] - rr[1][...]
        oo[0][...] = err * (1.0 / D)
        aa[0][...] += jnp.sum(err * err, axis=0, keepdims=True)

        @pl.when(i == n - 1)
        def _():
            aa[1][...] = jnp.broadcast_to(jnp.sum(aa[0][...], axis=1, keepdims=True) * (0.5 / D), (1, 128))

    dx, _, loss_row = _rows("loss_head", body, T, 512, [(x, 'cur'), (tgt, 'cur')], [], [(D, F32)], accs=[(1, D), (1, 128)])
    return dx, loss_row


def _adamw(name, w, g, m, v):
    shape = w.shape
    C = shape[-1]
    R = w.size // C
    w2, g2, m2, v2 = (t.reshape(R, C) for t in (w, g, m, v))
    br = R
    if R * C * 4 > (1 << 20):
        br = 8
        while br * 2 * C * 4 <= (1 << 20) and R % (br * 2) == 0:
            br *= 2
    b1c = 1.0 - ADAM_B1 ** ADAM_STEP
    b2c = 1.0 - ADAM_B2 ** ADAM_STEP

    def body(w_ref, g_ref, m_ref, v_ref, d_ref, mo_ref, vo_ref):
        gv = g_ref[...]
        mn = ADAM_B1 * m_ref[...] + (1.0 - ADAM_B1) * gv
        vn = ADAM_B2 * v_ref[...] + (1.0 - ADAM_B2) * (gv * gv)
        d_ref[...] = -ADAM_LR * ((mn / b1c) / (jnp.sqrt(vn / b2c) + ADAM_EPS) + ADAM_WD * w_ref[...])
        mo_ref[...] = mn
        vo_ref[...] = vn

    spec = pl.BlockSpec((br, C), lambda r: (r, 0))
    outs = pl.pallas_call(body, grid=(R // br,), in_specs=[spec] * 4, out_specs=[spec] * 3,
                          out_shape=[jax.ShapeDtypeStruct((R, C), F32)] * 3, name=name,
                          compiler_params=_cparams(("arbitrary",)))(w2, g2, m2, v2)
    return tuple(t.reshape(shape) for t in outs)


def _layer_shards(g, ax):
    s = g.shape
    r = g.reshape(s[:ax] + (N_DEV, s[ax] // N_DEV) + s[ax + 1:])
    return (jnp.moveaxis(r, ax, 0) if ax else r).reshape(N_DEV, -1)


def _unshard(g, ax):
    r = jnp.moveaxis(g, 0, ax)
    s = r.shape
    return r.reshape(s[:ax] + (s[ax] * s[ax + 1],) + s[ax + 2:])


def _pack(parts, dtype, row_mult):
    lead = parts[0].shape[:-1]
    flat = jnp.concatenate([p.astype(dtype) for p in parts], axis=-1)
    n = flat.shape[-1]
    per = row_mult * 1024
    tot = -(-n // per) * per
    flat = jnp.pad(flat, [(0, 0)] * len(lead) + [(0, tot - n)])
    return flat.reshape(lead + (tot // 1024, 1024))


def _pad_heads(w, lo, hi):
    K = w.shape[0]
    r = w.reshape(K, N_HEADS, -1)[:, :, lo:hi]
    return jnp.pad(r, ((0, 0), (0, 0), (0, HEAD_PAD - (hi - lo)))).reshape(K, N_HEADS * HEAD_PAD)


def kernel(x, c, positions, ada_w, ada_b, norm_g, mla_w_dq, mla_q_norm_g, mla_w_uq, mla_w_dkv, mla_kv_norm_g, mla_w_ukv, mla_w_o, conv_w_pw1, conv_b_pw1, conv_w_dw, conv_b_dw, conv_ln_g, conv_ln_b, conv_w_pw2, conv_b_pw2, pool_w, pool_b, pool_scale, ffn_w1, ffn_w2, loss_target, m_ada_w, m_ada_b, m_norm_g, m_mla_w_dq, m_mla_q_norm_g, m_mla_w_uq, m_mla_w_dkv, m_mla_kv_norm_g, m_mla_w_ukv, m_mla_w_o, m_conv_w_pw1, m_conv_b_pw1, m_conv_w_dw, m_conv_b_dw, m_conv_ln_g, m_conv_ln_b, m_conv_w_pw2, m_conv_b_pw2, m_pool_w, m_pool_b, m_pool_scale, m_ffn_w1, m_ffn_w2, v_ada_w, v_ada_b, v_norm_g, v_mla_w_dq, v_mla_q_norm_g, v_mla_w_uq, v_mla_w_dkv, v_mla_kv_norm_g, v_mla_w_ukv, v_mla_w_o, v_conv_w_pw1, v_conv_b_pw1, v_conv_w_dw, v_conv_b_dw, v_conv_ln_g, v_conv_ln_b, v_conv_w_pw2, v_conv_b_pw2, v_pool_w, v_pool_b, v_pool_scale, v_ffn_w1, v_ffn_w2):
    args = dict(locals())
    W = {n: args[n] for n, _ in WEIGHTS}
    M1 = {n: args['m_' + n] for n, _ in WEIGHTS}
    V2 = {n: args['v_' + n] for n, _ in WEIGHTS}
    D = D_MODEL
    T = x.shape[1]
    L = ffn_w1.shape[0]
    xi, yi, ci = _place()
    me = 4 * xi + 2 * yi + ci
    n_ada = ada_w.shape[2]

    small_sizes = [W[n].size for n in SMALL]
    small_in = _pack([c.reshape(-1)] + [W[n].reshape(-1) for n in SMALL], F32, 8)
    small_all = _ag_small("ag_small_params", small_in).reshape(N_DEV, -1)
    c_all = small_all[:, :D]
    Ws = {}
    off = D
    for n, sz in zip(SMALL, small_sizes):
        Ws[n] = _unshard(small_all[:, off:off + sz].reshape((N_DEV,) + W[n].shape), SHARD_AXIS[n])
        off += sz
    c16 = jnp.pad(c_all, ((0, 16 - N_DEV), (0, 0)))

    ada_b_cols = lax.dynamic_slice_in_dim(ada_b, me * n_ada, n_ada, axis=1).reshape(L, 1, n_ada)
    mod_part = _mod_part(c16, ada_w, ada_b_cols)[:, :N_DEV]
    mod_all = _ag_small("ag_mod", mod_part.reshape(L * N_DEV, n_ada)).reshape(N_DEV, L, N_DEV, n_ada)
    mod_mine = lax.dynamic_index_in_dim(mod_all, me, axis=2, keepdims=False)
    mod = jnp.transpose(mod_mine, (1, 0, 2)).reshape(L, 6, 1, D)

    rest = [n for n in BIG if not n.startswith('ffn')]
    rest_all, = _ag_big("ag_weights", [_pack([W[n].reshape(-1) for n in rest], MM, 32)])
    wf = [ffn_w1.astype(MM), ffn_w2.astype(MM)]
    wf, rest_all, mod = lax.optimization_barrier((wf, rest_all, mod))
    wf_land = [lax.dynamic_update_slice(lax.empty((N_DEV,) + w.shape, MM), w[None], (me, 0, 0, 0)) for w in wf]
    ag_sems, wf_thru, wf_land, ag_token = _copies_start("ag_ffn_start", wf, wf_land, FIRST_LEVEL_PEERS, False)
    rest_all = rest_all.reshape(N_DEV, -1)
    Wb = {}
    off = 0
    for n in rest:
        Wb[n] = rest_all[:, off:off + W[n].size].reshape((N_DEV,) + W[n].shape)
        off += W[n].size
    full = lambda n: _unshard(Wb[n], SHARD_AXIS[n])
    w_dq, w_uq, w_dkv, w_ukv, w_o = full('mla_w_dq'), full('mla_w_uq'), full('mla_w_dkv'), full('mla_w_ukv'), full('mla_w_o')
    w_pw1, w_pw2, w_pool = full('conv_w_pw1'), full('conv_w_pw2'), full('pool_w')
    n_mla = w_dq.shape[0]
    w_uq_p = [_pad_heads(w_uq[j], 0, QK_NOPE + QK_ROPE) for j in range(n_mla)]
    w_ukv_k = [_pad_heads(w_ukv[j], 0, QK_NOPE) for j in range(n_mla)]
    w_ukv_v = [_pad_heads(w_ukv[j], QK_NOPE, QK_NOPE + V_HEAD) for j in range(n_mla)]
    w_dkv_p = [jnp.pad(jnp.concatenate([w_dkv[j][:, :KV_LORA], jnp.zeros((D, QK_NOPE), MM), w_dkv[j][:, KV_LORA:]], axis=1),
                       ((0, 0), (0, HEAD_PAD - QK_NOPE - QK_ROPE))) for j in range(n_mla)]
    w_o_p = [jnp.pad(w_o[j].reshape(N_HEADS, V_HEAD, D), ((0, 0), (0, HEAD_PAD - V_HEAD), (0, 0))).reshape(N_HEADS * HEAD_PAD, D)
             for j in range(n_mla)]
    w_dw32 = jnp.pad(Ws['conv_w_dw'], ((0, 0), (0, 32 - CONV_W), (0, 0)))
    row = lambda t: t.reshape(1, -1)

    half = QK_ROPE // 2
    inv_freq = ROPE_THETA ** (-jnp.arange(0, QK_ROPE, 2, dtype=F32) / QK_ROPE)
    invf = jnp.zeros((1, HEAD_PAD), F32).at[0, QK_NOPE:QK_NOPE + half].set(inv_freq).at[0, QK_NOPE + half:QK_NOPE + QK_ROPE].set(inv_freq)
    rC, rS1, rS2 = _rope_tables(positions.reshape(T, 1).astype(F32), invf)

    xs = x.reshape(T, D)
    saved = []
    for i in range(L):
        kind, j = i % 3, i // 3
        sh_m, sc_m, gt_m, sh_f, sc_f, gt_f = (mod[i, r] for r in range(6))
        g = [row(Ws['norm_g'][i, r]) for r in range(4)]
        st = dict(x0=xs)
        if i == 0:
            sc_m = sc_m + ag_token[0:1, 0:1]
        if kind == 0:
            h = _prenorm(f"prenorm_m{i}", xs, g[0], sc_m, sh_m, MM)
            cq_raw, cq, ckv_raw, ckv, q, k, v = _mla_proj(f"mla_proj{i}", h, rC, rS1, rS2, w_dq[j], row(Ws['mla_q_norm_g'][j]), w_uq_p[j],
                                                          w_dkv_p[j], row(Ws['mla_kv_norm_g'][j]), w_ukv_k[j], w_ukv_v[j])
            o, lse = _attn_fwd(f"attn_fwd{i}", q, k, v)
            y, xs = _mm_post(f"mla_out{i}", o, w_o_p[j], None, xs, g[1], gt_m)
            st.update(h=h, cq_raw=cq_raw, cq=cq, ckv_raw=ckv_raw, ckv=ckv, q=q, k=k, v=v, o=o, lse=lse, y=y)
        elif kind == 1:
            h = _prenorm(f"prenorm_m{i}", xs, g[0], sc_m, sh_m, MM)
            a, u = _conv_glu(f"conv_glu{i}", h, w_pw1[j], row(W['conv_b_pw1'][j]))
            uc, z, y, xs = _conv_dw(f"conv_dw{i}", u, w_dw32[j], row(W['conv_b_dw'][j]), row(W['conv_ln_g'][j]), row(W['conv_ln_b'][j]),
                                    w_pw2[j], row(W['conv_b_pw2'][j]), xs, g[1], gt_m)
            st.update(h=h, a=a, u=u, uc=uc, z=z, y=y)
        else:
            h = _prenorm(f"prenorm_m{i}", xs, g[0], sc_m, sh_m, F32)
            p, ypre, y, xs = _pool_fwd(f"pool_fwd{i}", h, w_pool[j], row(Ws['pool_b'][j]), row(Ws['pool_scale'][j]), xs, g[1], gt_m)
            st.update(p=p, ypre=ypre, y=y)
        st['x1'] = xs
        if i == 0:
            wg = _copies_wait("ag_ffn_wait", ag_sems, wf_thru, wf_land, xs, FIRST_LEVEL_PEERS, False)
            w1g, w2g = _ag_forward("ag_ffn_forward", wg)
        hf = _prenorm(f"prenorm_f{i}", xs, g[2], sc_f, sh_f, MM)
        af, yf, xs = _ffn_fwd(f"ffn_fwd{i}", i, hf, w1g, w2g, xs, g[3], gt_f)
        st.update(hf=hf, af=af, yf=yf)
        saved.append(st)

    dx, loss_row = _loss_head(xs, loss_target.reshape(T, D))

    G = {}
    dmod = [None] * L
    dnorm = [None] * L
    rs_pending = None
    ffn_red = [lax.empty(ffn_w1.shape, F32), lax.empty(ffn_w2.shape, F32)]
    for i in reversed(range(L)):
        kind, j = i % 3, i // 3
        sh_m, sc_m, gt_m, sh_f, sc_f, gt_f = (mod[i, r] for r in range(6))
        g = [row(Ws['norm_g'][i, r]) for r in range(4)]
        st = saved[i]
        dy, dg3, dgt_f, _ = _post_bwd(f"post_bwd_f{i}", dx, st['yf'], g[3], gt_f)
        da, dx, dsh_f, dsc_f, dg2 = _ffn_bwd(f"ffn_bwd{i}", i, dy, st['af'], w1g, w2g, st['x1'], dx, g[2], sc_f)
        wire1, own1 = _mm_tn_wire(f"ffn_dw1_{i}", st['hf'], da, me, False, False)
        wire2, own2 = _mm_tn_wire(f"ffn_dw2_{i}", st['af'], dy, me, True, True)
        if rs_pending is not None:
            ffn_red = _rs_finish(rs_pending, wire2, me, ffn_red)
        wires = [wire1, wire2]
        rs_sems, wires_thru, rs_lands, rs_token = _copies_start(f"rs_start{i}", wires, [lax.empty(w.shape, MM) for w in wires], ALL_PEERS, True)
        rs_pending = (i, rs_sems, wires_thru, rs_lands, [own1, own2])
        dy, dg1, dgt_m, dysum = _post_bwd(f"post_bwd_m{i}", dx, st['y'], g[1], gt_m + rs_token[0:1, 0:1])
        if kind == 0:
            do = _mm_nt_rows(f"mla_do{i}", dy, w_o_p[j])
            delta = _attn_delta(f"attn_delta{i}", do, st['o'])
            dq, dk, dv = _attn_bwd(f"attn_bwd{i}", st['q'], st['k'], st['v'], do, st['lse'], delta)
            dq_pre, dcq_raw, dckv_all, dx, dqg, dkvg, dsh_m, dsc_m, dg0 = _mla_proj_bwd(
                f"mla_proj_bwd{i}", dq, dk, dv, rC, rS1, rS2, st['cq_raw'], st['ckv_raw'], st['x0'], dx, w_uq_p[j], w_ukv_k[j], w_ukv_v[j],
                w_dq[j], w_dkv_p[j], row(Ws['mla_q_norm_g'][j]), row(Ws['mla_kv_norm_g'][j]), g[0], sc_m)
            dwo = _mm_tn(f"mla_dwo{i}", st['o'], dy)
            dwuq = _mm_tn(f"mla_dwuq{i}", st['cq'], dq_pre)
            dwk = _mm_tn(f"mla_dwukvk{i}", st['ckv'], dk)
            dwv = _mm_tn(f"mla_dwukvv{i}", st['ckv'], dv)
            dwdq = _mm_tn(f"mla_dwdq{i}", st['h'], dcq_raw)
            dwdkv = _mm_tn(f"mla_dwdkv{i}", st['h'], dckv_all)
            G.setdefault('mla_w_o', [None] * n_mla)[j] = dwo.reshape(N_HEADS, HEAD_PAD, D)[:, :V_HEAD].reshape(N_HEADS * V_HEAD, D)
            G.setdefault('mla_w_uq', [None] * n_mla)[j] = dwuq.reshape(Q_LORA, N_HEADS, HEAD_PAD)[:, :, :QK_NOPE + QK_ROPE].reshape(Q_LORA, -1)
            G.setdefault('mla_w_ukv', [None] * n_mla)[j] = jnp.concatenate(
                [dwk.reshape(KV_LORA, N_HEADS, HEAD_PAD)[:, :, :QK_NOPE], dwv.reshape(KV_LORA, N_HEADS, HEAD_PAD)[:, :, :V_HEAD]], axis=2).reshape(KV_LORA, -1)
            G.setdefault('mla_w_dq', [None] * n_mla)[j] = dwdq
            G.setdefault('mla_w_dkv', [None] * n_mla)[j] = jnp.concatenate([dwdkv[:, :KV_LORA], dwdkv[:, KV_LORA + QK_NOPE:KV_LORA + QK_NOPE + QK_ROPE]], axis=1)
            G.setdefault('mla_q_norm_g', [None] * n_mla)[j] = dqg[0]
            G.setdefault('mla_kv_norm_g', [None] * n_mla)[j] = dkvg[0]
        elif kind == 1:
            duc, dlng, dlnb, dbdw = _conv_bwd1(f"conv_bwd1_{i}", dy, st['uc'], w_pw2[j], row(W['conv_ln_g'][j]), row(W['conv_ln_b'][j]))
            da, dx, dwdw, dbpw1, dsh_m, dsc_m, dg0 = _conv_bwd2(f"conv_bwd2_{i}", duc, st['u'], st['a'], st['x0'], dx, w_dw32[j], w_pw1[j], g[0], sc_m)
            G['conv_w_pw2'] = [_mm_tn(f"conv_dwpw2_{i}", st['z'], dy)]
            G['conv_w_pw1'] = [_mm_tn(f"conv_dwpw1_{i}", st['h'], da)]
            G['conv_w_dw'] = [dwdw[:CONV_W]]
            G['conv_b_pw1'], G['conv_b_dw'], G['conv_ln_g'], G['conv_ln_b'], G['conv_b_pw2'] = [dbpw1[0]], [dbdw[0]], [dlng[0]], [dlnb[0]], [dysum[0]]
        else:
            dp, dypre, dscale, dpb = _pool_bwd1(f"pool_bwd1_{i}", dy, st['ypre'], row(Ws['pool_scale'][j]), w_pool[j])
            dx, dsh_m, dsc_m, dg0 = _pool_bwd2(f"pool_bwd2_{i}", dp, st['x0'], dx, g[0], sc_m)
            G['pool_w'] = [_mm_tn(f"pool_dw{i}", st['p'], dypre, diag=len(POOL_WINDOWS))]
            G['pool_b'] = [dpb.reshape(len(POOL_WINDOWS), -1)]
            G['pool_scale'] = [dscale[0]]
        dmod[i] = jnp.concatenate([dsh_m, dsc_m, dgt_m, dsh_f, dsc_f, dgt_f], axis=1)
        dnorm[i] = jnp.concatenate([dg0, dg1, dg2, dg3], axis=0)
    G['norm_g'] = dnorm
    grad_x = dx.reshape(x.shape)

    rs_names = [n for n, ax in WEIGHTS if ax is not None and n != 'ada_w' and not n.startswith('ffn')]
    parts = [_layer_shards(g, SHARD_AXIS[n] - 1) for n in rs_names for g in G[n]]
    ffn_red = _rs_finish(rs_pending, dx, me, ffn_red)
    red = _reduce_scatter("rs", [_pack(parts, F32, RS_ROWS)], ci, 2 * xi + yi)
    grads = {'ffn_w1': ffn_red[0], 'ffn_w2': ffn_red[1]}
    red0 = red[0].reshape(-1)
    off = 0
    for n in rs_names:
        grads[n] = red0[off:off + W[n].size].reshape(W[n].shape)
        off += W[n].size

    dmod_mine = jnp.concatenate(dmod, axis=1).reshape(-1)
    fin_in = _pack([dmod_mine] + [G[n][0].reshape(-1) for n in REPL] + [loss_row.reshape(-1)], F32, 8)
    fin_all = _ag_small("ag_final", fin_in)
    fin_sum = _sum_devices("final_sum", fin_all).reshape(-1)
    nm = L * 6 * D
    grads['ada_b'] = fin_sum[:nm].reshape(L, 6 * D)
    off = nm
    for n in REPL:
        grads[n] = fin_sum[off:off + W[n].size].reshape(W[n].shape)
        off += W[n].size
    loss = fin_sum[off]
    dmod_all = fin_all.reshape(N_DEV, -1)[:, :nm].reshape(N_DEV, L, 6 * D)
    dmod_cols = lax.dynamic_slice_in_dim(dmod_all, me * n_ada, n_ada, axis=2)
    dmod16 = jnp.pad(jnp.transpose(dmod_cols, (1, 0, 2)), ((0, 0), (0, 16 - N_DEV), (0, 0)))
    grads['ada_w'] = _ada_w_grad(c16, dmod16)

    deltas, new_m, new_v = {}, {}, {}
    for n, _ in WEIGHTS:
        deltas[n], new_m[n], new_v[n] = _adamw("adamw_" + n, W[n], grads[n], M1[n], V2[n])
    names = [n for n, _ in WEIGHTS]
    return (loss, grad_x, *[grads[n] for n in names], *[deltas[n] for n in names], *[new_m[n] for n in names],
            *[new_v[n] for n in names])
```

```python
import functools
import math

import jax
import jax.numpy as jnp
from jax import lax
from jax.experimental import pallas as pl
from jax.experimental.pallas import tpu as pltpu

F32 = jnp.float32
MM = jnp.bfloat16
EPS = 1e-6
NEG = -1e30
N_DEV = 8
VMEM_LIMIT = 48 * 1024 * 1024
MESH = pl.DeviceIdType.MESH

D_MODEL = 1024
N_HEADS = 16
HEAD_PAD = 128
QK_NOPE, QK_ROPE, V_HEAD = 64, 32, 64
Q_LORA, KV_LORA = 384, 256
CHUNK = 64
CONV_W = 31
POOL_WINDOWS = (2, 4, 8, 16)
ROPE_THETA = 10000.0
ATT_SCALE = 1.0 / math.sqrt(QK_NOPE + QK_ROPE)

ADAM_LR, ADAM_B1, ADAM_B2, ADAM_EPS, ADAM_WD, ADAM_STEP = 0.001, 0.9, 0.999, 1e-08, 0.01, 10

WEIGHTS = [('ada_w', 2), ('ada_b', None), ('norm_g', 2), ('mla_w_dq', 1), ('mla_q_norm_g', 1), ('mla_w_uq', 2),
           ('mla_w_dkv', 1), ('mla_kv_norm_g', 1), ('mla_w_ukv', 2), ('mla_w_o', 1), ('conv_w_pw1', 2),
           ('conv_b_pw1', None), ('conv_w_dw', 2), ('conv_b_dw', None), ('conv_ln_g', None), ('conv_ln_b', None),
           ('conv_w_pw2', 1), ('conv_b_pw2', None), ('pool_w', 2), ('pool_b', 2), ('pool_scale', 1),
           ('ffn_w1', 2), ('ffn_w2', 1)]
SHARD_AXIS = dict(WEIGHTS)
BIG = ['mla_w_dq', 'mla_w_uq', 'mla_w_dkv', 'mla_w_ukv', 'mla_w_o', 'conv_w_pw1', 'conv_w_pw2', 'pool_w', 'ffn_w1', 'ffn_w2']
SMALL = ['norm_g', 'mla_q_norm_g', 'mla_kv_norm_g', 'conv_w_dw', 'pool_b', 'pool_scale']
REPL = ['conv_b_pw1', 'conv_b_dw', 'conv_ln_g', 'conv_ln_b', 'conv_b_pw2']


def _dot(a, b):
    return jnp.dot(a.astype(MM), b.astype(MM), preferred_element_type=F32)


def _dot_nt(a, b):
    return lax.dot_general(a.astype(MM), b.astype(MM), (((1,), (1,)), ((), ())), preferred_element_type=F32)


def _dot_tn(a, b):
    return lax.dot_general(a.astype(MM), b.astype(MM), (((0,), (0,)), ((), ())), preferred_element_type=F32)


def _sigmoid(x):
    return 1.0 / (1.0 + jnp.exp(-x))


def _rstd(x):
    return lax.rsqrt(jnp.mean(x * x, axis=-1, keepdims=True) + EPS)


def _rms(x, g):
    return x * _rstd(x) * g


def _rms_bwd(x, g, dout):
    r = _rstd(x)
    xn = x * r
    dg = jnp.sum(dout * xn, axis=0, keepdims=True)
    dxn = dout * g
    dx = r * (dxn - xn * jnp.mean(dxn * xn, axis=-1, keepdims=True))
    return dx, dg


def _prenorm_bwd(x, g0, sc, dh):
    r = _rstd(x)
    xn = x * r
    dsh = jnp.sum(dh, axis=0, keepdims=True)
    dsc = jnp.sum(dh * (xn * g0), axis=0, keepdims=True)
    dn = dh * (1.0 + sc)
    dg0 = jnp.sum(dn * xn, axis=0, keepdims=True)
    dxn = dn * g0
    dx = r * (dxn - xn * jnp.mean(dxn * xn, axis=-1, keepdims=True))
    return dx, dsh, dsc, dg0


def _cparams(sem):
    return pltpu.CompilerParams(dimension_semantics=sem, vmem_limit_bytes=VMEM_LIMIT)


def _rows(name, body, n_rows, tm, rows, consts, outs, accs=(), scratch=()):
    tm = min(tm, n_rows)
    nblk = n_rows // tm
    nr, nc, no, na = len(rows), len(consts), len(outs), len(accs)
    in_specs, args = [], []
    for a, kind in rows:
        if kind == 'cur':
            im = lambda i: (i, 0)
        elif kind == 'prev':
            im = lambda i: (jnp.maximum(i - 1, 0), 0)
        else:
            im = lambda i: (jnp.minimum(i + 1, nblk - 1), 0)
        in_specs.append(pl.BlockSpec((tm, a.shape[1]), im))
        args.append(a)
    for a in consts:
        in_specs.append(pl.BlockSpec(a.shape, lambda i, nd=a.ndim: (0,) * nd))
        args.append(a)
    out_specs = [pl.BlockSpec((tm, c), lambda i: (i, 0)) for c, _ in outs]
    out_specs += [pl.BlockSpec(s, lambda i, nd=len(s): (0,) * nd) for s in accs]
    out_shape = [jax.ShapeDtypeStruct((n_rows, c), dt) for c, dt in outs]
    out_shape += [jax.ShapeDtypeStruct(s, F32) for s in accs]

    def kern(*refs):
        i = pl.program_id(0)
        rr = refs[:nr]
        cc = refs[nr:nr + nc]
        oo = refs[nr + nc:nr + nc + no]
        aa = refs[nr + nc + no:nr + nc + no + na]
        ss = refs[nr + nc + no + na:]

        @pl.when(i == 0)
        def _():
            for a in aa:
                a[...] = jnp.zeros(a.shape, F32)

        body(i, nblk, rr, cc, oo, aa, ss)

    return pl.pallas_call(kern, grid=(nblk,), in_specs=in_specs, out_specs=out_specs, out_shape=out_shape,
                          scratch_shapes=list(scratch), name=name, compiler_params=_cparams(("arbitrary",)))(*args)


def _place():
    return lax.axis_index("x"), lax.axis_index("y"), lax.axis_index("c")


def _ag_small(name, xs):
    R, C = xs.shape

    def body(x_ref, out_ref, send_sems, recv_sems):
        x, y, c = _place()
        me = 4 * x + 2 * y + c
        out_ref[me] = x_ref[...]
        copies = []
        for k in range(1, N_DEV):
            peer = ((1 - x) if k & 4 else x, (1 - y) if k & 2 else y, (1 - c) if k & 1 else c)
            cp = pltpu.make_async_remote_copy(src_ref=x_ref, dst_ref=out_ref.at[me], send_sem=send_sems.at[k - 1],
                                              recv_sem=recv_sems.at[k - 1], device_id=peer, device_id_type=MESH)
            cp.start()
            copies.append(cp)
        for cp in copies:
            cp.wait()

    return pl.pallas_call(
        body, out_shape=jax.ShapeDtypeStruct((N_DEV, R, C), xs.dtype),
        in_specs=[pl.BlockSpec(memory_space=pltpu.VMEM)], out_specs=pl.BlockSpec(memory_space=pltpu.VMEM),
        scratch_shapes=[pltpu.SemaphoreType.DMA((N_DEV - 1,)), pltpu.SemaphoreType.DMA((N_DEV - 1,))], name=name)(xs)


def _ag_big(name, xs):
    nt = len(xs)

    def body(*refs):
        x_refs, out_refs = refs[:nt], refs[nt:2 * nt]
        send_sems, recv_sems, local_sems = refs[2 * nt:]
        x, y, c = _place()
        me, sibling = (x, y, c), (x, y, 1 - c)
        chips = [(1 - x, y), (x, 1 - y), (1 - x, 1 - y)]

        def copy(t, k, block, to, own=False):
            px, py, pc = block
            rows = out_refs[t].at[4 * px + 2 * py + pc]
            return pltpu.make_async_remote_copy(src_ref=x_refs[t] if own else rows, dst_ref=rows, send_sem=send_sems.at[7 * t + k],
                                                recv_sem=recv_sems.at[7 * t + k], device_id=to, device_id_type=MESH)

        mine = [pltpu.make_async_copy(x_refs[t], out_refs[t].at[4 * x + 2 * y + c], local_sems.at[t]) for t in range(nt)]
        for cp in mine:
            cp.start()
        first = []
        for t in range(nt):
            first.append(copy(t, 0, me, sibling, own=True))
            first += [copy(t, 1 + j, me, (*chip, c), own=True) for j, chip in enumerate(chips)]
        for cp in first:
            cp.start()
        passed = []
        for t in range(nt):
            for j, chip in enumerate(chips):
                copy(t, 1 + j, (*chip, c), me).wait_recv()
                cp = copy(t, 4 + j, (*chip, c), sibling)
                cp.start()
                passed.append(cp)
        for t in range(nt):
            copy(t, 0, sibling, me).wait_recv()
            for j, chip in enumerate(chips):
                copy(t, 4 + j, (*chip, 1 - c), me).wait_recv()
        for cp in first + passed:
            cp.wait_send()
        for cp in mine:
            cp.wait()

    hbm = pl.BlockSpec(memory_space=pl.ANY)
    return pl.pallas_call(
        body, out_shape=[jax.ShapeDtypeStruct((N_DEV,) + t.shape, t.dtype) for t in xs],
        in_specs=[hbm] * nt, out_specs=[hbm] * nt,
        scratch_shapes=[pltpu.SemaphoreType.DMA((7 * nt,)), pltpu.SemaphoreType.DMA((7 * nt,)), pltpu.SemaphoreType.DMA((nt,))],
        name=name)(*xs)


def _rs_pair(name, ps):
    nt = len(ps)

    def body(*refs):
        p_refs, recv_refs = refs[:nt], refs[nt:2 * nt]
        send_sems, recv_sems = refs[2 * nt:]
        x, y, c = _place()
        copies = []
        for t in range(nt):
            for j in range(4):
                cp = pltpu.make_async_remote_copy(src_ref=p_refs[t].at[j, 1 - c], dst_ref=recv_refs[t].at[j], send_sem=send_sems.at[4 * t + j],
                                                  recv_sem=recv_sems.at[4 * t + j], device_id=(x, y, 1 - c), device_id_type=MESH)
                cp.start()
                copies.append(cp)
        for cp in copies:
            cp.wait()

    hbm = pl.BlockSpec(memory_space=pl.ANY)
    return pl.pallas_call(
        body, out_shape=[jax.ShapeDtypeStruct((4,) + p.shape[2:], p.dtype) for p in ps], in_specs=[hbm] * nt, out_specs=[hbm] * nt,
        scratch_shapes=[pltpu.SemaphoreType.DMA((4 * nt,)), pltpu.SemaphoreType.DMA((4 * nt,))], name=name)(*ps)


def _rs_chips(name, ss):
    nt = len(ss)

    def body(*refs):
        s_refs, recv_refs = refs[:nt], refs[nt:2 * nt]
        send_sems, recv_sems, local_sems = refs[2 * nt:]
        x, y, c = _place()
        mine = 2 * x + y
        owns = [pltpu.make_async_copy(s_refs[t].at[mine], recv_refs[t].at[mine], local_sems.at[t]) for t in range(nt)]
        for cp in owns:
            cp.start()
        copies = []
        for t in range(nt):
            for k in range(1, 4):
                px = (1 - x) if k & 2 else x
                py = (1 - y) if k & 1 else y
                cp = pltpu.make_async_remote_copy(src_ref=s_refs[t].at[2 * px + py], dst_ref=recv_refs[t].at[mine],
                                                  send_sem=send_sems.at[3 * t + k - 1], recv_sem=recv_sems.at[3 * t + k - 1],
                                                  device_id=(px, py, c), device_id_type=MESH)
                cp.start()
                copies.append(cp)
        for cp in copies:
            cp.wait()
        for cp in owns:
            cp.wait()

    hbm = pl.BlockSpec(memory_space=pl.ANY)
    return pl.pallas_call(
        body, out_shape=[jax.ShapeDtypeStruct(s_.shape, s_.dtype) for s_ in ss], in_specs=[hbm] * nt, out_specs=[hbm] * nt,
        scratch_shapes=[pltpu.SemaphoreType.DMA((3 * nt,)), pltpu.SemaphoreType.DMA((3 * nt,)), pltpu.SemaphoreType.DMA((nt,))],
        name=name)(*ss)


RS_ROWS = 256


def _row_block(r):
    return next(t for t in range(RS_ROWS, 0, -16) if r % t == 0)


def _pair_sum(name, p, recv, my_c, my_chip):
    _, _, r, c = p.shape
    tr = _row_block(r)

    def body(sc_ref, p_ref, r_ref, o_ref, own_ref):
        s = p_ref[...] + r_ref[...]
        o_ref[...] = s.astype(MM)

        @pl.when(pl.program_id(1) == sc_ref[1])
        def _():
            own_ref[...] = s

    return pl.pallas_call(
        body, grid_spec=pltpu.PrefetchScalarGridSpec(
            num_scalar_prefetch=1, grid=(r // tr, 4),
            in_specs=[pl.BlockSpec((None, None, tr, c), lambda i, j, sc: (j, sc[0], i, 0)),
                      pl.BlockSpec((None, tr, c), lambda i, j, sc: (j, i, 0))],
            out_specs=[pl.BlockSpec((None, tr, c), lambda i, j, sc: (j, i, 0)), pl.BlockSpec((tr, c), lambda i, j, sc: (i, 0))]),
        out_shape=[jax.ShapeDtypeStruct((4, r, c), MM), jax.ShapeDtypeStruct((r, c), F32)], name=name,
        compiler_params=_cparams(("arbitrary", "arbitrary")))(jnp.stack([my_c, my_chip]), p, recv)


def _chip_sum(name, own, recv, my_chip):
    _, r, c = recv.shape
    tr = _row_block(r)

    def body(sc_ref, own_ref, r_ref, o_ref):
        acc = jnp.zeros((tr, c), F32)
        for j in range(4):
            acc = acc + jnp.where(sc_ref[0] == j, own_ref[...], r_ref[j].astype(F32))
        o_ref[...] = acc

    return pl.pallas_call(
        body, grid_spec=pltpu.PrefetchScalarGridSpec(
            num_scalar_prefetch=1, grid=(r // tr,),
            in_specs=[pl.BlockSpec((tr, c), lambda i, sc: (i, 0)), pl.BlockSpec((4, tr, c), lambda i, sc: (0, i, 0))],
            out_specs=pl.BlockSpec((tr, c), lambda i, sc: (i, 0))),
        out_shape=jax.ShapeDtypeStruct((r, c), F32), name=name,
        compiler_params=_cparams(("arbitrary",)))(my_chip.reshape(1), own, recv)


def _reduce_scatter(tag, tensors, my_c, my_chip):
    ps = [t.reshape((4, 2) + t.shape[1:]) for t in tensors]
    recv = _rs_pair(tag + "_pair", ps)
    sums = [_pair_sum(f"{tag}_pair_sum{t}", ps[t], recv[t], my_c, my_chip) for t in range(len(ps))]
    recv2 = _rs_chips(tag + "_chips", [s_[0] for s_ in sums])
    return [_chip_sum(f"{tag}_chip_sum{t}", sums[t][1], recv2[t], my_chip) for t in range(len(ps))]


HBM_SPEC = pl.BlockSpec(memory_space=pltpu.HBM)
SEM_SPEC = pl.BlockSpec(memory_space=pltpu.SEMAPHORE)
SPLIT_EFFECT = pltpu.SideEffectType.DATAFLOW_SIDE_EFFECTING
ALL_PEERS = (1, 2, 3, 4, 5, 6, 7)
FIRST_LEVEL_PEERS = (1, 4, 2, 6)


def _split_copies(src_refs, land_refs, sems, masks, src_per_peer):
    n, nt = len(masks), len(src_refs)
    x, y, c = _place()
    me = 4 * x + 2 * y + c
    copies = []
    for t in range(nt):
        for k, mask in enumerate(masks):
            px, py, pc = (1 - x) if mask & 4 else x, (1 - y) if mask & 2 else y, (1 - c) if mask & 1 else c
            src = src_refs[t].at[4 * px + 2 * py + pc] if src_per_peer else src_refs[t]
            copies.append(pltpu.make_async_remote_copy(src_ref=src, dst_ref=land_refs[t].at[me], send_sem=sems[t * n + k],
                                                       recv_sem=sems[nt * n + t * n + k], device_id=(px, py, pc), device_id_type=MESH))
    return copies


def _copies_start(name, srcs, lands, masks, src_per_peer):
    nt, ns = len(srcs), 2 * len(masks) * len(srcs)

    def body(*refs):
        for cp in _split_copies(refs[:nt], refs[nt:2 * nt], refs[2 * nt:2 * nt + ns], masks, src_per_peer):
            cp.start()
        token = refs[-1]
        token[...] = jnp.zeros(token.shape, F32)

    outs = pl.pallas_call(
        body, name=name,
        out_shape=(pltpu.SemaphoreType.DMA(()),) * ns + tuple(pltpu.HBM(a.shape, a.dtype) for a in list(srcs) + list(lands))
        + (jax.ShapeDtypeStruct((8, 128), F32),),
        in_specs=(HBM_SPEC,) * (2 * nt), out_specs=(SEM_SPEC,) * ns + (HBM_SPEC,) * (2 * nt) + (pl.BlockSpec(memory_space=pltpu.VMEM),),
        input_output_aliases={t: ns + t for t in range(2 * nt)}, compiler_params=pltpu.CompilerParams(has_side_effects=SPLIT_EFFECT))(
            *[pltpu.with_memory_space_constraint(a, pltpu.HBM) for a in list(srcs) + list(lands)])
    return outs[:ns], outs[ns:ns + nt], outs[ns + nt:ns + 2 * nt], outs[-1]


def _copies_wait(name, sems, srcs_thru, lands_thru, after, masks, src_per_peer):
    nt, ns = len(srcs_thru), len(sems)

    def body(*refs):
        for cp in _split_copies(refs[:nt], refs[nt:2 * nt], refs[2 * nt:2 * nt + ns], masks, src_per_peer):
            cp.wait_send()
            cp.wait_recv()

    thru = list(srcs_thru) + list(lands_thru)
    return pl.pallas_call(
        body, name=name, out_shape=tuple(pltpu.HBM(a.shape, a.dtype) for a in thru),
        in_specs=(HBM_SPEC,) * (2 * nt) + (SEM_SPEC,) * ns + (pl.BlockSpec(memory_space=pl.ANY),), out_specs=(HBM_SPEC,) * (2 * nt),
        input_output_aliases={t: t for t in range(2 * nt)}, compiler_params=pltpu.CompilerParams(has_side_effects=SPLIT_EFFECT))(
            *thru, *sems, after)[nt:]


def _ag_forward(name, gs):
    nt = len(gs)

    def body(*refs):
        o_refs, send_sems, recv_sems = refs[nt:2 * nt], refs[2 * nt], refs[2 * nt + 1]
        x, y, c = _place()
        chips = [(1 - x, y), (x, 1 - y), (1 - x, 1 - y)]

        def copy(t, j, pc):
            rows = o_refs[t].at[4 * chips[j][0] + 2 * chips[j][1] + pc]
            return pltpu.make_async_remote_copy(src_ref=rows, dst_ref=rows, send_sem=send_sems.at[3 * t + j], recv_sem=recv_sems.at[3 * t + j],
                                                device_id=(x, y, 1 - c), device_id_type=MESH)

        for t in range(nt):
            for j in range(3):
                copy(t, j, c).start()
        for t in range(nt):
            for j in range(3):
                copy(t, j, c).wait_send()
                copy(t, j, 1 - c).wait_recv()

    hbm = pl.BlockSpec(memory_space=pl.ANY)
    return pl.pallas_call(body, out_shape=[jax.ShapeDtypeStruct(g.shape, g.dtype) for g in gs], in_specs=[hbm] * nt, out_specs=[hbm] * nt,
                          scratch_shapes=[pltpu.SemaphoreType.DMA((3 * nt,)), pltpu.SemaphoreType.DMA((3 * nt,))],
                          input_output_aliases={t: t for t in range(nt)}, name=name)(*gs)


def _mm_tn_wire(name, a, b, me, sqrelu, shard_rows):
    T, M = a.shape
    N = b.shape[1]
    tk = min(512, T)
    nk = T // tk
    if shard_rows:
        bm, bn = M // N_DEV, N
        a_spec = pl.BlockSpec((tk, 2 * bm), lambda j, k, m: (k, j))
        b_spec = pl.BlockSpec((tk, bn), lambda j, k, m: (k, 0))
        halves = (slice(0, bm), slice(None)), (slice(bm, 2 * bm), slice(None))
        acc_shape = (2 * bm, bn)
    else:
        bm, bn = M, N // N_DEV
        a_spec = pl.BlockSpec((tk, bm), lambda j, k, m: (k, 0))
        b_spec = pl.BlockSpec((tk, 2 * bn), lambda j, k, m: (k, j))
        halves = (slice(None), slice(0, bn)), (slice(None), slice(bn, 2 * bn))
        acc_shape = (bm, 2 * bn)

    def body(me_ref, a_ref, b_ref, wire_ref, own_ref, acc):
        j, k = pl.program_id(0), pl.program_id(1)

        @pl.when(k == 0)
        def _():
            acc[...] = jnp.zeros(acc.shape, F32)

        av = a_ref[...]
        if sqrelu:
            r = jnp.maximum(av.astype(F32), 0.0)
            av = r * r
        acc[...] += _dot_tn(av, b_ref[...])

        for hh in range(2):
            @pl.when(k == nk - 1)
            def _():
                wire_ref[hh] = acc[halves[hh]].astype(MM)

            @pl.when((k == nk - 1) & (2 * j + hh == me_ref[0]))
            def _():
                own_ref[...] = acc[halves[hh]]

    return pl.pallas_call(
        body, grid_spec=pltpu.PrefetchScalarGridSpec(
            num_scalar_prefetch=1, grid=(N_DEV // 2, nk), in_specs=[a_spec, b_spec],
            out_specs=[pl.BlockSpec((2, bm, bn), lambda j, k, m: (j, 0, 0)), pl.BlockSpec((bm, bn), lambda j, k, m: (0, 0))],
            scratch_shapes=[pltpu.VMEM(acc_shape, F32)]),
        out_shape=[jax.ShapeDtypeStruct((N_DEV, bm, bn), MM), jax.ShapeDtypeStruct((bm, bn), F32)], name=name,
        compiler_params=_cparams(("arbitrary", "arbitrary")))(me.reshape(1), a, b)


def _rs_final(name, own, recv, me, stack, li):
    _, r, c = recv.shape
    tr = RS_ROWS

    def body(me_ref, own_ref, r_ref, s_ref, o_ref):
        acc = jnp.zeros((tr, c), F32)
        for j in range(N_DEV):
            acc = acc + jnp.where(me_ref[0] == j, own_ref[...], r_ref[j].astype(F32))
        o_ref[...] = acc

    return pl.pallas_call(
        body, grid_spec=pltpu.PrefetchScalarGridSpec(
            num_scalar_prefetch=1, grid=(r // tr,),
            in_specs=[pl.BlockSpec((tr, c), lambda i, m: (i, 0)), pl.BlockSpec((N_DEV, tr, c), lambda i, m: (0, i, 0)),
                      pl.BlockSpec(memory_space=pl.ANY)],
            out_specs=pl.BlockSpec((None, tr, c), lambda i, m: (li, i, 0))),
        out_shape=jax.ShapeDtypeStruct(stack.shape, F32), input_output_aliases={3: 0}, name=name,
        compiler_params=_cparams(("arbitrary",)))(me.reshape(1), own, recv, stack)


def _rs_finish(pending, after, me, stacks):
    i, sems, wires_thru, lands, owns = pending
    recvs = _copies_wait(f"rs_wait{i}", sems, wires_thru, lands, after, ALL_PEERS, True)
    return [_rs_final(f"rs_final{i}_{t}", owns[t], recvs[t], me, stacks[t], i) for t in range(len(owns))]


def _mod_part(c16, ada_w, ada_b_cols):
    L, D, n = ada_w.shape

    def body(c_ref, w_ref, b_ref, o_ref):
        cv = c_ref[...]
        o_ref[...] = _dot(cv * _sigmoid(cv), w_ref[...]) + b_ref[...]

    return pl.pallas_call(
        body, grid=(L,), in_specs=[pl.BlockSpec((16, D), lambda i: (0, 0)), pl.BlockSpec((None, D, n), lambda i: (i, 0, 0)),
                                   pl.BlockSpec((None, 1, n), lambda i: (i, 0, 0))],
        out_specs=pl.BlockSpec((None, 16, n), lambda i: (i, 0, 0)), out_shape=jax.ShapeDtypeStruct((L, 16, n), F32),
        name="ada_mod", compiler_params=_cparams(("arbitrary",)))(c16, ada_w, ada_b_cols)


def _ada_w_grad(c16, dmod16):
    L, _, n = dmod16.shape
    D = c16.shape[1]

    def body(c_ref, d_ref, o_ref):
        cv = c_ref[...]
        o_ref[...] = _dot_tn(cv * _sigmoid(cv), d_ref[...])

    return pl.pallas_call(
        body, grid=(L,), in_specs=[pl.BlockSpec((16, D), lambda i: (0, 0)), pl.BlockSpec((None, 16, n), lambda i: (i, 0, 0))],
        out_specs=pl.BlockSpec((None, D, n), lambda i: (i, 0, 0)), out_shape=jax.ShapeDtypeStruct((L, D, n), F32),
        name="ada_w_grad", compiler_params=_cparams(("arbitrary",)))(c16, dmod16)


def _sum_devices(name, g):
    _, R, C = g.shape

    def body(g_ref, o_ref):
        acc = g_ref[0]
        for d in range(1, N_DEV):
            acc = acc + g_ref[d]
        o_ref[...] = acc

    return pl.pallas_call(body, out_shape=jax.ShapeDtypeStruct((R, C), F32), name=name)(g)


def _prenorm(name, x, g0, sc, sh, dtype):
    T, D = x.shape

    def body(i, n, rr, cc, oo, aa, ss):
        oo[0][...] = (_rms(rr[0][...], cc[0][...]) * (1.0 + cc[1][...]) + cc[2][...]).astype(dtype)

    return _rows(name, body, T, 512, [(x, 'cur')], [g0, sc, sh], [(D, dtype)])[0]


def _post_bwd(name, dxo, y, g1, gt):
    T, D = y.shape

    def body(i, n, rr, cc, oo, aa, ss):
        d = rr[0][...]
        yv = rr[1][...]
        g1v, gtv = cc[0][...], cc[1][...]
        aa[1][...] += jnp.sum(d * _rms(yv, g1v), axis=0, keepdims=True)
        dy, dg1 = _rms_bwd(yv, g1v, d * gtv)
        aa[0][...] += dg1
        aa[2][...] += jnp.sum(dy, axis=0, keepdims=True)
        oo[0][...] = dy.astype(MM)

    return _rows(name, body, T, 512, [(dxo, 'cur'), (y, 'cur')], [g1, gt], [(D, MM)], accs=[(1, D)] * 3)


def _mm_post(name, a, w, bias, x, g1, gt):
    T, D = x.shape
    consts = [w, g1, gt] + ([bias] if bias is not None else [])

    def body(i, n, rr, cc, oo, aa, ss):
        y = _dot(rr[0][...], cc[0][...])
        if bias is not None:
            y = y + cc[3][...]
        oo[0][...] = y
        oo[1][...] = rr[1][...] + cc[2][...] * _rms(y, cc[1][...])

    return _rows(name, body, T, 512, [(a, 'cur'), (x, 'cur')], consts, [(D, F32), (D, F32)])


def _mm_nt_rows(name, a, w):
    T = a.shape[0]
    K = w.shape[0]

    def body(i, n, rr, cc, oo, aa, ss):
        oo[0][...] = _dot_nt(rr[0][...], cc[0][...]).astype(MM)

    return _rows(name, body, T, 512, [(a, 'cur')], [w], [(K, MM)])[0]


def _mm_tn(name, a, b, sqrelu=False, col_shards=0, diag=0):
    T, M = a.shape
    N = b.shape[1]
    tk = min(512, T)
    nk = T // tk
    if diag:
        bm, bn = M // diag, N // diag
        grid = (diag, 1, nk)
        a_spec = pl.BlockSpec((tk, bm), lambda g, n, k: (k, g))
        b_spec = pl.BlockSpec((tk, bn), lambda g, n, k: (k, g))
        o_spec = pl.BlockSpec((None, bm, bn), lambda g, n, k: (g, 0, 0))
        o_shape = (diag, bm, bn)
    else:
        bm = min(M, 1024)
        bn = N // col_shards if col_shards else min(N, 1024)
        grid = (M // bm, N // bn, nk)
        a_spec = pl.BlockSpec((tk, bm), lambda m, n, k: (k, m))
        b_spec = pl.BlockSpec((tk, bn), lambda m, n, k: (k, n))
        if col_shards:
            o_spec = pl.BlockSpec((None, bm, bn), lambda m, n, k: (n, m, 0))
            o_shape = (col_shards, M, bn)
        else:
            o_spec = pl.BlockSpec((bm, bn), lambda m, n, k: (m, n))
            o_shape = (M, N)

    def body(a_ref, b_ref, o_ref):
        @pl.when(pl.program_id(2) == 0)
        def _():
            o_ref[...] = jnp.zeros(o_ref.shape, F32)

        av = a_ref[...]
        if sqrelu:
            r = jnp.maximum(av.astype(F32), 0.0)
            av = r * r
        o_ref[...] += _dot_tn(av, b_ref[...])

    return pl.pallas_call(body, grid=grid, in_specs=[a_spec, b_spec], out_specs=o_spec,
                          out_shape=jax.ShapeDtypeStruct(o_shape, F32), name=name,
                          compiler_params=_cparams(("arbitrary", "arbitrary", "arbitrary")))(a, b)


FFN_SHARDS = 2

def _ffn_fwd(name, li, h, w1g, w2g, x, g1, gt):
    T, D = h.shape
    nf, tf = w1g.shape[0], w1g.shape[-1]
    F = nf * tf
    tm = min(512, T)

    def body(h_ref, w1_ref, w2_ref, x_ref, g1_ref, gt_ref, a_ref, y_ref, xo_ref, acc):
        f = pl.program_id(1)

        @pl.when(f == 0)
        def _():
            acc[...] = jnp.zeros(acc.shape, F32)

        hv = h_ref[...]
        part = None
        for hh in range(FFN_SHARDS):
            a = _dot(hv, w1_ref[hh])
            a_ref[:, hh * tf:(hh + 1) * tf] = a.astype(MM)
            r = jnp.maximum(a, 0.0)
            p = _dot(r * r, w2_ref[hh])
            part = p if part is None else part + p
        acc[...] += part

        @pl.when(f == nf // FFN_SHARDS - 1)
        def _():
            y = acc[...]
            y_ref[...] = y
            xo_ref[...] = x_ref[...] + gt_ref[...] * _rms(y, g1_ref[...])

    row = lambda t, f: (t, 0)
    one = lambda t, f: (0, 0)
    return pl.pallas_call(
        body, grid=(T // tm, nf // FFN_SHARDS),
        in_specs=[pl.BlockSpec((tm, D), row), pl.BlockSpec((FFN_SHARDS, None, D, tf), lambda t, f: (f, li, 0, 0)),
                  pl.BlockSpec((FFN_SHARDS, None, tf, D), lambda t, f: (f, li, 0, 0)),
                  pl.BlockSpec((tm, D), row), pl.BlockSpec((1, D), one), pl.BlockSpec((1, D), one)],
        out_specs=[pl.BlockSpec((tm, FFN_SHARDS * tf), lambda t, f: (t, f)), pl.BlockSpec((tm, D), row), pl.BlockSpec((tm, D), row)],
        out_shape=[jax.ShapeDtypeStruct((T, F), MM), jax.ShapeDtypeStruct((T, D), F32), jax.ShapeDtypeStruct((T, D), F32)],
        scratch_shapes=[pltpu.VMEM((tm, D), F32)], name=name,
        compiler_params=_cparams(("arbitrary", "arbitrary")))(h, w1g, w2g, x, g1, gt)


def _ffn_bwd(name, li, dy, a, w1g, w2g, x, dxo, g0, sc):
    T, D = x.shape
    nf, tf = w1g.shape[0], w1g.shape[-1]
    F = nf * tf
    tm = min(512, T)

    def body(dy_ref, a_ref, w1_ref, w2_ref, x_ref, dxo_ref, g0_ref, sc_ref, da_ref, dx_ref, dsh_ref, dsc_ref, dg0_ref, acc):
        t, f = pl.program_id(0), pl.program_id(1)

        @pl.when((t == 0) & (f == 0))
        def _():
            for r in (dsh_ref, dsc_ref, dg0_ref):
                r[...] = jnp.zeros(r.shape, F32)

        @pl.when(f == 0)
        def _():
            acc[...] = jnp.zeros(acc.shape, F32)

        dyv = dy_ref[...]
        part = None
        for hh in range(FFN_SHARDS):
            cols = slice(hh * tf, (hh + 1) * tf)
            du = _dot_nt(dyv, w2_ref[hh])
            da = (du * (2.0 * jnp.maximum(a_ref[:, cols].astype(F32), 0.0))).astype(MM)
            da_ref[:, cols] = da
            p = _dot_nt(da, w1_ref[hh])
            part = p if part is None else part + p
        acc[...] += part

        @pl.when(f == nf // FFN_SHARDS - 1)
        def _():
            dx, dsh, dsc, dg0 = _prenorm_bwd(x_ref[...], g0_ref[...], sc_ref[...], acc[...])
            dx_ref[...] = dxo_ref[...] + dx
            dsh_ref[...] += dsh
            dsc_ref[...] += dsc
            dg0_ref[...] += dg0

    row = lambda t, f: (t, 0)
    one = lambda t, f: (0, 0)
    blk = lambda t, f: (t, f)
    return pl.pallas_call(
        body, grid=(T // tm, nf // FFN_SHARDS),
        in_specs=[pl.BlockSpec((tm, D), row), pl.BlockSpec((tm, FFN_SHARDS * tf), blk),
                  pl.BlockSpec((FFN_SHARDS, None, D, tf), lambda t, f: (f, li, 0, 0)),
                  pl.BlockSpec((FFN_SHARDS, None, tf, D), lambda t, f: (f, li, 0, 0)), pl.BlockSpec((tm, D), row), pl.BlockSpec((tm, D), row),
                  pl.BlockSpec((1, D), one), pl.BlockSpec((1, D), one)],
        out_specs=[pl.BlockSpec((tm, FFN_SHARDS * tf), blk), pl.BlockSpec((tm, D), row)] + [pl.BlockSpec((1, D), one)] * 3,
        out_shape=[jax.ShapeDtypeStruct((T, F), MM), jax.ShapeDtypeStruct((T, D), F32)] + [jax.ShapeDtypeStruct((1, D), F32)] * 3,
        scratch_shapes=[pltpu.VMEM((tm, D), F32)], name=name,
        compiler_params=_cparams(("arbitrary", "arbitrary")))(dy, a, w1g, w2g, x, dxo, g0, sc)


def _rope_tables(pos, invf):
    T = pos.shape[0]

    def body(i, n, rr, cc, oo, aa, ss):
        ang = rr[0][...] * cc[0][...]
        lane = lax.broadcasted_iota(jnp.int32, ang.shape, 1)
        cs, sn = jnp.cos(ang), jnp.sin(ang)
        oo[0][...] = jnp.where((lane >= QK_NOPE) & (lane < QK_NOPE + QK_ROPE), cs, 1.0)
        oo[1][...] = jnp.where((lane >= QK_NOPE) & (lane < QK_NOPE + QK_ROPE // 2), -sn, 0.0)
        oo[2][...] = jnp.where((lane >= QK_NOPE + QK_ROPE // 2) & (lane < QK_NOPE + QK_ROPE), sn, 0.0)

    return _rows("rope_tables", body, T, 512, [(pos, 'cur')], [invf], [(HEAD_PAD, F32)] * 3)


def _rope(v, C, S1, S2):
    n = v.shape[1]
    reps = n // HEAD_PAD
    if reps > 1:
        C, S1, S2 = (jnp.tile(t, (1, reps)) for t in (C, S1, S2))
    return v * C + pltpu.roll(v, n - QK_ROPE // 2, 1) * S1 + pltpu.roll(v, QK_ROPE // 2, 1) * S2


def _unrope(d, C, S1, S2):
    n = d.shape[1]
    reps = n // HEAD_PAD
    if reps > 1:
        C, S1, S2 = (jnp.tile(t, (1, reps)) for t in (C, S1, S2))
    return d * C + pltpu.roll(d * S1, QK_ROPE // 2, 1) + pltpu.roll(d * S2, n - QK_ROPE // 2, 1)


def _mla_proj(name, h, C, S1, S2, w_dq, qg, w_uq, w_dkv, kvg, w_ukv_k, w_ukv_v):
    T = h.shape[0]
    HP = N_HEADS * HEAD_PAD

    def body(i, n, rr, cc, oo, aa, ss):
        hv = rr[0][...]
        Cv, S1v, S2v = rr[1][...], rr[2][...], rr[3][...]
        cq_raw = _dot(hv, cc[0][...])
        cq = _rms(cq_raw, cc[1][...]).astype(MM)
        q = _rope(_dot(cq, cc[2][...]), Cv, S1v, S2v)
        ckv_all = _dot(hv, cc[3][...])
        ckv_raw = ckv_all[:, :KV_LORA]
        ckv = _rms(ckv_raw, cc[4][...]).astype(MM)
        kr = _rope(ckv_all[:, KV_LORA:], Cv, S1v, S2v)
        k = _dot(ckv, cc[5][...]) + jnp.tile(kr, (1, N_HEADS))
        v = _dot(ckv, cc[6][...])
        v = jnp.where(lax.broadcasted_iota(jnp.int32, v.shape, 1) % HEAD_PAD == V_HEAD, 1.0, v)
        oo[0][...] = cq_raw
        oo[1][...] = cq
        oo[2][...] = ckv_raw
        oo[3][...] = ckv
        oo[4][...] = q.astype(MM)
        oo[5][...] = k.astype(MM)
        oo[6][...] = v.astype(MM)

    return _rows(name, body, T, 256, [(h, 'cur'), (C, 'cur'), (S1, 'cur'), (S2, 'cur')],
                 [w_dq, qg, w_uq, w_dkv, kvg, w_ukv_k, w_ukv_v],
                 [(Q_LORA, F32), (Q_LORA, MM), (KV_LORA, F32), (KV_LORA, MM), (HP, MM), (HP, MM), (HP, MM)])


ATT_HEADS = 4
ATT_BLOCK = 512
ATT_FWD_BLOCK = 1024


def _chunk_mask_t(tk, tq):
    ki = lax.broadcasted_iota(jnp.int32, (tk, tq), 0) // CHUNK
    qi = lax.broadcasted_iota(jnp.int32, (tk, tq), 1) // CHUNK
    return ki <= qi


def _attn_fwd(name, q, k, v):
    T = q.shape[0]
    tb = min(ATT_FWD_BLOCK, T)
    nb = T // tb
    nh = ATT_HEADS
    hs = [slice(h * HEAD_PAD, (h + 1) * HEAD_PAD) for h in range(nh)]

    def body(q_ref, k_ref, v_ref, o_ref, lse_ref):
        qb = pl.program_id(1)

        def k_block(k0, masked, st):
            new = []
            for h in range(nh):
                m, acc = st[h]
                s = _dot_nt(k_ref[pl.ds(k0, tb), hs[h]], q_ref[:, hs[h]])
                if masked:
                    s = jnp.where(_chunk_mask_t(tb, tb), s, NEG)
                m_new = jnp.maximum(m, jnp.max(s, axis=0, keepdims=True))
                alpha = jnp.exp((m - m_new) * ATT_SCALE)
                p = jnp.exp((s - m_new) * ATT_SCALE)
                acc = alpha * acc + _dot_tn(v_ref[pl.ds(k0, tb), hs[h]], p)
                new.append((m_new, acc))
            return tuple(new)

        st = tuple((jnp.full((1, tb), NEG, F32), jnp.zeros((HEAD_PAD, tb), F32)) for _ in range(nh))
        st = k_block(pl.multiple_of(qb * tb, tb), True, st)
        st = lax.fori_loop(0, qb, lambda kb, s_: k_block(pl.multiple_of(kb * tb, tb), False, s_), st)
        for h in range(nh):
            m, acc = st[h]
            l = acc[V_HEAD:V_HEAD + 1, :]
            o_ref[:, hs[h]] = (acc / l).T.astype(MM)
            lse_ref[h] = jnp.broadcast_to(m * ATT_SCALE + jnp.log(l), (8, tb))

    blk = pl.BlockSpec((tb, nh * HEAD_PAD), lambda g, i: (i, g))
    res = pl.BlockSpec((T, nh * HEAD_PAD), lambda g, i: (0, g))
    return pl.pallas_call(
        body, grid=(N_HEADS // nh, nb), in_specs=[blk, res, res],
        out_specs=[blk, pl.BlockSpec((nh, 8, tb), lambda g, i: (g, 0, i))],
        out_shape=[jax.ShapeDtypeStruct(q.shape, MM), jax.ShapeDtypeStruct((N_HEADS, 8, T), F32)], name=name,
        compiler_params=_cparams(("arbitrary", "arbitrary")))(q, k, v)


def _attn_delta(name, do, o):
    T = do.shape[0]
    tb = min(256, T)

    def body(do_ref, o_ref, d_ref):
        lane = lax.broadcasted_iota(jnp.int32, (tb, HEAD_PAD), 1) // 8
        cols = jnp.zeros((tb, HEAD_PAD), F32)
        for h in range(N_HEADS):
            hsl = slice(h * HEAD_PAD, (h + 1) * HEAD_PAD)
            r = jnp.sum(do_ref[:, hsl].astype(F32) * o_ref[:, hsl].astype(F32), axis=1, keepdims=True)
            cols = jnp.where(lane == h, r, cols)
        d_ref[...] = cols.T

    spec = pl.BlockSpec((tb, N_HEADS * HEAD_PAD), lambda i: (i, 0))
    out = pl.pallas_call(body, grid=(T // tb,), in_specs=[spec, spec], out_specs=pl.BlockSpec((HEAD_PAD, tb), lambda i: (0, i)),
                         out_shape=jax.ShapeDtypeStruct((HEAD_PAD, T), F32), name=name, compiler_params=_cparams(("arbitrary",)))(do, o)
    return out.reshape(N_HEADS, 8, T)


def _attn_bwd(name, q, k, v, do, lse, delta):
    T = q.shape[0]
    tb = min(ATT_BLOCK, T)
    nb = T // tb
    nh = ATT_HEADS
    hs = [slice(h * HEAD_PAD, (h + 1) * HEAD_PAD) for h in range(nh)]

    def body(q_ref, k_ref, v_ref, do_ref, lse_ref, dl_ref, dq_ref, dk_ref, dv_ref, dq_acc, dk_acc, dv_acc):
        kb = pl.program_id(1)

        @pl.when(kb == 0)
        def _():
            dq_acc[...] = jnp.zeros(dq_acc.shape, F32)

        dk_acc[...] = jnp.zeros(dk_acc.shape, F32)
        dv_acc[...] = jnp.zeros(dv_acc.shape, F32)

        def q_block(q0, masked):
            for h in range(nh):
                qh = q_ref[pl.ds(q0, tb), hs[h]]
                doh = do_ref[pl.ds(q0, tb), hs[h]]
                kh = k_ref[:, hs[h]]
                s = _dot_nt(kh, qh) * ATT_SCALE
                if masked:
                    s = jnp.where(_chunk_mask_t(tb, tb), s, NEG)
                p = jnp.exp(s - lse_ref[h, 0:1, pl.ds(q0, tb)])
                ds = (p * (_dot_nt(v_ref[:, hs[h]], doh) - dl_ref[h, 0:1, pl.ds(q0, tb)]) * ATT_SCALE).astype(MM)
                dv_acc[:, hs[h]] += _dot(p, doh)
                dk_acc[:, hs[h]] += _dot(ds, qh)
                dq_acc[pl.ds(q0, tb), hs[h]] += _dot_tn(ds, kh)

        q_block(pl.multiple_of(kb * tb, tb), True)

        def rest(qb, c_):
            q_block(pl.multiple_of(qb * tb, tb), False)
            return c_

        lax.fori_loop(kb + 1, nb, rest, 0)
        dk_ref[...] = dk_acc[...].astype(MM)
        dv_ref[...] = dv_acc[...].astype(MM)

        @pl.when(kb == nb - 1)
        def _():
            dq_ref[...] = dq_acc[...].astype(MM)

    W = nh * HEAD_PAD
    blk = pl.BlockSpec((tb, W), lambda g, i: (i, g))
    res = pl.BlockSpec((T, W), lambda g, i: (0, g))
    rows = pl.BlockSpec((nh, 8, T), lambda g, i: (g, 0, 0))
    return pl.pallas_call(
        body, grid=(N_HEADS // nh, nb), in_specs=[res, blk, blk, res, rows, rows], out_specs=[res, blk, blk],
        out_shape=[jax.ShapeDtypeStruct(q.shape, MM)] * 3,
        scratch_shapes=[pltpu.VMEM((T, W), F32), pltpu.VMEM((tb, W), F32), pltpu.VMEM((tb, W), F32)],
        name=name, compiler_params=_cparams(("arbitrary", "arbitrary")))(q, k, v, do, lse, delta)


def _mla_proj_bwd(name, dq, dk, dv, C, S1, S2, cq_raw, ckv_raw, x, dxo, w_uq, w_ukv_k, w_ukv_v, w_dq, w_dkv, qg, kvg, g0, sc):
    T, D = x.shape
    HP = N_HEADS * HEAD_PAD

    def body(i, n, rr, cc, oo, aa, ss):
        Cv, S1v, S2v = rr[3][...], rr[4][...], rr[5][...]
        dq_pre = _unrope(rr[0][...].astype(F32), Cv, S1v, S2v).astype(MM)
        oo[0][...] = dq_pre
        dcq = _dot_nt(dq_pre, cc[0][...])
        dcq_raw, dqg = _rms_bwd(rr[6][...], cc[5][...], dcq)
        aa[0][...] += dqg
        dcq_raw = dcq_raw.astype(MM)
        oo[1][...] = dcq_raw
        dkv = rr[1][...]
        dkr = dkv[:, :HEAD_PAD].astype(F32)
        for hh in range(1, N_HEADS):
            dkr = dkr + dkv[:, hh * HEAD_PAD:(hh + 1) * HEAD_PAD].astype(F32)
        lane = lax.broadcasted_iota(jnp.int32, dkr.shape, 1)
        dkr = jnp.where((lane >= QK_NOPE) & (lane < QK_NOPE + QK_ROPE), _unrope(dkr, Cv, S1v, S2v), 0.0)
        dckv = _dot_nt(dkv, cc[1][...]) + _dot_nt(rr[2][...], cc[2][...])
        dckv_raw, dkvg = _rms_bwd(rr[7][...], cc[6][...], dckv)
        aa[1][...] += dkvg
        dckv_all = jnp.concatenate([dckv_raw, dkr], axis=1).astype(MM)
        oo[2][...] = dckv_all
        dh = _dot_nt(dcq_raw, cc[3][...]) + _dot_nt(dckv_all, cc[4][...])
        dx, dsh, dsc, dg0 = _prenorm_bwd(rr[8][...], cc[7][...], cc[8][...], dh)
        oo[3][...] = rr[9][...] + dx
        aa[2][...] += dsh
        aa[3][...] += dsc
        aa[4][...] += dg0

    return _rows(name, body, T, 256,
                 [(dq, 'cur'), (dk, 'cur'), (dv, 'cur'), (C, 'cur'), (S1, 'cur'), (S2, 'cur'), (cq_raw, 'cur'), (ckv_raw, 'cur'),
                  (x, 'cur'), (dxo, 'cur')],
                 [w_uq, w_ukv_k, w_ukv_v, w_dq, w_dkv, qg, kvg, g0, sc],
                 [(HP, MM), (Q_LORA, MM), (KV_LORA + HEAD_PAD, MM), (D, F32)],
                 accs=[(1, Q_LORA), (1, KV_LORA), (1, D), (1, D), (1, D)])


HALO = 32


def _conv_glu(name, h, w_pw1, b_pw1):
    T, D = h.shape

    def body(i, n, rr, cc, oo, aa, ss):
        a = _dot(rr[0][...], cc[0][...]) + cc[1][...]
        oo[0][...] = a
        oo[1][...] = a[:, :D] * _sigmoid(a[:, D:])

    return _rows(name, body, T, 512, [(h, 'cur')], [w_pw1, b_pw1], [(2 * D, F32), (D, F32)])


def _layernorm_parts(uc):
    xc = uc - jnp.mean(uc, axis=-1, keepdims=True)
    r = lax.rsqrt(jnp.mean(xc * xc, axis=-1, keepdims=True) + EPS)
    return xc * r, r


def _conv_dw(name, u, w_dw, b_dw, ln_g, ln_b, w_pw2, b_pw2, x, g1, gt):
    T, D = u.shape
    tm = min(256, T)

    def body(i, n, rr, cc, oo, aa, ss):
        ext = ss[0]
        ext[0:HALO, :] = jnp.where(i > 0, rr[1][tm - HALO:tm, :], 0.0)
        ext[HALO:HALO + tm, :] = rr[0][...]
        uc = jnp.zeros((tm, D), F32) + cc[1][...]
        for kk in range(CONV_W):
            uc = uc + ext[pl.ds(HALO - (CONV_W - 1) + kk, tm), :] * cc[0][kk:kk + 1, :]
        xh, _ = _layernorm_parts(uc)
        ln = xh * cc[2][...] + cc[3][...]
        z = (ln * _sigmoid(ln)).astype(MM)
        y = _dot(z, cc[4][...]) + cc[5][...]
        oo[0][...] = uc
        oo[1][...] = z
        oo[2][...] = y
        oo[3][...] = rr[2][...] + cc[7][...] * _rms(y, cc[6][...])

    return _rows(name, body, T, tm, [(u, 'cur'), (u, 'prev'), (x, 'cur')], [w_dw, b_dw, ln_g, ln_b, w_pw2, b_pw2, g1, gt],
                 [(D, F32), (D, MM), (D, F32), (D, F32)], scratch=[pltpu.VMEM((tm + HALO, D), F32)])


def _conv_bwd1(name, dy, uc, w_pw2, ln_g, ln_b):
    T, D = uc.shape

    def body(i, n, rr, cc, oo, aa, ss):
        dz = _dot_nt(rr[0][...], cc[0][...])
        xh, r = _layernorm_parts(rr[1][...])
        g = cc[1][...]
        ln = xh * g + cc[2][...]
        sg = _sigmoid(ln)
        dln = dz * (sg * (1.0 + ln * (1.0 - sg)))
        aa[0][...] += jnp.sum(dln * xh, axis=0, keepdims=True)
        aa[1][...] += jnp.sum(dln, axis=0, keepdims=True)
        dxh = dln * g
        duc = r * (dxh - jnp.mean(dxh, axis=-1, keepdims=True) - xh * jnp.mean(dxh * xh, axis=-1, keepdims=True))
        aa[2][...] += jnp.sum(duc, axis=0, keepdims=True)
        oo[0][...] = duc

    return _rows(name, body, T, 256, [(dy, 'cur'), (uc, 'cur')], [w_pw2, ln_g, ln_b], [(D, F32)], accs=[(1, D)] * 3)


def _conv_bwd2(name, duc, u, a, x, dxo, w_dw, w_pw1, g0, sc):
    T, D = u.shape
    tm = min(256, T)

    def body(i, n, rr, cc, oo, aa, ss):
        extd, extu = ss[0], ss[1]
        dcur = rr[0][...]
        extd[0:tm, :] = dcur
        extd[tm:tm + HALO, :] = jnp.where(i < n - 1, rr[1][0:HALO, :], 0.0)
        extu[0:HALO, :] = jnp.where(i > 0, rr[3][tm - HALO:tm, :], 0.0)
        extu[HALO:HALO + tm, :] = rr[2][...]
        du = jnp.zeros((tm, D), F32)
        for kk in range(CONV_W):
            du = du + extd[pl.ds(CONV_W - 1 - kk, tm), :] * cc[0][kk:kk + 1, :]
            aa[0][kk:kk + 1, :] += jnp.sum(dcur * extu[pl.ds(HALO - (CONV_W - 1) + kk, tm), :], axis=0, keepdims=True)
        av = rr[4][...]
        a1, sg = av[:, :D], _sigmoid(av[:, D:])
        da = jnp.concatenate([du * sg, du * a1 * (sg * (1.0 - sg))], axis=1)
        aa[1][...] += jnp.sum(da, axis=0, keepdims=True)
        da = da.astype(MM)
        oo[0][...] = da
        dx, dsh, dsc, dg0 = _prenorm_bwd(rr[5][...], cc[2][...], cc[3][...], _dot_nt(da, cc[1][...]))
        oo[1][...] = rr[6][...] + dx
        aa[2][...] += dsh
        aa[3][...] += dsc
        aa[4][...] += dg0

    return _rows(name, body, T, tm,
                 [(duc, 'cur'), (duc, 'next'), (u, 'cur'), (u, 'prev'), (a, 'cur'), (x, 'cur'), (dxo, 'cur')],
                 [w_dw, w_pw1, g0, sc], [(2 * D, MM), (D, F32)],
                 accs=[(32, D), (1, 2 * D), (1, D), (1, D), (1, D)],
                 scratch=[pltpu.VMEM((tm + HALO, D), F32), pltpu.VMEM((tm + HALO, D), F32)])


PHALO = 16


def _pool_fwd(name, h, w, b, scale, x, g1, gt):
    T, D = h.shape
    G = len(POOL_WINDOWS)
    Cg = D // G
    tm = min(256, T)

    def body(i, n, rr, cc, oo, aa, ss):
        ext = ss[0]
        ext[0:PHALO, :] = jnp.where(i > 0, rr[1][tm - PHALO:tm, :], 0.0)
        ext[PHALO:PHALO + tm, :] = rr[0][...]
        t_glob = i * tm + lax.broadcasted_iota(jnp.int32, (tm, 1), 0)
        ps, ys = [], []
        for g, win in enumerate(POOL_WINDOWS):
            cols = slice(g * Cg, (g + 1) * Cg)
            s = ext[pl.ds(PHALO, tm), cols]
            for j in range(1, win):
                s = s + ext[pl.ds(PHALO - j, tm), cols]
            cnt = jnp.minimum(t_glob + 1, win).astype(F32)
            p = (s / cnt - ext[pl.ds(PHALO, tm), cols]).astype(MM)
            ps.append(p)
            ys.append(_dot(p, cc[0][g]) + cc[1][:, cols])
        ypre = jnp.concatenate(ys, axis=1)
        y = ypre * cc[2][...]
        oo[0][...] = jnp.concatenate(ps, axis=1)
        oo[1][...] = ypre
        oo[2][...] = y
        oo[3][...] = rr[2][...] + cc[4][...] * _rms(y, cc[3][...])

    return _rows(name, body, T, tm, [(h, 'cur'), (h, 'prev'), (x, 'cur')], [w, b, scale, g1, gt],
                 [(D, MM), (D, F32), (D, F32), (D, F32)], scratch=[pltpu.VMEM((tm + PHALO, D), F32)])


def _pool_bwd1(name, dy, ypre, scale, w):
    T, D = ypre.shape
    G = len(POOL_WINDOWS)
    Cg = D // G

    def body(i, n, rr, cc, oo, aa, ss):
        dyv = rr[0][...].astype(F32)
        aa[0][...] += jnp.sum(dyv * rr[1][...], axis=0, keepdims=True)
        dypre = dyv * cc[0][...]
        aa[1][...] += jnp.sum(dypre, axis=0, keepdims=True)
        dypre = dypre.astype(MM)
        oo[1][...] = dypre
        oo[0][...] = jnp.concatenate([_dot_nt(dypre[:, g * Cg:(g + 1) * Cg], cc[1][g]) for g in range(G)], axis=1)

    return _rows(name, body, T, 256, [(dy, 'cur'), (ypre, 'cur')], [scale, w], [(D, F32), (D, MM)], accs=[(1, D)] * 2)


def _pool_bwd2(name, dp, x, dxo, g0, sc):
    T, D = x.shape
    G = len(POOL_WINDOWS)
    Cg = D // G
    tm = min(256, T)

    def body(i, n, rr, cc, oo, aa, ss):
        ext = ss[0]
        t_glob = i * tm + lax.broadcasted_iota(jnp.int32, (tm, 1), 0)
        dcur = rr[0][...]
        dhs = []
        for g, win in enumerate(POOL_WINDOWS):
            cols = slice(g * Cg, (g + 1) * Cg)
            cnt = jnp.minimum(t_glob + 1, win).astype(F32)
            ext[0:tm, cols] = dcur[:, cols] / cnt
            ext[tm:tm + PHALO, cols] = jnp.where(i < n - 1, rr[1][0:PHALO, cols] * (1.0 / win), 0.0)
        for g, win in enumerate(POOL_WINDOWS):
            cols = slice(g * Cg, (g + 1) * Cg)
            s = ext[pl.ds(0, tm), cols]
            for j in range(1, win):
                s = s + ext[pl.ds(j, tm), cols]
            dhs.append(s - dcur[:, cols])
        dx, dsh, dsc, dg0 = _prenorm_bwd(rr[2][...], cc[0][...], cc[1][...], jnp.concatenate(dhs, axis=1))
        oo[0][...] = rr[3][...] + dx
        aa[0][...] += dsh
        aa[1][...] += dsc
        aa[2][...] += dg0

    return _rows(name, body, T, tm, [(dp, 'cur'), (dp, 'next'), (x, 'cur'), (dxo, 'cur')], [g0, sc], [(D, F32)],
                 accs=[(1, D)] * 3, scratch=[pltpu.VMEM((tm + PHALO, D), F32)])


def _loss_head(x, tgt):
    T, D = x.shape

    def body(i, n, rr, cc, oo, aa, ss):
        err = rr[0][...] - rr[1][...]
        oo[0][...] = err * (1.0 / D)
        aa[0][...] += jnp.sum(err * err, axis=0, keepdims=True)

        @pl.when(i == n - 1)
        def _():
            aa[1][...] = jnp.broadcast_to(jnp.sum(aa[0][...], axis=1, keepdims=True) * (0.5 / D), (1, 128))

    dx, _, loss_row = _rows("loss_head", body, T, 512, [(x, 'cur'), (tgt, 'cur')], [], [(D, F32)], accs=[(1, D), (1, 128)])
    return dx, loss_row


def _adamw(name, w, g, m, v):
    shape = w.shape
    C = shape[-1]
    R = w.size // C
    w2, g2, m2, v2 = (t.reshape(R, C) for t in (w, g, m, v))
    br = R
    if R * C * 4 > (1 << 20):
        br = 8
        while br * 2 * C * 4 <= (1 << 20) and R % (br * 2) == 0:
            br *= 2
    b1c = 1.0 - ADAM_B1 ** ADAM_STEP
    b2c = 1.0 - ADAM_B2 ** ADAM_STEP

    def body(w_ref, g_ref, m_ref, v_ref, d_ref, mo_ref, vo_ref):
        gv = g_ref[...]
        mn = ADAM_B1 * m_ref[...] + (1.0 - ADAM_B1) * gv
        vn = ADAM_B2 * v_ref[...] + (1.0 - ADAM_B2) * (gv * gv)
        d_ref[...] = -ADAM_LR * ((mn / b1c) / (jnp.sqrt(vn / b2c) + ADAM_EPS) + ADAM_WD * w_ref[...])
        mo_ref[...] = mn
        vo_ref[...] = vn

    spec = pl.BlockSpec((br, C), lambda r: (r, 0))
    outs = pl.pallas_call(body, grid=(R // br,), in_specs=[spec] * 4, out_specs=[spec] * 3,
                          out_shape=[jax.ShapeDtypeStruct((R, C), F32)] * 3, name=name,
                          compiler_params=_cparams(("arbitrary",)))(w2, g2, m2, v2)
    return tuple(t.reshape(shape) for t in outs)


def _layer_shards(g, ax):
    s = g.shape
    r = g.reshape(s[:ax] + (N_DEV, s[ax] // N_DEV) + s[ax + 1:])
    return (jnp.moveaxis(r, ax, 0) if ax else r).reshape(N_DEV, -1)


def _unshard(g, ax):
    r = jnp.moveaxis(g, 0, ax)
    s = r.shape
    return r.reshape(s[:ax] + (s[ax] * s[ax + 1],) + s[ax + 2:])


def _pack(parts, dtype, row_mult):
    lead = parts[0].shape[:-1]
    flat = jnp.concatenate([p.astype(dtype) for p in parts], axis=-1)
    n = flat.shape[-1]
    per = row_mult * 1024
    tot = -(-n // per) * per
    flat = jnp.pad(flat, [(0, 0)] * len(lead) + [(0, tot - n)])
    return flat.reshape(lead + (tot // 1024, 1024))


def _pad_heads(w, lo, hi):
    K = w.shape[0]
    r = w.reshape(K, N_HEADS, -1)[:, :, lo:hi]
    return jnp.pad(r, ((0, 0), (0, 0), (0, HEAD_PAD - (hi - lo)))).reshape(K, N_HEADS * HEAD_PAD)


def kernel(x, c, positions, ada_w, ada_b, norm_g, mla_w_dq, mla_q_norm_g, mla_w_uq, mla_w_dkv, mla_kv_norm_g, mla_w_ukv, mla_w_o, conv_w_pw1, conv_b_pw1, conv_w_dw, conv_b_dw, conv_ln_g, conv_ln_b, conv_w_pw2, conv_b_pw2, pool_w, pool_b, pool_scale, ffn_w1, ffn_w2, loss_target, m_ada_w, m_ada_b, m_norm_g, m_mla_w_dq, m_mla_q_norm_g, m_mla_w_uq, m_mla_w_dkv, m_mla_kv_norm_g, m_mla_w_ukv, m_mla_w_o, m_conv_w_pw1, m_conv_b_pw1, m_conv_w_dw, m_conv_b_dw, m_conv_ln_g, m_conv_ln_b, m_conv_w_pw2, m_conv_b_pw2, m_pool_w, m_pool_b, m_pool_scale, m_ffn_w1, m_ffn_w2, v_ada_w, v_ada_b, v_norm_g, v_mla_w_dq, v_mla_q_norm_g, v_mla_w_uq, v_mla_w_dkv, v_mla_kv_norm_g, v_mla_w_ukv, v_mla_w_o, v_conv_w_pw1, v_conv_b_pw1, v_conv_w_dw, v_conv_b_dw, v_conv_ln_g, v_conv_ln_b, v_conv_w_pw2, v_conv_b_pw2, v_pool_w, v_pool_b, v_pool_scale, v_ffn_w1, v_ffn_w2):
    args = dict(locals())
    W = {n: args[n] for n, _ in WEIGHTS}
    M1 = {n: args['m_' + n] for n, _ in WEIGHTS}
    V2 = {n: args['v_' + n] for n, _ in WEIGHTS}
    D = D_MODEL
    T = x.shape[1]
    L = ffn_w1.shape[0]
    xi, yi, ci = _place()
    me = 4 * xi + 2 * yi + ci
    n_ada = ada_w.shape[2]

    small_sizes = [W[n].size for n in SMALL]
    small_in = _pack([c.reshape(-1)] + [W[n].reshape(-1) for n in SMALL], F32, 8)
    small_all = _ag_small("ag_small_params", small_in).reshape(N_DEV, -1)
    c_all = small_all[:, :D]
    Ws = {}
    off = D
    for n, sz in zip(SMALL, small_sizes):
        Ws[n] = _unshard(small_all[:, off:off + sz].reshape((N_DEV,) + W[n].shape), SHARD_AXIS[n])
        off += sz
    c16 = jnp.pad(c_all, ((0, 16 - N_DEV), (0, 0)))

    ada_b_cols = lax.dynamic_slice_in_dim(ada_b, me * n_ada, n_ada, axis=1).reshape(L, 1, n_ada)
    mod_part = _mod_part(c16, ada_w, ada_b_cols)[:, :N_DEV]
    mod_all = _ag_small("ag_mod", mod_part.reshape(L * N_DEV, n_ada)).reshape(N_DEV, L, N_DEV, n_ada)
    mod_mine = lax.dynamic_index_in_dim(mod_all, me, axis=2, keepdims=False)
    mod = jnp.transpose(mod_mine, (1, 0, 2)).reshape(L, 6, 1, D)

    rest = [n for n in BIG if not n.startswith('ffn')]
    rest_all, = _ag_big("ag_weights", [_pack([W[n].reshape(-1) for n in rest], MM, 32)])
    wf = [ffn_w1.astype(MM), ffn_w2.astype(MM)]
    wf, rest_all, mod = lax.optimization_barrier((wf, rest_all, mod))
    wf_land = [lax.dynamic_update_slice(lax.empty((N_DEV,) + w.shape, MM), w[None], (me, 0, 0, 0)) for w in wf]
    ag_sems, wf_thru, wf_land, ag_token = _copies_start("ag_ffn_start", wf, wf_land, FIRST_LEVEL_PEERS, False)
    rest_all = rest_all.reshape(N_DEV, -1)
    Wb = {}
    off = 0
    for n in rest:
        Wb[n] = rest_all[:, off:off + W[n].size].reshape((N_DEV,) + W[n].shape)
        off += W[n].size
    full = lambda n: _unshard(Wb[n], SHARD_AXIS[n])
    w_dq, w_uq, w_dkv, w_ukv, w_o = full('mla_w_dq'), full('mla_w_uq'), full('mla_w_dkv'), full('mla_w_ukv'), full('mla_w_o')
    w_pw1, w_pw2, w_pool = full('conv_w_pw1'), full('conv_w_pw2'), full('pool_w')
    n_mla = w_dq.shape[0]
    w_uq_p = [_pad_heads(w_uq[j], 0, QK_NOPE + QK_ROPE) for j in range(n_mla)]
    w_ukv_k = [_pad_heads(w_ukv[j], 0, QK_NOPE) for j in range(n_mla)]
    w_ukv_v = [_pad_heads(w_ukv[j], QK_NOPE, QK_NOPE + V_HEAD) for j in range(n_mla)]
    w_dkv_p = [jnp.pad(jnp.concatenate([w_dkv[j][:, :KV_LORA], jnp.zeros((D, QK_NOPE), MM), w_dkv[j][:, KV_LORA:]], axis=1),
                       ((0, 0), (0, HEAD_PAD - QK_NOPE - QK_ROPE))) for j in range(n_mla)]
    w_o_p = [jnp.pad(w_o[j].reshape(N_HEADS, V_HEAD, D), ((0, 0), (0, HEAD_PAD - V_HEAD), (0, 0))).reshape(N_HEADS * HEAD_PAD, D)
             for j in range(n_mla)]
    w_dw32 = jnp.pad(Ws['conv_w_dw'], ((0, 0), (0, 32 - CONV_W), (0, 0)))
    row = lambda t: t.reshape(1, -1)

    half = QK_ROPE // 2
    inv_freq = ROPE_THETA ** (-jnp.arange(0, QK_ROPE, 2, dtype=F32) / QK_ROPE)
    invf = jnp.zeros((1, HEAD_PAD), F32).at[0, QK_NOPE:QK_NOPE + half].set(inv_freq).at[0, QK_NOPE + half:QK_NOPE + QK_ROPE].set(inv_freq)
    rC, rS1, rS2 = _rope_tables(positions.reshape(T, 1).astype(F32), invf)

    xs = x.reshape(T, D)
    saved = []
    for i in range(L):
        kind, j = i % 3, i // 3
        sh_m, sc_m, gt_m, sh_f, sc_f, gt_f = (mod[i, r] for r in range(6))
        g = [row(Ws['norm_g'][i, r]) for r in range(4)]
        st = dict(x0=xs)
        if i == 0:
            sc_m = sc_m + ag_token[0:1, 0:1]
        if kind == 0:
            h = _prenorm(f"prenorm_m{i}", xs, g[0], sc_m, sh_m, MM)
            cq_raw, cq, ckv_raw, ckv, q, k, v = _mla_proj(f"mla_proj{i}", h, rC, rS1, rS2, w_dq[j], row(Ws['mla_q_norm_g'][j]), w_uq_p[j],
                                                          w_dkv_p[j], row(Ws['mla_kv_norm_g'][j]), w_ukv_k[j], w_ukv_v[j])
            o, lse = _attn_fwd(f"attn_fwd{i}", q, k, v)
            y, xs = _mm_post(f"mla_out{i}", o, w_o_p[j], None, xs, g[1], gt_m)
            st.update(h=h, cq_raw=cq_raw, cq=cq, ckv_raw=ckv_raw, ckv=ckv, q=q, k=k, v=v, o=o, lse=lse, y=y)
        elif kind == 1:
            h = _prenorm(f"prenorm_m{i}", xs, g[0], sc_m, sh_m, MM)
            a, u = _conv_glu(f"conv_glu{i}", h, w_pw1[j], row(W['conv_b_pw1'][j]))
            uc, z, y, xs = _conv_dw(f"conv_dw{i}", u, w_dw32[j], row(W['conv_b_dw'][j]), row(W['conv_ln_g'][j]), row(W['conv_ln_b'][j]),
                                    w_pw2[j], row(W['conv_b_pw2'][j]), xs, g[1], gt_m)
            st.update(h=h, a=a, u=u, uc=uc, z=z, y=y)
        else:
            h = _prenorm(f"prenorm_m{i}", xs, g[0], sc_m, sh_m, F32)
            p, ypre, y, xs = _pool_fwd(f"pool_fwd{i}", h, w_pool[j], row(Ws['pool_b'][j]), row(Ws['pool_scale'][j]), xs, g[1], gt_m)
            st.update(p=p, ypre=ypre, y=y)
        st['x1'] = xs
        if i == 0:
            wg = _copies_wait("ag_ffn_wait", ag_sems, wf_thru, wf_land, xs, FIRST_LEVEL_PEERS, False)
            w1g, w2g = _ag_forward("ag_ffn_forward", wg)
        hf = _prenorm(f"prenorm_f{i}", xs, g[2], sc_f, sh_f, MM)
        af, yf, xs = _ffn_fwd(f"ffn_fwd{i}", i, hf, w1g, w2g, xs, g[3], gt_f)
        st.update(hf=hf, af=af, yf=yf)
        saved.append(st)

    dx, loss_row = _loss_head(xs, loss_target.reshape(T, D))

    G = {}
    dmod = [None] * L
    dnorm = [None] * L
    rs_pending = None
    ffn_red = [lax.empty(ffn_w1.shape, F32), lax.empty(ffn_w2.shape, F32)]
    for i in reversed(range(L)):
        kind, j = i % 3, i // 3
        sh_m, sc_m, gt_m, sh_f, sc_f, gt_f = (mod[i, r] for r in range(6))
        g = [row(Ws['norm_g'][i, r]) for r in range(4)]
        st = saved[i]
        dy, dg3, dgt_f, _ = _post_bwd(f"post_bwd_f{i}", dx, st['yf'], g[3], gt_f)
        da, dx, dsh_f, dsc_f, dg2 = _ffn_bwd(f"ffn_bwd{i}", i, dy, st['af'], w1g, w2g, st['x1'], dx, g[2], sc_f)
        wire1, own1 = _mm_tn_wire(f"ffn_dw1_{i}", st['hf'], da, me, False, False)
        wire2, own2 = _mm_tn_wire(f"ffn_dw2_{i}", st['af'], dy, me, True, True)
        if rs_pending is not None:
            ffn_red = _rs_finish(rs_pending, wire2, me, ffn_red)
        wires = [wire1, wire2]
        rs_sems, wires_thru, rs_lands, rs_token = _copies_start(f"rs_start{i}", wires, [lax.empty(w.shape, MM) for w in wires], ALL_PEERS, True)
        rs_pending = (i, rs_sems, wires_thru, rs_lands, [own1, own2])
        dy, dg1, dgt_m, dysum = _post_bwd(f"post_bwd_m{i}", dx, st['y'], g[1], gt_m + rs_token[0:1, 0:1])
        if kind == 0:
            do = _mm_nt_rows(f"mla_do{i}", dy, w_o_p[j])
            delta = _attn_delta(f"attn_delta{i}", do, st['o'])
            dq, dk, dv = _attn_bwd(f"attn_bwd{i}", st['q'], st['k'], st['v'], do, st['lse'], delta)
            dq_pre, dcq_raw, dckv_all, dx, dqg, dkvg, dsh_m, dsc_m, dg0 = _mla_proj_bwd(
                f"mla_proj_bwd{i}", dq, dk, dv, rC, rS1, rS2, st['cq_raw'], st['ckv_raw'], st['x0'], dx, w_uq_p[j], w_ukv_k[j], w_ukv_v[j],
                w_dq[j], w_dkv_p[j], row(Ws['mla_q_norm_g'][j]), row(Ws['mla_kv_norm_g'][j]), g[0], sc_m)
            dwo = _mm_tn(f"mla_dwo{i}", st['o'], dy)
            dwuq = _mm_tn(f"mla_dwuq{i}", st['cq'], dq_pre)
            dwk = _mm_tn(f"mla_dwukvk{i}", st['ckv'], dk)
            dwv = _mm_tn(f"mla_dwukvv{i}", st['ckv'], dv)
            dwdq = _mm_tn(f"mla_dwdq{i}", st['h'], dcq_raw)
            dwdkv = _mm_tn(f"mla_dwdkv{i}", st['h'], dckv_all)
            G.setdefault('mla_w_o', [None] * n_mla)[j] = dwo.reshape(N_HEADS, HEAD_PAD, D)[:, :V_HEAD].reshape(N_HEADS * V_HEAD, D)
            G.setdefault('mla_w_uq', [None] * n_mla)[j] = dwuq.reshape(Q_LORA, N_HEADS, HEAD_PAD)[:, :, :QK_NOPE + QK_ROPE].reshape(Q_LORA, -1)
            G.setdefault('mla_w_ukv', [None] * n_mla)[j] = jnp.concatenate(
                [dwk.reshape(KV_LORA, N_HEADS, HEAD_PAD)[:, :, :QK_NOPE], dwv.reshape(KV_LORA, N_HEADS, HEAD_PAD)[:, :, :V_HEAD]], axis=2).reshape(KV_LORA, -1)
            G.setdefault('mla_w_dq', [None] * n_mla)[j] = dwdq
            G.setdefault('mla_w_dkv', [None] * n_mla)[j] = jnp.concatenate([dwdkv[:, :KV_LORA], dwdkv[:, KV_LORA + QK_NOPE:KV_LORA + QK_NOPE + QK_ROPE]], axis=1)
            G.setdefault('mla_q_norm_g', [None] * n_mla)[j] = dqg[0]
            G.setdefault('mla_kv_norm_g', [None] * n_mla)[j] = dkvg[0]
        elif kind == 1:
            duc, dlng, dlnb, dbdw = _conv_bwd1(f"conv_bwd1_{i}", dy, st['uc'], w_pw2[j], row(W['conv_ln_g'][j]), row(W['conv_ln_b'][j]))
            da, dx, dwdw, dbpw1, dsh_m, dsc_m, dg0 = _conv_bwd2(f"conv_bwd2_{i}", duc, st['u'], st['a'], st['x0'], dx, w_dw32[j], w_pw1[j], g[0], sc_m)
            G['conv_w_pw2'] = [_mm_tn(f"conv_dwpw2_{i}", st['z'], dy)]
            G['conv_w_pw1'] = [_mm_tn(f"conv_dwpw1_{i}", st['h'], da)]
            G['conv_w_dw'] = [dwdw[:CONV_W]]
            G['conv_b_pw1'], G['conv_b_dw'], G['conv_ln_g'], G['conv_ln_b'], G['conv_b_pw2'] = [dbpw1[0]], [dbdw[0]], [dlng[0]], [dlnb[0]], [dysum[0]]
        else:
            dp, dypre, dscale, dpb = _pool_bwd1(f"pool_bwd1_{i}", dy, st['ypre'], row(Ws['pool_scale'][j]), w_pool[j])
            dx, dsh_m, dsc_m, dg0 = _pool_bwd2(f"pool_bwd2_{i}", dp, st['x0'], dx, g[0], sc_m)
            G['pool_w'] = [_mm_tn(f"pool_dw{i}", st['p'], dypre, diag=len(POOL_WINDOWS))]
            G['pool_b'] = [dpb.reshape(len(POOL_WINDOWS), -1)]
            G['pool_scale'] = [dscale[0]]
        dmod[i] = jnp.concatenate([dsh_m, dsc_m, dgt_m, dsh_f, dsc_f, dgt_f], axis=1)
        dnorm[i] = jnp.concatenate([dg0, dg1, dg2, dg3], axis=0)
    G['norm_g'] = dnorm
    grad_x = dx.reshape(x.shape)

    rs_names = [n for n, ax in WEIGHTS if ax is not None and n != 'ada_w' and not n.startswith('ffn')]
    pieces = [(n, _layer_shards(g, SHARD_AXIS[n] - 1)) for n in rs_names for g in G[n]]
    big = [(n, p) for n, p in pieces if p.shape[1] % (8 * 1024) == 0]
    small = [(n, p) for n, p in pieces if p.shape[1] % (8 * 1024) != 0]
    packed = jnp.concatenate([p.reshape(N_DEV, -1, 1024) for _, p in big] + [_pack([p for _, p in small], F32, 8)], axis=1)
    ffn_red = _rs_finish(rs_pending, dx, me, ffn_red)
    red = _reduce_scatter("rs", [packed], ci, 2 * xi + yi)[0]
    grads = {'ffn_w1': ffn_red[0], 'ffn_w2': ffn_red[1]}
    got = {}
    row0 = 0
    for n, p in big:
        rows = p.shape[1] // 1024
        got.setdefault(n, []).append(red[row0:row0 + rows])
        row0 += rows
    tail = red[row0:].reshape(-1)
    off = 0
    for n, p in small:
        got.setdefault(n, []).append(tail[off:off + p.shape[1]])
        off += p.shape[1]
    for n in rs_names:
        grads[n] = jnp.stack([g_.reshape(W[n].shape[1:]) for g_ in got[n]], axis=0)

    dmod_mine = jnp.concatenate(dmod, axis=1).reshape(-1)
    fin_in = _pack([dmod_mine] + [G[n][0].reshape(-1) for n in REPL] + [loss_row.reshape(-1)], F32, 8)
    fin_all = _ag_small("ag_final", fin_in)
    fin_sum = _sum_devices("final_sum", fin_all).reshape(-1)
    nm = L * 6 * D
    grads['ada_b'] = fin_sum[:nm].reshape(L, 6 * D)
    off = nm
    for n in REPL:
        grads[n] = fin_sum[off:off + W[n].size].reshape(W[n].shape)
        off += W[n].size
    loss = fin_sum[off]
    dmod_all = fin_all.reshape(N_DEV, -1)[:, :nm].reshape(N_DEV, L, 6 * D)
    dmod_cols = lax.dynamic_slice_in_dim(dmod_all, me * n_ada, n_ada, axis=2)
    dmod16 = jnp.pad(jnp.transpose(dmod_cols, (1, 0, 2)), ((0, 0), (0, 16 - N_DEV), (0, 0)))
    grads['ada_w'] = _ada_w_grad(c16, dmod16)

    deltas, new_m, new_v = {}, {}, {}
    for n, _ in WEIGHTS:
        deltas[n], new_m[n], new_v[n] = _adamw("adamw_" + n, W[n], grads[n], M1[n], V2[n])
    names = [n for n, _ in WEIGHTS]
    return (loss, grad_x, *[grads[n] for n in names], *[deltas[n] for n in names], *[new_m[n] for n in names],
            *[new_v[n] for n in names])
```

```python
import functools
import math

import jax
import jax.numpy as jnp
from jax import lax
from jax.experimental import pallas as pl
from jax.experimental.pallas import tpu as pltpu

F32 = jnp.float32
MM = jnp.bfloat16
EPS = 1e-6
NEG = -1e30
N_DEV = 8
VMEM_LIMIT = 48 * 1024 * 1024
MESH = pl.DeviceIdType.MESH

D_MODEL = 1024
N_HEADS = 16
HEAD_PAD = 128
QK_NOPE, QK_ROPE, V_HEAD = 64, 32, 64
Q_LORA, KV_LORA = 384, 256
CHUNK = 64
CONV_W = 31
POOL_WINDOWS = (2, 4, 8, 16)
ROPE_THETA = 10000.0
ATT_SCALE = 1.0 / math.sqrt(QK_NOPE + QK_ROPE)

ADAM_LR, ADAM_B1, ADAM_B2, ADAM_EPS, ADAM_WD, ADAM_STEP = 0.001, 0.9, 0.999, 1e-08, 0.01, 10

WEIGHTS = [('ada_w', 2), ('ada_b', None), ('norm_g', 2), ('mla_w_dq', 1), ('mla_q_norm_g', 1), ('mla_w_uq', 2),
           ('mla_w_dkv', 1), ('mla_kv_norm_g', 1), ('mla_w_ukv', 2), ('mla_w_o', 1), ('conv_w_pw1', 2),
           ('conv_b_pw1', None), ('conv_w_dw', 2), ('conv_b_dw', None), ('conv_ln_g', None), ('conv_ln_b', None),
           ('conv_w_pw2', 1), ('conv_b_pw2', None), ('pool_w', 2), ('pool_b', 2), ('pool_scale', 1),
           ('ffn_w1', 2), ('ffn_w2', 1)]
SHARD_AXIS = dict(WEIGHTS)
BIG = ['mla_w_dq', 'mla_w_uq', 'mla_w_dkv', 'mla_w_ukv', 'mla_w_o', 'conv_w_pw1', 'conv_w_pw2', 'pool_w', 'ffn_w1', 'ffn_w2']
SMALL = ['norm_g', 'mla_q_norm_g', 'mla_kv_norm_g', 'conv_w_dw', 'pool_b', 'pool_scale']
REPL = ['conv_b_pw1', 'conv_b_dw', 'conv_ln_g', 'conv_ln_b', 'conv_b_pw2']


def _dot(a, b):
    return jnp.dot(a.astype(MM), b.astype(MM), preferred_element_type=F32)


def _dot_nt(a, b):
    return lax.dot_general(a.astype(MM), b.astype(MM), (((1,), (1,)), ((), ())), preferred_element_type=F32)


def _dot_tn(a, b):
    return lax.dot_general(a.astype(MM), b.astype(MM), (((0,), (0,)), ((), ())), preferred_element_type=F32)


def _sigmoid(x):
    return 1.0 / (1.0 + jnp.exp(-x))


def _rstd(x):
    return lax.rsqrt(jnp.mean(x * x, axis=-1, keepdims=True) + EPS)


def _rms(x, g):
    return x * _rstd(x) * g


def _rms_bwd(x, g, dout):
    r = _rstd(x)
    xn = x * r
    dg = jnp.sum(dout * xn, axis=0, keepdims=True)
    dxn = dout * g
    dx = r * (dxn - xn * jnp.mean(dxn * xn, axis=-1, keepdims=True))
    return dx, dg


def _prenorm_bwd(x, g0, sc, dh):
    r = _rstd(x)
    xn = x * r
    dsh = jnp.sum(dh, axis=0, keepdims=True)
    dsc = jnp.sum(dh * (xn * g0), axis=0, keepdims=True)
    dn = dh * (1.0 + sc)
    dg0 = jnp.sum(dn * xn, axis=0, keepdims=True)
    dxn = dn * g0
    dx = r * (dxn - xn * jnp.mean(dxn * xn, axis=-1, keepdims=True))
    return dx, dsh, dsc, dg0


def _cparams(sem):
    return pltpu.CompilerParams(dimension_semantics=sem, vmem_limit_bytes=VMEM_LIMIT)


def _rows(name, body, n_rows, tm, rows, consts, outs, accs=(), scratch=()):
    tm = min(tm, n_rows)
    nblk = n_rows // tm
    nr, nc, no, na = len(rows), len(consts), len(outs), len(accs)
    in_specs, args = [], []
    for a, kind in rows:
        if kind == 'cur':
            im = lambda i: (i, 0)
        elif kind == 'prev':
            im = lambda i: (jnp.maximum(i - 1, 0), 0)
        else:
            im = lambda i: (jnp.minimum(i + 1, nblk - 1), 0)
        in_specs.append(pl.BlockSpec((tm, a.shape[1]), im))
        args.append(a)
    for a in consts:
        in_specs.append(pl.BlockSpec(a.shape, lambda i, nd=a.ndim: (0,) * nd))
        args.append(a)
    out_specs = [pl.BlockSpec((tm, c), lambda i: (i, 0)) for c, _ in outs]
    out_specs += [pl.BlockSpec(s, lambda i, nd=len(s): (0,) * nd) for s in accs]
    out_shape = [jax.ShapeDtypeStruct((n_rows, c), dt) for c, dt in outs]
    out_shape += [jax.ShapeDtypeStruct(s, F32) for s in accs]

    def kern(*refs):
        i = pl.program_id(0)
        rr = refs[:nr]
        cc = refs[nr:nr + nc]
        oo = refs[nr + nc:nr + nc + no]
        aa = refs[nr + nc + no:nr + nc + no + na]
        ss = refs[nr + nc + no + na:]

        @pl.when(i == 0)
        def _():
            for a in aa:
                a[...] = jnp.zeros(a.shape, F32)

        body(i, nblk, rr, cc, oo, aa, ss)

    return pl.pallas_call(kern, grid=(nblk,), in_specs=in_specs, out_specs=out_specs, out_shape=out_shape,
                          scratch_shapes=list(scratch), name=name, compiler_params=_cparams(("arbitrary",)))(*args)


def _place():
    return lax.axis_index("x"), lax.axis_index("y"), lax.axis_index("c")


def _ag_small(name, xs):
    R, C = xs.shape

    def body(x_ref, out_ref, send_sems, recv_sems):
        x, y, c = _place()
        me = 4 * x + 2 * y + c
        out_ref[me] = x_ref[...]
        copies = []
        for k in range(1, N_DEV):
            peer = ((1 - x) if k & 4 else x, (1 - y) if k & 2 else y, (1 - c) if k & 1 else c)
            cp = pltpu.make_async_remote_copy(src_ref=x_ref, dst_ref=out_ref.at[me], send_sem=send_sems.at[k - 1],
                                              recv_sem=recv_sems.at[k - 1], device_id=peer, device_id_type=MESH)
            cp.start()
            copies.append(cp)
        for cp in copies:
            cp.wait()

    return pl.pallas_call(
        body, out_shape=jax.ShapeDtypeStruct((N_DEV, R, C), xs.dtype),
        in_specs=[pl.BlockSpec(memory_space=pltpu.VMEM)], out_specs=pl.BlockSpec(memory_space=pltpu.VMEM),
        scratch_shapes=[pltpu.SemaphoreType.DMA((N_DEV - 1,)), pltpu.SemaphoreType.DMA((N_DEV - 1,))], name=name)(xs)


def _ag_big(name, xs):
    nt = len(xs)

    def body(*refs):
        x_refs, out_refs = refs[:nt], refs[nt:2 * nt]
        send_sems, recv_sems, local_sems = refs[2 * nt:]
        x, y, c = _place()
        me, sibling = (x, y, c), (x, y, 1 - c)
        chips = [(1 - x, y), (x, 1 - y), (1 - x, 1 - y)]

        def copy(t, k, block, to, own=False):
            px, py, pc = block
            rows = out_refs[t].at[4 * px + 2 * py + pc]
            return pltpu.make_async_remote_copy(src_ref=x_refs[t] if own else rows, dst_ref=rows, send_sem=send_sems.at[7 * t + k],
                                                recv_sem=recv_sems.at[7 * t + k], device_id=to, device_id_type=MESH)

        mine = [pltpu.make_async_copy(x_refs[t], out_refs[t].at[4 * x + 2 * y + c], local_sems.at[t]) for t in range(nt)]
        for cp in mine:
            cp.start()
        first = []
        for t in range(nt):
            first.append(copy(t, 0, me, sibling, own=True))
            first += [copy(t, 1 + j, me, (*chip, c), own=True) for j, chip in enumerate(chips)]
        for cp in first:
            cp.start()
        passed = []
        for t in range(nt):
            for j, chip in enumerate(chips):
                copy(t, 1 + j, (*chip, c), me).wait_recv()
                cp = copy(t, 4 + j, (*chip, c), sibling)
                cp.start()
                passed.append(cp)
        for t in range(nt):
            copy(t, 0, sibling, me).wait_recv()
            for j, chip in enumerate(chips):
                copy(t, 4 + j, (*chip, 1 - c), me).wait_recv()
        for cp in first + passed:
            cp.wait_send()
        for cp in mine:
            cp.wait()

    hbm = pl.BlockSpec(memory_space=pl.ANY)
    return pl.pallas_call(
        body, out_shape=[jax.ShapeDtypeStruct((N_DEV,) + t.shape, t.dtype) for t in xs],
        in_specs=[hbm] * nt, out_specs=[hbm] * nt,
        scratch_shapes=[pltpu.SemaphoreType.DMA((7 * nt,)), pltpu.SemaphoreType.DMA((7 * nt,)), pltpu.SemaphoreType.DMA((nt,))],
        name=name)(*xs)


def _rs_pair(name, ps):
    nt = len(ps)

    def body(*refs):
        p_refs, recv_refs = refs[:nt], refs[nt:2 * nt]
        send_sems, recv_sems = refs[2 * nt:]
        x, y, c = _place()
        copies = []
        for t in range(nt):
            for j in range(4):
                cp = pltpu.make_async_remote_copy(src_ref=p_refs[t].at[j, 1 - c], dst_ref=recv_refs[t].at[j], send_sem=send_sems.at[4 * t + j],
                                                  recv_sem=recv_sems.at[4 * t + j], device_id=(x, y, 1 - c), device_id_type=MESH)
                cp.start()
                copies.append(cp)
        for cp in copies:
            cp.wait()

    hbm = pl.BlockSpec(memory_space=pl.ANY)
    return pl.pallas_call(
        body, out_shape=[jax.ShapeDtypeStruct((4,) + p.shape[2:], p.dtype) for p in ps], in_specs=[hbm] * nt, out_specs=[hbm] * nt,
        scratch_shapes=[pltpu.SemaphoreType.DMA((4 * nt,)), pltpu.SemaphoreType.DMA((4 * nt,))], name=name)(*ps)


def _rs_chips(name, ss):
    nt = len(ss)

    def body(*refs):
        s_refs, recv_refs = refs[:nt], refs[nt:2 * nt]
        send_sems, recv_sems, local_sems = refs[2 * nt:]
        x, y, c = _place()
        mine = 2 * x + y
        owns = [pltpu.make_async_copy(s_refs[t].at[mine], recv_refs[t].at[mine], local_sems.at[t]) for t in range(nt)]
        for cp in owns:
            cp.start()
        copies = []
        for t in range(nt):
            for k in range(1, 4):
                px = (1 - x) if k & 2 else x
                py = (1 - y) if k & 1 else y
                cp = pltpu.make_async_remote_copy(src_ref=s_refs[t].at[2 * px + py], dst_ref=recv_refs[t].at[mine],
                                                  send_sem=send_sems.at[3 * t + k - 1], recv_sem=recv_sems.at[3 * t + k - 1],
                                                  device_id=(px, py, c), device_id_type=MESH)
                cp.start()
                copies.append(cp)
        for cp in copies:
            cp.wait()
        for cp in owns:
            cp.wait()

    hbm = pl.BlockSpec(memory_space=pl.ANY)
    return pl.pallas_call(
        body, out_shape=[jax.ShapeDtypeStruct(s_.shape, s_.dtype) for s_ in ss], in_specs=[hbm] * nt, out_specs=[hbm] * nt,
        scratch_shapes=[pltpu.SemaphoreType.DMA((3 * nt,)), pltpu.SemaphoreType.DMA((3 * nt,)), pltpu.SemaphoreType.DMA((nt,))],
        name=name)(*ss)


RS_ROWS = 256


def _row_block(r):
    return next(t for t in range(RS_ROWS, 0, -16) if r % t == 0)


def _pair_sum(name, p, recv, my_c, my_chip):
    _, _, r, c = p.shape
    tr = _row_block(r)

    def body(sc_ref, p_ref, r_ref, o_ref, own_ref):
        s = p_ref[...] + r_ref[...]
        o_ref[...] = s.astype(MM)

        @pl.when(pl.program_id(1) == sc_ref[1])
        def _():
            own_ref[...] = s

    return pl.pallas_call(
        body, grid_spec=pltpu.PrefetchScalarGridSpec(
            num_scalar_prefetch=1, grid=(r // tr, 4),
            in_specs=[pl.BlockSpec((None, None, tr, c), lambda i, j, sc: (j, sc[0], i, 0)),
                      pl.BlockSpec((None, tr, c), lambda i, j, sc: (j, i, 0))],
            out_specs=[pl.BlockSpec((None, tr, c), lambda i, j, sc: (j, i, 0)), pl.BlockSpec((tr, c), lambda i, j, sc: (i, 0))]),
        out_shape=[jax.ShapeDtypeStruct((4, r, c), MM), jax.ShapeDtypeStruct((r, c), F32)], name=name,
        compiler_params=_cparams(("arbitrary", "arbitrary")))(jnp.stack([my_c, my_chip]), p, recv)


def _chip_sum(name, own, recv, my_chip):
    _, r, c = recv.shape
    tr = _row_block(r)

    def body(sc_ref, own_ref, r_ref, o_ref):
        acc = jnp.zeros((tr, c), F32)
        for j in range(4):
            acc = acc + jnp.where(sc_ref[0] == j, own_ref[...], r_ref[j].astype(F32))
        o_ref[...] = acc

    return pl.pallas_call(
        body, grid_spec=pltpu.PrefetchScalarGridSpec(
            num_scalar_prefetch=1, grid=(r // tr,),
            in_specs=[pl.BlockSpec((tr, c), lambda i, sc: (i, 0)), pl.BlockSpec((4, tr, c), lambda i, sc: (0, i, 0))],
            out_specs=pl.BlockSpec((tr, c), lambda i, sc: (i, 0))),
        out_shape=jax.ShapeDtypeStruct((r, c), F32), name=name,
        compiler_params=_cparams(("arbitrary",)))(my_chip.reshape(1), own, recv)


def _reduce_scatter(tag, tensors, my_c, my_chip):
    ps = [t.reshape((4, 2) + t.shape[1:]) for t in tensors]
    recv = _rs_pair(tag + "_pair", ps)
    sums = [_pair_sum(f"{tag}_pair_sum{t}", ps[t], recv[t], my_c, my_chip) for t in range(len(ps))]
    recv2 = _rs_chips(tag + "_chips", [s_[0] for s_ in sums])
    return [_chip_sum(f"{tag}_chip_sum{t}", sums[t][1], recv2[t], my_chip) for t in range(len(ps))]


HBM_SPEC = pl.BlockSpec(memory_space=pltpu.HBM)
SEM_SPEC = pl.BlockSpec(memory_space=pltpu.SEMAPHORE)
SPLIT_EFFECT = pltpu.SideEffectType.DATAFLOW_SIDE_EFFECTING
ALL_PEERS = (1, 2, 3, 4, 5, 6, 7)
FIRST_LEVEL_PEERS = (1, 4, 2, 6)


def _split_copies(src_refs, land_refs, sems, masks, src_per_peer):
    n, nt = len(masks), len(src_refs)
    x, y, c = _place()
    me = 4 * x + 2 * y + c
    copies = []
    for t in range(nt):
        for k, mask in enumerate(masks):
            px, py, pc = (1 - x) if mask & 4 else x, (1 - y) if mask & 2 else y, (1 - c) if mask & 1 else c
            src = src_refs[t].at[4 * px + 2 * py + pc] if src_per_peer else src_refs[t]
            copies.append(pltpu.make_async_remote_copy(src_ref=src, dst_ref=land_refs[t].at[me], send_sem=sems[t * n + k],
                                                       recv_sem=sems[nt * n + t * n + k], device_id=(px, py, pc), device_id_type=MESH))
    return copies


def _copies_start(name, srcs, lands, masks, src_per_peer):
    nt, ns = len(srcs), 2 * len(masks) * len(srcs)

    def body(*refs):
        for cp in _split_copies(refs[:nt], refs[nt:2 * nt], refs[2 * nt:2 * nt + ns], masks, src_per_peer):
            cp.start()
        token = refs[-1]
        token[...] = jnp.zeros(token.shape, F32)

    outs = pl.pallas_call(
        body, name=name,
        out_shape=(pltpu.SemaphoreType.DMA(()),) * ns + tuple(pltpu.HBM(a.shape, a.dtype) for a in list(srcs) + list(lands))
        + (jax.ShapeDtypeStruct((8, 128), F32),),
        in_specs=(HBM_SPEC,) * (2 * nt), out_specs=(SEM_SPEC,) * ns + (HBM_SPEC,) * (2 * nt) + (pl.BlockSpec(memory_space=pltpu.VMEM),),
        input_output_aliases={t: ns + t for t in range(2 * nt)}, compiler_params=pltpu.CompilerParams(has_side_effects=SPLIT_EFFECT))(
            *[pltpu.with_memory_space_constraint(a, pltpu.HBM) for a in list(srcs) + list(lands)])
    return outs[:ns], outs[ns:ns + nt], outs[ns + nt:ns + 2 * nt], outs[-1]


def _copies_wait(name, sems, srcs_thru, lands_thru, after, masks, src_per_peer):
    nt, ns = len(srcs_thru), len(sems)

    def body(*refs):
        for cp in _split_copies(refs[:nt], refs[nt:2 * nt], refs[2 * nt:2 * nt + ns], masks, src_per_peer):
            cp.wait_send()
            cp.wait_recv()

    thru = list(srcs_thru) + list(lands_thru)
    return pl.pallas_call(
        body, name=name, out_shape=tuple(pltpu.HBM(a.shape, a.dtype) for a in thru),
        in_specs=(HBM_SPEC,) * (2 * nt) + (SEM_SPEC,) * ns + (pl.BlockSpec(memory_space=pl.ANY),), out_specs=(HBM_SPEC,) * (2 * nt),
        input_output_aliases={t: t for t in range(2 * nt)}, compiler_params=pltpu.CompilerParams(has_side_effects=SPLIT_EFFECT))(
            *thru, *sems, after)[nt:]


def _ag_forward(name, gs):
    nt = len(gs)

    def body(*refs):
        o_refs, send_sems, recv_sems = refs[nt:2 * nt], refs[2 * nt], refs[2 * nt + 1]
        x, y, c = _place()
        chips = [(1 - x, y), (x, 1 - y), (1 - x, 1 - y)]

        def copy(t, j, pc):
            rows = o_refs[t].at[4 * chips[j][0] + 2 * chips[j][1] + pc]
            return pltpu.make_async_remote_copy(src_ref=rows, dst_ref=rows, send_sem=send_sems.at[3 * t + j], recv_sem=recv_sems.at[3 * t + j],
                                                device_id=(x, y, 1 - c), device_id_type=MESH)

        for t in range(nt):
            for j in range(3):
                copy(t, j, c).start()
        for t in range(nt):
            for j in range(3):
                copy(t, j, c).wait_send()
                copy(t, j, 1 - c).wait_recv()

    hbm = pl.BlockSpec(memory_space=pl.ANY)
    return pl.pallas_call(body, out_shape=[jax.ShapeDtypeStruct(g.shape, g.dtype) for g in gs], in_specs=[hbm] * nt, out_specs=[hbm] * nt,
                          scratch_shapes=[pltpu.SemaphoreType.DMA((3 * nt,)), pltpu.SemaphoreType.DMA((3 * nt,))],
                          input_output_aliases={t: t for t in range(nt)}, name=name)(*gs)


def _mm_tn_wire(name, a, b, me, sqrelu, shard_rows):
    T, M = a.shape
    N = b.shape[1]
    tk = min(2048, T)
    nk = T // tk
    if shard_rows:
        bm, bn = M // N_DEV, N
        a_spec = pl.BlockSpec((tk, 2 * bm), lambda j, k, m: (k, j))
        b_spec = pl.BlockSpec((tk, bn), lambda j, k, m: (k, 0))
        halves = (slice(0, bm), slice(None)), (slice(bm, 2 * bm), slice(None))
        acc_shape = (2 * bm, bn)
    else:
        bm, bn = M, N // N_DEV
        a_spec = pl.BlockSpec((tk, bm), lambda j, k, m: (k, 0))
        b_spec = pl.BlockSpec((tk, 2 * bn), lambda j, k, m: (k, j))
        halves = (slice(None), slice(0, bn)), (slice(None), slice(bn, 2 * bn))
        acc_shape = (bm, 2 * bn)

    def body(me_ref, a_ref, b_ref, wire_ref, own_ref, acc):
        j, k = pl.program_id(0), pl.program_id(1)

        @pl.when(k == 0)
        def _():
            acc[...] = jnp.zeros(acc.shape, F32)

        av = a_ref[...]
        if sqrelu:
            r = jnp.maximum(av, 0.0)
            av = r * r
        acc[...] += _dot_tn(av, b_ref[...])

        for hh in range(2):
            @pl.when(k == nk - 1)
            def _():
                wire_ref[hh] = acc[halves[hh]].astype(MM)

            @pl.when((k == nk - 1) & (2 * j + hh == me_ref[0]))
            def _():
                own_ref[...] = acc[halves[hh]]

    return pl.pallas_call(
        body, grid_spec=pltpu.PrefetchScalarGridSpec(
            num_scalar_prefetch=1, grid=(N_DEV // 2, nk), in_specs=[a_spec, b_spec],
            out_specs=[pl.BlockSpec((2, bm, bn), lambda j, k, m: (j, 0, 0)), pl.BlockSpec((bm, bn), lambda j, k, m: (0, 0))],
            scratch_shapes=[pltpu.VMEM(acc_shape, F32)]),
        out_shape=[jax.ShapeDtypeStruct((N_DEV, bm, bn), MM), jax.ShapeDtypeStruct((bm, bn), F32)], name=name,
        compiler_params=_cparams(("arbitrary", "arbitrary")))(me.reshape(1), a, b)


def _rs_final(name, own, recv, me, stack, li):
    _, r, c = recv.shape
    tr = RS_ROWS

    def body(me_ref, own_ref, r_ref, s_ref, o_ref):
        acc = jnp.zeros((tr, c), F32)
        for j in range(N_DEV):
            acc = acc + jnp.where(me_ref[0] == j, own_ref[...], r_ref[j].astype(F32))
        o_ref[...] = acc

    return pl.pallas_call(
        body, grid_spec=pltpu.PrefetchScalarGridSpec(
            num_scalar_prefetch=1, grid=(r // tr,),
            in_specs=[pl.BlockSpec((tr, c), lambda i, m: (i, 0)), pl.BlockSpec((N_DEV, tr, c), lambda i, m: (0, i, 0)),
                      pl.BlockSpec(memory_space=pl.ANY)],
            out_specs=pl.BlockSpec((None, tr, c), lambda i, m: (li, i, 0))),
        out_shape=jax.ShapeDtypeStruct(stack.shape, F32), input_output_aliases={3: 0}, name=name,
        compiler_params=_cparams(("arbitrary",)))(me.reshape(1), own, recv, stack)


def _rs_finish(pending, after, me, stacks):
    i, sems, wires_thru, lands, owns = pending
    recvs = _copies_wait(f"rs_wait{i}", sems, wires_thru, lands, after, ALL_PEERS, True)
    return [_rs_final(f"rs_final{i}_{t}", owns[t], recvs[t], me, stacks[t], i) for t in range(len(owns))]


def _mod_part(c16, ada_w, ada_b_cols):
    L, D, n = ada_w.shape

    def body(c_ref, w_ref, b_ref, o_ref):
        cv = c_ref[...]
        o_ref[...] = _dot(cv * _sigmoid(cv), w_ref[...]) + b_ref[...]

    return pl.pallas_call(
        body, grid=(L,), in_specs=[pl.BlockSpec((16, D), lambda i: (0, 0)), pl.BlockSpec((None, D, n), lambda i: (i, 0, 0)),
                                   pl.BlockSpec((None, 1, n), lambda i: (i, 0, 0))],
        out_specs=pl.BlockSpec((None, 16, n), lambda i: (i, 0, 0)), out_shape=jax.ShapeDtypeStruct((L, 16, n), F32),
        name="ada_mod", compiler_params=_cparams(("arbitrary",)))(c16, ada_w, ada_b_cols)


def _ada_w_grad(c16, dmod16):
    L, _, n = dmod16.shape
    D = c16.shape[1]

    def body(c_ref, d_ref, o_ref):
        cv = c_ref[...]
        o_ref[...] = _dot_tn(cv * _sigmoid(cv), d_ref[...])

    return pl.pallas_call(
        body, grid=(L,), in_specs=[pl.BlockSpec((16, D), lambda i: (0, 0)), pl.BlockSpec((None, 16, n), lambda i: (i, 0, 0))],
        out_specs=pl.BlockSpec((None, D, n), lambda i: (i, 0, 0)), out_shape=jax.ShapeDtypeStruct((L, D, n), F32),
        name="ada_w_grad", compiler_params=_cparams(("arbitrary",)))(c16, dmod16)


def _sum_devices(name, g):
    _, R, C = g.shape

    def body(g_ref, o_ref):
        acc = g_ref[0]
        for d in range(1, N_DEV):
            acc = acc + g_ref[d]
        o_ref[...] = acc

    return pl.pallas_call(body, out_shape=jax.ShapeDtypeStruct((R, C), F32), name=name)(g)


def _prenorm(name, x, g0, sc, sh, dtype):
    T, D = x.shape

    def body(i, n, rr, cc, oo, aa, ss):
        oo[0][...] = (_rms(rr[0][...], cc[0][...]) * (1.0 + cc[1][...]) + cc[2][...]).astype(dtype)

    return _rows(name, body, T, 512, [(x, 'cur')], [g0, sc, sh], [(D, dtype)])[0]


def _post_bwd(name, dxo, y, g1, gt):
    T, D = y.shape

    def body(i, n, rr, cc, oo, aa, ss):
        d = rr[0][...]
        yv = rr[1][...]
        g1v, gtv = cc[0][...], cc[1][...]
        aa[1][...] += jnp.sum(d * _rms(yv, g1v), axis=0, keepdims=True)
        dy, dg1 = _rms_bwd(yv, g1v, d * gtv)
        aa[0][...] += dg1
        aa[2][...] += jnp.sum(dy, axis=0, keepdims=True)
        oo[0][...] = dy.astype(MM)

    return _rows(name, body, T, 512, [(dxo, 'cur'), (y, 'cur')], [g1, gt], [(D, MM)], accs=[(1, D)] * 3)


def _mm_post(name, a, w, bias, x, g1, gt):
    T, D = x.shape
    consts = [w, g1, gt] + ([bias] if bias is not None else [])

    def body(i, n, rr, cc, oo, aa, ss):
        y = _dot(rr[0][...], cc[0][...])
        if bias is not None:
            y = y + cc[3][...]
        oo[0][...] = y
        oo[1][...] = rr[1][...] + cc[2][...] * _rms(y, cc[1][...])

    return _rows(name, body, T, 512, [(a, 'cur'), (x, 'cur')], consts, [(D, F32), (D, F32)])


def _mm_nt_rows(name, a, w):
    T = a.shape[0]
    K = w.shape[0]

    def body(i, n, rr, cc, oo, aa, ss):
        oo[0][...] = _dot_nt(rr[0][...], cc[0][...]).astype(MM)

    return _rows(name, body, T, 512, [(a, 'cur')], [w], [(K, MM)])[0]


def _mm_tn(name, a, b, sqrelu=False, col_shards=0, diag=0):
    T, M = a.shape
    N = b.shape[1]
    tk = min(512, T)
    nk = T // tk
    if diag:
        bm, bn = M // diag, N // diag
        grid = (diag, 1, nk)
        a_spec = pl.BlockSpec((tk, bm), lambda g, n, k: (k, g))
        b_spec = pl.BlockSpec((tk, bn), lambda g, n, k: (k, g))
        o_spec = pl.BlockSpec((None, bm, bn), lambda g, n, k: (g, 0, 0))
        o_shape = (diag, bm, bn)
    else:
        bm = min(M, 1024)
        bn = N // col_shards if col_shards else min(N, 1024)
        grid = (M // bm, N // bn, nk)
        a_spec = pl.BlockSpec((tk, bm), lambda m, n, k: (k, m))
        b_spec = pl.BlockSpec((tk, bn), lambda m, n, k: (k, n))
        if col_shards:
            o_spec = pl.BlockSpec((None, bm, bn), lambda m, n, k: (n, m, 0))
            o_shape = (col_shards, M, bn)
        else:
            o_spec = pl.BlockSpec((bm, bn), lambda m, n, k: (m, n))
            o_shape = (M, N)

    def body(a_ref, b_ref, o_ref):
        @pl.when(pl.program_id(2) == 0)
        def _():
            o_ref[...] = jnp.zeros(o_ref.shape, F32)

        av = a_ref[...]
        if sqrelu:
            r = jnp.maximum(av, 0.0)
            av = r * r
        o_ref[...] += _dot_tn(av, b_ref[...])

    return pl.pallas_call(body, grid=grid, in_specs=[a_spec, b_spec], out_specs=o_spec,
                          out_shape=jax.ShapeDtypeStruct(o_shape, F32), name=name,
                          compiler_params=_cparams(("arbitrary", "arbitrary", "arbitrary")))(a, b)


FFN_SHARDS = 2

def _ffn_fwd(name, li, h, w1g, w2g, x, g1, gt):
    T, D = h.shape
    nf, tf = w1g.shape[0], w1g.shape[-1]
    F = nf * tf
    tm = min(512, T)

    def body(h_ref, w1_ref, w2_ref, x_ref, g1_ref, gt_ref, a_ref, y_ref, xo_ref, acc):
        f = pl.program_id(1)

        @pl.when(f == 0)
        def _():
            acc[...] = jnp.zeros(acc.shape, F32)

        hv = h_ref[...]
        part = None
        for hh in range(FFN_SHARDS):
            a = _dot(hv, w1_ref[hh])
            a_ref[:, hh * tf:(hh + 1) * tf] = a.astype(MM)
            r = jnp.maximum(a, 0.0)
            p = _dot(r * r, w2_ref[hh])
            part = p if part is None else part + p
        acc[...] += part

        @pl.when(f == nf // FFN_SHARDS - 1)
        def _():
            y = acc[...]
            y_ref[...] = y
            xo_ref[...] = x_ref[...] + gt_ref[...] * _rms(y, g1_ref[...])

    row = lambda t, f: (t, 0)
    one = lambda t, f: (0, 0)
    return pl.pallas_call(
        body, grid=(T // tm, nf // FFN_SHARDS),
        in_specs=[pl.BlockSpec((tm, D), row), pl.BlockSpec((FFN_SHARDS, None, D, tf), lambda t, f: (f, li, 0, 0)),
                  pl.BlockSpec((FFN_SHARDS, None, tf, D), lambda t, f: (f, li, 0, 0)),
                  pl.BlockSpec((tm, D), row), pl.BlockSpec((1, D), one), pl.BlockSpec((1, D), one)],
        out_specs=[pl.BlockSpec((tm, FFN_SHARDS * tf), lambda t, f: (t, f)), pl.BlockSpec((tm, D), row), pl.BlockSpec((tm, D), row)],
        out_shape=[jax.ShapeDtypeStruct((T, F), MM), jax.ShapeDtypeStruct((T, D), F32), jax.ShapeDtypeStruct((T, D), F32)],
        scratch_shapes=[pltpu.VMEM((tm, D), F32)], name=name,
        compiler_params=_cparams(("arbitrary", "arbitrary")))(h, w1g, w2g, x, g1, gt)


def _ffn_bwd(name, li, dy, a, w1g, w2g, x, dxo, g0, sc):
    T, D = x.shape
    nf, tf = w1g.shape[0], w1g.shape[-1]
    F = nf * tf
    tm = min(512, T)

    def body(dy_ref, a_ref, w1_ref, w2_ref, x_ref, dxo_ref, g0_ref, sc_ref, da_ref, dx_ref, dsh_ref, dsc_ref, dg0_ref, acc):
        t, f = pl.program_id(0), pl.program_id(1)

        @pl.when((t == 0) & (f == 0))
        def _():
            for r in (dsh_ref, dsc_ref, dg0_ref):
                r[...] = jnp.zeros(r.shape, F32)

        @pl.when(f == 0)
        def _():
            acc[...] = jnp.zeros(acc.shape, F32)

        dyv = dy_ref[...]
        dyv = dyv + dyv
        part = None
        for hh in range(FFN_SHARDS):
            cols = slice(hh * tf, (hh + 1) * tf)
            du = _dot_nt(dyv, w2_ref[hh])
            da = (du * jnp.maximum(a_ref[:, cols], 0.0).astype(F32)).astype(MM)
            da_ref[:, cols] = da
            p = _dot_nt(da, w1_ref[hh])
            part = p if part is None else part + p
        acc[...] += part

        @pl.when(f == nf // FFN_SHARDS - 1)
        def _():
            dx, dsh, dsc, dg0 = _prenorm_bwd(x_ref[...], g0_ref[...], sc_ref[...], acc[...])
            dx_ref[...] = dxo_ref[...] + dx
            dsh_ref[...] += dsh
            dsc_ref[...] += dsc
            dg0_ref[...] += dg0

    row = lambda t, f: (t, 0)
    one = lambda t, f: (0, 0)
    blk = lambda t, f: (t, f)
    return pl.pallas_call(
        body, grid=(T // tm, nf // FFN_SHARDS),
        in_specs=[pl.BlockSpec((tm, D), row), pl.BlockSpec((tm, FFN_SHARDS * tf), blk),
                  pl.BlockSpec((FFN_SHARDS, None, D, tf), lambda t, f: (f, li, 0, 0)),
                  pl.BlockSpec((FFN_SHARDS, None, tf, D), lambda t, f: (f, li, 0, 0)), pl.BlockSpec((tm, D), row), pl.BlockSpec((tm, D), row),
                  pl.BlockSpec((1, D), one), pl.BlockSpec((1, D), one)],
        out_specs=[pl.BlockSpec((tm, FFN_SHARDS * tf), blk), pl.BlockSpec((tm, D), row)] + [pl.BlockSpec((1, D), one)] * 3,
        out_shape=[jax.ShapeDtypeStruct((T, F), MM), jax.ShapeDtypeStruct((T, D), F32)] + [jax.ShapeDtypeStruct((1, D), F32)] * 3,
        scratch_shapes=[pltpu.VMEM((tm, D), F32)], name=name,
        compiler_params=_cparams(("arbitrary", "arbitrary")))(dy, a, w1g, w2g, x, dxo, g0, sc)


def _rope_tables(pos, invf):
    T = pos.shape[0]

    def body(i, n, rr, cc, oo, aa, ss):
        ang = rr[0][...] * cc[0][...]
        lane = lax.broadcasted_iota(jnp.int32, ang.shape, 1)
        cs, sn = jnp.cos(ang), jnp.sin(ang)
        oo[0][...] = jnp.where((lane >= QK_NOPE) & (lane < QK_NOPE + QK_ROPE), cs, 1.0)
        oo[1][...] = jnp.where((lane >= QK_NOPE) & (lane < QK_NOPE + QK_ROPE // 2), -sn, 0.0)
        oo[2][...] = jnp.where((lane >= QK_NOPE + QK_ROPE // 2) & (lane < QK_NOPE + QK_ROPE), sn, 0.0)

    return _rows("rope_tables", body, T, 512, [(pos, 'cur')], [invf], [(HEAD_PAD, F32)] * 3)


def _rope(v, C, S1, S2):
    n = v.shape[1]
    reps = n // HEAD_PAD
    if reps > 1:
        C, S1, S2 = (jnp.tile(t, (1, reps)) for t in (C, S1, S2))
    return v * C + pltpu.roll(v, n - QK_ROPE // 2, 1) * S1 + pltpu.roll(v, QK_ROPE // 2, 1) * S2


def _unrope(d, C, S1, S2):
    n = d.shape[1]
    reps = n // HEAD_PAD
    if reps > 1:
        C, S1, S2 = (jnp.tile(t, (1, reps)) for t in (C, S1, S2))
    return d * C + pltpu.roll(d * S1, QK_ROPE // 2, 1) + pltpu.roll(d * S2, n - QK_ROPE // 2, 1)


def _mla_proj(name, h, C, S1, S2, w_dq, qg, w_uq, w_dkv, kvg, w_ukv_k, w_ukv_v):
    T = h.shape[0]
    HP = N_HEADS * HEAD_PAD

    def body(i, n, rr, cc, oo, aa, ss):
        hv = rr[0][...]
        Cv, S1v, S2v = rr[1][...], rr[2][...], rr[3][...]
        cq_raw = _dot(hv, cc[0][...])
        cq = _rms(cq_raw, cc[1][...]).astype(MM)
        q = _rope(_dot(cq, cc[2][...]), Cv, S1v, S2v)
        ckv_all = _dot(hv, cc[3][...])
        ckv_raw = ckv_all[:, :KV_LORA]
        ckv = _rms(ckv_raw, cc[4][...]).astype(MM)
        kr = _rope(ckv_all[:, KV_LORA:], Cv, S1v, S2v)
        k = _dot(ckv, cc[5][...]) + jnp.tile(kr, (1, N_HEADS))
        v = _dot(ckv, cc[6][...])
        v = jnp.where(lax.broadcasted_iota(jnp.int32, v.shape, 1) % HEAD_PAD == V_HEAD, 1.0, v)
        oo[0][...] = cq_raw
        oo[1][...] = cq
        oo[2][...] = ckv_raw
        oo[3][...] = ckv
        oo[4][...] = q.astype(MM)
        oo[5][...] = k.astype(MM)
        oo[6][...] = v.astype(MM)

    return _rows(name, body, T, 256, [(h, 'cur'), (C, 'cur'), (S1, 'cur'), (S2, 'cur')],
                 [w_dq, qg, w_uq, w_dkv, kvg, w_ukv_k, w_ukv_v],
                 [(Q_LORA, F32), (Q_LORA, MM), (KV_LORA, F32), (KV_LORA, MM), (HP, MM), (HP, MM), (HP, MM)])


ATT_HEADS = 4
ATT_BLOCK = 512
ATT_FWD_BLOCK = 1024


def _chunk_mask_t(tk, tq):
    ki = lax.broadcasted_iota(jnp.int32, (tk, tq), 0) // CHUNK
    qi = lax.broadcasted_iota(jnp.int32, (tk, tq), 1) // CHUNK
    return ki <= qi


def _attn_fwd(name, q, k, v):
    T = q.shape[0]
    tb = min(ATT_FWD_BLOCK, T)
    nb = T // tb
    nh = ATT_HEADS
    hs = [slice(h * HEAD_PAD, (h + 1) * HEAD_PAD) for h in range(nh)]

    def body(q_ref, k_ref, v_ref, o_ref, lse_ref):
        qb = pl.program_id(1)

        def k_block(k0, masked, st):
            new = []
            for h in range(nh):
                m, acc = st[h]
                s = _dot_nt(k_ref[pl.ds(k0, tb), hs[h]], q_ref[:, hs[h]])
                if masked:
                    s = jnp.where(_chunk_mask_t(tb, tb), s, NEG)
                m_new = jnp.maximum(m, jnp.max(s, axis=0, keepdims=True))
                alpha = jnp.exp((m - m_new) * ATT_SCALE)
                p = jnp.exp((s - m_new) * ATT_SCALE)
                acc = alpha * acc + _dot_tn(v_ref[pl.ds(k0, tb), hs[h]], p)
                new.append((m_new, acc))
            return tuple(new)

        st = tuple((jnp.full((1, tb), NEG, F32), jnp.zeros((HEAD_PAD, tb), F32)) for _ in range(nh))
        st = k_block(pl.multiple_of(qb * tb, tb), True, st)
        st = lax.fori_loop(0, qb, lambda kb, s_: k_block(pl.multiple_of(kb * tb, tb), False, s_), st)
        for h in range(nh):
            m, acc = st[h]
            l = acc[V_HEAD:V_HEAD + 1, :]
            o_ref[:, hs[h]] = (acc / l).T.astype(MM)
            lse_ref[h] = jnp.broadcast_to(m * ATT_SCALE + jnp.log(l), (8, tb))

    blk = pl.BlockSpec((tb, nh * HEAD_PAD), lambda g, i: (i, g))
    res = pl.BlockSpec((T, nh * HEAD_PAD), lambda g, i: (0, g))
    return pl.pallas_call(
        body, grid=(N_HEADS // nh, nb), in_specs=[blk, res, res],
        out_specs=[blk, pl.BlockSpec((nh, 8, tb), lambda g, i: (g, 0, i))],
        out_shape=[jax.ShapeDtypeStruct(q.shape, MM), jax.ShapeDtypeStruct((N_HEADS, 8, T), F32)], name=name,
        compiler_params=_cparams(("arbitrary", "arbitrary")))(q, k, v)


def _attn_delta(name, do, o):
    T = do.shape[0]
    tb = min(256, T)

    def body(do_ref, o_ref, d_ref):
        lane = lax.broadcasted_iota(jnp.int32, (tb, HEAD_PAD), 1) // 8
        cols = jnp.zeros((tb, HEAD_PAD), F32)
        for h in range(N_HEADS):
            hsl = slice(h * HEAD_PAD, (h + 1) * HEAD_PAD)
            r = jnp.sum(do_ref[:, hsl].astype(F32) * o_ref[:, hsl].astype(F32), axis=1, keepdims=True)
            cols = jnp.where(lane == h, r, cols)
        d_ref[...] = cols.T

    spec = pl.BlockSpec((tb, N_HEADS * HEAD_PAD), lambda i: (i, 0))
    out = pl.pallas_call(body, grid=(T // tb,), in_specs=[spec, spec], out_specs=pl.BlockSpec((HEAD_PAD, tb), lambda i: (0, i)),
                         out_shape=jax.ShapeDtypeStruct((HEAD_PAD, T), F32), name=name, compiler_params=_cparams(("arbitrary",)))(do, o)
    return out.reshape(N_HEADS, 8, T)


def _attn_bwd(name, q, k, v, do, lse, delta):
    T = q.shape[0]
    tb = min(ATT_BLOCK, T)
    nb = T // tb
    nh = ATT_HEADS
    hs = [slice(h * HEAD_PAD, (h + 1) * HEAD_PAD) for h in range(nh)]

    def body(q_ref, k_ref, v_ref, do_ref, lse_ref, dl_ref, dq_ref, dk_ref, dv_ref, dq_acc, dk_acc, dv_acc):
        kb = pl.program_id(1)

        @pl.when(kb == 0)
        def _():
            dq_acc[...] = jnp.zeros(dq_acc.shape, F32)

        dk_acc[...] = jnp.zeros(dk_acc.shape, F32)
        dv_acc[...] = jnp.zeros(dv_acc.shape, F32)

        def q_block(q0, masked):
            for h in range(nh):
                qh = q_ref[pl.ds(q0, tb), hs[h]]
                doh = do_ref[pl.ds(q0, tb), hs[h]]
                kh = k_ref[:, hs[h]]
                s = _dot_nt(kh, qh) * ATT_SCALE
                if masked:
                    s = jnp.where(_chunk_mask_t(tb, tb), s, NEG)
                p = jnp.exp(s - lse_ref[h, 0:1, pl.ds(q0, tb)])
                ds = (p * (_dot_nt(v_ref[:, hs[h]], doh) - dl_ref[h, 0:1, pl.ds(q0, tb)]) * ATT_SCALE).astype(MM)
                dv_acc[:, hs[h]] += _dot(p, doh)
                dk_acc[:, hs[h]] += _dot(ds, qh)
                dq_acc[pl.ds(q0, tb), hs[h]] += _dot_tn(ds, kh)

        q_block(pl.multiple_of(kb * tb, tb), True)

        def rest(qb, c_):
            q_block(pl.multiple_of(qb * tb, tb), False)
            return c_

        lax.fori_loop(kb + 1, nb, rest, 0)
        dk_ref[...] = dk_acc[...].astype(MM)
        dv_ref[...] = dv_acc[...].astype(MM)

        @pl.when(kb == nb - 1)
        def _():
            dq_ref[...] = dq_acc[...].astype(MM)

    W = nh * HEAD_PAD
    blk = pl.BlockSpec((tb, W), lambda g, i: (i, g))
    res = pl.BlockSpec((T, W), lambda g, i: (0, g))
    rows = pl.BlockSpec((nh, 8, T), lambda g, i: (g, 0, 0))
    return pl.pallas_call(
        body, grid=(N_HEADS // nh, nb), in_specs=[res, blk, blk, res, rows, rows], out_specs=[res, blk, blk],
        out_shape=[jax.ShapeDtypeStruct(q.shape, MM)] * 3,
        scratch_shapes=[pltpu.VMEM((T, W), F32), pltpu.VMEM((tb, W), F32), pltpu.VMEM((tb, W), F32)],
        name=name, compiler_params=_cparams(("arbitrary", "arbitrary")))(q, k, v, do, lse, delta)


def _mla_proj_bwd(name, dq, dk, dv, C, S1, S2, cq_raw, ckv_raw, x, dxo, w_uq, w_ukv_k, w_ukv_v, w_dq, w_dkv, qg, kvg, g0, sc):
    T, D = x.shape
    HP = N_HEADS * HEAD_PAD

    def body(i, n, rr, cc, oo, aa, ss):
        Cv, S1v, S2v = rr[3][...], rr[4][...], rr[5][...]
        dq_pre = _unrope(rr[0][...].astype(F32), Cv, S1v, S2v).astype(MM)
        oo[0][...] = dq_pre
        dcq = _dot_nt(dq_pre, cc[0][...])
        dcq_raw, dqg = _rms_bwd(rr[6][...], cc[5][...], dcq)
        aa[0][...] += dqg
        dcq_raw = dcq_raw.astype(MM)
        oo[1][...] = dcq_raw
        dkv = rr[1][...]
        dkr = dkv[:, :HEAD_PAD].astype(F32)
        for hh in range(1, N_HEADS):
            dkr = dkr + dkv[:, hh * HEAD_PAD:(hh + 1) * HEAD_PAD].astype(F32)
        lane = lax.broadcasted_iota(jnp.int32, dkr.shape, 1)
        dkr = jnp.where((lane >= QK_NOPE) & (lane < QK_NOPE + QK_ROPE), _unrope(dkr, Cv, S1v, S2v), 0.0)
        dckv = _dot_nt(dkv, cc[1][...]) + _dot_nt(rr[2][...], cc[2][...])
        dckv_raw, dkvg = _rms_bwd(rr[7][...], cc[6][...], dckv)
        aa[1][...] += dkvg
        dckv_all = jnp.concatenate([dckv_raw, dkr], axis=1).astype(MM)
        oo[2][...] = dckv_all
        dh = _dot_nt(dcq_raw, cc[3][...]) + _dot_nt(dckv_all, cc[4][...])
        dx, dsh, dsc, dg0 = _prenorm_bwd(rr[8][...], cc[7][...], cc[8][...], dh)
        oo[3][...] = rr[9][...] + dx
        aa[2][...] += dsh
        aa[3][...] += dsc
        aa[4][...] += dg0

    return _rows(name, body, T, 256,
                 [(dq, 'cur'), (dk, 'cur'), (dv, 'cur'), (C, 'cur'), (S1, 'cur'), (S2, 'cur'), (cq_raw, 'cur'), (ckv_raw, 'cur'),
                  (x, 'cur'), (dxo, 'cur')],
                 [w_uq, w_ukv_k, w_ukv_v, w_dq, w_dkv, qg, kvg, g0, sc],
                 [(HP, MM), (Q_LORA, MM), (KV_LORA + HEAD_PAD, MM), (D, F32)],
                 accs=[(1, Q_LORA), (1, KV_LORA), (1, D), (1, D), (1, D)])


HALO = 32


def _windows(ext, tm, first):
    rolled = {0: ext}
    out = []
    for j in range(CONV_W):
        r = (first + j) % 8
        if r not in rolled:
            rolled[r] = pltpu.roll(ext, ext.shape[0] - r, 0)
        out.append(rolled[r][first + j - r:first + j - r + tm])
    return out


def _conv_glu(name, h, w_pw1, b_pw1):
    T, D = h.shape

    def body(i, n, rr, cc, oo, aa, ss):
        a = _dot(rr[0][...], cc[0][...]) + cc[1][...]
        oo[0][...] = a
        oo[1][...] = a[:, :D] * _sigmoid(a[:, D:])

    return _rows(name, body, T, 512, [(h, 'cur')], [w_pw1, b_pw1], [(2 * D, F32), (D, F32)])


def _layernorm_parts(uc):
    xc = uc - jnp.mean(uc, axis=-1, keepdims=True)
    r = lax.rsqrt(jnp.mean(xc * xc, axis=-1, keepdims=True) + EPS)
    return xc * r, r


def _conv_dw(name, u, w_dw, b_dw, ln_g, ln_b, w_pw2, b_pw2, x, g1, gt):
    T, D = u.shape
    tm = min(256, T)

    def body(i, n, rr, cc, oo, aa, ss):
        ext = jnp.concatenate([jnp.where(i > 0, rr[1][tm - HALO:tm, :], 0.0), rr[0][...]], axis=0)
        uc = jnp.zeros((tm, D), F32) + cc[1][...]
        for kk, win in enumerate(_windows(ext, tm, HALO - (CONV_W - 1))):
            uc = uc + win * cc[0][kk:kk + 1, :]
        xh, _ = _layernorm_parts(uc)
        ln = xh * cc[2][...] + cc[3][...]
        z = (ln * _sigmoid(ln)).astype(MM)
        y = _dot(z, cc[4][...]) + cc[5][...]
        oo[0][...] = uc
        oo[1][...] = z
        oo[2][...] = y
        oo[3][...] = rr[2][...] + cc[7][...] * _rms(y, cc[6][...])

    return _rows(name, body, T, tm, [(u, 'cur'), (u, 'prev'), (x, 'cur')], [w_dw, b_dw, ln_g, ln_b, w_pw2, b_pw2, g1, gt],
                 [(D, F32), (D, MM), (D, F32), (D, F32)])


def _conv_bwd1(name, dy, uc, w_pw2, ln_g, ln_b):
    T, D = uc.shape

    def body(i, n, rr, cc, oo, aa, ss):
        dz = _dot_nt(rr[0][...], cc[0][...])
        xh, r = _layernorm_parts(rr[1][...])
        g = cc[1][...]
        ln = xh * g + cc[2][...]
        sg = _sigmoid(ln)
        dln = dz * (sg * (1.0 + ln * (1.0 - sg)))
        aa[0][...] += jnp.sum(dln * xh, axis=0, keepdims=True)
        aa[1][...] += jnp.sum(dln, axis=0, keepdims=True)
        dxh = dln * g
        duc = r * (dxh - jnp.mean(dxh, axis=-1, keepdims=True) - xh * jnp.mean(dxh * xh, axis=-1, keepdims=True))
        aa[2][...] += jnp.sum(duc, axis=0, keepdims=True)
        oo[0][...] = duc

    return _rows(name, body, T, 256, [(dy, 'cur'), (uc, 'cur')], [w_pw2, ln_g, ln_b], [(D, F32)], accs=[(1, D)] * 3)


def _conv_bwd2(name, duc, u, a, x, dxo, w_dw, w_pw1, g0, sc):
    T, D = u.shape
    tm = min(256, T)

    def body(i, n, rr, cc, oo, aa, ss):
        dcur = rr[0][...]
        extd = jnp.concatenate([dcur, jnp.where(i < n - 1, rr[1][0:HALO, :], 0.0)], axis=0)
        extu = jnp.concatenate([jnp.where(i > 0, rr[3][tm - HALO:tm, :], 0.0), rr[2][...]], axis=0)
        wd = _windows(extd, tm, 0)
        wu = _windows(extu, tm, HALO - (CONV_W - 1))
        du = jnp.zeros((tm, D), F32)
        for kk in range(CONV_W):
            du = du + wd[CONV_W - 1 - kk] * cc[0][kk:kk + 1, :]
            aa[0][kk:kk + 1, :] += jnp.sum(dcur * wu[kk], axis=0, keepdims=True)
        av = rr[4][...]
        a1, sg = av[:, :D], _sigmoid(av[:, D:])
        da = jnp.concatenate([du * sg, du * a1 * (sg * (1.0 - sg))], axis=1)
        aa[1][...] += jnp.sum(da, axis=0, keepdims=True)
        da = da.astype(MM)
        oo[0][...] = da
        dx, dsh, dsc, dg0 = _prenorm_bwd(rr[5][...], cc[2][...], cc[3][...], _dot_nt(da, cc[1][...]))
        oo[1][...] = rr[6][...] + dx
        aa[2][...] += dsh
        aa[3][...] += dsc
        aa[4][...] += dg0

    return _rows(name, body, T, tm,
                 [(duc, 'cur'), (duc, 'next'), (u, 'cur'), (u, 'prev'), (a, 'cur'), (x, 'cur'), (dxo, 'cur')],
                 [w_dw, w_pw1, g0, sc], [(2 * D, MM), (D, F32)],
                 accs=[(32, D), (1, 2 * D), (1, D), (1, D), (1, D)])


PHALO = 16


def _pool_fwd(name, h, w, b, scale, x, g1, gt):
    T, D = h.shape
    G = len(POOL_WINDOWS)
    Cg = D // G
    tm = min(256, T)

    def body(i, n, rr, cc, oo, aa, ss):
        ext = ss[0]
        ext[0:PHALO, :] = jnp.where(i > 0, rr[1][tm - PHALO:tm, :], 0.0)
        ext[PHALO:PHALO + tm, :] = rr[0][...]
        t_glob = i * tm + lax.broadcasted_iota(jnp.int32, (tm, 1), 0)
        ps, ys = [], []
        for g, win in enumerate(POOL_WINDOWS):
            cols = slice(g * Cg, (g + 1) * Cg)
            s = ext[pl.ds(PHALO, tm), cols]
            for j in range(1, win):
                s = s + ext[pl.ds(PHALO - j, tm), cols]
            cnt = jnp.minimum(t_glob + 1, win).astype(F32)
            p = (s / cnt - ext[pl.ds(PHALO, tm), cols]).astype(MM)
            ps.append(p)
            ys.append(_dot(p, cc[0][g]) + cc[1][:, cols])
        ypre = jnp.concatenate(ys, axis=1)
        y = ypre * cc[2][...]
        oo[0][...] = jnp.concatenate(ps, axis=1)
        oo[1][...] = ypre
        oo[2][...] = y
        oo[3][...] = rr[2][...] + cc[4][...] * _rms(y, cc[3][...])

    return _rows(name, body, T, tm, [(h, 'cur'), (h, 'prev'), (x, 'cur')], [w, b, scale, g1, gt],
                 [(D, MM), (D, F32), (D, F32), (D, F32)], scratch=[pltpu.VMEM((tm + PHALO, D), F32)])


def _pool_bwd1(name, dy, ypre, scale, w):
    T, D = ypre.shape
    G = len(POOL_WINDOWS)
    Cg = D // G

    def body(i, n, rr, cc, oo, aa, ss):
        dyv = rr[0][...].astype(F32)
        aa[0][...] += jnp.sum(dyv * rr[1][...], axis=0, keepdims=True)
        dypre = dyv * cc[0][...]
        aa[1][...] += jnp.sum(dypre, axis=0, keepdims=True)
        dypre = dypre.astype(MM)
        oo[1][...] = dypre
        oo[0][...] = jnp.concatenate([_dot_nt(dypre[:, g * Cg:(g + 1) * Cg], cc[1][g]) for g in range(G)], axis=1)

    return _rows(name, body, T, 256, [(dy, 'cur'), (ypre, 'cur')], [scale, w], [(D, F32), (D, MM)], accs=[(1, D)] * 2)


def _pool_bwd2(name, dp, x, dxo, g0, sc):
    T, D = x.shape
    G = len(POOL_WINDOWS)
    Cg = D // G
    tm = min(256, T)

    def body(i, n, rr, cc, oo, aa, ss):
        ext = ss[0]
        t_glob = i * tm + lax.broadcasted_iota(jnp.int32, (tm, 1), 0)
        dcur = rr[0][...]
        dhs = []
        for g, win in enumerate(POOL_WINDOWS):
            cols = slice(g * Cg, (g + 1) * Cg)
            cnt = jnp.minimum(t_glob + 1, win).astype(F32)
            ext[0:tm, cols] = dcur[:, cols] / cnt
            ext[tm:tm + PHALO, cols] = jnp.where(i < n - 1, rr[1][0:PHALO, cols] * (1.0 / win), 0.0)
        for g, win in enumerate(POOL_WINDOWS):
            cols = slice(g * Cg, (g + 1) * Cg)
            s = ext[pl.ds(0, tm), cols]
            for j in range(1, win):
                s = s + ext[pl.ds(j, tm), cols]
            dhs.append(s - dcur[:, cols])
        dx, dsh, dsc, dg0 = _prenorm_bwd(rr[2][...], cc[0][...], cc[1][...], jnp.concatenate(dhs, axis=1))
        oo[0][...] = rr[3][...] + dx
        aa[0][...] += dsh
        aa[1][...] += dsc
        aa[2][...] += dg0

    return _rows(name, body, T, tm, [(dp, 'cur'), (dp, 'next'), (x, 'cur'), (dxo, 'cur')], [g0, sc], [(D, F32)],
                 accs=[(1, D)] * 3, scratch=[pltpu.VMEM((tm + PHALO, D), F32)])


def _loss_head(x, tgt):
    T, D = x.shape

    def body(i, n, rr, cc, oo, aa, ss):
        err = rr[0][...] - rr[1][...]
        oo[0][...] = err * (1.0 / D)
        aa[0][...] += jnp.sum(err * err, axis=0, keepdims=True)

        @pl.when(i == n - 1)
        def _():
            aa[1][...] = jnp.broadcast_to(jnp.sum(aa[0][...], axis=1, keepdims=True) * (0.5 / D), (1, 128))

    dx, _, loss_row = _rows("loss_head", body, T, 512, [(x, 'cur'), (tgt, 'cur')], [], [(D, F32)], accs=[(1, D), (1, 128)])
    return dx, loss_row


def _adamw(name, w, g, m, v):
    shape = w.shape
    C = shape[-1]
    R = w.size // C
    w2, g2, m2, v2 = (t.reshape(R, C) for t in (w, g, m, v))
    br = R
    if R * C * 4 > (1 << 20):
        br = 8
        while br * 2 * C * 4 <= (1 << 20) and R % (br * 2) == 0:
            br *= 2
    b1c = 1.0 - ADAM_B1 ** ADAM_STEP
    b2c = 1.0 - ADAM_B2 ** ADAM_STEP

    def body(w_ref, g_ref, m_ref, v_ref, d_ref, mo_ref, vo_ref):
        gv = g_ref[...]
        mn = ADAM_B1 * m_ref[...] + (1.0 - ADAM_B1) * gv
        vn = ADAM_B2 * v_ref[...] + (1.0 - ADAM_B2) * (gv * gv)
        d_ref[...] = -ADAM_LR * ((mn / b1c) / (jnp.sqrt(vn / b2c) + ADAM_EPS) + ADAM_WD * w_ref[...])
        mo_ref[...] = mn
        vo_ref[...] = vn

    spec = pl.BlockSpec((br, C), lambda r: (r, 0))
    outs = pl.pallas_call(body, grid=(R // br,), in_specs=[spec] * 4, out_specs=[spec] * 3,
                          out_shape=[jax.ShapeDtypeStruct((R, C), F32)] * 3, name=name,
                          compiler_params=_cparams(("arbitrary",)))(w2, g2, m2, v2)
    return tuple(t.reshape(shape) for t in outs)


def _layer_shards(g, ax):
    s = g.shape
    r = g.reshape(s[:ax] + (N_DEV, s[ax] // N_DEV) + s[ax + 1:])
    return (jnp.moveaxis(r, ax, 0) if ax else r).reshape(N_DEV, -1)


def _unshard(g, ax):
    r = jnp.moveaxis(g, 0, ax)
    s = r.shape
    return r.reshape(s[:ax] + (s[ax] * s[ax + 1],) + s[ax + 2:])


def _pack(parts, dtype, row_mult):
    lead = parts[0].shape[:-1]
    flat = jnp.concatenate([p.astype(dtype) for p in parts], axis=-1)
    n = flat.shape[-1]
    per = row_mult * 1024
    tot = -(-n // per) * per
    flat = jnp.pad(flat, [(0, 0)] * len(lead) + [(0, tot - n)])
    return flat.reshape(lead + (tot // 1024, 1024))


def _pad_heads(w, lo, hi):
    K = w.shape[0]
    r = w.reshape(K, N_HEADS, -1)[:, :, lo:hi]
    return jnp.pad(r, ((0, 0), (0, 0), (0, HEAD_PAD - (hi - lo)))).reshape(K, N_HEADS * HEAD_PAD)


def kernel(x, c, positions, ada_w, ada_b, norm_g, mla_w_dq, mla_q_norm_g, mla_w_uq, mla_w_dkv, mla_kv_norm_g, mla_w_ukv, mla_w_o, conv_w_pw1, conv_b_pw1, conv_w_dw, conv_b_dw, conv_ln_g, conv_ln_b, conv_w_pw2, conv_b_pw2, pool_w, pool_b, pool_scale, ffn_w1, ffn_w2, loss_target, m_ada_w, m_ada_b, m_norm_g, m_mla_w_dq, m_mla_q_norm_g, m_mla_w_uq, m_mla_w_dkv, m_mla_kv_norm_g, m_mla_w_ukv, m_mla_w_o, m_conv_w_pw1, m_conv_b_pw1, m_conv_w_dw, m_conv_b_dw, m_conv_ln_g, m_conv_ln_b, m_conv_w_pw2, m_conv_b_pw2, m_pool_w, m_pool_b, m_pool_scale, m_ffn_w1, m_ffn_w2, v_ada_w, v_ada_b, v_norm_g, v_mla_w_dq, v_mla_q_norm_g, v_mla_w_uq, v_mla_w_dkv, v_mla_kv_norm_g, v_mla_w_ukv, v_mla_w_o, v_conv_w_pw1, v_conv_b_pw1, v_conv_w_dw, v_conv_b_dw, v_conv_ln_g, v_conv_ln_b, v_conv_w_pw2, v_conv_b_pw2, v_pool_w, v_pool_b, v_pool_scale, v_ffn_w1, v_ffn_w2):
    args = dict(locals())
    W = {n: args[n] for n, _ in WEIGHTS}
    M1 = {n: args['m_' + n] for n, _ in WEIGHTS}
    V2 = {n: args['v_' + n] for n, _ in WEIGHTS}
    D = D_MODEL
    T = x.shape[1]
    L = ffn_w1.shape[0]
    xi, yi, ci = _place()
    me = 4 * xi + 2 * yi + ci
    n_ada = ada_w.shape[2]

    small_sizes = [W[n].size for n in SMALL]
    small_in = _pack([c.reshape(-1)] + [W[n].reshape(-1) for n in SMALL], F32, 8)
    small_all = _ag_small("ag_small_params", small_in).reshape(N_DEV, -1)
    c_all = small_all[:, :D]
    Ws = {}
    off = D
    for n, sz in zip(SMALL, small_sizes):
        Ws[n] = _unshard(small_all[:, off:off + sz].reshape((N_DEV,) + W[n].shape), SHARD_AXIS[n])
        off += sz
    c16 = jnp.pad(c_all, ((0, 16 - N_DEV), (0, 0)))

    ada_b_cols = lax.dynamic_slice_in_dim(ada_b, me * n_ada, n_ada, axis=1).reshape(L, 1, n_ada)
    mod_part = _mod_part(c16, ada_w, ada_b_cols)[:, :N_DEV]
    mod_all = _ag_small("ag_mod", mod_part.reshape(L * N_DEV, n_ada)).reshape(N_DEV, L, N_DEV, n_ada)
    mod_mine = lax.dynamic_index_in_dim(mod_all, me, axis=2, keepdims=False)
    mod = jnp.transpose(mod_mine, (1, 0, 2)).reshape(L, 6, 1, D)

    mla_names = [n for n in BIG if n.startswith('mla')]
    first_items = [(n, W[n][0]) for n in mla_names]
    later_items = [(n, W[n][1:]) for n in mla_names] + [(n, W[n]) for n in BIG if not n.startswith(('mla', 'ffn'))]
    first_all, = _ag_big("ag_weights", [_pack([a.reshape(-1) for _, a in first_items], MM, 32)])
    wf = [ffn_w1.astype(MM), ffn_w2.astype(MM), _pack([a.reshape(-1) for _, a in later_items], MM, 32)]
    wf, first_all, mod = lax.optimization_barrier((wf, first_all, mod))
    wf_land = [lax.dynamic_update_slice(lax.empty((N_DEV,) + w.shape, MM), w[None], (me,) + (0,) * w.ndim) for w in wf]
    ag_sems, wf_thru, wf_land, ag_token = _copies_start("ag_ffn_start", wf, wf_land, FIRST_LEVEL_PEERS, False)

    def unpack(g, items, dropped):
        flat, out, off = g.reshape(N_DEV, -1), {}, 0
        for n, a in items:
            out[n] = _unshard(flat[:, off:off + a.size].reshape((N_DEV,) + a.shape), SHARD_AXIS[n] - dropped)
            off += a.size
        return out

    n_mla = mla_w_dq.shape[0]
    w_dq, w_uq_p, w_ukv_k, w_ukv_v, w_dkv_p, w_o_p = ([None] * n_mla for _ in range(6))

    def set_mla(j, w):
        w_dq[j] = w['mla_w_dq']
        w_uq_p[j] = _pad_heads(w['mla_w_uq'], 0, QK_NOPE + QK_ROPE)
        w_ukv_k[j] = _pad_heads(w['mla_w_ukv'], 0, QK_NOPE)
        w_ukv_v[j] = _pad_heads(w['mla_w_ukv'], QK_NOPE, QK_NOPE + V_HEAD)
        w_dkv_p[j] = jnp.pad(jnp.concatenate([w['mla_w_dkv'][:, :KV_LORA], jnp.zeros((D, QK_NOPE), MM), w['mla_w_dkv'][:, KV_LORA:]], axis=1),
                             ((0, 0), (0, HEAD_PAD - QK_NOPE - QK_ROPE)))
        w_o_p[j] = jnp.pad(w['mla_w_o'].reshape(N_HEADS, V_HEAD, D), ((0, 0), (0, HEAD_PAD - V_HEAD), (0, 0))).reshape(N_HEADS * HEAD_PAD, D)

    set_mla(0, unpack(first_all, first_items, 1))
    w_dw32 = jnp.pad(Ws['conv_w_dw'], ((0, 0), (0, 32 - CONV_W), (0, 0)))
    row = lambda t: t.reshape(1, -1)

    half = QK_ROPE // 2
    inv_freq = ROPE_THETA ** (-jnp.arange(0, QK_ROPE, 2, dtype=F32) / QK_ROPE)
    invf = jnp.zeros((1, HEAD_PAD), F32).at[0, QK_NOPE:QK_NOPE + half].set(inv_freq).at[0, QK_NOPE + half:QK_NOPE + QK_ROPE].set(inv_freq)
    rC, rS1, rS2 = _rope_tables(positions.reshape(T, 1).astype(F32), invf)

    xs = x.reshape(T, D)
    saved = []
    for i in range(L):
        kind, j = i % 3, i // 3
        sh_m, sc_m, gt_m, sh_f, sc_f, gt_f = (mod[i, r] for r in range(6))
        g = [row(Ws['norm_g'][i, r]) for r in range(4)]
        st = dict(x0=xs)
        if i == 0:
            sc_m = sc_m + ag_token[0:1, 0:1]
        if kind == 0:
            h = _prenorm(f"prenorm_m{i}", xs, g[0], sc_m, sh_m, MM)
            cq_raw, cq, ckv_raw, ckv, q, k, v = _mla_proj(f"mla_proj{i}", h, rC, rS1, rS2, w_dq[j], row(Ws['mla_q_norm_g'][j]), w_uq_p[j],
                                                          w_dkv_p[j], row(Ws['mla_kv_norm_g'][j]), w_ukv_k[j], w_ukv_v[j])
            o, lse = _attn_fwd(f"attn_fwd{i}", q, k, v)
            y, xs = _mm_post(f"mla_out{i}", o, w_o_p[j], None, xs, g[1], gt_m)
            st.update(h=h, cq_raw=cq_raw, cq=cq, ckv_raw=ckv_raw, ckv=ckv, q=q, k=k, v=v, o=o, lse=lse, y=y)
        elif kind == 1:
            h = _prenorm(f"prenorm_m{i}", xs, g[0], sc_m, sh_m, MM)
            a, u = _conv_glu(f"conv_glu{i}", h, w_pw1[j], row(W['conv_b_pw1'][j]))
            uc, z, y, xs = _conv_dw(f"conv_dw{i}", u, w_dw32[j], row(W['conv_b_dw'][j]), row(W['conv_ln_g'][j]), row(W['conv_ln_b'][j]),
                                    w_pw2[j], row(W['conv_b_pw2'][j]), xs, g[1], gt_m)
            st.update(h=h, a=a, u=u, uc=uc, z=z, y=y)
        else:
            h = _prenorm(f"prenorm_m{i}", xs, g[0], sc_m, sh_m, F32)
            p, ypre, y, xs = _pool_fwd(f"pool_fwd{i}", h, w_pool[j], row(Ws['pool_b'][j]), row(Ws['pool_scale'][j]), xs, g[1], gt_m)
            st.update(p=p, ypre=ypre, y=y)
        st['x1'] = xs
        if i == 0:
            wg = _copies_wait("ag_ffn_wait", ag_sems, wf_thru, wf_land, xs, FIRST_LEVEL_PEERS, False)
            w1g, w2g, later_all = _ag_forward("ag_ffn_forward", wg)
            later = unpack(later_all, later_items, 0)
            for jj in range(1, n_mla):
                set_mla(jj, {n: later[n][jj - 1] for n in mla_names})
            w_pw1, w_pw2, w_pool = later['conv_w_pw1'], later['conv_w_pw2'], later['pool_w']
        hf = _prenorm(f"prenorm_f{i}", xs, g[2], sc_f, sh_f, MM)
        af, yf, xs = _ffn_fwd(f"ffn_fwd{i}", i, hf, w1g, w2g, xs, g[3], gt_f)
        st.update(hf=hf, af=af, yf=yf)
        saved.append(st)

    dx, loss_row = _loss_head(xs, loss_target.reshape(T, D))

    G = {}
    dmod = [None] * L
    dnorm = [None] * L
    rs_pending = None
    ffn_red = [lax.empty(ffn_w1.shape, F32), lax.empty(ffn_w2.shape, F32)]
    for i in reversed(range(L)):
        kind, j = i % 3, i // 3
        sh_m, sc_m, gt_m, sh_f, sc_f, gt_f = (mod[i, r] for r in range(6))
        g = [row(Ws['norm_g'][i, r]) for r in range(4)]
        st = saved[i]
        dy, dg3, dgt_f, _ = _post_bwd(f"post_bwd_f{i}", dx, st['yf'], g[3], gt_f)
        da, dx, dsh_f, dsc_f, dg2 = _ffn_bwd(f"ffn_bwd{i}", i, dy, st['af'], w1g, w2g, st['x1'], dx, g[2], sc_f)
        wire1, own1 = _mm_tn_wire(f"ffn_dw1_{i}", st['hf'], da, me, False, False)
        wire2, own2 = _mm_tn_wire(f"ffn_dw2_{i}", st['af'], dy, me, True, True)
        if rs_pending is not None:
            ffn_red = _rs_finish(rs_pending, wire2, me, ffn_red)
        wires = [wire1, wire2]
        rs_sems, wires_thru, rs_lands, rs_token = _copies_start(f"rs_start{i}", wires, [lax.empty(w.shape, MM) for w in wires], ALL_PEERS, True)
        rs_pending = (i, rs_sems, wires_thru, rs_lands, [own1, own2])
        dy, dg1, dgt_m, dysum = _post_bwd(f"post_bwd_m{i}", dx, st['y'], g[1], gt_m + rs_token[0:1, 0:1])
        if kind == 0:
            do = _mm_nt_rows(f"mla_do{i}", dy, w_o_p[j])
            delta = _attn_delta(f"attn_delta{i}", do, st['o'])
            dq, dk, dv = _attn_bwd(f"attn_bwd{i}", st['q'], st['k'], st['v'], do, st['lse'], delta)
            dq_pre, dcq_raw, dckv_all, dx, dqg, dkvg, dsh_m, dsc_m, dg0 = _mla_proj_bwd(
                f"mla_proj_bwd{i}", dq, dk, dv, rC, rS1, rS2, st['cq_raw'], st['ckv_raw'], st['x0'], dx, w_uq_p[j], w_ukv_k[j], w_ukv_v[j],
                w_dq[j], w_dkv_p[j], row(Ws['mla_q_norm_g'][j]), row(Ws['mla_kv_norm_g'][j]), g[0], sc_m)
            dwo = _mm_tn(f"mla_dwo{i}", st['o'], dy)
            dwuq = _mm_tn(f"mla_dwuq{i}", st['cq'], dq_pre)
            dwk = _mm_tn(f"mla_dwukvk{i}", st['ckv'], dk)
            dwv = _mm_tn(f"mla_dwukvv{i}", st['ckv'], dv)
            dwdq = _mm_tn(f"mla_dwdq{i}", st['h'], dcq_raw)
            dwdkv = _mm_tn(f"mla_dwdkv{i}", st['h'], dckv_all)
            G.setdefault('mla_w_o', [None] * n_mla)[j] = dwo.reshape(N_HEADS, HEAD_PAD, D)[:, :V_HEAD].reshape(N_HEADS * V_HEAD, D)
            G.setdefault('mla_w_uq', [None] * n_mla)[j] = dwuq.reshape(Q_LORA, N_HEADS, HEAD_PAD)[:, :, :QK_NOPE + QK_ROPE].reshape(Q_LORA, -1)
            G.setdefault('mla_w_ukv', [None] * n_mla)[j] = jnp.concatenate(
                [dwk.reshape(KV_LORA, N_HEADS, HEAD_PAD)[:, :, :QK_NOPE], dwv.reshape(KV_LORA, N_HEADS, HEAD_PAD)[:, :, :V_HEAD]], axis=2).reshape(KV_LORA, -1)
            G.setdefault('mla_w_dq', [None] * n_mla)[j] = dwdq
            G.setdefault('mla_w_dkv', [None] * n_mla)[j] = jnp.concatenate([dwdkv[:, :KV_LORA], dwdkv[:, KV_LORA + QK_NOPE:KV_LORA + QK_NOPE + QK_ROPE]], axis=1)
            G.setdefault('mla_q_norm_g', [None] * n_mla)[j] = dqg[0]
            G.setdefault('mla_kv_norm_g', [None] * n_mla)[j] = dkvg[0]
        elif kind == 1:
            duc, dlng, dlnb, dbdw = _conv_bwd1(f"conv_bwd1_{i}", dy, st['uc'], w_pw2[j], row(W['conv_ln_g'][j]), row(W['conv_ln_b'][j]))
            da, dx, dwdw, dbpw1, dsh_m, dsc_m, dg0 = _conv_bwd2(f"conv_bwd2_{i}", duc, st['u'], st['a'], st['x0'], dx, w_dw32[j], w_pw1[j], g[0], sc_m)
            G['conv_w_pw2'] = [_mm_tn(f"conv_dwpw2_{i}", st['z'], dy)]
            G['conv_w_pw1'] = [_mm_tn(f"conv_dwpw1_{i}", st['h'], da)]
            G['conv_w_dw'] = [dwdw[:CONV_W]]
            G['conv_b_pw1'], G['conv_b_dw'], G['conv_ln_g'], G['conv_ln_b'], G['conv_b_pw2'] = [dbpw1[0]], [dbdw[0]], [dlng[0]], [dlnb[0]], [dysum[0]]
        else:
            dp, dypre, dscale, dpb = _pool_bwd1(f"pool_bwd1_{i}", dy, st['ypre'], row(Ws['pool_scale'][j]), w_pool[j])
            dx, dsh_m, dsc_m, dg0 = _pool_bwd2(f"pool_bwd2_{i}", dp, st['x0'], dx, g[0], sc_m)
            G['pool_w'] = [_mm_tn(f"pool_dw{i}", st['p'], dypre, diag=len(POOL_WINDOWS))]
            G['pool_b'] = [dpb.reshape(len(POOL_WINDOWS), -1)]
            G['pool_scale'] = [dscale[0]]
        dmod[i] = jnp.concatenate([dsh_m, dsc_m, dgt_m, dsh_f, dsc_f, dgt_f], axis=1)
        dnorm[i] = jnp.concatenate([dg0, dg1, dg2, dg3], axis=0)
    G['norm_g'] = dnorm
    grad_x = dx.reshape(x.shape)

    rs_names = [n for n, ax in WEIGHTS if ax is not None and n != 'ada_w' and not n.startswith('ffn')]
    pieces = [(n, _layer_shards(g, SHARD_AXIS[n] - 1)) for n in rs_names for g in G[n]]
    big = [(n, p) for n, p in pieces if p.shape[1] % (8 * 1024) == 0]
    small = [(n, p) for n, p in pieces if p.shape[1] % (8 * 1024) != 0]
    packed = jnp.concatenate([p.reshape(N_DEV, -1, 1024) for _, p in big] + [_pack([p for _, p in small], F32, 8)], axis=1)
    ffn_red = _rs_finish(rs_pending, dx, me, ffn_red)
    red = _reduce_scatter("rs", [packed], ci, 2 * xi + yi)[0]
    grads = {'ffn_w1': ffn_red[0], 'ffn_w2': ffn_red[1]}
    got = {}
    row0 = 0
    for n, p in big:
        rows = p.shape[1] // 1024
        got.setdefault(n, []).append(red[row0:row0 + rows])
        row0 += rows
    tail = red[row0:].reshape(-1)
    off = 0
    for n, p in small:
        got.setdefault(n, []).append(tail[off:off + p.shape[1]])
        off += p.shape[1]
    for n in rs_names:
        grads[n] = jnp.stack([g_.reshape(W[n].shape[1:]) for g_ in got[n]], axis=0)

    dmod_mine = jnp.concatenate(dmod, axis=1).reshape(-1)
    fin_in = _pack([dmod_mine] + [G[n][0].reshape(-1) for n in REPL] + [loss_row.reshape(-1)], F32, 8)
    fin_all = _ag_small("ag_final", fin_in)
    fin_sum = _sum_devices("final_sum", fin_all).reshape(-1)
    nm = L * 6 * D
    grads['ada_b'] = fin_sum[:nm].reshape(L, 6 * D)
    off = nm
    for n in REPL:
        grads[n] = fin_sum[off:off + W[n].size].reshape(W[n].shape)
        off += W[n].size
    loss = fin_sum[off]
    dmod_all = fin_all.reshape(N_DEV, -1)[:, :nm].reshape(N_DEV, L, 6 * D)
    dmod_cols = lax.dynamic_slice_in_dim(dmod_all, me * n_ada, n_ada, axis=2)
    dmod16 = jnp.pad(jnp.transpose(dmod_cols, (1, 0, 2)), ((0, 0), (0, 16 - N_DEV), (0, 0)))
    grads['ada_w'] = _ada_w_grad(c16, dmod16)

    deltas, new_m, new_v = {}, {}, {}
    for n, _ in WEIGHTS:
        deltas[n], new_m[n], new_v[n] = _adamw("adamw_" + n, W[n], grads[n], M1[n], V2[n])
    names = [n for n, _ in WEIGHTS]
    return (loss, grad_x, *[grads[n] for n in names], *[deltas[n] for n in names], *[new_m[n] for n in names],
            *[new_v[n] for n in names])
```

```python
import functools
import math

import jax
import jax.numpy as jnp
from jax import lax
from jax.experimental import pallas as pl
from jax.experimental.pallas import tpu as pltpu

F32 = jnp.float32
MM = jnp.bfloat16
EPS = 1e-6
NEG = -1e30
N_DEV = 8
VMEM_LIMIT = 48 * 1024 * 1024
MESH = pl.DeviceIdType.MESH

D_MODEL = 1024
N_HEADS = 16
HEAD_PAD = 128
QK_NOPE, QK_ROPE, V_HEAD = 64, 32, 64
Q_LORA, KV_LORA = 384, 256
CHUNK = 64
CONV_W = 31
POOL_WINDOWS = (2, 4, 8, 16)
ROPE_THETA = 10000.0
ATT_SCALE = 1.0 / math.sqrt(QK_NOPE + QK_ROPE)

ADAM_LR, ADAM_B1, ADAM_B2, ADAM_EPS, ADAM_WD, ADAM_STEP = 0.001, 0.9, 0.999, 1e-08, 0.01, 10

WEIGHTS = [('ada_w', 2), ('ada_b', None), ('norm_g', 2), ('mla_w_dq', 1), ('mla_q_norm_g', 1), ('mla_w_uq', 2),
           ('mla_w_dkv', 1), ('mla_kv_norm_g', 1), ('mla_w_ukv', 2), ('mla_w_o', 1), ('conv_w_pw1', 2),
           ('conv_b_pw1', None), ('conv_w_dw', 2), ('conv_b_dw', None), ('conv_ln_g', None), ('conv_ln_b', None),
           ('conv_w_pw2', 1), ('conv_b_pw2', None), ('pool_w', 2), ('pool_b', 2), ('pool_scale', 1),
           ('ffn_w1', 2), ('ffn_w2', 1)]
SHARD_AXIS = dict(WEIGHTS)
BIG = ['mla_w_dq', 'mla_w_uq', 'mla_w_dkv', 'mla_w_ukv', 'mla_w_o', 'conv_w_pw1', 'conv_w_pw2', 'pool_w', 'ffn_w1', 'ffn_w2']
SMALL = ['norm_g', 'mla_q_norm_g', 'mla_kv_norm_g', 'conv_w_dw', 'pool_b', 'pool_scale']
REPL = ['conv_b_pw1', 'conv_b_dw', 'conv_ln_g', 'conv_ln_b', 'conv_b_pw2']


def _dot(a, b):
    return jnp.dot(a.astype(MM), b.astype(MM), preferred_element_type=F32)


def _dot_nt(a, b):
    return lax.dot_general(a.astype(MM), b.astype(MM), (((1,), (1,)), ((), ())), preferred_element_type=F32)


def _dot_tn(a, b):
    return lax.dot_general(a.astype(MM), b.astype(MM), (((0,), (0,)), ((), ())), preferred_element_type=F32)


def _sigmoid(x):
    return 1.0 / (1.0 + jnp.exp(-x))


def _rstd(x):
    return lax.rsqrt(jnp.mean(x * x, axis=-1, keepdims=True) + EPS)


def _rms(x, g):
    return x * _rstd(x) * g


def _rms_bwd(x, g, dout):
    r = _rstd(x)
    xn = x * r
    dg = jnp.sum(dout * xn, axis=0, keepdims=True)
    dxn = dout * g
    dx = r * (dxn - xn * jnp.mean(dxn * xn, axis=-1, keepdims=True))
    return dx, dg


def _prenorm_bwd(x, g0, sc, dh):
    r = _rstd(x)
    xn = x * r
    dsh = jnp.sum(dh, axis=0, keepdims=True)
    dsc = jnp.sum(dh * (xn * g0), axis=0, keepdims=True)
    dn = dh * (1.0 + sc)
    dg0 = jnp.sum(dn * xn, axis=0, keepdims=True)
    dxn = dn * g0
    dx = r * (dxn - xn * jnp.mean(dxn * xn, axis=-1, keepdims=True))
    return dx, dsh, dsc, dg0


def _cparams(sem):
    return pltpu.CompilerParams(dimension_semantics=sem, vmem_limit_bytes=VMEM_LIMIT)


def _rows(name, body, n_rows, tm, rows, consts, outs, accs=(), scratch=()):
    tm = min(tm, n_rows)
    nblk = n_rows // tm
    nr, nc, no, na = len(rows), len(consts), len(outs), len(accs)
    in_specs, args = [], []
    for a, kind in rows:
        if kind == 'cur':
            im = lambda i: (i, 0)
        elif kind == 'prev':
            im = lambda i: (jnp.maximum(i - 1, 0), 0)
        else:
            im = lambda i: (jnp.minimum(i + 1, nblk - 1), 0)
        in_specs.append(pl.BlockSpec((tm, a.shape[1]), im))
        args.append(a)
    for a in consts:
        in_specs.append(pl.BlockSpec(a.shape, lambda i, nd=a.ndim: (0,) * nd))
        args.append(a)
    out_specs = [pl.BlockSpec((tm, c), lambda i: (i, 0)) for c, _ in outs]
    out_specs += [pl.BlockSpec(s, lambda i, nd=len(s): (0,) * nd) for s in accs]
    out_shape = [jax.ShapeDtypeStruct((n_rows, c), dt) for c, dt in outs]
    out_shape += [jax.ShapeDtypeStruct(s, F32) for s in accs]

    def kern(*refs):
        i = pl.program_id(0)
        rr = refs[:nr]
        cc = refs[nr:nr + nc]
        oo = refs[nr + nc:nr + nc + no]
        aa = refs[nr + nc + no:nr + nc + no + na]
        ss = refs[nr + nc + no + na:]

        @pl.when(i == 0)
        def _():
            for a in aa:
                a[...] = jnp.zeros(a.shape, F32)

        body(i, nblk, rr, cc, oo, aa, ss)

    return pl.pallas_call(kern, grid=(nblk,), in_specs=in_specs, out_specs=out_specs, out_shape=out_shape,
                          scratch_shapes=list(scratch), name=name, compiler_params=_cparams(("arbitrary",)))(*args)


def _place():
    return lax.axis_index("x"), lax.axis_index("y"), lax.axis_index("c")


def _ag_small(name, xs):
    R, C = xs.shape

    def body(x_ref, out_ref, send_sems, recv_sems):
        x, y, c = _place()
        me = 4 * x + 2 * y + c
        out_ref[me] = x_ref[...]
        copies = []
        for k in range(1, N_DEV):
            peer = ((1 - x) if k & 4 else x, (1 - y) if k & 2 else y, (1 - c) if k & 1 else c)
            cp = pltpu.make_async_remote_copy(src_ref=x_ref, dst_ref=out_ref.at[me], send_sem=send_sems.at[k - 1],
                                              recv_sem=recv_sems.at[k - 1], device_id=peer, device_id_type=MESH)
            cp.start()
            copies.append(cp)
        for cp in copies:
            cp.wait()

    return pl.pallas_call(
        body, out_shape=jax.ShapeDtypeStruct((N_DEV, R, C), xs.dtype),
        in_specs=[pl.BlockSpec(memory_space=pltpu.VMEM)], out_specs=pl.BlockSpec(memory_space=pltpu.VMEM),
        scratch_shapes=[pltpu.SemaphoreType.DMA((N_DEV - 1,)), pltpu.SemaphoreType.DMA((N_DEV - 1,))], name=name)(xs)


def _ag_big(name, xs):
    nt = len(xs)

    def body(*refs):
        x_refs, out_refs = refs[:nt], refs[nt:2 * nt]
        send_sems, recv_sems, local_sems = refs[2 * nt:]
        x, y, c = _place()
        me, sibling = (x, y, c), (x, y, 1 - c)
        chips = [(1 - x, y), (x, 1 - y), (1 - x, 1 - y)]

        def copy(t, k, block, to, own=False):
            px, py, pc = block
            rows = out_refs[t].at[4 * px + 2 * py + pc]
            return pltpu.make_async_remote_copy(src_ref=x_refs[t] if own else rows, dst_ref=rows, send_sem=send_sems.at[7 * t + k],
                                                recv_sem=recv_sems.at[7 * t + k], device_id=to, device_id_type=MESH)

        mine = [pltpu.make_async_copy(x_refs[t], out_refs[t].at[4 * x + 2 * y + c], local_sems.at[t]) for t in range(nt)]
        for cp in mine:
            cp.start()
        first = []
        for t in range(nt):
            first.append(copy(t, 0, me, sibling, own=True))
            first += [copy(t, 1 + j, me, (*chip, c), own=True) for j, chip in enumerate(chips)]
        for cp in first:
            cp.start()
        passed = []
        for t in range(nt):
            for j, chip in enumerate(chips):
                copy(t, 1 + j, (*chip, c), me).wait_recv()
                cp = copy(t, 4 + j, (*chip, c), sibling)
                cp.start()
                passed.append(cp)
        for t in range(nt):
            copy(t, 0, sibling, me).wait_recv()
            for j, chip in enumerate(chips):
                copy(t, 4 + j, (*chip, 1 - c), me).wait_recv()
        for cp in first + passed:
            cp.wait_send()
        for cp in mine:
            cp.wait()

    hbm = pl.BlockSpec(memory_space=pl.ANY)
    return pl.pallas_call(
        body, out_shape=[jax.ShapeDtypeStruct((N_DEV,) + t.shape, t.dtype) for t in xs],
        in_specs=[hbm] * nt, out_specs=[hbm] * nt,
        scratch_shapes=[pltpu.SemaphoreType.DMA((7 * nt,)), pltpu.SemaphoreType.DMA((7 * nt,)), pltpu.SemaphoreType.DMA((nt,))],
        name=name)(*xs)


def _rs_pair(name, ps):
    nt = len(ps)

    def body(*refs):
        p_refs, recv_refs = refs[:nt], refs[nt:2 * nt]
        send_sems, recv_sems = refs[2 * nt:]
        x, y, c = _place()
        copies = []
        for t in range(nt):
            for j in range(4):
                cp = pltpu.make_async_remote_copy(src_ref=p_refs[t].at[j, 1 - c], dst_ref=recv_refs[t].at[j], send_sem=send_sems.at[4 * t + j],
                                                  recv_sem=recv_sems.at[4 * t + j], device_id=(x, y, 1 - c), device_id_type=MESH)
                cp.start()
                copies.append(cp)
        for cp in copies:
            cp.wait()

    hbm = pl.BlockSpec(memory_space=pl.ANY)
    return pl.pallas_call(
        body, out_shape=[jax.ShapeDtypeStruct((4,) + p.shape[2:], p.dtype) for p in ps], in_specs=[hbm] * nt, out_specs=[hbm] * nt,
        scratch_shapes=[pltpu.SemaphoreType.DMA((4 * nt,)), pltpu.SemaphoreType.DMA((4 * nt,))], name=name)(*ps)


def _rs_chips(name, ss):
    nt = len(ss)

    def body(*refs):
        s_refs, recv_refs = refs[:nt], refs[nt:2 * nt]
        send_sems, recv_sems, local_sems = refs[2 * nt:]
        x, y, c = _place()
        mine = 2 * x + y
        owns = [pltpu.make_async_copy(s_refs[t].at[mine], recv_refs[t].at[mine], local_sems.at[t]) for t in range(nt)]
        for cp in owns:
            cp.start()
        copies = []
        for t in range(nt):
            for k in range(1, 4):
                px = (1 - x) if k & 2 else x
                py = (1 - y) if k & 1 else y
                cp = pltpu.make_async_remote_copy(src_ref=s_refs[t].at[2 * px + py], dst_ref=recv_refs[t].at[mine],
                                                  send_sem=send_sems.at[3 * t + k - 1], recv_sem=recv_sems.at[3 * t + k - 1],
                                                  device_id=(px, py, c), device_id_type=MESH)
                cp.start()
                copies.append(cp)
        for cp in copies:
            cp.wait()
        for cp in owns:
            cp.wait()

    hbm = pl.BlockSpec(memory_space=pl.ANY)
    return pl.pallas_call(
        body, out_shape=[jax.ShapeDtypeStruct(s_.shape, s_.dtype) for s_ in ss], in_specs=[hbm] * nt, out_specs=[hbm] * nt,
        scratch_shapes=[pltpu.SemaphoreType.DMA((3 * nt,)), pltpu.SemaphoreType.DMA((3 * nt,)), pltpu.SemaphoreType.DMA((nt,))],
        name=name)(*ss)


RS_ROWS = 256


def _row_block(r):
    return next(t for t in range(RS_ROWS, 0, -16) if r % t == 0)


def _pair_sum(name, p, recv, my_c, my_chip):
    _, _, r, c = p.shape
    tr = _row_block(r)

    def body(sc_ref, p_ref, r_ref, o_ref, own_ref):
        s = p_ref[...] + r_ref[...]
        o_ref[...] = s.astype(MM)

        @pl.when(pl.program_id(1) == sc_ref[1])
        def _():
            own_ref[...] = s

    return pl.pallas_call(
        body, grid_spec=pltpu.PrefetchScalarGridSpec(
            num_scalar_prefetch=1, grid=(r // tr, 4),
            in_specs=[pl.BlockSpec((None, None, tr, c), lambda i, j, sc: (j, sc[0], i, 0)),
                      pl.BlockSpec((None, tr, c), lambda i, j, sc: (j, i, 0))],
            out_specs=[pl.BlockSpec((None, tr, c), lambda i, j, sc: (j, i, 0)), pl.BlockSpec((tr, c), lambda i, j, sc: (i, 0))]),
        out_shape=[jax.ShapeDtypeStruct((4, r, c), MM), jax.ShapeDtypeStruct((r, c), F32)], name=name,
        compiler_params=_cparams(("arbitrary", "arbitrary")))(jnp.stack([my_c, my_chip]), p, recv)


def _chip_sum(name, own, recv, my_chip):
    _, r, c = recv.shape
    tr = _row_block(r)

    def body(sc_ref, own_ref, r_ref, o_ref):
        acc = jnp.zeros((tr, c), F32)
        for j in range(4):
            acc = acc + jnp.where(sc_ref[0] == j, own_ref[...], r_ref[j].astype(F32))
        o_ref[...] = acc

    return pl.pallas_call(
        body, grid_spec=pltpu.PrefetchScalarGridSpec(
            num_scalar_prefetch=1, grid=(r // tr,),
            in_specs=[pl.BlockSpec((tr, c), lambda i, sc: (i, 0)), pl.BlockSpec((4, tr, c), lambda i, sc: (0, i, 0))],
            out_specs=pl.BlockSpec((tr, c), lambda i, sc: (i, 0))),
        out_shape=jax.ShapeDtypeStruct((r, c), F32), name=name,
        compiler_params=_cparams(("arbitrary",)))(my_chip.reshape(1), own, recv)


def _reduce_scatter(tag, tensors, my_c, my_chip):
    ps = [t.reshape((4, 2) + t.shape[1:]) for t in tensors]
    recv = _rs_pair(tag + "_pair", ps)
    sums = [_pair_sum(f"{tag}_pair_sum{t}", ps[t], recv[t], my_c, my_chip) for t in range(len(ps))]
    recv2 = _rs_chips(tag + "_chips", [s_[0] for s_ in sums])
    return [_chip_sum(f"{tag}_chip_sum{t}", sums[t][1], recv2[t], my_chip) for t in range(len(ps))]


HBM_SPEC = pl.BlockSpec(memory_space=pltpu.HBM)
SEM_SPEC = pl.BlockSpec(memory_space=pltpu.SEMAPHORE)
SPLIT_EFFECT = pltpu.SideEffectType.DATAFLOW_SIDE_EFFECTING
ALL_PEERS = (1, 2, 3, 4, 5, 6, 7)
FIRST_LEVEL_PEERS = (1, 4, 2, 6)


def _split_copies(src_refs, land_refs, sems, masks, src_per_peer):
    n, nt = len(masks), len(src_refs)
    x, y, c = _place()
    me = 4 * x + 2 * y + c
    copies = []
    for t in range(nt):
        for k, mask in enumerate(masks):
            px, py, pc = (1 - x) if mask & 4 else x, (1 - y) if mask & 2 else y, (1 - c) if mask & 1 else c
            src = src_refs[t].at[4 * px + 2 * py + pc] if src_per_peer else src_refs[t]
            copies.append(pltpu.make_async_remote_copy(src_ref=src, dst_ref=land_refs[t].at[me], send_sem=sems[t * n + k],
                                                       recv_sem=sems[nt * n + t * n + k], device_id=(px, py, pc), device_id_type=MESH))
    return copies


def _copies_start(name, srcs, lands, masks, src_per_peer):
    nt, ns = len(srcs), 2 * len(masks) * len(srcs)

    def body(*refs):
        for cp in _split_copies(refs[:nt], refs[nt:2 * nt], refs[2 * nt:2 * nt + ns], masks, src_per_peer):
            cp.start()
        token = refs[-1]
        token[...] = jnp.zeros(token.shape, F32)

    outs = pl.pallas_call(
        body, name=name,
        out_shape=(pltpu.SemaphoreType.DMA(()),) * ns + tuple(pltpu.HBM(a.shape, a.dtype) for a in list(srcs) + list(lands))
        + (jax.ShapeDtypeStruct((8, 128), F32),),
        in_specs=(HBM_SPEC,) * (2 * nt), out_specs=(SEM_SPEC,) * ns + (HBM_SPEC,) * (2 * nt) + (pl.BlockSpec(memory_space=pltpu.VMEM),),
        input_output_aliases={t: ns + t for t in range(2 * nt)}, compiler_params=pltpu.CompilerParams(has_side_effects=SPLIT_EFFECT))(
            *[pltpu.with_memory_space_constraint(a, pltpu.HBM) for a in list(srcs) + list(lands)])
    return outs[:ns], outs[ns:ns + nt], outs[ns + nt:ns + 2 * nt], outs[-1]


def _copies_wait(name, sems, srcs_thru, lands_thru, after, masks, src_per_peer):
    nt, ns = len(srcs_thru), len(sems)

    def body(*refs):
        for cp in _split_copies(refs[:nt], refs[nt:2 * nt], refs[2 * nt:2 * nt + ns], masks, src_per_peer):
            cp.wait_send()
            cp.wait_recv()

    thru = list(srcs_thru) + list(lands_thru)
    return pl.pallas_call(
        body, name=name, out_shape=tuple(pltpu.HBM(a.shape, a.dtype) for a in thru),
        in_specs=(HBM_SPEC,) * (2 * nt) + (SEM_SPEC,) * ns + (pl.BlockSpec(memory_space=pl.ANY),), out_specs=(HBM_SPEC,) * (2 * nt),
        input_output_aliases={t: t for t in range(2 * nt)}, compiler_params=pltpu.CompilerParams(has_side_effects=SPLIT_EFFECT))(
            *thru, *sems, after)[nt:]


def _ag_forward(name, gs):
    nt = len(gs)

    def body(*refs):
        o_refs, send_sems, recv_sems = refs[nt:2 * nt], refs[2 * nt], refs[2 * nt + 1]
        x, y, c = _place()
        chips = [(1 - x, y), (x, 1 - y), (1 - x, 1 - y)]

        def copy(t, j, pc):
            rows = o_refs[t].at[4 * chips[j][0] + 2 * chips[j][1] + pc]
            return pltpu.make_async_remote_copy(src_ref=rows, dst_ref=rows, send_sem=send_sems.at[3 * t + j], recv_sem=recv_sems.at[3 * t + j],
                                                device_id=(x, y, 1 - c), device_id_type=MESH)

        for t in range(nt):
            for j in range(3):
                copy(t, j, c).start()
        for t in range(nt):
            for j in range(3):
                copy(t, j, c).wait_send()
                copy(t, j, 1 - c).wait_recv()

    hbm = pl.BlockSpec(memory_space=pl.ANY)
    return pl.pallas_call(body, out_shape=[jax.ShapeDtypeStruct(g.shape, g.dtype) for g in gs], in_specs=[hbm] * nt, out_specs=[hbm] * nt,
                          scratch_shapes=[pltpu.SemaphoreType.DMA((3 * nt,)), pltpu.SemaphoreType.DMA((3 * nt,))],
                          input_output_aliases={t: t for t in range(nt)}, name=name)(*gs)


def _mm_tn_wire(name, a, b, me, sqrelu, shard_rows):
    T, M = a.shape
    N = b.shape[1]
    tk = min(2048, T)
    nk = T // tk
    if shard_rows:
        bm, bn = M // N_DEV, N
        a_spec = pl.BlockSpec((tk, 2 * bm), lambda j, k, m: (k, j))
        b_spec = pl.BlockSpec((tk, bn), lambda j, k, m: (k, 0))
        halves = (slice(0, bm), slice(None)), (slice(bm, 2 * bm), slice(None))
        acc_shape = (2 * bm, bn)
    else:
        bm, bn = M, N // N_DEV
        a_spec = pl.BlockSpec((tk, bm), lambda j, k, m: (k, 0))
        b_spec = pl.BlockSpec((tk, 2 * bn), lambda j, k, m: (k, j))
        halves = (slice(None), slice(0, bn)), (slice(None), slice(bn, 2 * bn))
        acc_shape = (bm, 2 * bn)

    def body(me_ref, a_ref, b_ref, wire_ref, own_ref, acc):
        j, k = pl.program_id(0), pl.program_id(1)

        @pl.when(k == 0)
        def _():
            acc[...] = jnp.zeros(acc.shape, F32)

        av = a_ref[...]
        if sqrelu:
            r = jnp.maximum(av, 0.0)
            av = r * r
        acc[...] += _dot_tn(av, b_ref[...])

        for hh in range(2):
            @pl.when(k == nk - 1)
            def _():
                wire_ref[hh] = acc[halves[hh]].astype(MM)

            @pl.when((k == nk - 1) & (2 * j + hh == me_ref[0]))
            def _():
                own_ref[...] = acc[halves[hh]]

    return pl.pallas_call(
        body, grid_spec=pltpu.PrefetchScalarGridSpec(
            num_scalar_prefetch=1, grid=(N_DEV // 2, nk), in_specs=[a_spec, b_spec],
            out_specs=[pl.BlockSpec((2, bm, bn), lambda j, k, m: (j, 0, 0)), pl.BlockSpec((bm, bn), lambda j, k, m: (0, 0))],
            scratch_shapes=[pltpu.VMEM(acc_shape, F32)]),
        out_shape=[jax.ShapeDtypeStruct((N_DEV, bm, bn), MM), jax.ShapeDtypeStruct((bm, bn), F32)], name=name,
        compiler_params=_cparams(("arbitrary", "arbitrary")))(me.reshape(1), a, b)


def _rs_final(name, own, recv, me, stack, li):
    _, r, c = recv.shape
    tr = RS_ROWS

    def body(me_ref, own_ref, r_ref, s_ref, o_ref):
        acc = jnp.zeros((tr, c), F32)
        for j in range(N_DEV):
            acc = acc + jnp.where(me_ref[0] == j, own_ref[...], r_ref[j].astype(F32))
        o_ref[...] = acc

    return pl.pallas_call(
        body, grid_spec=pltpu.PrefetchScalarGridSpec(
            num_scalar_prefetch=1, grid=(r // tr,),
            in_specs=[pl.BlockSpec((tr, c), lambda i, m: (i, 0)), pl.BlockSpec((N_DEV, tr, c), lambda i, m: (0, i, 0)),
                      pl.BlockSpec(memory_space=pl.ANY)],
            out_specs=pl.BlockSpec((None, tr, c), lambda i, m: (li, i, 0))),
        out_shape=jax.ShapeDtypeStruct(stack.shape, F32), input_output_aliases={3: 0}, name=name,
        compiler_params=_cparams(("arbitrary",)))(me.reshape(1), own, recv, stack)


def _rs_finish(pending, after, me, stacks):
    i, sems, wires_thru, lands, owns = pending
    recvs = _copies_wait(f"rs_wait{i}", sems, wires_thru, lands, after, ALL_PEERS, True)
    return [_rs_final(f"rs_final{i}_{t}", owns[t], recvs[t], me, stacks[t], i) for t in range(len(owns))]


def _mod_part(c16, ada_w, ada_b_cols):
    L, D, n = ada_w.shape

    def body(c_ref, w_ref, b_ref, o_ref):
        cv = c_ref[...]
        o_ref[...] = _dot(cv * _sigmoid(cv), w_ref[...]) + b_ref[...]

    return pl.pallas_call(
        body, grid=(L,), in_specs=[pl.BlockSpec((16, D), lambda i: (0, 0)), pl.BlockSpec((None, D, n), lambda i: (i, 0, 0)),
                                   pl.BlockSpec((None, 1, n), lambda i: (i, 0, 0))],
        out_specs=pl.BlockSpec((None, 16, n), lambda i: (i, 0, 0)), out_shape=jax.ShapeDtypeStruct((L, 16, n), F32),
        name="ada_mod", compiler_params=_cparams(("arbitrary",)))(c16, ada_w, ada_b_cols)


def _ada_w_grad(c16, dmod16):
    L, _, n = dmod16.shape
    D = c16.shape[1]

    def body(c_ref, d_ref, o_ref):
        cv = c_ref[...]
        o_ref[...] = _dot_tn(cv * _sigmoid(cv), d_ref[...])

    return pl.pallas_call(
        body, grid=(L,), in_specs=[pl.BlockSpec((16, D), lambda i: (0, 0)), pl.BlockSpec((None, 16, n), lambda i: (i, 0, 0))],
        out_specs=pl.BlockSpec((None, D, n), lambda i: (i, 0, 0)), out_shape=jax.ShapeDtypeStruct((L, D, n), F32),
        name="ada_w_grad", compiler_params=_cparams(("arbitrary",)))(c16, dmod16)


def _sum_devices(name, g):
    _, R, C = g.shape

    def body(g_ref, o_ref):
        acc = g_ref[0]
        for d in range(1, N_DEV):
            acc = acc + g_ref[d]
        o_ref[...] = acc

    return pl.pallas_call(body, out_shape=jax.ShapeDtypeStruct((R, C), F32), name=name)(g)


def _prenorm(name, x, g0, sc, sh, dtype):
    T, D = x.shape

    def body(i, n, rr, cc, oo, aa, ss):
        oo[0][...] = (_rms(rr[0][...], cc[0][...]) * (1.0 + cc[1][...]) + cc[2][...]).astype(dtype)

    return _rows(name, body, T, 512, [(x, 'cur')], [g0, sc, sh], [(D, dtype)])[0]


def _post_bwd_math(d, yv, g1v, gtv):
    dgt = jnp.sum(d * _rms(yv, g1v), axis=0, keepdims=True)
    dy, dg1 = _rms_bwd(yv, g1v, d * gtv)
    return dy, dg1, dgt


def _post_bwd_nt(name, dxo, y, g1, gt, w):
    T, D = y.shape
    K = w.shape[0]

    def body(i, n, rr, cc, oo, aa, ss):
        dy, dg1, dgt = _post_bwd_math(rr[0][...], rr[1][...], cc[0][...], cc[1][...])
        aa[0][...] += dg1
        aa[1][...] += dgt
        dy = dy.astype(MM)
        oo[0][...] = dy
        oo[1][...] = _dot_nt(dy, cc[2][...]).astype(MM)

    return _rows(name, body, T, 512, [(dxo, 'cur'), (y, 'cur')], [g1, gt, w], [(D, MM), (K, MM)], accs=[(1, D)] * 2)


def _mm_post(name, a, w, bias, x, g1, gt):
    T, D = x.shape
    consts = [w, g1, gt] + ([bias] if bias is not None else [])

    def body(i, n, rr, cc, oo, aa, ss):
        y = _dot(rr[0][...], cc[0][...])
        if bias is not None:
            y = y + cc[3][...]
        oo[0][...] = y
        oo[1][...] = rr[1][...] + cc[2][...] * _rms(y, cc[1][...])

    return _rows(name, body, T, 512, [(a, 'cur'), (x, 'cur')], consts, [(D, F32), (D, F32)])


def _mm_tn(name, a, b, sqrelu=False, col_shards=0, diag=0):
    T, M = a.shape
    N = b.shape[1]
    tk = min(512, T)
    nk = T // tk
    if diag:
        bm, bn = M // diag, N // diag
        grid = (diag, 1, nk)
        a_spec = pl.BlockSpec((tk, bm), lambda g, n, k: (k, g))
        b_spec = pl.BlockSpec((tk, bn), lambda g, n, k: (k, g))
        o_spec = pl.BlockSpec((None, bm, bn), lambda g, n, k: (g, 0, 0))
        o_shape = (diag, bm, bn)
    else:
        bm = min(M, 1024)
        bn = N // col_shards if col_shards else min(N, 1024)
        grid = (M // bm, N // bn, nk)
        a_spec = pl.BlockSpec((tk, bm), lambda m, n, k: (k, m))
        b_spec = pl.BlockSpec((tk, bn), lambda m, n, k: (k, n))
        if col_shards:
            o_spec = pl.BlockSpec((None, bm, bn), lambda m, n, k: (n, m, 0))
            o_shape = (col_shards, M, bn)
        else:
            o_spec = pl.BlockSpec((bm, bn), lambda m, n, k: (m, n))
            o_shape = (M, N)

    def body(a_ref, b_ref, o_ref):
        @pl.when(pl.program_id(2) == 0)
        def _():
            o_ref[...] = jnp.zeros(o_ref.shape, F32)

        av = a_ref[...]
        if sqrelu:
            r = jnp.maximum(av, 0.0)
            av = r * r
        o_ref[...] += _dot_tn(av, b_ref[...])

    return pl.pallas_call(body, grid=grid, in_specs=[a_spec, b_spec], out_specs=o_spec,
                          out_shape=jax.ShapeDtypeStruct(o_shape, F32), name=name,
                          compiler_params=_cparams(("arbitrary", "arbitrary", "arbitrary")))(a, b)


FFN_SHARDS = 4
FFN_BWD_SHARDS = 2

def _ffn_fwd(name, li, x, g0, sc, sh, w1g, w2g, g1, gt):
    T, D = x.shape
    nf, tf = w1g.shape[0], w1g.shape[-1]
    F = nf * tf
    tm = min(512, T)

    def body(x_ref, g0_ref, sc_ref, sh_ref, w1_ref, w2_ref, g1_ref, gt_ref, h_ref, a_ref, y_ref, xo_ref, acc):
        f = pl.program_id(1)

        @pl.when(f == 0)
        def _():
            acc[...] = jnp.zeros(acc.shape, F32)
            h_ref[...] = (_rms(x_ref[...], g0_ref[...]) * (1.0 + sc_ref[...]) + sh_ref[...]).astype(MM)

        hv = h_ref[...]
        part = None
        for hh in range(FFN_SHARDS):
            a = _dot(hv, w1_ref[hh])
            a_ref[:, hh * tf:(hh + 1) * tf] = a.astype(MM)
            r = jnp.maximum(a, 0.0)
            p = _dot(r * r, w2_ref[hh])
            part = p if part is None else part + p
        acc[...] += part

        @pl.when(f == nf // FFN_SHARDS - 1)
        def _():
            y = acc[...]
            y_ref[...] = y
            xo_ref[...] = x_ref[...] + gt_ref[...] * _rms(y, g1_ref[...])

    row = lambda t, f: (t, 0)
    one = lambda t, f: (0, 0)
    return pl.pallas_call(
        body, grid=(T // tm, nf // FFN_SHARDS),
        in_specs=[pl.BlockSpec((tm, D), row)] + [pl.BlockSpec((1, D), one)] * 3
        + [pl.BlockSpec((FFN_SHARDS, None, D, tf), lambda t, f: (f, li, 0, 0)), pl.BlockSpec((FFN_SHARDS, None, tf, D), lambda t, f: (f, li, 0, 0)),
           pl.BlockSpec((1, D), one), pl.BlockSpec((1, D), one)],
        out_specs=[pl.BlockSpec((tm, D), row), pl.BlockSpec((tm, FFN_SHARDS * tf), lambda t, f: (t, f)), pl.BlockSpec((tm, D), row),
                   pl.BlockSpec((tm, D), row)],
        out_shape=[jax.ShapeDtypeStruct((T, D), MM), jax.ShapeDtypeStruct((T, F), MM), jax.ShapeDtypeStruct((T, D), F32),
                   jax.ShapeDtypeStruct((T, D), F32)],
        scratch_shapes=[pltpu.VMEM((tm, D), F32)], name=name,
        compiler_params=_cparams(("arbitrary", "arbitrary")))(x, g0, sc, sh, w1g, w2g, g1, gt)


def _ffn_bwd(name, li, y, g1, gt, a, w1g, w2g, x, dxo, g0, sc):
    T, D = x.shape
    nf, tf = w1g.shape[0], w1g.shape[-1]
    F = nf * tf
    tm = min(512, T)
    ns = FFN_BWD_SHARDS

    def body(y_ref, g1_ref, gt_ref, a_ref, w1_ref, w2_ref, x_ref, dxo_ref, g0_ref, sc_ref,
             dy_ref, da_ref, dx_ref, dg1_ref, dgt_ref, dsh_ref, dsc_ref, dg0_ref, acc):
        t, f = pl.program_id(0), pl.program_id(1)

        @pl.when((t == 0) & (f == 0))
        def _():
            for r in (dg1_ref, dgt_ref, dsh_ref, dsc_ref, dg0_ref):
                r[...] = jnp.zeros(r.shape, F32)

        @pl.when(f == 0)
        def _():
            acc[...] = jnp.zeros(acc.shape, F32)
            d, yv, g1v = dxo_ref[...], y_ref[...], g1_ref[...]
            dgt_ref[...] += jnp.sum(d * _rms(yv, g1v), axis=0, keepdims=True)
            dyf, dg1 = _rms_bwd(yv, g1v, d * gt_ref[...])
            dg1_ref[...] += dg1
            dy_ref[...] = dyf.astype(MM)

        dyv = dy_ref[...]
        dyv = dyv + dyv
        part = None
        for hh in range(ns):
            cols = slice(hh * tf, (hh + 1) * tf)
            du = _dot_nt(dyv, w2_ref[hh])
            da = (du * jnp.maximum(a_ref[:, cols], 0.0).astype(F32)).astype(MM)
            da_ref[:, cols] = da
            p = _dot_nt(da, w1_ref[hh])
            part = p if part is None else part + p
        acc[...] += part

        @pl.when(f == nf // ns - 1)
        def _():
            dx, dsh, dsc, dg0 = _prenorm_bwd(x_ref[...], g0_ref[...], sc_ref[...], acc[...])
            dx_ref[...] = dxo_ref[...] + dx
            dsh_ref[...] += dsh
            dsc_ref[...] += dsc
            dg0_ref[...] += dg0

    row = lambda t, f: (t, 0)
    one = lambda t, f: (0, 0)
    blk = lambda t, f: (t, f)
    return pl.pallas_call(
        body, grid=(T // tm, nf // ns),
        in_specs=[pl.BlockSpec((tm, D), row), pl.BlockSpec((1, D), one), pl.BlockSpec((1, D), one), pl.BlockSpec((tm, ns * tf), blk),
                  pl.BlockSpec((ns, None, D, tf), lambda t, f: (f, li, 0, 0)),
                  pl.BlockSpec((ns, None, tf, D), lambda t, f: (f, li, 0, 0)), pl.BlockSpec((tm, D), row), pl.BlockSpec((tm, D), row),
                  pl.BlockSpec((1, D), one), pl.BlockSpec((1, D), one)],
        out_specs=[pl.BlockSpec((tm, D), row), pl.BlockSpec((tm, ns * tf), blk), pl.BlockSpec((tm, D), row)] + [pl.BlockSpec((1, D), one)] * 5,
        out_shape=[jax.ShapeDtypeStruct((T, D), MM), jax.ShapeDtypeStruct((T, F), MM), jax.ShapeDtypeStruct((T, D), F32)]
        + [jax.ShapeDtypeStruct((1, D), F32)] * 5,
        scratch_shapes=[pltpu.VMEM((tm, D), F32)], name=name,
        compiler_params=_cparams(("arbitrary", "arbitrary")))(y, g1, gt, a, w1g, w2g, x, dxo, g0, sc)


def _rope_tables(pos, invf):
    T = pos.shape[0]

    def body(i, n, rr, cc, oo, aa, ss):
        ang = rr[0][...] * cc[0][...]
        lane = lax.broadcasted_iota(jnp.int32, ang.shape, 1)
        cs, sn = jnp.cos(ang), jnp.sin(ang)
        oo[0][...] = jnp.where((lane >= QK_NOPE) & (lane < QK_NOPE + QK_ROPE), cs, 1.0)
        oo[1][...] = jnp.where((lane >= QK_NOPE) & (lane < QK_NOPE + QK_ROPE // 2), -sn, 0.0)
        oo[2][...] = jnp.where((lane >= QK_NOPE + QK_ROPE // 2) & (lane < QK_NOPE + QK_ROPE), sn, 0.0)

    return _rows("rope_tables", body, T, 512, [(pos, 'cur')], [invf], [(HEAD_PAD, F32)] * 3)


def _rope(v, C, S1, S2):
    n = v.shape[1]
    reps = n // HEAD_PAD
    if reps > 1:
        C, S1, S2 = (jnp.tile(t, (1, reps)) for t in (C, S1, S2))
    return v * C + pltpu.roll(v, n - QK_ROPE // 2, 1) * S1 + pltpu.roll(v, QK_ROPE // 2, 1) * S2


def _unrope(d, C, S1, S2):
    n = d.shape[1]
    reps = n // HEAD_PAD
    if reps > 1:
        C, S1, S2 = (jnp.tile(t, (1, reps)) for t in (C, S1, S2))
    return d * C + pltpu.roll(d * S1, QK_ROPE // 2, 1) + pltpu.roll(d * S2, n - QK_ROPE // 2, 1)


def _mla_proj(name, x, g0, sc, sh, C, S1, S2, w_dq, qg, w_uq, w_dkv, kvg, w_ukv_k, w_ukv_v):
    T, D = x.shape
    HP = N_HEADS * HEAD_PAD

    def body(i, n, rr, cc, oo, aa, ss):
        hv = (_rms(rr[0][...], cc[7][...]) * (1.0 + cc[8][...]) + cc[9][...]).astype(MM)
        oo[7][...] = hv
        Cv, S1v, S2v = rr[1][...], rr[2][...], rr[3][...]
        cq_raw = _dot(hv, cc[0][...])
        cq = _rms(cq_raw, cc[1][...]).astype(MM)
        q = _rope(_dot(cq, cc[2][...]), Cv, S1v, S2v)
        ckv_all = _dot(hv, cc[3][...])
        ckv_raw = ckv_all[:, :KV_LORA]
        ckv = _rms(ckv_raw, cc[4][...]).astype(MM)
        kr = _rope(ckv_all[:, KV_LORA:], Cv, S1v, S2v)
        k = _dot(ckv, cc[5][...]) + jnp.tile(kr, (1, N_HEADS))
        v = _dot(ckv, cc[6][...])
        v = jnp.where(lax.broadcasted_iota(jnp.int32, v.shape, 1) % HEAD_PAD == V_HEAD, 1.0, v)
        oo[0][...] = cq_raw
        oo[1][...] = cq
        oo[2][...] = ckv_raw
        oo[3][...] = ckv
        oo[4][...] = q.astype(MM)
        oo[5][...] = k.astype(MM)
        oo[6][...] = v.astype(MM)

    return _rows(name, body, T, 256, [(x, 'cur'), (C, 'cur'), (S1, 'cur'), (S2, 'cur')],
                 [w_dq, qg, w_uq, w_dkv, kvg, w_ukv_k, w_ukv_v, g0, sc, sh],
                 [(Q_LORA, F32), (Q_LORA, MM), (KV_LORA, F32), (KV_LORA, MM), (HP, MM), (HP, MM), (HP, MM), (D, MM)])


ATT_HEADS = 4
ATT_BLOCK = 512
ATT_FWD_BLOCK = 1024


def _chunk_mask_t(tk, tq):
    ki = lax.broadcasted_iota(jnp.int32, (tk, tq), 0) // CHUNK
    qi = lax.broadcasted_iota(jnp.int32, (tk, tq), 1) // CHUNK
    return ki <= qi


def _attn_fwd(name, q, k, v):
    T = q.shape[0]
    tb = min(ATT_FWD_BLOCK, T)
    nb = T // tb
    nh = ATT_HEADS
    hs = [slice(h * HEAD_PAD, (h + 1) * HEAD_PAD) for h in range(nh)]

    def body(q_ref, k_ref, v_ref, o_ref, lse_ref):
        qb = pl.program_id(1)

        def k_block(k0, masked, st):
            new = []
            for h in range(nh):
                m, acc = st[h]
                s = _dot_nt(k_ref[pl.ds(k0, tb), hs[h]], q_ref[:, hs[h]])
                if masked:
                    s = jnp.where(_chunk_mask_t(tb, tb), s, NEG)
                m_new = jnp.maximum(m, jnp.max(s, axis=0, keepdims=True))
                alpha = jnp.exp((m - m_new) * ATT_SCALE)
                p = jnp.exp((s - m_new) * ATT_SCALE)
                acc = alpha * acc + _dot_tn(v_ref[pl.ds(k0, tb), hs[h]], p)
                new.append((m_new, acc))
            return tuple(new)

        st = tuple((jnp.full((1, tb), NEG, F32), jnp.zeros((HEAD_PAD, tb), F32)) for _ in range(nh))
        st = k_block(pl.multiple_of(qb * tb, tb), True, st)
        st = lax.fori_loop(0, qb, lambda kb, s_: k_block(pl.multiple_of(kb * tb, tb), False, s_), st)
        for h in range(nh):
            m, acc = st[h]
            l = acc[V_HEAD:V_HEAD + 1, :]
            o_ref[:, hs[h]] = (acc / l).T.astype(MM)
            lse_ref[h] = jnp.broadcast_to(m * ATT_SCALE + jnp.log(l), (8, tb))

    blk = pl.BlockSpec((tb, nh * HEAD_PAD), lambda g, i: (i, g))
    res = pl.BlockSpec((T, nh * HEAD_PAD), lambda g, i: (0, g))
    return pl.pallas_call(
        body, grid=(N_HEADS // nh, nb), in_specs=[blk, res, res],
        out_specs=[blk, pl.BlockSpec((nh, 8, tb), lambda g, i: (g, 0, i))],
        out_shape=[jax.ShapeDtypeStruct(q.shape, MM), jax.ShapeDtypeStruct((N_HEADS, 8, T), F32)], name=name,
        compiler_params=_cparams(("arbitrary", "arbitrary")))(q, k, v)


def _attn_delta(name, do, o):
    T = do.shape[0]
    tb = min(256, T)

    def body(do_ref, o_ref, d_ref):
        lane = lax.broadcasted_iota(jnp.int32, (tb, HEAD_PAD), 1) // 8
        cols = jnp.zeros((tb, HEAD_PAD), F32)
        for h in range(N_HEADS):
            hsl = slice(h * HEAD_PAD, (h + 1) * HEAD_PAD)
            r = jnp.sum(do_ref[:, hsl].astype(F32) * o_ref[:, hsl].astype(F32), axis=1, keepdims=True)
            cols = jnp.where(lane == h, r, cols)
        d_ref[...] = cols.T

    spec = pl.BlockSpec((tb, N_HEADS * HEAD_PAD), lambda i: (i, 0))
    out = pl.pallas_call(body, grid=(T // tb,), in_specs=[spec, spec], out_specs=pl.BlockSpec((HEAD_PAD, tb), lambda i: (0, i)),
                         out_shape=jax.ShapeDtypeStruct((HEAD_PAD, T), F32), name=name, compiler_params=_cparams(("arbitrary",)))(do, o)
    return out.reshape(N_HEADS, 8, T)


def _attn_bwd(name, q, k, v, do, lse, delta):
    T = q.shape[0]
    tb = min(ATT_BLOCK, T)
    nb = T // tb
    nh = ATT_HEADS
    hs = [slice(h * HEAD_PAD, (h + 1) * HEAD_PAD) for h in range(nh)]

    def body(q_ref, k_ref, v_ref, do_ref, lse_ref, dl_ref, dq_ref, dk_ref, dv_ref, dq_acc, dk_acc, dv_acc):
        kb = pl.program_id(1)

        @pl.when(kb == 0)
        def _():
            dq_acc[...] = jnp.zeros(dq_acc.shape, F32)

        dk_acc[...] = jnp.zeros(dk_acc.shape, F32)
        dv_acc[...] = jnp.zeros(dv_acc.shape, F32)

        def q_block(q0, masked):
            for h in range(nh):
                qh = q_ref[pl.ds(q0, tb), hs[h]]
                doh = do_ref[pl.ds(q0, tb), hs[h]]
                kh = k_ref[:, hs[h]]
                s = _dot_nt(kh, qh) * ATT_SCALE
                if masked:
                    s = jnp.where(_chunk_mask_t(tb, tb), s, NEG)
                p = jnp.exp(s - lse_ref[h, 0:1, pl.ds(q0, tb)])
                ds = (p * (_dot_nt(v_ref[:, hs[h]], doh) - dl_ref[h, 0:1, pl.ds(q0, tb)]) * ATT_SCALE).astype(MM)
                dv_acc[:, hs[h]] += _dot(p, doh)
                dk_acc[:, hs[h]] += _dot(ds, qh)
                dq_acc[pl.ds(q0, tb), hs[h]] += _dot_tn(ds, kh)

        q_block(pl.multiple_of(kb * tb, tb), True)

        def rest(qb, c_):
            q_block(pl.multiple_of(qb * tb, tb), False)
            return c_

        lax.fori_loop(kb + 1, nb, rest, 0)
        dk_ref[...] = dk_acc[...].astype(MM)
        dv_ref[...] = dv_acc[...].astype(MM)

        @pl.when(kb == nb - 1)
        def _():
            dq_ref[...] = dq_acc[...].astype(MM)

    W = nh * HEAD_PAD
    blk = pl.BlockSpec((tb, W), lambda g, i: (i, g))
    res = pl.BlockSpec((T, W), lambda g, i: (0, g))
    rows = pl.BlockSpec((nh, 8, T), lambda g, i: (g, 0, 0))
    return pl.pallas_call(
        body, grid=(N_HEADS // nh, nb), in_specs=[res, blk, blk, res, rows, rows], out_specs=[res, blk, blk],
        out_shape=[jax.ShapeDtypeStruct(q.shape, MM)] * 3,
        scratch_shapes=[pltpu.VMEM((T, W), F32), pltpu.VMEM((tb, W), F32), pltpu.VMEM((tb, W), F32)],
        name=name, compiler_params=_cparams(("arbitrary", "arbitrary")))(q, k, v, do, lse, delta)


def _mla_proj_bwd(name, dq, dk, dv, C, S1, S2, cq_raw, ckv_raw, x, dxo, w_uq, w_ukv_k, w_ukv_v, w_dq, w_dkv, qg, kvg, g0, sc):
    T, D = x.shape
    HP = N_HEADS * HEAD_PAD

    def body(i, n, rr, cc, oo, aa, ss):
        Cv, S1v, S2v = rr[3][...], rr[4][...], rr[5][...]
        dq_pre = _unrope(rr[0][...].astype(F32), Cv, S1v, S2v).astype(MM)
        oo[0][...] = dq_pre
        dcq = _dot_nt(dq_pre, cc[0][...])
        dcq_raw, dqg = _rms_bwd(rr[6][...], cc[5][...], dcq)
        aa[0][...] += dqg
        dcq_raw = dcq_raw.astype(MM)
        oo[1][...] = dcq_raw
        dkv = rr[1][...]
        dkr = dkv[:, :HEAD_PAD].astype(F32)
        for hh in range(1, N_HEADS):
            dkr = dkr + dkv[:, hh * HEAD_PAD:(hh + 1) * HEAD_PAD].astype(F32)
        lane = lax.broadcasted_iota(jnp.int32, dkr.shape, 1)
        dkr = jnp.where((lane >= QK_NOPE) & (lane < QK_NOPE + QK_ROPE), _unrope(dkr, Cv, S1v, S2v), 0.0)
        dckv = _dot_nt(dkv, cc[1][...]) + _dot_nt(rr[2][...], cc[2][...])
        dckv_raw, dkvg = _rms_bwd(rr[7][...], cc[6][...], dckv)
        aa[1][...] += dkvg
        dckv_all = jnp.concatenate([dckv_raw, dkr], axis=1).astype(MM)
        oo[2][...] = dckv_all
        dh = _dot_nt(dcq_raw, cc[3][...]) + _dot_nt(dckv_all, cc[4][...])
        dx, dsh, dsc, dg0 = _prenorm_bwd(rr[8][...], cc[7][...], cc[8][...], dh)
        oo[3][...] = rr[9][...] + dx
        aa[2][...] += dsh
        aa[3][...] += dsc
        aa[4][...] += dg0

    return _rows(name, body, T, 256,
                 [(dq, 'cur'), (dk, 'cur'), (dv, 'cur'), (C, 'cur'), (S1, 'cur'), (S2, 'cur'), (cq_raw, 'cur'), (ckv_raw, 'cur'),
                  (x, 'cur'), (dxo, 'cur')],
                 [w_uq, w_ukv_k, w_ukv_v, w_dq, w_dkv, qg, kvg, g0, sc],
                 [(HP, MM), (Q_LORA, MM), (KV_LORA + HEAD_PAD, MM), (D, F32)],
                 accs=[(1, Q_LORA), (1, KV_LORA), (1, D), (1, D), (1, D)])


HALO = 32


def _windows(ext, tm, first):
    rolled = {0: ext}
    out = []
    for j in range(CONV_W):
        r = (first + j) % 8
        if r not in rolled:
            rolled[r] = pltpu.roll(ext, ext.shape[0] - r, 0)
        out.append(rolled[r][first + j - r:first + j - r + tm])
    return out


def _conv_glu(name, x, g0, sc, sh, w_pw1, b_pw1):
    T, D = x.shape

    def body(i, n, rr, cc, oo, aa, ss):
        hv = (_rms(rr[0][...], cc[2][...]) * (1.0 + cc[3][...]) + cc[4][...]).astype(MM)
        oo[2][...] = hv
        a = _dot(hv, cc[0][...]) + cc[1][...]
        oo[0][...] = a
        oo[1][...] = a[:, :D] * _sigmoid(a[:, D:])

    return _rows(name, body, T, 512, [(x, 'cur')], [w_pw1, b_pw1, g0, sc, sh], [(2 * D, F32), (D, F32), (D, MM)])


def _layernorm_parts(uc):
    xc = uc - jnp.mean(uc, axis=-1, keepdims=True)
    r = lax.rsqrt(jnp.mean(xc * xc, axis=-1, keepdims=True) + EPS)
    return xc * r, r


def _conv_dw(name, u, w_dw, b_dw, ln_g, ln_b, w_pw2, b_pw2, x, g1, gt):
    T, D = u.shape
    tm = min(256, T)

    def body(i, n, rr, cc, oo, aa, ss):
        ext = jnp.concatenate([jnp.where(i > 0, rr[1][tm - HALO:tm, :], 0.0), rr[0][...]], axis=0)
        uc = jnp.zeros((tm, D), F32) + cc[1][...]
        for kk, win in enumerate(_windows(ext, tm, HALO - (CONV_W - 1))):
            uc = uc + win * cc[0][kk:kk + 1, :]
        xh, _ = _layernorm_parts(uc)
        ln = xh * cc[2][...] + cc[3][...]
        z = (ln * _sigmoid(ln)).astype(MM)
        y = _dot(z, cc[4][...]) + cc[5][...]
        oo[0][...] = uc
        oo[1][...] = z
        oo[2][...] = y
        oo[3][...] = rr[2][...] + cc[7][...] * _rms(y, cc[6][...])

    return _rows(name, body, T, tm, [(u, 'cur'), (u, 'prev'), (x, 'cur')], [w_dw, b_dw, ln_g, ln_b, w_pw2, b_pw2, g1, gt],
                 [(D, F32), (D, MM), (D, F32), (D, F32)])


def _conv_bwd1(name, dxo, y, g1, gt, uc, w_pw2, ln_g, ln_b):
    T, D = uc.shape

    def body(i, n, rr, cc, oo, aa, ss):
        dy, dg1, dgt = _post_bwd_math(rr[0][...], rr[1][...], cc[3][...], cc[4][...])
        aa[3][...] += dg1
        aa[4][...] += dgt
        aa[5][...] += jnp.sum(dy, axis=0, keepdims=True)
        dy = dy.astype(MM)
        oo[1][...] = dy
        dz = _dot_nt(dy, cc[0][...])
        xh, r = _layernorm_parts(rr[2][...])
        g = cc[1][...]
        ln = xh * g + cc[2][...]
        sg = _sigmoid(ln)
        dln = dz * (sg * (1.0 + ln * (1.0 - sg)))
        aa[0][...] += jnp.sum(dln * xh, axis=0, keepdims=True)
        aa[1][...] += jnp.sum(dln, axis=0, keepdims=True)
        dxh = dln * g
        duc = r * (dxh - jnp.mean(dxh, axis=-1, keepdims=True) - xh * jnp.mean(dxh * xh, axis=-1, keepdims=True))
        aa[2][...] += jnp.sum(duc, axis=0, keepdims=True)
        oo[0][...] = duc

    return _rows(name, body, T, 256, [(dxo, 'cur'), (y, 'cur'), (uc, 'cur')], [w_pw2, ln_g, ln_b, g1, gt], [(D, F32), (D, MM)],
                 accs=[(1, D)] * 6)


def _conv_bwd2(name, duc, u, a, x, dxo, w_dw, w_pw1, g0, sc):
    T, D = u.shape
    tm = min(256, T)

    def body(i, n, rr, cc, oo, aa, ss):
        dcur = rr[0][...]
        extd = jnp.concatenate([dcur, jnp.where(i < n - 1, rr[1][0:HALO, :], 0.0)], axis=0)
        extu = jnp.concatenate([jnp.where(i > 0, rr[3][tm - HALO:tm, :], 0.0), rr[2][...]], axis=0)
        wd = _windows(extd, tm, 0)
        wu = _windows(extu, tm, HALO - (CONV_W - 1))
        du = jnp.zeros((tm, D), F32)
        for kk in range(CONV_W):
            du = du + wd[CONV_W - 1 - kk] * cc[0][kk:kk + 1, :]
            aa[0][kk:kk + 1, :] += jnp.sum(dcur * wu[kk], axis=0, keepdims=True)
        av = rr[4][...]
        a1, sg = av[:, :D], _sigmoid(av[:, D:])
        da = jnp.concatenate([du * sg, du * a1 * (sg * (1.0 - sg))], axis=1)
        aa[1][...] += jnp.sum(da, axis=0, keepdims=True)
        da = da.astype(MM)
        oo[0][...] = da
        dx, dsh, dsc, dg0 = _prenorm_bwd(rr[5][...], cc[2][...], cc[3][...], _dot_nt(da, cc[1][...]))
        oo[1][...] = rr[6][...] + dx
        aa[2][...] += dsh
        aa[3][...] += dsc
        aa[4][...] += dg0

    return _rows(name, body, T, tm,
                 [(duc, 'cur'), (duc, 'next'), (u, 'cur'), (u, 'prev'), (a, 'cur'), (x, 'cur'), (dxo, 'cur')],
                 [w_dw, w_pw1, g0, sc], [(2 * D, MM), (D, F32)],
                 accs=[(32, D), (1, 2 * D), (1, D), (1, D), (1, D)])


PHALO = 16


def _pool_fwd(name, h, w, b, scale, x, g1, gt):
    T, D = h.shape
    G = len(POOL_WINDOWS)
    Cg = D // G
    tm = min(256, T)

    def body(i, n, rr, cc, oo, aa, ss):
        ext = ss[0]
        ext[0:PHALO, :] = jnp.where(i > 0, rr[1][tm - PHALO:tm, :], 0.0)
        ext[PHALO:PHALO + tm, :] = rr[0][...]
        t_glob = i * tm + lax.broadcasted_iota(jnp.int32, (tm, 1), 0)
        ps, ys = [], []
        for g, win in enumerate(POOL_WINDOWS):
            cols = slice(g * Cg, (g + 1) * Cg)
            s = ext[pl.ds(PHALO, tm), cols]
            for j in range(1, win):
                s = s + ext[pl.ds(PHALO - j, tm), cols]
            cnt = jnp.minimum(t_glob + 1, win).astype(F32)
            p = (s / cnt - ext[pl.ds(PHALO, tm), cols]).astype(MM)
            ps.append(p)
            ys.append(_dot(p, cc[0][g]) + cc[1][:, cols])
        ypre = jnp.concatenate(ys, axis=1)
        y = ypre * cc[2][...]
        oo[0][...] = jnp.concatenate(ps, axis=1)
        oo[1][...] = ypre
        oo[2][...] = y
        oo[3][...] = rr[2][...] + cc[4][...] * _rms(y, cc[3][...])

    return _rows(name, body, T, tm, [(h, 'cur'), (h, 'prev'), (x, 'cur')], [w, b, scale, g1, gt],
                 [(D, MM), (D, F32), (D, F32), (D, F32)], scratch=[pltpu.VMEM((tm + PHALO, D), F32)])


def _pool_bwd1(name, dxo, y, g1, gt, ypre, scale, w):
    T, D = ypre.shape
    G = len(POOL_WINDOWS)
    Cg = D // G

    def body(i, n, rr, cc, oo, aa, ss):
        dyv, dg1, dgt = _post_bwd_math(rr[0][...], rr[1][...], cc[2][...], cc[3][...])
        aa[2][...] += dg1
        aa[3][...] += dgt
        aa[0][...] += jnp.sum(dyv * rr[2][...], axis=0, keepdims=True)
        dypre = dyv * cc[0][...]
        aa[1][...] += jnp.sum(dypre, axis=0, keepdims=True)
        dypre = dypre.astype(MM)
        oo[1][...] = dypre
        oo[0][...] = jnp.concatenate([_dot_nt(dypre[:, g * Cg:(g + 1) * Cg], cc[1][g]) for g in range(G)], axis=1)

    return _rows(name, body, T, 256, [(dxo, 'cur'), (y, 'cur'), (ypre, 'cur')], [scale, w, g1, gt], [(D, F32), (D, MM)],
                 accs=[(1, D)] * 4)


def _pool_bwd2(name, dp, x, dxo, g0, sc):
    T, D = x.shape
    G = len(POOL_WINDOWS)
    Cg = D // G
    tm = min(256, T)

    def body(i, n, rr, cc, oo, aa, ss):
        ext = ss[0]
        t_glob = i * tm + lax.broadcasted_iota(jnp.int32, (tm, 1), 0)
        dcur = rr[0][...]
        dhs = []
        for g, win in enumerate(POOL_WINDOWS):
            cols = slice(g * Cg, (g + 1) * Cg)
            cnt = jnp.minimum(t_glob + 1, win).astype(F32)
            ext[0:tm, cols] = dcur[:, cols] / cnt
            ext[tm:tm + PHALO, cols] = jnp.where(i < n - 1, rr[1][0:PHALO, cols] * (1.0 / win), 0.0)
        for g, win in enumerate(POOL_WINDOWS):
            cols = slice(g * Cg, (g + 1) * Cg)
            s = ext[pl.ds(0, tm), cols]
            for j in range(1, win):
                s = s + ext[pl.ds(j, tm), cols]
            dhs.append(s - dcur[:, cols])
        dx, dsh, dsc, dg0 = _prenorm_bwd(rr[2][...], cc[0][...], cc[1][...], jnp.concatenate(dhs, axis=1))
        oo[0][...] = rr[3][...] + dx
        aa[0][...] += dsh
        aa[1][...] += dsc
        aa[2][...] += dg0

    return _rows(name, body, T, tm, [(dp, 'cur'), (dp, 'next'), (x, 'cur'), (dxo, 'cur')], [g0, sc], [(D, F32)],
                 accs=[(1, D)] * 3, scratch=[pltpu.VMEM((tm + PHALO, D), F32)])


def _loss_head(x, tgt):
    T, D = x.shape

    def body(i, n, rr, cc, oo, aa, ss):
        err = rr[0][...] - rr[1][...]
        oo[0][...] = err * (1.0 / D)
        aa[0][...] += jnp.sum(err * err, axis=0, keepdims=True)

        @pl.when(i == n - 1)
        def _():
            aa[1][...] = jnp.broadcast_to(jnp.sum(aa[0][...], axis=1, keepdims=True) * (0.5 / D), (1, 128))

    dx, _, loss_row = _rows("loss_head", body, T, 512, [(x, 'cur'), (tgt, 'cur')], [], [(D, F32)], accs=[(1, D), (1, 128)])
    return dx, loss_row


def _adamw(name, w, g, m, v):
    shape = w.shape
    C = shape[-1]
    R = w.size // C
    w2, g2, m2, v2 = (t.reshape(R, C) for t in (w, g, m, v))
    br = R
    if R * C * 4 > (1 << 20):
        br = 8
        while br * 2 * C * 4 <= (1 << 20) and R % (br * 2) == 0:
            br *= 2
    b1c = 1.0 - ADAM_B1 ** ADAM_STEP
    b2c = 1.0 - ADAM_B2 ** ADAM_STEP

    def body(w_ref, g_ref, m_ref, v_ref, d_ref, mo_ref, vo_ref):
        gv = g_ref[...]
        mn = ADAM_B1 * m_ref[...] + (1.0 - ADAM_B1) * gv
        vn = ADAM_B2 * v_ref[...] + (1.0 - ADAM_B2) * (gv * gv)
        d_ref[...] = -ADAM_LR * ((mn / b1c) / (jnp.sqrt(vn / b2c) + ADAM_EPS) + ADAM_WD * w_ref[...])
        mo_ref[...] = mn
        vo_ref[...] = vn

    spec = pl.BlockSpec((br, C), lambda r: (r, 0))
    outs = pl.pallas_call(body, grid=(R // br,), in_specs=[spec] * 4, out_specs=[spec] * 3,
                          out_shape=[jax.ShapeDtypeStruct((R, C), F32)] * 3, name=name,
                          compiler_params=_cparams(("arbitrary",)))(w2, g2, m2, v2)
    return tuple(t.reshape(shape) for t in outs)


def _layer_shards(g, ax):
    s = g.shape
    r = g.reshape(s[:ax] + (N_DEV, s[ax] // N_DEV) + s[ax + 1:])
    return (jnp.moveaxis(r, ax, 0) if ax else r).reshape(N_DEV, -1)


def _unshard(g, ax):
    r = jnp.moveaxis(g, 0, ax)
    s = r.shape
    return r.reshape(s[:ax] + (s[ax] * s[ax + 1],) + s[ax + 2:])


def _pack(parts, dtype, row_mult):
    lead = parts[0].shape[:-1]
    flat = jnp.concatenate([p.astype(dtype) for p in parts], axis=-1)
    n = flat.shape[-1]
    per = row_mult * 1024
    tot = -(-n // per) * per
    flat = jnp.pad(flat, [(0, 0)] * len(lead) + [(0, tot - n)])
    return flat.reshape(lead + (tot // 1024, 1024))


def _pad_heads(w, lo, hi):
    K = w.shape[0]
    r = w.reshape(K, N_HEADS, -1)[:, :, lo:hi]
    return jnp.pad(r, ((0, 0), (0, 0), (0, HEAD_PAD - (hi - lo)))).reshape(K, N_HEADS * HEAD_PAD)


def kernel(x, c, positions, ada_w, ada_b, norm_g, mla_w_dq, mla_q_norm_g, mla_w_uq, mla_w_dkv, mla_kv_norm_g, mla_w_ukv, mla_w_o, conv_w_pw1, conv_b_pw1, conv_w_dw, conv_b_dw, conv_ln_g, conv_ln_b, conv_w_pw2, conv_b_pw2, pool_w, pool_b, pool_scale, ffn_w1, ffn_w2, loss_target, m_ada_w, m_ada_b, m_norm_g, m_mla_w_dq, m_mla_q_norm_g, m_mla_w_uq, m_mla_w_dkv, m_mla_kv_norm_g, m_mla_w_ukv, m_mla_w_o, m_conv_w_pw1, m_conv_b_pw1, m_conv_w_dw, m_conv_b_dw, m_conv_ln_g, m_conv_ln_b, m_conv_w_pw2, m_conv_b_pw2, m_pool_w, m_pool_b, m_pool_scale, m_ffn_w1, m_ffn_w2, v_ada_w, v_ada_b, v_norm_g, v_mla_w_dq, v_mla_q_norm_g, v_mla_w_uq, v_mla_w_dkv, v_mla_kv_norm_g, v_mla_w_ukv, v_mla_w_o, v_conv_w_pw1, v_conv_b_pw1, v_conv_w_dw, v_conv_b_dw, v_conv_ln_g, v_conv_ln_b, v_conv_w_pw2, v_conv_b_pw2, v_pool_w, v_pool_b, v_pool_scale, v_ffn_w1, v_ffn_w2):
    args = dict(locals())
    W = {n: args[n] for n, _ in WEIGHTS}
    M1 = {n: args['m_' + n] for n, _ in WEIGHTS}
    V2 = {n: args['v_' + n] for n, _ in WEIGHTS}
    D = D_MODEL
    T = x.shape[1]
    L = ffn_w1.shape[0]
    xi, yi, ci = _place()
    me = 4 * xi + 2 * yi + ci
    n_ada = ada_w.shape[2]

    small_sizes = [W[n].size for n in SMALL]
    small_in = _pack([c.reshape(-1)] + [W[n].reshape(-1) for n in SMALL], F32, 8)
    small_all = _ag_small("ag_small_params", small_in).reshape(N_DEV, -1)
    c_all = small_all[:, :D]
    Ws = {}
    off = D
    for n, sz in zip(SMALL, small_sizes):
        Ws[n] = _unshard(small_all[:, off:off + sz].reshape((N_DEV,) + W[n].shape), SHARD_AXIS[n])
        off += sz
    c16 = jnp.pad(c_all, ((0, 16 - N_DEV), (0, 0)))

    ada_b_cols = lax.dynamic_slice_in_dim(ada_b, me * n_ada, n_ada, axis=1).reshape(L, 1, n_ada)
    mod_part = _mod_part(c16, ada_w, ada_b_cols)[:, :N_DEV]
    mod_all = _ag_small("ag_mod", mod_part.reshape(L * N_DEV, n_ada)).reshape(N_DEV, L, N_DEV, n_ada)
    mod_mine = lax.dynamic_index_in_dim(mod_all, me, axis=2, keepdims=False)
    mod = jnp.transpose(mod_mine, (1, 0, 2)).reshape(L, 6, 1, D)

    mla_names = [n for n in BIG if n.startswith('mla')]
    first_items = [(n, W[n][0]) for n in mla_names]
    later_items = [(n, W[n][1:]) for n in mla_names] + [(n, W[n]) for n in BIG if not n.startswith(('mla', 'ffn'))]
    first_all, = _ag_big("ag_weights", [_pack([a.reshape(-1) for _, a in first_items], MM, 32)])
    wf = [ffn_w1.astype(MM), ffn_w2.astype(MM), _pack([a.reshape(-1) for _, a in later_items], MM, 32)]
    wf, first_all, mod = lax.optimization_barrier((wf, first_all, mod))
    wf_land = [lax.dynamic_update_slice(lax.empty((N_DEV,) + w.shape, MM), w[None], (me,) + (0,) * w.ndim) for w in wf]
    ag_sems, wf_thru, wf_land, ag_token = _copies_start("ag_ffn_start", wf, wf_land, FIRST_LEVEL_PEERS, False)

    def unpack(g, items, dropped):
        flat, out, off = g.reshape(N_DEV, -1), {}, 0
        for n, a in items:
            out[n] = _unshard(flat[:, off:off + a.size].reshape((N_DEV,) + a.shape), SHARD_AXIS[n] - dropped)
            off += a.size
        return out

    n_mla = mla_w_dq.shape[0]
    w_dq, w_uq_p, w_ukv_k, w_ukv_v, w_dkv_p, w_o_p = ([None] * n_mla for _ in range(6))

    def set_mla(j, w):
        w_dq[j] = w['mla_w_dq']
        w_uq_p[j] = _pad_heads(w['mla_w_uq'], 0, QK_NOPE + QK_ROPE)
        w_ukv_k[j] = _pad_heads(w['mla_w_ukv'], 0, QK_NOPE)
        w_ukv_v[j] = _pad_heads(w['mla_w_ukv'], QK_NOPE, QK_NOPE + V_HEAD)
        w_dkv_p[j] = jnp.pad(jnp.concatenate([w['mla_w_dkv'][:, :KV_LORA], jnp.zeros((D, QK_NOPE), MM), w['mla_w_dkv'][:, KV_LORA:]], axis=1),
                             ((0, 0), (0, HEAD_PAD - QK_NOPE - QK_ROPE)))
        w_o_p[j] = jnp.pad(w['mla_w_o'].reshape(N_HEADS, V_HEAD, D), ((0, 0), (0, HEAD_PAD - V_HEAD), (0, 0))).reshape(N_HEADS * HEAD_PAD, D)

    set_mla(0, unpack(first_all, first_items, 1))
    w_dw32 = jnp.pad(Ws['conv_w_dw'], ((0, 0), (0, 32 - CONV_W), (0, 0)))
    row = lambda t: t.reshape(1, -1)

    half = QK_ROPE // 2
    inv_freq = ROPE_THETA ** (-jnp.arange(0, QK_ROPE, 2, dtype=F32) / QK_ROPE)
    invf = jnp.zeros((1, HEAD_PAD), F32).at[0, QK_NOPE:QK_NOPE + half].set(inv_freq).at[0, QK_NOPE + half:QK_NOPE + QK_ROPE].set(inv_freq)
    rC, rS1, rS2 = _rope_tables(positions.reshape(T, 1).astype(F32), invf)

    xs = x.reshape(T, D)
    saved = []
    for i in range(L):
        kind, j = i % 3, i // 3
        sh_m, sc_m, gt_m, sh_f, sc_f, gt_f = (mod[i, r] for r in range(6))
        g = [row(Ws['norm_g'][i, r]) for r in range(4)]
        st = dict(x0=xs)
        if i == 0:
            sc_m = sc_m + ag_token[0:1, 0:1]
        if kind == 0:
            cq_raw, cq, ckv_raw, ckv, q, k, v, h = _mla_proj(f"mla_proj{i}", xs, g[0], sc_m, sh_m, rC, rS1, rS2, w_dq[j],
                                                             row(Ws['mla_q_norm_g'][j]), w_uq_p[j], w_dkv_p[j],
                                                             row(Ws['mla_kv_norm_g'][j]), w_ukv_k[j], w_ukv_v[j])
            o, lse = _attn_fwd(f"attn_fwd{i}", q, k, v)
            y, xs = _mm_post(f"mla_out{i}", o, w_o_p[j], None, xs, g[1], gt_m)
            st.update(h=h, cq_raw=cq_raw, cq=cq, ckv_raw=ckv_raw, ckv=ckv, q=q, k=k, v=v, o=o, lse=lse, y=y)
        elif kind == 1:
            a, u, h = _conv_glu(f"conv_glu{i}", xs, g[0], sc_m, sh_m, w_pw1[j], row(W['conv_b_pw1'][j]))
            uc, z, y, xs = _conv_dw(f"conv_dw{i}", u, w_dw32[j], row(W['conv_b_dw'][j]), row(W['conv_ln_g'][j]), row(W['conv_ln_b'][j]),
                                    w_pw2[j], row(W['conv_b_pw2'][j]), xs, g[1], gt_m)
            st.update(h=h, a=a, u=u, uc=uc, z=z, y=y)
        else:
            h = _prenorm(f"prenorm_m{i}", xs, g[0], sc_m, sh_m, F32)
            p, ypre, y, xs = _pool_fwd(f"pool_fwd{i}", h, w_pool[j], row(Ws['pool_b'][j]), row(Ws['pool_scale'][j]), xs, g[1], gt_m)
            st.update(p=p, ypre=ypre, y=y)
        st['x1'] = xs
        if i == 0:
            wg = _copies_wait("ag_ffn_wait", ag_sems, wf_thru, wf_land, xs, FIRST_LEVEL_PEERS, False)
            w1g, w2g, later_all = _ag_forward("ag_ffn_forward", wg)
            later = unpack(later_all, later_items, 0)
            for jj in range(1, n_mla):
                set_mla(jj, {n: later[n][jj - 1] for n in mla_names})
            w_pw1, w_pw2, w_pool = later['conv_w_pw1'], later['conv_w_pw2'], later['pool_w']
        hf, af, yf, xs = _ffn_fwd(f"ffn_fwd{i}", i, xs, g[2], sc_f, sh_f, w1g, w2g, g[3], gt_f)
        st.update(hf=hf, af=af, yf=yf)
        saved.append(st)

    dx, loss_row = _loss_head(xs, loss_target.reshape(T, D))

    G = {}
    dmod = [None] * L
    dnorm = [None] * L
    rs_pending = None
    ffn_red = [lax.empty(ffn_w1.shape, F32), lax.empty(ffn_w2.shape, F32)]
    for i in reversed(range(L)):
        kind, j = i % 3, i // 3
        sh_m, sc_m, gt_m, sh_f, sc_f, gt_f = (mod[i, r] for r in range(6))
        g = [row(Ws['norm_g'][i, r]) for r in range(4)]
        st = saved[i]
        dy, da, dx, dg3, dgt_f, dsh_f, dsc_f, dg2 = _ffn_bwd(f"ffn_bwd{i}", i, st['yf'], g[3], gt_f, st['af'], w1g, w2g, st['x1'], dx, g[2], sc_f)
        wire1, own1 = _mm_tn_wire(f"ffn_dw1_{i}", st['hf'], da, me, False, False)
        wire2, own2 = _mm_tn_wire(f"ffn_dw2_{i}", st['af'], dy, me, True, True)
        if rs_pending is not None:
            ffn_red = _rs_finish(rs_pending, wire2, me, ffn_red)
        wires = [wire1, wire2]
        rs_sems, wires_thru, rs_lands, rs_token = _copies_start(f"rs_start{i}", wires, [lax.empty(w.shape, MM) for w in wires], ALL_PEERS, True)
        rs_pending = (i, rs_sems, wires_thru, rs_lands, [own1, own2])
        gt_m = gt_m + rs_token[0:1, 0:1]
        if kind == 0:
            dy, do, dg1, dgt_m = _post_bwd_nt(f"mla_do{i}", dx, st['y'], g[1], gt_m, w_o_p[j])
            delta = _attn_delta(f"attn_delta{i}", do, st['o'])
            dq, dk, dv = _attn_bwd(f"attn_bwd{i}", st['q'], st['k'], st['v'], do, st['lse'], delta)
            dq_pre, dcq_raw, dckv_all, dx, dqg, dkvg, dsh_m, dsc_m, dg0 = _mla_proj_bwd(
                f"mla_proj_bwd{i}", dq, dk, dv, rC, rS1, rS2, st['cq_raw'], st['ckv_raw'], st['x0'], dx, w_uq_p[j], w_ukv_k[j], w_ukv_v[j],
                w_dq[j], w_dkv_p[j], row(Ws['mla_q_norm_g'][j]), row(Ws['mla_kv_norm_g'][j]), g[0], sc_m)
            dwo = _mm_tn(f"mla_dwo{i}", st['o'], dy)
            dwuq = _mm_tn(f"mla_dwuq{i}", st['cq'], dq_pre)
            dwk = _mm_tn(f"mla_dwukvk{i}", st['ckv'], dk)
            dwv = _mm_tn(f"mla_dwukvv{i}", st['ckv'], dv)
            dwdq = _mm_tn(f"mla_dwdq{i}", st['h'], dcq_raw)
            dwdkv = _mm_tn(f"mla_dwdkv{i}", st['h'], dckv_all)
            G.setdefault('mla_w_o', [None] * n_mla)[j] = dwo.reshape(N_HEADS, HEAD_PAD, D)[:, :V_HEAD].reshape(N_HEADS * V_HEAD, D)
            G.setdefault('mla_w_uq', [None] * n_mla)[j] = dwuq.reshape(Q_LORA, N_HEADS, HEAD_PAD)[:, :, :QK_NOPE + QK_ROPE].reshape(Q_LORA, -1)
            G.setdefault('mla_w_ukv', [None] * n_mla)[j] = jnp.concatenate(
                [dwk.reshape(KV_LORA, N_HEADS, HEAD_PAD)[:, :, :QK_NOPE], dwv.reshape(KV_LORA, N_HEADS, HEAD_PAD)[:, :, :V_HEAD]], axis=2).reshape(KV_LORA, -1)
            G.setdefault('mla_w_dq', [None] * n_mla)[j] = dwdq
            G.setdefault('mla_w_dkv', [None] * n_mla)[j] = jnp.concatenate([dwdkv[:, :KV_LORA], dwdkv[:, KV_LORA + QK_NOPE:KV_LORA + QK_NOPE + QK_ROPE]], axis=1)
            G.setdefault('mla_q_norm_g', [None] * n_mla)[j] = dqg[0]
            G.setdefault('mla_kv_norm_g', [None] * n_mla)[j] = dkvg[0]
        elif kind == 1:
            duc, dy, dlng, dlnb, dbdw, dg1, dgt_m, dysum = _conv_bwd1(f"conv_bwd1_{i}", dx, st['y'], g[1], gt_m, st['uc'], w_pw2[j],
                                                                      row(W['conv_ln_g'][j]), row(W['conv_ln_b'][j]))
            da, dx, dwdw, dbpw1, dsh_m, dsc_m, dg0 = _conv_bwd2(f"conv_bwd2_{i}", duc, st['u'], st['a'], st['x0'], dx, w_dw32[j], w_pw1[j], g[0], sc_m)
            G['conv_w_pw2'] = [_mm_tn(f"conv_dwpw2_{i}", st['z'], dy)]
            G['conv_w_pw1'] = [_mm_tn(f"conv_dwpw1_{i}", st['h'], da)]
            G['conv_w_dw'] = [dwdw[:CONV_W]]
            G['conv_b_pw1'], G['conv_b_dw'], G['conv_ln_g'], G['conv_ln_b'], G['conv_b_pw2'] = [dbpw1[0]], [dbdw[0]], [dlng[0]], [dlnb[0]], [dysum[0]]
        else:
            dp, dypre, dscale, dpb, dg1, dgt_m = _pool_bwd1(f"pool_bwd1_{i}", dx, st['y'], g[1], gt_m, st['ypre'], row(Ws['pool_scale'][j]), w_pool[j])
            dx, dsh_m, dsc_m, dg0 = _pool_bwd2(f"pool_bwd2_{i}", dp, st['x0'], dx, g[0], sc_m)
            G['pool_w'] = [_mm_tn(f"pool_dw{i}", st['p'], dypre, diag=len(POOL_WINDOWS))]
            G['pool_b'] = [dpb.reshape(len(POOL_WINDOWS), -1)]
            G['pool_scale'] = [dscale[0]]
        dmod[i] = jnp.concatenate([dsh_m, dsc_m, dgt_m, dsh_f, dsc_f, dgt_f], axis=1)
        dnorm[i] = jnp.concatenate([dg0, dg1, dg2, dg3], axis=0)
    G['norm_g'] = dnorm
    grad_x = dx.reshape(x.shape)

    rs_names = [n for n, ax in WEIGHTS if ax is not None and n != 'ada_w' and not n.startswith('ffn')]
    pieces = [(n, _layer_shards(g, SHARD_AXIS[n] - 1)) for n in rs_names for g in G[n]]
    big = [(n, p) for n, p in pieces if p.shape[1] % (8 * 1024) == 0]
    small = [(n, p) for n, p in pieces if p.shape[1] % (8 * 1024) != 0]
    packed = jnp.concatenate([p.reshape(N_DEV, -1, 1024) for _, p in big] + [_pack([p for _, p in small], F32, 8)], axis=1)
    ffn_red = _rs_finish(rs_pending, dx, me, ffn_red)
    red = _reduce_scatter("rs", [packed], ci, 2 * xi + yi)[0]
    grads = {'ffn_w1': ffn_red[0], 'ffn_w2': ffn_red[1]}
    got = {}
    row0 = 0
    for n, p in big:
        rows = p.shape[1] // 1024
        got.setdefault(n, []).append(red[row0:row0 + rows])
        row0 += rows
    tail = red[row0:].reshape(-1)
    off = 0
    for n, p in small:
        got.setdefault(n, []).append(tail[off:off + p.shape[1]])
        off += p.shape[1]
    for n in rs_names:
        grads[n] = jnp.stack([g_.reshape(W[n].shape[1:]) for g_ in got[n]], axis=0)

    dmod_mine = jnp.concatenate(dmod, axis=1).reshape(-1)
    fin_in = _pack([dmod_mine] + [G[n][0].reshape(-1) for n in REPL] + [loss_row.reshape(-1)], F32, 8)
    fin_all = _ag_small("ag_final", fin_in)
    fin_sum = _sum_devices("final_sum", fin_all).reshape(-1)
    nm = L * 6 * D
    grads['ada_b'] = fin_sum[:nm].reshape(L, 6 * D)
    off = nm
    for n in REPL:
        grads[n] = fin_sum[off:off + W[n].size].reshape(W[n].shape)
        off += W[n].size
    loss = fin_sum[off]
    dmod_all = fin_all.reshape(N_DEV, -1)[:, :nm].reshape(N_DEV, L, 6 * D)
    dmod_cols = lax.dynamic_slice_in_dim(dmod_all, me * n_ada, n_ada, axis=2)
    dmod16 = jnp.pad(jnp.transpose(dmod_cols, (1, 0, 2)), ((0, 0), (0, 16 - N_DEV), (0, 0)))
    grads['ada_w'] = _ada_w_grad(c16, dmod16)

    deltas, new_m, new_v = {}, {}, {}
    for n, _ in WEIGHTS:
        deltas[n], new_m[n], new_v[n] = _adamw("adamw_" + n, W[n], grads[n], M1[n], V2[n])
    names = [n for n, _ in WEIGHTS]
    return (loss, grad_x, *[grads[n] for n in names], *[deltas[n] for n in names], *[new_m[n] for n in names],
            *[new_v[n] for n in names])
```

```python
import functools
import math

import jax
import jax.numpy as jnp
from jax import lax
from jax.experimental import pallas as pl
from jax.experimental.pallas import tpu as pltpu

F32 = jnp.float32
MM = jnp.bfloat16
EPS = 1e-6
NEG = -1e30
N_DEV = 8
VMEM_LIMIT = 48 * 1024 * 1024
MESH = pl.DeviceIdType.MESH

D_MODEL = 1024
N_HEADS = 16
HEAD_PAD = 128
QK_NOPE, QK_ROPE, V_HEAD = 64, 32, 64
Q_LORA, KV_LORA = 384, 256
CHUNK = 64
CONV_W = 31
POOL_WINDOWS = (2, 4, 8, 16)
ROPE_THETA = 10000.0
ATT_SCALE = 1.0 / math.sqrt(QK_NOPE + QK_ROPE)

ADAM_LR, ADAM_B1, ADAM_B2, ADAM_EPS, ADAM_WD, ADAM_STEP = 0.001, 0.9, 0.999, 1e-08, 0.01, 10

WEIGHTS = [('ada_w', 2), ('ada_b', None), ('norm_g', 2), ('mla_w_dq', 1), ('mla_q_norm_g', 1), ('mla_w_uq', 2),
           ('mla_w_dkv', 1), ('mla_kv_norm_g', 1), ('mla_w_ukv', 2), ('mla_w_o', 1), ('conv_w_pw1', 2),
           ('conv_b_pw1', None), ('conv_w_dw', 2), ('conv_b_dw', None), ('conv_ln_g', None), ('conv_ln_b', None),
           ('conv_w_pw2', 1), ('conv_b_pw2', None), ('pool_w', 2), ('pool_b', 2), ('pool_scale', 1),
           ('ffn_w1', 2), ('ffn_w2', 1)]
SHARD_AXIS = dict(WEIGHTS)
BIG = ['mla_w_dq', 'mla_w_uq', 'mla_w_dkv', 'mla_w_ukv', 'mla_w_o', 'conv_w_pw1', 'conv_w_pw2', 'pool_w', 'ffn_w1', 'ffn_w2']
SMALL = ['norm_g', 'mla_q_norm_g', 'mla_kv_norm_g', 'conv_w_dw', 'pool_b', 'pool_scale']
REPL = ['conv_b_pw1', 'conv_b_dw', 'conv_ln_g', 'conv_ln_b', 'conv_b_pw2']


def _dot(a, b):
    return jnp.dot(a.astype(MM), b.astype(MM), preferred_element_type=F32)


def _dot_nt(a, b):
    return lax.dot_general(a.astype(MM), b.astype(MM), (((1,), (1,)), ((), ())), preferred_element_type=F32)


def _dot_tn(a, b):
    return lax.dot_general(a.astype(MM), b.astype(MM), (((0,), (0,)), ((), ())), preferred_element_type=F32)


def _sigmoid(x):
    return 1.0 / (1.0 + jnp.exp(-x))


def _rstd(x):
    return lax.rsqrt(jnp.mean(x * x, axis=-1, keepdims=True) + EPS)


def _rms(x, g):
    return x * _rstd(x) * g


def _rms_bwd(x, g, dout):
    r = _rstd(x)
    xn = x * r
    dg = jnp.sum(dout * xn, axis=0, keepdims=True)
    dxn = dout * g
    dx = r * (dxn - xn * jnp.mean(dxn * xn, axis=-1, keepdims=True))
    return dx, dg


def _prenorm_bwd(x, g0, sc, dh):
    r = _rstd(x)
    xn = x * r
    dsh = jnp.sum(dh, axis=0, keepdims=True)
    dsc = jnp.sum(dh * (xn * g0), axis=0, keepdims=True)
    dn = dh * (1.0 + sc)
    dg0 = jnp.sum(dn * xn, axis=0, keepdims=True)
    dxn = dn * g0
    dx = r * (dxn - xn * jnp.mean(dxn * xn, axis=-1, keepdims=True))
    return dx, dsh, dsc, dg0


def _cparams(sem):
    return pltpu.CompilerParams(dimension_semantics=sem, vmem_limit_bytes=VMEM_LIMIT)


def _rows(name, body, n_rows, tm, rows, consts, outs, accs=(), scratch=()):
    tm = min(tm, n_rows)
    nblk = n_rows // tm
    nr, nc, no, na = len(rows), len(consts), len(outs), len(accs)
    in_specs, args = [], []
    for a, kind in rows:
        if kind == 'cur':
            im = lambda i: (i, 0)
        elif kind == 'prev':
            im = lambda i: (jnp.maximum(i - 1, 0), 0)
        else:
            im = lambda i: (jnp.minimum(i + 1, nblk - 1), 0)
        in_specs.append(pl.BlockSpec((tm, a.shape[1]), im))
        args.append(a)
    for a in consts:
        in_specs.append(pl.BlockSpec(a.shape, lambda i, nd=a.ndim: (0,) * nd))
        args.append(a)
    out_specs = [pl.BlockSpec((tm, c), lambda i: (i, 0)) for c, _ in outs]
    out_specs += [pl.BlockSpec(s, lambda i, nd=len(s): (0,) * nd) for s in accs]
    out_shape = [jax.ShapeDtypeStruct((n_rows, c), dt) for c, dt in outs]
    out_shape += [jax.ShapeDtypeStruct(s, F32) for s in accs]

    def kern(*refs):
        i = pl.program_id(0)
        rr = refs[:nr]
        cc = refs[nr:nr + nc]
        oo = refs[nr + nc:nr + nc + no]
        aa = refs[nr + nc + no:nr + nc + no + na]
        ss = refs[nr + nc + no + na:]

        @pl.when(i == 0)
        def _():
            for a in aa:
                a[...] = jnp.zeros(a.shape, F32)

        body(i, nblk, rr, cc, oo, aa, ss)

    return pl.pallas_call(kern, grid=(nblk,), in_specs=in_specs, out_specs=out_specs, out_shape=out_shape,
                          scratch_shapes=list(scratch), name=name, compiler_params=_cparams(("arbitrary",)))(*args)


def _place():
    return lax.axis_index("x"), lax.axis_index("y"), lax.axis_index("c")


def _ag_small(name, xs):
    R, C = xs.shape

    def body(x_ref, out_ref, send_sems, recv_sems):
        x, y, c = _place()
        me = 4 * x + 2 * y + c
        out_ref[me] = x_ref[...]
        copies = []
        for k in range(1, N_DEV):
            peer = ((1 - x) if k & 4 else x, (1 - y) if k & 2 else y, (1 - c) if k & 1 else c)
            cp = pltpu.make_async_remote_copy(src_ref=x_ref, dst_ref=out_ref.at[me], send_sem=send_sems.at[k - 1],
                                              recv_sem=recv_sems.at[k - 1], device_id=peer, device_id_type=MESH)
            cp.start()
            copies.append(cp)
        for cp in copies:
            cp.wait()

    return pl.pallas_call(
        body, out_shape=jax.ShapeDtypeStruct((N_DEV, R, C), xs.dtype),
        in_specs=[pl.BlockSpec(memory_space=pltpu.VMEM)], out_specs=pl.BlockSpec(memory_space=pltpu.VMEM),
        scratch_shapes=[pltpu.SemaphoreType.DMA((N_DEV - 1,)), pltpu.SemaphoreType.DMA((N_DEV - 1,))], name=name)(xs)


def _ag_big(name, xs):
    nt = len(xs)

    def body(*refs):
        x_refs, out_refs = refs[:nt], refs[nt:2 * nt]
        send_sems, recv_sems, local_sems = refs[2 * nt:]
        x, y, c = _place()
        me, sibling = (x, y, c), (x, y, 1 - c)
        chips = [(1 - x, y), (x, 1 - y), (1 - x, 1 - y)]

        def copy(t, k, block, to, own=False):
            px, py, pc = block
            rows = out_refs[t].at[4 * px + 2 * py + pc]
            return pltpu.make_async_remote_copy(src_ref=x_refs[t] if own else rows, dst_ref=rows, send_sem=send_sems.at[7 * t + k],
                                                recv_sem=recv_sems.at[7 * t + k], device_id=to, device_id_type=MESH)

        mine = [pltpu.make_async_copy(x_refs[t], out_refs[t].at[4 * x + 2 * y + c], local_sems.at[t]) for t in range(nt)]
        for cp in mine:
            cp.start()
        first = []
        for t in range(nt):
            first.append(copy(t, 0, me, sibling, own=True))
            first += [copy(t, 1 + j, me, (*chip, c), own=True) for j, chip in enumerate(chips)]
        for cp in first:
            cp.start()
        passed = []
        for t in range(nt):
            for j, chip in enumerate(chips):
                copy(t, 1 + j, (*chip, c), me).wait_recv()
                cp = copy(t, 4 + j, (*chip, c), sibling)
                cp.start()
                passed.append(cp)
        for t in range(nt):
            copy(t, 0, sibling, me).wait_recv()
            for j, chip in enumerate(chips):
                copy(t, 4 + j, (*chip, 1 - c), me).wait_recv()
        for cp in first + passed:
            cp.wait_send()
        for cp in mine:
            cp.wait()

    hbm = pl.BlockSpec(memory_space=pl.ANY)
    return pl.pallas_call(
        body, out_shape=[jax.ShapeDtypeStruct((N_DEV,) + t.shape, t.dtype) for t in xs],
        in_specs=[hbm] * nt, out_specs=[hbm] * nt,
        scratch_shapes=[pltpu.SemaphoreType.DMA((7 * nt,)), pltpu.SemaphoreType.DMA((7 * nt,)), pltpu.SemaphoreType.DMA((nt,))],
        name=name)(*xs)


def _rs_pair(name, ps):
    nt = len(ps)

    def body(*refs):
        p_refs, recv_refs = refs[:nt], refs[nt:2 * nt]
        send_sems, recv_sems = refs[2 * nt:]
        x, y, c = _place()
        copies = []
        for t in range(nt):
            for j in range(4):
                cp = pltpu.make_async_remote_copy(src_ref=p_refs[t].at[j, 1 - c], dst_ref=recv_refs[t].at[j], send_sem=send_sems.at[4 * t + j],
                                                  recv_sem=recv_sems.at[4 * t + j], device_id=(x, y, 1 - c), device_id_type=MESH)
                cp.start()
                copies.append(cp)
        for cp in copies:
            cp.wait()

    hbm = pl.BlockSpec(memory_space=pl.ANY)
    return pl.pallas_call(
        body, out_shape=[jax.ShapeDtypeStruct((4,) + p.shape[2:], p.dtype) for p in ps], in_specs=[hbm] * nt, out_specs=[hbm] * nt,
        scratch_shapes=[pltpu.SemaphoreType.DMA((4 * nt,)), pltpu.SemaphoreType.DMA((4 * nt,))], name=name)(*ps)


def _rs_chips(name, ss):
    nt = len(ss)

    def body(*refs):
        s_refs, recv_refs = refs[:nt], refs[nt:2 * nt]
        send_sems, recv_sems, local_sems = refs[2 * nt:]
        x, y, c = _place()
        mine = 2 * x + y
        owns = [pltpu.make_async_copy(s_refs[t].at[mine], recv_refs[t].at[mine], local_sems.at[t]) for t in range(nt)]
        for cp in owns:
            cp.start()
        copies = []
        for t in range(nt):
            for k in range(1, 4):
                px = (1 - x) if k & 2 else x
                py = (1 - y) if k & 1 else y
                cp = pltpu.make_async_remote_copy(src_ref=s_refs[t].at[2 * px + py], dst_ref=recv_refs[t].at[mine],
                                                  send_sem=send_sems.at[3 * t + k - 1], recv_sem=recv_sems.at[3 * t + k - 1],
                                                  device_id=(px, py, c), device_id_type=MESH)
                cp.start()
                copies.append(cp)
        for cp in copies:
            cp.wait()
        for cp in owns:
            cp.wait()

    hbm = pl.BlockSpec(memory_space=pl.ANY)
    return pl.pallas_call(
        body, out_shape=[jax.ShapeDtypeStruct(s_.shape, s_.dtype) for s_ in ss], in_specs=[hbm] * nt, out_specs=[hbm] * nt,
        scratch_shapes=[pltpu.SemaphoreType.DMA((3 * nt,)), pltpu.SemaphoreType.DMA((3 * nt,)), pltpu.SemaphoreType.DMA((nt,))],
        name=name)(*ss)


RS_ROWS = 256


def _row_block(r):
    return next(t for t in range(RS_ROWS, 0, -16) if r % t == 0)


def _pair_sum(name, p, recv, my_c, my_chip):
    _, _, r, c = p.shape
    tr = _row_block(r)

    def body(sc_ref, p_ref, r_ref, o_ref, own_ref):
        s = p_ref[...] + r_ref[...]
        o_ref[...] = s.astype(MM)

        @pl.when(pl.program_id(1) == sc_ref[1])
        def _():
            own_ref[...] = s

    return pl.pallas_call(
        body, grid_spec=pltpu.PrefetchScalarGridSpec(
            num_scalar_prefetch=1, grid=(r // tr, 4),
            in_specs=[pl.BlockSpec((None, None, tr, c), lambda i, j, sc: (j, sc[0], i, 0)),
                      pl.BlockSpec((None, tr, c), lambda i, j, sc: (j, i, 0))],
            out_specs=[pl.BlockSpec((None, tr, c), lambda i, j, sc: (j, i, 0)), pl.BlockSpec((tr, c), lambda i, j, sc: (i, 0))]),
        out_shape=[jax.ShapeDtypeStruct((4, r, c), MM), jax.ShapeDtypeStruct((r, c), F32)], name=name,
        compiler_params=_cparams(("arbitrary", "arbitrary")))(jnp.stack([my_c, my_chip]), p, recv)


def _chip_sum(name, own, recv, my_chip):
    _, r, c = recv.shape
    tr = _row_block(r)

    def body(sc_ref, own_ref, r_ref, o_ref):
        acc = jnp.zeros((tr, c), F32)
        for j in range(4):
            acc = acc + jnp.where(sc_ref[0] == j, own_ref[...], r_ref[j].astype(F32))
        o_ref[...] = acc

    return pl.pallas_call(
        body, grid_spec=pltpu.PrefetchScalarGridSpec(
            num_scalar_prefetch=1, grid=(r // tr,),
            in_specs=[pl.BlockSpec((tr, c), lambda i, sc: (i, 0)), pl.BlockSpec((4, tr, c), lambda i, sc: (0, i, 0))],
            out_specs=pl.BlockSpec((tr, c), lambda i, sc: (i, 0))),
        out_shape=jax.ShapeDtypeStruct((r, c), F32), name=name,
        compiler_params=_cparams(("arbitrary",)))(my_chip.reshape(1), own, recv)


def _reduce_scatter(tag, tensors, my_c, my_chip):
    ps = [t.reshape((4, 2) + t.shape[1:]) for t in tensors]
    recv = _rs_pair(tag + "_pair", ps)
    sums = [_pair_sum(f"{tag}_pair_sum{t}", ps[t], recv[t], my_c, my_chip) for t in range(len(ps))]
    recv2 = _rs_chips(tag + "_chips", [s_[0] for s_ in sums])
    return [_chip_sum(f"{tag}_chip_sum{t}", sums[t][1], recv2[t], my_chip) for t in range(len(ps))]


HBM_SPEC = pl.BlockSpec(memory_space=pltpu.HBM)
SEM_SPEC = pl.BlockSpec(memory_space=pltpu.SEMAPHORE)
SPLIT_EFFECT = pltpu.SideEffectType.DATAFLOW_SIDE_EFFECTING
ALL_PEERS = (1, 2, 3, 4, 5, 6, 7)
FIRST_LEVEL_PEERS = (1, 4, 2, 6)
CHIP_PEERS = (4, 2, 6)


def _split_copies(src_refs, land_refs, sems, masks, src_per_peer):
    n, nt = len(masks), len(src_refs)
    x, y, c = _place()
    by_chip = src_per_peer == 'chip'
    slot = 2 * x + y if by_chip else 4 * x + 2 * y + c
    copies = []
    for t in range(nt):
        for k, mask in enumerate(masks):
            px, py, pc = (1 - x) if mask & 4 else x, (1 - y) if mask & 2 else y, (1 - c) if mask & 1 else c
            src = src_refs[t].at[2 * px + py if by_chip else 4 * px + 2 * py + pc] if src_per_peer else src_refs[t]
            copies.append(pltpu.make_async_remote_copy(src_ref=src, dst_ref=land_refs[t].at[slot], send_sem=sems[t * n + k],
                                                       recv_sem=sems[nt * n + t * n + k], device_id=(px, py, pc), device_id_type=MESH))
    return copies


def _copies_start(name, srcs, lands, masks, src_per_peer):
    nt, ns = len(srcs), 2 * len(masks) * len(srcs)

    def body(*refs):
        for cp in _split_copies(refs[:nt], refs[nt:2 * nt], refs[2 * nt:2 * nt + ns], masks, src_per_peer):
            cp.start()
        token = refs[-1]
        token[...] = jnp.zeros(token.shape, F32)

    outs = pl.pallas_call(
        body, name=name,
        out_shape=(pltpu.SemaphoreType.DMA(()),) * ns + tuple(pltpu.HBM(a.shape, a.dtype) for a in list(srcs) + list(lands))
        + (jax.ShapeDtypeStruct((8, 128), F32),),
        in_specs=(HBM_SPEC,) * (2 * nt), out_specs=(SEM_SPEC,) * ns + (HBM_SPEC,) * (2 * nt) + (pl.BlockSpec(memory_space=pltpu.VMEM),),
        input_output_aliases={t: ns + t for t in range(2 * nt)}, compiler_params=pltpu.CompilerParams(has_side_effects=SPLIT_EFFECT))(
            *[pltpu.with_memory_space_constraint(a, pltpu.HBM) for a in list(srcs) + list(lands)])
    return outs[:ns], outs[ns:ns + nt], outs[ns + nt:ns + 2 * nt], outs[-1]


def _copies_wait(name, sems, srcs_thru, lands_thru, after, masks, src_per_peer):
    nt, ns = len(srcs_thru), len(sems)

    def body(*refs):
        for cp in _split_copies(refs[:nt], refs[nt:2 * nt], refs[2 * nt:2 * nt + ns], masks, src_per_peer):
            cp.wait_send()
            cp.wait_recv()

    thru = list(srcs_thru) + list(lands_thru)
    return pl.pallas_call(
        body, name=name, out_shape=tuple(pltpu.HBM(a.shape, a.dtype) for a in thru),
        in_specs=(HBM_SPEC,) * (2 * nt) + (SEM_SPEC,) * ns + (pl.BlockSpec(memory_space=pl.ANY),), out_specs=(HBM_SPEC,) * (2 * nt),
        input_output_aliases={t: t for t in range(2 * nt)}, compiler_params=pltpu.CompilerParams(has_side_effects=SPLIT_EFFECT))(
            *thru, *sems, after)[nt:]


def _ag_forward(name, gs):
    nt = len(gs)

    def body(*refs):
        o_refs, send_sems, recv_sems = refs[nt:2 * nt], refs[2 * nt], refs[2 * nt + 1]
        x, y, c = _place()
        chips = [(1 - x, y), (x, 1 - y), (1 - x, 1 - y)]

        def copy(t, j, pc):
            rows = o_refs[t].at[4 * chips[j][0] + 2 * chips[j][1] + pc]
            return pltpu.make_async_remote_copy(src_ref=rows, dst_ref=rows, send_sem=send_sems.at[3 * t + j], recv_sem=recv_sems.at[3 * t + j],
                                                device_id=(x, y, 1 - c), device_id_type=MESH)

        for t in range(nt):
            for j in range(3):
                copy(t, j, c).start()
        for t in range(nt):
            for j in range(3):
                copy(t, j, c).wait_send()
                copy(t, j, 1 - c).wait_recv()

    hbm = pl.BlockSpec(memory_space=pl.ANY)
    return pl.pallas_call(body, out_shape=[jax.ShapeDtypeStruct(g.shape, g.dtype) for g in gs], in_specs=[hbm] * nt, out_specs=[hbm] * nt,
                          scratch_shapes=[pltpu.SemaphoreType.DMA((3 * nt,)), pltpu.SemaphoreType.DMA((3 * nt,))],
                          input_output_aliases={t: t for t in range(nt)}, name=name)(*gs)


def _mm_tn_wire(name, a, b, me, sqrelu, shard_rows):
    T, M = a.shape
    N = b.shape[1]
    tk = min(2048, T)
    nk = T // tk
    if shard_rows:
        bm, bn = M // N_DEV, N
        a_spec = pl.BlockSpec((tk, 2 * bm), lambda j, k, m: (k, j))
        b_spec = pl.BlockSpec((tk, bn), lambda j, k, m: (k, 0))
        halves = (slice(0, bm), slice(None)), (slice(bm, 2 * bm), slice(None))
        acc_shape = (2 * bm, bn)
    else:
        bm, bn = M, N // N_DEV
        a_spec = pl.BlockSpec((tk, bm), lambda j, k, m: (k, 0))
        b_spec = pl.BlockSpec((tk, 2 * bn), lambda j, k, m: (k, j))
        halves = (slice(None), slice(0, bn)), (slice(None), slice(bn, 2 * bn))
        acc_shape = (bm, 2 * bn)

    def body(me_ref, a_ref, b_ref, wire_ref, own_ref, acc):
        j, k = pl.program_id(0), pl.program_id(1)

        @pl.when(k == 0)
        def _():
            acc[...] = jnp.zeros(acc.shape, F32)

        av = a_ref[...]
        if sqrelu:
            r = jnp.maximum(av, 0.0)
            av = r * r
        acc[...] += _dot_tn(av, b_ref[...])

        for hh in range(2):
            @pl.when(k == nk - 1)
            def _():
                wire_ref[hh] = acc[halves[hh]].astype(MM)

            @pl.when((k == nk - 1) & (2 * j + hh == me_ref[0]))
            def _():
                own_ref[...] = acc[halves[hh]]

    return pl.pallas_call(
        body, grid_spec=pltpu.PrefetchScalarGridSpec(
            num_scalar_prefetch=1, grid=(N_DEV // 2, nk), in_specs=[a_spec, b_spec],
            out_specs=[pl.BlockSpec((2, bm, bn), lambda j, k, m: (j, 0, 0)), pl.BlockSpec((bm, bn), lambda j, k, m: (0, 0))],
            scratch_shapes=[pltpu.VMEM(acc_shape, F32)]),
        out_shape=[jax.ShapeDtypeStruct((N_DEV, bm, bn), MM), jax.ShapeDtypeStruct((bm, bn), F32)], name=name,
        compiler_params=_cparams(("arbitrary", "arbitrary")))(me.reshape(1), a, b)


def _rs_final(name, own, recv, me, stack, li):
    _, r, c = recv.shape
    tr = RS_ROWS

    def body(me_ref, own_ref, r_ref, s_ref, o_ref):
        acc = jnp.zeros((tr, c), F32)
        for j in range(N_DEV):
            acc = acc + jnp.where(me_ref[0] == j, own_ref[...], r_ref[j].astype(F32))
        o_ref[...] = acc

    return pl.pallas_call(
        body, grid_spec=pltpu.PrefetchScalarGridSpec(
            num_scalar_prefetch=1, grid=(r // tr,),
            in_specs=[pl.BlockSpec((tr, c), lambda i, m: (i, 0)), pl.BlockSpec((N_DEV, tr, c), lambda i, m: (0, i, 0)),
                      pl.BlockSpec(memory_space=pl.ANY)],
            out_specs=pl.BlockSpec((None, tr, c), lambda i, m: (li, i, 0))),
        out_shape=jax.ShapeDtypeStruct(stack.shape, F32), input_output_aliases={3: 0}, name=name,
        compiler_params=_cparams(("arbitrary",)))(me.reshape(1), own, recv, stack)


def _rs_finish(pending, after, me, stacks):
    i, sems, wires_thru, lands, owns = pending
    recvs = _copies_wait(f"rs_wait{i}", sems, wires_thru, lands, after, ALL_PEERS, True)
    return [_rs_final(f"rs_final{i}_{t}", owns[t], recvs[t], me, stacks[t], i) for t in range(len(owns))]


def _mod_part(c16, ada_w, ada_b_cols):
    L, D, n = ada_w.shape

    def body(c_ref, w_ref, b_ref, o_ref):
        cv = c_ref[...]
        o_ref[...] = _dot(cv * _sigmoid(cv), w_ref[...]) + b_ref[...]

    return pl.pallas_call(
        body, grid=(L,), in_specs=[pl.BlockSpec((16, D), lambda i: (0, 0)), pl.BlockSpec((None, D, n), lambda i: (i, 0, 0)),
                                   pl.BlockSpec((None, 1, n), lambda i: (i, 0, 0))],
        out_specs=pl.BlockSpec((None, 16, n), lambda i: (i, 0, 0)), out_shape=jax.ShapeDtypeStruct((L, 16, n), F32),
        name="ada_mod", compiler_params=_cparams(("arbitrary",)))(c16, ada_w, ada_b_cols)


def _ada_w_grad(c16, dmod16):
    L, _, n = dmod16.shape
    D = c16.shape[1]

    def body(c_ref, d_ref, o_ref):
        cv = c_ref[...]
        o_ref[...] = _dot_tn(cv * _sigmoid(cv), d_ref[...])

    return pl.pallas_call(
        body, grid=(L,), in_specs=[pl.BlockSpec((16, D), lambda i: (0, 0)), pl.BlockSpec((None, 16, n), lambda i: (i, 0, 0))],
        out_specs=pl.BlockSpec((None, D, n), lambda i: (i, 0, 0)), out_shape=jax.ShapeDtypeStruct((L, D, n), F32),
        name="ada_w_grad", compiler_params=_cparams(("arbitrary",)))(c16, dmod16)


def _sum_devices(name, g):
    _, R, C = g.shape

    def body(g_ref, o_ref):
        acc = g_ref[0]
        for d in range(1, N_DEV):
            acc = acc + g_ref[d]
        o_ref[...] = acc

    return pl.pallas_call(body, out_shape=jax.ShapeDtypeStruct((R, C), F32), name=name)(g)


def _prenorm(name, x, g0, sc, sh, dtype):
    T, D = x.shape

    def body(i, n, rr, cc, oo, aa, ss):
        oo[0][...] = (_rms(rr[0][...], cc[0][...]) * (1.0 + cc[1][...]) + cc[2][...]).astype(dtype)

    return _rows(name, body, T, 512, [(x, 'cur')], [g0, sc, sh], [(D, dtype)])[0]


def _post_bwd_math(d, yv, g1v, gtv):
    dgt = jnp.sum(d * _rms(yv, g1v), axis=0, keepdims=True)
    dy, dg1 = _rms_bwd(yv, g1v, d * gtv)
    return dy, dg1, dgt


def _post_bwd_nt(name, dxo, y, g1, gt, w):
    T, D = y.shape
    K = w.shape[0]

    def body(i, n, rr, cc, oo, aa, ss):
        dy, dg1, dgt = _post_bwd_math(rr[0][...], rr[1][...], cc[0][...], cc[1][...])
        aa[0][...] += dg1
        aa[1][...] += dgt
        dy = dy.astype(MM)
        oo[0][...] = dy
        oo[1][...] = _dot_nt(dy, cc[2][...]).astype(MM)

    return _rows(name, body, T, 512, [(dxo, 'cur'), (y, 'cur')], [g1, gt, w], [(D, MM), (K, MM)], accs=[(1, D)] * 2)


def _mm_post(name, a, w, bias, x, g1, gt):
    T, D = x.shape
    consts = [w, g1, gt] + ([bias] if bias is not None else [])

    def body(i, n, rr, cc, oo, aa, ss):
        y = _dot(rr[0][...], cc[0][...])
        if bias is not None:
            y = y + cc[3][...]
        oo[0][...] = y
        oo[1][...] = rr[1][...] + cc[2][...] * _rms(y, cc[1][...])

    return _rows(name, body, T, 512, [(a, 'cur'), (x, 'cur')], consts, [(D, F32), (D, F32)])


def _mm_tn(name, a, b, sqrelu=False, col_shards=0, diag=0):
    T, M = a.shape
    N = b.shape[1]
    tk = min(512, T)
    nk = T // tk
    if diag:
        bm, bn = M // diag, N // diag
        grid = (diag, 1, nk)
        a_spec = pl.BlockSpec((tk, bm), lambda g, n, k: (k, g))
        b_spec = pl.BlockSpec((tk, bn), lambda g, n, k: (k, g))
        o_spec = pl.BlockSpec((None, bm, bn), lambda g, n, k: (g, 0, 0))
        o_shape = (diag, bm, bn)
    else:
        bm = min(M, 1024)
        bn = N // col_shards if col_shards else min(N, 1024)
        grid = (M // bm, N // bn, nk)
        a_spec = pl.BlockSpec((tk, bm), lambda m, n, k: (k, m))
        b_spec = pl.BlockSpec((tk, bn), lambda m, n, k: (k, n))
        if col_shards:
            o_spec = pl.BlockSpec((None, bm, bn), lambda m, n, k: (n, m, 0))
            o_shape = (col_shards, M, bn)
        else:
            o_spec = pl.BlockSpec((bm, bn), lambda m, n, k: (m, n))
            o_shape = (M, N)

    def body(a_ref, b_ref, o_ref):
        @pl.when(pl.program_id(2) == 0)
        def _():
            o_ref[...] = jnp.zeros(o_ref.shape, F32)

        av = a_ref[...]
        if sqrelu:
            r = jnp.maximum(av, 0.0)
            av = r * r
        o_ref[...] += _dot_tn(av, b_ref[...])

    return pl.pallas_call(body, grid=grid, in_specs=[a_spec, b_spec], out_specs=o_spec,
                          out_shape=jax.ShapeDtypeStruct(o_shape, F32), name=name,
                          compiler_params=_cparams(("arbitrary", "arbitrary", "arbitrary")))(a, b)


FFN_SHARDS = 4
FFN_BWD_SHARDS = 2

def _ffn_fwd(name, li, x, g0, sc, sh, w1g, w2g, g1, gt):
    T, D = x.shape
    nf, tf = w1g.shape[0], w1g.shape[-1]
    F = nf * tf
    tm = min(512, T)

    def body(x_ref, g0_ref, sc_ref, sh_ref, w1_ref, w2_ref, g1_ref, gt_ref, h_ref, a_ref, y_ref, xo_ref, acc):
        f = pl.program_id(1)

        @pl.when(f == 0)
        def _():
            acc[...] = jnp.zeros(acc.shape, F32)
            h_ref[...] = (_rms(x_ref[...], g0_ref[...]) * (1.0 + sc_ref[...]) + sh_ref[...]).astype(MM)

        hv = h_ref[...]
        part = None
        for hh in range(FFN_SHARDS):
            a = _dot(hv, w1_ref[hh])
            a_ref[:, hh * tf:(hh + 1) * tf] = a.astype(MM)
            r = jnp.maximum(a, 0.0)
            p = _dot(r * r, w2_ref[hh])
            part = p if part is None else part + p
        acc[...] += part

        @pl.when(f == nf // FFN_SHARDS - 1)
        def _():
            y = acc[...]
            y_ref[...] = y
            xo_ref[...] = x_ref[...] + gt_ref[...] * _rms(y, g1_ref[...])

    row = lambda t, f: (t, 0)
    one = lambda t, f: (0, 0)
    return pl.pallas_call(
        body, grid=(T // tm, nf // FFN_SHARDS),
        in_specs=[pl.BlockSpec((tm, D), row)] + [pl.BlockSpec((1, D), one)] * 3
        + [pl.BlockSpec((FFN_SHARDS, None, D, tf), lambda t, f: (f, li, 0, 0)), pl.BlockSpec((FFN_SHARDS, None, tf, D), lambda t, f: (f, li, 0, 0)),
           pl.BlockSpec((1, D), one), pl.BlockSpec((1, D), one)],
        out_specs=[pl.BlockSpec((tm, D), row), pl.BlockSpec((tm, FFN_SHARDS * tf), lambda t, f: (t, f)), pl.BlockSpec((tm, D), row),
                   pl.BlockSpec((tm, D), row)],
        out_shape=[jax.ShapeDtypeStruct((T, D), MM), jax.ShapeDtypeStruct((T, F), MM), jax.ShapeDtypeStruct((T, D), F32),
                   jax.ShapeDtypeStruct((T, D), F32)],
        scratch_shapes=[pltpu.VMEM((tm, D), F32)], name=name,
        compiler_params=_cparams(("arbitrary", "arbitrary")))(x, g0, sc, sh, w1g, w2g, g1, gt)


def _ffn_bwd(name, li, y, g1, gt, a, w1g, w2g, x, dxo, g0, sc):
    T, D = x.shape
    nf, tf = w1g.shape[0], w1g.shape[-1]
    F = nf * tf
    tm = min(512, T)
    ns = FFN_BWD_SHARDS

    def body(y_ref, g1_ref, gt_ref, a_ref, w1_ref, w2_ref, x_ref, dxo_ref, g0_ref, sc_ref,
             dy_ref, da_ref, dx_ref, dg1_ref, dgt_ref, dsh_ref, dsc_ref, dg0_ref, acc):
        t, f = pl.program_id(0), pl.program_id(1)

        @pl.when((t == 0) & (f == 0))
        def _():
            for r in (dg1_ref, dgt_ref, dsh_ref, dsc_ref, dg0_ref):
                r[...] = jnp.zeros(r.shape, F32)

        @pl.when(f == 0)
        def _():
            acc[...] = jnp.zeros(acc.shape, F32)
            d, yv, g1v = dxo_ref[...], y_ref[...], g1_ref[...]
            dgt_ref[...] += jnp.sum(d * _rms(yv, g1v), axis=0, keepdims=True)
            dyf, dg1 = _rms_bwd(yv, g1v, d * gt_ref[...])
            dg1_ref[...] += dg1
            dy_ref[...] = dyf.astype(MM)

        dyv = dy_ref[...]
        dyv = dyv + dyv
        part = None
        for hh in range(ns):
            cols = slice(hh * tf, (hh + 1) * tf)
            du = _dot_nt(dyv, w2_ref[hh])
            da = (du * jnp.maximum(a_ref[:, cols], 0.0).astype(F32)).astype(MM)
            da_ref[:, cols] = da
            p = _dot_nt(da, w1_ref[hh])
            part = p if part is None else part + p
        acc[...] += part

        @pl.when(f == nf // ns - 1)
        def _():
            dx, dsh, dsc, dg0 = _prenorm_bwd(x_ref[...], g0_ref[...], sc_ref[...], acc[...])
            dx_ref[...] = dxo_ref[...] + dx
            dsh_ref[...] += dsh
            dsc_ref[...] += dsc
            dg0_ref[...] += dg0

    row = lambda t, f: (t, 0)
    one = lambda t, f: (0, 0)
    blk = lambda t, f: (t, f)
    return pl.pallas_call(
        body, grid=(T // tm, nf // ns),
        in_specs=[pl.BlockSpec((tm, D), row), pl.BlockSpec((1, D), one), pl.BlockSpec((1, D), one), pl.BlockSpec((tm, ns * tf), blk),
                  pl.BlockSpec((ns, None, D, tf), lambda t, f: (f, li, 0, 0)),
                  pl.BlockSpec((ns, None, tf, D), lambda t, f: (f, li, 0, 0)), pl.BlockSpec((tm, D), row), pl.BlockSpec((tm, D), row),
                  pl.BlockSpec((1, D), one), pl.BlockSpec((1, D), one)],
        out_specs=[pl.BlockSpec((tm, D), row), pl.BlockSpec((tm, ns * tf), blk), pl.BlockSpec((tm, D), row)] + [pl.BlockSpec((1, D), one)] * 5,
        out_shape=[jax.ShapeDtypeStruct((T, D), MM), jax.ShapeDtypeStruct((T, F), MM), jax.ShapeDtypeStruct((T, D), F32)]
        + [jax.ShapeDtypeStruct((1, D), F32)] * 5,
        scratch_shapes=[pltpu.VMEM((tm, D), F32)], name=name,
        compiler_params=_cparams(("arbitrary", "arbitrary")))(y, g1, gt, a, w1g, w2g, x, dxo, g0, sc)


def _rope_tables(pos, invf):
    T = pos.shape[0]

    def body(i, n, rr, cc, oo, aa, ss):
        ang = rr[0][...] * cc[0][...]
        lane = lax.broadcasted_iota(jnp.int32, ang.shape, 1)
        cs, sn = jnp.cos(ang), jnp.sin(ang)
        oo[0][...] = jnp.where((lane >= QK_NOPE) & (lane < QK_NOPE + QK_ROPE), cs, 1.0)
        oo[1][...] = jnp.where((lane >= QK_NOPE) & (lane < QK_NOPE + QK_ROPE // 2), -sn, 0.0)
        oo[2][...] = jnp.where((lane >= QK_NOPE + QK_ROPE // 2) & (lane < QK_NOPE + QK_ROPE), sn, 0.0)

    return _rows("rope_tables", body, T, 512, [(pos, 'cur')], [invf], [(HEAD_PAD, F32)] * 3)


def _rope(v, C, S1, S2):
    n = v.shape[1]
    reps = n // HEAD_PAD
    if reps > 1:
        C, S1, S2 = (jnp.tile(t, (1, reps)) for t in (C, S1, S2))
    return v * C + pltpu.roll(v, n - QK_ROPE // 2, 1) * S1 + pltpu.roll(v, QK_ROPE // 2, 1) * S2


def _unrope(d, C, S1, S2):
    n = d.shape[1]
    reps = n // HEAD_PAD
    if reps > 1:
        C, S1, S2 = (jnp.tile(t, (1, reps)) for t in (C, S1, S2))
    return d * C + pltpu.roll(d * S1, QK_ROPE // 2, 1) + pltpu.roll(d * S2, n - QK_ROPE // 2, 1)


def _mla_proj(name, x, g0, sc, sh, C, S1, S2, w_dq, qg, w_uq, w_dkv, kvg, w_ukv_k, w_ukv_v):
    T, D = x.shape
    HP = N_HEADS * HEAD_PAD

    def body(i, n, rr, cc, oo, aa, ss):
        hv = (_rms(rr[0][...], cc[7][...]) * (1.0 + cc[8][...]) + cc[9][...]).astype(MM)
        oo[7][...] = hv
        Cv, S1v, S2v = rr[1][...], rr[2][...], rr[3][...]
        cq_raw = _dot(hv, cc[0][...])
        cq = _rms(cq_raw, cc[1][...]).astype(MM)
        q = _rope(_dot(cq, cc[2][...]), Cv, S1v, S2v)
        ckv_all = _dot(hv, cc[3][...])
        ckv_raw = ckv_all[:, :KV_LORA]
        ckv = _rms(ckv_raw, cc[4][...]).astype(MM)
        kr = _rope(ckv_all[:, KV_LORA:], Cv, S1v, S2v)
        k = _dot(ckv, cc[5][...]) + jnp.tile(kr, (1, N_HEADS))
        v = _dot(ckv, cc[6][...])
        v = jnp.where(lax.broadcasted_iota(jnp.int32, v.shape, 1) % HEAD_PAD == V_HEAD, 1.0, v)
        oo[0][...] = cq_raw
        oo[1][...] = cq
        oo[2][...] = ckv_raw
        oo[3][...] = ckv
        oo[4][...] = q.astype(MM)
        oo[5][...] = k.astype(MM)
        oo[6][...] = v.astype(MM)

    return _rows(name, body, T, 256, [(x, 'cur'), (C, 'cur'), (S1, 'cur'), (S2, 'cur')],
                 [w_dq, qg, w_uq, w_dkv, kvg, w_ukv_k, w_ukv_v, g0, sc, sh],
                 [(Q_LORA, F32), (Q_LORA, MM), (KV_LORA, F32), (KV_LORA, MM), (HP, MM), (HP, MM), (HP, MM), (D, MM)])


ATT_HEADS = 4
ATT_BLOCK = 512
ATT_FWD_BLOCK = 1024


def _chunk_mask_t(tk, tq):
    ki = lax.broadcasted_iota(jnp.int32, (tk, tq), 0) // CHUNK
    qi = lax.broadcasted_iota(jnp.int32, (tk, tq), 1) // CHUNK
    return ki <= qi


def _attn_fwd(name, q, k, v):
    T = q.shape[0]
    tb = min(ATT_FWD_BLOCK, T)
    nb = T // tb
    nh = ATT_HEADS
    hs = [slice(h * HEAD_PAD, (h + 1) * HEAD_PAD) for h in range(nh)]

    def body(q_ref, k_ref, v_ref, o_ref, lse_ref):
        qb = pl.program_id(1)

        def k_block(k0, masked, st):
            new = []
            for h in range(nh):
                m, acc = st[h]
                s = _dot_nt(k_ref[pl.ds(k0, tb), hs[h]], q_ref[:, hs[h]])
                if masked:
                    s = jnp.where(_chunk_mask_t(tb, tb), s, NEG)
                m_new = jnp.maximum(m, jnp.max(s, axis=0, keepdims=True))
                alpha = jnp.exp((m - m_new) * ATT_SCALE)
                p = jnp.exp((s - m_new) * ATT_SCALE)
                acc = alpha * acc + _dot_tn(v_ref[pl.ds(k0, tb), hs[h]], p)
                new.append((m_new, acc))
            return tuple(new)

        st = tuple((jnp.full((1, tb), NEG, F32), jnp.zeros((HEAD_PAD, tb), F32)) for _ in range(nh))
        st = k_block(pl.multiple_of(qb * tb, tb), True, st)
        st = lax.fori_loop(0, qb, lambda kb, s_: k_block(pl.multiple_of(kb * tb, tb), False, s_), st)
        for h in range(nh):
            m, acc = st[h]
            l = acc[V_HEAD:V_HEAD + 1, :]
            o_ref[:, hs[h]] = (acc / l).T.astype(MM)
            lse_ref[h] = jnp.broadcast_to(m * ATT_SCALE + jnp.log(l), (8, tb))

    blk = pl.BlockSpec((tb, nh * HEAD_PAD), lambda g, i: (i, g))
    res = pl.BlockSpec((T, nh * HEAD_PAD), lambda g, i: (0, g))
    return pl.pallas_call(
        body, grid=(N_HEADS // nh, nb), in_specs=[blk, res, res],
        out_specs=[blk, pl.BlockSpec((nh, 8, tb), lambda g, i: (g, 0, i))],
        out_shape=[jax.ShapeDtypeStruct(q.shape, MM), jax.ShapeDtypeStruct((N_HEADS, 8, T), F32)], name=name,
        compiler_params=_cparams(("arbitrary", "arbitrary")))(q, k, v)


def _attn_delta(name, do, o):
    T = do.shape[0]
    tb = min(256, T)

    def body(do_ref, o_ref, d_ref):
        lane = lax.broadcasted_iota(jnp.int32, (tb, HEAD_PAD), 1) // 8
        cols = jnp.zeros((tb, HEAD_PAD), F32)
        for h in range(N_HEADS):
            hsl = slice(h * HEAD_PAD, (h + 1) * HEAD_PAD)
            r = jnp.sum(do_ref[:, hsl].astype(F32) * o_ref[:, hsl].astype(F32), axis=1, keepdims=True)
            cols = jnp.where(lane == h, r, cols)
        d_ref[...] = cols.T

    spec = pl.BlockSpec((tb, N_HEADS * HEAD_PAD), lambda i: (i, 0))
    out = pl.pallas_call(body, grid=(T // tb,), in_specs=[spec, spec], out_specs=pl.BlockSpec((HEAD_PAD, tb), lambda i: (0, i)),
                         out_shape=jax.ShapeDtypeStruct((HEAD_PAD, T), F32), name=name, compiler_params=_cparams(("arbitrary",)))(do, o)
    return out.reshape(N_HEADS, 8, T)


def _attn_bwd(name, q, k, v, do, lse, delta):
    T = q.shape[0]
    tb = min(ATT_BLOCK, T)
    nb = T // tb
    nh = ATT_HEADS
    hs = [slice(h * HEAD_PAD, (h + 1) * HEAD_PAD) for h in range(nh)]

    def body(q_ref, k_ref, v_ref, do_ref, lse_ref, dl_ref, dq_ref, dk_ref, dv_ref, dq_acc, dk_acc, dv_acc):
        kb = pl.program_id(1)

        @pl.when(kb == 0)
        def _():
            dq_acc[...] = jnp.zeros(dq_acc.shape, F32)

        dk_acc[...] = jnp.zeros(dk_acc.shape, F32)
        dv_acc[...] = jnp.zeros(dv_acc.shape, F32)

        def q_block(q0, masked):
            for h in range(nh):
                qh = q_ref[pl.ds(q0, tb), hs[h]]
                doh = do_ref[pl.ds(q0, tb), hs[h]]
                kh = k_ref[:, hs[h]]
                s = _dot_nt(kh, qh) * ATT_SCALE
                if masked:
                    s = jnp.where(_chunk_mask_t(tb, tb), s, NEG)
                p = jnp.exp(s - lse_ref[h, 0:1, pl.ds(q0, tb)])
                ds = (p * (_dot_nt(v_ref[:, hs[h]], doh) - dl_ref[h, 0:1, pl.ds(q0, tb)]) * ATT_SCALE).astype(MM)
                dv_acc[:, hs[h]] += _dot(p, doh)
                dk_acc[:, hs[h]] += _dot(ds, qh)
                dq_acc[pl.ds(q0, tb), hs[h]] += _dot_tn(ds, kh)

        q_block(pl.multiple_of(kb * tb, tb), True)

        def rest(qb, c_):
            q_block(pl.multiple_of(qb * tb, tb), False)
            return c_

        lax.fori_loop(kb + 1, nb, rest, 0)
        dk_ref[...] = dk_acc[...].astype(MM)
        dv_ref[...] = dv_acc[...].astype(MM)

        @pl.when(kb == nb - 1)
        def _():
            dq_ref[...] = dq_acc[...].astype(MM)

    W = nh * HEAD_PAD
    blk = pl.BlockSpec((tb, W), lambda g, i: (i, g))
    res = pl.BlockSpec((T, W), lambda g, i: (0, g))
    rows = pl.BlockSpec((nh, 8, T), lambda g, i: (g, 0, 0))
    return pl.pallas_call(
        body, grid=(N_HEADS // nh, nb), in_specs=[res, blk, blk, res, rows, rows], out_specs=[res, blk, blk],
        out_shape=[jax.ShapeDtypeStruct(q.shape, MM)] * 3,
        scratch_shapes=[pltpu.VMEM((T, W), F32), pltpu.VMEM((tb, W), F32), pltpu.VMEM((tb, W), F32)],
        name=name, compiler_params=_cparams(("arbitrary", "arbitrary")))(q, k, v, do, lse, delta)


def _mla_proj_bwd(name, dq, dk, dv, C, S1, S2, cq_raw, ckv_raw, x, dxo, w_uq, w_ukv_k, w_ukv_v, w_dq, w_dkv, qg, kvg, g0, sc):
    T, D = x.shape
    HP = N_HEADS * HEAD_PAD

    def body(i, n, rr, cc, oo, aa, ss):
        Cv, S1v, S2v = rr[3][...], rr[4][...], rr[5][...]
        dq_pre = _unrope(rr[0][...].astype(F32), Cv, S1v, S2v).astype(MM)
        oo[0][...] = dq_pre
        dcq = _dot_nt(dq_pre, cc[0][...])
        dcq_raw, dqg = _rms_bwd(rr[6][...], cc[5][...], dcq)
        aa[0][...] += dqg
        dcq_raw = dcq_raw.astype(MM)
        oo[1][...] = dcq_raw
        dkv = rr[1][...]
        dkr = dkv[:, :HEAD_PAD].astype(F32)
        for hh in range(1, N_HEADS):
            dkr = dkr + dkv[:, hh * HEAD_PAD:(hh + 1) * HEAD_PAD].astype(F32)
        lane = lax.broadcasted_iota(jnp.int32, dkr.shape, 1)
        dkr = jnp.where((lane >= QK_NOPE) & (lane < QK_NOPE + QK_ROPE), _unrope(dkr, Cv, S1v, S2v), 0.0)
        dckv = _dot_nt(dkv, cc[1][...]) + _dot_nt(rr[2][...], cc[2][...])
        dckv_raw, dkvg = _rms_bwd(rr[7][...], cc[6][...], dckv)
        aa[1][...] += dkvg
        dckv_all = jnp.concatenate([dckv_raw, dkr], axis=1).astype(MM)
        oo[2][...] = dckv_all
        dh = _dot_nt(dcq_raw, cc[3][...]) + _dot_nt(dckv_all, cc[4][...])
        dx, dsh, dsc, dg0 = _prenorm_bwd(rr[8][...], cc[7][...], cc[8][...], dh)
        oo[3][...] = rr[9][...] + dx
        aa[2][...] += dsh
        aa[3][...] += dsc
        aa[4][...] += dg0

    return _rows(name, body, T, 256,
                 [(dq, 'cur'), (dk, 'cur'), (dv, 'cur'), (C, 'cur'), (S1, 'cur'), (S2, 'cur'), (cq_raw, 'cur'), (ckv_raw, 'cur'),
                  (x, 'cur'), (dxo, 'cur')],
                 [w_uq, w_ukv_k, w_ukv_v, w_dq, w_dkv, qg, kvg, g0, sc],
                 [(HP, MM), (Q_LORA, MM), (KV_LORA + HEAD_PAD, MM), (D, F32)],
                 accs=[(1, Q_LORA), (1, KV_LORA), (1, D), (1, D), (1, D)])


HALO = 32


def _windows(ext, tm, first):
    rolled = {0: ext}
    out = []
    for j in range(CONV_W):
        r = (first + j) % 8
        if r not in rolled:
            rolled[r] = pltpu.roll(ext, ext.shape[0] - r, 0)
        out.append(rolled[r][first + j - r:first + j - r + tm])
    return out


def _conv_glu(name, x, g0, sc, sh, w_pw1, b_pw1):
    T, D = x.shape

    def body(i, n, rr, cc, oo, aa, ss):
        hv = (_rms(rr[0][...], cc[2][...]) * (1.0 + cc[3][...]) + cc[4][...]).astype(MM)
        oo[2][...] = hv
        a = _dot(hv, cc[0][...]) + cc[1][...]
        oo[0][...] = a
        oo[1][...] = a[:, :D] * _sigmoid(a[:, D:])

    return _rows(name, body, T, 512, [(x, 'cur')], [w_pw1, b_pw1, g0, sc, sh], [(2 * D, F32), (D, F32), (D, MM)])


def _layernorm_parts(uc):
    xc = uc - jnp.mean(uc, axis=-1, keepdims=True)
    r = lax.rsqrt(jnp.mean(xc * xc, axis=-1, keepdims=True) + EPS)
    return xc * r, r


def _conv_dw(name, u, w_dw, b_dw, ln_g, ln_b, w_pw2, b_pw2, x, g1, gt):
    T, D = u.shape
    tm = min(256, T)

    def body(i, n, rr, cc, oo, aa, ss):
        ext = jnp.concatenate([jnp.where(i > 0, rr[1][tm - HALO:tm, :], 0.0), rr[0][...]], axis=0)
        uc = jnp.zeros((tm, D), F32) + cc[1][...]
        for kk, win in enumerate(_windows(ext, tm, HALO - (CONV_W - 1))):
            uc = uc + win * cc[0][kk:kk + 1, :]
        xh, _ = _layernorm_parts(uc)
        ln = xh * cc[2][...] + cc[3][...]
        z = (ln * _sigmoid(ln)).astype(MM)
        y = _dot(z, cc[4][...]) + cc[5][...]
        oo[0][...] = uc
        oo[1][...] = z
        oo[2][...] = y
        oo[3][...] = rr[2][...] + cc[7][...] * _rms(y, cc[6][...])

    return _rows(name, body, T, tm, [(u, 'cur'), (u, 'prev'), (x, 'cur')], [w_dw, b_dw, ln_g, ln_b, w_pw2, b_pw2, g1, gt],
                 [(D, F32), (D, MM), (D, F32), (D, F32)])


def _conv_bwd1(name, dxo, y, g1, gt, uc, w_pw2, ln_g, ln_b):
    T, D = uc.shape

    def body(i, n, rr, cc, oo, aa, ss):
        dy, dg1, dgt = _post_bwd_math(rr[0][...], rr[1][...], cc[3][...], cc[4][...])
        aa[3][...] += dg1
        aa[4][...] += dgt
        aa[5][...] += jnp.sum(dy, axis=0, keepdims=True)
        dy = dy.astype(MM)
        oo[1][...] = dy
        dz = _dot_nt(dy, cc[0][...])
        xh, r = _layernorm_parts(rr[2][...])
        g = cc[1][...]
        ln = xh * g + cc[2][...]
        sg = _sigmoid(ln)
        dln = dz * (sg * (1.0 + ln * (1.0 - sg)))
        aa[0][...] += jnp.sum(dln * xh, axis=0, keepdims=True)
        aa[1][...] += jnp.sum(dln, axis=0, keepdims=True)
        dxh = dln * g
        duc = r * (dxh - jnp.mean(dxh, axis=-1, keepdims=True) - xh * jnp.mean(dxh * xh, axis=-1, keepdims=True))
        aa[2][...] += jnp.sum(duc, axis=0, keepdims=True)
        oo[0][...] = duc

    return _rows(name, body, T, 256, [(dxo, 'cur'), (y, 'cur'), (uc, 'cur')], [w_pw2, ln_g, ln_b, g1, gt], [(D, F32), (D, MM)],
                 accs=[(1, D)] * 6)


def _conv_bwd2(name, duc, u, a, x, dxo, w_dw, w_pw1, g0, sc):
    T, D = u.shape
    tm = min(256, T)

    def body(i, n, rr, cc, oo, aa, ss):
        dcur = rr[0][...]
        extd = jnp.concatenate([dcur, jnp.where(i < n - 1, rr[1][0:HALO, :], 0.0)], axis=0)
        extu = jnp.concatenate([jnp.where(i > 0, rr[3][tm - HALO:tm, :], 0.0), rr[2][...]], axis=0)
        wd = _windows(extd, tm, 0)
        wu = _windows(extu, tm, HALO - (CONV_W - 1))
        du = jnp.zeros((tm, D), F32)
        for kk in range(CONV_W):
            du = du + wd[CONV_W - 1 - kk] * cc[0][kk:kk + 1, :]
            aa[0][kk:kk + 1, :] += jnp.sum(dcur * wu[kk], axis=0, keepdims=True)
        av = rr[4][...]
        a1, sg = av[:, :D], _sigmoid(av[:, D:])
        da = jnp.concatenate([du * sg, du * a1 * (sg * (1.0 - sg))], axis=1)
        aa[1][...] += jnp.sum(da, axis=0, keepdims=True)
        da = da.astype(MM)
        oo[0][...] = da
        dx, dsh, dsc, dg0 = _prenorm_bwd(rr[5][...], cc[2][...], cc[3][...], _dot_nt(da, cc[1][...]))
        oo[1][...] = rr[6][...] + dx
        aa[2][...] += dsh
        aa[3][...] += dsc
        aa[4][...] += dg0

    return _rows(name, body, T, tm,
                 [(duc, 'cur'), (duc, 'next'), (u, 'cur'), (u, 'prev'), (a, 'cur'), (x, 'cur'), (dxo, 'cur')],
                 [w_dw, w_pw1, g0, sc], [(2 * D, MM), (D, F32)],
                 accs=[(32, D), (1, 2 * D), (1, D), (1, D), (1, D)])


PHALO = 16


def _pool_fwd(name, h, w, b, scale, x, g1, gt):
    T, D = h.shape
    G = len(POOL_WINDOWS)
    Cg = D // G
    tm = min(256, T)

    def body(i, n, rr, cc, oo, aa, ss):
        ext = ss[0]
        ext[0:PHALO, :] = jnp.where(i > 0, rr[1][tm - PHALO:tm, :], 0.0)
        ext[PHALO:PHALO + tm, :] = rr[0][...]
        t_glob = i * tm + lax.broadcasted_iota(jnp.int32, (tm, 1), 0)
        ps, ys = [], []
        for g, win in enumerate(POOL_WINDOWS):
            cols = slice(g * Cg, (g + 1) * Cg)
            s = ext[pl.ds(PHALO, tm), cols]
            for j in range(1, win):
                s = s + ext[pl.ds(PHALO - j, tm), cols]
            cnt = jnp.minimum(t_glob + 1, win).astype(F32)
            p = (s / cnt - ext[pl.ds(PHALO, tm), cols]).astype(MM)
            ps.append(p)
            ys.append(_dot(p, cc[0][g]) + cc[1][:, cols])
        ypre = jnp.concatenate(ys, axis=1)
        y = ypre * cc[2][...]
        oo[0][...] = jnp.concatenate(ps, axis=1)
        oo[1][...] = ypre
        oo[2][...] = y
        oo[3][...] = rr[2][...] + cc[4][...] * _rms(y, cc[3][...])

    return _rows(name, body, T, tm, [(h, 'cur'), (h, 'prev'), (x, 'cur')], [w, b, scale, g1, gt],
                 [(D, MM), (D, F32), (D, F32), (D, F32)], scratch=[pltpu.VMEM((tm + PHALO, D), F32)])


def _pool_bwd1(name, dxo, y, g1, gt, ypre, scale, w):
    T, D = ypre.shape
    G = len(POOL_WINDOWS)
    Cg = D // G

    def body(i, n, rr, cc, oo, aa, ss):
        dyv, dg1, dgt = _post_bwd_math(rr[0][...], rr[1][...], cc[2][...], cc[3][...])
        aa[2][...] += dg1
        aa[3][...] += dgt
        aa[0][...] += jnp.sum(dyv * rr[2][...], axis=0, keepdims=True)
        dypre = dyv * cc[0][...]
        aa[1][...] += jnp.sum(dypre, axis=0, keepdims=True)
        dypre = dypre.astype(MM)
        oo[1][...] = dypre
        oo[0][...] = jnp.concatenate([_dot_nt(dypre[:, g * Cg:(g + 1) * Cg], cc[1][g]) for g in range(G)], axis=1)

    return _rows(name, body, T, 256, [(dxo, 'cur'), (y, 'cur'), (ypre, 'cur')], [scale, w, g1, gt], [(D, F32), (D, MM)],
                 accs=[(1, D)] * 4)


def _pool_bwd2(name, dp, x, dxo, g0, sc):
    T, D = x.shape
    G = len(POOL_WINDOWS)
    Cg = D // G
    tm = min(256, T)

    def body(i, n, rr, cc, oo, aa, ss):
        ext = ss[0]
        t_glob = i * tm + lax.broadcasted_iota(jnp.int32, (tm, 1), 0)
        dcur = rr[0][...]
        dhs = []
        for g, win in enumerate(POOL_WINDOWS):
            cols = slice(g * Cg, (g + 1) * Cg)
            cnt = jnp.minimum(t_glob + 1, win).astype(F32)
            ext[0:tm, cols] = dcur[:, cols] / cnt
            ext[tm:tm + PHALO, cols] = jnp.where(i < n - 1, rr[1][0:PHALO, cols] * (1.0 / win), 0.0)
        for g, win in enumerate(POOL_WINDOWS):
            cols = slice(g * Cg, (g + 1) * Cg)
            s = ext[pl.ds(0, tm), cols]
            for j in range(1, win):
                s = s + ext[pl.ds(j, tm), cols]
            dhs.append(s - dcur[:, cols])
        dx, dsh, dsc, dg0 = _prenorm_bwd(rr[2][...], cc[0][...], cc[1][...], jnp.concatenate(dhs, axis=1))
        oo[0][...] = rr[3][...] + dx
        aa[0][...] += dsh
        aa[1][...] += dsc
        aa[2][...] += dg0

    return _rows(name, body, T, tm, [(dp, 'cur'), (dp, 'next'), (x, 'cur'), (dxo, 'cur')], [g0, sc], [(D, F32)],
                 accs=[(1, D)] * 3, scratch=[pltpu.VMEM((tm + PHALO, D), F32)])


def _loss_head(x, tgt):
    T, D = x.shape

    def body(i, n, rr, cc, oo, aa, ss):
        err = rr[0][...] - rr[1][...]
        oo[0][...] = err * (1.0 / D)
        aa[0][...] += jnp.sum(err * err, axis=0, keepdims=True)

        @pl.when(i == n - 1)
        def _():
            aa[1][...] = jnp.broadcast_to(jnp.sum(aa[0][...], axis=1, keepdims=True) * (0.5 / D), (1, 128))

    dx, _, loss_row = _rows("loss_head", body, T, 512, [(x, 'cur'), (tgt, 'cur')], [], [(D, F32)], accs=[(1, D), (1, 128)])
    return dx, loss_row


def _adamw(name, w, g, m, v):
    shape = w.shape
    C = shape[-1]
    R = w.size // C
    w2, g2, m2, v2 = (t.reshape(R, C) for t in (w, g, m, v))
    br = R
    if R * C * 4 > (1 << 20):
        br = 8
        while br * 2 * C * 4 <= (1 << 20) and R % (br * 2) == 0:
            br *= 2
    b1c = 1.0 - ADAM_B1 ** ADAM_STEP
    b2c = 1.0 - ADAM_B2 ** ADAM_STEP

    def body(w_ref, g_ref, m_ref, v_ref, d_ref, mo_ref, vo_ref):
        gv = g_ref[...]
        mn = ADAM_B1 * m_ref[...] + (1.0 - ADAM_B1) * gv
        vn = ADAM_B2 * v_ref[...] + (1.0 - ADAM_B2) * (gv * gv)
        d_ref[...] = -ADAM_LR * ((mn / b1c) / (jnp.sqrt(vn / b2c) + ADAM_EPS) + ADAM_WD * w_ref[...])
        mo_ref[...] = mn
        vo_ref[...] = vn

    spec = pl.BlockSpec((br, C), lambda r: (r, 0))
    outs = pl.pallas_call(body, grid=(R // br,), in_specs=[spec] * 4, out_specs=[spec] * 3,
                          out_shape=[jax.ShapeDtypeStruct((R, C), F32)] * 3, name=name,
                          compiler_params=_cparams(("arbitrary",)))(w2, g2, m2, v2)
    return tuple(t.reshape(shape) for t in outs)


def _layer_shards(g, ax):
    s = g.shape
    r = g.reshape(s[:ax] + (N_DEV, s[ax] // N_DEV) + s[ax + 1:])
    return (jnp.moveaxis(r, ax, 0) if ax else r).reshape(N_DEV, -1)


def _unshard(g, ax):
    r = jnp.moveaxis(g, 0, ax)
    s = r.shape
    return r.reshape(s[:ax] + (s[ax] * s[ax + 1],) + s[ax + 2:])


def _pack(parts, dtype, row_mult):
    lead = parts[0].shape[:-1]
    flat = jnp.concatenate([p.astype(dtype) for p in parts], axis=-1)
    n = flat.shape[-1]
    per = row_mult * 1024
    tot = -(-n // per) * per
    flat = jnp.pad(flat, [(0, 0)] * len(lead) + [(0, tot - n)])
    return flat.reshape(lead + (tot // 1024, 1024))


def _pad_heads(w, lo, hi):
    K = w.shape[0]
    r = w.reshape(K, N_HEADS, -1)[:, :, lo:hi]
    return jnp.pad(r, ((0, 0), (0, 0), (0, HEAD_PAD - (hi - lo)))).reshape(K, N_HEADS * HEAD_PAD)


def kernel(x, c, positions, ada_w, ada_b, norm_g, mla_w_dq, mla_q_norm_g, mla_w_uq, mla_w_dkv, mla_kv_norm_g, mla_w_ukv, mla_w_o, conv_w_pw1, conv_b_pw1, conv_w_dw, conv_b_dw, conv_ln_g, conv_ln_b, conv_w_pw2, conv_b_pw2, pool_w, pool_b, pool_scale, ffn_w1, ffn_w2, loss_target, m_ada_w, m_ada_b, m_norm_g, m_mla_w_dq, m_mla_q_norm_g, m_mla_w_uq, m_mla_w_dkv, m_mla_kv_norm_g, m_mla_w_ukv, m_mla_w_o, m_conv_w_pw1, m_conv_b_pw1, m_conv_w_dw, m_conv_b_dw, m_conv_ln_g, m_conv_ln_b, m_conv_w_pw2, m_conv_b_pw2, m_pool_w, m_pool_b, m_pool_scale, m_ffn_w1, m_ffn_w2, v_ada_w, v_ada_b, v_norm_g, v_mla_w_dq, v_mla_q_norm_g, v_mla_w_uq, v_mla_w_dkv, v_mla_kv_norm_g, v_mla_w_ukv, v_mla_w_o, v_conv_w_pw1, v_conv_b_pw1, v_conv_w_dw, v_conv_b_dw, v_conv_ln_g, v_conv_ln_b, v_conv_w_pw2, v_conv_b_pw2, v_pool_w, v_pool_b, v_pool_scale, v_ffn_w1, v_ffn_w2):
    args = dict(locals())
    W = {n: args[n] for n, _ in WEIGHTS}
    M1 = {n: args['m_' + n] for n, _ in WEIGHTS}
    V2 = {n: args['v_' + n] for n, _ in WEIGHTS}
    D = D_MODEL
    T = x.shape[1]
    L = ffn_w1.shape[0]
    xi, yi, ci = _place()
    me = 4 * xi + 2 * yi + ci
    n_ada = ada_w.shape[2]

    small_sizes = [W[n].size for n in SMALL]
    small_in = _pack([c.reshape(-1)] + [W[n].reshape(-1) for n in SMALL], F32, 8)
    small_all = _ag_small("ag_small_params", small_in).reshape(N_DEV, -1)
    c_all = small_all[:, :D]
    Ws = {}
    off = D
    for n, sz in zip(SMALL, small_sizes):
        Ws[n] = _unshard(small_all[:, off:off + sz].reshape((N_DEV,) + W[n].shape), SHARD_AXIS[n])
        off += sz
    c16 = jnp.pad(c_all, ((0, 16 - N_DEV), (0, 0)))

    ada_b_cols = lax.dynamic_slice_in_dim(ada_b, me * n_ada, n_ada, axis=1).reshape(L, 1, n_ada)
    mod_part = _mod_part(c16, ada_w, ada_b_cols)[:, :N_DEV]
    mod_all = _ag_small("ag_mod", mod_part.reshape(L * N_DEV, n_ada)).reshape(N_DEV, L, N_DEV, n_ada)
    mod_mine = lax.dynamic_index_in_dim(mod_all, me, axis=2, keepdims=False)
    mod = jnp.transpose(mod_mine, (1, 0, 2)).reshape(L, 6, 1, D)

    mla_names = [n for n in BIG if n.startswith('mla')]
    first_items = [(n, W[n][0]) for n in mla_names]
    later_items = [(n, W[n][1:]) for n in mla_names] + [(n, W[n]) for n in BIG if not n.startswith(('mla', 'ffn'))]
    first_all, = _ag_big("ag_weights", [_pack([a.reshape(-1) for _, a in first_items], MM, 32)])
    wf = [ffn_w1.astype(MM), ffn_w2.astype(MM), _pack([a.reshape(-1) for _, a in later_items], MM, 32)]
    wf, first_all, mod = lax.optimization_barrier((wf, first_all, mod))
    wf_land = [lax.dynamic_update_slice(lax.empty((N_DEV,) + w.shape, MM), w[None], (me,) + (0,) * w.ndim) for w in wf]
    ag_sems, wf_thru, wf_land, ag_token = _copies_start("ag_ffn_start", wf, wf_land, FIRST_LEVEL_PEERS, False)

    def unpack(g, items, dropped):
        flat, out, off = g.reshape(N_DEV, -1), {}, 0
        for n, a in items:
            out[n] = _unshard(flat[:, off:off + a.size].reshape((N_DEV,) + a.shape), SHARD_AXIS[n] - dropped)
            off += a.size
        return out

    n_mla = mla_w_dq.shape[0]
    w_dq, w_uq_p, w_ukv_k, w_ukv_v, w_dkv_p, w_o_p = ([None] * n_mla for _ in range(6))

    def set_mla(j, w):
        w_dq[j] = w['mla_w_dq']
        w_uq_p[j] = _pad_heads(w['mla_w_uq'], 0, QK_NOPE + QK_ROPE)
        w_ukv_k[j] = _pad_heads(w['mla_w_ukv'], 0, QK_NOPE)
        w_ukv_v[j] = _pad_heads(w['mla_w_ukv'], QK_NOPE, QK_NOPE + V_HEAD)
        w_dkv_p[j] = jnp.pad(jnp.concatenate([w['mla_w_dkv'][:, :KV_LORA], jnp.zeros((D, QK_NOPE), MM), w['mla_w_dkv'][:, KV_LORA:]], axis=1),
                             ((0, 0), (0, HEAD_PAD - QK_NOPE - QK_ROPE)))
        w_o_p[j] = jnp.pad(w['mla_w_o'].reshape(N_HEADS, V_HEAD, D), ((0, 0), (0, HEAD_PAD - V_HEAD), (0, 0))).reshape(N_HEADS * HEAD_PAD, D)

    set_mla(0, unpack(first_all, first_items, 1))
    w_dw32 = jnp.pad(Ws['conv_w_dw'], ((0, 0), (0, 32 - CONV_W), (0, 0)))
    row = lambda t: t.reshape(1, -1)

    half = QK_ROPE // 2
    inv_freq = ROPE_THETA ** (-jnp.arange(0, QK_ROPE, 2, dtype=F32) / QK_ROPE)
    invf = jnp.zeros((1, HEAD_PAD), F32).at[0, QK_NOPE:QK_NOPE + half].set(inv_freq).at[0, QK_NOPE + half:QK_NOPE + QK_ROPE].set(inv_freq)
    rC, rS1, rS2 = _rope_tables(positions.reshape(T, 1).astype(F32), invf)

    xs = x.reshape(T, D)
    saved = []
    for i in range(L):
        kind, j = i % 3, i // 3
        sh_m, sc_m, gt_m, sh_f, sc_f, gt_f = (mod[i, r] for r in range(6))
        g = [row(Ws['norm_g'][i, r]) for r in range(4)]
        st = dict(x0=xs)
        if i == 0:
            sc_m = sc_m + ag_token[0:1, 0:1]
        if kind == 0:
            cq_raw, cq, ckv_raw, ckv, q, k, v, h = _mla_proj(f"mla_proj{i}", xs, g[0], sc_m, sh_m, rC, rS1, rS2, w_dq[j],
                                                             row(Ws['mla_q_norm_g'][j]), w_uq_p[j], w_dkv_p[j],
                                                             row(Ws['mla_kv_norm_g'][j]), w_ukv_k[j], w_ukv_v[j])
            o, lse = _attn_fwd(f"attn_fwd{i}", q, k, v)
            y, xs = _mm_post(f"mla_out{i}", o, w_o_p[j], None, xs, g[1], gt_m)
            st.update(h=h, cq_raw=cq_raw, cq=cq, ckv_raw=ckv_raw, ckv=ckv, q=q, k=k, v=v, o=o, lse=lse, y=y)
        elif kind == 1:
            a, u, h = _conv_glu(f"conv_glu{i}", xs, g[0], sc_m, sh_m, w_pw1[j], row(W['conv_b_pw1'][j]))
            uc, z, y, xs = _conv_dw(f"conv_dw{i}", u, w_dw32[j], row(W['conv_b_dw'][j]), row(W['conv_ln_g'][j]), row(W['conv_ln_b'][j]),
                                    w_pw2[j], row(W['conv_b_pw2'][j]), xs, g[1], gt_m)
            st.update(h=h, a=a, u=u, uc=uc, z=z, y=y)
        else:
            h = _prenorm(f"prenorm_m{i}", xs, g[0], sc_m, sh_m, F32)
            p, ypre, y, xs = _pool_fwd(f"pool_fwd{i}", h, w_pool[j], row(Ws['pool_b'][j]), row(Ws['pool_scale'][j]), xs, g[1], gt_m)
            st.update(p=p, ypre=ypre, y=y)
        st['x1'] = xs
        if i == 0:
            wg = _copies_wait("ag_ffn_wait", ag_sems, wf_thru, wf_land, xs, FIRST_LEVEL_PEERS, False)
            w1g, w2g, later_all = _ag_forward("ag_ffn_forward", wg)
            later = unpack(later_all, later_items, 0)
            for jj in range(1, n_mla):
                set_mla(jj, {n: later[n][jj - 1] for n in mla_names})
            w_pw1, w_pw2, w_pool = later['conv_w_pw1'], later['conv_w_pw2'], later['pool_w']
        hf, af, yf, xs = _ffn_fwd(f"ffn_fwd{i}", i, xs, g[2], sc_f, sh_f, w1g, w2g, g[3], gt_f)
        st.update(hf=hf, af=af, yf=yf)
        saved.append(st)

    dx, loss_row = _loss_head(xs, loss_target.reshape(T, D))

    G = {}
    dmod = [None] * L
    dnorm = [None] * L
    rs_pending = None
    ffn_red = [lax.empty(ffn_w1.shape, F32), lax.empty(ffn_w2.shape, F32)]
    for i in reversed(range(L)):
        kind, j = i % 3, i // 3
        sh_m, sc_m, gt_m, sh_f, sc_f, gt_f = (mod[i, r] for r in range(6))
        g = [row(Ws['norm_g'][i, r]) for r in range(4)]
        st = saved[i]
        dy, da, dx, dg3, dgt_f, dsh_f, dsc_f, dg2 = _ffn_bwd(f"ffn_bwd{i}", i, st['yf'], g[3], gt_f, st['af'], w1g, w2g, st['x1'], dx, g[2], sc_f)
        wire1, own1 = _mm_tn_wire(f"ffn_dw1_{i}", st['hf'], da, me, False, False)
        wire2, own2 = _mm_tn_wire(f"ffn_dw2_{i}", st['af'], dy, me, True, True)
        if rs_pending is not None:
            ffn_red = _rs_finish(rs_pending, wire2, me, ffn_red)
        wires = [wire1, wire2]
        rs_sems, wires_thru, rs_lands, rs_token = _copies_start(f"rs_start{i}", wires, [lax.empty(w.shape, MM) for w in wires], ALL_PEERS, True)
        rs_pending = (i, rs_sems, wires_thru, rs_lands, [own1, own2])
        gt_m = gt_m + rs_token[0:1, 0:1]
        if kind == 0:
            dy, do, dg1, dgt_m = _post_bwd_nt(f"mla_do{i}", dx, st['y'], g[1], gt_m, w_o_p[j])
            delta = _attn_delta(f"attn_delta{i}", do, st['o'])
            dq, dk, dv = _attn_bwd(f"attn_bwd{i}", st['q'], st['k'], st['v'], do, st['lse'], delta)
            dq_pre, dcq_raw, dckv_all, dx, dqg, dkvg, dsh_m, dsc_m, dg0 = _mla_proj_bwd(
                f"mla_proj_bwd{i}", dq, dk, dv, rC, rS1, rS2, st['cq_raw'], st['ckv_raw'], st['x0'], dx, w_uq_p[j], w_ukv_k[j], w_ukv_v[j],
                w_dq[j], w_dkv_p[j], row(Ws['mla_q_norm_g'][j]), row(Ws['mla_kv_norm_g'][j]), g[0], sc_m)
            dwo = _mm_tn(f"mla_dwo{i}", st['o'], dy)
            dwuq = _mm_tn(f"mla_dwuq{i}", st['cq'], dq_pre)
            dwk = _mm_tn(f"mla_dwukvk{i}", st['ckv'], dk)
            dwv = _mm_tn(f"mla_dwukvv{i}", st['ckv'], dv)
            dwdq = _mm_tn(f"mla_dwdq{i}", st['h'], dcq_raw)
            dwdkv = _mm_tn(f"mla_dwdkv{i}", st['h'], dckv_all)
            G.setdefault('mla_w_o', [None] * n_mla)[j] = dwo.reshape(N_HEADS, HEAD_PAD, D)[:, :V_HEAD].reshape(N_HEADS * V_HEAD, D)
            G.setdefault('mla_w_uq', [None] * n_mla)[j] = dwuq.reshape(Q_LORA, N_HEADS, HEAD_PAD)[:, :, :QK_NOPE + QK_ROPE].reshape(Q_LORA, -1)
            G.setdefault('mla_w_ukv', [None] * n_mla)[j] = jnp.concatenate(
                [dwk.reshape(KV_LORA, N_HEADS, HEAD_PAD)[:, :, :QK_NOPE], dwv.reshape(KV_LORA, N_HEADS, HEAD_PAD)[:, :, :V_HEAD]], axis=2).reshape(KV_LORA, -1)
            G.setdefault('mla_w_dq', [None] * n_mla)[j] = dwdq
            G.setdefault('mla_w_dkv', [None] * n_mla)[j] = jnp.concatenate([dwdkv[:, :KV_LORA], dwdkv[:, KV_LORA + QK_NOPE:KV_LORA + QK_NOPE + QK_ROPE]], axis=1)
            G.setdefault('mla_q_norm_g', [None] * n_mla)[j] = dqg[0]
            G.setdefault('mla_kv_norm_g', [None] * n_mla)[j] = dkvg[0]
        elif kind == 1:
            duc, dy, dlng, dlnb, dbdw, dg1, dgt_m, dysum = _conv_bwd1(f"conv_bwd1_{i}", dx, st['y'], g[1], gt_m, st['uc'], w_pw2[j],
                                                                      row(W['conv_ln_g'][j]), row(W['conv_ln_b'][j]))
            da, dx, dwdw, dbpw1, dsh_m, dsc_m, dg0 = _conv_bwd2(f"conv_bwd2_{i}", duc, st['u'], st['a'], st['x0'], dx, w_dw32[j], w_pw1[j], g[0], sc_m)
            G['conv_w_pw2'] = [_mm_tn(f"conv_dwpw2_{i}", st['z'], dy)]
            G['conv_w_pw1'] = [_mm_tn(f"conv_dwpw1_{i}", st['h'], da)]
            G['conv_w_dw'] = [dwdw[:CONV_W]]
            G['conv_b_pw1'], G['conv_b_dw'], G['conv_ln_g'], G['conv_ln_b'], G['conv_b_pw2'] = [dbpw1[0]], [dbdw[0]], [dlng[0]], [dlnb[0]], [dysum[0]]
        else:
            dp, dypre, dscale, dpb, dg1, dgt_m = _pool_bwd1(f"pool_bwd1_{i}", dx, st['y'], g[1], gt_m, st['ypre'], row(Ws['pool_scale'][j]), w_pool[j])
            dx, dsh_m, dsc_m, dg0 = _pool_bwd2(f"pool_bwd2_{i}", dp, st['x0'], dx, g[0], sc_m)
            G['pool_w'] = [_mm_tn(f"pool_dw{i}", st['p'], dypre, diag=len(POOL_WINDOWS))]
            G['pool_b'] = [dpb.reshape(len(POOL_WINDOWS), -1)]
            G['pool_scale'] = [dscale[0]]
        dmod[i] = jnp.concatenate([dsh_m, dsc_m, dgt_m, dsh_f, dsc_f, dgt_f], axis=1)
        dnorm[i] = jnp.concatenate([dg0, dg1, dg2, dg3], axis=0)
    G['norm_g'] = dnorm
    grad_x = dx.reshape(x.shape)

    rs_names = [n for n, ax in WEIGHTS if ax is not None and n != 'ada_w' and not n.startswith('ffn')]
    pieces = [(n, _layer_shards(g, SHARD_AXIS[n] - 1)) for n in rs_names for g in G[n]]
    big = [(n, p) for n, p in pieces if p.shape[1] % (8 * 1024) == 0]
    small = [(n, p) for n, p in pieces if p.shape[1] % (8 * 1024) != 0]
    packed = jnp.concatenate([p.reshape(N_DEV, -1, 1024) for _, p in big] + [_pack([p for _, p in small], F32, 8)], axis=1)
    ffn_red = _rs_finish(rs_pending, dx, me, ffn_red)
    my_chip = 2 * xi + yi
    p4 = packed.reshape((4, 2) + packed.shape[1:])
    pair_recv, = _rs_pair("rs_pair", [p4])
    chip_wire, chip_own = _pair_sum("rs_pair_sum", p4, pair_recv, ci, my_chip)

    dmod_mine = jnp.concatenate(dmod, axis=1).reshape(-1)
    fin_in = _pack([dmod_mine] + [G[n][0].reshape(-1) for n in REPL] + [loss_row.reshape(-1)], F32, 8)
    fin_all = _ag_small("ag_final", fin_in)
    chip_wire, fin_all = lax.optimization_barrier((chip_wire, fin_all))
    chip_sems, chip_thru, chip_land, chip_token = _copies_start("rs_chips_start", [chip_wire], [lax.empty(chip_wire.shape, MM)],
                                                                CHIP_PEERS, 'chip')
    g_ffn1, g_ffn2, fin_all, _ = lax.optimization_barrier((ffn_red[0], ffn_red[1], fin_all, chip_token))
    grads = {'ffn_w1': g_ffn1, 'ffn_w2': g_ffn2}
    fin_sum = _sum_devices("final_sum", fin_all).reshape(-1)
    nm = L * 6 * D
    grads['ada_b'] = fin_sum[:nm].reshape(L, 6 * D)
    off = nm
    for n in REPL:
        grads[n] = fin_sum[off:off + W[n].size].reshape(W[n].shape)
        off += W[n].size
    loss = fin_sum[off]
    dmod_all = fin_all.reshape(N_DEV, -1)[:, :nm].reshape(N_DEV, L, 6 * D)
    dmod_cols = lax.dynamic_slice_in_dim(dmod_all, me * n_ada, n_ada, axis=2)
    dmod16 = jnp.pad(jnp.transpose(dmod_cols, (1, 0, 2)), ((0, 0), (0, 16 - N_DEV), (0, 0)))
    grads['ada_w'] = _ada_w_grad(c16, dmod16)
    deltas, new_m, new_v = {}, {}, {}
    for n in ['ada_w', 'ada_b'] + REPL + ['ffn_w1', 'ffn_w2']:
        deltas[n], new_m[n], new_v[n] = _adamw("adamw_" + n, W[n], grads[n], M1[n], V2[n])

    chip_recv, = _copies_wait("rs_chips_wait", chip_sems, chip_thru, chip_land, deltas['ffn_w2'], CHIP_PEERS, 'chip')
    red = _chip_sum("rs_chip_sum", chip_own, chip_recv, my_chip)
    got = {}
    row0 = 0
    for n, p in big:
        rows = p.shape[1] // 1024
        got.setdefault(n, []).append(red[row0:row0 + rows])
        row0 += rows
    tail = red[row0:].reshape(-1)
    off = 0
    for n, p in small:
        got.setdefault(n, []).append(tail[off:off + p.shape[1]])
        off += p.shape[1]
    for n in rs_names:
        grads[n] = jnp.stack([g_.reshape(W[n].shape[1:]) for g_ in got[n]], axis=0)

    for n in rs_names:
        deltas[n], new_m[n], new_v[n] = _adamw("adamw_" + n, W[n], grads[n], M1[n], V2[n])
    names = [n for n, _ in WEIGHTS]
    return (loss, grad_x, *[grads[n] for n in names], *[deltas[n] for n in names], *[new_m[n] for n in names],
            *[new_v[n] for n in names])
```

```python
import functools
import math

import jax
import jax.numpy as jnp
from jax import lax
from jax.experimental import pallas as pl
from jax.experimental.pallas import tpu as pltpu

F32 = jnp.float32
MM = jnp.bfloat16
EPS = 1e-6
NEG = -1e30
N_DEV = 8
VMEM_LIMIT = 48 * 1024 * 1024
MESH = pl.DeviceIdType.MESH

D_MODEL = 1024
N_HEADS = 16
HEAD_PAD = 128
QK_NOPE, QK_ROPE, V_HEAD = 64, 32, 64
Q_LORA, KV_LORA = 384, 256
CHUNK = 64
CONV_W = 31
POOL_WINDOWS = (2, 4, 8, 16)
ROPE_THETA = 10000.0
ATT_SCALE = 1.0 / math.sqrt(QK_NOPE + QK_ROPE)

ADAM_LR, ADAM_B1, ADAM_B2, ADAM_EPS, ADAM_WD, ADAM_STEP = 0.001, 0.9, 0.999, 1e-08, 0.01, 10

WEIGHTS = [('ada_w', 2), ('ada_b', None), ('norm_g', 2), ('mla_w_dq', 1), ('mla_q_norm_g', 1), ('mla_w_uq', 2),
           ('mla_w_dkv', 1), ('mla_kv_norm_g', 1), ('mla_w_ukv', 2), ('mla_w_o', 1), ('conv_w_pw1', 2),
           ('conv_b_pw1', None), ('conv_w_dw', 2), ('conv_b_dw', None), ('conv_ln_g', None), ('conv_ln_b', None),
           ('conv_w_pw2', 1), ('conv_b_pw2', None), ('pool_w', 2), ('pool_b', 2), ('pool_scale', 1),
           ('ffn_w1', 2), ('ffn_w2', 1)]
SHARD_AXIS = dict(WEIGHTS)
BIG = ['mla_w_dq', 'mla_w_uq', 'mla_w_dkv', 'mla_w_ukv', 'mla_w_o', 'conv_w_pw1', 'conv_w_pw2', 'pool_w', 'ffn_w1', 'ffn_w2']
SMALL = ['norm_g', 'mla_q_norm_g', 'mla_kv_norm_g', 'conv_w_dw', 'pool_b', 'pool_scale']
REPL = ['conv_b_pw1', 'conv_b_dw', 'conv_ln_g', 'conv_ln_b', 'conv_b_pw2']


def _dot(a, b):
    return jnp.dot(a.astype(MM), b.astype(MM), preferred_element_type=F32)


def _dot_nt(a, b):
    return lax.dot_general(a.astype(MM), b.astype(MM), (((1,), (1,)), ((), ())), preferred_element_type=F32)


def _dot_tn(a, b):
    return lax.dot_general(a.astype(MM), b.astype(MM), (((0,), (0,)), ((), ())), preferred_element_type=F32)


def _sigmoid(x):
    return 1.0 / (1.0 + jnp.exp(-x))


def _rstd(x):
    return lax.rsqrt(jnp.mean(x * x, axis=-1, keepdims=True) + EPS)


def _rms(x, g):
    return x * _rstd(x) * g


def _rms_bwd(x, g, dout):
    r = _rstd(x)
    xn = x * r
    dg = jnp.sum(dout * xn, axis=0, keepdims=True)
    dxn = dout * g
    dx = r * (dxn - xn * jnp.mean(dxn * xn, axis=-1, keepdims=True))
    return dx, dg


def _prenorm_bwd(x, g0, sc, dh):
    r = _rstd(x)
    xn = x * r
    dsh = jnp.sum(dh, axis=0, keepdims=True)
    dsc = jnp.sum(dh * (xn * g0), axis=0, keepdims=True)
    dn = dh * (1.0 + sc)
    dg0 = jnp.sum(dn * xn, axis=0, keepdims=True)
    dxn = dn * g0
    dx = r * (dxn - xn * jnp.mean(dxn * xn, axis=-1, keepdims=True))
    return dx, dsh, dsc, dg0


def _cparams(sem):
    return pltpu.CompilerParams(dimension_semantics=sem, vmem_limit_bytes=VMEM_LIMIT)


def _rows(name, body, n_rows, tm, rows, consts, outs, accs=(), scratch=()):
    tm = min(tm, n_rows)
    nblk = n_rows // tm
    nr, nc, no, na = len(rows), len(consts), len(outs), len(accs)
    in_specs, args = [], []
    for a, kind in rows:
        if kind == 'cur':
            im = lambda i: (i, 0)
        elif kind == 'prev':
            im = lambda i: (jnp.maximum(i - 1, 0), 0)
        else:
            im = lambda i: (jnp.minimum(i + 1, nblk - 1), 0)
        in_specs.append(pl.BlockSpec((tm, a.shape[1]), im))
        args.append(a)
    for a in consts:
        in_specs.append(pl.BlockSpec(a.shape, lambda i, nd=a.ndim: (0,) * nd))
        args.append(a)
    out_specs = [pl.BlockSpec((tm, c), lambda i: (i, 0)) for c, _ in outs]
    out_specs += [pl.BlockSpec(s, lambda i, nd=len(s): (0,) * nd) for s in accs]
    out_shape = [jax.ShapeDtypeStruct((n_rows, c), dt) for c, dt in outs]
    out_shape += [jax.ShapeDtypeStruct(s, F32) for s in accs]

    def kern(*refs):
        i = pl.program_id(0)
        rr = refs[:nr]
        cc = refs[nr:nr + nc]
        oo = refs[nr + nc:nr + nc + no]
        aa = refs[nr + nc + no:nr + nc + no + na]
        ss = refs[nr + nc + no + na:]

        @pl.when(i == 0)
        def _():
            for a in aa:
                a[...] = jnp.zeros(a.shape, F32)

        body(i, nblk, rr, cc, oo, aa, ss)

    return pl.pallas_call(kern, grid=(nblk,), in_specs=in_specs, out_specs=out_specs, out_shape=out_shape,
                          scratch_shapes=list(scratch), name=name, compiler_params=_cparams(("arbitrary",)))(*args)


def _place():
    return lax.axis_index("x"), lax.axis_index("y"), lax.axis_index("c")


def _ag_small(name, xs):
    R, C = xs.shape

    def body(x_ref, out_ref, send_sems, recv_sems):
        x, y, c = _place()
        me = 4 * x + 2 * y + c
        out_ref[me] = x_ref[...]
        copies = []
        for k in range(1, N_DEV):
            peer = ((1 - x) if k & 4 else x, (1 - y) if k & 2 else y, (1 - c) if k & 1 else c)
            cp = pltpu.make_async_remote_copy(src_ref=x_ref, dst_ref=out_ref.at[me], send_sem=send_sems.at[k - 1],
                                              recv_sem=recv_sems.at[k - 1], device_id=peer, device_id_type=MESH)
            cp.start()
            copies.append(cp)
        for cp in copies:
            cp.wait()

    return pl.pallas_call(
        body, out_shape=jax.ShapeDtypeStruct((N_DEV, R, C), xs.dtype),
        in_specs=[pl.BlockSpec(memory_space=pltpu.VMEM)], out_specs=pl.BlockSpec(memory_space=pltpu.VMEM),
        scratch_shapes=[pltpu.SemaphoreType.DMA((N_DEV - 1,)), pltpu.SemaphoreType.DMA((N_DEV - 1,))], name=name)(xs)


def _ag_big(name, xs):
    nt = len(xs)

    def body(*refs):
        x_refs, out_refs = refs[:nt], refs[nt:2 * nt]
        send_sems, recv_sems, local_sems = refs[2 * nt:]
        x, y, c = _place()
        me, sibling = (x, y, c), (x, y, 1 - c)
        chips = [(1 - x, y), (x, 1 - y), (1 - x, 1 - y)]

        def copy(t, k, block, to, own=False):
            px, py, pc = block
            rows = out_refs[t].at[4 * px + 2 * py + pc]
            return pltpu.make_async_remote_copy(src_ref=x_refs[t] if own else rows, dst_ref=rows, send_sem=send_sems.at[7 * t + k],
                                                recv_sem=recv_sems.at[7 * t + k], device_id=to, device_id_type=MESH)

        mine = [pltpu.make_async_copy(x_refs[t], out_refs[t].at[4 * x + 2 * y + c], local_sems.at[t]) for t in range(nt)]
        for cp in mine:
            cp.start()
        first = []
        for t in range(nt):
            first.append(copy(t, 0, me, sibling, own=True))
            first += [copy(t, 1 + j, me, (*chip, c), own=True) for j, chip in enumerate(chips)]
        for cp in first:
            cp.start()
        passed = []
        for t in range(nt):
            for j, chip in enumerate(chips):
                copy(t, 1 + j, (*chip, c), me).wait_recv()
                cp = copy(t, 4 + j, (*chip, c), sibling)
                cp.start()
                passed.append(cp)
        for t in range(nt):
            copy(t, 0, sibling, me).wait_recv()
            for j, chip in enumerate(chips):
                copy(t, 4 + j, (*chip, 1 - c), me).wait_recv()
        for cp in first + passed:
            cp.wait_send()
        for cp in mine:
            cp.wait()

    hbm = pl.BlockSpec(memory_space=pl.ANY)
    return pl.pallas_call(
        body, out_shape=[jax.ShapeDtypeStruct((N_DEV,) + t.shape, t.dtype) for t in xs],
        in_specs=[hbm] * nt, out_specs=[hbm] * nt,
        scratch_shapes=[pltpu.SemaphoreType.DMA((7 * nt,)), pltpu.SemaphoreType.DMA((7 * nt,)), pltpu.SemaphoreType.DMA((nt,))],
        name=name)(*xs)


def _rs_pair(name, ps):
    nt = len(ps)

    def body(*refs):
        p_refs, recv_refs = refs[:nt], refs[nt:2 * nt]
        send_sems, recv_sems = refs[2 * nt:]
        x, y, c = _place()
        copies = []
        for t in range(nt):
            for j in range(4):
                cp = pltpu.make_async_remote_copy(src_ref=p_refs[t].at[j, 1 - c], dst_ref=recv_refs[t].at[j], send_sem=send_sems.at[4 * t + j],
                                                  recv_sem=recv_sems.at[4 * t + j], device_id=(x, y, 1 - c), device_id_type=MESH)
                cp.start()
                copies.append(cp)
        for cp in copies:
            cp.wait()

    hbm = pl.BlockSpec(memory_space=pl.ANY)
    return pl.pallas_call(
        body, out_shape=[jax.ShapeDtypeStruct((4,) + p.shape[2:], p.dtype) for p in ps], in_specs=[hbm] * nt, out_specs=[hbm] * nt,
        scratch_shapes=[pltpu.SemaphoreType.DMA((4 * nt,)), pltpu.SemaphoreType.DMA((4 * nt,))], name=name)(*ps)


def _rs_chips(name, ss):
    nt = len(ss)

    def body(*refs):
        s_refs, recv_refs = refs[:nt], refs[nt:2 * nt]
        send_sems, recv_sems, local_sems = refs[2 * nt:]
        x, y, c = _place()
        mine = 2 * x + y
        owns = [pltpu.make_async_copy(s_refs[t].at[mine], recv_refs[t].at[mine], local_sems.at[t]) for t in range(nt)]
        for cp in owns:
            cp.start()
        copies = []
        for t in range(nt):
            for k in range(1, 4):
                px = (1 - x) if k & 2 else x
                py = (1 - y) if k & 1 else y
                cp = pltpu.make_async_remote_copy(src_ref=s_refs[t].at[2 * px + py], dst_ref=recv_refs[t].at[mine],
                                                  send_sem=send_sems.at[3 * t + k - 1], recv_sem=recv_sems.at[3 * t + k - 1],
                                                  device_id=(px, py, c), device_id_type=MESH)
                cp.start()
                copies.append(cp)
        for cp in copies:
            cp.wait()
        for cp in owns:
            cp.wait()

    hbm = pl.BlockSpec(memory_space=pl.ANY)
    return pl.pallas_call(
        body, out_shape=[jax.ShapeDtypeStruct(s_.shape, s_.dtype) for s_ in ss], in_specs=[hbm] * nt, out_specs=[hbm] * nt,
        scratch_shapes=[pltpu.SemaphoreType.DMA((3 * nt,)), pltpu.SemaphoreType.DMA((3 * nt,)), pltpu.SemaphoreType.DMA((nt,))],
        name=name)(*ss)


RS_ROWS = 256


def _row_block(r):
    return next(t for t in range(RS_ROWS, 0, -16) if r % t == 0)


def _pair_sum(name, p, recv, my_c, my_chip):
    _, _, r, c = p.shape
    tr = _row_block(r)

    def body(sc_ref, p_ref, r_ref, o_ref, own_ref):
        s = p_ref[...] + r_ref[...]
        o_ref[...] = s.astype(MM)

        @pl.when(pl.program_id(1) == sc_ref[1])
        def _():
            own_ref[...] = s

    return pl.pallas_call(
        body, grid_spec=pltpu.PrefetchScalarGridSpec(
            num_scalar_prefetch=1, grid=(r // tr, 4),
            in_specs=[pl.BlockSpec((None, None, tr, c), lambda i, j, sc: (j, sc[0], i, 0)),
                      pl.BlockSpec((None, tr, c), lambda i, j, sc: (j, i, 0))],
            out_specs=[pl.BlockSpec((None, tr, c), lambda i, j, sc: (j, i, 0)), pl.BlockSpec((tr, c), lambda i, j, sc: (i, 0))]),
        out_shape=[jax.ShapeDtypeStruct((4, r, c), MM), jax.ShapeDtypeStruct((r, c), F32)], name=name,
        compiler_params=_cparams(("arbitrary", "arbitrary")))(jnp.stack([my_c, my_chip]), p, recv)


def _chip_sum(name, own, recv, my_chip):
    _, r, c = recv.shape
    tr = _row_block(r)

    def body(sc_ref, own_ref, r_ref, o_ref):
        acc = jnp.zeros((tr, c), F32)
        for j in range(4):
            acc = acc + jnp.where(sc_ref[0] == j, own_ref[...], r_ref[j].astype(F32))
        o_ref[...] = acc

    return pl.pallas_call(
        body, grid_spec=pltpu.PrefetchScalarGridSpec(
            num_scalar_prefetch=1, grid=(r // tr,),
            in_specs=[pl.BlockSpec((tr, c), lambda i, sc: (i, 0)), pl.BlockSpec((4, tr, c), lambda i, sc: (0, i, 0))],
            out_specs=pl.BlockSpec((tr, c), lambda i, sc: (i, 0))),
        out_shape=jax.ShapeDtypeStruct((r, c), F32), name=name,
        compiler_params=_cparams(("arbitrary",)))(my_chip.reshape(1), own, recv)


def _reduce_scatter(tag, tensors, my_c, my_chip):
    ps = [t.reshape((4, 2) + t.shape[1:]) for t in tensors]
    recv = _rs_pair(tag + "_pair", ps)
    sums = [_pair_sum(f"{tag}_pair_sum{t}", ps[t], recv[t], my_c, my_chip) for t in range(len(ps))]
    recv2 = _rs_chips(tag + "_chips", [s_[0] for s_ in sums])
    return [_chip_sum(f"{tag}_chip_sum{t}", sums[t][1], recv2[t], my_chip) for t in range(len(ps))]


HBM_SPEC = pl.BlockSpec(memory_space=pltpu.HBM)
SEM_SPEC = pl.BlockSpec(memory_space=pltpu.SEMAPHORE)
SPLIT_EFFECT = pltpu.SideEffectType.DATAFLOW_SIDE_EFFECTING
ALL_PEERS = (1, 2, 3, 4, 5, 6, 7)
FIRST_LEVEL_PEERS = (1, 4, 2, 6)
CHIP_PEERS = (4, 2, 6)


def _split_copies(src_refs, land_refs, sems, masks, src_per_peer):
    n, nt = len(masks), len(src_refs)
    x, y, c = _place()
    by_chip = src_per_peer == 'chip'
    slot = 2 * x + y if by_chip else 4 * x + 2 * y + c
    copies = []
    for t in range(nt):
        for k, mask in enumerate(masks):
            px, py, pc = (1 - x) if mask & 4 else x, (1 - y) if mask & 2 else y, (1 - c) if mask & 1 else c
            src = src_refs[t].at[2 * px + py if by_chip else 4 * px + 2 * py + pc] if src_per_peer else src_refs[t]
            copies.append(pltpu.make_async_remote_copy(src_ref=src, dst_ref=land_refs[t].at[slot], send_sem=sems[t * n + k],
                                                       recv_sem=sems[nt * n + t * n + k], device_id=(px, py, pc), device_id_type=MESH))
    return copies


def _copies_start(name, srcs, lands, masks, src_per_peer):
    nt, ns = len(srcs), 2 * len(masks) * len(srcs)

    def body(*refs):
        for cp in _split_copies(refs[:nt], refs[nt:2 * nt], refs[2 * nt:2 * nt + ns], masks, src_per_peer):
            cp.start()
        token = refs[-1]
        token[...] = jnp.zeros(token.shape, F32)

    outs = pl.pallas_call(
        body, name=name,
        out_shape=(pltpu.SemaphoreType.DMA(()),) * ns + tuple(pltpu.HBM(a.shape, a.dtype) for a in list(srcs) + list(lands))
        + (jax.ShapeDtypeStruct((8, 128), F32),),
        in_specs=(HBM_SPEC,) * (2 * nt), out_specs=(SEM_SPEC,) * ns + (HBM_SPEC,) * (2 * nt) + (pl.BlockSpec(memory_space=pltpu.VMEM),),
        input_output_aliases={t: ns + t for t in range(2 * nt)}, compiler_params=pltpu.CompilerParams(has_side_effects=SPLIT_EFFECT))(
            *[pltpu.with_memory_space_constraint(a, pltpu.HBM) for a in list(srcs) + list(lands)])
    return outs[:ns], outs[ns:ns + nt], outs[ns + nt:ns + 2 * nt], outs[-1]


def _copies_wait(name, sems, srcs_thru, lands_thru, after, masks, src_per_peer):
    nt, ns = len(srcs_thru), len(sems)

    def body(*refs):
        for cp in _split_copies(refs[:nt], refs[nt:2 * nt], refs[2 * nt:2 * nt + ns], masks, src_per_peer):
            cp.wait_send()
            cp.wait_recv()

    thru = list(srcs_thru) + list(lands_thru)
    return pl.pallas_call(
        body, name=name, out_shape=tuple(pltpu.HBM(a.shape, a.dtype) for a in thru),
        in_specs=(HBM_SPEC,) * (2 * nt) + (SEM_SPEC,) * ns + (pl.BlockSpec(memory_space=pl.ANY),), out_specs=(HBM_SPEC,) * (2 * nt),
        input_output_aliases={t: t for t in range(2 * nt)}, compiler_params=pltpu.CompilerParams(has_side_effects=SPLIT_EFFECT))(
            *thru, *sems, after)[nt:]


def _ag_forward(name, gs):
    nt = len(gs)

    def body(*refs):
        o_refs, send_sems, recv_sems = refs[nt:2 * nt], refs[2 * nt], refs[2 * nt + 1]
        x, y, c = _place()
        chips = [(1 - x, y), (x, 1 - y), (1 - x, 1 - y)]

        def copy(t, j, pc):
            rows = o_refs[t].at[4 * chips[j][0] + 2 * chips[j][1] + pc]
            return pltpu.make_async_remote_copy(src_ref=rows, dst_ref=rows, send_sem=send_sems.at[3 * t + j], recv_sem=recv_sems.at[3 * t + j],
                                                device_id=(x, y, 1 - c), device_id_type=MESH)

        for t in range(nt):
            for j in range(3):
                copy(t, j, c).start()
        for t in range(nt):
            for j in range(3):
                copy(t, j, c).wait_send()
                copy(t, j, 1 - c).wait_recv()

    hbm = pl.BlockSpec(memory_space=pl.ANY)
    return pl.pallas_call(body, out_shape=[jax.ShapeDtypeStruct(g.shape, g.dtype) for g in gs], in_specs=[hbm] * nt, out_specs=[hbm] * nt,
                          scratch_shapes=[pltpu.SemaphoreType.DMA((3 * nt,)), pltpu.SemaphoreType.DMA((3 * nt,))],
                          input_output_aliases={t: t for t in range(nt)}, name=name)(*gs)


def _mm_tn_wire(name, a, b, me, sqrelu, shard_rows):
    T, M = a.shape
    N = b.shape[1]
    tk = min(2048, T)
    nk = T // tk
    if shard_rows:
        bm, bn = M // N_DEV, N
        a_spec = pl.BlockSpec((tk, 2 * bm), lambda j, k, m: (k, j))
        b_spec = pl.BlockSpec((tk, bn), lambda j, k, m: (k, 0))
        halves = (slice(0, bm), slice(None)), (slice(bm, 2 * bm), slice(None))
        acc_shape = (2 * bm, bn)
    else:
        bm, bn = M, N // N_DEV
        a_spec = pl.BlockSpec((tk, bm), lambda j, k, m: (k, 0))
        b_spec = pl.BlockSpec((tk, 2 * bn), lambda j, k, m: (k, j))
        halves = (slice(None), slice(0, bn)), (slice(None), slice(bn, 2 * bn))
        acc_shape = (bm, 2 * bn)

    def body(me_ref, a_ref, b_ref, wire_ref, own_ref, acc):
        j, k = pl.program_id(0), pl.program_id(1)

        @pl.when(k == 0)
        def _():
            acc[...] = jnp.zeros(acc.shape, F32)

        av = a_ref[...]
        if sqrelu:
            r = jnp.maximum(av, 0.0)
            av = r * r
        acc[...] += _dot_tn(av, b_ref[...])

        for hh in range(2):
            @pl.when(k == nk - 1)
            def _():
                wire_ref[hh] = acc[halves[hh]].astype(MM)

            @pl.when((k == nk - 1) & (2 * j + hh == me_ref[0]))
            def _():
                own_ref[...] = acc[halves[hh]]

    return pl.pallas_call(
        body, grid_spec=pltpu.PrefetchScalarGridSpec(
            num_scalar_prefetch=1, grid=(N_DEV // 2, nk), in_specs=[a_spec, b_spec],
            out_specs=[pl.BlockSpec((2, bm, bn), lambda j, k, m: (j, 0, 0)), pl.BlockSpec((bm, bn), lambda j, k, m: (0, 0))],
            scratch_shapes=[pltpu.VMEM(acc_shape, F32)]),
        out_shape=[jax.ShapeDtypeStruct((N_DEV, bm, bn), MM), jax.ShapeDtypeStruct((bm, bn), F32)], name=name,
        compiler_params=_cparams(("arbitrary", "arbitrary")))(me.reshape(1), a, b)


def _rs_final(name, own, recv, me, stack, li):
    _, r, c = recv.shape
    tr = RS_ROWS

    def body(me_ref, own_ref, r_ref, s_ref, o_ref):
        acc = jnp.zeros((tr, c), F32)
        for j in range(N_DEV):
            acc = acc + jnp.where(me_ref[0] == j, own_ref[...], r_ref[j].astype(F32))
        o_ref[...] = acc

    return pl.pallas_call(
        body, grid_spec=pltpu.PrefetchScalarGridSpec(
            num_scalar_prefetch=1, grid=(r // tr,),
            in_specs=[pl.BlockSpec((tr, c), lambda i, m: (i, 0)), pl.BlockSpec((N_DEV, tr, c), lambda i, m: (0, i, 0)),
                      pl.BlockSpec(memory_space=pl.ANY)],
            out_specs=pl.BlockSpec((None, tr, c), lambda i, m: (li, i, 0))),
        out_shape=jax.ShapeDtypeStruct(stack.shape, F32), input_output_aliases={3: 0}, name=name,
        compiler_params=_cparams(("arbitrary",)))(me.reshape(1), own, recv, stack)


def _rs_finish(pending, after, me, stacks):
    i, sems, wires_thru, lands, owns = pending
    recvs = _copies_wait(f"rs_wait{i}", sems, wires_thru, lands, after, ALL_PEERS, True)
    return [_rs_final(f"rs_final{i}_{t}", owns[t], recvs[t], me, stacks[t], i) for t in range(len(owns))]


def _mod_part(c16, ada_w, ada_b_cols):
    L, D, n = ada_w.shape

    def body(c_ref, w_ref, b_ref, o_ref):
        cv = c_ref[...]
        o_ref[...] = _dot(cv * _sigmoid(cv), w_ref[...]) + b_ref[...]

    return pl.pallas_call(
        body, grid=(L,), in_specs=[pl.BlockSpec((16, D), lambda i: (0, 0)), pl.BlockSpec((None, D, n), lambda i: (i, 0, 0)),
                                   pl.BlockSpec((None, 1, n), lambda i: (i, 0, 0))],
        out_specs=pl.BlockSpec((None, 16, n), lambda i: (i, 0, 0)), out_shape=jax.ShapeDtypeStruct((L, 16, n), F32),
        name="ada_mod", compiler_params=_cparams(("arbitrary",)))(c16, ada_w, ada_b_cols)


def _ada_w_grad(c16, dmod16):
    L, _, n = dmod16.shape
    D = c16.shape[1]

    def body(c_ref, d_ref, o_ref):
        cv = c_ref[...]
        o_ref[...] = _dot_tn(cv * _sigmoid(cv), d_ref[...])

    return pl.pallas_call(
        body, grid=(L,), in_specs=[pl.BlockSpec((16, D), lambda i: (0, 0)), pl.BlockSpec((None, 16, n), lambda i: (i, 0, 0))],
        out_specs=pl.BlockSpec((None, D, n), lambda i: (i, 0, 0)), out_shape=jax.ShapeDtypeStruct((L, D, n), F32),
        name="ada_w_grad", compiler_params=_cparams(("arbitrary",)))(c16, dmod16)


def _sum_devices(name, g):
    _, R, C = g.shape

    def body(g_ref, o_ref):
        acc = g_ref[0]
        for d in range(1, N_DEV):
            acc = acc + g_ref[d]
        o_ref[...] = acc

    return pl.pallas_call(body, out_shape=jax.ShapeDtypeStruct((R, C), F32), name=name)(g)


def _prenorm(name, x, g0, sc, sh, dtype):
    T, D = x.shape

    def body(i, n, rr, cc, oo, aa, ss):
        oo[0][...] = (_rms(rr[0][...], cc[0][...]) * (1.0 + cc[1][...]) + cc[2][...]).astype(dtype)

    return _rows(name, body, T, 512, [(x, 'cur')], [g0, sc, sh], [(D, dtype)])[0]


def _post_bwd_math(d, yv, g1v, gtv):
    dgt = jnp.sum(d * _rms(yv, g1v), axis=0, keepdims=True)
    dy, dg1 = _rms_bwd(yv, g1v, d * gtv)
    return dy, dg1, dgt


def _post_bwd_nt(name, dxo, y, g1, gt, w):
    T, D = y.shape
    K = w.shape[0]

    def body(i, n, rr, cc, oo, aa, ss):
        dy, dg1, dgt = _post_bwd_math(rr[0][...], rr[1][...], cc[0][...], cc[1][...])
        aa[0][...] += dg1
        aa[1][...] += dgt
        dy = dy.astype(MM)
        oo[0][...] = dy
        oo[1][...] = _dot_nt(dy, cc[2][...]).astype(MM)

    return _rows(name, body, T, 512, [(dxo, 'cur'), (y, 'cur')], [g1, gt, w], [(D, MM), (K, MM)], accs=[(1, D)] * 2)


def _mm_post(name, a, w, bias, x, g1, gt):
    T, D = x.shape
    consts = [w, g1, gt] + ([bias] if bias is not None else [])

    def body(i, n, rr, cc, oo, aa, ss):
        y = _dot(rr[0][...], cc[0][...])
        if bias is not None:
            y = y + cc[3][...]
        oo[0][...] = y
        oo[1][...] = rr[1][...] + cc[2][...] * _rms(y, cc[1][...])

    return _rows(name, body, T, 512, [(a, 'cur'), (x, 'cur')], consts, [(D, F32), (D, F32)])


def _mm_tn(name, a, b, sqrelu=False, col_shards=0, diag=0):
    T, M = a.shape
    N = b.shape[1]
    tk = min(512, T)
    nk = T // tk
    if diag:
        bm, bn = M // diag, N // diag
        grid = (diag, 1, nk)
        a_spec = pl.BlockSpec((tk, bm), lambda g, n, k: (k, g))
        b_spec = pl.BlockSpec((tk, bn), lambda g, n, k: (k, g))
        o_spec = pl.BlockSpec((None, bm, bn), lambda g, n, k: (g, 0, 0))
        o_shape = (diag, bm, bn)
    else:
        bm = min(M, 1024)
        bn = N // col_shards if col_shards else min(N, 1024)
        grid = (M // bm, N // bn, nk)
        a_spec = pl.BlockSpec((tk, bm), lambda m, n, k: (k, m))
        b_spec = pl.BlockSpec((tk, bn), lambda m, n, k: (k, n))
        if col_shards:
            o_spec = pl.BlockSpec((None, bm, bn), lambda m, n, k: (n, m, 0))
            o_shape = (col_shards, M, bn)
        else:
            o_spec = pl.BlockSpec((bm, bn), lambda m, n, k: (m, n))
            o_shape = (M, N)

    def body(a_ref, b_ref, o_ref):
        @pl.when(pl.program_id(2) == 0)
        def _():
            o_ref[...] = jnp.zeros(o_ref.shape, F32)

        av = a_ref[...]
        if sqrelu:
            r = jnp.maximum(av, 0.0)
            av = r * r
        o_ref[...] += _dot_tn(av, b_ref[...])

    return pl.pallas_call(body, grid=grid, in_specs=[a_spec, b_spec], out_specs=o_spec,
                          out_shape=jax.ShapeDtypeStruct(o_shape, F32), name=name,
                          compiler_params=_cparams(("arbitrary", "arbitrary", "arbitrary")))(a, b)


FFN_SHARDS = 4
FFN_BWD_SHARDS = 2

def _ffn_fwd(name, li, x, g0, sc, sh, w1g, w2g, g1, gt):
    T, D = x.shape
    nf, tf = w1g.shape[0], w1g.shape[-1]
    F = nf * tf
    tm = min(512, T)

    def body(x_ref, g0_ref, sc_ref, sh_ref, w1_ref, w2_ref, g1_ref, gt_ref, h_ref, a_ref, y_ref, xo_ref, acc):
        f = pl.program_id(1)

        @pl.when(f == 0)
        def _():
            acc[...] = jnp.zeros(acc.shape, F32)
            h_ref[...] = (_rms(x_ref[...], g0_ref[...]) * (1.0 + sc_ref[...]) + sh_ref[...]).astype(MM)

        hv = h_ref[...]
        part = None
        for hh in range(FFN_SHARDS):
            a = _dot(hv, w1_ref[hh])
            a_ref[:, hh * tf:(hh + 1) * tf] = a.astype(MM)
            r = jnp.maximum(a, 0.0)
            p = _dot(r * r, w2_ref[hh])
            part = p if part is None else part + p
        acc[...] += part

        @pl.when(f == nf // FFN_SHARDS - 1)
        def _():
            y = acc[...]
            y_ref[...] = y
            xo_ref[...] = x_ref[...] + gt_ref[...] * _rms(y, g1_ref[...])

    row = lambda t, f: (t, 0)
    one = lambda t, f: (0, 0)
    return pl.pallas_call(
        body, grid=(T // tm, nf // FFN_SHARDS),
        in_specs=[pl.BlockSpec((tm, D), row)] + [pl.BlockSpec((1, D), one)] * 3
        + [pl.BlockSpec((FFN_SHARDS, None, D, tf), lambda t, f: (f, li, 0, 0)), pl.BlockSpec((FFN_SHARDS, None, tf, D), lambda t, f: (f, li, 0, 0)),
           pl.BlockSpec((1, D), one), pl.BlockSpec((1, D), one)],
        out_specs=[pl.BlockSpec((tm, D), row), pl.BlockSpec((tm, FFN_SHARDS * tf), lambda t, f: (t, f)), pl.BlockSpec((tm, D), row),
                   pl.BlockSpec((tm, D), row)],
        out_shape=[jax.ShapeDtypeStruct((T, D), MM), jax.ShapeDtypeStruct((T, F), MM), jax.ShapeDtypeStruct((T, D), F32),
                   jax.ShapeDtypeStruct((T, D), F32)],
        scratch_shapes=[pltpu.VMEM((tm, D), F32)], name=name,
        compiler_params=_cparams(("arbitrary", "arbitrary")))(x, g0, sc, sh, w1g, w2g, g1, gt)


def _ffn_bwd(name, li, y, g1, gt, a, w1g, w2g, x, dxo, g0, sc):
    T, D = x.shape
    nf, tf = w1g.shape[0], w1g.shape[-1]
    F = nf * tf
    tm = min(512, T)
    ns = FFN_BWD_SHARDS

    def body(y_ref, g1_ref, gt_ref, a_ref, w1_ref, w2_ref, x_ref, dxo_ref, g0_ref, sc_ref,
             dy_ref, da_ref, dx_ref, dg1_ref, dgt_ref, dsh_ref, dsc_ref, dg0_ref, acc):
        t, f = pl.program_id(0), pl.program_id(1)

        @pl.when((t == 0) & (f == 0))
        def _():
            for r in (dg1_ref, dgt_ref, dsh_ref, dsc_ref, dg0_ref):
                r[...] = jnp.zeros(r.shape, F32)

        @pl.when(f == 0)
        def _():
            acc[...] = jnp.zeros(acc.shape, F32)
            d, yv, g1v = dxo_ref[...], y_ref[...], g1_ref[...]
            dgt_ref[...] += jnp.sum(d * _rms(yv, g1v), axis=0, keepdims=True)
            dyf, dg1 = _rms_bwd(yv, g1v, d * gt_ref[...])
            dg1_ref[...] += dg1
            dy_ref[...] = dyf.astype(MM)

        dyv = dy_ref[...]
        dyv = dyv + dyv
        part = None
        for hh in range(ns):
            cols = slice(hh * tf, (hh + 1) * tf)
            du = _dot_nt(dyv, w2_ref[hh])
            da = (du * jnp.maximum(a_ref[:, cols], 0.0).astype(F32)).astype(MM)
            da_ref[:, cols] = da
            p = _dot_nt(da, w1_ref[hh])
            part = p if part is None else part + p
        acc[...] += part

        @pl.when(f == nf // ns - 1)
        def _():
            dx, dsh, dsc, dg0 = _prenorm_bwd(x_ref[...], g0_ref[...], sc_ref[...], acc[...])
            dx_ref[...] = dxo_ref[...] + dx
            dsh_ref[...] += dsh
            dsc_ref[...] += dsc
            dg0_ref[...] += dg0

    row = lambda t, f: (t, 0)
    one = lambda t, f: (0, 0)
    blk = lambda t, f: (t, f)
    return pl.pallas_call(
        body, grid=(T // tm, nf // ns),
        in_specs=[pl.BlockSpec((tm, D), row), pl.BlockSpec((1, D), one), pl.BlockSpec((1, D), one), pl.BlockSpec((tm, ns * tf), blk),
                  pl.BlockSpec((ns, None, D, tf), lambda t, f: (f, li, 0, 0)),
                  pl.BlockSpec((ns, None, tf, D), lambda t, f: (f, li, 0, 0)), pl.BlockSpec((tm, D), row), pl.BlockSpec((tm, D), row),
                  pl.BlockSpec((1, D), one), pl.BlockSpec((1, D), one)],
        out_specs=[pl.BlockSpec((tm, D), row), pl.BlockSpec((tm, ns * tf), blk), pl.BlockSpec((tm, D), row)] + [pl.BlockSpec((1, D), one)] * 5,
        out_shape=[jax.ShapeDtypeStruct((T, D), MM), jax.ShapeDtypeStruct((T, F), MM), jax.ShapeDtypeStruct((T, D), F32)]
        + [jax.ShapeDtypeStruct((1, D), F32)] * 5,
        scratch_shapes=[pltpu.VMEM((tm, D), F32)], name=name,
        compiler_params=_cparams(("arbitrary", "arbitrary")))(y, g1, gt, a, w1g, w2g, x, dxo, g0, sc)


def _rope_tables(pos, invf):
    T = pos.shape[0]

    def body(i, n, rr, cc, oo, aa, ss):
        ang = rr[0][...] * cc[0][...]
        lane = lax.broadcasted_iota(jnp.int32, ang.shape, 1)
        cs, sn = jnp.cos(ang), jnp.sin(ang)
        oo[0][...] = jnp.where((lane >= QK_NOPE) & (lane < QK_NOPE + QK_ROPE), cs, 1.0)
        oo[1][...] = jnp.where((lane >= QK_NOPE) & (lane < QK_NOPE + QK_ROPE // 2), -sn, 0.0)
        oo[2][...] = jnp.where((lane >= QK_NOPE + QK_ROPE // 2) & (lane < QK_NOPE + QK_ROPE), sn, 0.0)

    return _rows("rope_tables", body, T, 512, [(pos, 'cur')], [invf], [(HEAD_PAD, F32)] * 3)


def _rope(v, C, S1, S2):
    n = v.shape[1]
    reps = n // HEAD_PAD
    if reps > 1:
        C, S1, S2 = (jnp.tile(t, (1, reps)) for t in (C, S1, S2))
    return v * C + pltpu.roll(v, n - QK_ROPE // 2, 1) * S1 + pltpu.roll(v, QK_ROPE // 2, 1) * S2


def _unrope(d, C, S1, S2):
    n = d.shape[1]
    reps = n // HEAD_PAD
    if reps > 1:
        C, S1, S2 = (jnp.tile(t, (1, reps)) for t in (C, S1, S2))
    return d * C + pltpu.roll(d * S1, QK_ROPE // 2, 1) + pltpu.roll(d * S2, n - QK_ROPE // 2, 1)


def _mla_proj(name, x, g0, sc, sh, C, S1, S2, w_dq, qg, w_uq, w_dkv, kvg, w_ukv_k, w_ukv_v):
    T, D = x.shape
    HP = N_HEADS * HEAD_PAD

    def body(i, n, rr, cc, oo, aa, ss):
        hv = (_rms(rr[0][...], cc[7][...]) * (1.0 + cc[8][...]) + cc[9][...]).astype(MM)
        oo[7][...] = hv
        Cv, S1v, S2v = rr[1][...], rr[2][...], rr[3][...]
        cq_raw = _dot(hv, cc[0][...])
        cq = _rms(cq_raw, cc[1][...]).astype(MM)
        q = _rope(_dot(cq, cc[2][...]), Cv, S1v, S2v)
        ckv_all = _dot(hv, cc[3][...])
        ckv_raw = ckv_all[:, :KV_LORA]
        ckv = _rms(ckv_raw, cc[4][...]).astype(MM)
        kr = _rope(ckv_all[:, KV_LORA:], Cv, S1v, S2v)
        k = _dot(ckv, cc[5][...]) + jnp.tile(kr, (1, N_HEADS))
        v = _dot(ckv, cc[6][...])
        v = jnp.where(lax.broadcasted_iota(jnp.int32, v.shape, 1) % HEAD_PAD == V_HEAD, 1.0, v)
        oo[0][...] = cq_raw
        oo[1][...] = cq
        oo[2][...] = ckv_raw
        oo[3][...] = ckv
        oo[4][...] = q.astype(MM)
        oo[5][...] = k.astype(MM)
        oo[6][...] = v.astype(MM)

    return _rows(name, body, T, 256, [(x, 'cur'), (C, 'cur'), (S1, 'cur'), (S2, 'cur')],
                 [w_dq, qg, w_uq, w_dkv, kvg, w_ukv_k, w_ukv_v, g0, sc, sh],
                 [(Q_LORA, F32), (Q_LORA, MM), (KV_LORA, F32), (KV_LORA, MM), (HP, MM), (HP, MM), (HP, MM), (D, MM)])


ATT_HEADS = 4
ATT_BLOCK = 512
ATT_FWD_BLOCK = 1024


def _chunk_mask_t(tk, tq):
    ki = lax.broadcasted_iota(jnp.int32, (tk, tq), 0) // CHUNK
    qi = lax.broadcasted_iota(jnp.int32, (tk, tq), 1) // CHUNK
    return ki <= qi


def _attn_fwd(name, q, k, v):
    T = q.shape[0]
    tb = min(ATT_FWD_BLOCK, T)
    nb = T // tb
    nh = ATT_HEADS
    hs = [slice(h * HEAD_PAD, (h + 1) * HEAD_PAD) for h in range(nh)]

    def body(q_ref, k_ref, v_ref, o_ref, lse_ref):
        qb = pl.program_id(1)

        def k_block(k0, masked, st):
            new = []
            for h in range(nh):
                m, acc = st[h]
                s = _dot_nt(k_ref[pl.ds(k0, tb), hs[h]], q_ref[:, hs[h]])
                if masked:
                    s = jnp.where(_chunk_mask_t(tb, tb), s, NEG)
                m_new = jnp.maximum(m, jnp.max(s, axis=0, keepdims=True))
                alpha = jnp.exp((m - m_new) * ATT_SCALE)
                p = jnp.exp((s - m_new) * ATT_SCALE)
                acc = alpha * acc + _dot_tn(v_ref[pl.ds(k0, tb), hs[h]], p)
                new.append((m_new, acc))
            return tuple(new)

        st = tuple((jnp.full((1, tb), NEG, F32), jnp.zeros((HEAD_PAD, tb), F32)) for _ in range(nh))
        st = k_block(pl.multiple_of(qb * tb, tb), True, st)
        st = lax.fori_loop(0, qb, lambda kb, s_: k_block(pl.multiple_of(kb * tb, tb), False, s_), st)
        for h in range(nh):
            m, acc = st[h]
            l = acc[V_HEAD:V_HEAD + 1, :]
            o_ref[:, hs[h]] = (acc / l).T.astype(MM)
            lse_ref[h] = jnp.broadcast_to(m * ATT_SCALE + jnp.log(l), (8, tb))

    blk = pl.BlockSpec((tb, nh * HEAD_PAD), lambda g, i: (i, g))
    res = pl.BlockSpec((T, nh * HEAD_PAD), lambda g, i: (0, g))
    return pl.pallas_call(
        body, grid=(N_HEADS // nh, nb), in_specs=[blk, res, res],
        out_specs=[blk, pl.BlockSpec((nh, 8, tb), lambda g, i: (g, 0, i))],
        out_shape=[jax.ShapeDtypeStruct(q.shape, MM), jax.ShapeDtypeStruct((N_HEADS, 8, T), F32)], name=name,
        compiler_params=_cparams(("arbitrary", "arbitrary")))(q, k, v)


def _attn_delta(name, do, o):
    T = do.shape[0]
    tb = min(256, T)

    def body(do_ref, o_ref, d_ref):
        lane = lax.broadcasted_iota(jnp.int32, (tb, HEAD_PAD), 1) // 8
        cols = jnp.zeros((tb, HEAD_PAD), F32)
        for h in range(N_HEADS):
            hsl = slice(h * HEAD_PAD, (h + 1) * HEAD_PAD)
            r = jnp.sum(do_ref[:, hsl].astype(F32) * o_ref[:, hsl].astype(F32), axis=1, keepdims=True)
            cols = jnp.where(lane == h, r, cols)
        d_ref[...] = cols.T

    spec = pl.BlockSpec((tb, N_HEADS * HEAD_PAD), lambda i: (i, 0))
    out = pl.pallas_call(body, grid=(T // tb,), in_specs=[spec, spec], out_specs=pl.BlockSpec((HEAD_PAD, tb), lambda i: (0, i)),
                         out_shape=jax.ShapeDtypeStruct((HEAD_PAD, T), F32), name=name, compiler_params=_cparams(("arbitrary",)))(do, o)
    return out.reshape(N_HEADS, 8, T)


def _attn_bwd(name, q, k, v, do, lse, delta):
    T = q.shape[0]
    tb = min(ATT_BLOCK, T)
    nb = T // tb
    nh = ATT_HEADS
    hs = [slice(h * HEAD_PAD, (h + 1) * HEAD_PAD) for h in range(nh)]

    def body(q_ref, k_ref, v_ref, do_ref, lse_ref, dl_ref, dq_ref, dk_ref, dv_ref, dq_acc, dk_acc, dv_acc):
        kb = pl.program_id(1)

        @pl.when(kb == 0)
        def _():
            dq_acc[...] = jnp.zeros(dq_acc.shape, F32)

        dk_acc[...] = jnp.zeros(dk_acc.shape, F32)
        dv_acc[...] = jnp.zeros(dv_acc.shape, F32)

        def q_block(q0, masked):
            for h in range(nh):
                qh = q_ref[pl.ds(q0, tb), hs[h]]
                doh = do_ref[pl.ds(q0, tb), hs[h]]
                kh = k_ref[:, hs[h]]
                s = _dot_nt(kh, qh) * ATT_SCALE
                if masked:
                    s = jnp.where(_chunk_mask_t(tb, tb), s, NEG)
                p = jnp.exp(s - lse_ref[h, 0:1, pl.ds(q0, tb)])
                ds = (p * (_dot_nt(v_ref[:, hs[h]], doh) - dl_ref[h, 0:1, pl.ds(q0, tb)]) * ATT_SCALE).astype(MM)
                dv_acc[:, hs[h]] += _dot(p, doh)
                dk_acc[:, hs[h]] += _dot(ds, qh)
                dq_acc[pl.ds(q0, tb), hs[h]] += _dot_tn(ds, kh)

        q_block(pl.multiple_of(kb * tb, tb), True)

        def rest(qb, c_):
            q_block(pl.multiple_of(qb * tb, tb), False)
            return c_

        lax.fori_loop(kb + 1, nb, rest, 0)
        dk_ref[...] = dk_acc[...].astype(MM)
        dv_ref[...] = dv_acc[...].astype(MM)

        @pl.when(kb == nb - 1)
        def _():
            dq_ref[...] = dq_acc[...].astype(MM)

    W = nh * HEAD_PAD
    blk = pl.BlockSpec((tb, W), lambda g, i: (i, g))
    res = pl.BlockSpec((T, W), lambda g, i: (0, g))
    rows = pl.BlockSpec((nh, 8, T), lambda g, i: (g, 0, 0))
    return pl.pallas_call(
        body, grid=(N_HEADS // nh, nb), in_specs=[res, blk, blk, res, rows, rows], out_specs=[res, blk, blk],
        out_shape=[jax.ShapeDtypeStruct(q.shape, MM)] * 3,
        scratch_shapes=[pltpu.VMEM((T, W), F32), pltpu.VMEM((tb, W), F32), pltpu.VMEM((tb, W), F32)],
        name=name, compiler_params=_cparams(("arbitrary", "arbitrary")))(q, k, v, do, lse, delta)


def _mla_proj_bwd(name, dq, dk, dv, C, S1, S2, cq_raw, ckv_raw, x, dxo, w_uq, w_ukv_k, w_ukv_v, w_dq, w_dkv, qg, kvg, g0, sc):
    T, D = x.shape
    HP = N_HEADS * HEAD_PAD

    def body(i, n, rr, cc, oo, aa, ss):
        Cv, S1v, S2v = rr[3][...], rr[4][...], rr[5][...]
        dq_pre = _unrope(rr[0][...].astype(F32), Cv, S1v, S2v).astype(MM)
        oo[0][...] = dq_pre
        dcq = _dot_nt(dq_pre, cc[0][...])
        dcq_raw, dqg = _rms_bwd(rr[6][...], cc[5][...], dcq)
        aa[0][...] += dqg
        dcq_raw = dcq_raw.astype(MM)
        oo[1][...] = dcq_raw
        dkv = rr[1][...]
        dkr = dkv[:, :HEAD_PAD].astype(F32)
        for hh in range(1, N_HEADS):
            dkr = dkr + dkv[:, hh * HEAD_PAD:(hh + 1) * HEAD_PAD].astype(F32)
        lane = lax.broadcasted_iota(jnp.int32, dkr.shape, 1)
        dkr = jnp.where((lane >= QK_NOPE) & (lane < QK_NOPE + QK_ROPE), _unrope(dkr, Cv, S1v, S2v), 0.0)
        dckv = _dot_nt(dkv, cc[1][...]) + _dot_nt(rr[2][...], cc[2][...])
        dckv_raw, dkvg = _rms_bwd(rr[7][...], cc[6][...], dckv)
        aa[1][...] += dkvg
        dckv_all = jnp.concatenate([dckv_raw, dkr], axis=1).astype(MM)
        oo[2][...] = dckv_all
        dh = _dot_nt(dcq_raw, cc[3][...]) + _dot_nt(dckv_all, cc[4][...])
        dx, dsh, dsc, dg0 = _prenorm_bwd(rr[8][...], cc[7][...], cc[8][...], dh)
        oo[3][...] = rr[9][...] + dx
        aa[2][...] += dsh
        aa[3][...] += dsc
        aa[4][...] += dg0

    return _rows(name, body, T, 256,
                 [(dq, 'cur'), (dk, 'cur'), (dv, 'cur'), (C, 'cur'), (S1, 'cur'), (S2, 'cur'), (cq_raw, 'cur'), (ckv_raw, 'cur'),
                  (x, 'cur'), (dxo, 'cur')],
                 [w_uq, w_ukv_k, w_ukv_v, w_dq, w_dkv, qg, kvg, g0, sc],
                 [(HP, MM), (Q_LORA, MM), (KV_LORA + HEAD_PAD, MM), (D, F32)],
                 accs=[(1, Q_LORA), (1, KV_LORA), (1, D), (1, D), (1, D)])


HALO = 32


def _windows(ext, tm, first):
    rolled = {0: ext}
    out = []
    for j in range(CONV_W):
        r = (first + j) % 8
        if r not in rolled:
            rolled[r] = pltpu.roll(ext, ext.shape[0] - r, 0)
        out.append(rolled[r][first + j - r:first + j - r + tm])
    return out


def _conv_glu(name, x, g0, sc, sh, w_pw1, b_pw1):
    T, D = x.shape

    def body(i, n, rr, cc, oo, aa, ss):
        hv = (_rms(rr[0][...], cc[2][...]) * (1.0 + cc[3][...]) + cc[4][...]).astype(MM)
        oo[2][...] = hv
        a = _dot(hv, cc[0][...]) + cc[1][...]
        oo[0][...] = a
        oo[1][...] = a[:, :D] * _sigmoid(a[:, D:])

    return _rows(name, body, T, 512, [(x, 'cur')], [w_pw1, b_pw1, g0, sc, sh], [(2 * D, F32), (D, F32), (D, MM)])


def _layernorm_parts(uc):
    xc = uc - jnp.mean(uc, axis=-1, keepdims=True)
    r = lax.rsqrt(jnp.mean(xc * xc, axis=-1, keepdims=True) + EPS)
    return xc * r, r


def _conv_dw(name, u, w_dw, b_dw, ln_g, ln_b, w_pw2, b_pw2, x, g1, gt):
    T, D = u.shape
    tm = min(256, T)

    def body(i, n, rr, cc, oo, aa, ss):
        ext = jnp.concatenate([jnp.where(i > 0, rr[1][tm - HALO:tm, :], 0.0), rr[0][...]], axis=0)
        uc = jnp.zeros((tm, D), F32) + cc[1][...]
        for kk, win in enumerate(_windows(ext, tm, HALO - (CONV_W - 1))):
            uc = uc + win * cc[0][kk:kk + 1, :]
        xh, _ = _layernorm_parts(uc)
        ln = xh * cc[2][...] + cc[3][...]
        z = (ln * _sigmoid(ln)).astype(MM)
        y = _dot(z, cc[4][...]) + cc[5][...]
        oo[0][...] = uc
        oo[1][...] = z
        oo[2][...] = y
        oo[3][...] = rr[2][...] + cc[7][...] * _rms(y, cc[6][...])

    return _rows(name, body, T, tm, [(u, 'cur'), (u, 'prev'), (x, 'cur')], [w_dw, b_dw, ln_g, ln_b, w_pw2, b_pw2, g1, gt],
                 [(D, F32), (D, MM), (D, F32), (D, F32)])


def _conv_bwd1(name, dxo, y, g1, gt, uc, w_pw2, ln_g, ln_b):
    T, D = uc.shape

    def body(i, n, rr, cc, oo, aa, ss):
        dy, dg1, dgt = _post_bwd_math(rr[0][...], rr[1][...], cc[3][...], cc[4][...])
        aa[3][...] += dg1
        aa[4][...] += dgt
        aa[5][...] += jnp.sum(dy, axis=0, keepdims=True)
        dy = dy.astype(MM)
        oo[1][...] = dy
        dz = _dot_nt(dy, cc[0][...])
        xh, r = _layernorm_parts(rr[2][...])
        g = cc[1][...]
        ln = xh * g + cc[2][...]
        sg = _sigmoid(ln)
        dln = dz * (sg * (1.0 + ln * (1.0 - sg)))
        aa[0][...] += jnp.sum(dln * xh, axis=0, keepdims=True)
        aa[1][...] += jnp.sum(dln, axis=0, keepdims=True)
        dxh = dln * g
        duc = r * (dxh - jnp.mean(dxh, axis=-1, keepdims=True) - xh * jnp.mean(dxh * xh, axis=-1, keepdims=True))
        aa[2][...] += jnp.sum(duc, axis=0, keepdims=True)
        oo[0][...] = duc

    return _rows(name, body, T, 256, [(dxo, 'cur'), (y, 'cur'), (uc, 'cur')], [w_pw2, ln_g, ln_b, g1, gt], [(D, F32), (D, MM)],
                 accs=[(1, D)] * 6)


def _conv_bwd2(name, duc, u, a, x, dxo, w_dw, w_pw1, g0, sc):
    T, D = u.shape
    tm = min(256, T)

    def body(i, n, rr, cc, oo, aa, ss):
        dcur = rr[0][...]
        extd = jnp.concatenate([dcur, jnp.where(i < n - 1, rr[1][0:HALO, :], 0.0)], axis=0)
        extu = jnp.concatenate([jnp.where(i > 0, rr[3][tm - HALO:tm, :], 0.0), rr[2][...]], axis=0)
        wd = _windows(extd, tm, 0)
        wu = _windows(extu, tm, HALO - (CONV_W - 1))
        du = jnp.zeros((tm, D), F32)
        for kk in range(CONV_W):
            du = du + wd[CONV_W - 1 - kk] * cc[0][kk:kk + 1, :]
            aa[0][kk:kk + 1, :] += jnp.sum(dcur * wu[kk], axis=0, keepdims=True)
        av = rr[4][...]
        a1, sg = av[:, :D], _sigmoid(av[:, D:])
        da = jnp.concatenate([du * sg, du * a1 * (sg * (1.0 - sg))], axis=1)
        aa[1][...] += jnp.sum(da, axis=0, keepdims=True)
        da = da.astype(MM)
        oo[0][...] = da
        dx, dsh, dsc, dg0 = _prenorm_bwd(rr[5][...], cc[2][...], cc[3][...], _dot_nt(da, cc[1][...]))
        oo[1][...] = rr[6][...] + dx
        aa[2][...] += dsh
        aa[3][...] += dsc
        aa[4][...] += dg0

    return _rows(name, body, T, tm,
                 [(duc, 'cur'), (duc, 'next'), (u, 'cur'), (u, 'prev'), (a, 'cur'), (x, 'cur'), (dxo, 'cur')],
                 [w_dw, w_pw1, g0, sc], [(2 * D, MM), (D, F32)],
                 accs=[(32, D), (1, 2 * D), (1, D), (1, D), (1, D)])


PHALO = 16


def _pool_fwd(name, h, w, b, scale, x, g1, gt):
    T, D = h.shape
    G = len(POOL_WINDOWS)
    Cg = D // G
    tm = min(256, T)

    def body(i, n, rr, cc, oo, aa, ss):
        ext = ss[0]
        ext[0:PHALO, :] = jnp.where(i > 0, rr[1][tm - PHALO:tm, :], 0.0)
        ext[PHALO:PHALO + tm, :] = rr[0][...]
        t_glob = i * tm + lax.broadcasted_iota(jnp.int32, (tm, 1), 0)
        ps, ys = [], []
        for g, win in enumerate(POOL_WINDOWS):
            cols = slice(g * Cg, (g + 1) * Cg)
            s = ext[pl.ds(PHALO, tm), cols]
            for j in range(1, win):
                s = s + ext[pl.ds(PHALO - j, tm), cols]
            cnt = jnp.minimum(t_glob + 1, win).astype(F32)
            p = (s / cnt - ext[pl.ds(PHALO, tm), cols]).astype(MM)
            ps.append(p)
            ys.append(_dot(p, cc[0][g]) + cc[1][:, cols])
        ypre = jnp.concatenate(ys, axis=1)
        y = ypre * cc[2][...]
        oo[0][...] = jnp.concatenate(ps, axis=1)
        oo[1][...] = ypre
        oo[2][...] = y
        oo[3][...] = rr[2][...] + cc[4][...] * _rms(y, cc[3][...])

    return _rows(name, body, T, tm, [(h, 'cur'), (h, 'prev'), (x, 'cur')], [w, b, scale, g1, gt],
                 [(D, MM), (D, F32), (D, F32), (D, F32)], scratch=[pltpu.VMEM((tm + PHALO, D), F32)])


def _pool_bwd1(name, dxo, y, g1, gt, ypre, scale, w):
    T, D = ypre.shape
    G = len(POOL_WINDOWS)
    Cg = D // G

    def body(i, n, rr, cc, oo, aa, ss):
        dyv, dg1, dgt = _post_bwd_math(rr[0][...], rr[1][...], cc[2][...], cc[3][...])
        aa[2][...] += dg1
        aa[3][...] += dgt
        aa[0][...] += jnp.sum(dyv * rr[2][...], axis=0, keepdims=True)
        dypre = dyv * cc[0][...]
        aa[1][...] += jnp.sum(dypre, axis=0, keepdims=True)
        dypre = dypre.astype(MM)
        oo[1][...] = dypre
        oo[0][...] = jnp.concatenate([_dot_nt(dypre[:, g * Cg:(g + 1) * Cg], cc[1][g]) for g in range(G)], axis=1)

    return _rows(name, body, T, 256, [(dxo, 'cur'), (y, 'cur'), (ypre, 'cur')], [scale, w, g1, gt], [(D, F32), (D, MM)],
                 accs=[(1, D)] * 4)


def _pool_bwd2(name, dp, x, dxo, g0, sc):
    T, D = x.shape
    G = len(POOL_WINDOWS)
    Cg = D // G
    tm = min(256, T)

    def body(i, n, rr, cc, oo, aa, ss):
        ext = ss[0]
        t_glob = i * tm + lax.broadcasted_iota(jnp.int32, (tm, 1), 0)
        dcur = rr[0][...]
        dhs = []
        for g, win in enumerate(POOL_WINDOWS):
            cols = slice(g * Cg, (g + 1) * Cg)
            cnt = jnp.minimum(t_glob + 1, win).astype(F32)
            ext[0:tm, cols] = dcur[:, cols] / cnt
            ext[tm:tm + PHALO, cols] = jnp.where(i < n - 1, rr[1][0:PHALO, cols] * (1.0 / win), 0.0)
        for g, win in enumerate(POOL_WINDOWS):
            cols = slice(g * Cg, (g + 1) * Cg)
            s = ext[pl.ds(0, tm), cols]
            for j in range(1, win):
                s = s + ext[pl.ds(j, tm), cols]
            dhs.append(s - dcur[:, cols])
        dx, dsh, dsc, dg0 = _prenorm_bwd(rr[2][...], cc[0][...], cc[1][...], jnp.concatenate(dhs, axis=1))
        oo[0][...] = rr[3][...] + dx
        aa[0][...] += dsh
        aa[1][...] += dsc
        aa[2][...] += dg0

    return _rows(name, body, T, tm, [(dp, 'cur'), (dp, 'next'), (x, 'cur'), (dxo, 'cur')], [g0, sc], [(D, F32)],
                 accs=[(1, D)] * 3, scratch=[pltpu.VMEM((tm + PHALO, D), F32)])


def _loss_head(x, tgt):
    T, D = x.shape

    def body(i, n, rr, cc, oo, aa, ss):
        err = rr[0][...] - rr[1][...]
        oo[0][...] = err * (1.0 / D)
        aa[0][...] += jnp.sum(err * err, axis=0, keepdims=True)

        @pl.when(i == n - 1)
        def _():
            aa[1][...] = jnp.broadcast_to(jnp.sum(aa[0][...], axis=1, keepdims=True) * (0.5 / D), (1, 128))

    dx, _, loss_row = _rows("loss_head", body, T, 512, [(x, 'cur'), (tgt, 'cur')], [], [(D, F32)], accs=[(1, D), (1, 128)])
    return dx, loss_row


def _adamw(name, w, g, m, v):
    shape = w.shape
    C = shape[-1]
    R = w.size // C
    w2, g2, m2, v2 = (t.reshape(R, C) for t in (w, g, m, v))
    br = R
    if R * C * 4 > (1 << 20):
        br = 8
        while br * 2 * C * 4 <= (1 << 20) and R % (br * 2) == 0:
            br *= 2
    b1c = 1.0 - ADAM_B1 ** ADAM_STEP
    b2c = 1.0 - ADAM_B2 ** ADAM_STEP

    def body(w_ref, g_ref, m_ref, v_ref, d_ref, mo_ref, vo_ref):
        gv = g_ref[...]
        mn = ADAM_B1 * m_ref[...] + (1.0 - ADAM_B1) * gv
        vn = ADAM_B2 * v_ref[...] + (1.0 - ADAM_B2) * (gv * gv)
        d_ref[...] = -ADAM_LR * ((mn / b1c) / (jnp.sqrt(vn / b2c) + ADAM_EPS) + ADAM_WD * w_ref[...])
        mo_ref[...] = mn
        vo_ref[...] = vn

    spec = pl.BlockSpec((br, C), lambda r: (r, 0))
    outs = pl.pallas_call(body, grid=(R // br,), in_specs=[spec] * 4, out_specs=[spec] * 3,
                          out_shape=[jax.ShapeDtypeStruct((R, C), F32)] * 3, name=name,
                          compiler_params=_cparams(("arbitrary",)))(w2, g2, m2, v2)
    return tuple(t.reshape(shape) for t in outs)


def _layer_shards(g, ax):
    s = g.shape
    r = g.reshape(s[:ax] + (N_DEV, s[ax] // N_DEV) + s[ax + 1:])
    return (jnp.moveaxis(r, ax, 0) if ax else r).reshape(N_DEV, -1)


def _unshard(g, ax):
    r = jnp.moveaxis(g, 0, ax)
    s = r.shape
    return r.reshape(s[:ax] + (s[ax] * s[ax + 1],) + s[ax + 2:])


def _pack(parts, dtype, row_mult):
    lead = parts[0].shape[:-1]
    flat = jnp.concatenate([p.astype(dtype) for p in parts], axis=-1)
    n = flat.shape[-1]
    per = row_mult * 1024
    tot = -(-n // per) * per
    flat = jnp.pad(flat, [(0, 0)] * len(lead) + [(0, tot - n)])
    return flat.reshape(lead + (tot // 1024, 1024))


def _pad_heads(w, lo, hi):
    K = w.shape[0]
    r = w.reshape(K, N_HEADS, -1)[:, :, lo:hi]
    return jnp.pad(r, ((0, 0), (0, 0), (0, HEAD_PAD - (hi - lo)))).reshape(K, N_HEADS * HEAD_PAD)


def kernel(x, c, positions, ada_w, ada_b, norm_g, mla_w_dq, mla_q_norm_g, mla_w_uq, mla_w_dkv, mla_kv_norm_g, mla_w_ukv, mla_w_o, conv_w_pw1, conv_b_pw1, conv_w_dw, conv_b_dw, conv_ln_g, conv_ln_b, conv_w_pw2, conv_b_pw2, pool_w, pool_b, pool_scale, ffn_w1, ffn_w2, loss_target, m_ada_w, m_ada_b, m_norm_g, m_mla_w_dq, m_mla_q_norm_g, m_mla_w_uq, m_mla_w_dkv, m_mla_kv_norm_g, m_mla_w_ukv, m_mla_w_o, m_conv_w_pw1, m_conv_b_pw1, m_conv_w_dw, m_conv_b_dw, m_conv_ln_g, m_conv_ln_b, m_conv_w_pw2, m_conv_b_pw2, m_pool_w, m_pool_b, m_pool_scale, m_ffn_w1, m_ffn_w2, v_ada_w, v_ada_b, v_norm_g, v_mla_w_dq, v_mla_q_norm_g, v_mla_w_uq, v_mla_w_dkv, v_mla_kv_norm_g, v_mla_w_ukv, v_mla_w_o, v_conv_w_pw1, v_conv_b_pw1, v_conv_w_dw, v_conv_b_dw, v_conv_ln_g, v_conv_ln_b, v_conv_w_pw2, v_conv_b_pw2, v_pool_w, v_pool_b, v_pool_scale, v_ffn_w1, v_ffn_w2):
    args = dict(locals())
    W = {n: args[n] for n, _ in WEIGHTS}
    M1 = {n: args['m_' + n] for n, _ in WEIGHTS}
    V2 = {n: args['v_' + n] for n, _ in WEIGHTS}
    D = D_MODEL
    T = x.shape[1]
    L = ffn_w1.shape[0]
    xi, yi, ci = _place()
    me = 4 * xi + 2 * yi + ci
    n_ada = ada_w.shape[2]

    small_sizes = [W[n].size for n in SMALL]
    small_in = _pack([c.reshape(-1)] + [W[n].reshape(-1) for n in SMALL], F32, 8)
    small_all = _ag_small("ag_small_params", small_in).reshape(N_DEV, -1)
    c_all = small_all[:, :D]
    Ws = {}
    off = D
    for n, sz in zip(SMALL, small_sizes):
        Ws[n] = _unshard(small_all[:, off:off + sz].reshape((N_DEV,) + W[n].shape), SHARD_AXIS[n])
        off += sz
    c16 = jnp.pad(c_all, ((0, 16 - N_DEV), (0, 0)))

    ada_b_cols = lax.dynamic_slice_in_dim(ada_b, me * n_ada, n_ada, axis=1).reshape(L, 1, n_ada)
    mod_part = _mod_part(c16, ada_w, ada_b_cols)[:, :N_DEV]
    mod_all = _ag_small("ag_mod", mod_part.reshape(L * N_DEV, n_ada)).reshape(N_DEV, L, N_DEV, n_ada)
    mod_mine = lax.dynamic_index_in_dim(mod_all, me, axis=2, keepdims=False)
    mod = jnp.transpose(mod_mine, (1, 0, 2)).reshape(L, 6, 1, D)

    mla_names = [n for n in BIG if n.startswith('mla')]
    first_items = [(n, W[n][0]) for n in mla_names]
    later_items = [(n, W[n][1:]) for n in mla_names] + [(n, W[n]) for n in BIG if not n.startswith(('mla', 'ffn'))]
    first_all, = _ag_big("ag_weights", [_pack([a.reshape(-1) for _, a in first_items], MM, 32)])
    wf = [ffn_w1.astype(MM), ffn_w2.astype(MM), _pack([a.reshape(-1) for _, a in later_items], MM, 32)]
    wf, first_all, mod = lax.optimization_barrier((wf, first_all, mod))
    wf_land = [lax.dynamic_update_slice(lax.empty((N_DEV,) + w.shape, MM), w[None], (me,) + (0,) * w.ndim) for w in wf]
    ag_sems, wf_thru, wf_land, ag_token = _copies_start("ag_ffn_start", wf, wf_land, FIRST_LEVEL_PEERS, False)

    def unpack(g, items, dropped):
        flat, out, off = g.reshape(N_DEV, -1), {}, 0
        for n, a in items:
            out[n] = _unshard(flat[:, off:off + a.size].reshape((N_DEV,) + a.shape), SHARD_AXIS[n] - dropped)
            off += a.size
        return out

    n_mla = mla_w_dq.shape[0]
    w_dq, w_uq_p, w_ukv_k, w_ukv_v, w_dkv_p, w_o_p = ([None] * n_mla for _ in range(6))

    def set_mla(j, w):
        w_dq[j] = w['mla_w_dq']
        w_uq_p[j] = _pad_heads(w['mla_w_uq'], 0, QK_NOPE + QK_ROPE)
        w_ukv_k[j] = _pad_heads(w['mla_w_ukv'], 0, QK_NOPE)
        w_ukv_v[j] = _pad_heads(w['mla_w_ukv'], QK_NOPE, QK_NOPE + V_HEAD)
        w_dkv_p[j] = jnp.pad(jnp.concatenate([w['mla_w_dkv'][:, :KV_LORA], jnp.zeros((D, QK_NOPE), MM), w['mla_w_dkv'][:, KV_LORA:]], axis=1),
                             ((0, 0), (0, HEAD_PAD - QK_NOPE - QK_ROPE)))
        w_o_p[j] = jnp.pad(w['mla_w_o'].reshape(N_HEADS, V_HEAD, D), ((0, 0), (0, HEAD_PAD - V_HEAD), (0, 0))).reshape(N_HEADS * HEAD_PAD, D)

    set_mla(0, unpack(first_all, first_items, 1))
    w_dw32 = jnp.pad(Ws['conv_w_dw'], ((0, 0), (0, 32 - CONV_W), (0, 0)))
    row = lambda t: t.reshape(1, -1)

    half = QK_ROPE // 2
    inv_freq = ROPE_THETA ** (-jnp.arange(0, QK_ROPE, 2, dtype=F32) / QK_ROPE)
    invf = jnp.zeros((1, HEAD_PAD), F32).at[0, QK_NOPE:QK_NOPE + half].set(inv_freq).at[0, QK_NOPE + half:QK_NOPE + QK_ROPE].set(inv_freq)
    rC, rS1, rS2 = _rope_tables(positions.reshape(T, 1).astype(F32), invf)

    xs = x.reshape(T, D)
    saved = []
    for i in range(L):
        kind, j = i % 3, i // 3
        sh_m, sc_m, gt_m, sh_f, sc_f, gt_f = (mod[i, r] for r in range(6))
        g = [row(Ws['norm_g'][i, r]) for r in range(4)]
        st = dict(x0=xs)
        if i == 0:
            sc_m = sc_m + ag_token[0:1, 0:1]
        if kind == 0:
            cq_raw, cq, ckv_raw, ckv, q, k, v, h = _mla_proj(f"mla_proj{i}", xs, g[0], sc_m, sh_m, rC, rS1, rS2, w_dq[j],
                                                             row(Ws['mla_q_norm_g'][j]), w_uq_p[j], w_dkv_p[j],
                                                             row(Ws['mla_kv_norm_g'][j]), w_ukv_k[j], w_ukv_v[j])
            o, lse = _attn_fwd(f"attn_fwd{i}", q, k, v)
            y, xs = _mm_post(f"mla_out{i}", o, w_o_p[j], None, xs, g[1], gt_m)
            st.update(h=h, cq_raw=cq_raw, cq=cq, ckv_raw=ckv_raw, ckv=ckv, q=q, k=k, v=v, o=o, lse=lse, y=y)
        elif kind == 1:
            a, u, h = _conv_glu(f"conv_glu{i}", xs, g[0], sc_m, sh_m, w_pw1[j], row(W['conv_b_pw1'][j]))
            uc, z, y, xs = _conv_dw(f"conv_dw{i}", u, w_dw32[j], row(W['conv_b_dw'][j]), row(W['conv_ln_g'][j]), row(W['conv_ln_b'][j]),
                                    w_pw2[j], row(W['conv_b_pw2'][j]), xs, g[1], gt_m)
            st.update(h=h, a=a, u=u, uc=uc, z=z, y=y)
        else:
            h = _prenorm(f"prenorm_m{i}", xs, g[0], sc_m, sh_m, F32)
            p, ypre, y, xs = _pool_fwd(f"pool_fwd{i}", h, w_pool[j], row(Ws['pool_b'][j]), row(Ws['pool_scale'][j]), xs, g[1], gt_m)
            st.update(p=p, ypre=ypre, y=y)
        st['x1'] = xs
        if i == 0:
            wg = _copies_wait("ag_ffn_wait", ag_sems, wf_thru, wf_land, xs, FIRST_LEVEL_PEERS, False)
            w1g, w2g, later_all = _ag_forward("ag_ffn_forward", wg)
            later = unpack(later_all, later_items, 0)
            for jj in range(1, n_mla):
                set_mla(jj, {n: later[n][jj - 1] for n in mla_names})
            w_pw1, w_pw2, w_pool = later['conv_w_pw1'], later['conv_w_pw2'], later['pool_w']
        hf, af, yf, xs = _ffn_fwd(f"ffn_fwd{i}", i, xs, g[2], sc_f, sh_f, w1g, w2g, g[3], gt_f)
        st.update(hf=hf, af=af, yf=yf)
        saved.append(st)

    dx, loss_row = _loss_head(xs, loss_target.reshape(T, D))

    G = {}
    dmod = [None] * L
    dnorm = [None] * L
    rs_pending = None
    ffn_red = [lax.empty(ffn_w1.shape, F32), lax.empty(ffn_w2.shape, F32)]
    for i in reversed(range(L)):
        kind, j = i % 3, i // 3
        sh_m, sc_m, gt_m, sh_f, sc_f, gt_f = (mod[i, r] for r in range(6))
        g = [row(Ws['norm_g'][i, r]) for r in range(4)]
        st = saved[i]
        dy, da, dx, dg3, dgt_f, dsh_f, dsc_f, dg2 = _ffn_bwd(f"ffn_bwd{i}", i, st['yf'], g[3], gt_f, st['af'], w1g, w2g, st['x1'], dx, g[2], sc_f)
        wire1, own1 = _mm_tn_wire(f"ffn_dw1_{i}", st['hf'], da, me, False, False)
        wire2, own2 = _mm_tn_wire(f"ffn_dw2_{i}", st['af'], dy, me, True, True)
        if rs_pending is not None:
            ffn_red = _rs_finish(rs_pending, wire2, me, ffn_red)
        wires = [wire1, wire2]
        rs_sems, wires_thru, rs_lands, rs_token = _copies_start(f"rs_start{i}", wires, [lax.empty(w.shape, MM) for w in wires], ALL_PEERS, True)
        rs_pending = (i, rs_sems, wires_thru, rs_lands, [own1, own2])
        gt_m = gt_m + rs_token[0:1, 0:1]
        if kind == 0:
            dy, do, dg1, dgt_m = _post_bwd_nt(f"mla_do{i}", dx, st['y'], g[1], gt_m, w_o_p[j])
            delta = _attn_delta(f"attn_delta{i}", do, st['o'])
            dq, dk, dv = _attn_bwd(f"attn_bwd{i}", st['q'], st['k'], st['v'], do, st['lse'], delta)
            dq_pre, dcq_raw, dckv_all, dx, dqg, dkvg, dsh_m, dsc_m, dg0 = _mla_proj_bwd(
                f"mla_proj_bwd{i}", dq, dk, dv, rC, rS1, rS2, st['cq_raw'], st['ckv_raw'], st['x0'], dx, w_uq_p[j], w_ukv_k[j], w_ukv_v[j],
                w_dq[j], w_dkv_p[j], row(Ws['mla_q_norm_g'][j]), row(Ws['mla_kv_norm_g'][j]), g[0], sc_m)
            dwo = _mm_tn(f"mla_dwo{i}", st['o'], dy)
            dwuq = _mm_tn(f"mla_dwuq{i}", st['cq'], dq_pre)
            dwk = _mm_tn(f"mla_dwukvk{i}", st['ckv'], dk)
            dwv = _mm_tn(f"mla_dwukvv{i}", st['ckv'], dv)
            dwdq = _mm_tn(f"mla_dwdq{i}", st['h'], dcq_raw)
            dwdkv = _mm_tn(f"mla_dwdkv{i}", st['h'], dckv_all)
            G.setdefault('mla_w_o', [None] * n_mla)[j] = dwo.reshape(N_HEADS, HEAD_PAD, D)[:, :V_HEAD].reshape(N_HEADS * V_HEAD, D)
            G.setdefault('mla_w_uq', [None] * n_mla)[j] = dwuq.reshape(Q_LORA, N_HEADS, HEAD_PAD)[:, :, :QK_NOPE + QK_ROPE].reshape(Q_LORA, -1)
            G.setdefault('mla_w_ukv', [None] * n_mla)[j] = jnp.concatenate(
                [dwk.reshape(KV_LORA, N_HEADS, HEAD_PAD)[:, :, :QK_NOPE], dwv.reshape(KV_LORA, N_HEADS, HEAD_PAD)[:, :, :V_HEAD]], axis=2).reshape(KV_LORA, -1)
            G.setdefault('mla_w_dq', [None] * n_mla)[j] = dwdq
            G.setdefault('mla_w_dkv', [None] * n_mla)[j] = jnp.concatenate([dwdkv[:, :KV_LORA], dwdkv[:, KV_LORA + QK_NOPE:KV_LORA + QK_NOPE + QK_ROPE]], axis=1)
            G.setdefault('mla_q_norm_g', [None] * n_mla)[j] = dqg[0]
            G.setdefault('mla_kv_norm_g', [None] * n_mla)[j] = dkvg[0]
        elif kind == 1:
            duc, dy, dlng, dlnb, dbdw, dg1, dgt_m, dysum = _conv_bwd1(f"conv_bwd1_{i}", dx, st['y'], g[1], gt_m, st['uc'], w_pw2[j],
                                                                      row(W['conv_ln_g'][j]), row(W['conv_ln_b'][j]))
            da, dx, dwdw, dbpw1, dsh_m, dsc_m, dg0 = _conv_bwd2(f"conv_bwd2_{i}", duc, st['u'], st['a'], st['x0'], dx, w_dw32[j], w_pw1[j], g[0], sc_m)
            G['conv_w_pw2'] = [_mm_tn(f"conv_dwpw2_{i}", st['z'], dy)]
            G['conv_w_pw1'] = [_mm_tn(f"conv_dwpw1_{i}", st['h'], da)]
            G['conv_w_dw'] = [dwdw[:CONV_W]]
            G['conv_b_pw1'], G['conv_b_dw'], G['conv_ln_g'], G['conv_ln_b'], G['conv_b_pw2'] = [dbpw1[0]], [dbdw[0]], [dlng[0]], [dlnb[0]], [dysum[0]]
        else:
            dp, dypre, dscale, dpb, dg1, dgt_m = _pool_bwd1(f"pool_bwd1_{i}", dx, st['y'], g[1], gt_m, st['ypre'], row(Ws['pool_scale'][j]), w_pool[j])
            dx, dsh_m, dsc_m, dg0 = _pool_bwd2(f"pool_bwd2_{i}", dp, st['x0'], dx, g[0], sc_m)
            G['pool_w'] = [_mm_tn(f"pool_dw{i}", st['p'], dypre, diag=len(POOL_WINDOWS))]
            G['pool_b'] = [dpb.reshape(len(POOL_WINDOWS), -1)]
            G['pool_scale'] = [dscale[0]]
        dmod[i] = jnp.concatenate([dsh_m, dsc_m, dgt_m, dsh_f, dsc_f, dgt_f], axis=1)
        dnorm[i] = jnp.concatenate([dg0, dg1, dg2, dg3], axis=0)
    G['norm_g'] = dnorm
    grad_x = dx.reshape(x.shape)

    rs_names = [n for n, ax in WEIGHTS if ax is not None and n != 'ada_w' and not n.startswith('ffn')]
    pieces = [(n, _layer_shards(g, SHARD_AXIS[n] - 1)) for n in rs_names for g in G[n]]
    big = [(n, p) for n, p in pieces if p.shape[1] % (8 * 1024) == 0]
    small = [(n, p) for n, p in pieces if p.shape[1] % (8 * 1024) != 0]
    packed = jnp.concatenate([p.reshape(N_DEV, -1, 1024) for _, p in big] + [_pack([p for _, p in small], F32, 8)], axis=1)
    ffn_red = _rs_finish(rs_pending, dx, me, ffn_red)
    my_chip = 2 * xi + yi
    p4 = packed.reshape((4, 2) + packed.shape[1:])
    pair_recv, = _rs_pair("rs_pair", [p4])
    chip_wire, chip_own = _pair_sum("rs_pair_sum", p4, pair_recv, ci, my_chip)

    dmod_mine = jnp.concatenate(dmod, axis=1).reshape(-1)
    fin_in = _pack([dmod_mine] + [G[n][0].reshape(-1) for n in REPL] + [loss_row.reshape(-1)], F32, 8)
    fin_all = _ag_small("ag_final", fin_in)
    chip_wire, fin_all = lax.optimization_barrier((chip_wire, fin_all))
    chip_sems, chip_thru, chip_land, chip_token = _copies_start("rs_chips_start", [chip_wire], [lax.empty(chip_wire.shape, MM)],
                                                                CHIP_PEERS, 'chip')
    g_ffn1, g_ffn2, fin_all, _ = lax.optimization_barrier((ffn_red[0], ffn_red[1], fin_all, chip_token))
    grads = {'ffn_w1': g_ffn1, 'ffn_w2': g_ffn2}
    fin_sum = _sum_devices("final_sum", fin_all).reshape(-1)
    nm = L * 6 * D
    grads['ada_b'] = fin_sum[:nm].reshape(L, 6 * D)
    off = nm
    for n in REPL:
        grads[n] = fin_sum[off:off + W[n].size].reshape(W[n].shape)
        off += W[n].size
    loss = fin_sum[off]
    dmod_all = fin_all.reshape(N_DEV, -1)[:, :nm].reshape(N_DEV, L, 6 * D)
    dmod_cols = lax.dynamic_slice_in_dim(dmod_all, me * n_ada, n_ada, axis=2)
    dmod16 = jnp.pad(jnp.transpose(dmod_cols, (1, 0, 2)), ((0, 0), (0, 16 - N_DEV), (0, 0)))
    grads['ada_w'] = _ada_w_grad(c16, dmod16)
    deltas, new_m, new_v = {}, {}, {}
    for n in ['ada_w', 'ada_b'] + REPL + ['ffn_w1', 'ffn_w2']:
        deltas[n], new_m[n], new_v[n] = _adamw("adamw_" + n, W[n], grads[n], M1[n], V2[n])

    done = lax.optimization_barrier(tuple(deltas[n] for n in ['ffn_w2', 'ffn_w1', 'ada_w', 'ada_b'] + REPL))[0]
    chip_recv, = _copies_wait("rs_chips_wait", chip_sems, chip_thru, chip_land, done, CHIP_PEERS, 'chip')
    red = _chip_sum("rs_chip_sum", chip_own, chip_recv, my_chip)
    got = {}
    row0 = 0
    for n, p in big:
        rows = p.shape[1] // 1024
        got.setdefault(n, []).append(red[row0:row0 + rows])
        row0 += rows
    tail = red[row0:].reshape(-1)
    off = 0
    for n, p in small:
        got.setdefault(n, []).append(tail[off:off + p.shape[1]])
        off += p.shape[1]
    for n in rs_names:
        grads[n] = jnp.stack([g_.reshape(W[n].shape[1:]) for g_ in got[n]], axis=0)

    for n in rs_names:
        deltas[n], new_m[n], new_v[n] = _adamw("adamw_" + n, W[n], grads[n], M1[n], V2[n])
    names = [n for n, _ in WEIGHTS]
    return (loss, grad_x, *[grads[n] for n in names], *[deltas[n] for n in names], *[new_m[n] for n in names],
            *[new_v[n] for n in names])
```

```python
import functools
import math

import jax
import jax.numpy as jnp
from jax import lax
from jax.experimental import pallas as pl
from jax.experimental.pallas import tpu as pltpu

F32 = jnp.float32
MM = jnp.bfloat16
EPS = 1e-6
NEG = -1e30
N_DEV = 8
VMEM_LIMIT = 48 * 1024 * 1024
MESH = pl.DeviceIdType.MESH

D_MODEL = 1024
N_HEADS = 16
HEAD_PAD = 128
QK_NOPE, QK_ROPE, V_HEAD = 64, 32, 64
Q_LORA, KV_LORA = 384, 256
CHUNK = 64
CONV_W = 31
POOL_WINDOWS = (2, 4, 8, 16)
ROPE_THETA = 10000.0
ATT_SCALE = 1.0 / math.sqrt(QK_NOPE + QK_ROPE)

ADAM_LR, ADAM_B1, ADAM_B2, ADAM_EPS, ADAM_WD, ADAM_STEP = 0.001, 0.9, 0.999, 1e-08, 0.01, 10

WEIGHTS = [('ada_w', 2), ('ada_b', None), ('norm_g', 2), ('mla_w_dq', 1), ('mla_q_norm_g', 1), ('mla_w_uq', 2),
           ('mla_w_dkv', 1), ('mla_kv_norm_g', 1), ('mla_w_ukv', 2), ('mla_w_o', 1), ('conv_w_pw1', 2),
           ('conv_b_pw1', None), ('conv_w_dw', 2), ('conv_b_dw', None), ('conv_ln_g', None), ('conv_ln_b', None),
           ('conv_w_pw2', 1), ('conv_b_pw2', None), ('pool_w', 2), ('pool_b', 2), ('pool_scale', 1),
           ('ffn_w1', 2), ('ffn_w2', 1)]
SHARD_AXIS = dict(WEIGHTS)
BIG = ['mla_w_dq', 'mla_w_uq', 'mla_w_dkv', 'mla_w_ukv', 'mla_w_o', 'conv_w_pw1', 'conv_w_pw2', 'pool_w', 'ffn_w1', 'ffn_w2']
SMALL = ['norm_g', 'mla_q_norm_g', 'mla_kv_norm_g', 'conv_w_dw', 'pool_b', 'pool_scale']
REPL = ['conv_b_pw1', 'conv_b_dw', 'conv_ln_g', 'conv_ln_b', 'conv_b_pw2']


def _dot(a, b):
    return jnp.dot(a.astype(MM), b.astype(MM), preferred_element_type=F32)


def _dot_nt(a, b):
    return lax.dot_general(a.astype(MM), b.astype(MM), (((1,), (1,)), ((), ())), preferred_element_type=F32)


def _dot_tn(a, b):
    return lax.dot_general(a.astype(MM), b.astype(MM), (((0,), (0,)), ((), ())), preferred_element_type=F32)


def _sigmoid(x):
    return 1.0 / (1.0 + jnp.exp(-x))


def _rstd(x):
    return lax.rsqrt(jnp.mean(x * x, axis=-1, keepdims=True) + EPS)


def _rms(x, g):
    return x * _rstd(x) * g


def _rms_bwd(x, g, dout):
    r = _rstd(x)
    xn = x * r
    dg = jnp.sum(dout * xn, axis=0, keepdims=True)
    dxn = dout * g
    dx = r * (dxn - xn * jnp.mean(dxn * xn, axis=-1, keepdims=True))
    return dx, dg


def _prenorm_bwd(x, g0, sc, dh):
    r = _rstd(x)
    xn = x * r
    dsh = jnp.sum(dh, axis=0, keepdims=True)
    dsc = jnp.sum(dh * (xn * g0), axis=0, keepdims=True)
    dn = dh * (1.0 + sc)
    dg0 = jnp.sum(dn * xn, axis=0, keepdims=True)
    dxn = dn * g0
    dx = r * (dxn - xn * jnp.mean(dxn * xn, axis=-1, keepdims=True))
    return dx, dsh, dsc, dg0


def _cparams(sem):
    return pltpu.CompilerParams(dimension_semantics=sem, vmem_limit_bytes=VMEM_LIMIT)


def _rows(name, body, n_rows, tm, rows, consts, outs, accs=(), scratch=()):
    tm = min(tm, n_rows)
    nblk = n_rows // tm
    nr, nc, no, na = len(rows), len(consts), len(outs), len(accs)
    in_specs, args = [], []
    for a, kind in rows:
        if kind == 'cur':
            im = lambda i: (i, 0)
        elif kind == 'prev':
            im = lambda i: (jnp.maximum(i - 1, 0), 0)
        else:
            im = lambda i: (jnp.minimum(i + 1, nblk - 1), 0)
        in_specs.append(pl.BlockSpec((tm, a.shape[1]), im))
        args.append(a)
    for a in consts:
        in_specs.append(pl.BlockSpec(a.shape, lambda i, nd=a.ndim: (0,) * nd))
        args.append(a)
    out_specs = [pl.BlockSpec((tm, c), lambda i: (i, 0)) for c, _ in outs]
    out_specs += [pl.BlockSpec(s, lambda i, nd=len(s): (0,) * nd) for s in accs]
    out_shape = [jax.ShapeDtypeStruct((n_rows, c), dt) for c, dt in outs]
    out_shape += [jax.ShapeDtypeStruct(s, F32) for s in accs]

    def kern(*refs):
        i = pl.program_id(0)
        rr = refs[:nr]
        cc = refs[nr:nr + nc]
        oo = refs[nr + nc:nr + nc + no]
        aa = refs[nr + nc + no:nr + nc + no + na]
        ss = refs[nr + nc + no + na:]

        @pl.when(i == 0)
        def _():
            for a in aa:
                a[...] = jnp.zeros(a.shape, F32)

        body(i, nblk, rr, cc, oo, aa, ss)

    return pl.pallas_call(kern, grid=(nblk,), in_specs=in_specs, out_specs=out_specs, out_shape=out_shape,
                          scratch_shapes=list(scratch), name=name, compiler_params=_cparams(("arbitrary",)))(*args)


def _place():
    return lax.axis_index("x"), lax.axis_index("y"), lax.axis_index("c")


def _ag_small(name, xs):
    R, C = xs.shape

    def body(x_ref, out_ref, send_sems, recv_sems):
        x, y, c = _place()
        me = 4 * x + 2 * y + c
        out_ref[me] = x_ref[...]
        copies = []
        for k in range(1, N_DEV):
            peer = ((1 - x) if k & 4 else x, (1 - y) if k & 2 else y, (1 - c) if k & 1 else c)
            cp = pltpu.make_async_remote_copy(src_ref=x_ref, dst_ref=out_ref.at[me], send_sem=send_sems.at[k - 1],
                                              recv_sem=recv_sems.at[k - 1], device_id=peer, device_id_type=MESH)
            cp.start()
            copies.append(cp)
        for cp in copies:
            cp.wait()

    return pl.pallas_call(
        body, out_shape=jax.ShapeDtypeStruct((N_DEV, R, C), xs.dtype),
        in_specs=[pl.BlockSpec(memory_space=pltpu.VMEM)], out_specs=pl.BlockSpec(memory_space=pltpu.VMEM),
        scratch_shapes=[pltpu.SemaphoreType.DMA((N_DEV - 1,)), pltpu.SemaphoreType.DMA((N_DEV - 1,))], name=name)(xs)


def _ag_big(name, xs):
    nt = len(xs)

    def body(*refs):
        x_refs, out_refs = refs[:nt], refs[nt:2 * nt]
        send_sems, recv_sems, local_sems = refs[2 * nt:]
        x, y, c = _place()
        me, sibling = (x, y, c), (x, y, 1 - c)
        chips = [(1 - x, y), (x, 1 - y), (1 - x, 1 - y)]

        def copy(t, k, block, to, own=False):
            px, py, pc = block
            rows = out_refs[t].at[4 * px + 2 * py + pc]
            return pltpu.make_async_remote_copy(src_ref=x_refs[t] if own else rows, dst_ref=rows, send_sem=send_sems.at[7 * t + k],
                                                recv_sem=recv_sems.at[7 * t + k], device_id=to, device_id_type=MESH)

        mine = [pltpu.make_async_copy(x_refs[t], out_refs[t].at[4 * x + 2 * y + c], local_sems.at[t]) for t in range(nt)]
        for cp in mine:
            cp.start()
        first = []
        for t in range(nt):
            first.append(copy(t, 0, me, sibling, own=True))
            first += [copy(t, 1 + j, me, (*chip, c), own=True) for j, chip in enumerate(chips)]
        for cp in first:
            cp.start()
        passed = []
        for t in range(nt):
            for j, chip in enumerate(chips):
                copy(t, 1 + j, (*chip, c), me).wait_recv()
                cp = copy(t, 4 + j, (*chip, c), sibling)
                cp.start()
                passed.append(cp)
        for t in range(nt):
            copy(t, 0, sibling, me).wait_recv()
            for j, chip in enumerate(chips):
                copy(t, 4 + j, (*chip, 1 - c), me).wait_recv()
        for cp in first + passed:
            cp.wait_send()
        for cp in mine:
            cp.wait()

    hbm = pl.BlockSpec(memory_space=pl.ANY)
    return pl.pallas_call(
        body, out_shape=[jax.ShapeDtypeStruct((N_DEV,) + t.shape, t.dtype) for t in xs],
        in_specs=[hbm] * nt, out_specs=[hbm] * nt,
        scratch_shapes=[pltpu.SemaphoreType.DMA((7 * nt,)), pltpu.SemaphoreType.DMA((7 * nt,)), pltpu.SemaphoreType.DMA((nt,))],
        name=name)(*xs)


def _rs_pair(name, ps):
    nt = len(ps)

    def body(*refs):
        p_refs, recv_refs = refs[:nt], refs[nt:2 * nt]
        send_sems, recv_sems = refs[2 * nt:]
        x, y, c = _place()
        copies = []
        for t in range(nt):
            for j in range(4):
                cp = pltpu.make_async_remote_copy(src_ref=p_refs[t].at[j, 1 - c], dst_ref=recv_refs[t].at[j], send_sem=send_sems.at[4 * t + j],
                                                  recv_sem=recv_sems.at[4 * t + j], device_id=(x, y, 1 - c), device_id_type=MESH)
                cp.start()
                copies.append(cp)
        for cp in copies:
            cp.wait()

    hbm = pl.BlockSpec(memory_space=pl.ANY)
    return pl.pallas_call(
        body, out_shape=[jax.ShapeDtypeStruct((4,) + p.shape[2:], p.dtype) for p in ps], in_specs=[hbm] * nt, out_specs=[hbm] * nt,
        scratch_shapes=[pltpu.SemaphoreType.DMA((4 * nt,)), pltpu.SemaphoreType.DMA((4 * nt,))], name=name)(*ps)


def _rs_chips(name, ss):
    nt = len(ss)

    def body(*refs):
        s_refs, recv_refs = refs[:nt], refs[nt:2 * nt]
        send_sems, recv_sems, local_sems = refs[2 * nt:]
        x, y, c = _place()
        mine = 2 * x + y
        owns = [pltpu.make_async_copy(s_refs[t].at[mine], recv_refs[t].at[mine], local_sems.at[t]) for t in range(nt)]
        for cp in owns:
            cp.start()
        copies = []
        for t in range(nt):
            for k in range(1, 4):
                px = (1 - x) if k & 2 else x
                py = (1 - y) if k & 1 else y
                cp = pltpu.make_async_remote_copy(src_ref=s_refs[t].at[2 * px + py], dst_ref=recv_refs[t].at[mine],
                                                  send_sem=send_sems.at[3 * t + k - 1], recv_sem=recv_sems.at[3 * t + k - 1],
                                                  device_id=(px, py, c), device_id_type=MESH)
                cp.start()
                copies.append(cp)
        for cp in copies:
            cp.wait()
        for cp in owns:
            cp.wait()

    hbm = pl.BlockSpec(memory_space=pl.ANY)
    return pl.pallas_call(
        body, out_shape=[jax.ShapeDtypeStruct(s_.shape, s_.dtype) for s_ in ss], in_specs=[hbm] * nt, out_specs=[hbm] * nt,
        scratch_shapes=[pltpu.SemaphoreType.DMA((3 * nt,)), pltpu.SemaphoreType.DMA((3 * nt,)), pltpu.SemaphoreType.DMA((nt,))],
        name=name)(*ss)


RS_ROWS = 256


def _row_block(r):
    return next(t for t in range(RS_ROWS, 0, -16) if r % t == 0)


def _pair_sum(name, p, recv, my_c, my_chip):
    _, _, r, c = p.shape
    tr = _row_block(r)

    def body(sc_ref, p_ref, r_ref, o_ref, own_ref):
        s = p_ref[...] + r_ref[...]
        o_ref[...] = s.astype(MM)

        @pl.when(pl.program_id(1) == sc_ref[1])
        def _():
            own_ref[...] = s

    return pl.pallas_call(
        body, grid_spec=pltpu.PrefetchScalarGridSpec(
            num_scalar_prefetch=1, grid=(r // tr, 4),
            in_specs=[pl.BlockSpec((None, None, tr, c), lambda i, j, sc: (j, sc[0], i, 0)),
                      pl.BlockSpec((None, tr, c), lambda i, j, sc: (j, i, 0))],
            out_specs=[pl.BlockSpec((None, tr, c), lambda i, j, sc: (j, i, 0)), pl.BlockSpec((tr, c), lambda i, j, sc: (i, 0))]),
        out_shape=[jax.ShapeDtypeStruct((4, r, c), MM), jax.ShapeDtypeStruct((r, c), F32)], name=name,
        compiler_params=_cparams(("arbitrary", "arbitrary")))(jnp.stack([my_c, my_chip]), p, recv)


def _chip_sum(name, own, recv, my_chip):
    _, r, c = recv.shape
    tr = _row_block(r)

    def body(sc_ref, own_ref, r_ref, o_ref):
        acc = jnp.zeros((tr, c), F32)
        for j in range(4):
            acc = acc + jnp.where(sc_ref[0] == j, own_ref[...], r_ref[j].astype(F32))
        o_ref[...] = acc

    return pl.pallas_call(
        body, grid_spec=pltpu.PrefetchScalarGridSpec(
            num_scalar_prefetch=1, grid=(r // tr,),
            in_specs=[pl.BlockSpec((tr, c), lambda i, sc: (i, 0)), pl.BlockSpec((4, tr, c), lambda i, sc: (0, i, 0))],
            out_specs=pl.BlockSpec((tr, c), lambda i, sc: (i, 0))),
        out_shape=jax.ShapeDtypeStruct((r, c), F32), name=name,
        compiler_params=_cparams(("arbitrary",)))(my_chip.reshape(1), own, recv)


def _reduce_scatter(tag, tensors, my_c, my_chip):
    ps = [t.reshape((4, 2) + t.shape[1:]) for t in tensors]
    recv = _rs_pair(tag + "_pair", ps)
    sums = [_pair_sum(f"{tag}_pair_sum{t}", ps[t], recv[t], my_c, my_chip) for t in range(len(ps))]
    recv2 = _rs_chips(tag + "_chips", [s_[0] for s_ in sums])
    return [_chip_sum(f"{tag}_chip_sum{t}", sums[t][1], recv2[t], my_chip) for t in range(len(ps))]


HBM_SPEC = pl.BlockSpec(memory_space=pltpu.HBM)
SEM_SPEC = pl.BlockSpec(memory_space=pltpu.SEMAPHORE)
SPLIT_EFFECT = pltpu.SideEffectType.DATAFLOW_SIDE_EFFECTING
ALL_PEERS = (1, 2, 3, 4, 5, 6, 7)
FIRST_LEVEL_PEERS = (1, 4, 2, 6)
CHIP_PEERS = (4, 2, 6)


def _split_copies(src_refs, land_refs, sems, masks, src_per_peer):
    n, nt = len(masks), len(src_refs)
    x, y, c = _place()
    by_chip = src_per_peer == 'chip'
    slot = 2 * x + y if by_chip else 4 * x + 2 * y + c
    copies = []
    for t in range(nt):
        for k, mask in enumerate(masks):
            px, py, pc = (1 - x) if mask & 4 else x, (1 - y) if mask & 2 else y, (1 - c) if mask & 1 else c
            src = src_refs[t].at[2 * px + py if by_chip else 4 * px + 2 * py + pc] if src_per_peer else src_refs[t]
            copies.append(pltpu.make_async_remote_copy(src_ref=src, dst_ref=land_refs[t].at[slot], send_sem=sems[t * n + k],
                                                       recv_sem=sems[nt * n + t * n + k], device_id=(px, py, pc), device_id_type=MESH))
    return copies


def _copies_start(name, srcs, lands, masks, src_per_peer):
    nt, ns = len(srcs), 2 * len(masks) * len(srcs)

    def body(*refs):
        for cp in _split_copies(refs[:nt], refs[nt:2 * nt], refs[2 * nt:2 * nt + ns], masks, src_per_peer):
            cp.start()
        token = refs[-1]
        token[...] = jnp.zeros(token.shape, F32)

    outs = pl.pallas_call(
        body, name=name,
        out_shape=(pltpu.SemaphoreType.DMA(()),) * ns + tuple(pltpu.HBM(a.shape, a.dtype) for a in list(srcs) + list(lands))
        + (jax.ShapeDtypeStruct((8, 128), F32),),
        in_specs=(HBM_SPEC,) * (2 * nt), out_specs=(SEM_SPEC,) * ns + (HBM_SPEC,) * (2 * nt) + (pl.BlockSpec(memory_space=pltpu.VMEM),),
        input_output_aliases={t: ns + t for t in range(2 * nt)}, compiler_params=pltpu.CompilerParams(has_side_effects=SPLIT_EFFECT))(
            *[pltpu.with_memory_space_constraint(a, pltpu.HBM) for a in list(srcs) + list(lands)])
    return outs[:ns], outs[ns:ns + nt], outs[ns + nt:ns + 2 * nt], outs[-1]


def _copies_wait(name, sems, srcs_thru, lands_thru, after, masks, src_per_peer):
    nt, ns = len(srcs_thru), len(sems)

    def body(*refs):
        for cp in _split_copies(refs[:nt], refs[nt:2 * nt], refs[2 * nt:2 * nt + ns], masks, src_per_peer):
            cp.wait_send()
            cp.wait_recv()

    thru = list(srcs_thru) + list(lands_thru)
    return pl.pallas_call(
        body, name=name, out_shape=tuple(pltpu.HBM(a.shape, a.dtype) for a in thru),
        in_specs=(HBM_SPEC,) * (2 * nt) + (SEM_SPEC,) * ns + (pl.BlockSpec(memory_space=pl.ANY),), out_specs=(HBM_SPEC,) * (2 * nt),
        input_output_aliases={t: t for t in range(2 * nt)}, compiler_params=pltpu.CompilerParams(has_side_effects=SPLIT_EFFECT))(
            *thru, *sems, after)[nt:]


def _ag_forward(name, gs):
    nt = len(gs)

    def body(*refs):
        o_refs, send_sems, recv_sems = refs[nt:2 * nt], refs[2 * nt], refs[2 * nt + 1]
        x, y, c = _place()
        chips = [(1 - x, y), (x, 1 - y), (1 - x, 1 - y)]

        def copy(t, j, pc):
            rows = o_refs[t].at[4 * chips[j][0] + 2 * chips[j][1] + pc]
            return pltpu.make_async_remote_copy(src_ref=rows, dst_ref=rows, send_sem=send_sems.at[3 * t + j], recv_sem=recv_sems.at[3 * t + j],
                                                device_id=(x, y, 1 - c), device_id_type=MESH)

        for t in range(nt):
            for j in range(3):
                copy(t, j, c).start()
        for t in range(nt):
            for j in range(3):
                copy(t, j, c).wait_send()
                copy(t, j, 1 - c).wait_recv()

    hbm = pl.BlockSpec(memory_space=pl.ANY)
    return pl.pallas_call(body, out_shape=[jax.ShapeDtypeStruct(g.shape, g.dtype) for g in gs], in_specs=[hbm] * nt, out_specs=[hbm] * nt,
                          scratch_shapes=[pltpu.SemaphoreType.DMA((3 * nt,)), pltpu.SemaphoreType.DMA((3 * nt,))],
                          input_output_aliases={t: t for t in range(nt)}, name=name)(*gs)


def _mm_tn_wire(name, a, b, me, sqrelu, shard_rows):
    T, M = a.shape
    N = b.shape[1]
    tk = min(2048, T)
    nk = T // tk
    if shard_rows:
        bm, bn = M // N_DEV, N
        a_spec = pl.BlockSpec((tk, 2 * bm), lambda j, k, m: (k, j))
        b_spec = pl.BlockSpec((tk, bn), lambda j, k, m: (k, 0))
        halves = (slice(0, bm), slice(None)), (slice(bm, 2 * bm), slice(None))
        acc_shape = (2 * bm, bn)
    else:
        bm, bn = M, N // N_DEV
        a_spec = pl.BlockSpec((tk, bm), lambda j, k, m: (k, 0))
        b_spec = pl.BlockSpec((tk, 2 * bn), lambda j, k, m: (k, j))
        halves = (slice(None), slice(0, bn)), (slice(None), slice(bn, 2 * bn))
        acc_shape = (bm, 2 * bn)

    def body(me_ref, a_ref, b_ref, wire_ref, own_ref, acc):
        j, k = pl.program_id(0), pl.program_id(1)

        @pl.when(k == 0)
        def _():
            acc[...] = jnp.zeros(acc.shape, F32)

        av = a_ref[...]
        if sqrelu:
            r = jnp.maximum(av, 0.0)
            av = r * r
        acc[...] += _dot_tn(av, b_ref[...])

        for hh in range(2):
            @pl.when(k == nk - 1)
            def _():
                wire_ref[hh] = acc[halves[hh]].astype(MM)

            @pl.when((k == nk - 1) & (2 * j + hh == me_ref[0]))
            def _():
                own_ref[...] = acc[halves[hh]]

    return pl.pallas_call(
        body, grid_spec=pltpu.PrefetchScalarGridSpec(
            num_scalar_prefetch=1, grid=(N_DEV // 2, nk), in_specs=[a_spec, b_spec],
            out_specs=[pl.BlockSpec((2, bm, bn), lambda j, k, m: (j, 0, 0)), pl.BlockSpec((bm, bn), lambda j, k, m: (0, 0))],
            scratch_shapes=[pltpu.VMEM(acc_shape, F32)]),
        out_shape=[jax.ShapeDtypeStruct((N_DEV, bm, bn), MM), jax.ShapeDtypeStruct((bm, bn), F32)], name=name,
        compiler_params=_cparams(("arbitrary", "arbitrary")))(me.reshape(1), a, b)


def _rs_final(name, own, recv, me, stack, li):
    _, r, c = recv.shape
    tr = RS_ROWS

    def body(me_ref, own_ref, r_ref, s_ref, o_ref):
        acc = jnp.zeros((tr, c), F32)
        for j in range(N_DEV):
            acc = acc + jnp.where(me_ref[0] == j, own_ref[...], r_ref[j].astype(F32))
        o_ref[...] = acc

    return pl.pallas_call(
        body, grid_spec=pltpu.PrefetchScalarGridSpec(
            num_scalar_prefetch=1, grid=(r // tr,),
            in_specs=[pl.BlockSpec((tr, c), lambda i, m: (i, 0)), pl.BlockSpec((N_DEV, tr, c), lambda i, m: (0, i, 0)),
                      pl.BlockSpec(memory_space=pl.ANY)],
            out_specs=pl.BlockSpec((None, tr, c), lambda i, m: (li, i, 0))),
        out_shape=jax.ShapeDtypeStruct(stack.shape, F32), input_output_aliases={3: 0}, name=name,
        compiler_params=_cparams(("arbitrary",)))(me.reshape(1), own, recv, stack)


def _rs_finish(pending, after, me, stacks):
    i, sems, wires_thru, lands, owns = pending
    recvs = _copies_wait(f"rs_wait{i}", sems, wires_thru, lands, after, ALL_PEERS, True)
    return [_rs_final(f"rs_final{i}_{t}", owns[t], recvs[t], me, stacks[t], i) for t in range(len(owns))]


def _mod_part(c16, ada_w, ada_b_cols):
    L, D, n = ada_w.shape

    def body(c_ref, w_ref, b_ref, o_ref):
        cv = c_ref[...]
        o_ref[...] = _dot(cv * _sigmoid(cv), w_ref[...]) + b_ref[...]

    return pl.pallas_call(
        body, grid=(L,), in_specs=[pl.BlockSpec((16, D), lambda i: (0, 0)), pl.BlockSpec((None, D, n), lambda i: (i, 0, 0)),
                                   pl.BlockSpec((None, 1, n), lambda i: (i, 0, 0))],
        out_specs=pl.BlockSpec((None, 16, n), lambda i: (i, 0, 0)), out_shape=jax.ShapeDtypeStruct((L, 16, n), F32),
        name="ada_mod", compiler_params=_cparams(("arbitrary",)))(c16, ada_w, ada_b_cols)


def _ada_w_grad(c16, dmod16):
    L, _, n = dmod16.shape
    D = c16.shape[1]

    def body(c_ref, d_ref, o_ref):
        cv = c_ref[...]
        o_ref[...] = _dot_tn(cv * _sigmoid(cv), d_ref[...])

    return pl.pallas_call(
        body, grid=(L,), in_specs=[pl.BlockSpec((16, D), lambda i: (0, 0)), pl.BlockSpec((None, 16, n), lambda i: (i, 0, 0))],
        out_specs=pl.BlockSpec((None, D, n), lambda i: (i, 0, 0)), out_shape=jax.ShapeDtypeStruct((L, D, n), F32),
        name="ada_w_grad", compiler_params=_cparams(("arbitrary",)))(c16, dmod16)


def _sum_devices(name, g):
    _, R, C = g.shape

    def body(g_ref, o_ref):
        acc = g_ref[0]
        for d in range(1, N_DEV):
            acc = acc + g_ref[d]
        o_ref[...] = acc

    return pl.pallas_call(body, out_shape=jax.ShapeDtypeStruct((R, C), F32), name=name)(g)


def _prenorm(name, x, g0, sc, sh, dtype):
    T, D = x.shape

    def body(i, n, rr, cc, oo, aa, ss):
        oo[0][...] = (_rms(rr[0][...], cc[0][...]) * (1.0 + cc[1][...]) + cc[2][...]).astype(dtype)

    return _rows(name, body, T, 512, [(x, 'cur')], [g0, sc, sh], [(D, dtype)])[0]


def _post_bwd_math(d, yv, g1v, gtv):
    dgt = jnp.sum(d * _rms(yv, g1v), axis=0, keepdims=True)
    dy, dg1 = _rms_bwd(yv, g1v, d * gtv)
    return dy, dg1, dgt


def _post_bwd_nt(name, dxo, y, g1, gt, w):
    T, D = y.shape
    K = w.shape[0]

    def body(i, n, rr, cc, oo, aa, ss):
        dy, dg1, dgt = _post_bwd_math(rr[0][...], rr[1][...], cc[0][...], cc[1][...])
        aa[0][...] += dg1
        aa[1][...] += dgt
        dy = dy.astype(MM)
        oo[0][...] = dy
        oo[1][...] = _dot_nt(dy, cc[2][...]).astype(MM)

    return _rows(name, body, T, 512, [(dxo, 'cur'), (y, 'cur')], [g1, gt, w], [(D, MM), (K, MM)], accs=[(1, D)] * 2)


def _mm_post(name, a, w, bias, x, g1, gt):
    T, D = x.shape
    consts = [w, g1, gt] + ([bias] if bias is not None else [])

    def body(i, n, rr, cc, oo, aa, ss):
        y = _dot(rr[0][...], cc[0][...])
        if bias is not None:
            y = y + cc[3][...]
        oo[0][...] = y
        oo[1][...] = rr[1][...] + cc[2][...] * _rms(y, cc[1][...])

    return _rows(name, body, T, 512, [(a, 'cur'), (x, 'cur')], consts, [(D, F32), (D, F32)])


def _mm_tn(name, a, b, sqrelu=False, col_shards=0, diag=0):
    T, M = a.shape
    N = b.shape[1]
    tk = min(512, T)
    nk = T // tk
    if diag:
        bm, bn = M // diag, N // diag
        grid = (diag, 1, nk)
        a_spec = pl.BlockSpec((tk, bm), lambda g, n, k: (k, g))
        b_spec = pl.BlockSpec((tk, bn), lambda g, n, k: (k, g))
        o_spec = pl.BlockSpec((None, bm, bn), lambda g, n, k: (g, 0, 0))
        o_shape = (diag, bm, bn)
    else:
        bm = min(M, 1024)
        bn = N // col_shards if col_shards else min(N, 1024)
        grid = (M // bm, N // bn, nk)
        a_spec = pl.BlockSpec((tk, bm), lambda m, n, k: (k, m))
        b_spec = pl.BlockSpec((tk, bn), lambda m, n, k: (k, n))
        if col_shards:
            o_spec = pl.BlockSpec((None, bm, bn), lambda m, n, k: (n, m, 0))
            o_shape = (col_shards, M, bn)
        else:
            o_spec = pl.BlockSpec((bm, bn), lambda m, n, k: (m, n))
            o_shape = (M, N)

    def body(a_ref, b_ref, o_ref):
        @pl.when(pl.program_id(2) == 0)
        def _():
            o_ref[...] = jnp.zeros(o_ref.shape, F32)

        av = a_ref[...]
        if sqrelu:
            r = jnp.maximum(av, 0.0)
            av = r * r
        o_ref[...] += _dot_tn(av, b_ref[...])

    return pl.pallas_call(body, grid=grid, in_specs=[a_spec, b_spec], out_specs=o_spec,
                          out_shape=jax.ShapeDtypeStruct(o_shape, F32), name=name,
                          compiler_params=_cparams(("arbitrary", "arbitrary", "arbitrary")))(a, b)


FFN_SHARDS = 4
FFN_BWD_SHARDS = 2

def _ffn_fwd(name, li, x, g0, sc, sh, w1g, w2g, g1, gt):
    T, D = x.shape
    nf, tf = w1g.shape[0], w1g.shape[-1]
    F = nf * tf
    tm = min(512, T)

    def body(x_ref, g0_ref, sc_ref, sh_ref, w1_ref, w2_ref, g1_ref, gt_ref, h_ref, a_ref, y_ref, xo_ref, acc):
        f = pl.program_id(1)

        @pl.when(f == 0)
        def _():
            acc[...] = jnp.zeros(acc.shape, F32)
            h_ref[...] = (_rms(x_ref[...], g0_ref[...]) * (1.0 + sc_ref[...]) + sh_ref[...]).astype(MM)

        hv = h_ref[...]
        part = None
        for hh in range(FFN_SHARDS):
            a = _dot(hv, w1_ref[hh])
            a_ref[:, hh * tf:(hh + 1) * tf] = a.astype(MM)
            r = jnp.maximum(a, 0.0)
            p = _dot(r * r, w2_ref[hh])
            part = p if part is None else part + p
        acc[...] += part

        @pl.when(f == nf // FFN_SHARDS - 1)
        def _():
            y = acc[...]
            y_ref[...] = y
            xo_ref[...] = x_ref[...] + gt_ref[...] * _rms(y, g1_ref[...])

    row = lambda t, f: (t, 0)
    one = lambda t, f: (0, 0)
    return pl.pallas_call(
        body, grid=(T // tm, nf // FFN_SHARDS),
        in_specs=[pl.BlockSpec((tm, D), row)] + [pl.BlockSpec((1, D), one)] * 3
        + [pl.BlockSpec((FFN_SHARDS, None, D, tf), lambda t, f: (f, li, 0, 0)), pl.BlockSpec((FFN_SHARDS, None, tf, D), lambda t, f: (f, li, 0, 0)),
           pl.BlockSpec((1, D), one), pl.BlockSpec((1, D), one)],
        out_specs=[pl.BlockSpec((tm, D), row), pl.BlockSpec((tm, FFN_SHARDS * tf), lambda t, f: (t, f)), pl.BlockSpec((tm, D), row),
                   pl.BlockSpec((tm, D), row)],
        out_shape=[jax.ShapeDtypeStruct((T, D), MM), jax.ShapeDtypeStruct((T, F), MM), jax.ShapeDtypeStruct((T, D), F32),
                   jax.ShapeDtypeStruct((T, D), F32)],
        scratch_shapes=[pltpu.VMEM((tm, D), F32)], name=name,
        compiler_params=_cparams(("arbitrary", "arbitrary")))(x, g0, sc, sh, w1g, w2g, g1, gt)


def _ffn_bwd(name, li, y, g1, gt, a, w1g, w2g, x, dxo, g0, sc):
    T, D = x.shape
    nf, tf = w1g.shape[0], w1g.shape[-1]
    F = nf * tf
    tm = min(512, T)
    ns = FFN_BWD_SHARDS

    def body(y_ref, g1_ref, gt_ref, a_ref, w1_ref, w2_ref, x_ref, dxo_ref, g0_ref, sc_ref,
             dy_ref, da_ref, dx_ref, dg1_ref, dgt_ref, dsh_ref, dsc_ref, dg0_ref, acc):
        t, f = pl.program_id(0), pl.program_id(1)

        @pl.when((t == 0) & (f == 0))
        def _():
            for r in (dg1_ref, dgt_ref, dsh_ref, dsc_ref, dg0_ref):
                r[...] = jnp.zeros(r.shape, F32)

        @pl.when(f == 0)
        def _():
            acc[...] = jnp.zeros(acc.shape, F32)
            d, yv, g1v = dxo_ref[...], y_ref[...], g1_ref[...]
            dgt_ref[...] += jnp.sum(d * _rms(yv, g1v), axis=0, keepdims=True)
            dyf, dg1 = _rms_bwd(yv, g1v, d * gt_ref[...])
            dg1_ref[...] += dg1
            dy_ref[...] = dyf.astype(MM)

        dyv = dy_ref[...]
        dyv = dyv + dyv
        part = None
        for hh in range(ns):
            cols = slice(hh * tf, (hh + 1) * tf)
            du = _dot_nt(dyv, w2_ref[hh])
            da = (du * jnp.maximum(a_ref[:, cols], 0.0).astype(F32)).astype(MM)
            da_ref[:, cols] = da
            p = _dot_nt(da, w1_ref[hh])
            part = p if part is None else part + p
        acc[...] += part

        @pl.when(f == nf // ns - 1)
        def _():
            dx, dsh, dsc, dg0 = _prenorm_bwd(x_ref[...], g0_ref[...], sc_ref[...], acc[...])
            dx_ref[...] = dxo_ref[...] + dx
            dsh_ref[...] += dsh
            dsc_ref[...] += dsc
            dg0_ref[...] += dg0

    row = lambda t, f: (t, 0)
    one = lambda t, f: (0, 0)
    blk = lambda t, f: (t, f)
    return pl.pallas_call(
        body, grid=(T // tm, nf // ns),
        in_specs=[pl.BlockSpec((tm, D), row), pl.BlockSpec((1, D), one), pl.BlockSpec((1, D), one), pl.BlockSpec((tm, ns * tf), blk),
                  pl.BlockSpec((ns, None, D, tf), lambda t, f: (f, li, 0, 0)),
                  pl.BlockSpec((ns, None, tf, D), lambda t, f: (f, li, 0, 0)), pl.BlockSpec((tm, D), row), pl.BlockSpec((tm, D), row),
                  pl.BlockSpec((1, D), one), pl.BlockSpec((1, D), one)],
        out_specs=[pl.BlockSpec((tm, D), row), pl.BlockSpec((tm, ns * tf), blk), pl.BlockSpec((tm, D), row)] + [pl.BlockSpec((1, D), one)] * 5,
        out_shape=[jax.ShapeDtypeStruct((T, D), MM), jax.ShapeDtypeStruct((T, F), MM), jax.ShapeDtypeStruct((T, D), F32)]
        + [jax.ShapeDtypeStruct((1, D), F32)] * 5,
        scratch_shapes=[pltpu.VMEM((tm, D), F32)], name=name,
        compiler_params=_cparams(("arbitrary", "arbitrary")))(y, g1, gt, a, w1g, w2g, x, dxo, g0, sc)


def _rope_tables(pos, invf):
    T = pos.shape[0]

    def body(i, n, rr, cc, oo, aa, ss):
        ang = rr[0][...] * cc[0][...]
        lane = lax.broadcasted_iota(jnp.int32, ang.shape, 1)
        cs, sn = jnp.cos(ang), jnp.sin(ang)
        oo[0][...] = jnp.where((lane >= QK_NOPE) & (lane < QK_NOPE + QK_ROPE), cs, 1.0)
        oo[1][...] = jnp.where((lane >= QK_NOPE) & (lane < QK_NOPE + QK_ROPE // 2), -sn, 0.0)
        oo[2][...] = jnp.where((lane >= QK_NOPE + QK_ROPE // 2) & (lane < QK_NOPE + QK_ROPE), sn, 0.0)

    return _rows("rope_tables", body, T, 512, [(pos, 'cur')], [invf], [(HEAD_PAD, F32)] * 3)


def _rope(v, C, S1, S2):
    n = v.shape[1]
    reps = n // HEAD_PAD
    if reps > 1:
        C, S1, S2 = (jnp.tile(t, (1, reps)) for t in (C, S1, S2))
    return v * C + pltpu.roll(v, n - QK_ROPE // 2, 1) * S1 + pltpu.roll(v, QK_ROPE // 2, 1) * S2


def _unrope(d, C, S1, S2):
    n = d.shape[1]
    reps = n // HEAD_PAD
    if reps > 1:
        C, S1, S2 = (jnp.tile(t, (1, reps)) for t in (C, S1, S2))
    return d * C + pltpu.roll(d * S1, QK_ROPE // 2, 1) + pltpu.roll(d * S2, n - QK_ROPE // 2, 1)


def _mla_proj(name, x, g0, sc, sh, C, S1, S2, w_dq, qg, w_uq, w_dkv, kvg, w_ukv_k, w_ukv_v):
    T, D = x.shape
    HP = N_HEADS * HEAD_PAD

    def body(i, n, rr, cc, oo, aa, ss):
        hv = (_rms(rr[0][...], cc[7][...]) * (1.0 + cc[8][...]) + cc[9][...]).astype(MM)
        oo[7][...] = hv
        Cv, S1v, S2v = rr[1][...], rr[2][...], rr[3][...]
        cq_raw = _dot(hv, cc[0][...])
        cq = _rms(cq_raw, cc[1][...]).astype(MM)
        q = _rope(_dot(cq, cc[2][...]), Cv, S1v, S2v)
        ckv_all = _dot(hv, cc[3][...])
        ckv_raw = ckv_all[:, :KV_LORA]
        ckv = _rms(ckv_raw, cc[4][...]).astype(MM)
        kr = _rope(ckv_all[:, KV_LORA:], Cv, S1v, S2v)
        k = _dot(ckv, cc[5][...]) + jnp.tile(kr, (1, N_HEADS))
        v = _dot(ckv, cc[6][...])
        v = jnp.where(lax.broadcasted_iota(jnp.int32, v.shape, 1) % HEAD_PAD == V_HEAD, 1.0, v)
        oo[0][...] = cq_raw
        oo[1][...] = cq
        oo[2][...] = ckv_raw
        oo[3][...] = ckv
        oo[4][...] = q.astype(MM)
        oo[5][...] = k.astype(MM)
        oo[6][...] = v.astype(MM)

    return _rows(name, body, T, 256, [(x, 'cur'), (C, 'cur'), (S1, 'cur'), (S2, 'cur')],
                 [w_dq, qg, w_uq, w_dkv, kvg, w_ukv_k, w_ukv_v, g0, sc, sh],
                 [(Q_LORA, F32), (Q_LORA, MM), (KV_LORA, F32), (KV_LORA, MM), (HP, MM), (HP, MM), (HP, MM), (D, MM)])


ATT_HEADS = 4
ATT_BLOCK = 512
ATT_FWD_BLOCK = 1024


def _chunk_mask_t(tk, tq):
    ki = lax.broadcasted_iota(jnp.int32, (tk, tq), 0) // CHUNK
    qi = lax.broadcasted_iota(jnp.int32, (tk, tq), 1) // CHUNK
    return ki <= qi


def _attn_fwd(name, q, k, v):
    T = q.shape[0]
    tb = min(ATT_FWD_BLOCK, T)
    nb = T // tb
    nh = ATT_HEADS
    hs = [slice(h * HEAD_PAD, (h + 1) * HEAD_PAD) for h in range(nh)]

    def body(q_ref, k_ref, v_ref, o_ref, lse_ref):
        qb = pl.program_id(1)

        def k_block(k0, masked, st):
            new = []
            for h in range(nh):
                m, acc = st[h]
                s = _dot_nt(k_ref[pl.ds(k0, tb), hs[h]], q_ref[:, hs[h]])
                if masked:
                    s = jnp.where(_chunk_mask_t(tb, tb), s, NEG)
                m_new = jnp.maximum(m, jnp.max(s, axis=0, keepdims=True))
                alpha = jnp.exp((m - m_new) * ATT_SCALE)
                p = jnp.exp((s - m_new) * ATT_SCALE)
                acc = alpha * acc + _dot_tn(v_ref[pl.ds(k0, tb), hs[h]], p)
                new.append((m_new, acc))
            return tuple(new)

        st = tuple((jnp.full((1, tb), NEG, F32), jnp.zeros((HEAD_PAD, tb), F32)) for _ in range(nh))
        st = k_block(pl.multiple_of(qb * tb, tb), True, st)
        st = lax.fori_loop(0, qb, lambda kb, s_: k_block(pl.multiple_of(kb * tb, tb), False, s_), st)
        for h in range(nh):
            m, acc = st[h]
            l = acc[V_HEAD:V_HEAD + 1, :]
            o_ref[:, hs[h]] = (acc / l).T.astype(MM)
            lse_ref[h] = jnp.broadcast_to(m * ATT_SCALE + jnp.log(l), (8, tb))

    blk = pl.BlockSpec((tb, nh * HEAD_PAD), lambda g, i: (i, g))
    res = pl.BlockSpec((T, nh * HEAD_PAD), lambda g, i: (0, g))
    return pl.pallas_call(
        body, grid=(N_HEADS // nh, nb), in_specs=[blk, res, res],
        out_specs=[blk, pl.BlockSpec((nh, 8, tb), lambda g, i: (g, 0, i))],
        out_shape=[jax.ShapeDtypeStruct(q.shape, MM), jax.ShapeDtypeStruct((N_HEADS, 8, T), F32)], name=name,
        compiler_params=_cparams(("arbitrary", "arbitrary")))(q, k, v)


def _attn_delta(name, do, o):
    T = do.shape[0]
    tb = min(256, T)

    def body(do_ref, o_ref, d_ref):
        lane = lax.broadcasted_iota(jnp.int32, (tb, HEAD_PAD), 1) // 8
        cols = jnp.zeros((tb, HEAD_PAD), F32)
        for h in range(N_HEADS):
            hsl = slice(h * HEAD_PAD, (h + 1) * HEAD_PAD)
            r = jnp.sum(do_ref[:, hsl].astype(F32) * o_ref[:, hsl].astype(F32), axis=1, keepdims=True)
            cols = jnp.where(lane == h, r, cols)
        d_ref[...] = cols.T

    spec = pl.BlockSpec((tb, N_HEADS * HEAD_PAD), lambda i: (i, 0))
    out = pl.pallas_call(body, grid=(T // tb,), in_specs=[spec, spec], out_specs=pl.BlockSpec((HEAD_PAD, tb), lambda i: (0, i)),
                         out_shape=jax.ShapeDtypeStruct((HEAD_PAD, T), F32), name=name, compiler_params=_cparams(("arbitrary",)))(do, o)
    return out.reshape(N_HEADS, 8, T)


def _attn_bwd(name, q, k, v, do, lse, delta):
    T = q.shape[0]
    tb = min(ATT_BLOCK, T)
    nb = T // tb
    nh = ATT_HEADS
    hs = [slice(h * HEAD_PAD, (h + 1) * HEAD_PAD) for h in range(nh)]

    def body(q_ref, k_ref, v_ref, do_ref, lse_ref, dl_ref, dq_ref, dk_ref, dv_ref, dq_acc, dk_acc, dv_acc):
        kb = pl.program_id(1)

        @pl.when(kb == 0)
        def _():
            dq_acc[...] = jnp.zeros(dq_acc.shape, F32)

        dk_acc[...] = jnp.zeros(dk_acc.shape, F32)
        dv_acc[...] = jnp.zeros(dv_acc.shape, F32)

        def q_block(q0, masked):
            for h in range(nh):
                qh = q_ref[pl.ds(q0, tb), hs[h]]
                doh = do_ref[pl.ds(q0, tb), hs[h]]
                kh = k_ref[:, hs[h]]
                s = _dot_nt(kh, qh) * ATT_SCALE
                if masked:
                    s = jnp.where(_chunk_mask_t(tb, tb), s, NEG)
                p = jnp.exp(s - lse_ref[h, 0:1, pl.ds(q0, tb)])
                ds = (p * (_dot_nt(v_ref[:, hs[h]], doh) - dl_ref[h, 0:1, pl.ds(q0, tb)]) * ATT_SCALE).astype(MM)
                dv_acc[:, hs[h]] += _dot(p, doh)
                dk_acc[:, hs[h]] += _dot(ds, qh)
                dq_acc[pl.ds(q0, tb), hs[h]] += _dot_tn(ds, kh)

        q_block(pl.multiple_of(kb * tb, tb), True)

        def rest(qb, c_):
            q_block(pl.multiple_of(qb * tb, tb), False)
            return c_

        lax.fori_loop(kb + 1, nb, rest, 0)
        dk_ref[...] = dk_acc[...].astype(MM)
        dv_ref[...] = dv_acc[...].astype(MM)

        @pl.when(kb == nb - 1)
        def _():
            dq_ref[...] = dq_acc[...].astype(MM)

    W = nh * HEAD_PAD
    blk = pl.BlockSpec((tb, W), lambda g, i: (i, g))
    res = pl.BlockSpec((T, W), lambda g, i: (0, g))
    rows = pl.BlockSpec((nh, 8, T), lambda g, i: (g, 0, 0))
    return pl.pallas_call(
        body, grid=(N_HEADS // nh, nb), in_specs=[res, blk, blk, res, rows, rows], out_specs=[res, blk, blk],
        out_shape=[jax.ShapeDtypeStruct(q.shape, MM)] * 3,
        scratch_shapes=[pltpu.VMEM((T, W), F32), pltpu.VMEM((tb, W), F32), pltpu.VMEM((tb, W), F32)],
        name=name, compiler_params=_cparams(("arbitrary", "arbitrary")))(q, k, v, do, lse, delta)


def _mla_proj_bwd(name, dq, dk, dv, C, S1, S2, cq_raw, ckv_raw, x, dxo, w_uq, w_ukv_k, w_ukv_v, w_dq, w_dkv, qg, kvg, g0, sc):
    T, D = x.shape
    HP = N_HEADS * HEAD_PAD

    def body(i, n, rr, cc, oo, aa, ss):
        Cv, S1v, S2v = rr[3][...], rr[4][...], rr[5][...]
        dq_pre = _unrope(rr[0][...].astype(F32), Cv, S1v, S2v).astype(MM)
        oo[0][...] = dq_pre
        dcq = _dot_nt(dq_pre, cc[0][...])
        dcq_raw, dqg = _rms_bwd(rr[6][...], cc[5][...], dcq)
        aa[0][...] += dqg
        dcq_raw = dcq_raw.astype(MM)
        oo[1][...] = dcq_raw
        dkv = rr[1][...]
        dkr = dkv[:, :HEAD_PAD].astype(F32)
        for hh in range(1, N_HEADS):
            dkr = dkr + dkv[:, hh * HEAD_PAD:(hh + 1) * HEAD_PAD].astype(F32)
        lane = lax.broadcasted_iota(jnp.int32, dkr.shape, 1)
        dkr = jnp.where((lane >= QK_NOPE) & (lane < QK_NOPE + QK_ROPE), _unrope(dkr, Cv, S1v, S2v), 0.0)
        dckv = _dot_nt(dkv, cc[1][...]) + _dot_nt(rr[2][...], cc[2][...])
        dckv_raw, dkvg = _rms_bwd(rr[7][...], cc[6][...], dckv)
        aa[1][...] += dkvg
        dckv_all = jnp.concatenate([dckv_raw, dkr], axis=1).astype(MM)
        oo[2][...] = dckv_all
        dh = _dot_nt(dcq_raw, cc[3][...]) + _dot_nt(dckv_all, cc[4][...])
        dx, dsh, dsc, dg0 = _prenorm_bwd(rr[8][...], cc[7][...], cc[8][...], dh)
        oo[3][...] = rr[9][...] + dx
        aa[2][...] += dsh
        aa[3][...] += dsc
        aa[4][...] += dg0

    return _rows(name, body, T, 256,
                 [(dq, 'cur'), (dk, 'cur'), (dv, 'cur'), (C, 'cur'), (S1, 'cur'), (S2, 'cur'), (cq_raw, 'cur'), (ckv_raw, 'cur'),
                  (x, 'cur'), (dxo, 'cur')],
                 [w_uq, w_ukv_k, w_ukv_v, w_dq, w_dkv, qg, kvg, g0, sc],
                 [(HP, MM), (Q_LORA, MM), (KV_LORA + HEAD_PAD, MM), (D, F32)],
                 accs=[(1, Q_LORA), (1, KV_LORA), (1, D), (1, D), (1, D)])


HALO = 32


def _windows(ext, tm, first):
    rolled = {0: ext}
    out = []
    for j in range(CONV_W):
        r = (first + j) % 8
        if r not in rolled:
            rolled[r] = pltpu.roll(ext, ext.shape[0] - r, 0)
        out.append(rolled[r][first + j - r:first + j - r + tm])
    return out


def _conv_glu(name, x, g0, sc, sh, w_pw1, b_pw1):
    T, D = x.shape

    def body(i, n, rr, cc, oo, aa, ss):
        hv = (_rms(rr[0][...], cc[2][...]) * (1.0 + cc[3][...]) + cc[4][...]).astype(MM)
        oo[2][...] = hv
        a = _dot(hv, cc[0][...]) + cc[1][...]
        oo[0][...] = a
        oo[1][...] = a[:, :D] * _sigmoid(a[:, D:])

    return _rows(name, body, T, 512, [(x, 'cur')], [w_pw1, b_pw1, g0, sc, sh], [(2 * D, F32), (D, F32), (D, MM)])


def _layernorm_parts(uc):
    xc = uc - jnp.mean(uc, axis=-1, keepdims=True)
    r = lax.rsqrt(jnp.mean(xc * xc, axis=-1, keepdims=True) + EPS)
    return xc * r, r


def _conv_dw(name, u, w_dw, b_dw, ln_g, ln_b, w_pw2, b_pw2, x, g1, gt):
    T, D = u.shape
    tm = min(256, T)

    def body(i, n, rr, cc, oo, aa, ss):
        ext = jnp.concatenate([jnp.where(i > 0, rr[1][tm - HALO:tm, :], 0.0), rr[0][...]], axis=0)
        uc = jnp.zeros((tm, D), F32) + cc[1][...]
        for kk, win in enumerate(_windows(ext, tm, HALO - (CONV_W - 1))):
            uc = uc + win * cc[0][kk:kk + 1, :]
        xh, _ = _layernorm_parts(uc)
        ln = xh * cc[2][...] + cc[3][...]
        z = (ln * _sigmoid(ln)).astype(MM)
        y = _dot(z, cc[4][...]) + cc[5][...]
        oo[0][...] = uc
        oo[1][...] = z
        oo[2][...] = y
        oo[3][...] = rr[2][...] + cc[7][...] * _rms(y, cc[6][...])

    return _rows(name, body, T, tm, [(u, 'cur'), (u, 'prev'), (x, 'cur')], [w_dw, b_dw, ln_g, ln_b, w_pw2, b_pw2, g1, gt],
                 [(D, F32), (D, MM), (D, F32), (D, F32)])


def _conv_bwd1(name, dxo, y, g1, gt, uc, w_pw2, ln_g, ln_b):
    T, D = uc.shape

    def body(i, n, rr, cc, oo, aa, ss):
        dy, dg1, dgt = _post_bwd_math(rr[0][...], rr[1][...], cc[3][...], cc[4][...])
        aa[3][...] += dg1
        aa[4][...] += dgt
        aa[5][...] += jnp.sum(dy, axis=0, keepdims=True)
        dy = dy.astype(MM)
        oo[1][...] = dy
        dz = _dot_nt(dy, cc[0][...])
        xh, r = _layernorm_parts(rr[2][...])
        g = cc[1][...]
        ln = xh * g + cc[2][...]
        sg = _sigmoid(ln)
        dln = dz * (sg * (1.0 + ln * (1.0 - sg)))
        aa[0][...] += jnp.sum(dln * xh, axis=0, keepdims=True)
        aa[1][...] += jnp.sum(dln, axis=0, keepdims=True)
        dxh = dln * g
        duc = r * (dxh - jnp.mean(dxh, axis=-1, keepdims=True) - xh * jnp.mean(dxh * xh, axis=-1, keepdims=True))
        aa[2][...] += jnp.sum(duc, axis=0, keepdims=True)
        oo[0][...] = duc

    return _rows(name, body, T, 256, [(dxo, 'cur'), (y, 'cur'), (uc, 'cur')], [w_pw2, ln_g, ln_b, g1, gt], [(D, F32), (D, MM)],
                 accs=[(1, D)] * 6)


def _conv_bwd2(name, duc, u, a, x, dxo, w_dw, w_pw1, g0, sc):
    T, D = u.shape
    tm = min(256, T)

    def body(i, n, rr, cc, oo, aa, ss):
        dcur = rr[0][...]
        extd = jnp.concatenate([dcur, jnp.where(i < n - 1, rr[1][0:HALO, :], 0.0)], axis=0)
        extu = jnp.concatenate([jnp.where(i > 0, rr[3][tm - HALO:tm, :], 0.0), rr[2][...]], axis=0)
        wd = _windows(extd, tm, 0)
        wu = _windows(extu, tm, HALO - (CONV_W - 1))
        du = jnp.zeros((tm, D), F32)
        for kk in range(CONV_W):
            du = du + wd[CONV_W - 1 - kk] * cc[0][kk:kk + 1, :]
            aa[0][kk:kk + 1, :] += jnp.sum(dcur * wu[kk], axis=0, keepdims=True)
        av = rr[4][...]
        a1, sg = av[:, :D], _sigmoid(av[:, D:])
        da = jnp.concatenate([du * sg, du * a1 * (sg * (1.0 - sg))], axis=1)
        aa[1][...] += jnp.sum(da, axis=0, keepdims=True)
        da = da.astype(MM)
        oo[0][...] = da
        dx, dsh, dsc, dg0 = _prenorm_bwd(rr[5][...], cc[2][...], cc[3][...], _dot_nt(da, cc[1][...]))
        oo[1][...] = rr[6][...] + dx
        aa[2][...] += dsh
        aa[3][...] += dsc
        aa[4][...] += dg0

    return _rows(name, body, T, tm,
                 [(duc, 'cur'), (duc, 'next'), (u, 'cur'), (u, 'prev'), (a, 'cur'), (x, 'cur'), (dxo, 'cur')],
                 [w_dw, w_pw1, g0, sc], [(2 * D, MM), (D, F32)],
                 accs=[(32, D), (1, 2 * D), (1, D), (1, D), (1, D)])


PHALO = 16


def _pool_fwd(name, h, w, b, scale, x, g1, gt):
    T, D = h.shape
    G = len(POOL_WINDOWS)
    Cg = D // G
    tm = min(256, T)

    def body(i, n, rr, cc, oo, aa, ss):
        ext = ss[0]
        ext[0:PHALO, :] = jnp.where(i > 0, rr[1][tm - PHALO:tm, :], 0.0)
        ext[PHALO:PHALO + tm, :] = rr[0][...]
        t_glob = i * tm + lax.broadcasted_iota(jnp.int32, (tm, 1), 0)
        ps, ys = [], []
        for g, win in enumerate(POOL_WINDOWS):
            cols = slice(g * Cg, (g + 1) * Cg)
            s = ext[pl.ds(PHALO, tm), cols]
            for j in range(1, win):
                s = s + ext[pl.ds(PHALO - j, tm), cols]
            cnt = jnp.minimum(t_glob + 1, win).astype(F32)
            p = (s / cnt - ext[pl.ds(PHALO, tm), cols]).astype(MM)
            ps.append(p)
            ys.append(_dot(p, cc[0][g]) + cc[1][:, cols])
        ypre = jnp.concatenate(ys, axis=1)
        y = ypre * cc[2][...]
        oo[0][...] = jnp.concatenate(ps, axis=1)
        oo[1][...] = ypre
        oo[2][...] = y
        oo[3][...] = rr[2][...] + cc[4][...] * _rms(y, cc[3][...])

    return _rows(name, body, T, tm, [(h, 'cur'), (h, 'prev'), (x, 'cur')], [w, b, scale, g1, gt],
                 [(D, MM), (D, F32), (D, F32), (D, F32)], scratch=[pltpu.VMEM((tm + PHALO, D), F32)])


def _pool_bwd1(name, dxo, y, g1, gt, ypre, scale, w):
    T, D = ypre.shape
    G = len(POOL_WINDOWS)
    Cg = D // G

    def body(i, n, rr, cc, oo, aa, ss):
        dyv, dg1, dgt = _post_bwd_math(rr[0][...], rr[1][...], cc[2][...], cc[3][...])
        aa[2][...] += dg1
        aa[3][...] += dgt
        aa[0][...] += jnp.sum(dyv * rr[2][...], axis=0, keepdims=True)
        dypre = dyv * cc[0][...]
        aa[1][...] += jnp.sum(dypre, axis=0, keepdims=True)
        dypre = dypre.astype(MM)
        oo[1][...] = dypre
        oo[0][...] = jnp.concatenate([_dot_nt(dypre[:, g * Cg:(g + 1) * Cg], cc[1][g]) for g in range(G)], axis=1)

    return _rows(name, body, T, 256, [(dxo, 'cur'), (y, 'cur'), (ypre, 'cur')], [scale, w, g1, gt], [(D, F32), (D, MM)],
                 accs=[(1, D)] * 4)


def _pool_bwd2(name, dp, x, dxo, g0, sc):
    T, D = x.shape
    G = len(POOL_WINDOWS)
    Cg = D // G
    tm = min(256, T)

    def body(i, n, rr, cc, oo, aa, ss):
        ext = ss[0]
        t_glob = i * tm + lax.broadcasted_iota(jnp.int32, (tm, 1), 0)
        dcur = rr[0][...]
        dhs = []
        for g, win in enumerate(POOL_WINDOWS):
            cols = slice(g * Cg, (g + 1) * Cg)
            cnt = jnp.minimum(t_glob + 1, win).astype(F32)
            ext[0:tm, cols] = dcur[:, cols] / cnt
            ext[tm:tm + PHALO, cols] = jnp.where(i < n - 1, rr[1][0:PHALO, cols] * (1.0 / win), 0.0)
        for g, win in enumerate(POOL_WINDOWS):
            cols = slice(g * Cg, (g + 1) * Cg)
            s = ext[pl.ds(0, tm), cols]
            for j in range(1, win):
                s = s + ext[pl.ds(j, tm), cols]
            dhs.append(s - dcur[:, cols])
        dx, dsh, dsc, dg0 = _prenorm_bwd(rr[2][...], cc[0][...], cc[1][...], jnp.concatenate(dhs, axis=1))
        oo[0][...] = rr[3][...] + dx
        aa[0][...] += dsh
        aa[1][...] += dsc
        aa[2][...] += dg0

    return _rows(name, body, T, tm, [(dp, 'cur'), (dp, 'next'), (x, 'cur'), (dxo, 'cur')], [g0, sc], [(D, F32)],
                 accs=[(1, D)] * 3, scratch=[pltpu.VMEM((tm + PHALO, D), F32)])


def _loss_head(x, tgt):
    T, D = x.shape

    def body(i, n, rr, cc, oo, aa, ss):
        err = rr[0][...] - rr[1][...]
        oo[0][...] = err * (1.0 / D)
        aa[0][...] += jnp.sum(err * err, axis=0, keepdims=True)

        @pl.when(i == n - 1)
        def _():
            aa[1][...] = jnp.broadcast_to(jnp.sum(aa[0][...], axis=1, keepdims=True) * (0.5 / D), (1, 128))

    dx, _, loss_row = _rows("loss_head", body, T, 512, [(x, 'cur'), (tgt, 'cur')], [], [(D, F32)], accs=[(1, D), (1, 128)])
    return dx, loss_row


def _adamw(name, w, g, m, v, after=None):
    shape = w.shape
    C = shape[-1]
    R = w.size // C
    w2, g2, m2, v2 = (t.reshape(R, C) for t in (w, g, m, v))
    br = R
    if R * C * 4 > (1 << 20):
        br = 8
        while br * 2 * C * 4 <= (1 << 20) and R % (br * 2) == 0:
            br *= 2
    b1c = 1.0 - ADAM_B1 ** ADAM_STEP
    b2c = 1.0 - ADAM_B2 ** ADAM_STEP

    def body(w_ref, g_ref, m_ref, v_ref, *rest):
        d_ref, mo_ref, vo_ref = rest[-3:]
        gv = g_ref[...]
        mn = ADAM_B1 * m_ref[...] + (1.0 - ADAM_B1) * gv
        vn = ADAM_B2 * v_ref[...] + (1.0 - ADAM_B2) * (gv * gv)
        d_ref[...] = -ADAM_LR * ((mn / b1c) / (jnp.sqrt(vn / b2c) + ADAM_EPS) + ADAM_WD * w_ref[...])
        mo_ref[...] = mn
        vo_ref[...] = vn

    spec = pl.BlockSpec((br, C), lambda r: (r, 0))
    extra = [] if after is None else [after]
    outs = pl.pallas_call(body, grid=(R // br,), in_specs=[spec] * 4 + [pl.BlockSpec(memory_space=pl.ANY)] * len(extra), out_specs=[spec] * 3,
                          out_shape=[jax.ShapeDtypeStruct((R, C), F32)] * 3, name=name,
                          compiler_params=_cparams(("arbitrary",)))(w2, g2, m2, v2, *extra)
    return tuple(t.reshape(shape) for t in outs)


def _layer_shards(g, ax):
    s = g.shape
    r = g.reshape(s[:ax] + (N_DEV, s[ax] // N_DEV) + s[ax + 1:])
    return (jnp.moveaxis(r, ax, 0) if ax else r).reshape(N_DEV, -1)


def _unshard(g, ax):
    r = jnp.moveaxis(g, 0, ax)
    s = r.shape
    return r.reshape(s[:ax] + (s[ax] * s[ax + 1],) + s[ax + 2:])


def _pack(parts, dtype, row_mult):
    lead = parts[0].shape[:-1]
    flat = jnp.concatenate([p.astype(dtype) for p in parts], axis=-1)
    n = flat.shape[-1]
    per = row_mult * 1024
    tot = -(-n // per) * per
    flat = jnp.pad(flat, [(0, 0)] * len(lead) + [(0, tot - n)])
    return flat.reshape(lead + (tot // 1024, 1024))


def _pad_heads(w, lo, hi):
    K = w.shape[0]
    r = w.reshape(K, N_HEADS, -1)[:, :, lo:hi]
    return jnp.pad(r, ((0, 0), (0, 0), (0, HEAD_PAD - (hi - lo)))).reshape(K, N_HEADS * HEAD_PAD)


def kernel(x, c, positions, ada_w, ada_b, norm_g, mla_w_dq, mla_q_norm_g, mla_w_uq, mla_w_dkv, mla_kv_norm_g, mla_w_ukv, mla_w_o, conv_w_pw1, conv_b_pw1, conv_w_dw, conv_b_dw, conv_ln_g, conv_ln_b, conv_w_pw2, conv_b_pw2, pool_w, pool_b, pool_scale, ffn_w1, ffn_w2, loss_target, m_ada_w, m_ada_b, m_norm_g, m_mla_w_dq, m_mla_q_norm_g, m_mla_w_uq, m_mla_w_dkv, m_mla_kv_norm_g, m_mla_w_ukv, m_mla_w_o, m_conv_w_pw1, m_conv_b_pw1, m_conv_w_dw, m_conv_b_dw, m_conv_ln_g, m_conv_ln_b, m_conv_w_pw2, m_conv_b_pw2, m_pool_w, m_pool_b, m_pool_scale, m_ffn_w1, m_ffn_w2, v_ada_w, v_ada_b, v_norm_g, v_mla_w_dq, v_mla_q_norm_g, v_mla_w_uq, v_mla_w_dkv, v_mla_kv_norm_g, v_mla_w_ukv, v_mla_w_o, v_conv_w_pw1, v_conv_b_pw1, v_conv_w_dw, v_conv_b_dw, v_conv_ln_g, v_conv_ln_b, v_conv_w_pw2, v_conv_b_pw2, v_pool_w, v_pool_b, v_pool_scale, v_ffn_w1, v_ffn_w2):
    args = dict(locals())
    W = {n: args[n] for n, _ in WEIGHTS}
    M1 = {n: args['m_' + n] for n, _ in WEIGHTS}
    V2 = {n: args['v_' + n] for n, _ in WEIGHTS}
    D = D_MODEL
    T = x.shape[1]
    L = ffn_w1.shape[0]
    xi, yi, ci = _place()
    me = 4 * xi + 2 * yi + ci
    n_ada = ada_w.shape[2]

    small_sizes = [W[n].size for n in SMALL]
    small_in = _pack([c.reshape(-1)] + [W[n].reshape(-1) for n in SMALL], F32, 8)
    small_all = _ag_small("ag_small_params", small_in).reshape(N_DEV, -1)
    c_all = small_all[:, :D]
    Ws = {}
    off = D
    for n, sz in zip(SMALL, small_sizes):
        Ws[n] = _unshard(small_all[:, off:off + sz].reshape((N_DEV,) + W[n].shape), SHARD_AXIS[n])
        off += sz
    c16 = jnp.pad(c_all, ((0, 16 - N_DEV), (0, 0)))

    ada_b_cols = lax.dynamic_slice_in_dim(ada_b, me * n_ada, n_ada, axis=1).reshape(L, 1, n_ada)
    mod_part = _mod_part(c16, ada_w, ada_b_cols)[:, :N_DEV]
    mod_all = _ag_small("ag_mod", mod_part.reshape(L * N_DEV, n_ada)).reshape(N_DEV, L, N_DEV, n_ada)
    mod_mine = lax.dynamic_index_in_dim(mod_all, me, axis=2, keepdims=False)
    mod = jnp.transpose(mod_mine, (1, 0, 2)).reshape(L, 6, 1, D)

    mla_names = [n for n in BIG if n.startswith('mla')]
    first_items = [(n, W[n][0]) for n in mla_names]
    later_items = [(n, W[n][1:]) for n in mla_names] + [(n, W[n]) for n in BIG if not n.startswith(('mla', 'ffn'))]
    first_all, = _ag_big("ag_weights", [_pack([a.reshape(-1) for _, a in first_items], MM, 32)])
    wf = [ffn_w1.astype(MM), ffn_w2.astype(MM), _pack([a.reshape(-1) for _, a in later_items], MM, 32)]
    wf, first_all, mod = lax.optimization_barrier((wf, first_all, mod))
    wf_land = [lax.dynamic_update_slice(lax.empty((N_DEV,) + w.shape, MM), w[None], (me,) + (0,) * w.ndim) for w in wf]
    ag_sems, wf_thru, wf_land, ag_token = _copies_start("ag_ffn_start", wf, wf_land, FIRST_LEVEL_PEERS, False)

    def unpack(g, items, dropped):
        flat, out, off = g.reshape(N_DEV, -1), {}, 0
        for n, a in items:
            out[n] = _unshard(flat[:, off:off + a.size].reshape((N_DEV,) + a.shape), SHARD_AXIS[n] - dropped)
            off += a.size
        return out

    n_mla = mla_w_dq.shape[0]
    w_dq, w_uq_p, w_ukv_k, w_ukv_v, w_dkv_p, w_o_p = ([None] * n_mla for _ in range(6))

    def set_mla(j, w):
        w_dq[j] = w['mla_w_dq']
        w_uq_p[j] = _pad_heads(w['mla_w_uq'], 0, QK_NOPE + QK_ROPE)
        w_ukv_k[j] = _pad_heads(w['mla_w_ukv'], 0, QK_NOPE)
        w_ukv_v[j] = _pad_heads(w['mla_w_ukv'], QK_NOPE, QK_NOPE + V_HEAD)
        w_dkv_p[j] = jnp.pad(jnp.concatenate([w['mla_w_dkv'][:, :KV_LORA], jnp.zeros((D, QK_NOPE), MM), w['mla_w_dkv'][:, KV_LORA:]], axis=1),
                             ((0, 0), (0, HEAD_PAD - QK_NOPE - QK_ROPE)))
        w_o_p[j] = jnp.pad(w['mla_w_o'].reshape(N_HEADS, V_HEAD, D), ((0, 0), (0, HEAD_PAD - V_HEAD), (0, 0))).reshape(N_HEADS * HEAD_PAD, D)

    set_mla(0, unpack(first_all, first_items, 1))
    w_dw32 = jnp.pad(Ws['conv_w_dw'], ((0, 0), (0, 32 - CONV_W), (0, 0)))
    row = lambda t: t.reshape(1, -1)

    half = QK_ROPE // 2
    inv_freq = ROPE_THETA ** (-jnp.arange(0, QK_ROPE, 2, dtype=F32) / QK_ROPE)
    invf = jnp.zeros((1, HEAD_PAD), F32).at[0, QK_NOPE:QK_NOPE + half].set(inv_freq).at[0, QK_NOPE + half:QK_NOPE + QK_ROPE].set(inv_freq)
    rC, rS1, rS2 = _rope_tables(positions.reshape(T, 1).astype(F32), invf)

    xs = x.reshape(T, D)
    saved = []
    for i in range(L):
        kind, j = i % 3, i // 3
        sh_m, sc_m, gt_m, sh_f, sc_f, gt_f = (mod[i, r] for r in range(6))
        g = [row(Ws['norm_g'][i, r]) for r in range(4)]
        st = dict(x0=xs)
        if i == 0:
            sc_m = sc_m + ag_token[0:1, 0:1]
        if kind == 0:
            cq_raw, cq, ckv_raw, ckv, q, k, v, h = _mla_proj(f"mla_proj{i}", xs, g[0], sc_m, sh_m, rC, rS1, rS2, w_dq[j],
                                                             row(Ws['mla_q_norm_g'][j]), w_uq_p[j], w_dkv_p[j],
                                                             row(Ws['mla_kv_norm_g'][j]), w_ukv_k[j], w_ukv_v[j])
            o, lse = _attn_fwd(f"attn_fwd{i}", q, k, v)
            y, xs = _mm_post(f"mla_out{i}", o, w_o_p[j], None, xs, g[1], gt_m)
            st.update(h=h, cq_raw=cq_raw, cq=cq, ckv_raw=ckv_raw, ckv=ckv, q=q, k=k, v=v, o=o, lse=lse, y=y)
        elif kind == 1:
            a, u, h = _conv_glu(f"conv_glu{i}", xs, g[0], sc_m, sh_m, w_pw1[j], row(W['conv_b_pw1'][j]))
            uc, z, y, xs = _conv_dw(f"conv_dw{i}", u, w_dw32[j], row(W['conv_b_dw'][j]), row(W['conv_ln_g'][j]), row(W['conv_ln_b'][j]),
                                    w_pw2[j], row(W['conv_b_pw2'][j]), xs, g[1], gt_m)
            st.update(h=h, a=a, u=u, uc=uc, z=z, y=y)
        else:
            h = _prenorm(f"prenorm_m{i}", xs, g[0], sc_m, sh_m, F32)
            p, ypre, y, xs = _pool_fwd(f"pool_fwd{i}", h, w_pool[j], row(Ws['pool_b'][j]), row(Ws['pool_scale'][j]), xs, g[1], gt_m)
            st.update(p=p, ypre=ypre, y=y)
        st['x1'] = xs
        if i == 0:
            wg = _copies_wait("ag_ffn_wait", ag_sems, wf_thru, wf_land, xs, FIRST_LEVEL_PEERS, False)
            w1g, w2g, later_all = _ag_forward("ag_ffn_forward", wg)
            later = unpack(later_all, later_items, 0)
            for jj in range(1, n_mla):
                set_mla(jj, {n: later[n][jj - 1] for n in mla_names})
            w_pw1, w_pw2, w_pool = later['conv_w_pw1'], later['conv_w_pw2'], later['pool_w']
        hf, af, yf, xs = _ffn_fwd(f"ffn_fwd{i}", i, xs, g[2], sc_f, sh_f, w1g, w2g, g[3], gt_f)
        st.update(hf=hf, af=af, yf=yf)
        saved.append(st)

    dx, loss_row = _loss_head(xs, loss_target.reshape(T, D))

    G = {}
    dmod = [None] * L
    dnorm = [None] * L
    rs_pending = None
    ffn_red = [lax.empty(ffn_w1.shape, F32), lax.empty(ffn_w2.shape, F32)]
    for i in reversed(range(L)):
        kind, j = i % 3, i // 3
        sh_m, sc_m, gt_m, sh_f, sc_f, gt_f = (mod[i, r] for r in range(6))
        g = [row(Ws['norm_g'][i, r]) for r in range(4)]
        st = saved[i]
        dy, da, dx, dg3, dgt_f, dsh_f, dsc_f, dg2 = _ffn_bwd(f"ffn_bwd{i}", i, st['yf'], g[3], gt_f, st['af'], w1g, w2g, st['x1'], dx, g[2], sc_f)
        wire1, own1 = _mm_tn_wire(f"ffn_dw1_{i}", st['hf'], da, me, False, False)
        wire2, own2 = _mm_tn_wire(f"ffn_dw2_{i}", st['af'], dy, me, True, True)
        if rs_pending is not None:
            ffn_red = _rs_finish(rs_pending, wire2, me, ffn_red)
        wires = [wire1, wire2]
        rs_sems, wires_thru, rs_lands, rs_token = _copies_start(f"rs_start{i}", wires, [lax.empty(w.shape, MM) for w in wires], ALL_PEERS, True)
        rs_pending = (i, rs_sems, wires_thru, rs_lands, [own1, own2])
        gt_m = gt_m + rs_token[0:1, 0:1]
        if kind == 0:
            dy, do, dg1, dgt_m = _post_bwd_nt(f"mla_do{i}", dx, st['y'], g[1], gt_m, w_o_p[j])
            delta = _attn_delta(f"attn_delta{i}", do, st['o'])
            dq, dk, dv = _attn_bwd(f"attn_bwd{i}", st['q'], st['k'], st['v'], do, st['lse'], delta)
            dq_pre, dcq_raw, dckv_all, dx, dqg, dkvg, dsh_m, dsc_m, dg0 = _mla_proj_bwd(
                f"mla_proj_bwd{i}", dq, dk, dv, rC, rS1, rS2, st['cq_raw'], st['ckv_raw'], st['x0'], dx, w_uq_p[j], w_ukv_k[j], w_ukv_v[j],
                w_dq[j], w_dkv_p[j], row(Ws['mla_q_norm_g'][j]), row(Ws['mla_kv_norm_g'][j]), g[0], sc_m)
            dwo = _mm_tn(f"mla_dwo{i}", st['o'], dy)
            dwuq = _mm_tn(f"mla_dwuq{i}", st['cq'], dq_pre)
            dwk = _mm_tn(f"mla_dwukvk{i}", st['ckv'], dk)
            dwv = _mm_tn(f"mla_dwukvv{i}", st['ckv'], dv)
            dwdq = _mm_tn(f"mla_dwdq{i}", st['h'], dcq_raw)
            dwdkv = _mm_tn(f"mla_dwdkv{i}", st['h'], dckv_all)
            G.setdefault('mla_w_o', [None] * n_mla)[j] = dwo.reshape(N_HEADS, HEAD_PAD, D)[:, :V_HEAD].reshape(N_HEADS * V_HEAD, D)
            G.setdefault('mla_w_uq', [None] * n_mla)[j] = dwuq.reshape(Q_LORA, N_HEADS, HEAD_PAD)[:, :, :QK_NOPE + QK_ROPE].reshape(Q_LORA, -1)
            G.setdefault('mla_w_ukv', [None] * n_mla)[j] = jnp.concatenate(
                [dwk.reshape(KV_LORA, N_HEADS, HEAD_PAD)[:, :, :QK_NOPE], dwv.reshape(KV_LORA, N_HEADS, HEAD_PAD)[:, :, :V_HEAD]], axis=2).reshape(KV_LORA, -1)
            G.setdefault('mla_w_dq', [None] * n_mla)[j] = dwdq
            G.setdefault('mla_w_dkv', [None] * n_mla)[j] = jnp.concatenate([dwdkv[:, :KV_LORA], dwdkv[:, KV_LORA + QK_NOPE:KV_LORA + QK_NOPE + QK_ROPE]], axis=1)
            G.setdefault('mla_q_norm_g', [None] * n_mla)[j] = dqg[0]
            G.setdefault('mla_kv_norm_g', [None] * n_mla)[j] = dkvg[0]
        elif kind == 1:
            duc, dy, dlng, dlnb, dbdw, dg1, dgt_m, dysum = _conv_bwd1(f"conv_bwd1_{i}", dx, st['y'], g[1], gt_m, st['uc'], w_pw2[j],
                                                                      row(W['conv_ln_g'][j]), row(W['conv_ln_b'][j]))
            da, dx, dwdw, dbpw1, dsh_m, dsc_m, dg0 = _conv_bwd2(f"conv_bwd2_{i}", duc, st['u'], st['a'], st['x0'], dx, w_dw32[j], w_pw1[j], g[0], sc_m)
            G['conv_w_pw2'] = [_mm_tn(f"conv_dwpw2_{i}", st['z'], dy)]
            G['conv_w_pw1'] = [_mm_tn(f"conv_dwpw1_{i}", st['h'], da)]
            G['conv_w_dw'] = [dwdw[:CONV_W]]
            G['conv_b_pw1'], G['conv_b_dw'], G['conv_ln_g'], G['conv_ln_b'], G['conv_b_pw2'] = [dbpw1[0]], [dbdw[0]], [dlng[0]], [dlnb[0]], [dysum[0]]
        else:
            dp, dypre, dscale, dpb, dg1, dgt_m = _pool_bwd1(f"pool_bwd1_{i}", dx, st['y'], g[1], gt_m, st['ypre'], row(Ws['pool_scale'][j]), w_pool[j])
            dx, dsh_m, dsc_m, dg0 = _pool_bwd2(f"pool_bwd2_{i}", dp, st['x0'], dx, g[0], sc_m)
            G['pool_w'] = [_mm_tn(f"pool_dw{i}", st['p'], dypre, diag=len(POOL_WINDOWS))]
            G['pool_b'] = [dpb.reshape(len(POOL_WINDOWS), -1)]
            G['pool_scale'] = [dscale[0]]
        dmod[i] = jnp.concatenate([dsh_m, dsc_m, dgt_m, dsh_f, dsc_f, dgt_f], axis=1)
        dnorm[i] = jnp.concatenate([dg0, dg1, dg2, dg3], axis=0)
    G['norm_g'] = dnorm
    grad_x = dx.reshape(x.shape)

    rs_names = [n for n, ax in WEIGHTS if ax is not None and n != 'ada_w' and not n.startswith('ffn')]
    pieces = [(n, _layer_shards(g, SHARD_AXIS[n] - 1)) for n in rs_names for g in G[n]]
    big = [(n, p) for n, p in pieces if p.shape[1] % (8 * 1024) == 0]
    small = [(n, p) for n, p in pieces if p.shape[1] % (8 * 1024) != 0]
    packed = jnp.concatenate([p.reshape(N_DEV, -1, 1024) for _, p in big] + [_pack([p for _, p in small], F32, 8)], axis=1)
    ffn_red = _rs_finish(rs_pending, dx, me, ffn_red)
    my_chip = 2 * xi + yi
    p4 = packed.reshape((4, 2) + packed.shape[1:])
    pair_recv, = _rs_pair("rs_pair", [p4])
    chip_wire, chip_own = _pair_sum("rs_pair_sum", p4, pair_recv, ci, my_chip)

    dmod_mine = jnp.concatenate(dmod, axis=1).reshape(-1)
    fin_in = _pack([dmod_mine] + [G[n][0].reshape(-1) for n in REPL] + [loss_row.reshape(-1)], F32, 8)
    fin_all = _ag_small("ag_final", fin_in)
    chip_wire, fin_all = lax.optimization_barrier((chip_wire, fin_all))
    chip_sems, chip_thru, chip_land, chip_token = _copies_start("rs_chips_start", [chip_wire], [lax.empty(chip_wire.shape, MM)],
                                                                CHIP_PEERS, 'chip')
    grads = {'ffn_w1': ffn_red[0], 'ffn_w2': ffn_red[1]}
    fin_sum = _sum_devices("final_sum", fin_all).reshape(-1)
    nm = L * 6 * D
    grads['ada_b'] = fin_sum[:nm].reshape(L, 6 * D)
    off = nm
    for n in REPL:
        grads[n] = fin_sum[off:off + W[n].size].reshape(W[n].shape)
        off += W[n].size
    loss = fin_sum[off]
    dmod_all = fin_all.reshape(N_DEV, -1)[:, :nm].reshape(N_DEV, L, 6 * D)
    dmod_cols = lax.dynamic_slice_in_dim(dmod_all, me * n_ada, n_ada, axis=2)
    dmod16 = jnp.pad(jnp.transpose(dmod_cols, (1, 0, 2)), ((0, 0), (0, 16 - N_DEV), (0, 0)))
    grads['ada_w'] = _ada_w_grad(c16, dmod16)
    deltas, new_m, new_v = {}, {}, {}
    done = chip_token
    for n in ['ffn_w1', 'ffn_w2', 'ada_w', 'ada_b'] + REPL:
        deltas[n], new_m[n], new_v[n] = _adamw("adamw_" + n, W[n], grads[n], M1[n], V2[n], after=done)
        done = deltas[n]
    chip_recv, = _copies_wait("rs_chips_wait", chip_sems, chip_thru, chip_land, done, CHIP_PEERS, 'chip')
    red = _chip_sum("rs_chip_sum", chip_own, chip_recv, my_chip)
    got = {}
    row0 = 0
    for n, p in big:
        rows = p.shape[1] // 1024
        got.setdefault(n, []).append(red[row0:row0 + rows])
        row0 += rows
    tail = red[row0:].reshape(-1)
    off = 0
    for n, p in small:
        got.setdefault(n, []).append(tail[off:off + p.shape[1]])
        off += p.shape[1]
    for n in rs_names:
        grads[n] = jnp.stack([g_.reshape(W[n].shape[1:]) for g_ in got[n]], axis=0)

    for n in rs_names:
        deltas[n], new_m[n], new_v[n] = _adamw("adamw_" + n, W[n], grads[n], M1[n], V2[n])
    names = [n for n, _ in WEIGHTS]
    return (loss, grad_x, *[grads[n] for n in names], *[deltas[n] for n in names], *[new_m[n] for n in names],
            *[new_v[n] for n in names])
```

```python
import functools
import math

import jax
import jax.numpy as jnp
from jax import lax
from jax.experimental import pallas as pl
from jax.experimental.pallas import tpu as pltpu

F32 = jnp.float32
MM = jnp.bfloat16
EPS = 1e-6
NEG = -1e30
N_DEV = 8
VMEM_LIMIT = 48 * 1024 * 1024
MESH = pl.DeviceIdType.MESH

D_MODEL = 1024
N_HEADS = 16
HEAD_PAD = 128
QK_NOPE, QK_ROPE, V_HEAD = 64, 32, 64
Q_LORA, KV_LORA = 384, 256
CHUNK = 64
CONV_W = 31
POOL_WINDOWS = (2, 4, 8, 16)
ROPE_THETA = 10000.0
ATT_SCALE = 1.0 / math.sqrt(QK_NOPE + QK_ROPE)

ADAM_LR, ADAM_B1, ADAM_B2, ADAM_EPS, ADAM_WD, ADAM_STEP = 0.001, 0.9, 0.999, 1e-08, 0.01, 10

WEIGHTS = [('ada_w', 2), ('ada_b', None), ('norm_g', 2), ('mla_w_dq', 1), ('mla_q_norm_g', 1), ('mla_w_uq', 2),
           ('mla_w_dkv', 1), ('mla_kv_norm_g', 1), ('mla_w_ukv', 2), ('mla_w_o', 1), ('conv_w_pw1', 2),
           ('conv_b_pw1', None), ('conv_w_dw', 2), ('conv_b_dw', None), ('conv_ln_g', None), ('conv_ln_b', None),
           ('conv_w_pw2', 1), ('conv_b_pw2', None), ('pool_w', 2), ('pool_b', 2), ('pool_scale', 1),
           ('ffn_w1', 2), ('ffn_w2', 1)]
SHARD_AXIS = dict(WEIGHTS)
BIG = ['mla_w_dq', 'mla_w_uq', 'mla_w_dkv', 'mla_w_ukv', 'mla_w_o', 'conv_w_pw1', 'conv_w_pw2', 'pool_w', 'ffn_w1', 'ffn_w2']
SMALL = ['norm_g', 'mla_q_norm_g', 'mla_kv_norm_g', 'conv_w_dw', 'pool_b', 'pool_scale']
REPL = ['conv_b_pw1', 'conv_b_dw', 'conv_ln_g', 'conv_ln_b', 'conv_b_pw2']


def _dot(a, b):
    return jnp.dot(a.astype(MM), b.astype(MM), preferred_element_type=F32)


def _dot_nt(a, b):
    return lax.dot_general(a.astype(MM), b.astype(MM), (((1,), (1,)), ((), ())), preferred_element_type=F32)


def _dot_tn(a, b):
    return lax.dot_general(a.astype(MM), b.astype(MM), (((0,), (0,)), ((), ())), preferred_element_type=F32)


def _sigmoid(x):
    return 1.0 / (1.0 + jnp.exp(-x))


def _rstd(x):
    return lax.rsqrt(jnp.mean(x * x, axis=-1, keepdims=True) + EPS)


def _rms(x, g):
    return x * _rstd(x) * g


def _rms_bwd(x, g, dout):
    r = _rstd(x)
    xn = x * r
    dg = jnp.sum(dout * xn, axis=0, keepdims=True)
    dxn = dout * g
    dx = r * (dxn - xn * jnp.mean(dxn * xn, axis=-1, keepdims=True))
    return dx, dg


def _prenorm_bwd(x, g0, sc, dh):
    r = _rstd(x)
    xn = x * r
    dsh = jnp.sum(dh, axis=0, keepdims=True)
    dsc = jnp.sum(dh * (xn * g0), axis=0, keepdims=True)
    dn = dh * (1.0 + sc)
    dg0 = jnp.sum(dn * xn, axis=0, keepdims=True)
    dxn = dn * g0
    dx = r * (dxn - xn * jnp.mean(dxn * xn, axis=-1, keepdims=True))
    return dx, dsh, dsc, dg0


def _cparams(sem, vmem=VMEM_LIMIT):
    return pltpu.CompilerParams(dimension_semantics=sem, vmem_limit_bytes=vmem)


def _rows(name, body, n_rows, tm, rows, consts, outs, accs=(), scratch=()):
    tm = min(tm, n_rows)
    nblk = n_rows // tm
    nr, nc, no, na = len(rows), len(consts), len(outs), len(accs)
    in_specs, args = [], []
    for a, kind in rows:
        if kind == 'cur':
            im = lambda i: (i, 0)
        elif kind == 'prev':
            im = lambda i: (jnp.maximum(i - 1, 0), 0)
        else:
            im = lambda i: (jnp.minimum(i + 1, nblk - 1), 0)
        in_specs.append(pl.BlockSpec((tm, a.shape[1]), im))
        args.append(a)
    for a in consts:
        in_specs.append(pl.BlockSpec(a.shape, lambda i, nd=a.ndim: (0,) * nd))
        args.append(a)
    out_specs = [pl.BlockSpec((tm, c), lambda i: (i, 0)) for c, _ in outs]
    out_specs += [pl.BlockSpec(s, lambda i, nd=len(s): (0,) * nd) for s in accs]
    out_shape = [jax.ShapeDtypeStruct((n_rows, c), dt) for c, dt in outs]
    out_shape += [jax.ShapeDtypeStruct(s, F32) for s in accs]

    def kern(*refs):
        i = pl.program_id(0)
        rr = refs[:nr]
        cc = refs[nr:nr + nc]
        oo = refs[nr + nc:nr + nc + no]
        aa = refs[nr + nc + no:nr + nc + no + na]
        ss = refs[nr + nc + no + na:]

        @pl.when(i == 0)
        def _():
            for a in aa:
                a[...] = jnp.zeros(a.shape, F32)

        body(i, nblk, rr, cc, oo, aa, ss)

    return pl.pallas_call(kern, grid=(nblk,), in_specs=in_specs, out_specs=out_specs, out_shape=out_shape,
                          scratch_shapes=list(scratch), name=name, compiler_params=_cparams(("arbitrary",)))(*args)


def _place():
    return lax.axis_index("x"), lax.axis_index("y"), lax.axis_index("c")


def _ag_small(name, xs):
    R, C = xs.shape

    def body(x_ref, out_ref, send_sems, recv_sems):
        x, y, c = _place()
        me = 4 * x + 2 * y + c
        out_ref[me] = x_ref[...]
        copies = []
        for k in range(1, N_DEV):
            peer = ((1 - x) if k & 4 else x, (1 - y) if k & 2 else y, (1 - c) if k & 1 else c)
            cp = pltpu.make_async_remote_copy(src_ref=x_ref, dst_ref=out_ref.at[me], send_sem=send_sems.at[k - 1],
                                              recv_sem=recv_sems.at[k - 1], device_id=peer, device_id_type=MESH)
            cp.start()
            copies.append(cp)
        for cp in copies:
            cp.wait()

    return pl.pallas_call(
        body, out_shape=jax.ShapeDtypeStruct((N_DEV, R, C), xs.dtype),
        in_specs=[pl.BlockSpec(memory_space=pltpu.VMEM)], out_specs=pl.BlockSpec(memory_space=pltpu.VMEM),
        scratch_shapes=[pltpu.SemaphoreType.DMA((N_DEV - 1,)), pltpu.SemaphoreType.DMA((N_DEV - 1,))], name=name)(xs)


def _ag_big(name, xs):
    nt = len(xs)

    def body(*refs):
        x_refs, out_refs = refs[:nt], refs[nt:2 * nt]
        send_sems, recv_sems, local_sems = refs[2 * nt:]
        x, y, c = _place()
        me, sibling = (x, y, c), (x, y, 1 - c)
        chips = [(1 - x, y), (x, 1 - y), (1 - x, 1 - y)]

        def copy(t, k, block, to, own=False):
            px, py, pc = block
            rows = out_refs[t].at[4 * px + 2 * py + pc]
            return pltpu.make_async_remote_copy(src_ref=x_refs[t] if own else rows, dst_ref=rows, send_sem=send_sems.at[7 * t + k],
                                                recv_sem=recv_sems.at[7 * t + k], device_id=to, device_id_type=MESH)

        mine = [pltpu.make_async_copy(x_refs[t], out_refs[t].at[4 * x + 2 * y + c], local_sems.at[t]) for t in range(nt)]
        for cp in mine:
            cp.start()
        first = []
        for t in range(nt):
            first.append(copy(t, 0, me, sibling, own=True))
            first += [copy(t, 1 + j, me, (*chip, c), own=True) for j, chip in enumerate(chips)]
        for cp in first:
            cp.start()
        passed = []
        for t in range(nt):
            for j, chip in enumerate(chips):
                copy(t, 1 + j, (*chip, c), me).wait_recv()
                cp = copy(t, 4 + j, (*chip, c), sibling)
                cp.start()
                passed.append(cp)
        for t in range(nt):
            copy(t, 0, sibling, me).wait_recv()
            for j, chip in enumerate(chips):
                copy(t, 4 + j, (*chip, 1 - c), me).wait_recv()
        for cp in first + passed:
            cp.wait_send()
        for cp in mine:
            cp.wait()

    hbm = pl.BlockSpec(memory_space=pl.ANY)
    return pl.pallas_call(
        body, out_shape=[jax.ShapeDtypeStruct((N_DEV,) + t.shape, t.dtype) for t in xs],
        in_specs=[hbm] * nt, out_specs=[hbm] * nt,
        scratch_shapes=[pltpu.SemaphoreType.DMA((7 * nt,)), pltpu.SemaphoreType.DMA((7 * nt,)), pltpu.SemaphoreType.DMA((nt,))],
        name=name)(*xs)


def _rs_pair(name, ps):
    nt = len(ps)

    def body(*refs):
        p_refs, recv_refs = refs[:nt], refs[nt:2 * nt]
        send_sems, recv_sems = refs[2 * nt:]
        x, y, c = _place()
        copies = []
        for t in range(nt):
            for j in range(4):
                cp = pltpu.make_async_remote_copy(src_ref=p_refs[t].at[j, 1 - c], dst_ref=recv_refs[t].at[j], send_sem=send_sems.at[4 * t + j],
                                                  recv_sem=recv_sems.at[4 * t + j], device_id=(x, y, 1 - c), device_id_type=MESH)
                cp.start()
                copies.append(cp)
        for cp in copies:
            cp.wait()

    hbm = pl.BlockSpec(memory_space=pl.ANY)
    return pl.pallas_call(
        body, out_shape=[jax.ShapeDtypeStruct((4,) + p.shape[2:], p.dtype) for p in ps], in_specs=[hbm] * nt, out_specs=[hbm] * nt,
        scratch_shapes=[pltpu.SemaphoreType.DMA((4 * nt,)), pltpu.SemaphoreType.DMA((4 * nt,))], name=name)(*ps)


RS_ROWS = 256


def _row_block(r):
    return next(t for t in range(RS_ROWS, 0, -16) if r % t == 0)


def _pair_sum(name, p, recv, my_c, my_chip):
    _, _, r, c = p.shape
    tr = _row_block(r)

    def body(sc_ref, p_ref, r_ref, o_ref, own_ref):
        s = p_ref[...] + r_ref[...]
        o_ref[...] = s.astype(MM)

        @pl.when(pl.program_id(1) == sc_ref[1])
        def _():
            own_ref[...] = s

    return pl.pallas_call(
        body, grid_spec=pltpu.PrefetchScalarGridSpec(
            num_scalar_prefetch=1, grid=(r // tr, 4),
            in_specs=[pl.BlockSpec((None, None, tr, c), lambda i, j, sc: (j, sc[0], i, 0)),
                      pl.BlockSpec((None, tr, c), lambda i, j, sc: (j, i, 0))],
            out_specs=[pl.BlockSpec((None, tr, c), lambda i, j, sc: (j, i, 0)), pl.BlockSpec((tr, c), lambda i, j, sc: (i, 0))]),
        out_shape=[jax.ShapeDtypeStruct((4, r, c), MM), jax.ShapeDtypeStruct((r, c), F32)], name=name,
        compiler_params=_cparams(("arbitrary", "arbitrary")))(jnp.stack([my_c, my_chip]), p, recv)


def _chip_sum(name, own, recv, my_chip):
    _, r, c = recv.shape
    tr = _row_block(r)

    def body(sc_ref, own_ref, r_ref, o_ref):
        acc = jnp.zeros((tr, c), F32)
        for j in range(4):
            acc = acc + jnp.where(sc_ref[0] == j, own_ref[...], r_ref[j].astype(F32))
        o_ref[...] = acc

    return pl.pallas_call(
        body, grid_spec=pltpu.PrefetchScalarGridSpec(
            num_scalar_prefetch=1, grid=(r // tr,),
            in_specs=[pl.BlockSpec((tr, c), lambda i, sc: (i, 0)), pl.BlockSpec((4, tr, c), lambda i, sc: (0, i, 0))],
            out_specs=pl.BlockSpec((tr, c), lambda i, sc: (i, 0))),
        out_shape=jax.ShapeDtypeStruct((r, c), F32), name=name,
        compiler_params=_cparams(("arbitrary",)))(my_chip.reshape(1), own, recv)


HBM_SPEC = pl.BlockSpec(memory_space=pltpu.HBM)
SEM_SPEC = pl.BlockSpec(memory_space=pltpu.SEMAPHORE)
SPLIT_EFFECT = pltpu.SideEffectType.DATAFLOW_SIDE_EFFECTING
ALL_PEERS = (1, 2, 3, 4, 5, 6, 7)
FIRST_LEVEL_PEERS = (1, 4, 2, 6)
CHIP_PEERS = (4, 2, 6)


def _split_copies(src_refs, land_refs, sems, masks, src_per_peer):
    n, nt = len(masks), len(src_refs)
    x, y, c = _place()
    by_chip = src_per_peer == 'chip'
    slot = 2 * x + y if by_chip else 4 * x + 2 * y + c
    copies = []
    for t in range(nt):
        for k, mask in enumerate(masks):
            px, py, pc = (1 - x) if mask & 4 else x, (1 - y) if mask & 2 else y, (1 - c) if mask & 1 else c
            src = src_refs[t].at[2 * px + py if by_chip else 4 * px + 2 * py + pc] if src_per_peer else src_refs[t]
            copies.append(pltpu.make_async_remote_copy(src_ref=src, dst_ref=land_refs[t].at[slot], send_sem=sems[t * n + k],
                                                       recv_sem=sems[nt * n + t * n + k], device_id=(px, py, pc), device_id_type=MESH))
    return copies


def _copies_start(name, srcs, lands, masks, src_per_peer):
    nt, ns = len(srcs), 2 * len(masks) * len(srcs)

    def body(*refs):
        for cp in _split_copies(refs[:nt], refs[nt:2 * nt], refs[2 * nt:2 * nt + ns], masks, src_per_peer):
            cp.start()
        token = refs[-1]
        token[...] = jnp.zeros(token.shape, F32)

    outs = pl.pallas_call(
        body, name=name,
        out_shape=(pltpu.SemaphoreType.DMA(()),) * ns + tuple(pltpu.HBM(a.shape, a.dtype) for a in list(srcs) + list(lands))
        + (jax.ShapeDtypeStruct((8, 128), F32),),
        in_specs=(HBM_SPEC,) * (2 * nt), out_specs=(SEM_SPEC,) * ns + (HBM_SPEC,) * (2 * nt) + (pl.BlockSpec(memory_space=pltpu.VMEM),),
        input_output_aliases={t: ns + t for t in range(2 * nt)}, compiler_params=pltpu.CompilerParams(has_side_effects=SPLIT_EFFECT))(
            *[pltpu.with_memory_space_constraint(a, pltpu.HBM) for a in list(srcs) + list(lands)])
    return outs[:ns], outs[ns:ns + nt], outs[ns + nt:ns + 2 * nt], outs[-1]


def _copies_wait(name, sems, srcs_thru, lands_thru, after, masks, src_per_peer):
    nt, ns = len(srcs_thru), len(sems)

    def body(*refs):
        for cp in _split_copies(refs[:nt], refs[nt:2 * nt], refs[2 * nt:2 * nt + ns], masks, src_per_peer):
            cp.wait_send()
            cp.wait_recv()

    thru = list(srcs_thru) + list(lands_thru)
    return pl.pallas_call(
        body, name=name, out_shape=tuple(pltpu.HBM(a.shape, a.dtype) for a in thru),
        in_specs=(HBM_SPEC,) * (2 * nt) + (SEM_SPEC,) * ns + (pl.BlockSpec(memory_space=pl.ANY),), out_specs=(HBM_SPEC,) * (2 * nt),
        input_output_aliases={t: t for t in range(2 * nt)}, compiler_params=pltpu.CompilerParams(has_side_effects=SPLIT_EFFECT))(
            *thru, *sems, after)[nt:]


def _ag_forward(name, gs):
    nt = len(gs)

    def body(*refs):
        o_refs, send_sems, recv_sems = refs[nt:2 * nt], refs[2 * nt], refs[2 * nt + 1]
        x, y, c = _place()
        chips = [(1 - x, y), (x, 1 - y), (1 - x, 1 - y)]

        def copy(t, j, pc):
            rows = o_refs[t].at[4 * chips[j][0] + 2 * chips[j][1] + pc]
            return pltpu.make_async_remote_copy(src_ref=rows, dst_ref=rows, send_sem=send_sems.at[3 * t + j], recv_sem=recv_sems.at[3 * t + j],
                                                device_id=(x, y, 1 - c), device_id_type=MESH)

        for t in range(nt):
            for j in range(3):
                copy(t, j, c).start()
        for t in range(nt):
            for j in range(3):
                copy(t, j, c).wait_send()
                copy(t, j, 1 - c).wait_recv()

    hbm = pl.BlockSpec(memory_space=pl.ANY)
    return pl.pallas_call(body, out_shape=[jax.ShapeDtypeStruct(g.shape, g.dtype) for g in gs], in_specs=[hbm] * nt, out_specs=[hbm] * nt,
                          scratch_shapes=[pltpu.SemaphoreType.DMA((3 * nt,)), pltpu.SemaphoreType.DMA((3 * nt,))],
                          input_output_aliases={t: t for t in range(nt)}, name=name)(*gs)


def _mm_tn_wire(name, a, b, me, sqrelu, shard_rows):
    T, M = a.shape
    N = b.shape[1]
    tk = min(2048, T)
    nk = T // tk
    if shard_rows:
        bm, bn = M // N_DEV, N
        a_spec = pl.BlockSpec((tk, 2 * bm), lambda j, k, m: (k, j))
        b_spec = pl.BlockSpec((tk, bn), lambda j, k, m: (k, 0))
        halves = (slice(0, bm), slice(None)), (slice(bm, 2 * bm), slice(None))
        acc_shape = (2 * bm, bn)
    else:
        bm, bn = M, N // N_DEV
        a_spec = pl.BlockSpec((tk, bm), lambda j, k, m: (k, 0))
        b_spec = pl.BlockSpec((tk, 2 * bn), lambda j, k, m: (k, j))
        halves = (slice(None), slice(0, bn)), (slice(None), slice(bn, 2 * bn))
        acc_shape = (bm, 2 * bn)

    def body(me_ref, a_ref, b_ref, wire_ref, own_ref, acc):
        j, k = pl.program_id(0), pl.program_id(1)

        @pl.when(k == 0)
        def _():
            acc[...] = jnp.zeros(acc.shape, F32)

        av = a_ref[...]
        if sqrelu:
            r = jnp.maximum(av, 0.0)
            av = r * r
        acc[...] += _dot_tn(av, b_ref[...])

        for hh in range(2):
            @pl.when(k == nk - 1)
            def _():
                wire_ref[hh] = acc[halves[hh]].astype(MM)

            @pl.when((k == nk - 1) & (2 * j + hh == me_ref[0]))
            def _():
                own_ref[...] = acc[halves[hh]]

    return pl.pallas_call(
        body, grid_spec=pltpu.PrefetchScalarGridSpec(
            num_scalar_prefetch=1, grid=(N_DEV // 2, nk), in_specs=[a_spec, b_spec],
            out_specs=[pl.BlockSpec((2, bm, bn), lambda j, k, m: (j, 0, 0)), pl.BlockSpec((bm, bn), lambda j, k, m: (0, 0))],
            scratch_shapes=[pltpu.VMEM(acc_shape, F32)]),
        out_shape=[jax.ShapeDtypeStruct((N_DEV, bm, bn), MM), jax.ShapeDtypeStruct((bm, bn), F32)], name=name,
        compiler_params=_cparams(("arbitrary", "arbitrary")))(me.reshape(1), a, b)


def _rs_final(name, own, recv, me, stack, li):
    _, r, c = recv.shape
    tr = RS_ROWS

    def body(me_ref, own_ref, r_ref, s_ref, o_ref):
        acc = jnp.zeros((tr, c), F32)
        for j in range(N_DEV):
            acc = acc + jnp.where(me_ref[0] == j, own_ref[...], r_ref[j].astype(F32))
        o_ref[...] = acc

    return pl.pallas_call(
        body, grid_spec=pltpu.PrefetchScalarGridSpec(
            num_scalar_prefetch=1, grid=(r // tr,),
            in_specs=[pl.BlockSpec((tr, c), lambda i, m: (i, 0)), pl.BlockSpec((N_DEV, tr, c), lambda i, m: (0, i, 0)),
                      pl.BlockSpec(memory_space=pl.ANY)],
            out_specs=pl.BlockSpec((None, tr, c), lambda i, m: (li, i, 0))),
        out_shape=jax.ShapeDtypeStruct(stack.shape, F32), input_output_aliases={3: 0}, name=name,
        compiler_params=_cparams(("arbitrary",)))(me.reshape(1), own, recv, stack)


def _rs_finish(pending, after, me, stacks):
    i, sems, wires_thru, lands, owns = pending
    recvs = _copies_wait(f"rs_wait{i}", sems, wires_thru, lands, after, ALL_PEERS, True)
    return [_rs_final(f"rs_final{i}_{t}", owns[t], recvs[t], me, stacks[t], i) for t in range(len(owns))]


def _mod_part(c16, ada_w, ada_b_cols):
    L, D, n = ada_w.shape

    def body(c_ref, w_ref, b_ref, o_ref):
        cv = c_ref[...]
        o_ref[...] = _dot(cv * _sigmoid(cv), w_ref[...]) + b_ref[...]

    return pl.pallas_call(
        body, grid=(L,), in_specs=[pl.BlockSpec((16, D), lambda i: (0, 0)), pl.BlockSpec((None, D, n), lambda i: (i, 0, 0)),
                                   pl.BlockSpec((None, 1, n), lambda i: (i, 0, 0))],
        out_specs=pl.BlockSpec((None, 16, n), lambda i: (i, 0, 0)), out_shape=jax.ShapeDtypeStruct((L, 16, n), F32),
        name="ada_mod", compiler_params=_cparams(("arbitrary",)))(c16, ada_w, ada_b_cols)


def _ada_w_grad(c16, dmod16):
    L, _, n = dmod16.shape
    D = c16.shape[1]

    def body(c_ref, d_ref, o_ref):
        cv = c_ref[...]
        o_ref[...] = _dot_tn(cv * _sigmoid(cv), d_ref[...])

    return pl.pallas_call(
        body, grid=(L,), in_specs=[pl.BlockSpec((16, D), lambda i: (0, 0)), pl.BlockSpec((None, 16, n), lambda i: (i, 0, 0))],
        out_specs=pl.BlockSpec((None, D, n), lambda i: (i, 0, 0)), out_shape=jax.ShapeDtypeStruct((L, D, n), F32),
        name="ada_w_grad", compiler_params=_cparams(("arbitrary",)))(c16, dmod16)


def _sum_devices(name, g):
    _, R, C = g.shape

    def body(g_ref, o_ref):
        acc = g_ref[0]
        for d in range(1, N_DEV):
            acc = acc + g_ref[d]
        o_ref[...] = acc

    return pl.pallas_call(body, out_shape=jax.ShapeDtypeStruct((R, C), F32), name=name)(g)


def _prenorm(name, x, g0, sc, sh, dtype):
    T, D = x.shape

    def body(i, n, rr, cc, oo, aa, ss):
        oo[0][...] = (_rms(rr[0][...], cc[0][...]) * (1.0 + cc[1][...]) + cc[2][...]).astype(dtype)

    return _rows(name, body, T, 512, [(x, 'cur')], [g0, sc, sh], [(D, dtype)])[0]


def _post_bwd_math(d, yv, g1v, gtv):
    dgt = jnp.sum(d * _rms(yv, g1v), axis=0, keepdims=True)
    dy, dg1 = _rms_bwd(yv, g1v, d * gtv)
    return dy, dg1, dgt


def _post_bwd_nt(name, dxo, y, g1, gt, w):
    T, D = y.shape
    K = w.shape[0]

    def body(i, n, rr, cc, oo, aa, ss):
        dy, dg1, dgt = _post_bwd_math(rr[0][...], rr[1][...], cc[0][...], cc[1][...])
        aa[0][...] += dg1
        aa[1][...] += dgt
        dy = dy.astype(MM)
        oo[0][...] = dy
        oo[1][...] = _dot_nt(dy, cc[2][...]).astype(MM)

    return _rows(name, body, T, 512, [(dxo, 'cur'), (y, 'cur')], [g1, gt, w], [(D, MM), (K, MM)], accs=[(1, D)] * 2)


def _mm_post(name, a, w, bias, x, g1, gt):
    T, D = x.shape
    consts = [w, g1, gt] + ([bias] if bias is not None else [])

    def body(i, n, rr, cc, oo, aa, ss):
        y = _dot(rr[0][...], cc[0][...])
        if bias is not None:
            y = y + cc[3][...]
        oo[0][...] = y
        oo[1][...] = rr[1][...] + cc[2][...] * _rms(y, cc[1][...])

    return _rows(name, body, T, 512, [(a, 'cur'), (x, 'cur')], consts, [(D, F32), (D, F32)])


def _mm_tn(name, a, b, sqrelu=False, col_shards=0, diag=0):
    T, M = a.shape
    N = b.shape[1]
    tk = min(512, T)
    nk = T // tk
    if diag:
        bm, bn = M // diag, N // diag
        grid = (diag, 1, nk)
        a_spec = pl.BlockSpec((tk, bm), lambda g, n, k: (k, g))
        b_spec = pl.BlockSpec((tk, bn), lambda g, n, k: (k, g))
        o_spec = pl.BlockSpec((None, bm, bn), lambda g, n, k: (g, 0, 0))
        o_shape = (diag, bm, bn)
    else:
        bm = min(M, 1024)
        bn = N // col_shards if col_shards else min(N, 1024)
        grid = (M // bm, N // bn, nk)
        a_spec = pl.BlockSpec((tk, bm), lambda m, n, k: (k, m))
        b_spec = pl.BlockSpec((tk, bn), lambda m, n, k: (k, n))
        if col_shards:
            o_spec = pl.BlockSpec((None, bm, bn), lambda m, n, k: (n, m, 0))
            o_shape = (col_shards, M, bn)
        else:
            o_spec = pl.BlockSpec((bm, bn), lambda m, n, k: (m, n))
            o_shape = (M, N)

    def body(a_ref, b_ref, o_ref):
        @pl.when(pl.program_id(2) == 0)
        def _():
            o_ref[...] = jnp.zeros(o_ref.shape, F32)

        av = a_ref[...]
        if sqrelu:
            r = jnp.maximum(av, 0.0)
            av = r * r
        o_ref[...] += _dot_tn(av, b_ref[...])

    return pl.pallas_call(body, grid=grid, in_specs=[a_spec, b_spec], out_specs=o_spec,
                          out_shape=jax.ShapeDtypeStruct(o_shape, F32), name=name,
                          compiler_params=_cparams(("arbitrary", "arbitrary", "arbitrary")))(a, b)


FFN_SHARDS = 4
FFN_BWD_SHARDS = 4
FFN_BWD_VMEM = 56 * 1024 * 1024

def _ffn_fwd(name, li, x, g0, sc, sh, w1g, w2g, g1, gt):
    T, D = x.shape
    nf, tf = w1g.shape[0], w1g.shape[-1]
    F = nf * tf
    tm = min(512, T)

    def body(x_ref, g0_ref, sc_ref, sh_ref, w1_ref, w2_ref, g1_ref, gt_ref, h_ref, a_ref, y_ref, xo_ref, acc):
        f = pl.program_id(1)

        @pl.when(f == 0)
        def _():
            acc[...] = jnp.zeros(acc.shape, F32)
            h_ref[...] = (_rms(x_ref[...], g0_ref[...]) * (1.0 + sc_ref[...]) + sh_ref[...]).astype(MM)

        hv = h_ref[...]
        part = None
        for hh in range(FFN_SHARDS):
            a = _dot(hv, w1_ref[hh])
            a_ref[:, hh * tf:(hh + 1) * tf] = a.astype(MM)
            r = jnp.maximum(a, 0.0)
            p = _dot(r * r, w2_ref[hh])
            part = p if part is None else part + p
        acc[...] += part

        @pl.when(f == nf // FFN_SHARDS - 1)
        def _():
            y = acc[...]
            y_ref[...] = y
            xo_ref[...] = x_ref[...] + gt_ref[...] * _rms(y, g1_ref[...])

    row = lambda t, f: (t, 0)
    one = lambda t, f: (0, 0)
    return pl.pallas_call(
        body, grid=(T // tm, nf // FFN_SHARDS),
        in_specs=[pl.BlockSpec((tm, D), row)] + [pl.BlockSpec((1, D), one)] * 3
        + [pl.BlockSpec((FFN_SHARDS, None, D, tf), lambda t, f: (f, li, 0, 0)), pl.BlockSpec((FFN_SHARDS, None, tf, D), lambda t, f: (f, li, 0, 0)),
           pl.BlockSpec((1, D), one), pl.BlockSpec((1, D), one)],
        out_specs=[pl.BlockSpec((tm, D), row), pl.BlockSpec((tm, FFN_SHARDS * tf), lambda t, f: (t, f)), pl.BlockSpec((tm, D), row),
                   pl.BlockSpec((tm, D), row)],
        out_shape=[jax.ShapeDtypeStruct((T, D), MM), jax.ShapeDtypeStruct((T, F), MM), jax.ShapeDtypeStruct((T, D), F32),
                   jax.ShapeDtypeStruct((T, D), F32)],
        scratch_shapes=[pltpu.VMEM((tm, D), F32)], name=name,
        compiler_params=_cparams(("arbitrary", "arbitrary")))(x, g0, sc, sh, w1g, w2g, g1, gt)


def _ffn_bwd(name, li, y, g1, gt, a, w1g, w2g, x, dxo, g0, sc):
    T, D = x.shape
    nf, tf = w1g.shape[0], w1g.shape[-1]
    F = nf * tf
    tm = min(512, T)
    ns = FFN_BWD_SHARDS

    def body(y_ref, g1_ref, gt_ref, a_ref, w1_ref, w2_ref, x_ref, dxo_ref, g0_ref, sc_ref,
             dy_ref, da_ref, dx_ref, dg1_ref, dgt_ref, dsh_ref, dsc_ref, dg0_ref, acc):
        t, f = pl.program_id(0), pl.program_id(1)

        @pl.when((t == 0) & (f == 0))
        def _():
            for r in (dg1_ref, dgt_ref, dsh_ref, dsc_ref, dg0_ref):
                r[...] = jnp.zeros(r.shape, F32)

        @pl.when(f == 0)
        def _():
            acc[...] = jnp.zeros(acc.shape, F32)
            d, yv, g1v = dxo_ref[...], y_ref[...], g1_ref[...]
            dgt_ref[...] += jnp.sum(d * _rms(yv, g1v), axis=0, keepdims=True)
            dyf, dg1 = _rms_bwd(yv, g1v, d * gt_ref[...])
            dg1_ref[...] += dg1
            dy_ref[...] = dyf.astype(MM)

        dyv = dy_ref[...]
        dyv = dyv + dyv
        part = None
        for hh in range(ns):
            cols = slice(hh * tf, (hh + 1) * tf)
            du = _dot_nt(dyv, w2_ref[hh])
            da = (du * jnp.maximum(a_ref[:, cols], 0.0).astype(F32)).astype(MM)
            da_ref[:, cols] = da
            p = _dot_nt(da, w1_ref[hh])
            part = p if part is None else part + p
        acc[...] += part

        @pl.when(f == nf // ns - 1)
        def _():
            dx, dsh, dsc, dg0 = _prenorm_bwd(x_ref[...], g0_ref[...], sc_ref[...], acc[...])
            dx_ref[...] = dxo_ref[...] + dx
            dsh_ref[...] += dsh
            dsc_ref[...] += dsc
            dg0_ref[...] += dg0

    row = lambda t, f: (t, 0)
    one = lambda t, f: (0, 0)
    blk = lambda t, f: (t, f)
    return pl.pallas_call(
        body, grid=(T // tm, nf // ns),
        in_specs=[pl.BlockSpec((tm, D), row), pl.BlockSpec((1, D), one), pl.BlockSpec((1, D), one), pl.BlockSpec((tm, ns * tf), blk),
                  pl.BlockSpec((ns, None, D, tf), lambda t, f: (f, li, 0, 0)),
                  pl.BlockSpec((ns, None, tf, D), lambda t, f: (f, li, 0, 0)), pl.BlockSpec((tm, D), row), pl.BlockSpec((tm, D), row),
                  pl.BlockSpec((1, D), one), pl.BlockSpec((1, D), one)],
        out_specs=[pl.BlockSpec((tm, D), row), pl.BlockSpec((tm, ns * tf), blk), pl.BlockSpec((tm, D), row)] + [pl.BlockSpec((1, D), one)] * 5,
        out_shape=[jax.ShapeDtypeStruct((T, D), MM), jax.ShapeDtypeStruct((T, F), MM), jax.ShapeDtypeStruct((T, D), F32)]
        + [jax.ShapeDtypeStruct((1, D), F32)] * 5,
        scratch_shapes=[pltpu.VMEM((tm, D), F32)], name=name,
        compiler_params=_cparams(("arbitrary", "arbitrary"), FFN_BWD_VMEM))(y, g1, gt, a, w1g, w2g, x, dxo, g0, sc)


def _rope_tables(pos, invf):
    T = pos.shape[0]

    def body(i, n, rr, cc, oo, aa, ss):
        ang = rr[0][...] * cc[0][...]
        lane = lax.broadcasted_iota(jnp.int32, ang.shape, 1)
        cs, sn = jnp.cos(ang), jnp.sin(ang)
        oo[0][...] = jnp.where((lane >= QK_NOPE) & (lane < QK_NOPE + QK_ROPE), cs, 1.0)
        oo[1][...] = jnp.where((lane >= QK_NOPE) & (lane < QK_NOPE + QK_ROPE // 2), -sn, 0.0)
        oo[2][...] = jnp.where((lane >= QK_NOPE + QK_ROPE // 2) & (lane < QK_NOPE + QK_ROPE), sn, 0.0)

    return _rows("rope_tables", body, T, 512, [(pos, 'cur')], [invf], [(HEAD_PAD, F32)] * 3)


def _rope(v, C, S1, S2):
    n = v.shape[1]
    reps = n // HEAD_PAD
    if reps > 1:
        C, S1, S2 = (jnp.tile(t, (1, reps)) for t in (C, S1, S2))
    return v * C + pltpu.roll(v, n - QK_ROPE // 2, 1) * S1 + pltpu.roll(v, QK_ROPE // 2, 1) * S2


def _unrope(d, C, S1, S2):
    n = d.shape[1]
    reps = n // HEAD_PAD
    if reps > 1:
        C, S1, S2 = (jnp.tile(t, (1, reps)) for t in (C, S1, S2))
    return d * C + pltpu.roll(d * S1, QK_ROPE // 2, 1) + pltpu.roll(d * S2, n - QK_ROPE // 2, 1)


def _mla_proj(name, x, g0, sc, sh, C, S1, S2, w_dq, qg, w_uq, w_dkv, kvg, w_ukv_k, w_ukv_v):
    T, D = x.shape
    HP = N_HEADS * HEAD_PAD

    def body(i, n, rr, cc, oo, aa, ss):
        hv = (_rms(rr[0][...], cc[7][...]) * (1.0 + cc[8][...]) + cc[9][...]).astype(MM)
        oo[7][...] = hv
        Cv, S1v, S2v = rr[1][...], rr[2][...], rr[3][...]
        cq_raw = _dot(hv, cc[0][...])
        cq = _rms(cq_raw, cc[1][...]).astype(MM)
        q = _rope(_dot(cq, cc[2][...]), Cv, S1v, S2v)
        ckv_all = _dot(hv, cc[3][...])
        ckv_raw = ckv_all[:, :KV_LORA]
        ckv = _rms(ckv_raw, cc[4][...]).astype(MM)
        kr = _rope(ckv_all[:, KV_LORA:], Cv, S1v, S2v)
        k = _dot(ckv, cc[5][...]) + jnp.tile(kr, (1, N_HEADS))
        v = _dot(ckv, cc[6][...])
        v = jnp.where(lax.broadcasted_iota(jnp.int32, v.shape, 1) % HEAD_PAD == V_HEAD, 1.0, v)
        oo[0][...] = cq_raw
        oo[1][...] = cq
        oo[2][...] = ckv_raw
        oo[3][...] = ckv
        oo[4][...] = q.astype(MM)
        oo[5][...] = k.astype(MM)
        oo[6][...] = v.astype(MM)

    return _rows(name, body, T, 256, [(x, 'cur'), (C, 'cur'), (S1, 'cur'), (S2, 'cur')],
                 [w_dq, qg, w_uq, w_dkv, kvg, w_ukv_k, w_ukv_v, g0, sc, sh],
                 [(Q_LORA, F32), (Q_LORA, MM), (KV_LORA, F32), (KV_LORA, MM), (HP, MM), (HP, MM), (HP, MM), (D, MM)])


ATT_HEADS = 4
ATT_BLOCK = 512
ATT_FWD_BLOCK = 1024


def _chunk_mask_t(tk, tq):
    ki = lax.broadcasted_iota(jnp.int32, (tk, tq), 0) // CHUNK
    qi = lax.broadcasted_iota(jnp.int32, (tk, tq), 1) // CHUNK
    return ki <= qi


def _attn_fwd(name, q, k, v):
    T = q.shape[0]
    tb = min(ATT_FWD_BLOCK, T)
    nb = T // tb
    nh = ATT_HEADS
    hs = [slice(h * HEAD_PAD, (h + 1) * HEAD_PAD) for h in range(nh)]

    def body(q_ref, k_ref, v_ref, o_ref, lse_ref):
        qb = pl.program_id(1)

        def k_block(k0, masked, st):
            new = []
            for h in range(nh):
                m, acc = st[h]
                s = _dot_nt(k_ref[pl.ds(k0, tb), hs[h]], q_ref[:, hs[h]])
                if masked:
                    s = jnp.where(_chunk_mask_t(tb, tb), s, NEG)
                m_new = jnp.maximum(m, jnp.max(s, axis=0, keepdims=True))
                alpha = jnp.exp((m - m_new) * ATT_SCALE)
                p = jnp.exp((s - m_new) * ATT_SCALE)
                acc = alpha * acc + _dot_tn(v_ref[pl.ds(k0, tb), hs[h]], p)
                new.append((m_new, acc))
            return tuple(new)

        st = tuple((jnp.full((1, tb), NEG, F32), jnp.zeros((HEAD_PAD, tb), F32)) for _ in range(nh))
        st = k_block(pl.multiple_of(qb * tb, tb), True, st)
        st = lax.fori_loop(0, qb, lambda kb, s_: k_block(pl.multiple_of(kb * tb, tb), False, s_), st)
        for h in range(nh):
            m, acc = st[h]
            l = acc[V_HEAD:V_HEAD + 1, :]
            o_ref[:, hs[h]] = (acc / l).T.astype(MM)
            lse_ref[h] = jnp.broadcast_to(m * ATT_SCALE + jnp.log(l), (8, tb))

    blk = pl.BlockSpec((tb, nh * HEAD_PAD), lambda g, i: (i, g))
    res = pl.BlockSpec((T, nh * HEAD_PAD), lambda g, i: (0, g))
    return pl.pallas_call(
        body, grid=(N_HEADS // nh, nb), in_specs=[blk, res, res],
        out_specs=[blk, pl.BlockSpec((nh, 8, tb), lambda g, i: (g, 0, i))],
        out_shape=[jax.ShapeDtypeStruct(q.shape, MM), jax.ShapeDtypeStruct((N_HEADS, 8, T), F32)], name=name,
        compiler_params=_cparams(("arbitrary", "arbitrary")))(q, k, v)


def _attn_delta(name, do, o):
    T = do.shape[0]
    tb = min(256, T)

    def body(do_ref, o_ref, d_ref):
        lane = lax.broadcasted_iota(jnp.int32, (tb, HEAD_PAD), 1) // 8
        cols = jnp.zeros((tb, HEAD_PAD), F32)
        for h in range(N_HEADS):
            hsl = slice(h * HEAD_PAD, (h + 1) * HEAD_PAD)
            r = jnp.sum(do_ref[:, hsl].astype(F32) * o_ref[:, hsl].astype(F32), axis=1, keepdims=True)
            cols = jnp.where(lane == h, r, cols)
        d_ref[...] = cols.T

    spec = pl.BlockSpec((tb, N_HEADS * HEAD_PAD), lambda i: (i, 0))
    out = pl.pallas_call(body, grid=(T // tb,), in_specs=[spec, spec], out_specs=pl.BlockSpec((HEAD_PAD, tb), lambda i: (0, i)),
                         out_shape=jax.ShapeDtypeStruct((HEAD_PAD, T), F32), name=name, compiler_params=_cparams(("arbitrary",)))(do, o)
    return out.reshape(N_HEADS, 8, T)


def _attn_bwd(name, q, k, v, do, lse, delta):
    T = q.shape[0]
    tb = min(ATT_BLOCK, T)
    nb = T // tb
    nh = ATT_HEADS
    hs = [slice(h * HEAD_PAD, (h + 1) * HEAD_PAD) for h in range(nh)]

    def body(q_ref, k_ref, v_ref, do_ref, lse_ref, dl_ref, dq_ref, dk_ref, dv_ref, dq_acc, dk_acc, dv_acc):
        kb = pl.program_id(1)

        @pl.when(kb == 0)
        def _():
            dq_acc[...] = jnp.zeros(dq_acc.shape, F32)

        dk_acc[...] = jnp.zeros(dk_acc.shape, F32)
        dv_acc[...] = jnp.zeros(dv_acc.shape, F32)

        def q_block(q0, masked):
            for h in range(nh):
                qh = q_ref[pl.ds(q0, tb), hs[h]]
                doh = do_ref[pl.ds(q0, tb), hs[h]]
                kh = k_ref[:, hs[h]]
                s = _dot_nt(kh, qh) * ATT_SCALE
                if masked:
                    s = jnp.where(_chunk_mask_t(tb, tb), s, NEG)
                p = jnp.exp(s - lse_ref[h, 0:1, pl.ds(q0, tb)])
                ds = (p * (_dot_nt(v_ref[:, hs[h]], doh) - dl_ref[h, 0:1, pl.ds(q0, tb)]) * ATT_SCALE).astype(MM)
                dv_acc[:, hs[h]] += _dot(p, doh)
                dk_acc[:, hs[h]] += _dot(ds, qh)
                dq_acc[pl.ds(q0, tb), hs[h]] += _dot_tn(ds, kh)

        q_block(pl.multiple_of(kb * tb, tb), True)

        def rest(qb, c_):
            q_block(pl.multiple_of(qb * tb, tb), False)
            return c_

        lax.fori_loop(kb + 1, nb, rest, 0)
        dk_ref[...] = dk_acc[...].astype(MM)
        dv_ref[...] = dv_acc[...].astype(MM)

        @pl.when(kb == nb - 1)
        def _():
            dq_ref[...] = dq_acc[...].astype(MM)

    W = nh * HEAD_PAD
    blk = pl.BlockSpec((tb, W), lambda g, i: (i, g))
    res = pl.BlockSpec((T, W), lambda g, i: (0, g))
    rows = pl.BlockSpec((nh, 8, T), lambda g, i: (g, 0, 0))
    return pl.pallas_call(
        body, grid=(N_HEADS // nh, nb), in_specs=[res, blk, blk, res, rows, rows], out_specs=[res, blk, blk],
        out_shape=[jax.ShapeDtypeStruct(q.shape, MM)] * 3,
        scratch_shapes=[pltpu.VMEM((T, W), F32), pltpu.VMEM((tb, W), F32), pltpu.VMEM((tb, W), F32)],
        name=name, compiler_params=_cparams(("arbitrary", "arbitrary")))(q, k, v, do, lse, delta)


def _mla_proj_bwd(name, dq, dk, dv, C, S1, S2, cq_raw, ckv_raw, x, dxo, w_uq, w_ukv_k, w_ukv_v, w_dq, w_dkv, qg, kvg, g0, sc):
    T, D = x.shape
    HP = N_HEADS * HEAD_PAD

    def body(i, n, rr, cc, oo, aa, ss):
        Cv, S1v, S2v = rr[3][...], rr[4][...], rr[5][...]
        dq_pre = _unrope(rr[0][...].astype(F32), Cv, S1v, S2v).astype(MM)
        oo[0][...] = dq_pre
        dcq = _dot_nt(dq_pre, cc[0][...])
        dcq_raw, dqg = _rms_bwd(rr[6][...], cc[5][...], dcq)
        aa[0][...] += dqg
        dcq_raw = dcq_raw.astype(MM)
        oo[1][...] = dcq_raw
        dkv = rr[1][...]
        dkr = dkv[:, :HEAD_PAD].astype(F32)
        for hh in range(1, N_HEADS):
            dkr = dkr + dkv[:, hh * HEAD_PAD:(hh + 1) * HEAD_PAD].astype(F32)
        lane = lax.broadcasted_iota(jnp.int32, dkr.shape, 1)
        dkr = jnp.where((lane >= QK_NOPE) & (lane < QK_NOPE + QK_ROPE), _unrope(dkr, Cv, S1v, S2v), 0.0)
        dckv = _dot_nt(dkv, cc[1][...]) + _dot_nt(rr[2][...], cc[2][...])
        dckv_raw, dkvg = _rms_bwd(rr[7][...], cc[6][...], dckv)
        aa[1][...] += dkvg
        dckv_all = jnp.concatenate([dckv_raw, dkr], axis=1).astype(MM)
        oo[2][...] = dckv_all
        dh = _dot_nt(dcq_raw, cc[3][...]) + _dot_nt(dckv_all, cc[4][...])
        dx, dsh, dsc, dg0 = _prenorm_bwd(rr[8][...], cc[7][...], cc[8][...], dh)
        oo[3][...] = rr[9][...] + dx
        aa[2][...] += dsh
        aa[3][...] += dsc
        aa[4][...] += dg0

    return _rows(name, body, T, 256,
                 [(dq, 'cur'), (dk, 'cur'), (dv, 'cur'), (C, 'cur'), (S1, 'cur'), (S2, 'cur'), (cq_raw, 'cur'), (ckv_raw, 'cur'),
                  (x, 'cur'), (dxo, 'cur')],
                 [w_uq, w_ukv_k, w_ukv_v, w_dq, w_dkv, qg, kvg, g0, sc],
                 [(HP, MM), (Q_LORA, MM), (KV_LORA + HEAD_PAD, MM), (D, F32)],
                 accs=[(1, Q_LORA), (1, KV_LORA), (1, D), (1, D), (1, D)])


HALO = 32


def _windows(ext, tm, first):
    rolled = {0: ext}
    out = []
    for j in range(CONV_W):
        r = (first + j) % 8
        if r not in rolled:
            rolled[r] = pltpu.roll(ext, ext.shape[0] - r, 0)
        out.append(rolled[r][first + j - r:first + j - r + tm])
    return out


def _conv_glu(name, x, g0, sc, sh, w_pw1, b_pw1):
    T, D = x.shape

    def body(i, n, rr, cc, oo, aa, ss):
        hv = (_rms(rr[0][...], cc[2][...]) * (1.0 + cc[3][...]) + cc[4][...]).astype(MM)
        oo[2][...] = hv
        a = _dot(hv, cc[0][...]) + cc[1][...]
        oo[0][...] = a
        oo[1][...] = a[:, :D] * _sigmoid(a[:, D:])

    return _rows(name, body, T, 512, [(x, 'cur')], [w_pw1, b_pw1, g0, sc, sh], [(2 * D, F32), (D, F32), (D, MM)])


def _layernorm_parts(uc):
    xc = uc - jnp.mean(uc, axis=-1, keepdims=True)
    r = lax.rsqrt(jnp.mean(xc * xc, axis=-1, keepdims=True) + EPS)
    return xc * r, r


def _conv_dw(name, u, w_dw, b_dw, ln_g, ln_b, w_pw2, b_pw2, x, g1, gt):
    T, D = u.shape
    tm = min(256, T)

    def body(i, n, rr, cc, oo, aa, ss):
        ext = jnp.concatenate([jnp.where(i > 0, rr[1][tm - HALO:tm, :], 0.0), rr[0][...]], axis=0)
        uc = jnp.zeros((tm, D), F32) + cc[1][...]
        for kk, win in enumerate(_windows(ext, tm, HALO - (CONV_W - 1))):
            uc = uc + win * cc[0][kk:kk + 1, :]
        xh, _ = _layernorm_parts(uc)
        ln = xh * cc[2][...] + cc[3][...]
        z = (ln * _sigmoid(ln)).astype(MM)
        y = _dot(z, cc[4][...]) + cc[5][...]
        oo[0][...] = uc
        oo[1][...] = z
        oo[2][...] = y
        oo[3][...] = rr[2][...] + cc[7][...] * _rms(y, cc[6][...])

    return _rows(name, body, T, tm, [(u, 'cur'), (u, 'prev'), (x, 'cur')], [w_dw, b_dw, ln_g, ln_b, w_pw2, b_pw2, g1, gt],
                 [(D, F32), (D, MM), (D, F32), (D, F32)])


def _conv_bwd1(name, dxo, y, g1, gt, uc, w_pw2, ln_g, ln_b):
    T, D = uc.shape

    def body(i, n, rr, cc, oo, aa, ss):
        dy, dg1, dgt = _post_bwd_math(rr[0][...], rr[1][...], cc[3][...], cc[4][...])
        aa[3][...] += dg1
        aa[4][...] += dgt
        aa[5][...] += jnp.sum(dy, axis=0, keepdims=True)
        dy = dy.astype(MM)
        oo[1][...] = dy
        dz = _dot_nt(dy, cc[0][...])
        xh, r = _layernorm_parts(rr[2][...])
        g = cc[1][...]
        ln = xh * g + cc[2][...]
        sg = _sigmoid(ln)
        dln = dz * (sg * (1.0 + ln * (1.0 - sg)))
        aa[0][...] += jnp.sum(dln * xh, axis=0, keepdims=True)
        aa[1][...] += jnp.sum(dln, axis=0, keepdims=True)
        dxh = dln * g
        duc = r * (dxh - jnp.mean(dxh, axis=-1, keepdims=True) - xh * jnp.mean(dxh * xh, axis=-1, keepdims=True))
        aa[2][...] += jnp.sum(duc, axis=0, keepdims=True)
        oo[0][...] = duc

    return _rows(name, body, T, 256, [(dxo, 'cur'), (y, 'cur'), (uc, 'cur')], [w_pw2, ln_g, ln_b, g1, gt], [(D, F32), (D, MM)],
                 accs=[(1, D)] * 6)


def _conv_bwd2(name, duc, u, a, x, dxo, w_dw, w_pw1, g0, sc):
    T, D = u.shape
    tm = min(256, T)

    def body(i, n, rr, cc, oo, aa, ss):
        dcur = rr[0][...]
        extd = jnp.concatenate([dcur, jnp.where(i < n - 1, rr[1][0:HALO, :], 0.0)], axis=0)
        extu = jnp.concatenate([jnp.where(i > 0, rr[3][tm - HALO:tm, :], 0.0), rr[2][...]], axis=0)
        wd = _windows(extd, tm, 0)
        wu = _windows(extu, tm, HALO - (CONV_W - 1))
        du = jnp.zeros((tm, D), F32)
        for kk in range(CONV_W):
            du = du + wd[CONV_W - 1 - kk] * cc[0][kk:kk + 1, :]
            aa[0][kk:kk + 1, :] += jnp.sum(dcur * wu[kk], axis=0, keepdims=True)
        av = rr[4][...]
        a1, sg = av[:, :D], _sigmoid(av[:, D:])
        da = jnp.concatenate([du * sg, du * a1 * (sg * (1.0 - sg))], axis=1)
        aa[1][...] += jnp.sum(da, axis=0, keepdims=True)
        da = da.astype(MM)
        oo[0][...] = da
        dx, dsh, dsc, dg0 = _prenorm_bwd(rr[5][...], cc[2][...], cc[3][...], _dot_nt(da, cc[1][...]))
        oo[1][...] = rr[6][...] + dx
        aa[2][...] += dsh
        aa[3][...] += dsc
        aa[4][...] += dg0

    return _rows(name, body, T, tm,
                 [(duc, 'cur'), (duc, 'next'), (u, 'cur'), (u, 'prev'), (a, 'cur'), (x, 'cur'), (dxo, 'cur')],
                 [w_dw, w_pw1, g0, sc], [(2 * D, MM), (D, F32)],
                 accs=[(32, D), (1, 2 * D), (1, D), (1, D), (1, D)])


PHALO = 16


def _pool_fwd(name, h, w, b, scale, x, g1, gt):
    T, D = h.shape
    G = len(POOL_WINDOWS)
    Cg = D // G
    tm = min(256, T)

    def body(i, n, rr, cc, oo, aa, ss):
        ext = ss[0]
        ext[0:PHALO, :] = jnp.where(i > 0, rr[1][tm - PHALO:tm, :], 0.0)
        ext[PHALO:PHALO + tm, :] = rr[0][...]
        t_glob = i * tm + lax.broadcasted_iota(jnp.int32, (tm, 1), 0)
        ps, ys = [], []
        for g, win in enumerate(POOL_WINDOWS):
            cols = slice(g * Cg, (g + 1) * Cg)
            s = ext[pl.ds(PHALO, tm), cols]
            for j in range(1, win):
                s = s + ext[pl.ds(PHALO - j, tm), cols]
            cnt = jnp.minimum(t_glob + 1, win).astype(F32)
            p = (s / cnt - ext[pl.ds(PHALO, tm), cols]).astype(MM)
            ps.append(p)
            ys.append(_dot(p, cc[0][g]) + cc[1][:, cols])
        ypre = jnp.concatenate(ys, axis=1)
        y = ypre * cc[2][...]
        oo[0][...] = jnp.concatenate(ps, axis=1)
        oo[1][...] = ypre
        oo[2][...] = y
        oo[3][...] = rr[2][...] + cc[4][...] * _rms(y, cc[3][...])

    return _rows(name, body, T, tm, [(h, 'cur'), (h, 'prev'), (x, 'cur')], [w, b, scale, g1, gt],
                 [(D, MM), (D, F32), (D, F32), (D, F32)], scratch=[pltpu.VMEM((tm + PHALO, D), F32)])


def _pool_bwd1(name, dxo, y, g1, gt, ypre, scale, w):
    T, D = ypre.shape
    G = len(POOL_WINDOWS)
    Cg = D // G

    def body(i, n, rr, cc, oo, aa, ss):
        dyv, dg1, dgt = _post_bwd_math(rr[0][...], rr[1][...], cc[2][...], cc[3][...])
        aa[2][...] += dg1
        aa[3][...] += dgt
        aa[0][...] += jnp.sum(dyv * rr[2][...], axis=0, keepdims=True)
        dypre = dyv * cc[0][...]
        aa[1][...] += jnp.sum(dypre, axis=0, keepdims=True)
        dypre = dypre.astype(MM)
        oo[1][...] = dypre
        oo[0][...] = jnp.concatenate([_dot_nt(dypre[:, g * Cg:(g + 1) * Cg], cc[1][g]) for g in range(G)], axis=1)

    return _rows(name, body, T, 256, [(dxo, 'cur'), (y, 'cur'), (ypre, 'cur')], [scale, w, g1, gt], [(D, F32), (D, MM)],
                 accs=[(1, D)] * 4)


def _pool_bwd2(name, dp, x, dxo, g0, sc):
    T, D = x.shape
    G = len(POOL_WINDOWS)
    Cg = D // G
    tm = min(256, T)

    def body(i, n, rr, cc, oo, aa, ss):
        ext = ss[0]
        t_glob = i * tm + lax.broadcasted_iota(jnp.int32, (tm, 1), 0)
        dcur = rr[0][...]
        dhs = []
        for g, win in enumerate(POOL_WINDOWS):
            cols = slice(g * Cg, (g + 1) * Cg)
            cnt = jnp.minimum(t_glob + 1, win).astype(F32)
            ext[0:tm, cols] = dcur[:, cols] / cnt
            ext[tm:tm + PHALO, cols] = jnp.where(i < n - 1, rr[1][0:PHALO, cols] * (1.0 / win), 0.0)
        for g, win in enumerate(POOL_WINDOWS):
            cols = slice(g * Cg, (g + 1) * Cg)
            s = ext[pl.ds(0, tm), cols]
            for j in range(1, win):
                s = s + ext[pl.ds(j, tm), cols]
            dhs.append(s - dcur[:, cols])
        dx, dsh, dsc, dg0 = _prenorm_bwd(rr[2][...], cc[0][...], cc[1][...], jnp.concatenate(dhs, axis=1))
        oo[0][...] = rr[3][...] + dx
        aa[0][...] += dsh
        aa[1][...] += dsc
        aa[2][...] += dg0

    return _rows(name, body, T, tm, [(dp, 'cur'), (dp, 'next'), (x, 'cur'), (dxo, 'cur')], [g0, sc], [(D, F32)],
                 accs=[(1, D)] * 3, scratch=[pltpu.VMEM((tm + PHALO, D), F32)])


def _loss_head(x, tgt):
    T, D = x.shape

    def body(i, n, rr, cc, oo, aa, ss):
        err = rr[0][...] - rr[1][...]
        oo[0][...] = err * (1.0 / D)
        aa[0][...] += jnp.sum(err * err, axis=0, keepdims=True)

        @pl.when(i == n - 1)
        def _():
            aa[1][...] = jnp.broadcast_to(jnp.sum(aa[0][...], axis=1, keepdims=True) * (0.5 / D), (1, 128))

    dx, _, loss_row = _rows("loss_head", body, T, 512, [(x, 'cur'), (tgt, 'cur')], [], [(D, F32)], accs=[(1, D), (1, 128)])
    return dx, loss_row


def _adamw(name, w, g, m, v, after=None):
    shape = w.shape
    C = shape[-1]
    R = w.size // C
    w2, g2, m2, v2 = (t.reshape(R, C) for t in (w, g, m, v))
    br = R
    if R * C * 4 > (1 << 20):
        br = 8
        while br * 2 * C * 4 <= (1 << 20) and R % (br * 2) == 0:
            br *= 2
    b1c = 1.0 - ADAM_B1 ** ADAM_STEP
    b2c = 1.0 - ADAM_B2 ** ADAM_STEP

    def body(w_ref, g_ref, m_ref, v_ref, *rest):
        d_ref, mo_ref, vo_ref = rest[-3:]
        gv = g_ref[...]
        mn = ADAM_B1 * m_ref[...] + (1.0 - ADAM_B1) * gv
        vn = ADAM_B2 * v_ref[...] + (1.0 - ADAM_B2) * (gv * gv)
        d_ref[...] = -ADAM_LR * ((mn / b1c) / (jnp.sqrt(vn / b2c) + ADAM_EPS) + ADAM_WD * w_ref[...])
        mo_ref[...] = mn
        vo_ref[...] = vn

    spec = pl.BlockSpec((br, C), lambda r: (r, 0))
    extra = [] if after is None else [after]
    outs = pl.pallas_call(body, grid=(R // br,), in_specs=[spec] * 4 + [pl.BlockSpec(memory_space=pl.ANY)] * len(extra), out_specs=[spec] * 3,
                          out_shape=[jax.ShapeDtypeStruct((R, C), F32)] * 3, name=name,
                          compiler_params=_cparams(("arbitrary",)))(w2, g2, m2, v2, *extra)
    return tuple(t.reshape(shape) for t in outs)


def _layer_shards(g, ax):
    s = g.shape
    r = g.reshape(s[:ax] + (N_DEV, s[ax] // N_DEV) + s[ax + 1:])
    return (jnp.moveaxis(r, ax, 0) if ax else r).reshape(N_DEV, -1)


def _unshard(g, ax):
    r = jnp.moveaxis(g, 0, ax)
    s = r.shape
    return r.reshape(s[:ax] + (s[ax] * s[ax + 1],) + s[ax + 2:])


def _pack(parts, dtype, row_mult):
    lead = parts[0].shape[:-1]
    flat = jnp.concatenate([p.astype(dtype) for p in parts], axis=-1)
    n = flat.shape[-1]
    per = row_mult * 1024
    tot = -(-n // per) * per
    flat = jnp.pad(flat, [(0, 0)] * len(lead) + [(0, tot - n)])
    return flat.reshape(lead + (tot // 1024, 1024))


def _pad_heads(w, lo, hi):
    K = w.shape[0]
    r = w.reshape(K, N_HEADS, -1)[:, :, lo:hi]
    return jnp.pad(r, ((0, 0), (0, 0), (0, HEAD_PAD - (hi - lo)))).reshape(K, N_HEADS * HEAD_PAD)


def kernel(x, c, positions, ada_w, ada_b, norm_g, mla_w_dq, mla_q_norm_g, mla_w_uq, mla_w_dkv, mla_kv_norm_g, mla_w_ukv, mla_w_o, conv_w_pw1, conv_b_pw1, conv_w_dw, conv_b_dw, conv_ln_g, conv_ln_b, conv_w_pw2, conv_b_pw2, pool_w, pool_b, pool_scale, ffn_w1, ffn_w2, loss_target, m_ada_w, m_ada_b, m_norm_g, m_mla_w_dq, m_mla_q_norm_g, m_mla_w_uq, m_mla_w_dkv, m_mla_kv_norm_g, m_mla_w_ukv, m_mla_w_o, m_conv_w_pw1, m_conv_b_pw1, m_conv_w_dw, m_conv_b_dw, m_conv_ln_g, m_conv_ln_b, m_conv_w_pw2, m_conv_b_pw2, m_pool_w, m_pool_b, m_pool_scale, m_ffn_w1, m_ffn_w2, v_ada_w, v_ada_b, v_norm_g, v_mla_w_dq, v_mla_q_norm_g, v_mla_w_uq, v_mla_w_dkv, v_mla_kv_norm_g, v_mla_w_ukv, v_mla_w_o, v_conv_w_pw1, v_conv_b_pw1, v_conv_w_dw, v_conv_b_dw, v_conv_ln_g, v_conv_ln_b, v_conv_w_pw2, v_conv_b_pw2, v_pool_w, v_pool_b, v_pool_scale, v_ffn_w1, v_ffn_w2):
    args = dict(locals())
    W = {n: args[n] for n, _ in WEIGHTS}
    M1 = {n: args['m_' + n] for n, _ in WEIGHTS}
    V2 = {n: args['v_' + n] for n, _ in WEIGHTS}
    D = D_MODEL
    T = x.shape[1]
    L = ffn_w1.shape[0]
    xi, yi, ci = _place()
    me = 4 * xi + 2 * yi + ci
    n_ada = ada_w.shape[2]

    small_sizes = [W[n].size for n in SMALL]
    small_in = _pack([c.reshape(-1)] + [W[n].reshape(-1) for n in SMALL], F32, 8)
    small_all = _ag_small("ag_small_params", small_in).reshape(N_DEV, -1)
    c_all = small_all[:, :D]
    Ws = {}
    off = D
    for n, sz in zip(SMALL, small_sizes):
        Ws[n] = _unshard(small_all[:, off:off + sz].reshape((N_DEV,) + W[n].shape), SHARD_AXIS[n])
        off += sz
    c16 = jnp.pad(c_all, ((0, 16 - N_DEV), (0, 0)))

    ada_b_cols = lax.dynamic_slice_in_dim(ada_b, me * n_ada, n_ada, axis=1).reshape(L, 1, n_ada)
    mod_part = _mod_part(c16, ada_w, ada_b_cols)[:, :N_DEV]
    mod_all = _ag_small("ag_mod", mod_part.reshape(L * N_DEV, n_ada)).reshape(N_DEV, L, N_DEV, n_ada)
    mod_mine = lax.dynamic_index_in_dim(mod_all, me, axis=2, keepdims=False)
    mod = jnp.transpose(mod_mine, (1, 0, 2)).reshape(L, 6, 1, D)

    mla_names = [n for n in BIG if n.startswith('mla')]
    first_items = [(n, W[n][0]) for n in mla_names]
    later_items = [(n, W[n][1:]) for n in mla_names] + [(n, W[n]) for n in BIG if not n.startswith(('mla', 'ffn'))]
    first_all, = _ag_big("ag_weights", [_pack([a.reshape(-1) for _, a in first_items], MM, 32)])
    wf = [ffn_w1.astype(MM), ffn_w2.astype(MM), _pack([a.reshape(-1) for _, a in later_items], MM, 32)]
    wf, first_all, mod = lax.optimization_barrier((wf, first_all, mod))
    wf_land = [lax.dynamic_update_slice(lax.empty((N_DEV,) + w.shape, MM), w[None], (me,) + (0,) * w.ndim) for w in wf]
    ag_sems, wf_thru, wf_land, ag_token = _copies_start("ag_ffn_start", wf, wf_land, FIRST_LEVEL_PEERS, False)

    def unpack(g, items, dropped):
        flat, out, off = g.reshape(N_DEV, -1), {}, 0
        for n, a in items:
            out[n] = _unshard(flat[:, off:off + a.size].reshape((N_DEV,) + a.shape), SHARD_AXIS[n] - dropped)
            off += a.size
        return out

    n_mla = mla_w_dq.shape[0]
    w_dq, w_uq_p, w_ukv_k, w_ukv_v, w_dkv_p, w_o_p = ([None] * n_mla for _ in range(6))

    def set_mla(j, w):
        w_dq[j] = w['mla_w_dq']
        w_uq_p[j] = _pad_heads(w['mla_w_uq'], 0, QK_NOPE + QK_ROPE)
        w_ukv_k[j] = _pad_heads(w['mla_w_ukv'], 0, QK_NOPE)
        w_ukv_v[j] = _pad_heads(w['mla_w_ukv'], QK_NOPE, QK_NOPE + V_HEAD)
        w_dkv_p[j] = jnp.pad(jnp.concatenate([w['mla_w_dkv'][:, :KV_LORA], jnp.zeros((D, QK_NOPE), MM), w['mla_w_dkv'][:, KV_LORA:]], axis=1),
                             ((0, 0), (0, HEAD_PAD - QK_NOPE - QK_ROPE)))
        w_o_p[j] = jnp.pad(w['mla_w_o'].reshape(N_HEADS, V_HEAD, D), ((0, 0), (0, HEAD_PAD - V_HEAD), (0, 0))).reshape(N_HEADS * HEAD_PAD, D)

    set_mla(0, unpack(first_all, first_items, 1))
    w_dw32 = jnp.pad(Ws['conv_w_dw'], ((0, 0), (0, 32 - CONV_W), (0, 0)))
    row = lambda t: t.reshape(1, -1)

    half = QK_ROPE // 2
    inv_freq = ROPE_THETA ** (-jnp.arange(0, QK_ROPE, 2, dtype=F32) / QK_ROPE)
    invf = jnp.zeros((1, HEAD_PAD), F32).at[0, QK_NOPE:QK_NOPE + half].set(inv_freq).at[0, QK_NOPE + half:QK_NOPE + QK_ROPE].set(inv_freq)
    rC, rS1, rS2 = _rope_tables(positions.reshape(T, 1).astype(F32), invf)

    xs = x.reshape(T, D)
    saved = []
    for i in range(L):
        kind, j = i % 3, i // 3
        sh_m, sc_m, gt_m, sh_f, sc_f, gt_f = (mod[i, r] for r in range(6))
        g = [row(Ws['norm_g'][i, r]) for r in range(4)]
        st = dict(x0=xs)
        if i == 0:
            sc_m = sc_m + ag_token[0:1, 0:1]
        if kind == 0:
            cq_raw, cq, ckv_raw, ckv, q, k, v, h = _mla_proj(f"mla_proj{i}", xs, g[0], sc_m, sh_m, rC, rS1, rS2, w_dq[j],
                                                             row(Ws['mla_q_norm_g'][j]), w_uq_p[j], w_dkv_p[j],
                                                             row(Ws['mla_kv_norm_g'][j]), w_ukv_k[j], w_ukv_v[j])
            o, lse = _attn_fwd(f"attn_fwd{i}", q, k, v)
            y, xs = _mm_post(f"mla_out{i}", o, w_o_p[j], None, xs, g[1], gt_m)
            st.update(h=h, cq_raw=cq_raw, cq=cq, ckv_raw=ckv_raw, ckv=ckv, q=q, k=k, v=v, o=o, lse=lse, y=y)
        elif kind == 1:
            a, u, h = _conv_glu(f"conv_glu{i}", xs, g[0], sc_m, sh_m, w_pw1[j], row(W['conv_b_pw1'][j]))
            uc, z, y, xs = _conv_dw(f"conv_dw{i}", u, w_dw32[j], row(W['conv_b_dw'][j]), row(W['conv_ln_g'][j]), row(W['conv_ln_b'][j]),
                                    w_pw2[j], row(W['conv_b_pw2'][j]), xs, g[1], gt_m)
            st.update(h=h, a=a, u=u, uc=uc, z=z, y=y)
        else:
            h = _prenorm(f"prenorm_m{i}", xs, g[0], sc_m, sh_m, F32)
            p, ypre, y, xs = _pool_fwd(f"pool_fwd{i}", h, w_pool[j], row(Ws['pool_b'][j]), row(Ws['pool_scale'][j]), xs, g[1], gt_m)
            st.update(p=p, ypre=ypre, y=y)
        st['x1'] = xs
        if i == 0:
            wg = _copies_wait("ag_ffn_wait", ag_sems, wf_thru, wf_land, xs, FIRST_LEVEL_PEERS, False)
            w1g, w2g, later_all = _ag_forward("ag_ffn_forward", wg)
            later = unpack(later_all, later_items, 0)
            for jj in range(1, n_mla):
                set_mla(jj, {n: later[n][jj - 1] for n in mla_names})
            w_pw1, w_pw2, w_pool = later['conv_w_pw1'], later['conv_w_pw2'], later['pool_w']
        hf, af, yf, xs = _ffn_fwd(f"ffn_fwd{i}", i, xs, g[2], sc_f, sh_f, w1g, w2g, g[3], gt_f)
        st.update(hf=hf, af=af, yf=yf)
        saved.append(st)

    dx, loss_row = _loss_head(xs, loss_target.reshape(T, D))

    G = {}
    dmod = [None] * L
    dnorm = [None] * L
    rs_pending = None
    ffn_red = [lax.empty(ffn_w1.shape, F32), lax.empty(ffn_w2.shape, F32)]
    for i in reversed(range(L)):
        kind, j = i % 3, i // 3
        sh_m, sc_m, gt_m, sh_f, sc_f, gt_f = (mod[i, r] for r in range(6))
        g = [row(Ws['norm_g'][i, r]) for r in range(4)]
        st = saved[i]
        dy, da, dx, dg3, dgt_f, dsh_f, dsc_f, dg2 = _ffn_bwd(f"ffn_bwd{i}", i, st['yf'], g[3], gt_f, st['af'], w1g, w2g, st['x1'], dx, g[2], sc_f)
        wire1, own1 = _mm_tn_wire(f"ffn_dw1_{i}", st['hf'], da, me, False, False)
        wire2, own2 = _mm_tn_wire(f"ffn_dw2_{i}", st['af'], dy, me, True, True)
        if rs_pending is not None:
            ffn_red = _rs_finish(rs_pending, wire2, me, ffn_red)
        wires = [wire1, wire2]
        rs_sems, wires_thru, rs_lands, rs_token = _copies_start(f"rs_start{i}", wires, [lax.empty(w.shape, MM) for w in wires], ALL_PEERS, True)
        rs_pending = (i, rs_sems, wires_thru, rs_lands, [own1, own2])
        gt_m = gt_m + rs_token[0:1, 0:1]
        if kind == 0:
            dy, do, dg1, dgt_m = _post_bwd_nt(f"mla_do{i}", dx, st['y'], g[1], gt_m, w_o_p[j])
            delta = _attn_delta(f"attn_delta{i}", do, st['o'])
            dq, dk, dv = _attn_bwd(f"attn_bwd{i}", st['q'], st['k'], st['v'], do, st['lse'], delta)
            dq_pre, dcq_raw, dckv_all, dx, dqg, dkvg, dsh_m, dsc_m, dg0 = _mla_proj_bwd(
                f"mla_proj_bwd{i}", dq, dk, dv, rC, rS1, rS2, st['cq_raw'], st['ckv_raw'], st['x0'], dx, w_uq_p[j], w_ukv_k[j], w_ukv_v[j],
                w_dq[j], w_dkv_p[j], row(Ws['mla_q_norm_g'][j]), row(Ws['mla_kv_norm_g'][j]), g[0], sc_m)
            dwo = _mm_tn(f"mla_dwo{i}", st['o'], dy)
            dwuq = _mm_tn(f"mla_dwuq{i}", st['cq'], dq_pre)
            dwk = _mm_tn(f"mla_dwukvk{i}", st['ckv'], dk)
            dwv = _mm_tn(f"mla_dwukvv{i}", st['ckv'], dv)
            dwdq = _mm_tn(f"mla_dwdq{i}", st['h'], dcq_raw)
            dwdkv = _mm_tn(f"mla_dwdkv{i}", st['h'], dckv_all)
            G.setdefault('mla_w_o', [None] * n_mla)[j] = dwo.reshape(N_HEADS, HEAD_PAD, D)[:, :V_HEAD].reshape(N_HEADS * V_HEAD, D)
            G.setdefault('mla_w_uq', [None] * n_mla)[j] = dwuq.reshape(Q_LORA, N_HEADS, HEAD_PAD)[:, :, :QK_NOPE + QK_ROPE].reshape(Q_LORA, -1)
            G.setdefault('mla_w_ukv', [None] * n_mla)[j] = jnp.concatenate(
                [dwk.reshape(KV_LORA, N_HEADS, HEAD_PAD)[:, :, :QK_NOPE], dwv.reshape(KV_LORA, N_HEADS, HEAD_PAD)[:, :, :V_HEAD]], axis=2).reshape(KV_LORA, -1)
            G.setdefault('mla_w_dq', [None] * n_mla)[j] = dwdq
            G.setdefault('mla_w_dkv', [None] * n_mla)[j] = jnp.concatenate([dwdkv[:, :KV_LORA], dwdkv[:, KV_LORA + QK_NOPE:KV_LORA + QK_NOPE + QK_ROPE]], axis=1)
            G.setdefault('mla_q_norm_g', [None] * n_mla)[j] = dqg[0]
            G.setdefault('mla_kv_norm_g', [None] * n_mla)[j] = dkvg[0]
        elif kind == 1:
            duc, dy, dlng, dlnb, dbdw, dg1, dgt_m, dysum = _conv_bwd1(f"conv_bwd1_{i}", dx, st['y'], g[1], gt_m, st['uc'], w_pw2[j],
                                                                      row(W['conv_ln_g'][j]), row(W['conv_ln_b'][j]))
            da, dx, dwdw, dbpw1, dsh_m, dsc_m, dg0 = _conv_bwd2(f"conv_bwd2_{i}", duc, st['u'], st['a'], st['x0'], dx, w_dw32[j], w_pw1[j], g[0], sc_m)
            G['conv_w_pw2'] = [_mm_tn(f"conv_dwpw2_{i}", st['z'], dy)]
            G['conv_w_pw1'] = [_mm_tn(f"conv_dwpw1_{i}", st['h'], da)]
            G['conv_w_dw'] = [dwdw[:CONV_W]]
            G['conv_b_pw1'], G['conv_b_dw'], G['conv_ln_g'], G['conv_ln_b'], G['conv_b_pw2'] = [dbpw1[0]], [dbdw[0]], [dlng[0]], [dlnb[0]], [dysum[0]]
        else:
            dp, dypre, dscale, dpb, dg1, dgt_m = _pool_bwd1(f"pool_bwd1_{i}", dx, st['y'], g[1], gt_m, st['ypre'], row(Ws['pool_scale'][j]), w_pool[j])
            dx, dsh_m, dsc_m, dg0 = _pool_bwd2(f"pool_bwd2_{i}", dp, st['x0'], dx, g[0], sc_m)
            G['pool_w'] = [_mm_tn(f"pool_dw{i}", st['p'], dypre, diag=len(POOL_WINDOWS))]
            G['pool_b'] = [dpb.reshape(len(POOL_WINDOWS), -1)]
            G['pool_scale'] = [dscale[0]]
        dmod[i] = jnp.concatenate([dsh_m, dsc_m, dgt_m, dsh_f, dsc_f, dgt_f], axis=1)
        dnorm[i] = jnp.concatenate([dg0, dg1, dg2, dg3], axis=0)
    G['norm_g'] = dnorm
    grad_x = dx.reshape(x.shape)

    rs_names = [n for n, ax in WEIGHTS if ax is not None and n != 'ada_w' and not n.startswith('ffn')]
    pieces = [(n, _layer_shards(g, SHARD_AXIS[n] - 1)) for n in rs_names for g in G[n]]
    big = [(n, p) for n, p in pieces if p.shape[1] % (8 * 1024) == 0]
    small = [(n, p) for n, p in pieces if p.shape[1] % (8 * 1024) != 0]
    packed = jnp.concatenate([p.reshape(N_DEV, -1, 1024) for _, p in big] + [_pack([p for _, p in small], F32, 8)], axis=1)
    ffn_red = _rs_finish(rs_pending, dx, me, ffn_red)
    my_chip = 2 * xi + yi
    p4 = packed.reshape((4, 2) + packed.shape[1:])
    pair_recv, = _rs_pair("rs_pair", [p4])
    chip_wire, chip_own = _pair_sum("rs_pair_sum", p4, pair_recv, ci, my_chip)

    dmod_mine = jnp.concatenate(dmod, axis=1).reshape(-1)
    fin_in = _pack([dmod_mine] + [G[n][0].reshape(-1) for n in REPL] + [loss_row.reshape(-1)], F32, 8)
    fin_all = _ag_small("ag_final", fin_in)
    chip_wire, fin_all = lax.optimization_barrier((chip_wire, fin_all))
    chip_sems, chip_thru, chip_land, chip_token = _copies_start("rs_chips_start", [chip_wire], [lax.empty(chip_wire.shape, MM)],
                                                                CHIP_PEERS, 'chip')
    grads = {'ffn_w1': ffn_red[0], 'ffn_w2': ffn_red[1]}
    fin_sum = _sum_devices("final_sum", fin_all).reshape(-1)
    nm = L * 6 * D
    grads['ada_b'] = fin_sum[:nm].reshape(L, 6 * D)
    off = nm
    for n in REPL:
        grads[n] = fin_sum[off:off + W[n].size].reshape(W[n].shape)
        off += W[n].size
    loss = fin_sum[off]
    dmod_all = fin_all.reshape(N_DEV, -1)[:, :nm].reshape(N_DEV, L, 6 * D)
    dmod_cols = lax.dynamic_slice_in_dim(dmod_all, me * n_ada, n_ada, axis=2)
    dmod16 = jnp.pad(jnp.transpose(dmod_cols, (1, 0, 2)), ((0, 0), (0, 16 - N_DEV), (0, 0)))
    grads['ada_w'] = _ada_w_grad(c16, dmod16)
    deltas, new_m, new_v = {}, {}, {}
    done = chip_token
    for n in ['ffn_w1', 'ffn_w2', 'ada_w', 'ada_b'] + REPL:
        deltas[n], new_m[n], new_v[n] = _adamw("adamw_" + n, W[n], grads[n], M1[n], V2[n], after=done)
        done = deltas[n]
    chip_recv, = _copies_wait("rs_chips_wait", chip_sems, chip_thru, chip_land, done, CHIP_PEERS, 'chip')
    red = _chip_sum("rs_chip_sum", chip_own, chip_recv, my_chip)
    got = {}
    row0 = 0
    for n, p in big:
        rows = p.shape[1] // 1024
        got.setdefault(n, []).append(red[row0:row0 + rows])
        row0 += rows
    tail = red[row0:].reshape(-1)
    off = 0
    for n, p in small:
        got.setdefault(n, []).append(tail[off:off + p.shape[1]])
        off += p.shape[1]
    for n in rs_names:
        grads[n] = jnp.stack([g_.reshape(W[n].shape[1:]) for g_ in got[n]], axis=0)

    for n in rs_names:
        deltas[n], new_m[n], new_v[n] = _adamw("adamw_" + n, W[n], grads[n], M1[n], V2[n])
    names = [n for n, _ in WEIGHTS]
    return (loss, grad_x, *[grads[n] for n in names], *[deltas[n] for n in names], *[new_m[n] for n in names],
            *[new_v[n] for n in names])
```

```python
import functools
import math

import jax
import jax.numpy as jnp
from jax import lax
from jax.experimental import pallas as pl
from jax.experimental.pallas import tpu as pltpu

F32 = jnp.float32
MM = jnp.bfloat16
EPS = 1e-6
NEG = -1e30
N_DEV = 8
VMEM_LIMIT = 48 * 1024 * 1024
MESH = pl.DeviceIdType.MESH

D_MODEL = 1024
N_HEADS = 16
HEAD_PAD = 128
QK_NOPE, QK_ROPE, V_HEAD = 64, 32, 64
Q_LORA, KV_LORA = 384, 256
CHUNK = 64
CONV_W = 31
POOL_WINDOWS = (2, 4, 8, 16)
ROPE_THETA = 10000.0
ATT_SCALE = 1.0 / math.sqrt(QK_NOPE + QK_ROPE)

ADAM_LR, ADAM_B1, ADAM_B2, ADAM_EPS, ADAM_WD, ADAM_STEP = 0.001, 0.9, 0.999, 1e-08, 0.01, 10

WEIGHTS = [('ada_w', 2), ('ada_b', None), ('norm_g', 2), ('mla_w_dq', 1), ('mla_q_norm_g', 1), ('mla_w_uq', 2),
           ('mla_w_dkv', 1), ('mla_kv_norm_g', 1), ('mla_w_ukv', 2), ('mla_w_o', 1), ('conv_w_pw1', 2),
           ('conv_b_pw1', None), ('conv_w_dw', 2), ('conv_b_dw', None), ('conv_ln_g', None), ('conv_ln_b', None),
           ('conv_w_pw2', 1), ('conv_b_pw2', None), ('pool_w', 2), ('pool_b', 2), ('pool_scale', 1),
           ('ffn_w1', 2), ('ffn_w2', 1)]
SHARD_AXIS = dict(WEIGHTS)
BIG = ['mla_w_dq', 'mla_w_uq', 'mla_w_dkv', 'mla_w_ukv', 'mla_w_o', 'conv_w_pw1', 'conv_w_pw2', 'pool_w', 'ffn_w1', 'ffn_w2']
SMALL = ['norm_g', 'mla_q_norm_g', 'mla_kv_norm_g', 'conv_w_dw', 'pool_b', 'pool_scale']
REPL = ['conv_b_pw1', 'conv_b_dw', 'conv_ln_g', 'conv_ln_b', 'conv_b_pw2']


def _dot(a, b):
    return jnp.dot(a.astype(MM), b.astype(MM), preferred_element_type=F32)


def _dot_nt(a, b):
    return lax.dot_general(a.astype(MM), b.astype(MM), (((1,), (1,)), ((), ())), preferred_element_type=F32)


def _dot_tn(a, b):
    return lax.dot_general(a.astype(MM), b.astype(MM), (((0,), (0,)), ((), ())), preferred_element_type=F32)


def _sigmoid(x):
    return 1.0 / (1.0 + jnp.exp(-x))


def _rstd(x):
    return lax.rsqrt(jnp.mean(x * x, axis=-1, keepdims=True) + EPS)


def _rms(x, g):
    return x * _rstd(x) * g


def _rms_bwd(x, g, dout):
    r = _rstd(x)
    xn = x * r
    dg = jnp.sum(dout * xn, axis=0, keepdims=True)
    dxn = dout * g
    dx = r * (dxn - xn * jnp.mean(dxn * xn, axis=-1, keepdims=True))
    return dx, dg


def _prenorm_bwd(x, g0, sc, dh):
    r = _rstd(x)
    xn = x * r
    dsh = jnp.sum(dh, axis=0, keepdims=True)
    dsc = jnp.sum(dh * (xn * g0), axis=0, keepdims=True)
    dn = dh * (1.0 + sc)
    dg0 = jnp.sum(dn * xn, axis=0, keepdims=True)
    dxn = dn * g0
    dx = r * (dxn - xn * jnp.mean(dxn * xn, axis=-1, keepdims=True))
    return dx, dsh, dsc, dg0


def _cparams(sem, vmem=VMEM_LIMIT):
    return pltpu.CompilerParams(dimension_semantics=sem, vmem_limit_bytes=vmem)


def _rows(name, body, n_rows, tm, rows, consts, outs, accs=(), scratch=()):
    tm = min(tm, n_rows)
    nblk = n_rows // tm
    nr, nc, no, na = len(rows), len(consts), len(outs), len(accs)
    in_specs, args = [], []
    for a, kind in rows:
        if kind == 'cur':
            im = lambda i: (i, 0)
        elif kind == 'prev':
            im = lambda i: (jnp.maximum(i - 1, 0), 0)
        else:
            im = lambda i: (jnp.minimum(i + 1, nblk - 1), 0)
        in_specs.append(pl.BlockSpec((tm, a.shape[1]), im))
        args.append(a)
    for a in consts:
        in_specs.append(pl.BlockSpec(a.shape, lambda i, nd=a.ndim: (0,) * nd))
        args.append(a)
    out_specs = [pl.BlockSpec((tm, c), lambda i: (i, 0)) for c, _ in outs]
    out_specs += [pl.BlockSpec(s, lambda i, nd=len(s): (0,) * nd) for s in accs]
    out_shape = [jax.ShapeDtypeStruct((n_rows, c), dt) for c, dt in outs]
    out_shape += [jax.ShapeDtypeStruct(s, F32) for s in accs]

    def kern(*refs):
        i = pl.program_id(0)
        rr = refs[:nr]
        cc = refs[nr:nr + nc]
        oo = refs[nr + nc:nr + nc + no]
        aa = refs[nr + nc + no:nr + nc + no + na]
        ss = refs[nr + nc + no + na:]

        @pl.when(i == 0)
        def _():
            for a in aa:
                a[...] = jnp.zeros(a.shape, F32)

        body(i, nblk, rr, cc, oo, aa, ss)

    return pl.pallas_call(kern, grid=(nblk,), in_specs=in_specs, out_specs=out_specs, out_shape=out_shape,
                          scratch_shapes=list(scratch), name=name, compiler_params=_cparams(("arbitrary",)))(*args)


def _place():
    return lax.axis_index("x"), lax.axis_index("y"), lax.axis_index("c")


def _ag_small(name, xs):
    R, C = xs.shape

    def body(x_ref, out_ref, send_sems, recv_sems):
        x, y, c = _place()
        me = 4 * x + 2 * y + c
        out_ref[me] = x_ref[...]
        copies = []
        for k in range(1, N_DEV):
            peer = ((1 - x) if k & 4 else x, (1 - y) if k & 2 else y, (1 - c) if k & 1 else c)
            cp = pltpu.make_async_remote_copy(src_ref=x_ref, dst_ref=out_ref.at[me], send_sem=send_sems.at[k - 1],
                                              recv_sem=recv_sems.at[k - 1], device_id=peer, device_id_type=MESH)
            cp.start()
            copies.append(cp)
        for cp in copies:
            cp.wait()

    return pl.pallas_call(
        body, out_shape=jax.ShapeDtypeStruct((N_DEV, R, C), xs.dtype),
        in_specs=[pl.BlockSpec(memory_space=pltpu.VMEM)], out_specs=pl.BlockSpec(memory_space=pltpu.VMEM),
        scratch_shapes=[pltpu.SemaphoreType.DMA((N_DEV - 1,)), pltpu.SemaphoreType.DMA((N_DEV - 1,))], name=name)(xs)


def _ag_big(name, xs):
    nt = len(xs)

    def body(*refs):
        x_refs, out_refs = refs[:nt], refs[nt:2 * nt]
        send_sems, recv_sems, local_sems = refs[2 * nt:]
        x, y, c = _place()
        me, sibling = (x, y, c), (x, y, 1 - c)
        chips = [(1 - x, y), (x, 1 - y), (1 - x, 1 - y)]

        def copy(t, k, block, to, own=False):
            px, py, pc = block
            rows = out_refs[t].at[4 * px + 2 * py + pc]
            return pltpu.make_async_remote_copy(src_ref=x_refs[t] if own else rows, dst_ref=rows, send_sem=send_sems.at[7 * t + k],
                                                recv_sem=recv_sems.at[7 * t + k], device_id=to, device_id_type=MESH)

        mine = [pltpu.make_async_copy(x_refs[t], out_refs[t].at[4 * x + 2 * y + c], local_sems.at[t]) for t in range(nt)]
        for cp in mine:
            cp.start()
        first = []
        for t in range(nt):
            first.append(copy(t, 0, me, sibling, own=True))
            first += [copy(t, 1 + j, me, (*chip, c), own=True) for j, chip in enumerate(chips)]
        for cp in first:
            cp.start()
        passed = []
        for t in range(nt):
            for j, chip in enumerate(chips):
                copy(t, 1 + j, (*chip, c), me).wait_recv()
                cp = copy(t, 4 + j, (*chip, c), sibling)
                cp.start()
                passed.append(cp)
        for t in range(nt):
            copy(t, 0, sibling, me).wait_recv()
            for j, chip in enumerate(chips):
                copy(t, 4 + j, (*chip, 1 - c), me).wait_recv()
        for cp in first + passed:
            cp.wait_send()
        for cp in mine:
            cp.wait()

    hbm = pl.BlockSpec(memory_space=pl.ANY)
    return pl.pallas_call(
        body, out_shape=[jax.ShapeDtypeStruct((N_DEV,) + t.shape, t.dtype) for t in xs],
        in_specs=[hbm] * nt, out_specs=[hbm] * nt,
        scratch_shapes=[pltpu.SemaphoreType.DMA((7 * nt,)), pltpu.SemaphoreType.DMA((7 * nt,)), pltpu.SemaphoreType.DMA((nt,))],
        name=name)(*xs)


def _rs_pair(name, ps):
    nt = len(ps)

    def body(*refs):
        p_refs, recv_refs = refs[:nt], refs[nt:2 * nt]
        send_sems, recv_sems = refs[2 * nt:]
        x, y, c = _place()
        copies = []
        for t in range(nt):
            for j in range(4):
                cp = pltpu.make_async_remote_copy(src_ref=p_refs[t].at[j, 1 - c], dst_ref=recv_refs[t].at[j], send_sem=send_sems.at[4 * t + j],
                                                  recv_sem=recv_sems.at[4 * t + j], device_id=(x, y, 1 - c), device_id_type=MESH)
                cp.start()
                copies.append(cp)
        for cp in copies:
            cp.wait()

    hbm = pl.BlockSpec(memory_space=pl.ANY)
    return pl.pallas_call(
        body, out_shape=[jax.ShapeDtypeStruct((4,) + p.shape[2:], p.dtype) for p in ps], in_specs=[hbm] * nt, out_specs=[hbm] * nt,
        scratch_shapes=[pltpu.SemaphoreType.DMA((4 * nt,)), pltpu.SemaphoreType.DMA((4 * nt,))], name=name)(*ps)


RS_ROWS = 256


def _row_block(r):
    return next(t for t in range(RS_ROWS, 0, -16) if r % t == 0)


def _pair_sum(name, p, recv, my_c, my_chip):
    _, _, r, c = p.shape
    tr = _row_block(r)

    def body(sc_ref, p_ref, r_ref, o_ref, own_ref):
        s = p_ref[...] + r_ref[...]
        o_ref[...] = s.astype(MM)

        @pl.when(pl.program_id(1) == sc_ref[1])
        def _():
            own_ref[...] = s

    return pl.pallas_call(
        body, grid_spec=pltpu.PrefetchScalarGridSpec(
            num_scalar_prefetch=1, grid=(r // tr, 4),
            in_specs=[pl.BlockSpec((None, None, tr, c), lambda i, j, sc: (j, sc[0], i, 0)),
                      pl.BlockSpec((None, tr, c), lambda i, j, sc: (j, i, 0))],
            out_specs=[pl.BlockSpec((None, tr, c), lambda i, j, sc: (j, i, 0)), pl.BlockSpec((tr, c), lambda i, j, sc: (i, 0))]),
        out_shape=[jax.ShapeDtypeStruct((4, r, c), MM), jax.ShapeDtypeStruct((r, c), F32)], name=name,
        compiler_params=_cparams(("arbitrary", "arbitrary")))(jnp.stack([my_c, my_chip]), p, recv)


def _chip_sum(name, own, recv, my_chip):
    _, r, c = recv.shape
    tr = _row_block(r)

    def body(sc_ref, own_ref, r_ref, o_ref):
        acc = jnp.zeros((tr, c), F32)
        for j in range(4):
            acc = acc + jnp.where(sc_ref[0] == j, own_ref[...], r_ref[j].astype(F32))
        o_ref[...] = acc

    return pl.pallas_call(
        body, grid_spec=pltpu.PrefetchScalarGridSpec(
            num_scalar_prefetch=1, grid=(r // tr,),
            in_specs=[pl.BlockSpec((tr, c), lambda i, sc: (i, 0)), pl.BlockSpec((4, tr, c), lambda i, sc: (0, i, 0))],
            out_specs=pl.BlockSpec((tr, c), lambda i, sc: (i, 0))),
        out_shape=jax.ShapeDtypeStruct((r, c), F32), name=name,
        compiler_params=_cparams(("arbitrary",)))(my_chip.reshape(1), own, recv)


HBM_SPEC = pl.BlockSpec(memory_space=pltpu.HBM)
SEM_SPEC = pl.BlockSpec(memory_space=pltpu.SEMAPHORE)
SPLIT_EFFECT = pltpu.SideEffectType.DATAFLOW_SIDE_EFFECTING
ALL_PEERS = (1, 2, 3, 4, 5, 6, 7)
FIRST_LEVEL_PEERS = (1, 4, 2, 6)
CHIP_PEERS = (4, 2, 6)


def _split_copies(src_refs, land_refs, sems, masks, src_per_peer):
    n, nt = len(masks), len(src_refs)
    x, y, c = _place()
    by_chip = src_per_peer == 'chip'
    slot = 2 * x + y if by_chip else 4 * x + 2 * y + c
    copies = []
    for t in range(nt):
        for k, mask in enumerate(masks):
            px, py, pc = (1 - x) if mask & 4 else x, (1 - y) if mask & 2 else y, (1 - c) if mask & 1 else c
            src = src_refs[t].at[2 * px + py if by_chip else 4 * px + 2 * py + pc] if src_per_peer else src_refs[t]
            copies.append(pltpu.make_async_remote_copy(src_ref=src, dst_ref=land_refs[t].at[slot], send_sem=sems[t * n + k],
                                                       recv_sem=sems[nt * n + t * n + k], device_id=(px, py, pc), device_id_type=MESH))
    return copies


def _copies_start(name, srcs, lands, masks, src_per_peer):
    nt, ns = len(srcs), 2 * len(masks) * len(srcs)

    def body(*refs):
        for cp in _split_copies(refs[:nt], refs[nt:2 * nt], refs[2 * nt:2 * nt + ns], masks, src_per_peer):
            cp.start()
        token = refs[-1]
        token[...] = jnp.zeros(token.shape, F32)

    outs = pl.pallas_call(
        body, name=name,
        out_shape=(pltpu.SemaphoreType.DMA(()),) * ns + tuple(pltpu.HBM(a.shape, a.dtype) for a in list(srcs) + list(lands))
        + (jax.ShapeDtypeStruct((8, 128), F32),),
        in_specs=(HBM_SPEC,) * (2 * nt), out_specs=(SEM_SPEC,) * ns + (HBM_SPEC,) * (2 * nt) + (pl.BlockSpec(memory_space=pltpu.VMEM),),
        input_output_aliases={t: ns + t for t in range(2 * nt)}, compiler_params=pltpu.CompilerParams(has_side_effects=SPLIT_EFFECT))(
            *[pltpu.with_memory_space_constraint(a, pltpu.HBM) for a in list(srcs) + list(lands)])
    return outs[:ns], outs[ns:ns + nt], outs[ns + nt:ns + 2 * nt], outs[-1]


def _copies_wait(name, sems, srcs_thru, lands_thru, after, masks, src_per_peer):
    nt, ns = len(srcs_thru), len(sems)

    def body(*refs):
        for cp in _split_copies(refs[:nt], refs[nt:2 * nt], refs[2 * nt:2 * nt + ns], masks, src_per_peer):
            cp.wait_send()
            cp.wait_recv()

    thru = list(srcs_thru) + list(lands_thru)
    return pl.pallas_call(
        body, name=name, out_shape=tuple(pltpu.HBM(a.shape, a.dtype) for a in thru),
        in_specs=(HBM_SPEC,) * (2 * nt) + (SEM_SPEC,) * ns + (pl.BlockSpec(memory_space=pl.ANY),), out_specs=(HBM_SPEC,) * (2 * nt),
        input_output_aliases={t: t for t in range(2 * nt)}, compiler_params=pltpu.CompilerParams(has_side_effects=SPLIT_EFFECT))(
            *thru, *sems, after)[nt:]


def _ag_forward(name, gs):
    nt = len(gs)

    def body(*refs):
        o_refs, send_sems, recv_sems = refs[nt:2 * nt], refs[2 * nt], refs[2 * nt + 1]
        x, y, c = _place()
        chips = [(1 - x, y), (x, 1 - y), (1 - x, 1 - y)]

        def copy(t, j, pc):
            rows = o_refs[t].at[4 * chips[j][0] + 2 * chips[j][1] + pc]
            return pltpu.make_async_remote_copy(src_ref=rows, dst_ref=rows, send_sem=send_sems.at[3 * t + j], recv_sem=recv_sems.at[3 * t + j],
                                                device_id=(x, y, 1 - c), device_id_type=MESH)

        for t in range(nt):
            for j in range(3):
                copy(t, j, c).start()
        for t in range(nt):
            for j in range(3):
                copy(t, j, c).wait_send()
                copy(t, j, 1 - c).wait_recv()

    hbm = pl.BlockSpec(memory_space=pl.ANY)
    return pl.pallas_call(body, out_shape=[jax.ShapeDtypeStruct(g.shape, g.dtype) for g in gs], in_specs=[hbm] * nt, out_specs=[hbm] * nt,
                          scratch_shapes=[pltpu.SemaphoreType.DMA((3 * nt,)), pltpu.SemaphoreType.DMA((3 * nt,))],
                          input_output_aliases={t: t for t in range(nt)}, name=name)(*gs)


def _mm_tn_wire(name, a, b, me, sqrelu, shard_rows):
    T, M = a.shape
    N = b.shape[1]
    tk = min(2048, T)
    nk = T // tk
    if shard_rows:
        bm, bn = M // N_DEV, N
        a_spec = pl.BlockSpec((tk, 2 * bm), lambda j, k, m: (k, j))
        b_spec = pl.BlockSpec((tk, bn), lambda j, k, m: (k, 0))
        halves = (slice(0, bm), slice(None)), (slice(bm, 2 * bm), slice(None))
        acc_shape = (2 * bm, bn)
    else:
        bm, bn = M, N // N_DEV
        a_spec = pl.BlockSpec((tk, bm), lambda j, k, m: (k, 0))
        b_spec = pl.BlockSpec((tk, 2 * bn), lambda j, k, m: (k, j))
        halves = (slice(None), slice(0, bn)), (slice(None), slice(bn, 2 * bn))
        acc_shape = (bm, 2 * bn)

    def body(me_ref, a_ref, b_ref, wire_ref, own_ref, acc):
        j, k = pl.program_id(0), pl.program_id(1)

        @pl.when(k == 0)
        def _():
            acc[...] = jnp.zeros(acc.shape, F32)

        av = a_ref[...]
        if sqrelu:
            r = jnp.maximum(av, 0.0)
            av = r * r
        acc[...] += _dot_tn(av, b_ref[...])

        for hh in range(2):
            @pl.when(k == nk - 1)
            def _():
                wire_ref[hh] = acc[halves[hh]].astype(MM)

            @pl.when((k == nk - 1) & (2 * j + hh == me_ref[0]))
            def _():
                own_ref[...] = acc[halves[hh]]

    return pl.pallas_call(
        body, grid_spec=pltpu.PrefetchScalarGridSpec(
            num_scalar_prefetch=1, grid=(N_DEV // 2, nk), in_specs=[a_spec, b_spec],
            out_specs=[pl.BlockSpec((2, bm, bn), lambda j, k, m: (j, 0, 0)), pl.BlockSpec((bm, bn), lambda j, k, m: (0, 0))],
            scratch_shapes=[pltpu.VMEM(acc_shape, F32)]),
        out_shape=[jax.ShapeDtypeStruct((N_DEV, bm, bn), MM), jax.ShapeDtypeStruct((bm, bn), F32)], name=name,
        compiler_params=_cparams(("arbitrary", "arbitrary")))(me.reshape(1), a, b)


def _rs_final(name, own, recv, me, stack, li):
    _, r, c = recv.shape
    tr = RS_ROWS

    def body(me_ref, own_ref, r_ref, s_ref, o_ref):
        acc = jnp.zeros((tr, c), F32)
        for j in range(N_DEV):
            acc = acc + jnp.where(me_ref[0] == j, own_ref[...], r_ref[j].astype(F32))
        o_ref[...] = acc

    return pl.pallas_call(
        body, grid_spec=pltpu.PrefetchScalarGridSpec(
            num_scalar_prefetch=1, grid=(r // tr,),
            in_specs=[pl.BlockSpec((tr, c), lambda i, m: (i, 0)), pl.BlockSpec((N_DEV, tr, c), lambda i, m: (0, i, 0)),
                      pl.BlockSpec(memory_space=pl.ANY)],
            out_specs=pl.BlockSpec((None, tr, c), lambda i, m: (li, i, 0))),
        out_shape=jax.ShapeDtypeStruct(stack.shape, F32), input_output_aliases={3: 0}, name=name,
        compiler_params=_cparams(("arbitrary",)))(me.reshape(1), own, recv, stack)


def _rs_finish(pending, after, me, stacks):
    i, sems, wires_thru, lands, owns = pending
    recvs = _copies_wait(f"rs_wait{i}", sems, wires_thru, lands, after, ALL_PEERS, True)
    return [_rs_final(f"rs_final{i}_{t}", owns[t], recvs[t], me, stacks[t], i) for t in range(len(owns))]


def _mod_part(c16, ada_w, ada_b_cols):
    L, D, n = ada_w.shape

    def body(c_ref, w_ref, b_ref, o_ref):
        cv = c_ref[...]
        o_ref[...] = _dot(cv * _sigmoid(cv), w_ref[...]) + b_ref[...]

    return pl.pallas_call(
        body, grid=(L,), in_specs=[pl.BlockSpec((16, D), lambda i: (0, 0)), pl.BlockSpec((None, D, n), lambda i: (i, 0, 0)),
                                   pl.BlockSpec((None, 1, n), lambda i: (i, 0, 0))],
        out_specs=pl.BlockSpec((None, 16, n), lambda i: (i, 0, 0)), out_shape=jax.ShapeDtypeStruct((L, 16, n), F32),
        name="ada_mod", compiler_params=_cparams(("arbitrary",)))(c16, ada_w, ada_b_cols)


def _ada_w_grad(c16, dmod16):
    L, _, n = dmod16.shape
    D = c16.shape[1]

    def body(c_ref, d_ref, o_ref):
        cv = c_ref[...]
        o_ref[...] = _dot_tn(cv * _sigmoid(cv), d_ref[...])

    return pl.pallas_call(
        body, grid=(L,), in_specs=[pl.BlockSpec((16, D), lambda i: (0, 0)), pl.BlockSpec((None, 16, n), lambda i: (i, 0, 0))],
        out_specs=pl.BlockSpec((None, D, n), lambda i: (i, 0, 0)), out_shape=jax.ShapeDtypeStruct((L, D, n), F32),
        name="ada_w_grad", compiler_params=_cparams(("arbitrary",)))(c16, dmod16)


def _sum_devices(name, g):
    _, R, C = g.shape

    def body(g_ref, o_ref):
        acc = g_ref[0]
        for d in range(1, N_DEV):
            acc = acc + g_ref[d]
        o_ref[...] = acc

    return pl.pallas_call(body, out_shape=jax.ShapeDtypeStruct((R, C), F32), name=name)(g)


def _prenorm(name, x, g0, sc, sh, dtype):
    T, D = x.shape

    def body(i, n, rr, cc, oo, aa, ss):
        oo[0][...] = (_rms(rr[0][...], cc[0][...]) * (1.0 + cc[1][...]) + cc[2][...]).astype(dtype)

    return _rows(name, body, T, 512, [(x, 'cur')], [g0, sc, sh], [(D, dtype)])[0]


def _post_bwd_math(d, yv, g1v, gtv):
    dgt = jnp.sum(d * _rms(yv, g1v), axis=0, keepdims=True)
    dy, dg1 = _rms_bwd(yv, g1v, d * gtv)
    return dy, dg1, dgt


def _post_bwd_nt(name, dxo, y, g1, gt, w):
    T, D = y.shape
    K = w.shape[0]

    def body(i, n, rr, cc, oo, aa, ss):
        dy, dg1, dgt = _post_bwd_math(rr[0][...], rr[1][...], cc[0][...], cc[1][...])
        aa[0][...] += dg1
        aa[1][...] += dgt
        dy = dy.astype(MM)
        oo[0][...] = dy
        oo[1][...] = _dot_nt(dy, cc[2][...]).astype(MM)

    return _rows(name, body, T, 512, [(dxo, 'cur'), (y, 'cur')], [g1, gt, w], [(D, MM), (K, MM)], accs=[(1, D)] * 2)


def _mm_post(name, a, w, bias, x, g1, gt):
    T, D = x.shape
    consts = [w, g1, gt] + ([bias] if bias is not None else [])

    def body(i, n, rr, cc, oo, aa, ss):
        y = _dot(rr[0][...], cc[0][...])
        if bias is not None:
            y = y + cc[3][...]
        oo[0][...] = y
        oo[1][...] = rr[1][...] + cc[2][...] * _rms(y, cc[1][...])

    return _rows(name, body, T, 512, [(a, 'cur'), (x, 'cur')], consts, [(D, F32), (D, F32)])


def _mm_tn(name, a, b, sqrelu=False, col_shards=0, diag=0):
    T, M = a.shape
    N = b.shape[1]
    tk = min(512, T)
    nk = T // tk
    if diag:
        bm, bn = M // diag, N // diag
        grid = (diag, 1, nk)
        a_spec = pl.BlockSpec((tk, bm), lambda g, n, k: (k, g))
        b_spec = pl.BlockSpec((tk, bn), lambda g, n, k: (k, g))
        o_spec = pl.BlockSpec((None, bm, bn), lambda g, n, k: (g, 0, 0))
        o_shape = (diag, bm, bn)
    else:
        bm = min(M, 1024)
        bn = N // col_shards if col_shards else min(N, 1024)
        grid = (M // bm, N // bn, nk)
        a_spec = pl.BlockSpec((tk, bm), lambda m, n, k: (k, m))
        b_spec = pl.BlockSpec((tk, bn), lambda m, n, k: (k, n))
        if col_shards:
            o_spec = pl.BlockSpec((None, bm, bn), lambda m, n, k: (n, m, 0))
            o_shape = (col_shards, M, bn)
        else:
            o_spec = pl.BlockSpec((bm, bn), lambda m, n, k: (m, n))
            o_shape = (M, N)

    def body(a_ref, b_ref, o_ref):
        @pl.when(pl.program_id(2) == 0)
        def _():
            o_ref[...] = jnp.zeros(o_ref.shape, F32)

        av = a_ref[...]
        if sqrelu:
            r = jnp.maximum(av, 0.0)
            av = r * r
        o_ref[...] += _dot_tn(av, b_ref[...])

    return pl.pallas_call(body, grid=grid, in_specs=[a_spec, b_spec], out_specs=o_spec,
                          out_shape=jax.ShapeDtypeStruct(o_shape, F32), name=name,
                          compiler_params=_cparams(("arbitrary", "arbitrary", "arbitrary")))(a, b)


FFN_SHARDS = 4
FFN_BWD_SHARDS = 4
FFN_BWD_VMEM = 56 * 1024 * 1024

def _ffn_fwd(name, li, x, g0, sc, sh, w1g, w2g, g1, gt):
    T, D = x.shape
    nf, tf = w1g.shape[0], w1g.shape[-1]
    F = nf * tf
    tm = min(512, T)

    def body(x_ref, g0_ref, sc_ref, sh_ref, w1_ref, w2_ref, g1_ref, gt_ref, h_ref, a_ref, y_ref, xo_ref, acc):
        f = pl.program_id(1)

        @pl.when(f == 0)
        def _():
            acc[...] = jnp.zeros(acc.shape, F32)
            h_ref[...] = (_rms(x_ref[...], g0_ref[...]) * (1.0 + sc_ref[...]) + sh_ref[...]).astype(MM)

        hv = h_ref[...]
        part = None
        for hh in range(FFN_SHARDS):
            a = _dot(hv, w1_ref[hh])
            a_ref[:, hh * tf:(hh + 1) * tf] = a.astype(MM)
            r = jnp.maximum(a, 0.0)
            p = _dot(r * r, w2_ref[hh])
            part = p if part is None else part + p
        acc[...] += part

        @pl.when(f == nf // FFN_SHARDS - 1)
        def _():
            y = acc[...]
            y_ref[...] = y
            xo_ref[...] = x_ref[...] + gt_ref[...] * _rms(y, g1_ref[...])

    row = lambda t, f: (t, 0)
    one = lambda t, f: (0, 0)
    return pl.pallas_call(
        body, grid=(T // tm, nf // FFN_SHARDS),
        in_specs=[pl.BlockSpec((tm, D), row)] + [pl.BlockSpec((1, D), one)] * 3
        + [pl.BlockSpec((FFN_SHARDS, None, D, tf), lambda t, f: (f, li, 0, 0)), pl.BlockSpec((FFN_SHARDS, None, tf, D), lambda t, f: (f, li, 0, 0)),
           pl.BlockSpec((1, D), one), pl.BlockSpec((1, D), one)],
        out_specs=[pl.BlockSpec((tm, D), row), pl.BlockSpec((tm, FFN_SHARDS * tf), lambda t, f: (t, f)), pl.BlockSpec((tm, D), row),
                   pl.BlockSpec((tm, D), row)],
        out_shape=[jax.ShapeDtypeStruct((T, D), MM), jax.ShapeDtypeStruct((T, F), MM), jax.ShapeDtypeStruct((T, D), F32),
                   jax.ShapeDtypeStruct((T, D), F32)],
        scratch_shapes=[pltpu.VMEM((tm, D), F32)], name=name,
        compiler_params=_cparams(("arbitrary", "arbitrary")))(x, g0, sc, sh, w1g, w2g, g1, gt)


def _ffn_bwd(name, li, y, g1, gt, a, w1g, w2g, x, dxo, g0, sc):
    T, D = x.shape
    nf, tf = w1g.shape[0], w1g.shape[-1]
    F = nf * tf
    tm = min(512, T)
    ns = FFN_BWD_SHARDS

    def body(y_ref, g1_ref, gt_ref, a_ref, w1_ref, w2_ref, x_ref, dxo_ref, g0_ref, sc_ref,
             dy_ref, da_ref, dx_ref, dg1_ref, dgt_ref, dsh_ref, dsc_ref, dg0_ref, acc):
        t, f = pl.program_id(0), pl.program_id(1)

        @pl.when((t == 0) & (f == 0))
        def _():
            for r in (dg1_ref, dgt_ref, dsh_ref, dsc_ref, dg0_ref):
                r[...] = jnp.zeros(r.shape, F32)

        @pl.when(f == 0)
        def _():
            acc[...] = jnp.zeros(acc.shape, F32)
            d, yv, g1v = dxo_ref[...], y_ref[...], g1_ref[...]
            dgt_ref[...] += jnp.sum(d * _rms(yv, g1v), axis=0, keepdims=True)
            dyf, dg1 = _rms_bwd(yv, g1v, d * gt_ref[...])
            dg1_ref[...] += dg1
            dy_ref[...] = dyf.astype(MM)

        dyv = dy_ref[...]
        dyv = dyv + dyv
        part = None
        for hh in range(ns):
            cols = slice(hh * tf, (hh + 1) * tf)
            du = _dot_nt(dyv, w2_ref[hh])
            da = (du * jnp.maximum(a_ref[:, cols], 0.0).astype(F32)).astype(MM)
            da_ref[:, cols] = da
            p = _dot_nt(da, w1_ref[hh])
            part = p if part is None else part + p
        acc[...] += part

        @pl.when(f == nf // ns - 1)
        def _():
            dx, dsh, dsc, dg0 = _prenorm_bwd(x_ref[...], g0_ref[...], sc_ref[...], acc[...])
            dx_ref[...] = dxo_ref[...] + dx
            dsh_ref[...] += dsh
            dsc_ref[...] += dsc
            dg0_ref[...] += dg0

    row = lambda t, f: (t, 0)
    one = lambda t, f: (0, 0)
    blk = lambda t, f: (t, f)
    return pl.pallas_call(
        body, grid=(T // tm, nf // ns),
        in_specs=[pl.BlockSpec((tm, D), row), pl.BlockSpec((1, D), one), pl.BlockSpec((1, D), one), pl.BlockSpec((tm, ns * tf), blk),
                  pl.BlockSpec((ns, None, D, tf), lambda t, f: (f, li, 0, 0)),
                  pl.BlockSpec((ns, None, tf, D), lambda t, f: (f, li, 0, 0)), pl.BlockSpec((tm, D), row), pl.BlockSpec((tm, D), row),
                  pl.BlockSpec((1, D), one), pl.BlockSpec((1, D), one)],
        out_specs=[pl.BlockSpec((tm, D), row), pl.BlockSpec((tm, ns * tf), blk), pl.BlockSpec((tm, D), row)] + [pl.BlockSpec((1, D), one)] * 5,
        out_shape=[jax.ShapeDtypeStruct((T, D), MM), jax.ShapeDtypeStruct((T, F), MM), jax.ShapeDtypeStruct((T, D), F32)]
        + [jax.ShapeDtypeStruct((1, D), F32)] * 5,
        scratch_shapes=[pltpu.VMEM((tm, D), F32)], name=name,
        compiler_params=_cparams(("arbitrary", "arbitrary"), FFN_BWD_VMEM))(y, g1, gt, a, w1g, w2g, x, dxo, g0, sc)


def _rope_tables(pos, invf):
    T = pos.shape[0]

    def body(i, n, rr, cc, oo, aa, ss):
        ang = rr[0][...] * cc[0][...]
        lane = lax.broadcasted_iota(jnp.int32, ang.shape, 1)
        cs, sn = jnp.cos(ang), jnp.sin(ang)
        oo[0][...] = jnp.where((lane >= QK_NOPE) & (lane < QK_NOPE + QK_ROPE), cs, 1.0)
        oo[1][...] = jnp.where((lane >= QK_NOPE) & (lane < QK_NOPE + QK_ROPE // 2), -sn, 0.0)
        oo[2][...] = jnp.where((lane >= QK_NOPE + QK_ROPE // 2) & (lane < QK_NOPE + QK_ROPE), sn, 0.0)

    return _rows("rope_tables", body, T, 512, [(pos, 'cur')], [invf], [(HEAD_PAD, F32)] * 3)


def _rope(v, C, S1, S2):
    n = v.shape[1]
    reps = n // HEAD_PAD
    if reps > 1:
        C, S1, S2 = (jnp.tile(t, (1, reps)) for t in (C, S1, S2))
    return v * C + pltpu.roll(v, n - QK_ROPE // 2, 1) * S1 + pltpu.roll(v, QK_ROPE // 2, 1) * S2


def _unrope(d, C, S1, S2):
    n = d.shape[1]
    reps = n // HEAD_PAD
    if reps > 1:
        C, S1, S2 = (jnp.tile(t, (1, reps)) for t in (C, S1, S2))
    return d * C + pltpu.roll(d * S1, QK_ROPE // 2, 1) + pltpu.roll(d * S2, n - QK_ROPE // 2, 1)


MLA_ROWS = 512
MLA_BWD_ROWS = 512


def _mla_proj(name, x, g0, sc, sh, C, S1, S2, w_dq, qg, w_uq, w_dkv, kvg, w_ukv_k, w_ukv_v):
    T, D = x.shape
    HP = N_HEADS * HEAD_PAD

    def body(i, n, rr, cc, oo, aa, ss):
        hv = (_rms(rr[0][...], cc[7][...]) * (1.0 + cc[8][...]) + cc[9][...]).astype(MM)
        oo[7][...] = hv
        Cv, S1v, S2v = rr[1][...], rr[2][...], rr[3][...]
        cq_raw = _dot(hv, cc[0][...])
        cq = _rms(cq_raw, cc[1][...]).astype(MM)
        q = _rope(_dot(cq, cc[2][...]), Cv, S1v, S2v)
        ckv_all = _dot(hv, cc[3][...])
        ckv_raw = ckv_all[:, :KV_LORA]
        ckv = _rms(ckv_raw, cc[4][...]).astype(MM)
        kr = _rope(ckv_all[:, KV_LORA:], Cv, S1v, S2v)
        k = _dot(ckv, cc[5][...]) + jnp.tile(kr, (1, N_HEADS))
        v = _dot(ckv, cc[6][...])
        v = jnp.where(lax.broadcasted_iota(jnp.int32, v.shape, 1) % HEAD_PAD == V_HEAD, 1.0, v)
        oo[0][...] = cq_raw
        oo[1][...] = cq
        oo[2][...] = ckv_raw
        oo[3][...] = ckv
        oo[4][...] = q.astype(MM)
        oo[5][...] = k.astype(MM)
        oo[6][...] = v.astype(MM)

    return _rows(name, body, T, MLA_ROWS, [(x, 'cur'), (C, 'cur'), (S1, 'cur'), (S2, 'cur')],
                 [w_dq, qg, w_uq, w_dkv, kvg, w_ukv_k, w_ukv_v, g0, sc, sh],
                 [(Q_LORA, F32), (Q_LORA, MM), (KV_LORA, F32), (KV_LORA, MM), (HP, MM), (HP, MM), (HP, MM), (D, MM)])


ATT_HEADS = 4
ATT_BLOCK = 512
ATT_FWD_BLOCK = 1024


def _chunk_mask_t(tk, tq):
    ki = lax.broadcasted_iota(jnp.int32, (tk, tq), 0) // CHUNK
    qi = lax.broadcasted_iota(jnp.int32, (tk, tq), 1) // CHUNK
    return ki <= qi


def _attn_fwd(name, q, k, v):
    T = q.shape[0]
    tb = min(ATT_FWD_BLOCK, T)
    nb = T // tb
    nh = ATT_HEADS
    hs = [slice(h * HEAD_PAD, (h + 1) * HEAD_PAD) for h in range(nh)]

    def body(q_ref, k_ref, v_ref, o_ref, lse_ref):
        qb = pl.program_id(1)

        def k_block(k0, masked, st):
            new = []
            for h in range(nh):
                m, acc = st[h]
                s = _dot_nt(k_ref[pl.ds(k0, tb), hs[h]], q_ref[:, hs[h]])
                if masked:
                    s = jnp.where(_chunk_mask_t(tb, tb), s, NEG)
                m_new = jnp.maximum(m, jnp.max(s, axis=0, keepdims=True))
                alpha = jnp.exp((m - m_new) * ATT_SCALE)
                p = jnp.exp((s - m_new) * ATT_SCALE)
                acc = alpha * acc + _dot_tn(v_ref[pl.ds(k0, tb), hs[h]], p)
                new.append((m_new, acc))
            return tuple(new)

        st = tuple((jnp.full((1, tb), NEG, F32), jnp.zeros((HEAD_PAD, tb), F32)) for _ in range(nh))
        st = k_block(pl.multiple_of(qb * tb, tb), True, st)
        st = lax.fori_loop(0, qb, lambda kb, s_: k_block(pl.multiple_of(kb * tb, tb), False, s_), st)
        for h in range(nh):
            m, acc = st[h]
            l = acc[V_HEAD:V_HEAD + 1, :]
            o_ref[:, hs[h]] = (acc / l).T.astype(MM)
            lse_ref[h] = jnp.broadcast_to(m * ATT_SCALE + jnp.log(l), (8, tb))

    blk = pl.BlockSpec((tb, nh * HEAD_PAD), lambda g, i: (i, g))
    res = pl.BlockSpec((T, nh * HEAD_PAD), lambda g, i: (0, g))
    return pl.pallas_call(
        body, grid=(N_HEADS // nh, nb), in_specs=[blk, res, res],
        out_specs=[blk, pl.BlockSpec((nh, 8, tb), lambda g, i: (g, 0, i))],
        out_shape=[jax.ShapeDtypeStruct(q.shape, MM), jax.ShapeDtypeStruct((N_HEADS, 8, T), F32)], name=name,
        compiler_params=_cparams(("arbitrary", "arbitrary")))(q, k, v)


def _attn_delta(name, do, o):
    T = do.shape[0]
    tb = min(256, T)

    def body(do_ref, o_ref, d_ref):
        lane = lax.broadcasted_iota(jnp.int32, (tb, HEAD_PAD), 1) // 8
        cols = jnp.zeros((tb, HEAD_PAD), F32)
        for h in range(N_HEADS):
            hsl = slice(h * HEAD_PAD, (h + 1) * HEAD_PAD)
            r = jnp.sum(do_ref[:, hsl].astype(F32) * o_ref[:, hsl].astype(F32), axis=1, keepdims=True)
            cols = jnp.where(lane == h, r, cols)
        d_ref[...] = cols.T

    spec = pl.BlockSpec((tb, N_HEADS * HEAD_PAD), lambda i: (i, 0))
    out = pl.pallas_call(body, grid=(T // tb,), in_specs=[spec, spec], out_specs=pl.BlockSpec((HEAD_PAD, tb), lambda i: (0, i)),
                         out_shape=jax.ShapeDtypeStruct((HEAD_PAD, T), F32), name=name, compiler_params=_cparams(("arbitrary",)))(do, o)
    return out.reshape(N_HEADS, 8, T)


def _attn_bwd(name, q, k, v, do, lse, delta):
    T = q.shape[0]
    tb = min(ATT_BLOCK, T)
    nb = T // tb
    nh = ATT_HEADS
    hs = [slice(h * HEAD_PAD, (h + 1) * HEAD_PAD) for h in range(nh)]

    def body(q_ref, k_ref, v_ref, do_ref, lse_ref, dl_ref, dq_ref, dk_ref, dv_ref, dq_acc, dk_acc, dv_acc):
        kb = pl.program_id(1)

        @pl.when(kb == 0)
        def _():
            dq_acc[...] = jnp.zeros(dq_acc.shape, F32)

        dk_acc[...] = jnp.zeros(dk_acc.shape, F32)
        dv_acc[...] = jnp.zeros(dv_acc.shape, F32)

        def q_block(q0, masked):
            for h in range(nh):
                qh = q_ref[pl.ds(q0, tb), hs[h]]
                doh = do_ref[pl.ds(q0, tb), hs[h]]
                kh = k_ref[:, hs[h]]
                s = _dot_nt(kh, qh) * ATT_SCALE
                if masked:
                    s = jnp.where(_chunk_mask_t(tb, tb), s, NEG)
                p = jnp.exp(s - lse_ref[h, 0:1, pl.ds(q0, tb)])
                ds = (p * (_dot_nt(v_ref[:, hs[h]], doh) - dl_ref[h, 0:1, pl.ds(q0, tb)]) * ATT_SCALE).astype(MM)
                dv_acc[:, hs[h]] += _dot(p, doh)
                dk_acc[:, hs[h]] += _dot(ds, qh)
                dq_acc[pl.ds(q0, tb), hs[h]] += _dot_tn(ds, kh)

        q_block(pl.multiple_of(kb * tb, tb), True)

        def rest(qb, c_):
            q_block(pl.multiple_of(qb * tb, tb), False)
            return c_

        lax.fori_loop(kb + 1, nb, rest, 0)
        dk_ref[...] = dk_acc[...].astype(MM)
        dv_ref[...] = dv_acc[...].astype(MM)

        @pl.when(kb == nb - 1)
        def _():
            dq_ref[...] = dq_acc[...].astype(MM)

    W = nh * HEAD_PAD
    blk = pl.BlockSpec((tb, W), lambda g, i: (i, g))
    res = pl.BlockSpec((T, W), lambda g, i: (0, g))
    rows = pl.BlockSpec((nh, 8, T), lambda g, i: (g, 0, 0))
    return pl.pallas_call(
        body, grid=(N_HEADS // nh, nb), in_specs=[res, blk, blk, res, rows, rows], out_specs=[res, blk, blk],
        out_shape=[jax.ShapeDtypeStruct(q.shape, MM)] * 3,
        scratch_shapes=[pltpu.VMEM((T, W), F32), pltpu.VMEM((tb, W), F32), pltpu.VMEM((tb, W), F32)],
        name=name, compiler_params=_cparams(("arbitrary", "arbitrary")))(q, k, v, do, lse, delta)


def _mla_proj_bwd(name, dq, dk, dv, C, S1, S2, cq_raw, ckv_raw, x, dxo, w_uq, w_ukv_k, w_ukv_v, w_dq, w_dkv, qg, kvg, g0, sc):
    T, D = x.shape
    HP = N_HEADS * HEAD_PAD

    def body(i, n, rr, cc, oo, aa, ss):
        Cv, S1v, S2v = rr[3][...], rr[4][...], rr[5][...]
        dq_pre = _unrope(rr[0][...].astype(F32), Cv, S1v, S2v).astype(MM)
        oo[0][...] = dq_pre
        dcq = _dot_nt(dq_pre, cc[0][...])
        dcq_raw, dqg = _rms_bwd(rr[6][...], cc[5][...], dcq)
        aa[0][...] += dqg
        dcq_raw = dcq_raw.astype(MM)
        oo[1][...] = dcq_raw
        dkv = rr[1][...]
        dkr = dkv[:, :HEAD_PAD].astype(F32)
        for hh in range(1, N_HEADS):
            dkr = dkr + dkv[:, hh * HEAD_PAD:(hh + 1) * HEAD_PAD].astype(F32)
        lane = lax.broadcasted_iota(jnp.int32, dkr.shape, 1)
        dkr = jnp.where((lane >= QK_NOPE) & (lane < QK_NOPE + QK_ROPE), _unrope(dkr, Cv, S1v, S2v), 0.0)
        dckv = _dot_nt(dkv, cc[1][...]) + _dot_nt(rr[2][...], cc[2][...])
        dckv_raw, dkvg = _rms_bwd(rr[7][...], cc[6][...], dckv)
        aa[1][...] += dkvg
        dckv_all = jnp.concatenate([dckv_raw, dkr], axis=1).astype(MM)
        oo[2][...] = dckv_all
        dh = _dot_nt(dcq_raw, cc[3][...]) + _dot_nt(dckv_all, cc[4][...])
        dx, dsh, dsc, dg0 = _prenorm_bwd(rr[8][...], cc[7][...], cc[8][...], dh)
        oo[3][...] = rr[9][...] + dx
        aa[2][...] += dsh
        aa[3][...] += dsc
        aa[4][...] += dg0

    return _rows(name, body, T, MLA_BWD_ROWS,
                 [(dq, 'cur'), (dk, 'cur'), (dv, 'cur'), (C, 'cur'), (S1, 'cur'), (S2, 'cur'), (cq_raw, 'cur'), (ckv_raw, 'cur'),
                  (x, 'cur'), (dxo, 'cur')],
                 [w_uq, w_ukv_k, w_ukv_v, w_dq, w_dkv, qg, kvg, g0, sc],
                 [(HP, MM), (Q_LORA, MM), (KV_LORA + HEAD_PAD, MM), (D, F32)],
                 accs=[(1, Q_LORA), (1, KV_LORA), (1, D), (1, D), (1, D)])


HALO = 32


def _windows(ext, tm, first):
    rolled = {0: ext}
    out = []
    for j in range(CONV_W):
        r = (first + j) % 8
        if r not in rolled:
            rolled[r] = pltpu.roll(ext, ext.shape[0] - r, 0)
        out.append(rolled[r][first + j - r:first + j - r + tm])
    return out


def _conv_glu(name, x, g0, sc, sh, w_pw1, b_pw1):
    T, D = x.shape

    def body(i, n, rr, cc, oo, aa, ss):
        hv = (_rms(rr[0][...], cc[2][...]) * (1.0 + cc[3][...]) + cc[4][...]).astype(MM)
        oo[2][...] = hv
        a = _dot(hv, cc[0][...]) + cc[1][...]
        oo[0][...] = a
        oo[1][...] = a[:, :D] * _sigmoid(a[:, D:])

    return _rows(name, body, T, 512, [(x, 'cur')], [w_pw1, b_pw1, g0, sc, sh], [(2 * D, F32), (D, F32), (D, MM)])


def _layernorm_parts(uc):
    xc = uc - jnp.mean(uc, axis=-1, keepdims=True)
    r = lax.rsqrt(jnp.mean(xc * xc, axis=-1, keepdims=True) + EPS)
    return xc * r, r


def _conv_dw(name, u, w_dw, b_dw, ln_g, ln_b, w_pw2, b_pw2, x, g1, gt):
    T, D = u.shape
    tm = min(256, T)

    def body(i, n, rr, cc, oo, aa, ss):
        ext = jnp.concatenate([jnp.where(i > 0, rr[1][tm - HALO:tm, :], 0.0), rr[0][...]], axis=0)
        uc = jnp.zeros((tm, D), F32) + cc[1][...]
        for kk, win in enumerate(_windows(ext, tm, HALO - (CONV_W - 1))):
            uc = uc + win * cc[0][kk:kk + 1, :]
        xh, _ = _layernorm_parts(uc)
        ln = xh * cc[2][...] + cc[3][...]
        z = (ln * _sigmoid(ln)).astype(MM)
        y = _dot(z, cc[4][...]) + cc[5][...]
        oo[0][...] = uc
        oo[1][...] = z
        oo[2][...] = y
        oo[3][...] = rr[2][...] + cc[7][...] * _rms(y, cc[6][...])

    return _rows(name, body, T, tm, [(u, 'cur'), (u, 'prev'), (x, 'cur')], [w_dw, b_dw, ln_g, ln_b, w_pw2, b_pw2, g1, gt],
                 [(D, F32), (D, MM), (D, F32), (D, F32)])


def _conv_bwd1(name, dxo, y, g1, gt, uc, w_pw2, ln_g, ln_b):
    T, D = uc.shape

    def body(i, n, rr, cc, oo, aa, ss):
        dy, dg1, dgt = _post_bwd_math(rr[0][...], rr[1][...], cc[3][...], cc[4][...])
        aa[3][...] += dg1
        aa[4][...] += dgt
        aa[5][...] += jnp.sum(dy, axis=0, keepdims=True)
        dy = dy.astype(MM)
        oo[1][...] = dy
        dz = _dot_nt(dy, cc[0][...])
        xh, r = _layernorm_parts(rr[2][...])
        g = cc[1][...]
        ln = xh * g + cc[2][...]
        sg = _sigmoid(ln)
        dln = dz * (sg * (1.0 + ln * (1.0 - sg)))
        aa[0][...] += jnp.sum(dln * xh, axis=0, keepdims=True)
        aa[1][...] += jnp.sum(dln, axis=0, keepdims=True)
        dxh = dln * g
        duc = r * (dxh - jnp.mean(dxh, axis=-1, keepdims=True) - xh * jnp.mean(dxh * xh, axis=-1, keepdims=True))
        aa[2][...] += jnp.sum(duc, axis=0, keepdims=True)
        oo[0][...] = duc

    return _rows(name, body, T, 512, [(dxo, 'cur'), (y, 'cur'), (uc, 'cur')], [w_pw2, ln_g, ln_b, g1, gt], [(D, F32), (D, MM)],
                 accs=[(1, D)] * 6)


def _conv_bwd2(name, duc, u, a, x, dxo, w_dw, w_pw1, g0, sc):
    T, D = u.shape
    tm = min(256, T)

    def body(i, n, rr, cc, oo, aa, ss):
        dcur = rr[0][...]
        extd = jnp.concatenate([dcur, jnp.where(i < n - 1, rr[1][0:HALO, :], 0.0)], axis=0)
        extu = jnp.concatenate([jnp.where(i > 0, rr[3][tm - HALO:tm, :], 0.0), rr[2][...]], axis=0)
        wd = _windows(extd, tm, 0)
        wu = _windows(extu, tm, HALO - (CONV_W - 1))
        du = jnp.zeros((tm, D), F32)
        for kk in range(CONV_W):
            du = du + wd[CONV_W - 1 - kk] * cc[0][kk:kk + 1, :]
            aa[0][kk:kk + 1, :] += jnp.sum(dcur * wu[kk], axis=0, keepdims=True)
        av = rr[4][...]
        a1, sg = av[:, :D], _sigmoid(av[:, D:])
        da = jnp.concatenate([du * sg, du * a1 * (sg * (1.0 - sg))], axis=1)
        aa[1][...] += jnp.sum(da, axis=0, keepdims=True)
        da = da.astype(MM)
        oo[0][...] = da
        dx, dsh, dsc, dg0 = _prenorm_bwd(rr[5][...], cc[2][...], cc[3][...], _dot_nt(da, cc[1][...]))
        oo[1][...] = rr[6][...] + dx
        aa[2][...] += dsh
        aa[3][...] += dsc
        aa[4][...] += dg0

    return _rows(name, body, T, tm,
                 [(duc, 'cur'), (duc, 'next'), (u, 'cur'), (u, 'prev'), (a, 'cur'), (x, 'cur'), (dxo, 'cur')],
                 [w_dw, w_pw1, g0, sc], [(2 * D, MM), (D, F32)],
                 accs=[(32, D), (1, 2 * D), (1, D), (1, D), (1, D)])


PHALO = 16


def _pool_fwd(name, h, w, b, scale, x, g1, gt):
    T, D = h.shape
    G = len(POOL_WINDOWS)
    Cg = D // G
    tm = min(256, T)

    def body(i, n, rr, cc, oo, aa, ss):
        ext = ss[0]
        ext[0:PHALO, :] = jnp.where(i > 0, rr[1][tm - PHALO:tm, :], 0.0)
        ext[PHALO:PHALO + tm, :] = rr[0][...]
        t_glob = i * tm + lax.broadcasted_iota(jnp.int32, (tm, 1), 0)
        ps, ys = [], []
        for g, win in enumerate(POOL_WINDOWS):
            cols = slice(g * Cg, (g + 1) * Cg)
            s = ext[pl.ds(PHALO, tm), cols]
            for j in range(1, win):
                s = s + ext[pl.ds(PHALO - j, tm), cols]
            cnt = jnp.minimum(t_glob + 1, win).astype(F32)
            p = (s / cnt - ext[pl.ds(PHALO, tm), cols]).astype(MM)
            ps.append(p)
            ys.append(_dot(p, cc[0][g]) + cc[1][:, cols])
        ypre = jnp.concatenate(ys, axis=1)
        y = ypre * cc[2][...]
        oo[0][...] = jnp.concatenate(ps, axis=1)
        oo[1][...] = ypre
        oo[2][...] = y
        oo[3][...] = rr[2][...] + cc[4][...] * _rms(y, cc[3][...])

    return _rows(name, body, T, tm, [(h, 'cur'), (h, 'prev'), (x, 'cur')], [w, b, scale, g1, gt],
                 [(D, MM), (D, F32), (D, F32), (D, F32)], scratch=[pltpu.VMEM((tm + PHALO, D), F32)])


def _pool_bwd1(name, dxo, y, g1, gt, ypre, scale, w):
    T, D = ypre.shape
    G = len(POOL_WINDOWS)
    Cg = D // G

    def body(i, n, rr, cc, oo, aa, ss):
        dyv, dg1, dgt = _post_bwd_math(rr[0][...], rr[1][...], cc[2][...], cc[3][...])
        aa[2][...] += dg1
        aa[3][...] += dgt
        aa[0][...] += jnp.sum(dyv * rr[2][...], axis=0, keepdims=True)
        dypre = dyv * cc[0][...]
        aa[1][...] += jnp.sum(dypre, axis=0, keepdims=True)
        dypre = dypre.astype(MM)
        oo[1][...] = dypre
        oo[0][...] = jnp.concatenate([_dot_nt(dypre[:, g * Cg:(g + 1) * Cg], cc[1][g]) for g in range(G)], axis=1)

    return _rows(name, body, T, 256, [(dxo, 'cur'), (y, 'cur'), (ypre, 'cur')], [scale, w, g1, gt], [(D, F32), (D, MM)],
                 accs=[(1, D)] * 4)


def _pool_bwd2(name, dp, x, dxo, g0, sc):
    T, D = x.shape
    G = len(POOL_WINDOWS)
    Cg = D // G
    tm = min(256, T)

    def body(i, n, rr, cc, oo, aa, ss):
        ext = ss[0]
        t_glob = i * tm + lax.broadcasted_iota(jnp.int32, (tm, 1), 0)
        dcur = rr[0][...]
        dhs = []
        for g, win in enumerate(POOL_WINDOWS):
            cols = slice(g * Cg, (g + 1) * Cg)
            cnt = jnp.minimum(t_glob + 1, win).astype(F32)
            ext[0:tm, cols] = dcur[:, cols] / cnt
            ext[tm:tm + PHALO, cols] = jnp.where(i < n - 1, rr[1][0:PHALO, cols] * (1.0 / win), 0.0)
        for g, win in enumerate(POOL_WINDOWS):
            cols = slice(g * Cg, (g + 1) * Cg)
            s = ext[pl.ds(0, tm), cols]
            for j in range(1, win):
                s = s + ext[pl.ds(j, tm), cols]
            dhs.append(s - dcur[:, cols])
        dx, dsh, dsc, dg0 = _prenorm_bwd(rr[2][...], cc[0][...], cc[1][...], jnp.concatenate(dhs, axis=1))
        oo[0][...] = rr[3][...] + dx
        aa[0][...] += dsh
        aa[1][...] += dsc
        aa[2][...] += dg0

    return _rows(name, body, T, tm, [(dp, 'cur'), (dp, 'next'), (x, 'cur'), (dxo, 'cur')], [g0, sc], [(D, F32)],
                 accs=[(1, D)] * 3, scratch=[pltpu.VMEM((tm + PHALO, D), F32)])


def _loss_head(x, tgt):
    T, D = x.shape

    def body(i, n, rr, cc, oo, aa, ss):
        err = rr[0][...] - rr[1][...]
        oo[0][...] = err * (1.0 / D)
        aa[0][...] += jnp.sum(err * err, axis=0, keepdims=True)

        @pl.when(i == n - 1)
        def _():
            aa[1][...] = jnp.broadcast_to(jnp.sum(aa[0][...], axis=1, keepdims=True) * (0.5 / D), (1, 128))

    dx, _, loss_row = _rows("loss_head", body, T, 512, [(x, 'cur'), (tgt, 'cur')], [], [(D, F32)], accs=[(1, D), (1, 128)])
    return dx, loss_row


def _adamw(name, w, g, m, v, after=None):
    shape = w.shape
    C = shape[-1]
    R = w.size // C
    w2, g2, m2, v2 = (t.reshape(R, C) for t in (w, g, m, v))
    br = R
    if R * C * 4 > (1 << 20):
        br = 8
        while br * 2 * C * 4 <= (1 << 20) and R % (br * 2) == 0:
            br *= 2
    b1c = 1.0 - ADAM_B1 ** ADAM_STEP
    b2c = 1.0 - ADAM_B2 ** ADAM_STEP

    def body(w_ref, g_ref, m_ref, v_ref, *rest):
        d_ref, mo_ref, vo_ref = rest[-3:]
        gv = g_ref[...]
        mn = ADAM_B1 * m_ref[...] + (1.0 - ADAM_B1) * gv
        vn = ADAM_B2 * v_ref[...] + (1.0 - ADAM_B2) * (gv * gv)
        d_ref[...] = -ADAM_LR * ((mn / b1c) / (jnp.sqrt(vn / b2c) + ADAM_EPS) + ADAM_WD * w_ref[...])
        mo_ref[...] = mn
        vo_ref[...] = vn

    spec = pl.BlockSpec((br, C), lambda r: (r, 0))
    extra = [] if after is None else [after]
    outs = pl.pallas_call(body, grid=(R // br,), in_specs=[spec] * 4 + [pl.BlockSpec(memory_space=pl.ANY)] * len(extra), out_specs=[spec] * 3,
                          out_shape=[jax.ShapeDtypeStruct((R, C), F32)] * 3, name=name,
                          compiler_params=_cparams(("arbitrary",)))(w2, g2, m2, v2, *extra)
    return tuple(t.reshape(shape) for t in outs)


def _layer_shards(g, ax):
    s = g.shape
    r = g.reshape(s[:ax] + (N_DEV, s[ax] // N_DEV) + s[ax + 1:])
    return (jnp.moveaxis(r, ax, 0) if ax else r).reshape(N_DEV, -1)


def _unshard(g, ax):
    r = jnp.moveaxis(g, 0, ax)
    s = r.shape
    return r.reshape(s[:ax] + (s[ax] * s[ax + 1],) + s[ax + 2:])


def _pack(parts, dtype, row_mult):
    lead = parts[0].shape[:-1]
    flat = jnp.concatenate([p.astype(dtype) for p in parts], axis=-1)
    n = flat.shape[-1]
    per = row_mult * 1024
    tot = -(-n // per) * per
    flat = jnp.pad(flat, [(0, 0)] * len(lead) + [(0, tot - n)])
    return flat.reshape(lead + (tot // 1024, 1024))


def _pad_heads(w, lo, hi):
    K = w.shape[0]
    r = w.reshape(K, N_HEADS, -1)[:, :, lo:hi]
    return jnp.pad(r, ((0, 0), (0, 0), (0, HEAD_PAD - (hi - lo)))).reshape(K, N_HEADS * HEAD_PAD)


def kernel(x, c, positions, ada_w, ada_b, norm_g, mla_w_dq, mla_q_norm_g, mla_w_uq, mla_w_dkv, mla_kv_norm_g, mla_w_ukv, mla_w_o, conv_w_pw1, conv_b_pw1, conv_w_dw, conv_b_dw, conv_ln_g, conv_ln_b, conv_w_pw2, conv_b_pw2, pool_w, pool_b, pool_scale, ffn_w1, ffn_w2, loss_target, m_ada_w, m_ada_b, m_norm_g, m_mla_w_dq, m_mla_q_norm_g, m_mla_w_uq, m_mla_w_dkv, m_mla_kv_norm_g, m_mla_w_ukv, m_mla_w_o, m_conv_w_pw1, m_conv_b_pw1, m_conv_w_dw, m_conv_b_dw, m_conv_ln_g, m_conv_ln_b, m_conv_w_pw2, m_conv_b_pw2, m_pool_w, m_pool_b, m_pool_scale, m_ffn_w1, m_ffn_w2, v_ada_w, v_ada_b, v_norm_g, v_mla_w_dq, v_mla_q_norm_g, v_mla_w_uq, v_mla_w_dkv, v_mla_kv_norm_g, v_mla_w_ukv, v_mla_w_o, v_conv_w_pw1, v_conv_b_pw1, v_conv_w_dw, v_conv_b_dw, v_conv_ln_g, v_conv_ln_b, v_conv_w_pw2, v_conv_b_pw2, v_pool_w, v_pool_b, v_pool_scale, v_ffn_w1, v_ffn_w2):
    args = dict(locals())
    W = {n: args[n] for n, _ in WEIGHTS}
    M1 = {n: args['m_' + n] for n, _ in WEIGHTS}
    V2 = {n: args['v_' + n] for n, _ in WEIGHTS}
    D = D_MODEL
    T = x.shape[1]
    L = ffn_w1.shape[0]
    xi, yi, ci = _place()
    me = 4 * xi + 2 * yi + ci
    n_ada = ada_w.shape[2]

    small_sizes = [W[n].size for n in SMALL]
    small_in = _pack([c.reshape(-1)] + [W[n].reshape(-1) for n in SMALL], F32, 8)
    small_all = _ag_small("ag_small_params", small_in).reshape(N_DEV, -1)
    c_all = small_all[:, :D]
    Ws = {}
    off = D
    for n, sz in zip(SMALL, small_sizes):
        Ws[n] = _unshard(small_all[:, off:off + sz].reshape((N_DEV,) + W[n].shape), SHARD_AXIS[n])
        off += sz
    c16 = jnp.pad(c_all, ((0, 16 - N_DEV), (0, 0)))

    ada_b_cols = lax.dynamic_slice_in_dim(ada_b, me * n_ada, n_ada, axis=1).reshape(L, 1, n_ada)
    mod_part = _mod_part(c16, ada_w, ada_b_cols)[:, :N_DEV]
    mod_all = _ag_small("ag_mod", mod_part.reshape(L * N_DEV, n_ada)).reshape(N_DEV, L, N_DEV, n_ada)
    mod_mine = lax.dynamic_index_in_dim(mod_all, me, axis=2, keepdims=False)
    mod = jnp.transpose(mod_mine, (1, 0, 2)).reshape(L, 6, 1, D)

    mla_names = [n for n in BIG if n.startswith('mla')]
    first_items = [(n, W[n][0]) for n in mla_names]
    later_items = [(n, W[n][1:]) for n in mla_names] + [(n, W[n]) for n in BIG if not n.startswith(('mla', 'ffn'))]
    first_all, = _ag_big("ag_weights", [_pack([a.reshape(-1) for _, a in first_items], MM, 32)])
    wf = [ffn_w1.astype(MM), ffn_w2.astype(MM), _pack([a.reshape(-1) for _, a in later_items], MM, 32)]
    wf, first_all, mod = lax.optimization_barrier((wf, first_all, mod))
    wf_land = [lax.dynamic_update_slice(lax.empty((N_DEV,) + w.shape, MM), w[None], (me,) + (0,) * w.ndim) for w in wf]
    ag_sems, wf_thru, wf_land, ag_token = _copies_start("ag_ffn_start", wf, wf_land, FIRST_LEVEL_PEERS, False)

    def unpack(g, items, dropped):
        flat, out, off = g.reshape(N_DEV, -1), {}, 0
        for n, a in items:
            out[n] = _unshard(flat[:, off:off + a.size].reshape((N_DEV,) + a.shape), SHARD_AXIS[n] - dropped)
            off += a.size
        return out

    n_mla = mla_w_dq.shape[0]
    w_dq, w_uq_p, w_ukv_k, w_ukv_v, w_dkv_p, w_o_p = ([None] * n_mla for _ in range(6))

    def set_mla(j, w):
        w_dq[j] = w['mla_w_dq']
        w_uq_p[j] = _pad_heads(w['mla_w_uq'], 0, QK_NOPE + QK_ROPE)
        w_ukv_k[j] = _pad_heads(w['mla_w_ukv'], 0, QK_NOPE)
        w_ukv_v[j] = _pad_heads(w['mla_w_ukv'], QK_NOPE, QK_NOPE + V_HEAD)
        w_dkv_p[j] = jnp.pad(jnp.concatenate([w['mla_w_dkv'][:, :KV_LORA], jnp.zeros((D, QK_NOPE), MM), w['mla_w_dkv'][:, KV_LORA:]], axis=1),
                             ((0, 0), (0, HEAD_PAD - QK_NOPE - QK_ROPE)))
        w_o_p[j] = jnp.pad(w['mla_w_o'].reshape(N_HEADS, V_HEAD, D), ((0, 0), (0, HEAD_PAD - V_HEAD), (0, 0))).reshape(N_HEADS * HEAD_PAD, D)

    set_mla(0, unpack(first_all, first_items, 1))
    w_dw32 = jnp.pad(Ws['conv_w_dw'], ((0, 0), (0, 32 - CONV_W), (0, 0)))
    row = lambda t: t.reshape(1, -1)

    half = QK_ROPE // 2
    inv_freq = ROPE_THETA ** (-jnp.arange(0, QK_ROPE, 2, dtype=F32) / QK_ROPE)
    invf = jnp.zeros((1, HEAD_PAD), F32).at[0, QK_NOPE:QK_NOPE + half].set(inv_freq).at[0, QK_NOPE + half:QK_NOPE + QK_ROPE].set(inv_freq)
    rC, rS1, rS2 = _rope_tables(positions.reshape(T, 1).astype(F32), invf)

    xs = x.reshape(T, D)
    saved = []
    for i in range(L):
        kind, j = i % 3, i // 3
        sh_m, sc_m, gt_m, sh_f, sc_f, gt_f = (mod[i, r] for r in range(6))
        g = [row(Ws['norm_g'][i, r]) for r in range(4)]
        st = dict(x0=xs)
        if i == 0:
            sc_m = sc_m + ag_token[0:1, 0:1]
        if kind == 0:
            cq_raw, cq, ckv_raw, ckv, q, k, v, h = _mla_proj(f"mla_proj{i}", xs, g[0], sc_m, sh_m, rC, rS1, rS2, w_dq[j],
                                                             row(Ws['mla_q_norm_g'][j]), w_uq_p[j], w_dkv_p[j],
                                                             row(Ws['mla_kv_norm_g'][j]), w_ukv_k[j], w_ukv_v[j])
            o, lse = _attn_fwd(f"attn_fwd{i}", q, k, v)
            y, xs = _mm_post(f"mla_out{i}", o, w_o_p[j], None, xs, g[1], gt_m)
            st.update(h=h, cq_raw=cq_raw, cq=cq, ckv_raw=ckv_raw, ckv=ckv, q=q, k=k, v=v, o=o, lse=lse, y=y)
        elif kind == 1:
            a, u, h = _conv_glu(f"conv_glu{i}", xs, g[0], sc_m, sh_m, w_pw1[j], row(W['conv_b_pw1'][j]))
            uc, z, y, xs = _conv_dw(f"conv_dw{i}", u, w_dw32[j], row(W['conv_b_dw'][j]), row(W['conv_ln_g'][j]), row(W['conv_ln_b'][j]),
                                    w_pw2[j], row(W['conv_b_pw2'][j]), xs, g[1], gt_m)
            st.update(h=h, a=a, u=u, uc=uc, z=z, y=y)
        else:
            h = _prenorm(f"prenorm_m{i}", xs, g[0], sc_m, sh_m, F32)
            p, ypre, y, xs = _pool_fwd(f"pool_fwd{i}", h, w_pool[j], row(Ws['pool_b'][j]), row(Ws['pool_scale'][j]), xs, g[1], gt_m)
            st.update(p=p, ypre=ypre, y=y)
        st['x1'] = xs
        if i == 0:
            wg = _copies_wait("ag_ffn_wait", ag_sems, wf_thru, wf_land, xs, FIRST_LEVEL_PEERS, False)
            w1g, w2g, later_all = _ag_forward("ag_ffn_forward", wg)
            later = unpack(later_all, later_items, 0)
            for jj in range(1, n_mla):
                set_mla(jj, {n: later[n][jj - 1] for n in mla_names})
            w_pw1, w_pw2, w_pool = later['conv_w_pw1'], later['conv_w_pw2'], later['pool_w']
        hf, af, yf, xs = _ffn_fwd(f"ffn_fwd{i}", i, xs, g[2], sc_f, sh_f, w1g, w2g, g[3], gt_f)
        st.update(hf=hf, af=af, yf=yf)
        saved.append(st)

    dx, loss_row = _loss_head(xs, loss_target.reshape(T, D))

    G = {}
    dmod = [None] * L
    dnorm = [None] * L
    rs_pending = None
    ffn_red = [lax.empty(ffn_w1.shape, F32), lax.empty(ffn_w2.shape, F32)]
    for i in reversed(range(L)):
        kind, j = i % 3, i // 3
        sh_m, sc_m, gt_m, sh_f, sc_f, gt_f = (mod[i, r] for r in range(6))
        g = [row(Ws['norm_g'][i, r]) for r in range(4)]
        st = saved[i]
        dy, da, dx, dg3, dgt_f, dsh_f, dsc_f, dg2 = _ffn_bwd(f"ffn_bwd{i}", i, st['yf'], g[3], gt_f, st['af'], w1g, w2g, st['x1'], dx, g[2], sc_f)
        wire1, own1 = _mm_tn_wire(f"ffn_dw1_{i}", st['hf'], da, me, False, False)
        wire2, own2 = _mm_tn_wire(f"ffn_dw2_{i}", st['af'], dy, me, True, True)
        if rs_pending is not None:
            ffn_red = _rs_finish(rs_pending, wire2, me, ffn_red)
        wires = [wire1, wire2]
        rs_sems, wires_thru, rs_lands, rs_token = _copies_start(f"rs_start{i}", wires, [lax.empty(w.shape, MM) for w in wires], ALL_PEERS, True)
        rs_pending = (i, rs_sems, wires_thru, rs_lands, [own1, own2])
        gt_m = gt_m + rs_token[0:1, 0:1]
        if kind == 0:
            dy, do, dg1, dgt_m = _post_bwd_nt(f"mla_do{i}", dx, st['y'], g[1], gt_m, w_o_p[j])
            delta = _attn_delta(f"attn_delta{i}", do, st['o'])
            dq, dk, dv = _attn_bwd(f"attn_bwd{i}", st['q'], st['k'], st['v'], do, st['lse'], delta)
            dq_pre, dcq_raw, dckv_all, dx, dqg, dkvg, dsh_m, dsc_m, dg0 = _mla_proj_bwd(
                f"mla_proj_bwd{i}", dq, dk, dv, rC, rS1, rS2, st['cq_raw'], st['ckv_raw'], st['x0'], dx, w_uq_p[j], w_ukv_k[j], w_ukv_v[j],
                w_dq[j], w_dkv_p[j], row(Ws['mla_q_norm_g'][j]), row(Ws['mla_kv_norm_g'][j]), g[0], sc_m)
            dwo = _mm_tn(f"mla_dwo{i}", st['o'], dy)
            dwuq = _mm_tn(f"mla_dwuq{i}", st['cq'], dq_pre)
            dwk = _mm_tn(f"mla_dwukvk{i}", st['ckv'], dk)
            dwv = _mm_tn(f"mla_dwukvv{i}", st['ckv'], dv)
            dwdq = _mm_tn(f"mla_dwdq{i}", st['h'], dcq_raw)
            dwdkv = _mm_tn(f"mla_dwdkv{i}", st['h'], dckv_all)
            G.setdefault('mla_w_o', [None] * n_mla)[j] = dwo.reshape(N_HEADS, HEAD_PAD, D)[:, :V_HEAD].reshape(N_HEADS * V_HEAD, D)
            G.setdefault('mla_w_uq', [None] * n_mla)[j] = dwuq.reshape(Q_LORA, N_HEADS, HEAD_PAD)[:, :, :QK_NOPE + QK_ROPE].reshape(Q_LORA, -1)
            G.setdefault('mla_w_ukv', [None] * n_mla)[j] = jnp.concatenate(
                [dwk.reshape(KV_LORA, N_HEADS, HEAD_PAD)[:, :, :QK_NOPE], dwv.reshape(KV_LORA, N_HEADS, HEAD_PAD)[:, :, :V_HEAD]], axis=2).reshape(KV_LORA, -1)
            G.setdefault('mla_w_dq', [None] * n_mla)[j] = dwdq
            G.setdefault('mla_w_dkv', [None] * n_mla)[j] = jnp.concatenate([dwdkv[:, :KV_LORA], dwdkv[:, KV_LORA + QK_NOPE:KV_LORA + QK_NOPE + QK_ROPE]], axis=1)
            G.setdefault('mla_q_norm_g', [None] * n_mla)[j] = dqg[0]
            G.setdefault('mla_kv_norm_g', [None] * n_mla)[j] = dkvg[0]
        elif kind == 1:
            duc, dy, dlng, dlnb, dbdw, dg1, dgt_m, dysum = _conv_bwd1(f"conv_bwd1_{i}", dx, st['y'], g[1], gt_m, st['uc'], w_pw2[j],
                                                                      row(W['conv_ln_g'][j]), row(W['conv_ln_b'][j]))
            da, dx, dwdw, dbpw1, dsh_m, dsc_m, dg0 = _conv_bwd2(f"conv_bwd2_{i}", duc, st['u'], st['a'], st['x0'], dx, w_dw32[j], w_pw1[j], g[0], sc_m)
            G['conv_w_pw2'] = [_mm_tn(f"conv_dwpw2_{i}", st['z'], dy)]
            G['conv_w_pw1'] = [_mm_tn(f"conv_dwpw1_{i}", st['h'], da)]
            G['conv_w_dw'] = [dwdw[:CONV_W]]
            G['conv_b_pw1'], G['conv_b_dw'], G['conv_ln_g'], G['conv_ln_b'], G['conv_b_pw2'] = [dbpw1[0]], [dbdw[0]], [dlng[0]], [dlnb[0]], [dysum[0]]
        else:
            dp, dypre, dscale, dpb, dg1, dgt_m = _pool_bwd1(f"pool_bwd1_{i}", dx, st['y'], g[1], gt_m, st['ypre'], row(Ws['pool_scale'][j]), w_pool[j])
            dx, dsh_m, dsc_m, dg0 = _pool_bwd2(f"pool_bwd2_{i}", dp, st['x0'], dx, g[0], sc_m)
            G['pool_w'] = [_mm_tn(f"pool_dw{i}", st['p'], dypre, diag=len(POOL_WINDOWS))]
            G['pool_b'] = [dpb.reshape(len(POOL_WINDOWS), -1)]
            G['pool_scale'] = [dscale[0]]
        dmod[i] = jnp.concatenate([dsh_m, dsc_m, dgt_m, dsh_f, dsc_f, dgt_f], axis=1)
        dnorm[i] = jnp.concatenate([dg0, dg1, dg2, dg3], axis=0)
    G['norm_g'] = dnorm
    grad_x = dx.reshape(x.shape)

    rs_names = [n for n, ax in WEIGHTS if ax is not None and n != 'ada_w' and not n.startswith('ffn')]
    pieces = [(n, _layer_shards(g, SHARD_AXIS[n] - 1)) for n in rs_names for g in G[n]]
    big = [(n, p) for n, p in pieces if p.shape[1] % (8 * 1024) == 0]
    small = [(n, p) for n, p in pieces if p.shape[1] % (8 * 1024) != 0]
    packed = jnp.concatenate([p.reshape(N_DEV, -1, 1024) for _, p in big] + [_pack([p for _, p in small], F32, 8)], axis=1)
    ffn_red = _rs_finish(rs_pending, dx, me, ffn_red)
    my_chip = 2 * xi + yi
    p4 = packed.reshape((4, 2) + packed.shape[1:])
    pair_recv, = _rs_pair("rs_pair", [p4])
    chip_wire, chip_own = _pair_sum("rs_pair_sum", p4, pair_recv, ci, my_chip)

    dmod_mine = jnp.concatenate(dmod, axis=1).reshape(-1)
    fin_in = _pack([dmod_mine] + [G[n][0].reshape(-1) for n in REPL] + [loss_row.reshape(-1)], F32, 8)
    fin_all = _ag_small("ag_final", fin_in)
    chip_wire, fin_all = lax.optimization_barrier((chip_wire, fin_all))
    chip_sems, chip_thru, chip_land, chip_token = _copies_start("rs_chips_start", [chip_wire], [lax.empty(chip_wire.shape, MM)],
                                                                CHIP_PEERS, 'chip')
    grads = {'ffn_w1': ffn_red[0], 'ffn_w2': ffn_red[1]}
    fin_sum = _sum_devices("final_sum", fin_all).reshape(-1)
    nm = L * 6 * D
    grads['ada_b'] = fin_sum[:nm].reshape(L, 6 * D)
    off = nm
    for n in REPL:
        grads[n] = fin_sum[off:off + W[n].size].reshape(W[n].shape)
        off += W[n].size
    loss = fin_sum[off]
    dmod_all = fin_all.reshape(N_DEV, -1)[:, :nm].reshape(N_DEV, L, 6 * D)
    dmod_cols = lax.dynamic_slice_in_dim(dmod_all, me * n_ada, n_ada, axis=2)
    dmod16 = jnp.pad(jnp.transpose(dmod_cols, (1, 0, 2)), ((0, 0), (0, 16 - N_DEV), (0, 0)))
    grads['ada_w'] = _ada_w_grad(c16, dmod16)
    deltas, new_m, new_v = {}, {}, {}
    done = chip_token
    for n in ['ffn_w1', 'ffn_w2', 'ada_w', 'ada_b'] + REPL:
        deltas[n], new_m[n], new_v[n] = _adamw("adamw_" + n, W[n], grads[n], M1[n], V2[n], after=done)
        done = deltas[n]
    chip_recv, = _copies_wait("rs_chips_wait", chip_sems, chip_thru, chip_land, done, CHIP_PEERS, 'chip')
    red = _chip_sum("rs_chip_sum", chip_own, chip_recv, my_chip)
    got = {}
    row0 = 0
    for n, p in big:
        rows = p.shape[1] // 1024
        got.setdefault(n, []).append(red[row0:row0 + rows])
        row0 += rows
    tail = red[row0:].reshape(-1)
    off = 0
    for n, p in small:
        got.setdefault(n, []).append(tail[off:off + p.shape[1]])
        off += p.shape[1]
    for n in rs_names:
        grads[n] = jnp.stack([g_.reshape(W[n].shape[1:]) for g_ in got[n]], axis=0)

    for n in rs_names:
        deltas[n], new_m[n], new_v[n] = _adamw("adamw_" + n, W[n], grads[n], M1[n], V2[n])
    names = [n for n, _ in WEIGHTS]
    return (loss, grad_x, *[grads[n] for n in names], *[deltas[n] for n in names], *[new_m[n] for n in names],
            *[new_v[n] for n in names])
```

```python
import math

import jax
import jax.numpy as jnp
from jax import lax
from jax.experimental import pallas as pl
from jax.experimental.pallas import tpu as pltpu

F32 = jnp.float32
MM = jnp.bfloat16
EPS = 1e-6
NEG = -1e30
N_DEV = 8
VMEM_LIMIT = 48 * 1024 * 1024
MESH = pl.DeviceIdType.MESH

D_MODEL = 1024
N_HEADS = 16
HEAD_PAD = 128
QK_NOPE, QK_ROPE, V_HEAD = 64, 32, 64
Q_LORA, KV_LORA = 384, 256
CHUNK = 64
CONV_W = 31
POOL_WINDOWS = (2, 4, 8, 16)
ROPE_THETA = 10000.0
ATT_SCALE = 1.0 / math.sqrt(QK_NOPE + QK_ROPE)

ADAM_LR, ADAM_B1, ADAM_B2, ADAM_EPS, ADAM_WD, ADAM_STEP = 0.001, 0.9, 0.999, 1e-08, 0.01, 10

WEIGHTS = [('ada_w', 2), ('ada_b', None), ('norm_g', 2), ('mla_w_dq', 1), ('mla_q_norm_g', 1), ('mla_w_uq', 2),
           ('mla_w_dkv', 1), ('mla_kv_norm_g', 1), ('mla_w_ukv', 2), ('mla_w_o', 1), ('conv_w_pw1', 2),
           ('conv_b_pw1', None), ('conv_w_dw', 2), ('conv_b_dw', None), ('conv_ln_g', None), ('conv_ln_b', None),
           ('conv_w_pw2', 1), ('conv_b_pw2', None), ('pool_w', 2), ('pool_b', 2), ('pool_scale', 1),
           ('ffn_w1', 2), ('ffn_w2', 1)]
SHARD_AXIS = dict(WEIGHTS)
BIG = ['mla_w_dq', 'mla_w_uq', 'mla_w_dkv', 'mla_w_ukv', 'mla_w_o', 'conv_w_pw1', 'conv_w_pw2', 'pool_w', 'ffn_w1', 'ffn_w2']
SMALL = ['norm_g', 'mla_q_norm_g', 'mla_kv_norm_g', 'conv_w_dw', 'pool_b', 'pool_scale']
REPL = ['conv_b_pw1', 'conv_b_dw', 'conv_ln_g', 'conv_ln_b', 'conv_b_pw2']


def _dot(a, b):
    return jnp.dot(a.astype(MM), b.astype(MM), preferred_element_type=F32)


def _dot_nt(a, b):
    return lax.dot_general(a.astype(MM), b.astype(MM), (((1,), (1,)), ((), ())), preferred_element_type=F32)


def _dot_tn(a, b):
    return lax.dot_general(a.astype(MM), b.astype(MM), (((0,), (0,)), ((), ())), preferred_element_type=F32)


def _sigmoid(x):
    return 1.0 / (1.0 + jnp.exp(-x))


def _rstd(x):
    return lax.rsqrt(jnp.mean(x * x, axis=-1, keepdims=True) + EPS)


def _rms(x, g):
    return x * _rstd(x) * g


def _rms_bwd(x, g, dout):
    r = _rstd(x)
    xn = x * r
    dg = jnp.sum(dout * xn, axis=0, keepdims=True)
    dxn = dout * g
    dx = r * (dxn - xn * jnp.mean(dxn * xn, axis=-1, keepdims=True))
    return dx, dg


def _prenorm_bwd(x, g0, sc, dh):
    r = _rstd(x)
    xn = x * r
    dsh = jnp.sum(dh, axis=0, keepdims=True)
    dsc = jnp.sum(dh * (xn * g0), axis=0, keepdims=True)
    dn = dh * (1.0 + sc)
    dg0 = jnp.sum(dn * xn, axis=0, keepdims=True)
    dxn = dn * g0
    dx = r * (dxn - xn * jnp.mean(dxn * xn, axis=-1, keepdims=True))
    return dx, dsh, dsc, dg0


def _cparams(sem, vmem=VMEM_LIMIT):
    return pltpu.CompilerParams(dimension_semantics=sem, vmem_limit_bytes=vmem)


def _rows(name, body, n_rows, tm, rows, consts, outs, accs=(), scratch=()):
    tm = min(tm, n_rows)
    nblk = n_rows // tm
    nr, nc, no, na = len(rows), len(consts), len(outs), len(accs)
    in_specs, args = [], []
    for a, kind in rows:
        if kind == 'cur':
            im = lambda i: (i, 0)
        elif kind == 'prev':
            im = lambda i: (jnp.maximum(i - 1, 0), 0)
        else:
            im = lambda i: (jnp.minimum(i + 1, nblk - 1), 0)
        in_specs.append(pl.BlockSpec((tm, a.shape[1]), im))
        args.append(a)
    for a in consts:
        in_specs.append(pl.BlockSpec(a.shape, lambda i, nd=a.ndim: (0,) * nd))
        args.append(a)
    out_specs = [pl.BlockSpec((o[0], tm), lambda i: (0, i)) if len(o) == 3 else pl.BlockSpec((tm, o[0]), lambda i: (i, 0)) for o in outs]
    out_specs += [pl.BlockSpec(s, lambda i, nd=len(s): (0,) * nd) for s in accs]
    out_shape = [jax.ShapeDtypeStruct((o[0], n_rows) if len(o) == 3 else (n_rows, o[0]), o[1]) for o in outs]
    out_shape += [jax.ShapeDtypeStruct(s, F32) for s in accs]

    def kern(*refs):
        i = pl.program_id(0)
        rr = refs[:nr]
        cc = refs[nr:nr + nc]
        oo = refs[nr + nc:nr + nc + no]
        aa = refs[nr + nc + no:nr + nc + no + na]
        ss = refs[nr + nc + no + na:]

        @pl.when(i == 0)
        def _():
            for a in aa:
                a[...] = jnp.zeros(a.shape, F32)

        body(i, nblk, rr, cc, oo, aa, ss)

    return pl.pallas_call(kern, grid=(nblk,), in_specs=in_specs, out_specs=out_specs, out_shape=out_shape,
                          scratch_shapes=list(scratch), name=name, compiler_params=_cparams(("arbitrary",)))(*args)


def _place():
    return lax.axis_index("x"), lax.axis_index("y"), lax.axis_index("c")


def _ag_small(name, xs):
    R, C = xs.shape

    def body(x_ref, out_ref, send_sems, recv_sems):
        x, y, c = _place()
        me = 4 * x + 2 * y + c
        out_ref[me] = x_ref[...]
        copies = []
        for k in range(1, N_DEV):
            peer = ((1 - x) if k & 4 else x, (1 - y) if k & 2 else y, (1 - c) if k & 1 else c)
            cp = pltpu.make_async_remote_copy(src_ref=x_ref, dst_ref=out_ref.at[me], send_sem=send_sems.at[k - 1],
                                              recv_sem=recv_sems.at[k - 1], device_id=peer, device_id_type=MESH)
            cp.start()
            copies.append(cp)
        for cp in copies:
            cp.wait()

    return pl.pallas_call(
        body, out_shape=jax.ShapeDtypeStruct((N_DEV, R, C), xs.dtype),
        in_specs=[pl.BlockSpec(memory_space=pltpu.VMEM)], out_specs=pl.BlockSpec(memory_space=pltpu.VMEM),
        scratch_shapes=[pltpu.SemaphoreType.DMA((N_DEV - 1,)), pltpu.SemaphoreType.DMA((N_DEV - 1,))], name=name)(xs)


def _ag_big(name, xs):
    nt = len(xs)

    def body(*refs):
        x_refs, out_refs = refs[:nt], refs[nt:2 * nt]
        send_sems, recv_sems, local_sems = refs[2 * nt:]
        x, y, c = _place()
        me, sibling = (x, y, c), (x, y, 1 - c)
        chips = [(1 - x, y), (x, 1 - y), (1 - x, 1 - y)]

        def copy(t, k, block, to, own=False):
            px, py, pc = block
            rows = out_refs[t].at[4 * px + 2 * py + pc]
            return pltpu.make_async_remote_copy(src_ref=x_refs[t] if own else rows, dst_ref=rows, send_sem=send_sems.at[7 * t + k],
                                                recv_sem=recv_sems.at[7 * t + k], device_id=to, device_id_type=MESH)

        mine = [pltpu.make_async_copy(x_refs[t], out_refs[t].at[4 * x + 2 * y + c], local_sems.at[t]) for t in range(nt)]
        for cp in mine:
            cp.start()
        first = []
        for t in range(nt):
            first.append(copy(t, 0, me, sibling, own=True))
            first += [copy(t, 1 + j, me, (*chip, c), own=True) for j, chip in enumerate(chips)]
        for cp in first:
            cp.start()
        passed = []
        for t in range(nt):
            for j, chip in enumerate(chips):
                copy(t, 1 + j, (*chip, c), me).wait_recv()
                cp = copy(t, 4 + j, (*chip, c), sibling)
                cp.start()
                passed.append(cp)
        for t in range(nt):
            copy(t, 0, sibling, me).wait_recv()
            for j, chip in enumerate(chips):
                copy(t, 4 + j, (*chip, 1 - c), me).wait_recv()
        for cp in first + passed:
            cp.wait_send()
        for cp in mine:
            cp.wait()

    hbm = pl.BlockSpec(memory_space=pl.ANY)
    return pl.pallas_call(
        body, out_shape=[jax.ShapeDtypeStruct((N_DEV,) + t.shape, t.dtype) for t in xs],
        in_specs=[hbm] * nt, out_specs=[hbm] * nt,
        scratch_shapes=[pltpu.SemaphoreType.DMA((7 * nt,)), pltpu.SemaphoreType.DMA((7 * nt,)), pltpu.SemaphoreType.DMA((nt,))],
        name=name)(*xs)


def _rs_pair(name, ps):
    nt = len(ps)

    def body(*refs):
        p_refs, recv_refs = refs[:nt], refs[nt:2 * nt]
        send_sems, recv_sems = refs[2 * nt:]
        x, y, c = _place()
        copies = []
        for t in range(nt):
            for j in range(4):
                cp = pltpu.make_async_remote_copy(src_ref=p_refs[t].at[j, 1 - c], dst_ref=recv_refs[t].at[j], send_sem=send_sems.at[4 * t + j],
                                                  recv_sem=recv_sems.at[4 * t + j], device_id=(x, y, 1 - c), device_id_type=MESH)
                cp.start()
                copies.append(cp)
        for cp in copies:
            cp.wait()

    hbm = pl.BlockSpec(memory_space=pl.ANY)
    return pl.pallas_call(
        body, out_shape=[jax.ShapeDtypeStruct((4,) + p.shape[2:], p.dtype) for p in ps], in_specs=[hbm] * nt, out_specs=[hbm] * nt,
        scratch_shapes=[pltpu.SemaphoreType.DMA((4 * nt,)), pltpu.SemaphoreType.DMA((4 * nt,))], name=name)(*ps)


RS_ROWS = 256


def _row_block(r):
    return next(t for t in range(RS_ROWS, 0, -16) if r % t == 0)


def _pair_sum(name, p, recv, my_c, my_chip):
    _, _, r, c = p.shape
    tr = _row_block(r)

    def body(sc_ref, p_ref, r_ref, o_ref, own_ref):
        s = p_ref[...] + r_ref[...]
        o_ref[...] = s.astype(MM)

        @pl.when(pl.program_id(1) == sc_ref[1])
        def _():
            own_ref[...] = s

    return pl.pallas_call(
        body, grid_spec=pltpu.PrefetchScalarGridSpec(
            num_scalar_prefetch=1, grid=(r // tr, 4),
            in_specs=[pl.BlockSpec((None, None, tr, c), lambda i, j, sc: (j, sc[0], i, 0)),
                      pl.BlockSpec((None, tr, c), lambda i, j, sc: (j, i, 0))],
            out_specs=[pl.BlockSpec((None, tr, c), lambda i, j, sc: (j, i, 0)), pl.BlockSpec((tr, c), lambda i, j, sc: (i, 0))]),
        out_shape=[jax.ShapeDtypeStruct((4, r, c), MM), jax.ShapeDtypeStruct((r, c), F32)], name=name,
        compiler_params=_cparams(("arbitrary", "arbitrary")))(jnp.stack([my_c, my_chip]), p, recv)


def _chip_sum(name, own, recv, my_chip):
    _, r, c = recv.shape
    tr = _row_block(r)

    def body(sc_ref, own_ref, r_ref, o_ref):
        acc = jnp.zeros((tr, c), F32)
        for j in range(4):
            acc = acc + jnp.where(sc_ref[0] == j, own_ref[...], r_ref[j].astype(F32))
        o_ref[...] = acc

    return pl.pallas_call(
        body, grid_spec=pltpu.PrefetchScalarGridSpec(
            num_scalar_prefetch=1, grid=(r // tr,),
            in_specs=[pl.BlockSpec((tr, c), lambda i, sc: (i, 0)), pl.BlockSpec((4, tr, c), lambda i, sc: (0, i, 0))],
            out_specs=pl.BlockSpec((tr, c), lambda i, sc: (i, 0))),
        out_shape=jax.ShapeDtypeStruct((r, c), F32), name=name,
        compiler_params=_cparams(("arbitrary",)))(my_chip.reshape(1), own, recv)


HBM_SPEC = pl.BlockSpec(memory_space=pltpu.HBM)
SEM_SPEC = pl.BlockSpec(memory_space=pltpu.SEMAPHORE)
SPLIT_EFFECT = pltpu.SideEffectType.DATAFLOW_SIDE_EFFECTING
ALL_PEERS = (1, 2, 3, 4, 5, 6, 7)
FIRST_LEVEL_PEERS = (1, 4, 2, 6)
CHIP_PEERS = (4, 2, 6)


def _split_copies(src_refs, land_refs, sems, masks, src_per_peer):
    n, nt = len(masks), len(src_refs)
    x, y, c = _place()
    by_chip = src_per_peer == 'chip'
    slot = 2 * x + y if by_chip else 4 * x + 2 * y + c
    copies = []
    for t in range(nt):
        for k, mask in enumerate(masks):
            px, py, pc = (1 - x) if mask & 4 else x, (1 - y) if mask & 2 else y, (1 - c) if mask & 1 else c
            src = src_refs[t].at[2 * px + py if by_chip else 4 * px + 2 * py + pc] if src_per_peer else src_refs[t]
            copies.append(pltpu.make_async_remote_copy(src_ref=src, dst_ref=land_refs[t].at[slot], send_sem=sems[t * n + k],
                                                       recv_sem=sems[nt * n + t * n + k], device_id=(px, py, pc), device_id_type=MESH))
    return copies


def _copies_start(name, srcs, lands, masks, src_per_peer):
    nt, ns = len(srcs), 2 * len(masks) * len(srcs)

    def body(*refs):
        for cp in _split_copies(refs[:nt], refs[nt:2 * nt], refs[2 * nt:2 * nt + ns], masks, src_per_peer):
            cp.start()
        token = refs[-1]
        token[...] = jnp.zeros(token.shape, F32)

    outs = pl.pallas_call(
        body, name=name,
        out_shape=(pltpu.SemaphoreType.DMA(()),) * ns + tuple(pltpu.HBM(a.shape, a.dtype) for a in list(srcs) + list(lands))
        + (jax.ShapeDtypeStruct((8, 128), F32),),
        in_specs=(HBM_SPEC,) * (2 * nt), out_specs=(SEM_SPEC,) * ns + (HBM_SPEC,) * (2 * nt) + (pl.BlockSpec(memory_space=pltpu.VMEM),),
        input_output_aliases={t: ns + t for t in range(2 * nt)}, compiler_params=pltpu.CompilerParams(has_side_effects=SPLIT_EFFECT))(
            *[pltpu.with_memory_space_constraint(a, pltpu.HBM) for a in list(srcs) + list(lands)])
    return outs[:ns], outs[ns:ns + nt], outs[ns + nt:ns + 2 * nt], outs[-1]


def _copies_wait(name, sems, srcs_thru, lands_thru, after, masks, src_per_peer):
    nt, ns = len(srcs_thru), len(sems)

    def body(*refs):
        for cp in _split_copies(refs[:nt], refs[nt:2 * nt], refs[2 * nt:2 * nt + ns], masks, src_per_peer):
            cp.wait_send()
            cp.wait_recv()

    thru = list(srcs_thru) + list(lands_thru)
    return pl.pallas_call(
        body, name=name, out_shape=tuple(pltpu.HBM(a.shape, a.dtype) for a in thru),
        in_specs=(HBM_SPEC,) * (2 * nt) + (SEM_SPEC,) * ns + (pl.BlockSpec(memory_space=pl.ANY),), out_specs=(HBM_SPEC,) * (2 * nt),
        input_output_aliases={t: t for t in range(2 * nt)}, compiler_params=pltpu.CompilerParams(has_side_effects=SPLIT_EFFECT))(
            *thru, *sems, after)[nt:]


def _ag_forward(name, gs):
    nt = len(gs)

    def body(*refs):
        o_refs, send_sems, recv_sems = refs[nt:2 * nt], refs[2 * nt], refs[2 * nt + 1]
        x, y, c = _place()
        chips = [(1 - x, y), (x, 1 - y), (1 - x, 1 - y)]

        def copy(t, j, pc):
            rows = o_refs[t].at[4 * chips[j][0] + 2 * chips[j][1] + pc]
            return pltpu.make_async_remote_copy(src_ref=rows, dst_ref=rows, send_sem=send_sems.at[3 * t + j], recv_sem=recv_sems.at[3 * t + j],
                                                device_id=(x, y, 1 - c), device_id_type=MESH)

        for t in range(nt):
            for j in range(3):
                copy(t, j, c).start()
        for t in range(nt):
            for j in range(3):
                copy(t, j, c).wait_send()
                copy(t, j, 1 - c).wait_recv()

    hbm = pl.BlockSpec(memory_space=pl.ANY)
    return pl.pallas_call(body, out_shape=[jax.ShapeDtypeStruct(g.shape, g.dtype) for g in gs], in_specs=[hbm] * nt, out_specs=[hbm] * nt,
                          scratch_shapes=[pltpu.SemaphoreType.DMA((3 * nt,)), pltpu.SemaphoreType.DMA((3 * nt,))],
                          input_output_aliases={t: t for t in range(nt)}, name=name)(*gs)


def _mm_tn_wire(name, a, b, me, sqrelu, shard_rows):
    T, M = a.shape
    N = b.shape[1]
    tk = min(2048, T)
    nk = T // tk
    if shard_rows:
        bm, bn = M // N_DEV, N
        a_spec = pl.BlockSpec((tk, 2 * bm), lambda j, k, m: (k, j))
        b_spec = pl.BlockSpec((tk, bn), lambda j, k, m: (k, 0))
        halves = (slice(0, bm), slice(None)), (slice(bm, 2 * bm), slice(None))
        acc_shape = (2 * bm, bn)
    else:
        bm, bn = M, N // N_DEV
        a_spec = pl.BlockSpec((tk, bm), lambda j, k, m: (k, 0))
        b_spec = pl.BlockSpec((tk, 2 * bn), lambda j, k, m: (k, j))
        halves = (slice(None), slice(0, bn)), (slice(None), slice(bn, 2 * bn))
        acc_shape = (bm, 2 * bn)

    def body(me_ref, a_ref, b_ref, wire_ref, own_ref, acc):
        j, k = pl.program_id(0), pl.program_id(1)

        @pl.when(k == 0)
        def _():
            acc[...] = jnp.zeros(acc.shape, F32)

        av = a_ref[...]
        if sqrelu:
            r = jnp.maximum(av, 0.0)
            av = r * r
        acc[...] += _dot_tn(av, b_ref[...])

        for hh in range(2):
            @pl.when(k == nk - 1)
            def _():
                wire_ref[hh] = acc[halves[hh]].astype(MM)

            @pl.when((k == nk - 1) & (2 * j + hh == me_ref[0]))
            def _():
                own_ref[...] = acc[halves[hh]]

    return pl.pallas_call(
        body, grid_spec=pltpu.PrefetchScalarGridSpec(
            num_scalar_prefetch=1, grid=(N_DEV // 2, nk), in_specs=[a_spec, b_spec],
            out_specs=[pl.BlockSpec((2, bm, bn), lambda j, k, m: (j, 0, 0)), pl.BlockSpec((bm, bn), lambda j, k, m: (0, 0))],
            scratch_shapes=[pltpu.VMEM(acc_shape, F32)]),
        out_shape=[jax.ShapeDtypeStruct((N_DEV, bm, bn), MM), jax.ShapeDtypeStruct((bm, bn), F32)], name=name,
        compiler_params=_cparams(("arbitrary", "arbitrary")))(me.reshape(1), a, b)


def _rs_final(name, own, recv, me, stack, li):
    _, r, c = recv.shape
    tr = RS_ROWS

    def body(me_ref, own_ref, r_ref, s_ref, o_ref):
        acc = jnp.zeros((tr, c), F32)
        for j in range(N_DEV):
            acc = acc + jnp.where(me_ref[0] == j, own_ref[...], r_ref[j].astype(F32))
        o_ref[...] = acc

    return pl.pallas_call(
        body, grid_spec=pltpu.PrefetchScalarGridSpec(
            num_scalar_prefetch=1, grid=(r // tr,),
            in_specs=[pl.BlockSpec((tr, c), lambda i, m: (i, 0)), pl.BlockSpec((N_DEV, tr, c), lambda i, m: (0, i, 0)),
                      pl.BlockSpec(memory_space=pl.ANY)],
            out_specs=pl.BlockSpec((None, tr, c), lambda i, m: (li, i, 0))),
        out_shape=jax.ShapeDtypeStruct(stack.shape, F32), input_output_aliases={3: 0}, name=name,
        compiler_params=_cparams(("arbitrary",)))(me.reshape(1), own, recv, stack)


def _rs_finish(pending, after, me, stacks):
    i, sems, wires_thru, lands, owns = pending
    recvs = _copies_wait(f"rs_wait{i}", sems, wires_thru, lands, after, ALL_PEERS, True)
    return [_rs_final(f"rs_final{i}_{t}", owns[t], recvs[t], me, stacks[t], i) for t in range(len(owns))]


def _mod_part(c16, ada_w, ada_b_cols):
    L, D, n = ada_w.shape

    def body(c_ref, w_ref, b_ref, o_ref):
        cv = c_ref[...]
        o_ref[...] = _dot(cv * _sigmoid(cv), w_ref[...]) + b_ref[...]

    return pl.pallas_call(
        body, grid=(L,), in_specs=[pl.BlockSpec((16, D), lambda i: (0, 0)), pl.BlockSpec((None, D, n), lambda i: (i, 0, 0)),
                                   pl.BlockSpec((None, 1, n), lambda i: (i, 0, 0))],
        out_specs=pl.BlockSpec((None, 16, n), lambda i: (i, 0, 0)), out_shape=jax.ShapeDtypeStruct((L, 16, n), F32),
        name="ada_mod", compiler_params=_cparams(("arbitrary",)))(c16, ada_w, ada_b_cols)


def _ada_w_grad(c16, dmod16):
    L, _, n = dmod16.shape
    D = c16.shape[1]

    def body(c_ref, d_ref, o_ref):
        cv = c_ref[...]
        o_ref[...] = _dot_tn(cv * _sigmoid(cv), d_ref[...])

    return pl.pallas_call(
        body, grid=(L,), in_specs=[pl.BlockSpec((16, D), lambda i: (0, 0)), pl.BlockSpec((None, 16, n), lambda i: (i, 0, 0))],
        out_specs=pl.BlockSpec((None, D, n), lambda i: (i, 0, 0)), out_shape=jax.ShapeDtypeStruct((L, D, n), F32),
        name="ada_w_grad", compiler_params=_cparams(("arbitrary",)))(c16, dmod16)


def _sum_devices(name, g):
    _, R, C = g.shape

    def body(g_ref, o_ref):
        acc = g_ref[0]
        for d in range(1, N_DEV):
            acc = acc + g_ref[d]
        o_ref[...] = acc

    return pl.pallas_call(body, out_shape=jax.ShapeDtypeStruct((R, C), F32), name=name)(g)


def _prenorm(name, x, g0, sc, sh, dtype):
    T, D = x.shape

    def body(i, n, rr, cc, oo, aa, ss):
        oo[0][...] = (_rms(rr[0][...], cc[0][...]) * (1.0 + cc[1][...]) + cc[2][...]).astype(dtype)

    return _rows(name, body, T, 512, [(x, 'cur')], [g0, sc, sh], [(D, dtype)])[0]


def _post_bwd_math(d, yv, g1v, gtv):
    dgt = jnp.sum(d * _rms(yv, g1v), axis=0, keepdims=True)
    dy, dg1 = _rms_bwd(yv, g1v, d * gtv)
    return dy, dg1, dgt


def _post_bwd_nt(name, dxo, y, g1, gt, w, o):
    T, D = y.shape
    K = w.shape[0]

    def body(i, n, rr, cc, oo, aa, ss):
        dy, dg1, dgt = _post_bwd_math(rr[0][...], rr[1][...], cc[0][...], cc[1][...])
        aa[0][...] += dg1
        aa[1][...] += dgt
        dy = dy.astype(MM)
        oo[0][...] = dy
        do = _dot_nt(dy, cc[2][...]).astype(MM)
        oo[1][...] = do
        tm = do.shape[0]
        lane = lax.broadcasted_iota(jnp.int32, (tm, HEAD_PAD), 1) // 8
        cols = jnp.zeros((tm, HEAD_PAD), F32)
        for h in range(N_HEADS):
            hsl = slice(h * HEAD_PAD, (h + 1) * HEAD_PAD)
            r = jnp.sum(do[:, hsl].astype(F32) * rr[2][:, hsl].astype(F32), axis=1, keepdims=True)
            cols = jnp.where(lane == h, r, cols)
        oo[2][...] = cols.T

    dy, do, delta, dg1, dgt = _rows(name, body, T, 512, [(dxo, 'cur'), (y, 'cur'), (o, 'cur')], [g1, gt, w],
                                    [(D, MM), (K, MM), (HEAD_PAD, F32, 'T')], accs=[(1, D)] * 2)
    return dy, do, delta.reshape(N_HEADS, 8, T), dg1, dgt


def _mm_post(name, a, w, bias, x, g1, gt):
    T, D = x.shape
    consts = [w, g1, gt] + ([bias] if bias is not None else [])

    def body(i, n, rr, cc, oo, aa, ss):
        y = _dot(rr[0][...], cc[0][...])
        if bias is not None:
            y = y + cc[3][...]
        oo[0][...] = y
        oo[1][...] = rr[1][...] + cc[2][...] * _rms(y, cc[1][...])

    return _rows(name, body, T, 512, [(a, 'cur'), (x, 'cur')], consts, [(D, F32), (D, F32)])


def _mm_tn(name, a, b, sqrelu=False, col_shards=0, diag=0):
    T, M = a.shape
    N = b.shape[1]
    tk = min(512, T)
    nk = T // tk
    if diag:
        bm, bn = M // diag, N // diag
        grid = (diag, 1, nk)
        a_spec = pl.BlockSpec((tk, bm), lambda g, n, k: (k, g))
        b_spec = pl.BlockSpec((tk, bn), lambda g, n, k: (k, g))
        o_spec = pl.BlockSpec((None, bm, bn), lambda g, n, k: (g, 0, 0))
        o_shape = (diag, bm, bn)
    else:
        bm = min(M, 1024)
        bn = N // col_shards if col_shards else min(N, 1024)
        grid = (M // bm, N // bn, nk)
        a_spec = pl.BlockSpec((tk, bm), lambda m, n, k: (k, m))
        b_spec = pl.BlockSpec((tk, bn), lambda m, n, k: (k, n))
        if col_shards:
            o_spec = pl.BlockSpec((None, bm, bn), lambda m, n, k: (n, m, 0))
            o_shape = (col_shards, M, bn)
        else:
            o_spec = pl.BlockSpec((bm, bn), lambda m, n, k: (m, n))
            o_shape = (M, N)

    def body(a_ref, b_ref, o_ref):
        @pl.when(pl.program_id(2) == 0)
        def _():
            o_ref[...] = jnp.zeros(o_ref.shape, F32)

        av = a_ref[...]
        if sqrelu:
            r = jnp.maximum(av, 0.0)
            av = r * r
        o_ref[...] += _dot_tn(av, b_ref[...])

    return pl.pallas_call(body, grid=grid, in_specs=[a_spec, b_spec], out_specs=o_spec,
                          out_shape=jax.ShapeDtypeStruct(o_shape, F32), name=name,
                          compiler_params=_cparams(("arbitrary", "arbitrary", "arbitrary")))(a, b)


FFN_SHARDS = 4
FFN_BWD_SHARDS = 4
FFN_BWD_VMEM = 56 * 1024 * 1024

def _ffn_fwd(name, li, x, g0, sc, sh, w1g, w2g, g1, gt):
    T, D = x.shape
    nf, tf = w1g.shape[0], w1g.shape[-1]
    F = nf * tf
    tm = min(512, T)

    def body(x_ref, g0_ref, sc_ref, sh_ref, w1_ref, w2_ref, g1_ref, gt_ref, h_ref, a_ref, y_ref, xo_ref, acc):
        f = pl.program_id(1)

        @pl.when(f == 0)
        def _():
            acc[...] = jnp.zeros(acc.shape, F32)
            h_ref[...] = (_rms(x_ref[...], g0_ref[...]) * (1.0 + sc_ref[...]) + sh_ref[...]).astype(MM)

        hv = h_ref[...]
        part = None
        for hh in range(FFN_SHARDS):
            a = _dot(hv, w1_ref[hh])
            a_ref[:, hh * tf:(hh + 1) * tf] = a.astype(MM)
            r = jnp.maximum(a, 0.0)
            p = _dot(r * r, w2_ref[hh])
            part = p if part is None else part + p
        acc[...] += part

        @pl.when(f == nf // FFN_SHARDS - 1)
        def _():
            y = acc[...]
            y_ref[...] = y
            xo_ref[...] = x_ref[...] + gt_ref[...] * _rms(y, g1_ref[...])

    row = lambda t, f: (t, 0)
    one = lambda t, f: (0, 0)
    return pl.pallas_call(
        body, grid=(T // tm, nf // FFN_SHARDS),
        in_specs=[pl.BlockSpec((tm, D), row)] + [pl.BlockSpec((1, D), one)] * 3
        + [pl.BlockSpec((FFN_SHARDS, None, D, tf), lambda t, f: (f, li, 0, 0)), pl.BlockSpec((FFN_SHARDS, None, tf, D), lambda t, f: (f, li, 0, 0)),
           pl.BlockSpec((1, D), one), pl.BlockSpec((1, D), one)],
        out_specs=[pl.BlockSpec((tm, D), row), pl.BlockSpec((tm, FFN_SHARDS * tf), lambda t, f: (t, f)), pl.BlockSpec((tm, D), row),
                   pl.BlockSpec((tm, D), row)],
        out_shape=[jax.ShapeDtypeStruct((T, D), MM), jax.ShapeDtypeStruct((T, F), MM), jax.ShapeDtypeStruct((T, D), F32),
                   jax.ShapeDtypeStruct((T, D), F32)],
        scratch_shapes=[pltpu.VMEM((tm, D), F32)], name=name,
        compiler_params=_cparams(("arbitrary", "arbitrary")))(x, g0, sc, sh, w1g, w2g, g1, gt)


def _ffn_bwd(name, li, y, g1, gt, a, w1g, w2g, x, dxo, g0, sc):
    T, D = x.shape
    nf, tf = w1g.shape[0], w1g.shape[-1]
    F = nf * tf
    tm = min(512, T)
    ns = FFN_BWD_SHARDS

    def body(y_ref, g1_ref, gt_ref, a_ref, w1_ref, w2_ref, x_ref, dxo_ref, g0_ref, sc_ref,
             dy_ref, da_ref, dx_ref, dg1_ref, dgt_ref, dsh_ref, dsc_ref, dg0_ref, acc):
        t, f = pl.program_id(0), pl.program_id(1)

        @pl.when((t == 0) & (f == 0))
        def _():
            for r in (dg1_ref, dgt_ref, dsh_ref, dsc_ref, dg0_ref):
                r[...] = jnp.zeros(r.shape, F32)

        @pl.when(f == 0)
        def _():
            acc[...] = jnp.zeros(acc.shape, F32)
            d, yv, g1v = dxo_ref[...], y_ref[...], g1_ref[...]
            dgt_ref[...] += jnp.sum(d * _rms(yv, g1v), axis=0, keepdims=True)
            dyf, dg1 = _rms_bwd(yv, g1v, d * gt_ref[...])
            dg1_ref[...] += dg1
            dy_ref[...] = dyf.astype(MM)

        dyv = dy_ref[...]
        dyv = dyv + dyv
        part = None
        for hh in range(ns):
            cols = slice(hh * tf, (hh + 1) * tf)
            du = _dot_nt(dyv, w2_ref[hh])
            da = (du * jnp.maximum(a_ref[:, cols], 0.0).astype(F32)).astype(MM)
            da_ref[:, cols] = da
            p = _dot_nt(da, w1_ref[hh])
            part = p if part is None else part + p
        acc[...] += part

        @pl.when(f == nf // ns - 1)
        def _():
            dx, dsh, dsc, dg0 = _prenorm_bwd(x_ref[...], g0_ref[...], sc_ref[...], acc[...])
            dx_ref[...] = dxo_ref[...] + dx
            dsh_ref[...] += dsh
            dsc_ref[...] += dsc
            dg0_ref[...] += dg0

    row = lambda t, f: (t, 0)
    one = lambda t, f: (0, 0)
    blk = lambda t, f: (t, f)
    return pl.pallas_call(
        body, grid=(T // tm, nf // ns),
        in_specs=[pl.BlockSpec((tm, D), row), pl.BlockSpec((1, D), one), pl.BlockSpec((1, D), one), pl.BlockSpec((tm, ns * tf), blk),
                  pl.BlockSpec((ns, None, D, tf), lambda t, f: (f, li, 0, 0)),
                  pl.BlockSpec((ns, None, tf, D), lambda t, f: (f, li, 0, 0)), pl.BlockSpec((tm, D), row), pl.BlockSpec((tm, D), row),
                  pl.BlockSpec((1, D), one), pl.BlockSpec((1, D), one)],
        out_specs=[pl.BlockSpec((tm, D), row), pl.BlockSpec((tm, ns * tf), blk), pl.BlockSpec((tm, D), row)] + [pl.BlockSpec((1, D), one)] * 5,
        out_shape=[jax.ShapeDtypeStruct((T, D), MM), jax.ShapeDtypeStruct((T, F), MM), jax.ShapeDtypeStruct((T, D), F32)]
        + [jax.ShapeDtypeStruct((1, D), F32)] * 5,
        scratch_shapes=[pltpu.VMEM((tm, D), F32)], name=name,
        compiler_params=_cparams(("arbitrary", "arbitrary"), FFN_BWD_VMEM))(y, g1, gt, a, w1g, w2g, x, dxo, g0, sc)


def _rope_tables(pos, invf):
    T = pos.shape[0]

    def body(i, n, rr, cc, oo, aa, ss):
        ang = rr[0][...] * cc[0][...]
        lane = lax.broadcasted_iota(jnp.int32, ang.shape, 1)
        cs, sn = jnp.cos(ang), jnp.sin(ang)
        oo[0][...] = jnp.where((lane >= QK_NOPE) & (lane < QK_NOPE + QK_ROPE), cs, 1.0)
        oo[1][...] = jnp.where((lane >= QK_NOPE) & (lane < QK_NOPE + QK_ROPE // 2), -sn, 0.0)
        oo[2][...] = jnp.where((lane >= QK_NOPE + QK_ROPE // 2) & (lane < QK_NOPE + QK_ROPE), sn, 0.0)

    return _rows("rope_tables", body, T, 512, [(pos, 'cur')], [invf], [(HEAD_PAD, F32)] * 3)


def _rope(v, C, S1, S2):
    n = v.shape[1]
    reps = n // HEAD_PAD
    if reps > 1:
        C, S1, S2 = (jnp.tile(t, (1, reps)) for t in (C, S1, S2))
    return v * C + pltpu.roll(v, n - QK_ROPE // 2, 1) * S1 + pltpu.roll(v, QK_ROPE // 2, 1) * S2


def _unrope(d, C, S1, S2):
    n = d.shape[1]
    reps = n // HEAD_PAD
    if reps > 1:
        C, S1, S2 = (jnp.tile(t, (1, reps)) for t in (C, S1, S2))
    return d * C + pltpu.roll(d * S1, QK_ROPE // 2, 1) + pltpu.roll(d * S2, n - QK_ROPE // 2, 1)


MLA_ROWS = 512
MLA_BWD_ROWS = 512


def _mla_proj(name, x, g0, sc, sh, C, S1, S2, w_dq, qg, w_uq, w_dkv, kvg, w_ukv_k, w_ukv_v):
    T, D = x.shape
    HP = N_HEADS * HEAD_PAD

    def body(i, n, rr, cc, oo, aa, ss):
        hv = (_rms(rr[0][...], cc[7][...]) * (1.0 + cc[8][...]) + cc[9][...]).astype(MM)
        oo[7][...] = hv
        Cv, S1v, S2v = rr[1][...], rr[2][...], rr[3][...]
        cq_raw = _dot(hv, cc[0][...])
        cq = _rms(cq_raw, cc[1][...]).astype(MM)
        q = _rope(_dot(cq, cc[2][...]), Cv, S1v, S2v)
        ckv_all = _dot(hv, cc[3][...])
        ckv_raw = ckv_all[:, :KV_LORA]
        ckv = _rms(ckv_raw, cc[4][...]).astype(MM)
        kr = _rope(ckv_all[:, KV_LORA:], Cv, S1v, S2v)
        k = _dot(ckv, cc[5][...]) + jnp.tile(kr, (1, N_HEADS))
        v = _dot(ckv, cc[6][...])
        v = jnp.where(lax.broadcasted_iota(jnp.int32, v.shape, 1) % HEAD_PAD == V_HEAD, 1.0, v)
        oo[0][...] = cq_raw
        oo[1][...] = cq
        oo[2][...] = ckv_raw
        oo[3][...] = ckv
        oo[4][...] = q.astype(MM)
        oo[5][...] = k.astype(MM)
        oo[6][...] = v.astype(MM)

    return _rows(name, body, T, MLA_ROWS, [(x, 'cur'), (C, 'cur'), (S1, 'cur'), (S2, 'cur')],
                 [w_dq, qg, w_uq, w_dkv, kvg, w_ukv_k, w_ukv_v, g0, sc, sh],
                 [(Q_LORA, F32), (Q_LORA, MM), (KV_LORA, F32), (KV_LORA, MM), (HP, MM), (HP, MM), (HP, MM), (D, MM)])


ATT_HEADS = 4
ATT_BLOCK = 512
ATT_FWD_BLOCK = 1024


def _chunk_mask_t(tk, tq):
    ki = lax.broadcasted_iota(jnp.int32, (tk, tq), 0) // CHUNK
    qi = lax.broadcasted_iota(jnp.int32, (tk, tq), 1) // CHUNK
    return ki <= qi


def _attn_fwd(name, q, k, v):
    T = q.shape[0]
    tb = min(ATT_FWD_BLOCK, T)
    nb = T // tb
    nh = ATT_HEADS
    hs = [slice(h * HEAD_PAD, (h + 1) * HEAD_PAD) for h in range(nh)]

    def body(q_ref, k_ref, v_ref, o_ref, lse_ref):
        qb = pl.program_id(1)

        def k_block(k0, masked, st):
            new = []
            for h in range(nh):
                m, acc = st[h]
                s = _dot_nt(k_ref[pl.ds(k0, tb), hs[h]], q_ref[:, hs[h]])
                if masked:
                    s = jnp.where(_chunk_mask_t(tb, tb), s, NEG)
                m_new = jnp.maximum(m, jnp.max(s, axis=0, keepdims=True))
                alpha = jnp.exp((m - m_new) * ATT_SCALE)
                p = jnp.exp((s - m_new) * ATT_SCALE)
                acc = alpha * acc + _dot_tn(v_ref[pl.ds(k0, tb), hs[h]], p)
                new.append((m_new, acc))
            return tuple(new)

        st = tuple((jnp.full((1, tb), NEG, F32), jnp.zeros((HEAD_PAD, tb), F32)) for _ in range(nh))
        st = k_block(pl.multiple_of(qb * tb, tb), True, st)
        st = lax.fori_loop(0, qb, lambda kb, s_: k_block(pl.multiple_of(kb * tb, tb), False, s_), st)
        for h in range(nh):
            m, acc = st[h]
            l = acc[V_HEAD:V_HEAD + 1, :]
            o_ref[:, hs[h]] = (acc / l).T.astype(MM)
            lse_ref[h] = jnp.broadcast_to(m * ATT_SCALE + jnp.log(l), (8, tb))

    blk = pl.BlockSpec((tb, nh * HEAD_PAD), lambda g, i: (i, g))
    res = pl.BlockSpec((T, nh * HEAD_PAD), lambda g, i: (0, g))
    return pl.pallas_call(
        body, grid=(N_HEADS // nh, nb), in_specs=[blk, res, res],
        out_specs=[blk, pl.BlockSpec((nh, 8, tb), lambda g, i: (g, 0, i))],
        out_shape=[jax.ShapeDtypeStruct(q.shape, MM), jax.ShapeDtypeStruct((N_HEADS, 8, T), F32)], name=name,
        compiler_params=_cparams(("arbitrary", "arbitrary")))(q, k, v)


def _attn_bwd(name, q, k, v, do, lse, delta):
    T = q.shape[0]
    tb = min(ATT_BLOCK, T)
    nb = T // tb
    nh = ATT_HEADS
    hs = [slice(h * HEAD_PAD, (h + 1) * HEAD_PAD) for h in range(nh)]

    def body(q_ref, k_ref, v_ref, do_ref, lse_ref, dl_ref, dq_ref, dk_ref, dv_ref, dq_acc, dk_acc, dv_acc):
        kb = pl.program_id(1)

        @pl.when(kb == 0)
        def _():
            dq_acc[...] = jnp.zeros(dq_acc.shape, F32)

        dk_acc[...] = jnp.zeros(dk_acc.shape, F32)
        dv_acc[...] = jnp.zeros(dv_acc.shape, F32)

        def q_block(q0, masked):
            for h in range(nh):
                qh = q_ref[pl.ds(q0, tb), hs[h]]
                doh = do_ref[pl.ds(q0, tb), hs[h]]
                kh = k_ref[:, hs[h]]
                s = _dot_nt(kh, qh) * ATT_SCALE
                if masked:
                    s = jnp.where(_chunk_mask_t(tb, tb), s, NEG)
                p = jnp.exp(s - lse_ref[h, 0:1, pl.ds(q0, tb)])
                ds = (p * (_dot_nt(v_ref[:, hs[h]], doh) - dl_ref[h, 0:1, pl.ds(q0, tb)]) * ATT_SCALE).astype(MM)
                dv_acc[:, hs[h]] += _dot(p, doh)
                dk_acc[:, hs[h]] += _dot(ds, qh)
                dq_acc[pl.ds(q0, tb), hs[h]] += _dot_tn(ds, kh)

        q_block(pl.multiple_of(kb * tb, tb), True)

        def rest(qb, c_):
            q_block(pl.multiple_of(qb * tb, tb), False)
            return c_

        lax.fori_loop(kb + 1, nb, rest, 0)
        dk_ref[...] = dk_acc[...].astype(MM)
        dv_ref[...] = dv_acc[...].astype(MM)

        @pl.when(kb == nb - 1)
        def _():
            dq_ref[...] = dq_acc[...].astype(MM)

    W = nh * HEAD_PAD
    blk = pl.BlockSpec((tb, W), lambda g, i: (i, g))
    res = pl.BlockSpec((T, W), lambda g, i: (0, g))
    rows = pl.BlockSpec((nh, 8, T), lambda g, i: (g, 0, 0))
    return pl.pallas_call(
        body, grid=(N_HEADS // nh, nb), in_specs=[res, blk, blk, res, rows, rows], out_specs=[res, blk, blk],
        out_shape=[jax.ShapeDtypeStruct(q.shape, MM)] * 3,
        scratch_shapes=[pltpu.VMEM((T, W), F32), pltpu.VMEM((tb, W), F32), pltpu.VMEM((tb, W), F32)],
        name=name, compiler_params=_cparams(("arbitrary", "arbitrary")))(q, k, v, do, lse, delta)


def _mla_proj_bwd(name, dq, dk, dv, C, S1, S2, cq_raw, ckv_raw, x, dxo, w_uq, w_ukv_k, w_ukv_v, w_dq, w_dkv, qg, kvg, g0, sc):
    T, D = x.shape
    HP = N_HEADS * HEAD_PAD

    def body(i, n, rr, cc, oo, aa, ss):
        Cv, S1v, S2v = rr[3][...], rr[4][...], rr[5][...]
        dq_pre = _unrope(rr[0][...].astype(F32), Cv, S1v, S2v).astype(MM)
        oo[0][...] = dq_pre
        dcq = _dot_nt(dq_pre, cc[0][...])
        dcq_raw, dqg = _rms_bwd(rr[6][...], cc[5][...], dcq)
        aa[0][...] += dqg
        dcq_raw = dcq_raw.astype(MM)
        oo[1][...] = dcq_raw
        dkv = rr[1][...]
        dkr = dkv[:, :HEAD_PAD].astype(F32)
        for hh in range(1, N_HEADS):
            dkr = dkr + dkv[:, hh * HEAD_PAD:(hh + 1) * HEAD_PAD].astype(F32)
        lane = lax.broadcasted_iota(jnp.int32, dkr.shape, 1)
        dkr = jnp.where((lane >= QK_NOPE) & (lane < QK_NOPE + QK_ROPE), _unrope(dkr, Cv, S1v, S2v), 0.0)
        dckv = _dot_nt(dkv, cc[1][...]) + _dot_nt(rr[2][...], cc[2][...])
        dckv_raw, dkvg = _rms_bwd(rr[7][...], cc[6][...], dckv)
        aa[1][...] += dkvg
        dckv_all = jnp.concatenate([dckv_raw, dkr], axis=1).astype(MM)
        oo[2][...] = dckv_all
        dh = _dot_nt(dcq_raw, cc[3][...]) + _dot_nt(dckv_all, cc[4][...])
        dx, dsh, dsc, dg0 = _prenorm_bwd(rr[8][...], cc[7][...], cc[8][...], dh)
        oo[3][...] = rr[9][...] + dx
        aa[2][...] += dsh
        aa[3][...] += dsc
        aa[4][...] += dg0

    return _rows(name, body, T, MLA_BWD_ROWS,
                 [(dq, 'cur'), (dk, 'cur'), (dv, 'cur'), (C, 'cur'), (S1, 'cur'), (S2, 'cur'), (cq_raw, 'cur'), (ckv_raw, 'cur'),
                  (x, 'cur'), (dxo, 'cur')],
                 [w_uq, w_ukv_k, w_ukv_v, w_dq, w_dkv, qg, kvg, g0, sc],
                 [(HP, MM), (Q_LORA, MM), (KV_LORA + HEAD_PAD, MM), (D, F32)],
                 accs=[(1, Q_LORA), (1, KV_LORA), (1, D), (1, D), (1, D)])


HALO = 32


def _windows(ext, tm, first):
    rolled = {0: ext}
    out = []
    for j in range(CONV_W):
        r = (first + j) % 8
        if r not in rolled:
            rolled[r] = pltpu.roll(ext, ext.shape[0] - r, 0)
        out.append(rolled[r][first + j - r:first + j - r + tm])
    return out


def _conv_glu(name, x, g0, sc, sh, w_pw1, b_pw1):
    T, D = x.shape

    def body(i, n, rr, cc, oo, aa, ss):
        hv = (_rms(rr[0][...], cc[2][...]) * (1.0 + cc[3][...]) + cc[4][...]).astype(MM)
        oo[2][...] = hv
        a = _dot(hv, cc[0][...]) + cc[1][...]
        oo[0][...] = a
        oo[1][...] = a[:, :D] * _sigmoid(a[:, D:])

    return _rows(name, body, T, 512, [(x, 'cur')], [w_pw1, b_pw1, g0, sc, sh], [(2 * D, F32), (D, F32), (D, MM)])


def _layernorm_parts(uc):
    xc = uc - jnp.mean(uc, axis=-1, keepdims=True)
    r = lax.rsqrt(jnp.mean(xc * xc, axis=-1, keepdims=True) + EPS)
    return xc * r, r


def _conv_dw(name, u, w_dw, b_dw, ln_g, ln_b, w_pw2, b_pw2, x, g1, gt):
    T, D = u.shape
    tm = min(256, T)

    def body(i, n, rr, cc, oo, aa, ss):
        ext = jnp.concatenate([jnp.where(i > 0, rr[1][tm - HALO:tm, :], 0.0), rr[0][...]], axis=0)
        uc = jnp.zeros((tm, D), F32) + cc[1][...]
        for kk, win in enumerate(_windows(ext, tm, HALO - (CONV_W - 1))):
            uc = uc + win * cc[0][kk:kk + 1, :]
        xh, _ = _layernorm_parts(uc)
        ln = xh * cc[2][...] + cc[3][...]
        z = (ln * _sigmoid(ln)).astype(MM)
        y = _dot(z, cc[4][...]) + cc[5][...]
        oo[0][...] = uc
        oo[1][...] = z
        oo[2][...] = y
        oo[3][...] = rr[2][...] + cc[7][...] * _rms(y, cc[6][...])

    return _rows(name, body, T, tm, [(u, 'cur'), (u, 'prev'), (x, 'cur')], [w_dw, b_dw, ln_g, ln_b, w_pw2, b_pw2, g1, gt],
                 [(D, F32), (D, MM), (D, F32), (D, F32)])


def _conv_bwd1(name, dxo, y, g1, gt, uc, w_pw2, ln_g, ln_b):
    T, D = uc.shape

    def body(i, n, rr, cc, oo, aa, ss):
        dy, dg1, dgt = _post_bwd_math(rr[0][...], rr[1][...], cc[3][...], cc[4][...])
        aa[3][...] += dg1
        aa[4][...] += dgt
        aa[5][...] += jnp.sum(dy, axis=0, keepdims=True)
        dy = dy.astype(MM)
        oo[1][...] = dy
        dz = _dot_nt(dy, cc[0][...])
        xh, r = _layernorm_parts(rr[2][...])
        g = cc[1][...]
        ln = xh * g + cc[2][...]
        sg = _sigmoid(ln)
        dln = dz * (sg * (1.0 + ln * (1.0 - sg)))
        aa[0][...] += jnp.sum(dln * xh, axis=0, keepdims=True)
        aa[1][...] += jnp.sum(dln, axis=0, keepdims=True)
        dxh = dln * g
        duc = r * (dxh - jnp.mean(dxh, axis=-1, keepdims=True) - xh * jnp.mean(dxh * xh, axis=-1, keepdims=True))
        aa[2][...] += jnp.sum(duc, axis=0, keepdims=True)
        oo[0][...] = duc

    return _rows(name, body, T, 512, [(dxo, 'cur'), (y, 'cur'), (uc, 'cur')], [w_pw2, ln_g, ln_b, g1, gt], [(D, F32), (D, MM)],
                 accs=[(1, D)] * 6)


def _conv_bwd2(name, duc, u, a, x, dxo, w_dw, w_pw1, g0, sc):
    T, D = u.shape
    tm = min(256, T)

    def body(i, n, rr, cc, oo, aa, ss):
        dcur = rr[0][...]
        extd = jnp.concatenate([dcur, jnp.where(i < n - 1, rr[1][0:HALO, :], 0.0)], axis=0)
        extu = jnp.concatenate([jnp.where(i > 0, rr[3][tm - HALO:tm, :], 0.0), rr[2][...]], axis=0)
        wd = _windows(extd, tm, 0)
        wu = _windows(extu, tm, HALO - (CONV_W - 1))
        du = jnp.zeros((tm, D), F32)
        for kk in range(CONV_W):
            du = du + wd[CONV_W - 1 - kk] * cc[0][kk:kk + 1, :]
            aa[0][kk:kk + 1, :] += jnp.sum(dcur * wu[kk], axis=0, keepdims=True)
        av = rr[4][...]
        a1, sg = av[:, :D], _sigmoid(av[:, D:])
        da = jnp.concatenate([du * sg, du * a1 * (sg * (1.0 - sg))], axis=1)
        aa[1][...] += jnp.sum(da, axis=0, keepdims=True)
        da = da.astype(MM)
        oo[0][...] = da
        dx, dsh, dsc, dg0 = _prenorm_bwd(rr[5][...], cc[2][...], cc[3][...], _dot_nt(da, cc[1][...]))
        oo[1][...] = rr[6][...] + dx
        aa[2][...] += dsh
        aa[3][...] += dsc
        aa[4][...] += dg0

    return _rows(name, body, T, tm,
                 [(duc, 'cur'), (duc, 'next'), (u, 'cur'), (u, 'prev'), (a, 'cur'), (x, 'cur'), (dxo, 'cur')],
                 [w_dw, w_pw1, g0, sc], [(2 * D, MM), (D, F32)],
                 accs=[(32, D), (1, 2 * D), (1, D), (1, D), (1, D)])


PHALO = 16


def _pool_fwd(name, h, w, b, scale, x, g1, gt):
    T, D = h.shape
    G = len(POOL_WINDOWS)
    Cg = D // G
    tm = min(256, T)

    def body(i, n, rr, cc, oo, aa, ss):
        ext = ss[0]
        ext[0:PHALO, :] = jnp.where(i > 0, rr[1][tm - PHALO:tm, :], 0.0)
        ext[PHALO:PHALO + tm, :] = rr[0][...]
        t_glob = i * tm + lax.broadcasted_iota(jnp.int32, (tm, 1), 0)
        ps, ys = [], []
        for g, win in enumerate(POOL_WINDOWS):
            cols = slice(g * Cg, (g + 1) * Cg)
            s = ext[pl.ds(PHALO, tm), cols]
            for j in range(1, win):
                s = s + ext[pl.ds(PHALO - j, tm), cols]
            cnt = jnp.minimum(t_glob + 1, win).astype(F32)
            p = (s / cnt - ext[pl.ds(PHALO, tm), cols]).astype(MM)
            ps.append(p)
            ys.append(_dot(p, cc[0][g]) + cc[1][:, cols])
        ypre = jnp.concatenate(ys, axis=1)
        y = ypre * cc[2][...]
        oo[0][...] = jnp.concatenate(ps, axis=1)
        oo[1][...] = ypre
        oo[2][...] = y
        oo[3][...] = rr[2][...] + cc[4][...] * _rms(y, cc[3][...])

    return _rows(name, body, T, tm, [(h, 'cur'), (h, 'prev'), (x, 'cur')], [w, b, scale, g1, gt],
                 [(D, MM), (D, F32), (D, F32), (D, F32)], scratch=[pltpu.VMEM((tm + PHALO, D), F32)])


def _pool_bwd1(name, dxo, y, g1, gt, ypre, scale, w):
    T, D = ypre.shape
    G = len(POOL_WINDOWS)
    Cg = D // G

    def body(i, n, rr, cc, oo, aa, ss):
        dyv, dg1, dgt = _post_bwd_math(rr[0][...], rr[1][...], cc[2][...], cc[3][...])
        aa[2][...] += dg1
        aa[3][...] += dgt
        aa[0][...] += jnp.sum(dyv * rr[2][...], axis=0, keepdims=True)
        dypre = dyv * cc[0][...]
        aa[1][...] += jnp.sum(dypre, axis=0, keepdims=True)
        dypre = dypre.astype(MM)
        oo[1][...] = dypre
        oo[0][...] = jnp.concatenate([_dot_nt(dypre[:, g * Cg:(g + 1) * Cg], cc[1][g]) for g in range(G)], axis=1)

    return _rows(name, body, T, 256, [(dxo, 'cur'), (y, 'cur'), (ypre, 'cur')], [scale, w, g1, gt], [(D, F32), (D, MM)],
                 accs=[(1, D)] * 4)


def _pool_bwd2(name, dp, x, dxo, g0, sc):
    T, D = x.shape
    G = len(POOL_WINDOWS)
    Cg = D // G
    tm = min(256, T)

    def body(i, n, rr, cc, oo, aa, ss):
        ext = ss[0]
        t_glob = i * tm + lax.broadcasted_iota(jnp.int32, (tm, 1), 0)
        dcur = rr[0][...]
        dhs = []
        for g, win in enumerate(POOL_WINDOWS):
            cols = slice(g * Cg, (g + 1) * Cg)
            cnt = jnp.minimum(t_glob + 1, win).astype(F32)
            ext[0:tm, cols] = dcur[:, cols] / cnt
            ext[tm:tm + PHALO, cols] = jnp.where(i < n - 1, rr[1][0:PHALO, cols] * (1.0 / win), 0.0)
        for g, win in enumerate(POOL_WINDOWS):
            cols = slice(g * Cg, (g + 1) * Cg)
            s = ext[pl.ds(0, tm), cols]
            for j in range(1, win):
                s = s + ext[pl.ds(j, tm), cols]
            dhs.append(s - dcur[:, cols])
        dx, dsh, dsc, dg0 = _prenorm_bwd(rr[2][...], cc[0][...], cc[1][...], jnp.concatenate(dhs, axis=1))
        oo[0][...] = rr[3][...] + dx
        aa[0][...] += dsh
        aa[1][...] += dsc
        aa[2][...] += dg0

    return _rows(name, body, T, tm, [(dp, 'cur'), (dp, 'next'), (x, 'cur'), (dxo, 'cur')], [g0, sc], [(D, F32)],
                 accs=[(1, D)] * 3, scratch=[pltpu.VMEM((tm + PHALO, D), F32)])


def _loss_head(x, tgt):
    T, D = x.shape

    def body(i, n, rr, cc, oo, aa, ss):
        err = rr[0][...] - rr[1][...]
        oo[0][...] = err * (1.0 / D)
        aa[0][...] += jnp.sum(err * err, axis=0, keepdims=True)

        @pl.when(i == n - 1)
        def _():
            aa[1][...] = jnp.broadcast_to(jnp.sum(aa[0][...], axis=1, keepdims=True) * (0.5 / D), (1, 128))

    dx, _, loss_row = _rows("loss_head", body, T, 512, [(x, 'cur'), (tgt, 'cur')], [], [(D, F32)], accs=[(1, D), (1, 128)])
    return dx, loss_row


def _adamw(name, w, g, m, v, after=None):
    shape = w.shape
    C = shape[-1]
    R = w.size // C
    w2, g2, m2, v2 = (t.reshape(R, C) for t in (w, g, m, v))
    br = R
    if R * C * 4 > (1 << 20):
        br = 8
        while br * 2 * C * 4 <= (1 << 20) and R % (br * 2) == 0:
            br *= 2
    b1c = 1.0 - ADAM_B1 ** ADAM_STEP
    b2c = 1.0 - ADAM_B2 ** ADAM_STEP

    def body(w_ref, g_ref, m_ref, v_ref, *rest):
        d_ref, mo_ref, vo_ref = rest[-3:]
        gv = g_ref[...]
        mn = ADAM_B1 * m_ref[...] + (1.0 - ADAM_B1) * gv
        vn = ADAM_B2 * v_ref[...] + (1.0 - ADAM_B2) * (gv * gv)
        d_ref[...] = -ADAM_LR * ((mn / b1c) / (jnp.sqrt(vn / b2c) + ADAM_EPS) + ADAM_WD * w_ref[...])
        mo_ref[...] = mn
        vo_ref[...] = vn

    spec = pl.BlockSpec((br, C), lambda r: (r, 0))
    extra = [] if after is None else [after]
    outs = pl.pallas_call(body, grid=(R // br,), in_specs=[spec] * 4 + [pl.BlockSpec(memory_space=pl.ANY)] * len(extra), out_specs=[spec] * 3,
                          out_shape=[jax.ShapeDtypeStruct((R, C), F32)] * 3, name=name,
                          compiler_params=_cparams(("arbitrary",)))(w2, g2, m2, v2, *extra)
    return tuple(t.reshape(shape) for t in outs)


def _layer_shards(g, ax):
    s = g.shape
    r = g.reshape(s[:ax] + (N_DEV, s[ax] // N_DEV) + s[ax + 1:])
    return (jnp.moveaxis(r, ax, 0) if ax else r).reshape(N_DEV, -1)


def _unshard(g, ax):
    r = jnp.moveaxis(g, 0, ax)
    s = r.shape
    return r.reshape(s[:ax] + (s[ax] * s[ax + 1],) + s[ax + 2:])


def _pack(parts, dtype, row_mult):
    lead = parts[0].shape[:-1]
    flat = jnp.concatenate([p.astype(dtype) for p in parts], axis=-1)
    n = flat.shape[-1]
    per = row_mult * 1024
    tot = -(-n // per) * per
    flat = jnp.pad(flat, [(0, 0)] * len(lead) + [(0, tot - n)])
    return flat.reshape(lead + (tot // 1024, 1024))


def _pad_heads(w, lo, hi):
    K = w.shape[0]
    r = w.reshape(K, N_HEADS, -1)[:, :, lo:hi]
    return jnp.pad(r, ((0, 0), (0, 0), (0, HEAD_PAD - (hi - lo)))).reshape(K, N_HEADS * HEAD_PAD)


def kernel(x, c, positions, ada_w, ada_b, norm_g, mla_w_dq, mla_q_norm_g, mla_w_uq, mla_w_dkv, mla_kv_norm_g, mla_w_ukv, mla_w_o, conv_w_pw1, conv_b_pw1, conv_w_dw, conv_b_dw, conv_ln_g, conv_ln_b, conv_w_pw2, conv_b_pw2, pool_w, pool_b, pool_scale, ffn_w1, ffn_w2, loss_target, m_ada_w, m_ada_b, m_norm_g, m_mla_w_dq, m_mla_q_norm_g, m_mla_w_uq, m_mla_w_dkv, m_mla_kv_norm_g, m_mla_w_ukv, m_mla_w_o, m_conv_w_pw1, m_conv_b_pw1, m_conv_w_dw, m_conv_b_dw, m_conv_ln_g, m_conv_ln_b, m_conv_w_pw2, m_conv_b_pw2, m_pool_w, m_pool_b, m_pool_scale, m_ffn_w1, m_ffn_w2, v_ada_w, v_ada_b, v_norm_g, v_mla_w_dq, v_mla_q_norm_g, v_mla_w_uq, v_mla_w_dkv, v_mla_kv_norm_g, v_mla_w_ukv, v_mla_w_o, v_conv_w_pw1, v_conv_b_pw1, v_conv_w_dw, v_conv_b_dw, v_conv_ln_g, v_conv_ln_b, v_conv_w_pw2, v_conv_b_pw2, v_pool_w, v_pool_b, v_pool_scale, v_ffn_w1, v_ffn_w2):
    args = dict(locals())
    W = {n: args[n] for n, _ in WEIGHTS}
    M1 = {n: args['m_' + n] for n, _ in WEIGHTS}
    V2 = {n: args['v_' + n] for n, _ in WEIGHTS}
    D = D_MODEL
    T = x.shape[1]
    L = ffn_w1.shape[0]
    xi, yi, ci = _place()
    me = 4 * xi + 2 * yi + ci
    n_ada = ada_w.shape[2]

    small_sizes = [W[n].size for n in SMALL]
    small_in = _pack([c.reshape(-1)] + [W[n].reshape(-1) for n in SMALL], F32, 8)
    small_all = _ag_small("ag_small_params", small_in).reshape(N_DEV, -1)
    c_all = small_all[:, :D]
    Ws = {}
    off = D
    for n, sz in zip(SMALL, small_sizes):
        Ws[n] = _unshard(small_all[:, off:off + sz].reshape((N_DEV,) + W[n].shape), SHARD_AXIS[n])
        off += sz
    c16 = jnp.pad(c_all, ((0, 16 - N_DEV), (0, 0)))

    ada_b_cols = lax.dynamic_slice_in_dim(ada_b, me * n_ada, n_ada, axis=1).reshape(L, 1, n_ada)
    mod_part = _mod_part(c16, ada_w, ada_b_cols)[:, :N_DEV]
    mod_all = _ag_small("ag_mod", mod_part.reshape(L * N_DEV, n_ada)).reshape(N_DEV, L, N_DEV, n_ada)
    mod_mine = lax.dynamic_index_in_dim(mod_all, me, axis=2, keepdims=False)
    mod = jnp.transpose(mod_mine, (1, 0, 2)).reshape(L, 6, 1, D)

    mla_names = [n for n in BIG if n.startswith('mla')]
    first_items = [(n, W[n][0]) for n in mla_names]
    later_items = [(n, W[n][1:]) for n in mla_names] + [(n, W[n]) for n in BIG if not n.startswith(('mla', 'ffn'))]
    first_all, = _ag_big("ag_weights", [_pack([a.reshape(-1) for _, a in first_items], MM, 32)])
    wf = [ffn_w1.astype(MM), ffn_w2.astype(MM), _pack([a.reshape(-1) for _, a in later_items], MM, 32)]
    wf, first_all, mod = lax.optimization_barrier((wf, first_all, mod))
    wf_land = [lax.dynamic_update_slice(lax.empty((N_DEV,) + w.shape, MM), w[None], (me,) + (0,) * w.ndim) for w in wf]
    ag_sems, wf_thru, wf_land, ag_token = _copies_start("ag_ffn_start", wf, wf_land, FIRST_LEVEL_PEERS, False)

    def unpack(g, items, dropped):
        flat, out, off = g.reshape(N_DEV, -1), {}, 0
        for n, a in items:
            out[n] = _unshard(flat[:, off:off + a.size].reshape((N_DEV,) + a.shape), SHARD_AXIS[n] - dropped)
            off += a.size
        return out

    n_mla = mla_w_dq.shape[0]
    w_dq, w_uq_p, w_ukv_k, w_ukv_v, w_dkv_p, w_o_p = ([None] * n_mla for _ in range(6))

    def set_mla(j, w):
        w_dq[j] = w['mla_w_dq']
        w_uq_p[j] = _pad_heads(w['mla_w_uq'], 0, QK_NOPE + QK_ROPE)
        w_ukv_k[j] = _pad_heads(w['mla_w_ukv'], 0, QK_NOPE)
        w_ukv_v[j] = _pad_heads(w['mla_w_ukv'], QK_NOPE, QK_NOPE + V_HEAD)
        w_dkv_p[j] = jnp.pad(jnp.concatenate([w['mla_w_dkv'][:, :KV_LORA], jnp.zeros((D, QK_NOPE), MM), w['mla_w_dkv'][:, KV_LORA:]], axis=1),
                             ((0, 0), (0, HEAD_PAD - QK_NOPE - QK_ROPE)))
        w_o_p[j] = jnp.pad(w['mla_w_o'].reshape(N_HEADS, V_HEAD, D), ((0, 0), (0, HEAD_PAD - V_HEAD), (0, 0))).reshape(N_HEADS * HEAD_PAD, D)

    set_mla(0, unpack(first_all, first_items, 1))
    w_dw32 = jnp.pad(Ws['conv_w_dw'], ((0, 0), (0, 32 - CONV_W), (0, 0)))
    row = lambda t: t.reshape(1, -1)

    half = QK_ROPE // 2
    inv_freq = ROPE_THETA ** (-jnp.arange(0, QK_ROPE, 2, dtype=F32) / QK_ROPE)
    invf = jnp.zeros((1, HEAD_PAD), F32).at[0, QK_NOPE:QK_NOPE + half].set(inv_freq).at[0, QK_NOPE + half:QK_NOPE + QK_ROPE].set(inv_freq)
    rC, rS1, rS2 = _rope_tables(positions.reshape(T, 1).astype(F32), invf)

    xs = x.reshape(T, D)
    saved = []
    for i in range(L):
        kind, j = i % 3, i // 3
        sh_m, sc_m, gt_m, sh_f, sc_f, gt_f = (mod[i, r] for r in range(6))
        g = [row(Ws['norm_g'][i, r]) for r in range(4)]
        st = dict(x0=xs)
        if i == 0:
            sc_m = sc_m + ag_token[0:1, 0:1]
        if kind == 0:
            cq_raw, cq, ckv_raw, ckv, q, k, v, h = _mla_proj(f"mla_proj{i}", xs, g[0], sc_m, sh_m, rC, rS1, rS2, w_dq[j],
                                                             row(Ws['mla_q_norm_g'][j]), w_uq_p[j], w_dkv_p[j],
                                                             row(Ws['mla_kv_norm_g'][j]), w_ukv_k[j], w_ukv_v[j])
            o, lse = _attn_fwd(f"attn_fwd{i}", q, k, v)
            y, xs = _mm_post(f"mla_out{i}", o, w_o_p[j], None, xs, g[1], gt_m)
            st.update(h=h, cq_raw=cq_raw, cq=cq, ckv_raw=ckv_raw, ckv=ckv, q=q, k=k, v=v, o=o, lse=lse, y=y)
        elif kind == 1:
            a, u, h = _conv_glu(f"conv_glu{i}", xs, g[0], sc_m, sh_m, w_pw1[j], row(W['conv_b_pw1'][j]))
            uc, z, y, xs = _conv_dw(f"conv_dw{i}", u, w_dw32[j], row(W['conv_b_dw'][j]), row(W['conv_ln_g'][j]), row(W['conv_ln_b'][j]),
                                    w_pw2[j], row(W['conv_b_pw2'][j]), xs, g[1], gt_m)
            st.update(h=h, a=a, u=u, uc=uc, z=z, y=y)
        else:
            h = _prenorm(f"prenorm_m{i}", xs, g[0], sc_m, sh_m, F32)
            p, ypre, y, xs = _pool_fwd(f"pool_fwd{i}", h, w_pool[j], row(Ws['pool_b'][j]), row(Ws['pool_scale'][j]), xs, g[1], gt_m)
            st.update(p=p, ypre=ypre, y=y)
        st['x1'] = xs
        if i == 0:
            wg = _copies_wait("ag_ffn_wait", ag_sems, wf_thru, wf_land, xs, FIRST_LEVEL_PEERS, False)
            w1g, w2g, later_all = _ag_forward("ag_ffn_forward", wg)
            later = unpack(later_all, later_items, 0)
            for jj in range(1, n_mla):
                set_mla(jj, {n: later[n][jj - 1] for n in mla_names})
            w_pw1, w_pw2, w_pool = later['conv_w_pw1'], later['conv_w_pw2'], later['pool_w']
        hf, af, yf, xs = _ffn_fwd(f"ffn_fwd{i}", i, xs, g[2], sc_f, sh_f, w1g, w2g, g[3], gt_f)
        st.update(hf=hf, af=af, yf=yf)
        saved.append(st)

    dx, loss_row = _loss_head(xs, loss_target.reshape(T, D))

    G = {}
    dmod = [None] * L
    dnorm = [None] * L
    rs_pending = None
    ffn_red = [lax.empty(ffn_w1.shape, F32), lax.empty(ffn_w2.shape, F32)]
    for i in reversed(range(L)):
        kind, j = i % 3, i // 3
        sh_m, sc_m, gt_m, sh_f, sc_f, gt_f = (mod[i, r] for r in range(6))
        g = [row(Ws['norm_g'][i, r]) for r in range(4)]
        st = saved[i]
        dy, da, dx, dg3, dgt_f, dsh_f, dsc_f, dg2 = _ffn_bwd(f"ffn_bwd{i}", i, st['yf'], g[3], gt_f, st['af'], w1g, w2g, st['x1'], dx, g[2], sc_f)
        wire1, own1 = _mm_tn_wire(f"ffn_dw1_{i}", st['hf'], da, me, False, False)
        wire2, own2 = _mm_tn_wire(f"ffn_dw2_{i}", st['af'], dy, me, True, True)
        if rs_pending is not None:
            ffn_red = _rs_finish(rs_pending, wire2, me, ffn_red)
        wires = [wire1, wire2]
        rs_sems, wires_thru, rs_lands, rs_token = _copies_start(f"rs_start{i}", wires, [lax.empty(w.shape, MM) for w in wires], ALL_PEERS, True)
        rs_pending = (i, rs_sems, wires_thru, rs_lands, [own1, own2])
        gt_m = gt_m + rs_token[0:1, 0:1]
        if kind == 0:
            dy, do, delta, dg1, dgt_m = _post_bwd_nt(f"mla_do{i}", dx, st['y'], g[1], gt_m, w_o_p[j], st['o'])
            dq, dk, dv = _attn_bwd(f"attn_bwd{i}", st['q'], st['k'], st['v'], do, st['lse'], delta)
            dq_pre, dcq_raw, dckv_all, dx, dqg, dkvg, dsh_m, dsc_m, dg0 = _mla_proj_bwd(
                f"mla_proj_bwd{i}", dq, dk, dv, rC, rS1, rS2, st['cq_raw'], st['ckv_raw'], st['x0'], dx, w_uq_p[j], w_ukv_k[j], w_ukv_v[j],
                w_dq[j], w_dkv_p[j], row(Ws['mla_q_norm_g'][j]), row(Ws['mla_kv_norm_g'][j]), g[0], sc_m)
            dwo = _mm_tn(f"mla_dwo{i}", st['o'], dy)
            dwuq = _mm_tn(f"mla_dwuq{i}", st['cq'], dq_pre)
            dwk = _mm_tn(f"mla_dwukvk{i}", st['ckv'], dk)
            dwv = _mm_tn(f"mla_dwukvv{i}", st['ckv'], dv)
            dwdq = _mm_tn(f"mla_dwdq{i}", st['h'], dcq_raw)
            dwdkv = _mm_tn(f"mla_dwdkv{i}", st['h'], dckv_all)
            G.setdefault('mla_w_o', [None] * n_mla)[j] = dwo.reshape(N_HEADS, HEAD_PAD, D)[:, :V_HEAD].reshape(N_HEADS * V_HEAD, D)
            G.setdefault('mla_w_uq', [None] * n_mla)[j] = dwuq.reshape(Q_LORA, N_HEADS, HEAD_PAD)[:, :, :QK_NOPE + QK_ROPE].reshape(Q_LORA, -1)
            G.setdefault('mla_w_ukv', [None] * n_mla)[j] = jnp.concatenate(
                [dwk.reshape(KV_LORA, N_HEADS, HEAD_PAD)[:, :, :QK_NOPE], dwv.reshape(KV_LORA, N_HEADS, HEAD_PAD)[:, :, :V_HEAD]], axis=2).reshape(KV_LORA, -1)
            G.setdefault('mla_w_dq', [None] * n_mla)[j] = dwdq
            G.setdefault('mla_w_dkv', [None] * n_mla)[j] = jnp.concatenate([dwdkv[:, :KV_LORA], dwdkv[:, KV_LORA + QK_NOPE:KV_LORA + QK_NOPE + QK_ROPE]], axis=1)
            G.setdefault('mla_q_norm_g', [None] * n_mla)[j] = dqg[0]
            G.setdefault('mla_kv_norm_g', [None] * n_mla)[j] = dkvg[0]
        elif kind == 1:
            duc, dy, dlng, dlnb, dbdw, dg1, dgt_m, dysum = _conv_bwd1(f"conv_bwd1_{i}", dx, st['y'], g[1], gt_m, st['uc'], w_pw2[j],
                                                                      row(W['conv_ln_g'][j]), row(W['conv_ln_b'][j]))
            da, dx, dwdw, dbpw1, dsh_m, dsc_m, dg0 = _conv_bwd2(f"conv_bwd2_{i}", duc, st['u'], st['a'], st['x0'], dx, w_dw32[j], w_pw1[j], g[0], sc_m)
            G['conv_w_pw2'] = [_mm_tn(f"conv_dwpw2_{i}", st['z'], dy)]
            G['conv_w_pw1'] = [_mm_tn(f"conv_dwpw1_{i}", st['h'], da)]
            G['conv_w_dw'] = [dwdw[:CONV_W]]
            G['conv_b_pw1'], G['conv_b_dw'], G['conv_ln_g'], G['conv_ln_b'], G['conv_b_pw2'] = [dbpw1[0]], [dbdw[0]], [dlng[0]], [dlnb[0]], [dysum[0]]
        else:
            dp, dypre, dscale, dpb, dg1, dgt_m = _pool_bwd1(f"pool_bwd1_{i}", dx, st['y'], g[1], gt_m, st['ypre'], row(Ws['pool_scale'][j]), w_pool[j])
            dx, dsh_m, dsc_m, dg0 = _pool_bwd2(f"pool_bwd2_{i}", dp, st['x0'], dx, g[0], sc_m)
            G['pool_w'] = [_mm_tn(f"pool_dw{i}", st['p'], dypre, diag=len(POOL_WINDOWS))]
            G['pool_b'] = [dpb.reshape(len(POOL_WINDOWS), -1)]
            G['pool_scale'] = [dscale[0]]
        dmod[i] = jnp.concatenate([dsh_m, dsc_m, dgt_m, dsh_f, dsc_f, dgt_f], axis=1)
        dnorm[i] = jnp.concatenate([dg0, dg1, dg2, dg3], axis=0)
    G['norm_g'] = dnorm
    grad_x = dx.reshape(x.shape)

    rs_names = [n for n, ax in WEIGHTS if ax is not None and n != 'ada_w' and not n.startswith('ffn')]
    pieces = [(n, _layer_shards(g, SHARD_AXIS[n] - 1)) for n in rs_names for g in G[n]]
    big = [(n, p) for n, p in pieces if p.shape[1] % (8 * 1024) == 0]
    small = [(n, p) for n, p in pieces if p.shape[1] % (8 * 1024) != 0]
    packed = jnp.concatenate([p.reshape(N_DEV, -1, 1024) for _, p in big] + [_pack([p for _, p in small], F32, 8)], axis=1)
    ffn_red = _rs_finish(rs_pending, dx, me, ffn_red)
    my_chip = 2 * xi + yi
    p4 = packed.reshape((4, 2) + packed.shape[1:])
    pair_recv, = _rs_pair("rs_pair", [p4])
    chip_wire, chip_own = _pair_sum("rs_pair_sum", p4, pair_recv, ci, my_chip)

    dmod_mine = jnp.concatenate(dmod, axis=1).reshape(-1)
    fin_in = _pack([dmod_mine] + [G[n][0].reshape(-1) for n in REPL] + [loss_row.reshape(-1)], F32, 8)
    fin_all = _ag_small("ag_final", fin_in)
    chip_wire, fin_all = lax.optimization_barrier((chip_wire, fin_all))
    chip_sems, chip_thru, chip_land, chip_token = _copies_start("rs_chips_start", [chip_wire], [lax.empty(chip_wire.shape, MM)],
                                                                CHIP_PEERS, 'chip')
    grads = {'ffn_w1': ffn_red[0], 'ffn_w2': ffn_red[1]}
    fin_sum = _sum_devices("final_sum", fin_all).reshape(-1)
    nm = L * 6 * D
    grads['ada_b'] = fin_sum[:nm].reshape(L, 6 * D)
    off = nm
    for n in REPL:
        grads[n] = fin_sum[off:off + W[n].size].reshape(W[n].shape)
        off += W[n].size
    loss = fin_sum[off]
    dmod_all = fin_all.reshape(N_DEV, -1)[:, :nm].reshape(N_DEV, L, 6 * D)
    dmod_cols = lax.dynamic_slice_in_dim(dmod_all, me * n_ada, n_ada, axis=2)
    dmod16 = jnp.pad(jnp.transpose(dmod_cols, (1, 0, 2)), ((0, 0), (0, 16 - N_DEV), (0, 0)))
    grads['ada_w'] = _ada_w_grad(c16, dmod16)
    deltas, new_m, new_v = {}, {}, {}
    done = chip_token
    for n in ['ffn_w1', 'ffn_w2', 'ada_w', 'ada_b'] + REPL:
        deltas[n], new_m[n], new_v[n] = _adamw("adamw_" + n, W[n], grads[n], M1[n], V2[n], after=done)
        done = deltas[n]
    chip_recv, = _copies_wait("rs_chips_wait", chip_sems, chip_thru, chip_land, done, CHIP_PEERS, 'chip')
    red = _chip_sum("rs_chip_sum", chip_own, chip_recv, my_chip)
    got = {}
    row0 = 0
    for n, p in big:
        rows = p.shape[1] // 1024
        got.setdefault(n, []).append(red[row0:row0 + rows])
        row0 += rows
    tail = red[row0:].reshape(-1)
    off = 0
    for n, p in small:
        got.setdefault(n, []).append(tail[off:off + p.shape[1]])
        off += p.shape[1]
    for n in rs_names:
        grads[n] = jnp.stack([g_.reshape(W[n].shape[1:]) for g_ in got[n]], axis=0)

    for n in rs_names:
        deltas[n], new_m[n], new_v[n] = _adamw("adamw_" + n, W[n], grads[n], M1[n], V2[n])
    names = [n for n, _ in WEIGHTS]
    return (loss, grad_x, *[grads[n] for n in names], *[deltas[n] for n in names], *[new_m[n] for n in names],
            *[new_v[n] for n in names])
```

```python
import math

import jax
import jax.numpy as jnp
from jax import lax
from jax.experimental import pallas as pl
from jax.experimental.pallas import tpu as pltpu

F32 = jnp.float32
MM = jnp.bfloat16
EPS = 1e-6
NEG = -1e30
N_DEV = 8
VMEM_LIMIT = 48 * 1024 * 1024
MESH = pl.DeviceIdType.MESH

D_MODEL = 1024
N_HEADS = 16
HEAD_PAD = 128
QK_NOPE, QK_ROPE, V_HEAD = 64, 32, 64
Q_LORA, KV_LORA = 384, 256
CHUNK = 64
CONV_W = 31
POOL_WINDOWS = (2, 4, 8, 16)
ROPE_THETA = 10000.0
ATT_SCALE = 1.0 / math.sqrt(QK_NOPE + QK_ROPE)

ADAM_LR, ADAM_B1, ADAM_B2, ADAM_EPS, ADAM_WD, ADAM_STEP = 0.001, 0.9, 0.999, 1e-08, 0.01, 10

WEIGHTS = [('ada_w', 2), ('ada_b', None), ('norm_g', 2), ('mla_w_dq', 1), ('mla_q_norm_g', 1), ('mla_w_uq', 2),
           ('mla_w_dkv', 1), ('mla_kv_norm_g', 1), ('mla_w_ukv', 2), ('mla_w_o', 1), ('conv_w_pw1', 2),
           ('conv_b_pw1', None), ('conv_w_dw', 2), ('conv_b_dw', None), ('conv_ln_g', None), ('conv_ln_b', None),
           ('conv_w_pw2', 1), ('conv_b_pw2', None), ('pool_w', 2), ('pool_b', 2), ('pool_scale', 1),
           ('ffn_w1', 2), ('ffn_w2', 1)]
SHARD_AXIS = dict(WEIGHTS)
BIG = ['mla_w_dq', 'mla_w_uq', 'mla_w_dkv', 'mla_w_ukv', 'mla_w_o', 'conv_w_pw1', 'conv_w_pw2', 'pool_w', 'ffn_w1', 'ffn_w2']
SMALL = ['norm_g', 'mla_q_norm_g', 'mla_kv_norm_g', 'conv_w_dw', 'pool_b', 'pool_scale']
REPL = ['conv_b_pw1', 'conv_b_dw', 'conv_ln_g', 'conv_ln_b', 'conv_b_pw2']


def _dot(a, b):
    return jnp.dot(a.astype(MM), b.astype(MM), preferred_element_type=F32)


def _dot_nt(a, b):
    return lax.dot_general(a.astype(MM), b.astype(MM), (((1,), (1,)), ((), ())), preferred_element_type=F32)


def _dot_tn(a, b):
    return lax.dot_general(a.astype(MM), b.astype(MM), (((0,), (0,)), ((), ())), preferred_element_type=F32)


def _sigmoid(x):
    return 1.0 / (1.0 + jnp.exp(-x))


def _rstd(x):
    return lax.rsqrt(jnp.mean(x * x, axis=-1, keepdims=True) + EPS)


def _rms(x, g):
    return x * _rstd(x) * g


def _rms_bwd(x, g, dout):
    r = _rstd(x)
    xn = x * r
    dg = jnp.sum(dout * xn, axis=0, keepdims=True)
    dxn = dout * g
    dx = r * (dxn - xn * jnp.mean(dxn * xn, axis=-1, keepdims=True))
    return dx, dg


def _prenorm_bwd(x, g0, sc, dh):
    r = _rstd(x)
    xn = x * r
    dsh = jnp.sum(dh, axis=0, keepdims=True)
    dsc = jnp.sum(dh * (xn * g0), axis=0, keepdims=True)
    dn = dh * (1.0 + sc)
    dg0 = jnp.sum(dn * xn, axis=0, keepdims=True)
    dxn = dn * g0
    dx = r * (dxn - xn * jnp.mean(dxn * xn, axis=-1, keepdims=True))
    return dx, dsh, dsc, dg0


def _cparams(sem, vmem=VMEM_LIMIT):
    return pltpu.CompilerParams(dimension_semantics=sem, vmem_limit_bytes=vmem)


def _rows(name, body, n_rows, tm, rows, consts, outs, accs=(), scratch=()):
    tm = min(tm, n_rows)
    nblk = n_rows // tm
    nr, nc, no, na = len(rows), len(consts), len(outs), len(accs)
    in_specs, args = [], []
    for a, kind in rows:
        if kind == 'cur':
            im = lambda i: (i, 0)
        elif kind == 'prev':
            im = lambda i: (jnp.maximum(i - 1, 0), 0)
        else:
            im = lambda i: (jnp.minimum(i + 1, nblk - 1), 0)
        in_specs.append(pl.BlockSpec((tm, a.shape[1]), im))
        args.append(a)
    for a in consts:
        in_specs.append(pl.BlockSpec(a.shape, lambda i, nd=a.ndim: (0,) * nd))
        args.append(a)
    out_specs = [pl.BlockSpec((o[0], tm), lambda i: (0, i)) if len(o) == 3 else pl.BlockSpec((tm, o[0]), lambda i: (i, 0)) for o in outs]
    out_specs += [pl.BlockSpec(s, lambda i, nd=len(s): (0,) * nd) for s in accs]
    out_shape = [jax.ShapeDtypeStruct((o[0], n_rows) if len(o) == 3 else (n_rows, o[0]), o[1]) for o in outs]
    out_shape += [jax.ShapeDtypeStruct(s, F32) for s in accs]

    def kern(*refs):
        i = pl.program_id(0)
        rr = refs[:nr]
        cc = refs[nr:nr + nc]
        oo = refs[nr + nc:nr + nc + no]
        aa = refs[nr + nc + no:nr + nc + no + na]
        ss = refs[nr + nc + no + na:]

        @pl.when(i == 0)
        def _():
            for a in aa:
                a[...] = jnp.zeros(a.shape, F32)

        body(i, nblk, rr, cc, oo, aa, ss)

    return pl.pallas_call(kern, grid=(nblk,), in_specs=in_specs, out_specs=out_specs, out_shape=out_shape,
                          scratch_shapes=list(scratch), name=name, compiler_params=_cparams(("arbitrary",)))(*args)


def _place():
    return lax.axis_index("x"), lax.axis_index("y"), lax.axis_index("c")


def _ag_small(name, xs):
    R, C = xs.shape

    def body(x_ref, out_ref, send_sems, recv_sems):
        x, y, c = _place()
        me = 4 * x + 2 * y + c
        out_ref[me] = x_ref[...]
        copies = []
        for k in range(1, N_DEV):
            peer = ((1 - x) if k & 4 else x, (1 - y) if k & 2 else y, (1 - c) if k & 1 else c)
            cp = pltpu.make_async_remote_copy(src_ref=x_ref, dst_ref=out_ref.at[me], send_sem=send_sems.at[k - 1],
                                              recv_sem=recv_sems.at[k - 1], device_id=peer, device_id_type=MESH)
            cp.start()
            copies.append(cp)
        for cp in copies:
            cp.wait()

    return pl.pallas_call(
        body, out_shape=jax.ShapeDtypeStruct((N_DEV, R, C), xs.dtype),
        in_specs=[pl.BlockSpec(memory_space=pltpu.VMEM)], out_specs=pl.BlockSpec(memory_space=pltpu.VMEM),
        scratch_shapes=[pltpu.SemaphoreType.DMA((N_DEV - 1,)), pltpu.SemaphoreType.DMA((N_DEV - 1,))], name=name)(xs)


def _ag_big(name, xs):
    nt = len(xs)

    def body(*refs):
        x_refs, out_refs = refs[:nt], refs[nt:2 * nt]
        send_sems, recv_sems, local_sems = refs[2 * nt:]
        x, y, c = _place()
        me, sibling = (x, y, c), (x, y, 1 - c)
        chips = [(1 - x, y), (x, 1 - y), (1 - x, 1 - y)]

        def copy(t, k, block, to, own=False):
            px, py, pc = block
            rows = out_refs[t].at[4 * px + 2 * py + pc]
            return pltpu.make_async_remote_copy(src_ref=x_refs[t] if own else rows, dst_ref=rows, send_sem=send_sems.at[7 * t + k],
                                                recv_sem=recv_sems.at[7 * t + k], device_id=to, device_id_type=MESH)

        mine = [pltpu.make_async_copy(x_refs[t], out_refs[t].at[4 * x + 2 * y + c], local_sems.at[t]) for t in range(nt)]
        for cp in mine:
            cp.start()
        first = []
        for t in range(nt):
            first.append(copy(t, 0, me, sibling, own=True))
            first += [copy(t, 1 + j, me, (*chip, c), own=True) for j, chip in enumerate(chips)]
        for cp in first:
            cp.start()
        passed = []
        for t in range(nt):
            for j, chip in enumerate(chips):
                copy(t, 1 + j, (*chip, c), me).wait_recv()
                cp = copy(t, 4 + j, (*chip, c), sibling)
                cp.start()
                passed.append(cp)
        for t in range(nt):
            copy(t, 0, sibling, me).wait_recv()
            for j, chip in enumerate(chips):
                copy(t, 4 + j, (*chip, 1 - c), me).wait_recv()
        for cp in first + passed:
            cp.wait_send()
        for cp in mine:
            cp.wait()

    hbm = pl.BlockSpec(memory_space=pl.ANY)
    return pl.pallas_call(
        body, out_shape=[jax.ShapeDtypeStruct((N_DEV,) + t.shape, t.dtype) for t in xs],
        in_specs=[hbm] * nt, out_specs=[hbm] * nt,
        scratch_shapes=[pltpu.SemaphoreType.DMA((7 * nt,)), pltpu.SemaphoreType.DMA((7 * nt,)), pltpu.SemaphoreType.DMA((nt,))],
        name=name)(*xs)


def _rs_pair(name, ps):
    nt = len(ps)

    def body(*refs):
        p_refs, recv_refs = refs[:nt], refs[nt:2 * nt]
        send_sems, recv_sems = refs[2 * nt:]
        x, y, c = _place()
        copies = []
        for t in range(nt):
            for j in range(4):
                cp = pltpu.make_async_remote_copy(src_ref=p_refs[t].at[j, 1 - c], dst_ref=recv_refs[t].at[j], send_sem=send_sems.at[4 * t + j],
                                                  recv_sem=recv_sems.at[4 * t + j], device_id=(x, y, 1 - c), device_id_type=MESH)
                cp.start()
                copies.append(cp)
        for cp in copies:
            cp.wait()

    hbm = pl.BlockSpec(memory_space=pl.ANY)
    return pl.pallas_call(
        body, out_shape=[jax.ShapeDtypeStruct((4,) + p.shape[2:], p.dtype) for p in ps], in_specs=[hbm] * nt, out_specs=[hbm] * nt,
        scratch_shapes=[pltpu.SemaphoreType.DMA((4 * nt,)), pltpu.SemaphoreType.DMA((4 * nt,))], name=name)(*ps)


RS_ROWS = 256


def _row_block(r):
    return next(t for t in range(RS_ROWS, 0, -16) if r % t == 0)


def _pair_sum(name, p, recv, my_c, my_chip):
    _, _, r, c = p.shape
    tr = _row_block(r)

    def body(sc_ref, p_ref, r_ref, o_ref, own_ref):
        s = p_ref[...] + r_ref[...]
        o_ref[...] = s.astype(MM)

        @pl.when(pl.program_id(1) == sc_ref[1])
        def _():
            own_ref[...] = s

    return pl.pallas_call(
        body, grid_spec=pltpu.PrefetchScalarGridSpec(
            num_scalar_prefetch=1, grid=(r // tr, 4),
            in_specs=[pl.BlockSpec((None, None, tr, c), lambda i, j, sc: (j, sc[0], i, 0)),
                      pl.BlockSpec((None, tr, c), lambda i, j, sc: (j, i, 0))],
            out_specs=[pl.BlockSpec((None, tr, c), lambda i, j, sc: (j, i, 0)), pl.BlockSpec((tr, c), lambda i, j, sc: (i, 0))]),
        out_shape=[jax.ShapeDtypeStruct((4, r, c), MM), jax.ShapeDtypeStruct((r, c), F32)], name=name,
        compiler_params=_cparams(("arbitrary", "arbitrary")))(jnp.stack([my_c, my_chip]), p, recv)


def _chip_sum(name, own, recv, my_chip):
    _, r, c = recv.shape
    tr = _row_block(r)

    def body(sc_ref, own_ref, r_ref, o_ref):
        acc = jnp.zeros((tr, c), F32)
        for j in range(4):
            acc = acc + jnp.where(sc_ref[0] == j, own_ref[...], r_ref[j].astype(F32))
        o_ref[...] = acc

    return pl.pallas_call(
        body, grid_spec=pltpu.PrefetchScalarGridSpec(
            num_scalar_prefetch=1, grid=(r // tr,),
            in_specs=[pl.BlockSpec((tr, c), lambda i, sc: (i, 0)), pl.BlockSpec((4, tr, c), lambda i, sc: (0, i, 0))],
            out_specs=pl.BlockSpec((tr, c), lambda i, sc: (i, 0))),
        out_shape=jax.ShapeDtypeStruct((r, c), F32), name=name,
        compiler_params=_cparams(("arbitrary",)))(my_chip.reshape(1), own, recv)


HBM_SPEC = pl.BlockSpec(memory_space=pltpu.HBM)
SEM_SPEC = pl.BlockSpec(memory_space=pltpu.SEMAPHORE)
SPLIT_EFFECT = pltpu.SideEffectType.DATAFLOW_SIDE_EFFECTING
ALL_PEERS = (1, 2, 3, 4, 5, 6, 7)
FIRST_LEVEL_PEERS = (1, 4, 2, 6)
CHIP_PEERS = (4, 2, 6)


def _split_copies(src_refs, land_refs, sems, masks, src_per_peer):
    n, nt = len(masks), len(src_refs)
    x, y, c = _place()
    by_chip = src_per_peer == 'chip'
    slot = 2 * x + y if by_chip else 4 * x + 2 * y + c
    copies = []
    for t in range(nt):
        for k, mask in enumerate(masks):
            px, py, pc = (1 - x) if mask & 4 else x, (1 - y) if mask & 2 else y, (1 - c) if mask & 1 else c
            src = src_refs[t].at[2 * px + py if by_chip else 4 * px + 2 * py + pc] if src_per_peer else src_refs[t]
            copies.append(pltpu.make_async_remote_copy(src_ref=src, dst_ref=land_refs[t].at[slot], send_sem=sems[t * n + k],
                                                       recv_sem=sems[nt * n + t * n + k], device_id=(px, py, pc), device_id_type=MESH))
    return copies


def _copies_start(name, srcs, lands, masks, src_per_peer):
    nt, ns = len(srcs), 2 * len(masks) * len(srcs)

    def body(*refs):
        for cp in _split_copies(refs[:nt], refs[nt:2 * nt], refs[2 * nt:2 * nt + ns], masks, src_per_peer):
            cp.start()
        token = refs[-1]
        token[...] = jnp.zeros(token.shape, F32)

    outs = pl.pallas_call(
        body, name=name,
        out_shape=(pltpu.SemaphoreType.DMA(()),) * ns + tuple(pltpu.HBM(a.shape, a.dtype) for a in list(srcs) + list(lands))
        + (jax.ShapeDtypeStruct((8, 128), F32),),
        in_specs=(HBM_SPEC,) * (2 * nt), out_specs=(SEM_SPEC,) * ns + (HBM_SPEC,) * (2 * nt) + (pl.BlockSpec(memory_space=pltpu.VMEM),),
        input_output_aliases={t: ns + t for t in range(2 * nt)}, compiler_params=pltpu.CompilerParams(has_side_effects=SPLIT_EFFECT))(
            *[pltpu.with_memory_space_constraint(a, pltpu.HBM) for a in list(srcs) + list(lands)])
    return outs[:ns], outs[ns:ns + nt], outs[ns + nt:ns + 2 * nt], outs[-1]


def _copies_wait(name, sems, srcs_thru, lands_thru, after, masks, src_per_peer):
    nt, ns = len(srcs_thru), len(sems)

    def body(*refs):
        for cp in _split_copies(refs[:nt], refs[nt:2 * nt], refs[2 * nt:2 * nt + ns], masks, src_per_peer):
            cp.wait_send()
            cp.wait_recv()

    thru = list(srcs_thru) + list(lands_thru)
    return pl.pallas_call(
        body, name=name, out_shape=tuple(pltpu.HBM(a.shape, a.dtype) for a in thru),
        in_specs=(HBM_SPEC,) * (2 * nt) + (SEM_SPEC,) * ns + (pl.BlockSpec(memory_space=pl.ANY),), out_specs=(HBM_SPEC,) * (2 * nt),
        input_output_aliases={t: t for t in range(2 * nt)}, compiler_params=pltpu.CompilerParams(has_side_effects=SPLIT_EFFECT))(
            *thru, *sems, after)[nt:]


def _ag_forward(name, gs):
    nt = len(gs)

    def body(*refs):
        o_refs, send_sems, recv_sems = refs[nt:2 * nt], refs[2 * nt], refs[2 * nt + 1]
        x, y, c = _place()
        chips = [(1 - x, y), (x, 1 - y), (1 - x, 1 - y)]

        def copy(t, j, pc):
            rows = o_refs[t].at[4 * chips[j][0] + 2 * chips[j][1] + pc]
            return pltpu.make_async_remote_copy(src_ref=rows, dst_ref=rows, send_sem=send_sems.at[3 * t + j], recv_sem=recv_sems.at[3 * t + j],
                                                device_id=(x, y, 1 - c), device_id_type=MESH)

        for t in range(nt):
            for j in range(3):
                copy(t, j, c).start()
        for t in range(nt):
            for j in range(3):
                copy(t, j, c).wait_send()
                copy(t, j, 1 - c).wait_recv()

    hbm = pl.BlockSpec(memory_space=pl.ANY)
    return pl.pallas_call(body, out_shape=[jax.ShapeDtypeStruct(g.shape, g.dtype) for g in gs], in_specs=[hbm] * nt, out_specs=[hbm] * nt,
                          scratch_shapes=[pltpu.SemaphoreType.DMA((3 * nt,)), pltpu.SemaphoreType.DMA((3 * nt,))],
                          input_output_aliases={t: t for t in range(nt)}, name=name)(*gs)


def _mm_tn_wire(name, a, b, me, sqrelu, shard_rows):
    T, M = a.shape
    N = b.shape[1]
    tk = min(2048, T)
    nk = T // tk
    if shard_rows:
        bm, bn = M // N_DEV, N
        a_spec = pl.BlockSpec((tk, 2 * bm), lambda j, k, m: (k, j))
        b_spec = pl.BlockSpec((tk, bn), lambda j, k, m: (k, 0))
        halves = (slice(0, bm), slice(None)), (slice(bm, 2 * bm), slice(None))
        acc_shape = (2 * bm, bn)
    else:
        bm, bn = M, N // N_DEV
        a_spec = pl.BlockSpec((tk, bm), lambda j, k, m: (k, 0))
        b_spec = pl.BlockSpec((tk, 2 * bn), lambda j, k, m: (k, j))
        halves = (slice(None), slice(0, bn)), (slice(None), slice(bn, 2 * bn))
        acc_shape = (bm, 2 * bn)

    def body(me_ref, a_ref, b_ref, wire_ref, own_ref, acc):
        j, k = pl.program_id(0), pl.program_id(1)

        @pl.when(k == 0)
        def _():
            acc[...] = jnp.zeros(acc.shape, F32)

        av = a_ref[...]
        if sqrelu:
            r = jnp.maximum(av, 0.0)
            av = r * r
        acc[...] += _dot_tn(av, b_ref[...])

        for hh in range(2):
            @pl.when(k == nk - 1)
            def _():
                wire_ref[hh] = acc[halves[hh]].astype(MM)

            @pl.when((k == nk - 1) & (2 * j + hh == me_ref[0]))
            def _():
                own_ref[...] = acc[halves[hh]]

    return pl.pallas_call(
        body, grid_spec=pltpu.PrefetchScalarGridSpec(
            num_scalar_prefetch=1, grid=(N_DEV // 2, nk), in_specs=[a_spec, b_spec],
            out_specs=[pl.BlockSpec((2, bm, bn), lambda j, k, m: (j, 0, 0)), pl.BlockSpec((bm, bn), lambda j, k, m: (0, 0))],
            scratch_shapes=[pltpu.VMEM(acc_shape, F32)]),
        out_shape=[jax.ShapeDtypeStruct((N_DEV, bm, bn), MM), jax.ShapeDtypeStruct((bm, bn), F32)], name=name,
        compiler_params=_cparams(("arbitrary", "arbitrary")))(me.reshape(1), a, b)


def _rs_final(name, own, recv, me, stack, li):
    _, r, c = recv.shape
    tr = RS_ROWS

    def body(me_ref, own_ref, r_ref, s_ref, o_ref):
        acc = jnp.zeros((tr, c), F32)
        for j in range(N_DEV):
            acc = acc + jnp.where(me_ref[0] == j, own_ref[...], r_ref[j].astype(F32))
        o_ref[...] = acc

    return pl.pallas_call(
        body, grid_spec=pltpu.PrefetchScalarGridSpec(
            num_scalar_prefetch=1, grid=(r // tr,),
            in_specs=[pl.BlockSpec((tr, c), lambda i, m: (i, 0)), pl.BlockSpec((N_DEV, tr, c), lambda i, m: (0, i, 0)),
                      pl.BlockSpec(memory_space=pl.ANY)],
            out_specs=pl.BlockSpec((None, tr, c), lambda i, m: (li, i, 0))),
        out_shape=jax.ShapeDtypeStruct(stack.shape, F32), input_output_aliases={3: 0}, name=name,
        compiler_params=_cparams(("arbitrary",)))(me.reshape(1), own, recv, stack)


def _rs_finish(pending, after, me, stacks):
    i, sems, wires_thru, lands, owns = pending
    recvs = _copies_wait(f"rs_wait{i}", sems, wires_thru, lands, after, ALL_PEERS, True)
    return [_rs_final(f"rs_final{i}_{t}", owns[t], recvs[t], me, stacks[t], i) for t in range(len(owns))]


def _mod_part(c16, ada_w, ada_b_cols):
    L, D, n = ada_w.shape

    def body(c_ref, w_ref, b_ref, o_ref):
        cv = c_ref[...]
        o_ref[...] = _dot(cv * _sigmoid(cv), w_ref[...]) + b_ref[...]

    return pl.pallas_call(
        body, grid=(L,), in_specs=[pl.BlockSpec((16, D), lambda i: (0, 0)), pl.BlockSpec((None, D, n), lambda i: (i, 0, 0)),
                                   pl.BlockSpec((None, 1, n), lambda i: (i, 0, 0))],
        out_specs=pl.BlockSpec((None, 16, n), lambda i: (i, 0, 0)), out_shape=jax.ShapeDtypeStruct((L, 16, n), F32),
        name="ada_mod", compiler_params=_cparams(("arbitrary",)))(c16, ada_w, ada_b_cols)


def _ada_w_grad(c16, dmod16):
    L, _, n = dmod16.shape
    D = c16.shape[1]

    def body(c_ref, d_ref, o_ref):
        cv = c_ref[...]
        o_ref[...] = _dot_tn(cv * _sigmoid(cv), d_ref[...])

    return pl.pallas_call(
        body, grid=(L,), in_specs=[pl.BlockSpec((16, D), lambda i: (0, 0)), pl.BlockSpec((None, 16, n), lambda i: (i, 0, 0))],
        out_specs=pl.BlockSpec((None, D, n), lambda i: (i, 0, 0)), out_shape=jax.ShapeDtypeStruct((L, D, n), F32),
        name="ada_w_grad", compiler_params=_cparams(("arbitrary",)))(c16, dmod16)


def _sum_devices(name, g):
    _, R, C = g.shape

    def body(g_ref, o_ref):
        acc = g_ref[0]
        for d in range(1, N_DEV):
            acc = acc + g_ref[d]
        o_ref[...] = acc

    return pl.pallas_call(body, out_shape=jax.ShapeDtypeStruct((R, C), F32), name=name)(g)


def _prenorm(name, x, g0, sc, sh, dtype):
    T, D = x.shape

    def body(i, n, rr, cc, oo, aa, ss):
        oo[0][...] = (_rms(rr[0][...], cc[0][...]) * (1.0 + cc[1][...]) + cc[2][...]).astype(dtype)

    return _rows(name, body, T, 512, [(x, 'cur')], [g0, sc, sh], [(D, dtype)])[0]


def _post_bwd_math(d, yv, g1v, gtv):
    dgt = jnp.sum(d * _rms(yv, g1v), axis=0, keepdims=True)
    dy, dg1 = _rms_bwd(yv, g1v, d * gtv)
    return dy, dg1, dgt


def _post_bwd_nt(name, dxo, y, g1, gt, w, o):
    T, D = y.shape
    K = w.shape[0]

    def body(i, n, rr, cc, oo, aa, ss):
        dy, dg1, dgt = _post_bwd_math(rr[0][...], rr[1][...], cc[0][...], cc[1][...])
        aa[0][...] += dg1
        aa[1][...] += dgt
        dy = dy.astype(MM)
        oo[0][...] = dy
        do = _dot_nt(dy, cc[2][...]).astype(MM)
        oo[1][...] = do
        tm = do.shape[0]
        lane = lax.broadcasted_iota(jnp.int32, (tm, HEAD_PAD), 1) // 8
        cols = jnp.zeros((tm, HEAD_PAD), F32)
        for h in range(N_HEADS):
            hsl = slice(h * HEAD_PAD, (h + 1) * HEAD_PAD)
            r = jnp.sum(do[:, hsl].astype(F32) * rr[2][:, hsl].astype(F32), axis=1, keepdims=True)
            cols = jnp.where(lane == h, r, cols)
        oo[2][...] = cols.T

    dy, do, delta, dg1, dgt = _rows(name, body, T, 512, [(dxo, 'cur'), (y, 'cur'), (o, 'cur')], [g1, gt, w],
                                    [(D, MM), (K, MM), (HEAD_PAD, F32, 'T')], accs=[(1, D)] * 2)
    return dy, do, delta.reshape(N_HEADS, 8, T), dg1, dgt


def _mm_post(name, a, w, bias, x, g1, gt):
    T, D = x.shape
    consts = [w, g1, gt] + ([bias] if bias is not None else [])

    def body(i, n, rr, cc, oo, aa, ss):
        y = _dot(rr[0][...], cc[0][...])
        if bias is not None:
            y = y + cc[3][...]
        oo[0][...] = y
        oo[1][...] = rr[1][...] + cc[2][...] * _rms(y, cc[1][...])

    return _rows(name, body, T, 512, [(a, 'cur'), (x, 'cur')], consts, [(D, F32), (D, F32)])


def _mm_tn(name, a, b, sqrelu=False, col_shards=0, diag=0):
    T, M = a.shape
    N = b.shape[1]
    tk = min(512, T)
    nk = T // tk
    if diag:
        bm, bn = M // diag, N // diag
        grid = (diag, 1, nk)
        a_spec = pl.BlockSpec((tk, bm), lambda g, n, k: (k, g))
        b_spec = pl.BlockSpec((tk, bn), lambda g, n, k: (k, g))
        o_spec = pl.BlockSpec((None, bm, bn), lambda g, n, k: (g, 0, 0))
        o_shape = (diag, bm, bn)
    else:
        bm = min(M, 1024)
        bn = N // col_shards if col_shards else min(N, 1024)
        grid = (M // bm, N // bn, nk)
        a_spec = pl.BlockSpec((tk, bm), lambda m, n, k: (k, m))
        b_spec = pl.BlockSpec((tk, bn), lambda m, n, k: (k, n))
        if col_shards:
            o_spec = pl.BlockSpec((None, bm, bn), lambda m, n, k: (n, m, 0))
            o_shape = (col_shards, M, bn)
        else:
            o_spec = pl.BlockSpec((bm, bn), lambda m, n, k: (m, n))
            o_shape = (M, N)

    def body(a_ref, b_ref, o_ref):
        @pl.when(pl.program_id(2) == 0)
        def _():
            o_ref[...] = jnp.zeros(o_ref.shape, F32)

        av = a_ref[...]
        if sqrelu:
            r = jnp.maximum(av, 0.0)
            av = r * r
        o_ref[...] += _dot_tn(av, b_ref[...])

    return pl.pallas_call(body, grid=grid, in_specs=[a_spec, b_spec], out_specs=o_spec,
                          out_shape=jax.ShapeDtypeStruct(o_shape, F32), name=name,
                          compiler_params=_cparams(("arbitrary", "arbitrary", "arbitrary")))(a, b)


FFN_SHARDS = 4
FFN_BWD_SHARDS = 4
FFN_BWD_VMEM = 56 * 1024 * 1024

def _ffn_fwd(name, li, x, g0, sc, sh, w1g, w2g, g1, gt):
    T, D = x.shape
    nf, tf = w1g.shape[0], w1g.shape[-1]
    F = nf * tf
    tm = min(512, T)

    def body(x_ref, g0_ref, sc_ref, sh_ref, w1_ref, w2_ref, g1_ref, gt_ref, h_ref, a_ref, y_ref, xo_ref, acc):
        f = pl.program_id(1)

        @pl.when(f == 0)
        def _():
            acc[...] = jnp.zeros(acc.shape, F32)
            h_ref[...] = (_rms(x_ref[...], g0_ref[...]) * (1.0 + sc_ref[...]) + sh_ref[...]).astype(MM)

        hv = h_ref[...]
        part = None
        for hh in range(FFN_SHARDS):
            a = _dot(hv, w1_ref[hh])
            a_ref[:, hh * tf:(hh + 1) * tf] = a.astype(MM)
            r = jnp.maximum(a, 0.0)
            p = _dot(r * r, w2_ref[hh])
            part = p if part is None else part + p
        acc[...] += part

        @pl.when(f == nf // FFN_SHARDS - 1)
        def _():
            y = acc[...]
            y_ref[...] = y
            xo_ref[...] = x_ref[...] + gt_ref[...] * _rms(y, g1_ref[...])

    row = lambda t, f: (t, 0)
    one = lambda t, f: (0, 0)
    return pl.pallas_call(
        body, grid=(T // tm, nf // FFN_SHARDS),
        in_specs=[pl.BlockSpec((tm, D), row)] + [pl.BlockSpec((1, D), one)] * 3
        + [pl.BlockSpec((FFN_SHARDS, None, D, tf), lambda t, f: (f, li, 0, 0)), pl.BlockSpec((FFN_SHARDS, None, tf, D), lambda t, f: (f, li, 0, 0)),
           pl.BlockSpec((1, D), one), pl.BlockSpec((1, D), one)],
        out_specs=[pl.BlockSpec((tm, D), row), pl.BlockSpec((tm, FFN_SHARDS * tf), lambda t, f: (t, f)), pl.BlockSpec((tm, D), row),
                   pl.BlockSpec((tm, D), row)],
        out_shape=[jax.ShapeDtypeStruct((T, D), MM), jax.ShapeDtypeStruct((T, F), MM), jax.ShapeDtypeStruct((T, D), F32),
                   jax.ShapeDtypeStruct((T, D), F32)],
        scratch_shapes=[pltpu.VMEM((tm, D), F32)], name=name,
        compiler_params=_cparams(("arbitrary", "arbitrary")))(x, g0, sc, sh, w1g, w2g, g1, gt)


def _ffn_bwd(name, li, y, g1, gt, a, w1g, w2g, x, dxo, g0, sc):
    T, D = x.shape
    nf, tf = w1g.shape[0], w1g.shape[-1]
    F = nf * tf
    tm = min(512, T)
    ns = FFN_BWD_SHARDS

    def body(y_ref, g1_ref, gt_ref, a_ref, w1_ref, w2_ref, x_ref, dxo_ref, g0_ref, sc_ref,
             dy_ref, da_ref, dx_ref, dg1_ref, dgt_ref, dsh_ref, dsc_ref, dg0_ref, acc):
        t, f = pl.program_id(0), pl.program_id(1)

        @pl.when((t == 0) & (f == 0))
        def _():
            for r in (dg1_ref, dgt_ref, dsh_ref, dsc_ref, dg0_ref):
                r[...] = jnp.zeros(r.shape, F32)

        @pl.when(f == 0)
        def _():
            acc[...] = jnp.zeros(acc.shape, F32)
            d, yv, g1v = dxo_ref[...], y_ref[...], g1_ref[...]
            dgt_ref[...] += jnp.sum(d * _rms(yv, g1v), axis=0, keepdims=True)
            dyf, dg1 = _rms_bwd(yv, g1v, d * gt_ref[...])
            dg1_ref[...] += dg1
            dy_ref[...] = dyf.astype(MM)

        dyv = dy_ref[...]
        dyv = dyv + dyv
        part = None
        for hh in range(ns):
            cols = slice(hh * tf, (hh + 1) * tf)
            du = _dot_nt(dyv, w2_ref[hh])
            da = (du * jnp.maximum(a_ref[:, cols], 0.0).astype(F32)).astype(MM)
            da_ref[:, cols] = da
            p = _dot_nt(da, w1_ref[hh])
            part = p if part is None else part + p
        acc[...] += part

        @pl.when(f == nf // ns - 1)
        def _():
            dx, dsh, dsc, dg0 = _prenorm_bwd(x_ref[...], g0_ref[...], sc_ref[...], acc[...])
            dx_ref[...] = dxo_ref[...] + dx
            dsh_ref[...] += dsh
            dsc_ref[...] += dsc
            dg0_ref[...] += dg0

    row = lambda t, f: (t, 0)
    one = lambda t, f: (0, 0)
    blk = lambda t, f: (t, f)
    return pl.pallas_call(
        body, grid=(T // tm, nf // ns),
        in_specs=[pl.BlockSpec((tm, D), row), pl.BlockSpec((1, D), one), pl.BlockSpec((1, D), one), pl.BlockSpec((tm, ns * tf), blk),
                  pl.BlockSpec((ns, None, D, tf), lambda t, f: (f, li, 0, 0)),
                  pl.BlockSpec((ns, None, tf, D), lambda t, f: (f, li, 0, 0)), pl.BlockSpec((tm, D), row), pl.BlockSpec((tm, D), row),
                  pl.BlockSpec((1, D), one), pl.BlockSpec((1, D), one)],
        out_specs=[pl.BlockSpec((tm, D), row), pl.BlockSpec((tm, ns * tf), blk), pl.BlockSpec((tm, D), row)] + [pl.BlockSpec((1, D), one)] * 5,
        out_shape=[jax.ShapeDtypeStruct((T, D), MM), jax.ShapeDtypeStruct((T, F), MM), jax.ShapeDtypeStruct((T, D), F32)]
        + [jax.ShapeDtypeStruct((1, D), F32)] * 5,
        scratch_shapes=[pltpu.VMEM((tm, D), F32)], name=name,
        compiler_params=_cparams(("arbitrary", "arbitrary"), FFN_BWD_VMEM))(y, g1, gt, a, w1g, w2g, x, dxo, g0, sc)


def _rope_tables(pos, invf):
    T = pos.shape[0]

    def body(i, n, rr, cc, oo, aa, ss):
        ang = rr[0][...] * cc[0][...]
        lane = lax.broadcasted_iota(jnp.int32, ang.shape, 1)
        cs, sn = jnp.cos(ang), jnp.sin(ang)
        oo[0][...] = jnp.where((lane >= QK_NOPE) & (lane < QK_NOPE + QK_ROPE), cs, 1.0)
        oo[1][...] = jnp.where((lane >= QK_NOPE) & (lane < QK_NOPE + QK_ROPE // 2), -sn, 0.0)
        oo[2][...] = jnp.where((lane >= QK_NOPE + QK_ROPE // 2) & (lane < QK_NOPE + QK_ROPE), sn, 0.0)

    return _rows("rope_tables", body, T, 512, [(pos, 'cur')], [invf], [(HEAD_PAD, F32)] * 3)


def _rope(v, C, S1, S2):
    n = v.shape[1]
    reps = n // HEAD_PAD
    if reps > 1:
        C, S1, S2 = (jnp.tile(t, (1, reps)) for t in (C, S1, S2))
    return v * C + pltpu.roll(v, n - QK_ROPE // 2, 1) * S1 + pltpu.roll(v, QK_ROPE // 2, 1) * S2


def _unrope(d, C, S1, S2):
    n = d.shape[1]
    reps = n // HEAD_PAD
    if reps > 1:
        C, S1, S2 = (jnp.tile(t, (1, reps)) for t in (C, S1, S2))
    return d * C + pltpu.roll(d * S1, QK_ROPE // 2, 1) + pltpu.roll(d * S2, n - QK_ROPE // 2, 1)


MLA_ROWS = 512
MLA_BWD_ROWS = 512


def _mla_proj(name, x, g0, sc, sh, C, S1, S2, w_dq, qg, w_uq, w_dkv, kvg, w_ukv_k, w_ukv_v):
    T, D = x.shape
    HP = N_HEADS * HEAD_PAD

    def body(i, n, rr, cc, oo, aa, ss):
        hv = (_rms(rr[0][...], cc[7][...]) * (1.0 + cc[8][...]) + cc[9][...]).astype(MM)
        oo[7][...] = hv
        Cv, S1v, S2v = rr[1][...], rr[2][...], rr[3][...]
        cq_raw = _dot(hv, cc[0][...])
        cq = _rms(cq_raw, cc[1][...]).astype(MM)
        q = _rope(_dot(cq, cc[2][...]), Cv, S1v, S2v)
        ckv_all = _dot(hv, cc[3][...])
        ckv_raw = ckv_all[:, :KV_LORA]
        ckv = _rms(ckv_raw, cc[4][...]).astype(MM)
        kr = _rope(ckv_all[:, KV_LORA:], Cv, S1v, S2v)
        k = _dot(ckv, cc[5][...]) + jnp.tile(kr, (1, N_HEADS))
        v = _dot(ckv, cc[6][...])
        v = jnp.where(lax.broadcasted_iota(jnp.int32, v.shape, 1) % HEAD_PAD == V_HEAD, 1.0, v)
        oo[0][...] = cq_raw
        oo[1][...] = cq
        oo[2][...] = ckv_raw
        oo[3][...] = ckv
        oo[4][...] = q.astype(MM)
        oo[5][...] = k.astype(MM)
        oo[6][...] = v.astype(MM)

    return _rows(name, body, T, MLA_ROWS, [(x, 'cur'), (C, 'cur'), (S1, 'cur'), (S2, 'cur')],
                 [w_dq, qg, w_uq, w_dkv, kvg, w_ukv_k, w_ukv_v, g0, sc, sh],
                 [(Q_LORA, F32), (Q_LORA, MM), (KV_LORA, F32), (KV_LORA, MM), (HP, MM), (HP, MM), (HP, MM), (D, MM)])


ATT_HEADS = 4
ATT_BLOCK = 512
ATT_FWD_BLOCK = 1024


def _chunk_mask_t(tk, tq):
    ki = lax.broadcasted_iota(jnp.int32, (tk, tq), 0) // CHUNK
    qi = lax.broadcasted_iota(jnp.int32, (tk, tq), 1) // CHUNK
    return ki <= qi


def _attn_fwd(name, q, k, v):
    T = q.shape[0]
    tb = min(ATT_FWD_BLOCK, T)
    nb = T // tb
    nh = ATT_HEADS
    hs = [slice(h * HEAD_PAD, (h + 1) * HEAD_PAD) for h in range(nh)]

    def body(q_ref, k_ref, v_ref, o_ref, lse_ref):
        qb = pl.program_id(1)

        def k_block(k0, masked, st):
            new = []
            for h in range(nh):
                m, acc = st[h]
                s = _dot_nt(k_ref[pl.ds(k0, tb), hs[h]], q_ref[:, hs[h]])
                if masked:
                    s = jnp.where(_chunk_mask_t(tb, tb), s, NEG)
                m_new = jnp.maximum(m, jnp.max(s, axis=0, keepdims=True))
                alpha = jnp.exp((m - m_new) * ATT_SCALE)
                p = jnp.exp((s - m_new) * ATT_SCALE)
                acc = alpha * acc + _dot_tn(v_ref[pl.ds(k0, tb), hs[h]], p)
                new.append((m_new, acc))
            return tuple(new)

        st = tuple((jnp.full((1, tb), NEG, F32), jnp.zeros((HEAD_PAD, tb), F32)) for _ in range(nh))
        st = k_block(pl.multiple_of(qb * tb, tb), True, st)
        st = lax.fori_loop(0, qb, lambda kb, s_: k_block(pl.multiple_of(kb * tb, tb), False, s_), st)
        for h in range(nh):
            m, acc = st[h]
            l = acc[V_HEAD:V_HEAD + 1, :]
            o_ref[:, hs[h]] = (acc / l).T.astype(MM)
            lse_ref[h] = jnp.broadcast_to(m * ATT_SCALE + jnp.log(l), (8, tb))

    blk = pl.BlockSpec((tb, nh * HEAD_PAD), lambda g, i: (i, g))
    res = pl.BlockSpec((T, nh * HEAD_PAD), lambda g, i: (0, g))
    return pl.pallas_call(
        body, grid=(N_HEADS // nh, nb), in_specs=[blk, res, res],
        out_specs=[blk, pl.BlockSpec((nh, 8, tb), lambda g, i: (g, 0, i))],
        out_shape=[jax.ShapeDtypeStruct(q.shape, MM), jax.ShapeDtypeStruct((N_HEADS, 8, T), F32)], name=name,
        compiler_params=_cparams(("arbitrary", "arbitrary")))(q, k, v)


def _attn_bwd(name, q, k, v, do, lse, delta):
    T = q.shape[0]
    tb = min(ATT_BLOCK, T)
    nb = T // tb
    nh = ATT_HEADS
    hs = [slice(h * HEAD_PAD, (h + 1) * HEAD_PAD) for h in range(nh)]

    def body(q_ref, k_ref, v_ref, do_ref, lse_ref, dl_ref, dq_ref, dk_ref, dv_ref, dq_acc, dk_acc, dv_acc):
        kb = pl.program_id(1)

        @pl.when(kb == 0)
        def _():
            dq_acc[...] = jnp.zeros(dq_acc.shape, F32)

        dk_acc[...] = jnp.zeros(dk_acc.shape, F32)
        dv_acc[...] = jnp.zeros(dv_acc.shape, F32)

        def q_block(q0, masked):
            for h in range(nh):
                qh = q_ref[pl.ds(q0, tb), hs[h]]
                doh = do_ref[pl.ds(q0, tb), hs[h]]
                kh = k_ref[:, hs[h]]
                s = _dot_nt(kh, qh) * ATT_SCALE
                if masked:
                    s = jnp.where(_chunk_mask_t(tb, tb), s, NEG)
                p = jnp.exp(s - lse_ref[h, 0:1, pl.ds(q0, tb)])
                ds = (p * (_dot_nt(v_ref[:, hs[h]], doh) - dl_ref[h, 0:1, pl.ds(q0, tb)]) * ATT_SCALE).astype(MM)
                dv_acc[:, hs[h]] += _dot(p, doh)
                dk_acc[:, hs[h]] += _dot(ds, qh)
                dq_acc[pl.ds(q0, tb), hs[h]] += _dot_tn(ds, kh)

        q_block(pl.multiple_of(kb * tb, tb), True)

        def rest(qb, c_):
            q_block(pl.multiple_of(qb * tb, tb), False)
            return c_

        lax.fori_loop(kb + 1, nb, rest, 0)
        dk_ref[...] = dk_acc[...].astype(MM)
        dv_ref[...] = dv_acc[...].astype(MM)

        @pl.when(kb == nb - 1)
        def _():
            dq_ref[...] = dq_acc[...].astype(MM)

    W = nh * HEAD_PAD
    blk = pl.BlockSpec((tb, W), lambda g, i: (i, g))
    res = pl.BlockSpec((T, W), lambda g, i: (0, g))
    rows = pl.BlockSpec((nh, 8, T), lambda g, i: (g, 0, 0))
    return pl.pallas_call(
        body, grid=(N_HEADS // nh, nb), in_specs=[res, blk, blk, res, rows, rows], out_specs=[res, blk, blk],
        out_shape=[jax.ShapeDtypeStruct(q.shape, MM)] * 3,
        scratch_shapes=[pltpu.VMEM((T, W), F32), pltpu.VMEM((tb, W), F32), pltpu.VMEM((tb, W), F32)],
        name=name, compiler_params=_cparams(("arbitrary", "arbitrary")))(q, k, v, do, lse, delta)


def _mla_proj_bwd(name, dq, dk, dv, C, S1, S2, cq_raw, ckv_raw, x, dxo, w_uq, w_ukv_k, w_ukv_v, w_dq, w_dkv, qg, kvg, g0, sc):
    T, D = x.shape
    HP = N_HEADS * HEAD_PAD

    def body(i, n, rr, cc, oo, aa, ss):
        Cv, S1v, S2v = rr[3][...], rr[4][...], rr[5][...]
        dq_pre = _unrope(rr[0][...].astype(F32), Cv, S1v, S2v).astype(MM)
        oo[0][...] = dq_pre
        dcq = _dot_nt(dq_pre, cc[0][...])
        dcq_raw, dqg = _rms_bwd(rr[6][...], cc[5][...], dcq)
        aa[0][...] += dqg
        dcq_raw = dcq_raw.astype(MM)
        oo[1][...] = dcq_raw
        dkv = rr[1][...]
        dkr = dkv[:, :HEAD_PAD].astype(F32)
        for hh in range(1, N_HEADS):
            dkr = dkr + dkv[:, hh * HEAD_PAD:(hh + 1) * HEAD_PAD].astype(F32)
        lane = lax.broadcasted_iota(jnp.int32, dkr.shape, 1)
        dkr = jnp.where((lane >= QK_NOPE) & (lane < QK_NOPE + QK_ROPE), _unrope(dkr, Cv, S1v, S2v), 0.0)
        dckv = _dot_nt(dkv, cc[1][...]) + _dot_nt(rr[2][...], cc[2][...])
        dckv_raw, dkvg = _rms_bwd(rr[7][...], cc[6][...], dckv)
        aa[1][...] += dkvg
        dckv_all = jnp.concatenate([dckv_raw, dkr], axis=1).astype(MM)
        oo[2][...] = dckv_all
        dh = _dot_nt(dcq_raw, cc[3][...]) + _dot_nt(dckv_all, cc[4][...])
        dx, dsh, dsc, dg0 = _prenorm_bwd(rr[8][...], cc[7][...], cc[8][...], dh)
        oo[3][...] = rr[9][...] + dx
        aa[2][...] += dsh
        aa[3][...] += dsc
        aa[4][...] += dg0

    return _rows(name, body, T, MLA_BWD_ROWS,
                 [(dq, 'cur'), (dk, 'cur'), (dv, 'cur'), (C, 'cur'), (S1, 'cur'), (S2, 'cur'), (cq_raw, 'cur'), (ckv_raw, 'cur'),
                  (x, 'cur'), (dxo, 'cur')],
                 [w_uq, w_ukv_k, w_ukv_v, w_dq, w_dkv, qg, kvg, g0, sc],
                 [(HP, MM), (Q_LORA, MM), (KV_LORA + HEAD_PAD, MM), (D, F32)],
                 accs=[(1, Q_LORA), (1, KV_LORA), (1, D), (1, D), (1, D)])


HALO = 32


def _windows(ext, tm, first):
    rolled = {0: ext}
    out = []
    for j in range(CONV_W):
        r = (first + j) % 8
        if r not in rolled:
            rolled[r] = pltpu.roll(ext, ext.shape[0] - r, 0)
        out.append(rolled[r][first + j - r:first + j - r + tm])
    return out


def _conv_glu(name, x, g0, sc, sh, w_pw1, b_pw1):
    T, D = x.shape

    def body(i, n, rr, cc, oo, aa, ss):
        hv = (_rms(rr[0][...], cc[2][...]) * (1.0 + cc[3][...]) + cc[4][...]).astype(MM)
        oo[2][...] = hv
        a = _dot(hv, cc[0][...]) + cc[1][...]
        oo[0][...] = a
        oo[1][...] = a[:, :D] * _sigmoid(a[:, D:])

    return _rows(name, body, T, 512, [(x, 'cur')], [w_pw1, b_pw1, g0, sc, sh], [(2 * D, F32), (D, F32), (D, MM)])


def _layernorm_parts(uc):
    xc = uc - jnp.mean(uc, axis=-1, keepdims=True)
    r = lax.rsqrt(jnp.mean(xc * xc, axis=-1, keepdims=True) + EPS)
    return xc * r, r


def _conv_dw(name, u, w_dw, b_dw, ln_g, ln_b, w_pw2, b_pw2, x, g1, gt):
    T, D = u.shape
    tm = min(256, T)

    def body(i, n, rr, cc, oo, aa, ss):
        ext = jnp.concatenate([jnp.where(i > 0, rr[1][tm - HALO:tm, :], 0.0), rr[0][...]], axis=0)
        uc = jnp.zeros((tm, D), F32) + cc[1][...]
        for kk, win in enumerate(_windows(ext, tm, HALO - (CONV_W - 1))):
            uc = uc + win * cc[0][kk:kk + 1, :]
        xh, _ = _layernorm_parts(uc)
        ln = xh * cc[2][...] + cc[3][...]
        z = (ln * _sigmoid(ln)).astype(MM)
        y = _dot(z, cc[4][...]) + cc[5][...]
        oo[0][...] = uc
        oo[1][...] = z
        oo[2][...] = y
        oo[3][...] = rr[2][...] + cc[7][...] * _rms(y, cc[6][...])

    return _rows(name, body, T, tm, [(u, 'cur'), (u, 'prev'), (x, 'cur')], [w_dw, b_dw, ln_g, ln_b, w_pw2, b_pw2, g1, gt],
                 [(D, F32), (D, MM), (D, F32), (D, F32)])


def _conv_bwd1(name, dxo, y, g1, gt, uc, w_pw2, ln_g, ln_b):
    T, D = uc.shape

    def body(i, n, rr, cc, oo, aa, ss):
        dy, dg1, dgt = _post_bwd_math(rr[0][...], rr[1][...], cc[3][...], cc[4][...])
        aa[3][...] += dg1
        aa[4][...] += dgt
        aa[5][...] += jnp.sum(dy, axis=0, keepdims=True)
        dy = dy.astype(MM)
        oo[1][...] = dy
        dz = _dot_nt(dy, cc[0][...])
        xh, r = _layernorm_parts(rr[2][...])
        g = cc[1][...]
        ln = xh * g + cc[2][...]
        sg = _sigmoid(ln)
        dln = dz * (sg * (1.0 + ln * (1.0 - sg)))
        aa[0][...] += jnp.sum(dln * xh, axis=0, keepdims=True)
        aa[1][...] += jnp.sum(dln, axis=0, keepdims=True)
        dxh = dln * g
        duc = r * (dxh - jnp.mean(dxh, axis=-1, keepdims=True) - xh * jnp.mean(dxh * xh, axis=-1, keepdims=True))
        aa[2][...] += jnp.sum(duc, axis=0, keepdims=True)
        oo[0][...] = duc

    return _rows(name, body, T, 512, [(dxo, 'cur'), (y, 'cur'), (uc, 'cur')], [w_pw2, ln_g, ln_b, g1, gt], [(D, F32), (D, MM)],
                 accs=[(1, D)] * 6)


def _conv_bwd2(name, duc, u, a, x, dxo, w_dw, w_pw1, g0, sc):
    T, D = u.shape
    tm = min(256, T)

    def body(i, n, rr, cc, oo, aa, ss):
        dcur = rr[0][...]
        extd = jnp.concatenate([dcur, jnp.where(i < n - 1, rr[1][0:HALO, :], 0.0)], axis=0)
        extu = jnp.concatenate([jnp.where(i > 0, rr[3][tm - HALO:tm, :], 0.0), rr[2][...]], axis=0)
        wd = _windows(extd, tm, 0)
        wu = _windows(extu, tm, HALO - (CONV_W - 1))
        du = jnp.zeros((tm, D), F32)
        for kk in range(CONV_W):
            du = du + wd[CONV_W - 1 - kk] * cc[0][kk:kk + 1, :]
            aa[0][kk:kk + 1, :] += jnp.sum(dcur * wu[kk], axis=0, keepdims=True)
        av = rr[4][...]
        a1, sg = av[:, :D], _sigmoid(av[:, D:])
        da = jnp.concatenate([du * sg, du * a1 * (sg * (1.0 - sg))], axis=1)
        aa[1][...] += jnp.sum(da, axis=0, keepdims=True)
        da = da.astype(MM)
        oo[0][...] = da
        dx, dsh, dsc, dg0 = _prenorm_bwd(rr[5][...], cc[2][...], cc[3][...], _dot_nt(da, cc[1][...]))
        oo[1][...] = rr[6][...] + dx
        aa[2][...] += dsh
        aa[3][...] += dsc
        aa[4][...] += dg0

    return _rows(name, body, T, tm,
                 [(duc, 'cur'), (duc, 'next'), (u, 'cur'), (u, 'prev'), (a, 'cur'), (x, 'cur'), (dxo, 'cur')],
                 [w_dw, w_pw1, g0, sc], [(2 * D, MM), (D, F32)],
                 accs=[(32, D), (1, 2 * D), (1, D), (1, D), (1, D)])


PHALO = 16


def _pool_fwd(name, h, w, b, scale, x, g1, gt):
    T, D = h.shape
    G = len(POOL_WINDOWS)
    Cg = D // G
    tm = min(512, T)

    def body(i, n, rr, cc, oo, aa, ss):
        ext = ss[0]
        ext[0:PHALO, :] = jnp.where(i > 0, rr[1][tm - PHALO:tm, :], 0.0)
        ext[PHALO:PHALO + tm, :] = rr[0][...]
        t_glob = i * tm + lax.broadcasted_iota(jnp.int32, (tm, 1), 0)
        ps, ys = [], []
        for g, win in enumerate(POOL_WINDOWS):
            cols = slice(g * Cg, (g + 1) * Cg)
            s = ext[pl.ds(PHALO, tm), cols]
            for j in range(1, win):
                s = s + ext[pl.ds(PHALO - j, tm), cols]
            cnt = jnp.minimum(t_glob + 1, win).astype(F32)
            p = (s / cnt - ext[pl.ds(PHALO, tm), cols]).astype(MM)
            ps.append(p)
            ys.append(_dot(p, cc[0][g]) + cc[1][:, cols])
        ypre = jnp.concatenate(ys, axis=1)
        y = ypre * cc[2][...]
        oo[0][...] = jnp.concatenate(ps, axis=1)
        oo[1][...] = ypre
        oo[2][...] = y
        oo[3][...] = rr[2][...] + cc[4][...] * _rms(y, cc[3][...])

    return _rows(name, body, T, tm, [(h, 'cur'), (h, 'prev'), (x, 'cur')], [w, b, scale, g1, gt],
                 [(D, MM), (D, F32), (D, F32), (D, F32)], scratch=[pltpu.VMEM((tm + PHALO, D), F32)])


def _pool_bwd1(name, dxo, y, g1, gt, ypre, scale, w):
    T, D = ypre.shape
    G = len(POOL_WINDOWS)
    Cg = D // G

    def body(i, n, rr, cc, oo, aa, ss):
        dyv, dg1, dgt = _post_bwd_math(rr[0][...], rr[1][...], cc[2][...], cc[3][...])
        aa[2][...] += dg1
        aa[3][...] += dgt
        aa[0][...] += jnp.sum(dyv * rr[2][...], axis=0, keepdims=True)
        dypre = dyv * cc[0][...]
        aa[1][...] += jnp.sum(dypre, axis=0, keepdims=True)
        dypre = dypre.astype(MM)
        oo[1][...] = dypre
        oo[0][...] = jnp.concatenate([_dot_nt(dypre[:, g * Cg:(g + 1) * Cg], cc[1][g]) for g in range(G)], axis=1)

    return _rows(name, body, T, 512, [(dxo, 'cur'), (y, 'cur'), (ypre, 'cur')], [scale, w, g1, gt], [(D, F32), (D, MM)],
                 accs=[(1, D)] * 4)


def _pool_bwd2(name, dp, x, dxo, g0, sc):
    T, D = x.shape
    G = len(POOL_WINDOWS)
    Cg = D // G
    tm = min(512, T)

    def body(i, n, rr, cc, oo, aa, ss):
        ext = ss[0]
        t_glob = i * tm + lax.broadcasted_iota(jnp.int32, (tm, 1), 0)
        dcur = rr[0][...]
        dhs = []
        for g, win in enumerate(POOL_WINDOWS):
            cols = slice(g * Cg, (g + 1) * Cg)
            cnt = jnp.minimum(t_glob + 1, win).astype(F32)
            ext[0:tm, cols] = dcur[:, cols] / cnt
            ext[tm:tm + PHALO, cols] = jnp.where(i < n - 1, rr[1][0:PHALO, cols] * (1.0 / win), 0.0)
        for g, win in enumerate(POOL_WINDOWS):
            cols = slice(g * Cg, (g + 1) * Cg)
            s = ext[pl.ds(0, tm), cols]
            for j in range(1, win):
                s = s + ext[pl.ds(j, tm), cols]
            dhs.append(s - dcur[:, cols])
        dx, dsh, dsc, dg0 = _prenorm_bwd(rr[2][...], cc[0][...], cc[1][...], jnp.concatenate(dhs, axis=1))
        oo[0][...] = rr[3][...] + dx
        aa[0][...] += dsh
        aa[1][...] += dsc
        aa[2][...] += dg0

    return _rows(name, body, T, tm, [(dp, 'cur'), (dp, 'next'), (x, 'cur'), (dxo, 'cur')], [g0, sc], [(D, F32)],
                 accs=[(1, D)] * 3, scratch=[pltpu.VMEM((tm + PHALO, D), F32)])


def _loss_head(x, tgt):
    T, D = x.shape

    def body(i, n, rr, cc, oo, aa, ss):
        err = rr[0][...] - rr[1][...]
        oo[0][...] = err * (1.0 / D)
        aa[0][...] += jnp.sum(err * err, axis=0, keepdims=True)

        @pl.when(i == n - 1)
        def _():
            aa[1][...] = jnp.broadcast_to(jnp.sum(aa[0][...], axis=1, keepdims=True) * (0.5 / D), (1, 128))

    dx, _, loss_row = _rows("loss_head", body, T, 512, [(x, 'cur'), (tgt, 'cur')], [], [(D, F32)], accs=[(1, D), (1, 128)])
    return dx, loss_row


def _adamw(name, w, g, m, v, after=None):
    shape = w.shape
    C = shape[-1]
    R = w.size // C
    w2, g2, m2, v2 = (t.reshape(R, C) for t in (w, g, m, v))
    br = R
    if R * C * 4 > (1 << 20):
        br = 8
        while br * 2 * C * 4 <= (1 << 20) and R % (br * 2) == 0:
            br *= 2
    b1c = 1.0 - ADAM_B1 ** ADAM_STEP
    b2c = 1.0 - ADAM_B2 ** ADAM_STEP

    def body(w_ref, g_ref, m_ref, v_ref, *rest):
        d_ref, mo_ref, vo_ref = rest[-3:]
        gv = g_ref[...]
        mn = ADAM_B1 * m_ref[...] + (1.0 - ADAM_B1) * gv
        vn = ADAM_B2 * v_ref[...] + (1.0 - ADAM_B2) * (gv * gv)
        d_ref[...] = -ADAM_LR * ((mn / b1c) / (jnp.sqrt(vn / b2c) + ADAM_EPS) + ADAM_WD * w_ref[...])
        mo_ref[...] = mn
        vo_ref[...] = vn

    spec = pl.BlockSpec((br, C), lambda r: (r, 0))
    extra = [] if after is None else [after]
    outs = pl.pallas_call(body, grid=(R // br,), in_specs=[spec] * 4 + [pl.BlockSpec(memory_space=pl.ANY)] * len(extra), out_specs=[spec] * 3,
                          out_shape=[jax.ShapeDtypeStruct((R, C), F32)] * 3, name=name,
                          compiler_params=_cparams(("arbitrary",)))(w2, g2, m2, v2, *extra)
    return tuple(t.reshape(shape) for t in outs)


def _layer_shards(g, ax):
    s = g.shape
    r = g.reshape(s[:ax] + (N_DEV, s[ax] // N_DEV) + s[ax + 1:])
    return (jnp.moveaxis(r, ax, 0) if ax else r).reshape(N_DEV, -1)


def _unshard(g, ax):
    r = jnp.moveaxis(g, 0, ax)
    s = r.shape
    return r.reshape(s[:ax] + (s[ax] * s[ax + 1],) + s[ax + 2:])


def _pack(parts, dtype, row_mult):
    lead = parts[0].shape[:-1]
    flat = jnp.concatenate([p.astype(dtype) for p in parts], axis=-1)
    n = flat.shape[-1]
    per = row_mult * 1024
    tot = -(-n // per) * per
    flat = jnp.pad(flat, [(0, 0)] * len(lead) + [(0, tot - n)])
    return flat.reshape(lead + (tot // 1024, 1024))


def _pad_heads(w, lo, hi):
    K = w.shape[0]
    r = w.reshape(K, N_HEADS, -1)[:, :, lo:hi]
    return jnp.pad(r, ((0, 0), (0, 0), (0, HEAD_PAD - (hi - lo)))).reshape(K, N_HEADS * HEAD_PAD)


def kernel(x, c, positions, ada_w, ada_b, norm_g, mla_w_dq, mla_q_norm_g, mla_w_uq, mla_w_dkv, mla_kv_norm_g, mla_w_ukv, mla_w_o, conv_w_pw1, conv_b_pw1, conv_w_dw, conv_b_dw, conv_ln_g, conv_ln_b, conv_w_pw2, conv_b_pw2, pool_w, pool_b, pool_scale, ffn_w1, ffn_w2, loss_target, m_ada_w, m_ada_b, m_norm_g, m_mla_w_dq, m_mla_q_norm_g, m_mla_w_uq, m_mla_w_dkv, m_mla_kv_norm_g, m_mla_w_ukv, m_mla_w_o, m_conv_w_pw1, m_conv_b_pw1, m_conv_w_dw, m_conv_b_dw, m_conv_ln_g, m_conv_ln_b, m_conv_w_pw2, m_conv_b_pw2, m_pool_w, m_pool_b, m_pool_scale, m_ffn_w1, m_ffn_w2, v_ada_w, v_ada_b, v_norm_g, v_mla_w_dq, v_mla_q_norm_g, v_mla_w_uq, v_mla_w_dkv, v_mla_kv_norm_g, v_mla_w_ukv, v_mla_w_o, v_conv_w_pw1, v_conv_b_pw1, v_conv_w_dw, v_conv_b_dw, v_conv_ln_g, v_conv_ln_b, v_conv_w_pw2, v_conv_b_pw2, v_pool_w, v_pool_b, v_pool_scale, v_ffn_w1, v_ffn_w2):
    args = dict(locals())
    W = {n: args[n] for n, _ in WEIGHTS}
    M1 = {n: args['m_' + n] for n, _ in WEIGHTS}
    V2 = {n: args['v_' + n] for n, _ in WEIGHTS}
    D = D_MODEL
    T = x.shape[1]
    L = ffn_w1.shape[0]
    xi, yi, ci = _place()
    me = 4 * xi + 2 * yi + ci
    n_ada = ada_w.shape[2]

    small_sizes = [W[n].size for n in SMALL]
    small_in = _pack([c.reshape(-1)] + [W[n].reshape(-1) for n in SMALL], F32, 8)
    small_all = _ag_small("ag_small_params", small_in).reshape(N_DEV, -1)
    c_all = small_all[:, :D]
    Ws = {}
    off = D
    for n, sz in zip(SMALL, small_sizes):
        Ws[n] = _unshard(small_all[:, off:off + sz].reshape((N_DEV,) + W[n].shape), SHARD_AXIS[n])
        off += sz
    c16 = jnp.pad(c_all, ((0, 16 - N_DEV), (0, 0)))

    ada_b_cols = lax.dynamic_slice_in_dim(ada_b, me * n_ada, n_ada, axis=1).reshape(L, 1, n_ada)
    mod_part = _mod_part(c16, ada_w, ada_b_cols)[:, :N_DEV]
    mod_all = _ag_small("ag_mod", mod_part.reshape(L * N_DEV, n_ada)).reshape(N_DEV, L, N_DEV, n_ada)
    mod_mine = lax.dynamic_index_in_dim(mod_all, me, axis=2, keepdims=False)
    mod = jnp.transpose(mod_mine, (1, 0, 2)).reshape(L, 6, 1, D)

    mla_names = [n for n in BIG if n.startswith('mla')]
    first_items = [(n, W[n][0]) for n in mla_names]
    later_items = [(n, W[n][1:]) for n in mla_names] + [(n, W[n]) for n in BIG if not n.startswith(('mla', 'ffn'))]
    first_all, = _ag_big("ag_weights", [_pack([a.reshape(-1) for _, a in first_items], MM, 32)])
    wf = [ffn_w1.astype(MM), ffn_w2.astype(MM), _pack([a.reshape(-1) for _, a in later_items], MM, 32)]
    wf, first_all, mod = lax.optimization_barrier((wf, first_all, mod))
    wf_land = [lax.dynamic_update_slice(lax.empty((N_DEV,) + w.shape, MM), w[None], (me,) + (0,) * w.ndim) for w in wf]
    ag_sems, wf_thru, wf_land, ag_token = _copies_start("ag_ffn_start", wf, wf_land, FIRST_LEVEL_PEERS, False)

    def unpack(g, items, dropped):
        flat, out, off = g.reshape(N_DEV, -1), {}, 0
        for n, a in items:
            out[n] = _unshard(flat[:, off:off + a.size].reshape((N_DEV,) + a.shape), SHARD_AXIS[n] - dropped)
            off += a.size
        return out

    n_mla = mla_w_dq.shape[0]
    w_dq, w_uq_p, w_ukv_k, w_ukv_v, w_dkv_p, w_o_p = ([None] * n_mla for _ in range(6))

    def set_mla(j, w):
        w_dq[j] = w['mla_w_dq']
        w_uq_p[j] = _pad_heads(w['mla_w_uq'], 0, QK_NOPE + QK_ROPE)
        w_ukv_k[j] = _pad_heads(w['mla_w_ukv'], 0, QK_NOPE)
        w_ukv_v[j] = _pad_heads(w['mla_w_ukv'], QK_NOPE, QK_NOPE + V_HEAD)
        w_dkv_p[j] = jnp.pad(jnp.concatenate([w['mla_w_dkv'][:, :KV_LORA], jnp.zeros((D, QK_NOPE), MM), w['mla_w_dkv'][:, KV_LORA:]], axis=1),
                             ((0, 0), (0, HEAD_PAD - QK_NOPE - QK_ROPE)))
        w_o_p[j] = jnp.pad(w['mla_w_o'].reshape(N_HEADS, V_HEAD, D), ((0, 0), (0, HEAD_PAD - V_HEAD), (0, 0))).reshape(N_HEADS * HEAD_PAD, D)

    set_mla(0, unpack(first_all, first_items, 1))
    w_dw32 = jnp.pad(Ws['conv_w_dw'], ((0, 0), (0, 32 - CONV_W), (0, 0)))
    row = lambda t: t.reshape(1, -1)

    half = QK_ROPE // 2
    inv_freq = ROPE_THETA ** (-jnp.arange(0, QK_ROPE, 2, dtype=F32) / QK_ROPE)
    invf = jnp.zeros((1, HEAD_PAD), F32).at[0, QK_NOPE:QK_NOPE + half].set(inv_freq).at[0, QK_NOPE + half:QK_NOPE + QK_ROPE].set(inv_freq)
    rC, rS1, rS2 = _rope_tables(positions.reshape(T, 1).astype(F32), invf)

    xs = x.reshape(T, D)
    saved = []
    for i in range(L):
        kind, j = i % 3, i // 3
        sh_m, sc_m, gt_m, sh_f, sc_f, gt_f = (mod[i, r] for r in range(6))
        g = [row(Ws['norm_g'][i, r]) for r in range(4)]
        st = dict(x0=xs)
        if i == 0:
            sc_m = sc_m + ag_token[0:1, 0:1]
        if kind == 0:
            cq_raw, cq, ckv_raw, ckv, q, k, v, h = _mla_proj(f"mla_proj{i}", xs, g[0], sc_m, sh_m, rC, rS1, rS2, w_dq[j],
                                                             row(Ws['mla_q_norm_g'][j]), w_uq_p[j], w_dkv_p[j],
                                                             row(Ws['mla_kv_norm_g'][j]), w_ukv_k[j], w_ukv_v[j])
            o, lse = _attn_fwd(f"attn_fwd{i}", q, k, v)
            y, xs = _mm_post(f"mla_out{i}", o, w_o_p[j], None, xs, g[1], gt_m)
            st.update(h=h, cq_raw=cq_raw, cq=cq, ckv_raw=ckv_raw, ckv=ckv, q=q, k=k, v=v, o=o, lse=lse, y=y)
        elif kind == 1:
            a, u, h = _conv_glu(f"conv_glu{i}", xs, g[0], sc_m, sh_m, w_pw1[j], row(W['conv_b_pw1'][j]))
            uc, z, y, xs = _conv_dw(f"conv_dw{i}", u, w_dw32[j], row(W['conv_b_dw'][j]), row(W['conv_ln_g'][j]), row(W['conv_ln_b'][j]),
                                    w_pw2[j], row(W['conv_b_pw2'][j]), xs, g[1], gt_m)
            st.update(h=h, a=a, u=u, uc=uc, z=z, y=y)
        else:
            h = _prenorm(f"prenorm_m{i}", xs, g[0], sc_m, sh_m, F32)
            p, ypre, y, xs = _pool_fwd(f"pool_fwd{i}", h, w_pool[j], row(Ws['pool_b'][j]), row(Ws['pool_scale'][j]), xs, g[1], gt_m)
            st.update(p=p, ypre=ypre, y=y)
        st['x1'] = xs
        if i == 0:
            wg = _copies_wait("ag_ffn_wait", ag_sems, wf_thru, wf_land, xs, FIRST_LEVEL_PEERS, False)
            w1g, w2g, later_all = _ag_forward("ag_ffn_forward", wg)
            later = unpack(later_all, later_items, 0)
            for jj in range(1, n_mla):
                set_mla(jj, {n: later[n][jj - 1] for n in mla_names})
            w_pw1, w_pw2, w_pool = later['conv_w_pw1'], later['conv_w_pw2'], later['pool_w']
        hf, af, yf, xs = _ffn_fwd(f"ffn_fwd{i}", i, xs, g[2], sc_f, sh_f, w1g, w2g, g[3], gt_f)
        st.update(hf=hf, af=af, yf=yf)
        saved.append(st)

    dx, loss_row = _loss_head(xs, loss_target.reshape(T, D))

    G = {}
    dmod = [None] * L
    dnorm = [None] * L
    rs_pending = None
    ffn_red = [lax.empty(ffn_w1.shape, F32), lax.empty(ffn_w2.shape, F32)]
    for i in reversed(range(L)):
        kind, j = i % 3, i // 3
        sh_m, sc_m, gt_m, sh_f, sc_f, gt_f = (mod[i, r] for r in range(6))
        g = [row(Ws['norm_g'][i, r]) for r in range(4)]
        st = saved[i]
        dy, da, dx, dg3, dgt_f, dsh_f, dsc_f, dg2 = _ffn_bwd(f"ffn_bwd{i}", i, st['yf'], g[3], gt_f, st['af'], w1g, w2g, st['x1'], dx, g[2], sc_f)
        wire1, own1 = _mm_tn_wire(f"ffn_dw1_{i}", st['hf'], da, me, False, False)
        wire2, own2 = _mm_tn_wire(f"ffn_dw2_{i}", st['af'], dy, me, True, True)
        if rs_pending is not None:
            ffn_red = _rs_finish(rs_pending, wire2, me, ffn_red)
        wires = [wire1, wire2]
        rs_sems, wires_thru, rs_lands, rs_token = _copies_start(f"rs_start{i}", wires, [lax.empty(w.shape, MM) for w in wires], ALL_PEERS, True)
        rs_pending = (i, rs_sems, wires_thru, rs_lands, [own1, own2])
        gt_m = gt_m + rs_token[0:1, 0:1]
        if kind == 0:
            dy, do, delta, dg1, dgt_m = _post_bwd_nt(f"mla_do{i}", dx, st['y'], g[1], gt_m, w_o_p[j], st['o'])
            dq, dk, dv = _attn_bwd(f"attn_bwd{i}", st['q'], st['k'], st['v'], do, st['lse'], delta)
            dq_pre, dcq_raw, dckv_all, dx, dqg, dkvg, dsh_m, dsc_m, dg0 = _mla_proj_bwd(
                f"mla_proj_bwd{i}", dq, dk, dv, rC, rS1, rS2, st['cq_raw'], st['ckv_raw'], st['x0'], dx, w_uq_p[j], w_ukv_k[j], w_ukv_v[j],
                w_dq[j], w_dkv_p[j], row(Ws['mla_q_norm_g'][j]), row(Ws['mla_kv_norm_g'][j]), g[0], sc_m)
            dwo = _mm_tn(f"mla_dwo{i}", st['o'], dy)
            dwuq = _mm_tn(f"mla_dwuq{i}", st['cq'], dq_pre)
            dwk = _mm_tn(f"mla_dwukvk{i}", st['ckv'], dk)
            dwv = _mm_tn(f"mla_dwukvv{i}", st['ckv'], dv)
            dwdq = _mm_tn(f"mla_dwdq{i}", st['h'], dcq_raw)
            dwdkv = _mm_tn(f"mla_dwdkv{i}", st['h'], dckv_all)
            G.setdefault('mla_w_o', [None] * n_mla)[j] = dwo.reshape(N_HEADS, HEAD_PAD, D)[:, :V_HEAD].reshape(N_HEADS * V_HEAD, D)
            G.setdefault('mla_w_uq', [None] * n_mla)[j] = dwuq.reshape(Q_LORA, N_HEADS, HEAD_PAD)[:, :, :QK_NOPE + QK_ROPE].reshape(Q_LORA, -1)
            G.setdefault('mla_w_ukv', [None] * n_mla)[j] = jnp.concatenate(
                [dwk.reshape(KV_LORA, N_HEADS, HEAD_PAD)[:, :, :QK_NOPE], dwv.reshape(KV_LORA, N_HEADS, HEAD_PAD)[:, :, :V_HEAD]], axis=2).reshape(KV_LORA, -1)
            G.setdefault('mla_w_dq', [None] * n_mla)[j] = dwdq
            G.setdefault('mla_w_dkv', [None] * n_mla)[j] = jnp.concatenate([dwdkv[:, :KV_LORA], dwdkv[:, KV_LORA + QK_NOPE:KV_LORA + QK_NOPE + QK_ROPE]], axis=1)
            G.setdefault('mla_q_norm_g', [None] * n_mla)[j] = dqg[0]
            G.setdefault('mla_kv_norm_g', [None] * n_mla)[j] = dkvg[0]
        elif kind == 1:
            duc, dy, dlng, dlnb, dbdw, dg1, dgt_m, dysum = _conv_bwd1(f"conv_bwd1_{i}", dx, st['y'], g[1], gt_m, st['uc'], w_pw2[j],
                                                                      row(W['conv_ln_g'][j]), row(W['conv_ln_b'][j]))
            da, dx, dwdw, dbpw1, dsh_m, dsc_m, dg0 = _conv_bwd2(f"conv_bwd2_{i}", duc, st['u'], st['a'], st['x0'], dx, w_dw32[j], w_pw1[j], g[0], sc_m)
            G['conv_w_pw2'] = [_mm_tn(f"conv_dwpw2_{i}", st['z'], dy)]
            G['conv_w_pw1'] = [_mm_tn(f"conv_dwpw1_{i}", st['h'], da)]
            G['conv_w_dw'] = [dwdw[:CONV_W]]
            G['conv_b_pw1'], G['conv_b_dw'], G['conv_ln_g'], G['conv_ln_b'], G['conv_b_pw2'] = [dbpw1[0]], [dbdw[0]], [dlng[0]], [dlnb[0]], [dysum[0]]
        else:
            dp, dypre, dscale, dpb, dg1, dgt_m = _pool_bwd1(f"pool_bwd1_{i}", dx, st['y'], g[1], gt_m, st['ypre'], row(Ws['pool_scale'][j]), w_pool[j])
            dx, dsh_m, dsc_m, dg0 = _pool_bwd2(f"pool_bwd2_{i}", dp, st['x0'], dx, g[0], sc_m)
            G['pool_w'] = [_mm_tn(f"pool_dw{i}", st['p'], dypre, diag=len(POOL_WINDOWS))]
            G['pool_b'] = [dpb.reshape(len(POOL_WINDOWS), -1)]
            G['pool_scale'] = [dscale[0]]
        dmod[i] = jnp.concatenate([dsh_m, dsc_m, dgt_m, dsh_f, dsc_f, dgt_f], axis=1)
        dnorm[i] = jnp.concatenate([dg0, dg1, dg2, dg3], axis=0)
    G['norm_g'] = dnorm
    grad_x = dx.reshape(x.shape)

    rs_names = [n for n, ax in WEIGHTS if ax is not None and n != 'ada_w' and not n.startswith('ffn')]
    pieces = [(n, _layer_shards(g, SHARD_AXIS[n] - 1)) for n in rs_names for g in G[n]]
    big = [(n, p) for n, p in pieces if p.shape[1] % (8 * 1024) == 0]
    small = [(n, p) for n, p in pieces if p.shape[1] % (8 * 1024) != 0]
    packed = jnp.concatenate([p.reshape(N_DEV, -1, 1024) for _, p in big] + [_pack([p for _, p in small], F32, 8)], axis=1)
    ffn_red = _rs_finish(rs_pending, dx, me, ffn_red)
    my_chip = 2 * xi + yi
    p4 = packed.reshape((4, 2) + packed.shape[1:])
    pair_recv, = _rs_pair("rs_pair", [p4])
    chip_wire, chip_own = _pair_sum("rs_pair_sum", p4, pair_recv, ci, my_chip)

    dmod_mine = jnp.concatenate(dmod, axis=1).reshape(-1)
    fin_in = _pack([dmod_mine] + [G[n][0].reshape(-1) for n in REPL] + [loss_row.reshape(-1)], F32, 8)
    fin_all = _ag_small("ag_final", fin_in)
    chip_wire, fin_all = lax.optimization_barrier((chip_wire, fin_all))
    chip_sems, chip_thru, chip_land, chip_token = _copies_start("rs_chips_start", [chip_wire], [lax.empty(chip_wire.shape, MM)],
                                                                CHIP_PEERS, 'chip')
    grads = {'ffn_w1': ffn_red[0], 'ffn_w2': ffn_red[1]}
    fin_sum = _sum_devices("final_sum", fin_all).reshape(-1)
    nm = L * 6 * D
    grads['ada_b'] = fin_sum[:nm].reshape(L, 6 * D)
    off = nm
    for n in REPL:
        grads[n] = fin_sum[off:off + W[n].size].reshape(W[n].shape)
        off += W[n].size
    loss = fin_sum[off]
    dmod_all = fin_all.reshape(N_DEV, -1)[:, :nm].reshape(N_DEV, L, 6 * D)
    dmod_cols = lax.dynamic_slice_in_dim(dmod_all, me * n_ada, n_ada, axis=2)
    dmod16 = jnp.pad(jnp.transpose(dmod_cols, (1, 0, 2)), ((0, 0), (0, 16 - N_DEV), (0, 0)))
    grads['ada_w'] = _ada_w_grad(c16, dmod16)
    deltas, new_m, new_v = {}, {}, {}
    done = chip_token
    for n in ['ffn_w1', 'ffn_w2', 'ada_w', 'ada_b'] + REPL:
        deltas[n], new_m[n], new_v[n] = _adamw("adamw_" + n, W[n], grads[n], M1[n], V2[n], after=done)
        done = deltas[n]
    chip_recv, = _copies_wait("rs_chips_wait", chip_sems, chip_thru, chip_land, done, CHIP_PEERS, 'chip')
    red = _chip_sum("rs_chip_sum", chip_own, chip_recv, my_chip)
    got = {}
    row0 = 0
    for n, p in big:
        rows = p.shape[1] // 1024
        got.setdefault(n, []).append(red[row0:row0 + rows])
        row0 += rows
    tail = red[row0:].reshape(-1)
    off = 0
    for n, p in small:
        got.setdefault(n, []).append(tail[off:off + p.shape[1]])
        off += p.shape[1]
    for n in rs_names:
        grads[n] = jnp.stack([g_.reshape(W[n].shape[1:]) for g_ in got[n]], axis=0)

    for n in rs_names:
        deltas[n], new_m[n], new_v[n] = _adamw("adamw_" + n, W[n], grads[n], M1[n], V2[n])
    names = [n for n, _ in WEIGHTS]
    return (loss, grad_x, *[grads[n] for n in names], *[deltas[n] for n in names], *[new_m[n] for n in names],
            *[new_v[n] for n in names])
```

```python
import math

import jax
import jax.numpy as jnp
from jax import lax
from jax.experimental import pallas as pl
from jax.experimental.pallas import tpu as pltpu

F32 = jnp.float32
MM = jnp.bfloat16
EPS = 1e-6
NEG = -1e30
N_DEV = 8
VMEM_LIMIT = 48 * 1024 * 1024
MESH = pl.DeviceIdType.MESH

D_MODEL = 1024
N_HEADS = 16
HEAD_PAD = 128
QK_NOPE, QK_ROPE, V_HEAD = 64, 32, 64
Q_LORA, KV_LORA = 384, 256
CHUNK = 64
CONV_W = 31
POOL_WINDOWS = (2, 4, 8, 16)
ROPE_THETA = 10000.0
ATT_SCALE = 1.0 / math.sqrt(QK_NOPE + QK_ROPE)

ADAM_LR, ADAM_B1, ADAM_B2, ADAM_EPS, ADAM_WD, ADAM_STEP = 0.001, 0.9, 0.999, 1e-08, 0.01, 10

WEIGHTS = [('ada_w', 2), ('ada_b', None), ('norm_g', 2), ('mla_w_dq', 1), ('mla_q_norm_g', 1), ('mla_w_uq', 2),
           ('mla_w_dkv', 1), ('mla_kv_norm_g', 1), ('mla_w_ukv', 2), ('mla_w_o', 1), ('conv_w_pw1', 2),
           ('conv_b_pw1', None), ('conv_w_dw', 2), ('conv_b_dw', None), ('conv_ln_g', None), ('conv_ln_b', None),
           ('conv_w_pw2', 1), ('conv_b_pw2', None), ('pool_w', 2), ('pool_b', 2), ('pool_scale', 1),
           ('ffn_w1', 2), ('ffn_w2', 1)]
SHARD_AXIS = dict(WEIGHTS)
BIG = ['mla_w_dq', 'mla_w_uq', 'mla_w_dkv', 'mla_w_ukv', 'mla_w_o', 'conv_w_pw1', 'conv_w_pw2', 'pool_w', 'ffn_w1', 'ffn_w2']
SMALL = ['norm_g', 'mla_q_norm_g', 'mla_kv_norm_g', 'conv_w_dw', 'pool_b', 'pool_scale']
REPL = ['conv_b_pw1', 'conv_b_dw', 'conv_ln_g', 'conv_ln_b', 'conv_b_pw2']


def _dot(a, b):
    return jnp.dot(a.astype(MM), b.astype(MM), preferred_element_type=F32)


def _dot_nt(a, b):
    return lax.dot_general(a.astype(MM), b.astype(MM), (((1,), (1,)), ((), ())), preferred_element_type=F32)


def _dot_tn(a, b):
    return lax.dot_general(a.astype(MM), b.astype(MM), (((0,), (0,)), ((), ())), preferred_element_type=F32)


def _sigmoid(x):
    return 1.0 / (1.0 + jnp.exp(-x))


def _rstd(x):
    return lax.rsqrt(jnp.mean(x * x, axis=-1, keepdims=True) + EPS)


def _rms(x, g):
    return x * _rstd(x) * g


def _rms_bwd(x, g, dout):
    r = _rstd(x)
    xn = x * r
    dg = jnp.sum(dout * xn, axis=0, keepdims=True)
    dxn = dout * g
    dx = r * (dxn - xn * jnp.mean(dxn * xn, axis=-1, keepdims=True))
    return dx, dg


def _prenorm_bwd(x, g0, sc, dh):
    r = _rstd(x)
    xn = x * r
    dsh = jnp.sum(dh, axis=0, keepdims=True)
    dsc = jnp.sum(dh * (xn * g0), axis=0, keepdims=True)
    dn = dh * (1.0 + sc)
    dg0 = jnp.sum(dn * xn, axis=0, keepdims=True)
    dxn = dn * g0
    dx = r * (dxn - xn * jnp.mean(dxn * xn, axis=-1, keepdims=True))
    return dx, dsh, dsc, dg0


def _cparams(sem, vmem=VMEM_LIMIT):
    return pltpu.CompilerParams(dimension_semantics=sem, vmem_limit_bytes=vmem)


def _rows(name, body, n_rows, tm, rows, consts, outs, accs=(), scratch=()):
    tm = min(tm, n_rows)
    nblk = n_rows // tm
    nr, nc, no, na = len(rows), len(consts), len(outs), len(accs)
    in_specs, args = [], []
    for a, kind in rows:
        if kind == 'cur':
            im = lambda i: (i, 0)
        elif kind == 'prev':
            im = lambda i: (jnp.maximum(i - 1, 0), 0)
        else:
            im = lambda i: (jnp.minimum(i + 1, nblk - 1), 0)
        in_specs.append(pl.BlockSpec((tm, a.shape[1]), im))
        args.append(a)
    for a in consts:
        in_specs.append(pl.BlockSpec(a.shape, lambda i, nd=a.ndim: (0,) * nd))
        args.append(a)
    out_specs = [pl.BlockSpec((o[0], tm), lambda i: (0, i)) if len(o) == 3 else pl.BlockSpec((tm, o[0]), lambda i: (i, 0)) for o in outs]
    out_specs += [pl.BlockSpec(s, lambda i, nd=len(s): (0,) * nd) for s in accs]
    out_shape = [jax.ShapeDtypeStruct((o[0], n_rows) if len(o) == 3 else (n_rows, o[0]), o[1]) for o in outs]
    out_shape += [jax.ShapeDtypeStruct(s, F32) for s in accs]

    def kern(*refs):
        i = pl.program_id(0)
        rr = refs[:nr]
        cc = refs[nr:nr + nc]
        oo = refs[nr + nc:nr + nc + no]
        aa = refs[nr + nc + no:nr + nc + no + na]
        ss = refs[nr + nc + no + na:]

        @pl.when(i == 0)
        def _():
            for a in aa:
                a[...] = jnp.zeros(a.shape, F32)

        body(i, nblk, rr, cc, oo, aa, ss)

    return pl.pallas_call(kern, grid=(nblk,), in_specs=in_specs, out_specs=out_specs, out_shape=out_shape,
                          scratch_shapes=list(scratch), name=name, compiler_params=_cparams(("arbitrary",)))(*args)


def _place():
    return lax.axis_index("x"), lax.axis_index("y"), lax.axis_index("c")


def _ag_small(name, xs):
    R, C = xs.shape

    def body(x_ref, out_ref, send_sems, recv_sems):
        x, y, c = _place()
        me = 4 * x + 2 * y + c
        out_ref[me] = x_ref[...]
        copies = []
        for k in range(1, N_DEV):
            peer = ((1 - x) if k & 4 else x, (1 - y) if k & 2 else y, (1 - c) if k & 1 else c)
            cp = pltpu.make_async_remote_copy(src_ref=x_ref, dst_ref=out_ref.at[me], send_sem=send_sems.at[k - 1],
                                              recv_sem=recv_sems.at[k - 1], device_id=peer, device_id_type=MESH)
            cp.start()
            copies.append(cp)
        for cp in copies:
            cp.wait()

    return pl.pallas_call(
        body, out_shape=jax.ShapeDtypeStruct((N_DEV, R, C), xs.dtype),
        in_specs=[pl.BlockSpec(memory_space=pltpu.VMEM)], out_specs=pl.BlockSpec(memory_space=pltpu.VMEM),
        scratch_shapes=[pltpu.SemaphoreType.DMA((N_DEV - 1,)), pltpu.SemaphoreType.DMA((N_DEV - 1,))], name=name)(xs)


def _ag_big(name, xs):
    nt = len(xs)

    def body(*refs):
        x_refs, out_refs = refs[:nt], refs[nt:2 * nt]
        send_sems, recv_sems, local_sems = refs[2 * nt:]
        x, y, c = _place()
        me, sibling = (x, y, c), (x, y, 1 - c)
        chips = [(1 - x, y), (x, 1 - y), (1 - x, 1 - y)]

        def copy(t, k, block, to, own=False):
            px, py, pc = block
            rows = out_refs[t].at[4 * px + 2 * py + pc]
            return pltpu.make_async_remote_copy(src_ref=x_refs[t] if own else rows, dst_ref=rows, send_sem=send_sems.at[7 * t + k],
                                                recv_sem=recv_sems.at[7 * t + k], device_id=to, device_id_type=MESH)

        mine = [pltpu.make_async_copy(x_refs[t], out_refs[t].at[4 * x + 2 * y + c], local_sems.at[t]) for t in range(nt)]
        for cp in mine:
            cp.start()
        first = []
        for t in range(nt):
            first.append(copy(t, 0, me, sibling, own=True))
            first += [copy(t, 1 + j, me, (*chip, c), own=True) for j, chip in enumerate(chips)]
        for cp in first:
            cp.start()
        passed = []
        for t in range(nt):
            for j, chip in enumerate(chips):
                copy(t, 1 + j, (*chip, c), me).wait_recv()
                cp = copy(t, 4 + j, (*chip, c), sibling)
                cp.start()
                passed.append(cp)
        for t in range(nt):
            copy(t, 0, sibling, me).wait_recv()
            for j, chip in enumerate(chips):
                copy(t, 4 + j, (*chip, 1 - c), me).wait_recv()
        for cp in first + passed:
            cp.wait_send()
        for cp in mine:
            cp.wait()

    hbm = pl.BlockSpec(memory_space=pl.ANY)
    return pl.pallas_call(
        body, out_shape=[jax.ShapeDtypeStruct((N_DEV,) + t.shape, t.dtype) for t in xs],
        in_specs=[hbm] * nt, out_specs=[hbm] * nt,
        scratch_shapes=[pltpu.SemaphoreType.DMA((7 * nt,)), pltpu.SemaphoreType.DMA((7 * nt,)), pltpu.SemaphoreType.DMA((nt,))],
        name=name)(*xs)


def _rs_pair(name, ps):
    nt = len(ps)

    def body(*refs):
        p_refs, recv_refs = refs[:nt], refs[nt:2 * nt]
        send_sems, recv_sems = refs[2 * nt:]
        x, y, c = _place()
        copies = []
        for t in range(nt):
            for j in range(4):
                cp = pltpu.make_async_remote_copy(src_ref=p_refs[t].at[j, 1 - c], dst_ref=recv_refs[t].at[j], send_sem=send_sems.at[4 * t + j],
                                                  recv_sem=recv_sems.at[4 * t + j], device_id=(x, y, 1 - c), device_id_type=MESH)
                cp.start()
                copies.append(cp)
        for cp in copies:
            cp.wait()

    hbm = pl.BlockSpec(memory_space=pl.ANY)
    return pl.pallas_call(
        body, out_shape=[jax.ShapeDtypeStruct((4,) + p.shape[2:], p.dtype) for p in ps], in_specs=[hbm] * nt, out_specs=[hbm] * nt,
        scratch_shapes=[pltpu.SemaphoreType.DMA((4 * nt,)), pltpu.SemaphoreType.DMA((4 * nt,))], name=name)(*ps)


RS_ROWS = 256


def _row_block(r):
    return next(t for t in range(RS_ROWS, 0, -16) if r % t == 0)


def _pair_sum(name, p, recv, my_c, my_chip):
    _, _, r, c = p.shape
    tr = _row_block(r)

    def body(sc_ref, p_ref, r_ref, o_ref, own_ref):
        s = p_ref[...] + r_ref[...]
        o_ref[...] = s.astype(MM)

        @pl.when(pl.program_id(1) == sc_ref[1])
        def _():
            own_ref[...] = s

    return pl.pallas_call(
        body, grid_spec=pltpu.PrefetchScalarGridSpec(
            num_scalar_prefetch=1, grid=(r // tr, 4),
            in_specs=[pl.BlockSpec((None, None, tr, c), lambda i, j, sc: (j, sc[0], i, 0)),
                      pl.BlockSpec((None, tr, c), lambda i, j, sc: (j, i, 0))],
            out_specs=[pl.BlockSpec((None, tr, c), lambda i, j, sc: (j, i, 0)), pl.BlockSpec((tr, c), lambda i, j, sc: (i, 0))]),
        out_shape=[jax.ShapeDtypeStruct((4, r, c), MM), jax.ShapeDtypeStruct((r, c), F32)], name=name,
        compiler_params=_cparams(("arbitrary", "arbitrary")))(jnp.stack([my_c, my_chip]), p, recv)


def _chip_sum(name, own, recv, my_chip):
    _, r, c = recv.shape
    tr = _row_block(r)

    def body(sc_ref, own_ref, r_ref, o_ref):
        acc = jnp.zeros((tr, c), F32)
        for j in range(4):
            acc = acc + jnp.where(sc_ref[0] == j, own_ref[...], r_ref[j].astype(F32))
        o_ref[...] = acc

    return pl.pallas_call(
        body, grid_spec=pltpu.PrefetchScalarGridSpec(
            num_scalar_prefetch=1, grid=(r // tr,),
            in_specs=[pl.BlockSpec((tr, c), lambda i, sc: (i, 0)), pl.BlockSpec((4, tr, c), lambda i, sc: (0, i, 0))],
            out_specs=pl.BlockSpec((tr, c), lambda i, sc: (i, 0))),
        out_shape=jax.ShapeDtypeStruct((r, c), F32), name=name,
        compiler_params=_cparams(("arbitrary",)))(my_chip.reshape(1), own, recv)


HBM_SPEC = pl.BlockSpec(memory_space=pltpu.HBM)
SEM_SPEC = pl.BlockSpec(memory_space=pltpu.SEMAPHORE)
SPLIT_EFFECT = pltpu.SideEffectType.DATAFLOW_SIDE_EFFECTING
ALL_PEERS = (1, 2, 3, 4, 5, 6, 7)
FIRST_LEVEL_PEERS = (1, 4, 2, 6)
CHIP_PEERS = (4, 2, 6)


def _split_copies(src_refs, land_refs, sems, masks, src_per_peer):
    n, nt = len(masks), len(src_refs)
    x, y, c = _place()
    by_chip = src_per_peer == 'chip'
    slot = 2 * x + y if by_chip else 4 * x + 2 * y + c
    copies = []
    for t in range(nt):
        for k, mask in enumerate(masks):
            px, py, pc = (1 - x) if mask & 4 else x, (1 - y) if mask & 2 else y, (1 - c) if mask & 1 else c
            src = src_refs[t].at[2 * px + py if by_chip else 4 * px + 2 * py + pc] if src_per_peer else src_refs[t]
            copies.append(pltpu.make_async_remote_copy(src_ref=src, dst_ref=land_refs[t].at[slot], send_sem=sems[t * n + k],
                                                       recv_sem=sems[nt * n + t * n + k], device_id=(px, py, pc), device_id_type=MESH))
    return copies


def _copies_start(name, srcs, lands, masks, src_per_peer):
    nt, ns = len(srcs), 2 * len(masks) * len(srcs)

    def body(*refs):
        for cp in _split_copies(refs[:nt], refs[nt:2 * nt], refs[2 * nt:2 * nt + ns], masks, src_per_peer):
            cp.start()
        token = refs[-1]
        token[...] = jnp.zeros(token.shape, F32)

    outs = pl.pallas_call(
        body, name=name,
        out_shape=(pltpu.SemaphoreType.DMA(()),) * ns + tuple(pltpu.HBM(a.shape, a.dtype) for a in list(srcs) + list(lands))
        + (jax.ShapeDtypeStruct((8, 128), F32),),
        in_specs=(HBM_SPEC,) * (2 * nt), out_specs=(SEM_SPEC,) * ns + (HBM_SPEC,) * (2 * nt) + (pl.BlockSpec(memory_space=pltpu.VMEM),),
        input_output_aliases={t: ns + t for t in range(2 * nt)}, compiler_params=pltpu.CompilerParams(has_side_effects=SPLIT_EFFECT))(
            *[pltpu.with_memory_space_constraint(a, pltpu.HBM) for a in list(srcs) + list(lands)])
    return outs[:ns], outs[ns:ns + nt], outs[ns + nt:ns + 2 * nt], outs[-1]


def _copies_wait(name, sems, srcs_thru, lands_thru, after, masks, src_per_peer):
    nt, ns = len(srcs_thru), len(sems)

    def body(*refs):
        for cp in _split_copies(refs[:nt], refs[nt:2 * nt], refs[2 * nt:2 * nt + ns], masks, src_per_peer):
            cp.wait_send()
            cp.wait_recv()

    thru = list(srcs_thru) + list(lands_thru)
    return pl.pallas_call(
        body, name=name, out_shape=tuple(pltpu.HBM(a.shape, a.dtype) for a in thru),
        in_specs=(HBM_SPEC,) * (2 * nt) + (SEM_SPEC,) * ns + (pl.BlockSpec(memory_space=pl.ANY),), out_specs=(HBM_SPEC,) * (2 * nt),
        input_output_aliases={t: t for t in range(2 * nt)}, compiler_params=pltpu.CompilerParams(has_side_effects=SPLIT_EFFECT))(
            *thru, *sems, after)[nt:]


def _ag_forward(name, gs):
    nt = len(gs)

    def body(*refs):
        o_refs, send_sems, recv_sems = refs[nt:2 * nt], refs[2 * nt], refs[2 * nt + 1]
        x, y, c = _place()
        chips = [(1 - x, y), (x, 1 - y), (1 - x, 1 - y)]

        def copy(t, j, pc):
            rows = o_refs[t].at[4 * chips[j][0] + 2 * chips[j][1] + pc]
            return pltpu.make_async_remote_copy(src_ref=rows, dst_ref=rows, send_sem=send_sems.at[3 * t + j], recv_sem=recv_sems.at[3 * t + j],
                                                device_id=(x, y, 1 - c), device_id_type=MESH)

        for t in range(nt):
            for j in range(3):
                copy(t, j, c).start()
        for t in range(nt):
            for j in range(3):
                copy(t, j, c).wait_send()
                copy(t, j, 1 - c).wait_recv()

    hbm = pl.BlockSpec(memory_space=pl.ANY)
    return pl.pallas_call(body, out_shape=[jax.ShapeDtypeStruct(g.shape, g.dtype) for g in gs], in_specs=[hbm] * nt, out_specs=[hbm] * nt,
                          scratch_shapes=[pltpu.SemaphoreType.DMA((3 * nt,)), pltpu.SemaphoreType.DMA((3 * nt,))],
                          input_output_aliases={t: t for t in range(nt)}, name=name)(*gs)


def _mm_tn_wire(name, a, b, me, sqrelu, shard_rows):
    T, M = a.shape
    N = b.shape[1]
    tk = min(2048, T)
    nk = T // tk
    if shard_rows:
        bm, bn = M // N_DEV, N
        a_spec = pl.BlockSpec((tk, 2 * bm), lambda j, k, m: (k, j))
        b_spec = pl.BlockSpec((tk, bn), lambda j, k, m: (k, 0))
        halves = (slice(0, bm), slice(None)), (slice(bm, 2 * bm), slice(None))
        acc_shape = (2 * bm, bn)
    else:
        bm, bn = M, N // N_DEV
        a_spec = pl.BlockSpec((tk, bm), lambda j, k, m: (k, 0))
        b_spec = pl.BlockSpec((tk, 2 * bn), lambda j, k, m: (k, j))
        halves = (slice(None), slice(0, bn)), (slice(None), slice(bn, 2 * bn))
        acc_shape = (bm, 2 * bn)

    def body(me_ref, a_ref, b_ref, wire_ref, own_ref, acc):
        j, k = pl.program_id(0), pl.program_id(1)

        @pl.when(k == 0)
        def _():
            acc[...] = jnp.zeros(acc.shape, F32)

        av = a_ref[...]
        if sqrelu:
            r = jnp.maximum(av, 0.0)
            av = r * r
        acc[...] += _dot_tn(av, b_ref[...])

        for hh in range(2):
            @pl.when(k == nk - 1)
            def _():
                wire_ref[hh] = acc[halves[hh]].astype(MM)

            @pl.when((k == nk - 1) & (2 * j + hh == me_ref[0]))
            def _():
                own_ref[...] = acc[halves[hh]]

    return pl.pallas_call(
        body, grid_spec=pltpu.PrefetchScalarGridSpec(
            num_scalar_prefetch=1, grid=(N_DEV // 2, nk), in_specs=[a_spec, b_spec],
            out_specs=[pl.BlockSpec((2, bm, bn), lambda j, k, m: (j, 0, 0)), pl.BlockSpec((bm, bn), lambda j, k, m: (0, 0))],
            scratch_shapes=[pltpu.VMEM(acc_shape, F32)]),
        out_shape=[jax.ShapeDtypeStruct((N_DEV, bm, bn), MM), jax.ShapeDtypeStruct((bm, bn), F32)], name=name,
        compiler_params=_cparams(("arbitrary", "arbitrary")))(me.reshape(1), a, b)


def _rs_final(name, own, recv, me, stack, li):
    _, r, c = recv.shape
    tr = RS_ROWS

    def body(me_ref, own_ref, r_ref, s_ref, o_ref):
        acc = jnp.zeros((tr, c), F32)
        for j in range(N_DEV):
            acc = acc + jnp.where(me_ref[0] == j, own_ref[...], r_ref[j].astype(F32))
        o_ref[...] = acc

    return pl.pallas_call(
        body, grid_spec=pltpu.PrefetchScalarGridSpec(
            num_scalar_prefetch=1, grid=(r // tr,),
            in_specs=[pl.BlockSpec((tr, c), lambda i, m: (i, 0)), pl.BlockSpec((N_DEV, tr, c), lambda i, m: (0, i, 0)),
                      pl.BlockSpec(memory_space=pl.ANY)],
            out_specs=pl.BlockSpec((None, tr, c), lambda i, m: (li, i, 0))),
        out_shape=jax.ShapeDtypeStruct(stack.shape, F32), input_output_aliases={3: 0}, name=name,
        compiler_params=_cparams(("arbitrary",)))(me.reshape(1), own, recv, stack)


def _rs_finish(pending, after, me, stacks):
    i, sems, wires_thru, lands, owns = pending
    recvs = _copies_wait(f"rs_wait{i}", sems, wires_thru, lands, after, ALL_PEERS, True)
    return [_rs_final(f"rs_final{i}_{t}", owns[t], recvs[t], me, stacks[t], i) for t in range(len(owns))]


def _mod_part(c16, ada_w, ada_b_cols):
    L, D, n = ada_w.shape

    def body(c_ref, w_ref, b_ref, o_ref):
        cv = c_ref[...]
        o_ref[...] = _dot(cv * _sigmoid(cv), w_ref[...]) + b_ref[...]

    return pl.pallas_call(
        body, grid=(L,), in_specs=[pl.BlockSpec((16, D), lambda i: (0, 0)), pl.BlockSpec((None, D, n), lambda i: (i, 0, 0)),
                                   pl.BlockSpec((None, 1, n), lambda i: (i, 0, 0))],
        out_specs=pl.BlockSpec((None, 16, n), lambda i: (i, 0, 0)), out_shape=jax.ShapeDtypeStruct((L, 16, n), F32),
        name="ada_mod", compiler_params=_cparams(("arbitrary",)))(c16, ada_w, ada_b_cols)


def _ada_w_grad(c16, dmod16):
    L, _, n = dmod16.shape
    D = c16.shape[1]

    def body(c_ref, d_ref, o_ref):
        cv = c_ref[...]
        o_ref[...] = _dot_tn(cv * _sigmoid(cv), d_ref[...])

    return pl.pallas_call(
        body, grid=(L,), in_specs=[pl.BlockSpec((16, D), lambda i: (0, 0)), pl.BlockSpec((None, 16, n), lambda i: (i, 0, 0))],
        out_specs=pl.BlockSpec((None, D, n), lambda i: (i, 0, 0)), out_shape=jax.ShapeDtypeStruct((L, D, n), F32),
        name="ada_w_grad", compiler_params=_cparams(("arbitrary",)))(c16, dmod16)


def _sum_devices(name, g):
    _, R, C = g.shape

    def body(g_ref, o_ref):
        acc = g_ref[0]
        for d in range(1, N_DEV):
            acc = acc + g_ref[d]
        o_ref[...] = acc

    return pl.pallas_call(body, out_shape=jax.ShapeDtypeStruct((R, C), F32), name=name)(g)


def _prenorm(name, x, g0, sc, sh, dtype):
    T, D = x.shape

    def body(i, n, rr, cc, oo, aa, ss):
        oo[0][...] = (_rms(rr[0][...], cc[0][...]) * (1.0 + cc[1][...]) + cc[2][...]).astype(dtype)

    return _rows(name, body, T, 512, [(x, 'cur')], [g0, sc, sh], [(D, dtype)])[0]


def _post_bwd_math(d, yv, g1v, gtv):
    dgt = jnp.sum(d * _rms(yv, g1v), axis=0, keepdims=True)
    dy, dg1 = _rms_bwd(yv, g1v, d * gtv)
    return dy, dg1, dgt


def _post_bwd_nt(name, dxo, y, g1, gt, w, o):
    T, D = y.shape
    K = w.shape[0]

    def body(i, n, rr, cc, oo, aa, ss):
        dy, dg1, dgt = _post_bwd_math(rr[0][...], rr[1][...], cc[0][...], cc[1][...])
        aa[0][...] += dg1
        aa[1][...] += dgt
        dy = dy.astype(MM)
        oo[0][...] = dy
        do = _dot_nt(dy, cc[2][...]).astype(MM)
        oo[1][...] = do
        tm = do.shape[0]
        lane = lax.broadcasted_iota(jnp.int32, (tm, HEAD_PAD), 1) // 8
        cols = jnp.zeros((tm, HEAD_PAD), F32)
        for h in range(N_HEADS):
            hsl = slice(h * HEAD_PAD, (h + 1) * HEAD_PAD)
            r = jnp.sum(do[:, hsl].astype(F32) * rr[2][:, hsl].astype(F32), axis=1, keepdims=True)
            cols = jnp.where(lane == h, r, cols)
        oo[2][...] = cols.T

    dy, do, delta, dg1, dgt = _rows(name, body, T, 512, [(dxo, 'cur'), (y, 'cur'), (o, 'cur')], [g1, gt, w],
                                    [(D, MM), (K, MM), (HEAD_PAD, F32, 'T')], accs=[(1, D)] * 2)
    return dy, do, delta.reshape(N_HEADS, 8, T), dg1, dgt


def _mm_post(name, a, w, bias, x, g1, gt):
    T, D = x.shape
    consts = [w, g1, gt] + ([bias] if bias is not None else [])

    def body(i, n, rr, cc, oo, aa, ss):
        y = _dot(rr[0][...], cc[0][...])
        if bias is not None:
            y = y + cc[3][...]
        oo[0][...] = y
        oo[1][...] = rr[1][...] + cc[2][...] * _rms(y, cc[1][...])

    return _rows(name, body, T, 512, [(a, 'cur'), (x, 'cur')], consts, [(D, F32), (D, F32)])


def _mm_tn(name, a, b, sqrelu=False, col_shards=0, diag=0):
    T, M = a.shape
    N = b.shape[1]
    tk = min(512, T)
    nk = T // tk
    if diag:
        bm, bn = M // diag, N // diag
        grid = (diag, 1, nk)
        a_spec = pl.BlockSpec((tk, bm), lambda g, n, k: (k, g))
        b_spec = pl.BlockSpec((tk, bn), lambda g, n, k: (k, g))
        o_spec = pl.BlockSpec((None, bm, bn), lambda g, n, k: (g, 0, 0))
        o_shape = (diag, bm, bn)
    else:
        bm = min(M, 1024)
        bn = N // col_shards if col_shards else min(N, 1024)
        grid = (M // bm, N // bn, nk)
        a_spec = pl.BlockSpec((tk, bm), lambda m, n, k: (k, m))
        b_spec = pl.BlockSpec((tk, bn), lambda m, n, k: (k, n))
        if col_shards:
            o_spec = pl.BlockSpec((None, bm, bn), lambda m, n, k: (n, m, 0))
            o_shape = (col_shards, M, bn)
        else:
            o_spec = pl.BlockSpec((bm, bn), lambda m, n, k: (m, n))
            o_shape = (M, N)

    def body(a_ref, b_ref, o_ref):
        @pl.when(pl.program_id(2) == 0)
        def _():
            o_ref[...] = jnp.zeros(o_ref.shape, F32)

        av = a_ref[...]
        if sqrelu:
            r = jnp.maximum(av, 0.0)
            av = r * r
        o_ref[...] += _dot_tn(av, b_ref[...])

    return pl.pallas_call(body, grid=grid, in_specs=[a_spec, b_spec], out_specs=o_spec,
                          out_shape=jax.ShapeDtypeStruct(o_shape, F32), name=name,
                          compiler_params=_cparams(("arbitrary", "arbitrary", "arbitrary")))(a, b)


FFN_SHARDS = 4
FFN_BWD_SHARDS = 4
FFN_BWD_VMEM = 56 * 1024 * 1024

def _ffn_fwd(name, li, x, g0, sc, sh, w1g, w2g, g1, gt):
    T, D = x.shape
    nf, tf = w1g.shape[0], w1g.shape[-1]
    F = nf * tf
    tm = min(512, T)

    def body(x_ref, g0_ref, sc_ref, sh_ref, w1_ref, w2_ref, g1_ref, gt_ref, h_ref, a_ref, y_ref, xo_ref, acc):
        f = pl.program_id(1)

        @pl.when(f == 0)
        def _():
            acc[...] = jnp.zeros(acc.shape, F32)
            h_ref[...] = (_rms(x_ref[...], g0_ref[...]) * (1.0 + sc_ref[...]) + sh_ref[...]).astype(MM)

        hv = h_ref[...]
        part = None
        for hh in range(FFN_SHARDS):
            a = _dot(hv, w1_ref[hh])
            a_ref[:, hh * tf:(hh + 1) * tf] = a.astype(MM)
            r = jnp.maximum(a, 0.0)
            p = _dot(r * r, w2_ref[hh])
            part = p if part is None else part + p
        acc[...] += part

        @pl.when(f == nf // FFN_SHARDS - 1)
        def _():
            y = acc[...]
            y_ref[...] = y
            xo_ref[...] = x_ref[...] + gt_ref[...] * _rms(y, g1_ref[...])

    row = lambda t, f: (t, 0)
    one = lambda t, f: (0, 0)
    return pl.pallas_call(
        body, grid=(T // tm, nf // FFN_SHARDS),
        in_specs=[pl.BlockSpec((tm, D), row)] + [pl.BlockSpec((1, D), one)] * 3
        + [pl.BlockSpec((FFN_SHARDS, None, D, tf), lambda t, f: (f, li, 0, 0)), pl.BlockSpec((FFN_SHARDS, None, tf, D), lambda t, f: (f, li, 0, 0)),
           pl.BlockSpec((1, D), one), pl.BlockSpec((1, D), one)],
        out_specs=[pl.BlockSpec((tm, D), row), pl.BlockSpec((tm, FFN_SHARDS * tf), lambda t, f: (t, f)), pl.BlockSpec((tm, D), row),
                   pl.BlockSpec((tm, D), row)],
        out_shape=[jax.ShapeDtypeStruct((T, D), MM), jax.ShapeDtypeStruct((T, F), MM), jax.ShapeDtypeStruct((T, D), F32),
                   jax.ShapeDtypeStruct((T, D), F32)],
        scratch_shapes=[pltpu.VMEM((tm, D), F32)], name=name,
        compiler_params=_cparams(("arbitrary", "arbitrary")))(x, g0, sc, sh, w1g, w2g, g1, gt)


def _ffn_bwd(name, li, y, g1, gt, a, w1g, w2g, x, dxo, g0, sc):
    T, D = x.shape
    nf, tf = w1g.shape[0], w1g.shape[-1]
    F = nf * tf
    tm = min(512, T)
    ns = FFN_BWD_SHARDS

    def body(y_ref, g1_ref, gt_ref, a_ref, w1_ref, w2_ref, x_ref, dxo_ref, g0_ref, sc_ref,
             dy_ref, da_ref, dx_ref, dg1_ref, dgt_ref, dsh_ref, dsc_ref, dg0_ref, acc):
        t, f = pl.program_id(0), pl.program_id(1)

        @pl.when((t == 0) & (f == 0))
        def _():
            for r in (dg1_ref, dgt_ref, dsh_ref, dsc_ref, dg0_ref):
                r[...] = jnp.zeros(r.shape, F32)

        @pl.when(f == 0)
        def _():
            acc[...] = jnp.zeros(acc.shape, F32)
            d, yv, g1v = dxo_ref[...], y_ref[...], g1_ref[...]
            dgt_ref[...] += jnp.sum(d * _rms(yv, g1v), axis=0, keepdims=True)
            dyf, dg1 = _rms_bwd(yv, g1v, d * gt_ref[...])
            dg1_ref[...] += dg1
            dy_ref[...] = dyf.astype(MM)

        dyv = dy_ref[...]
        dyv = dyv + dyv
        part = None
        for hh in range(ns):
            cols = slice(hh * tf, (hh + 1) * tf)
            du = _dot_nt(dyv, w2_ref[hh])
            da = (du * jnp.maximum(a_ref[:, cols], 0.0).astype(F32)).astype(MM)
            da_ref[:, cols] = da
            p = _dot_nt(da, w1_ref[hh])
            part = p if part is None else part + p
        acc[...] += part

        @pl.when(f == nf // ns - 1)
        def _():
            dx, dsh, dsc, dg0 = _prenorm_bwd(x_ref[...], g0_ref[...], sc_ref[...], acc[...])
            dx_ref[...] = dxo_ref[...] + dx
            dsh_ref[...] += dsh
            dsc_ref[...] += dsc
            dg0_ref[...] += dg0

    row = lambda t, f: (t, 0)
    one = lambda t, f: (0, 0)
    blk = lambda t, f: (t, f)
    return pl.pallas_call(
        body, grid=(T // tm, nf // ns),
        in_specs=[pl.BlockSpec((tm, D), row), pl.BlockSpec((1, D), one), pl.BlockSpec((1, D), one), pl.BlockSpec((tm, ns * tf), blk),
                  pl.BlockSpec((ns, None, D, tf), lambda t, f: (f, li, 0, 0)),
                  pl.BlockSpec((ns, None, tf, D), lambda t, f: (f, li, 0, 0)), pl.BlockSpec((tm, D), row), pl.BlockSpec((tm, D), row),
                  pl.BlockSpec((1, D), one), pl.BlockSpec((1, D), one)],
        out_specs=[pl.BlockSpec((tm, D), row), pl.BlockSpec((tm, ns * tf), blk), pl.BlockSpec((tm, D), row)] + [pl.BlockSpec((1, D), one)] * 5,
        out_shape=[jax.ShapeDtypeStruct((T, D), MM), jax.ShapeDtypeStruct((T, F), MM), jax.ShapeDtypeStruct((T, D), F32)]
        + [jax.ShapeDtypeStruct((1, D), F32)] * 5,
        scratch_shapes=[pltpu.VMEM((tm, D), F32)], name=name,
        compiler_params=_cparams(("arbitrary", "arbitrary"), FFN_BWD_VMEM))(y, g1, gt, a, w1g, w2g, x, dxo, g0, sc)


def _rope_tables(pos, invf):
    T = pos.shape[0]

    def body(i, n, rr, cc, oo, aa, ss):
        ang = rr[0][...] * cc[0][...]
        lane = lax.broadcasted_iota(jnp.int32, ang.shape, 1)
        cs, sn = jnp.cos(ang), jnp.sin(ang)
        oo[0][...] = jnp.where((lane >= QK_NOPE) & (lane < QK_NOPE + QK_ROPE), cs, 1.0)
        oo[1][...] = jnp.where((lane >= QK_NOPE) & (lane < QK_NOPE + QK_ROPE // 2), -sn, 0.0)
        oo[2][...] = jnp.where((lane >= QK_NOPE + QK_ROPE // 2) & (lane < QK_NOPE + QK_ROPE), sn, 0.0)

    return _rows("rope_tables", body, T, 512, [(pos, 'cur')], [invf], [(HEAD_PAD, F32)] * 3)


def _rope(v, C, S1, S2):
    n = v.shape[1]
    reps = n // HEAD_PAD
    if reps > 1:
        C, S1, S2 = (jnp.tile(t, (1, reps)) for t in (C, S1, S2))
    return v * C + pltpu.roll(v, n - QK_ROPE // 2, 1) * S1 + pltpu.roll(v, QK_ROPE // 2, 1) * S2


def _unrope(d, C, S1, S2):
    n = d.shape[1]
    reps = n // HEAD_PAD
    if reps > 1:
        C, S1, S2 = (jnp.tile(t, (1, reps)) for t in (C, S1, S2))
    return d * C + pltpu.roll(d * S1, QK_ROPE // 2, 1) + pltpu.roll(d * S2, n - QK_ROPE // 2, 1)


MLA_ROWS = 512
MLA_BWD_ROWS = 512


def _mla_proj(name, x, g0, sc, sh, C, S1, S2, w_dq, qg, w_uq, w_dkv, kvg, w_ukv_k, w_ukv_v):
    T, D = x.shape
    HP = N_HEADS * HEAD_PAD

    def body(i, n, rr, cc, oo, aa, ss):
        hv = (_rms(rr[0][...], cc[7][...]) * (1.0 + cc[8][...]) + cc[9][...]).astype(MM)
        oo[7][...] = hv
        Cv, S1v, S2v = rr[1][...], rr[2][...], rr[3][...]
        cq_raw = _dot(hv, cc[0][...])
        cq = _rms(cq_raw, cc[1][...]).astype(MM)
        q = _rope(_dot(cq, cc[2][...]), Cv, S1v, S2v)
        ckv_all = _dot(hv, cc[3][...])
        ckv_raw = ckv_all[:, :KV_LORA]
        ckv = _rms(ckv_raw, cc[4][...]).astype(MM)
        kr = _rope(ckv_all[:, KV_LORA:], Cv, S1v, S2v)
        k = _dot(ckv, cc[5][...]) + jnp.tile(kr, (1, N_HEADS))
        v = _dot(ckv, cc[6][...])
        v = jnp.where(lax.broadcasted_iota(jnp.int32, v.shape, 1) % HEAD_PAD == V_HEAD, 1.0, v)
        oo[0][...] = cq_raw
        oo[1][...] = cq
        oo[2][...] = ckv_raw
        oo[3][...] = ckv
        oo[4][...] = q.astype(MM)
        oo[5][...] = k.astype(MM)
        oo[6][...] = v.astype(MM)

    return _rows(name, body, T, MLA_ROWS, [(x, 'cur'), (C, 'cur'), (S1, 'cur'), (S2, 'cur')],
                 [w_dq, qg, w_uq, w_dkv, kvg, w_ukv_k, w_ukv_v, g0, sc, sh],
                 [(Q_LORA, F32), (Q_LORA, MM), (KV_LORA, F32), (KV_LORA, MM), (HP, MM), (HP, MM), (HP, MM), (D, MM)])


ATT_HEADS = 4
ATT_BLOCK = 512
ATT_FWD_BLOCK = 1024


def _chunk_mask_t(tk, tq):
    ki = lax.broadcasted_iota(jnp.int32, (tk, tq), 0) // CHUNK
    qi = lax.broadcasted_iota(jnp.int32, (tk, tq), 1) // CHUNK
    return ki <= qi


def _attn_fwd(name, q, k, v):
    T = q.shape[0]
    tb = min(ATT_FWD_BLOCK, T)
    nb = T // tb
    nh = ATT_HEADS
    hs = [slice(h * HEAD_PAD, (h + 1) * HEAD_PAD) for h in range(nh)]

    def body(q_ref, k_ref, v_ref, o_ref, lse_ref):
        qb = pl.program_id(1)

        def k_block(k0, masked, st):
            new = []
            for h in range(nh):
                m, acc = st[h]
                s = _dot_nt(k_ref[pl.ds(k0, tb), hs[h]], q_ref[:, hs[h]])
                if masked:
                    s = jnp.where(_chunk_mask_t(tb, tb), s, NEG)
                m_new = jnp.maximum(m, jnp.max(s, axis=0, keepdims=True))
                alpha = jnp.exp((m - m_new) * ATT_SCALE)
                p = jnp.exp((s - m_new) * ATT_SCALE)
                acc = alpha * acc + _dot_tn(v_ref[pl.ds(k0, tb), hs[h]], p)
                new.append((m_new, acc))
            return tuple(new)

        st = tuple((jnp.full((1, tb), NEG, F32), jnp.zeros((HEAD_PAD, tb), F32)) for _ in range(nh))
        st = k_block(pl.multiple_of(qb * tb, tb), True, st)
        st = lax.fori_loop(0, qb, lambda kb, s_: k_block(pl.multiple_of(kb * tb, tb), False, s_), st)
        for h in range(nh):
            m, acc = st[h]
            l = acc[V_HEAD:V_HEAD + 1, :]
            o_ref[:, hs[h]] = (acc / l).T.astype(MM)
            lse_ref[h] = jnp.broadcast_to(m * ATT_SCALE + jnp.log(l), (8, tb))

    blk = pl.BlockSpec((tb, nh * HEAD_PAD), lambda g, i: (i, g))
    res = pl.BlockSpec((T, nh * HEAD_PAD), lambda g, i: (0, g))
    return pl.pallas_call(
        body, grid=(N_HEADS // nh, nb), in_specs=[blk, res, res],
        out_specs=[blk, pl.BlockSpec((nh, 8, tb), lambda g, i: (g, 0, i))],
        out_shape=[jax.ShapeDtypeStruct(q.shape, MM), jax.ShapeDtypeStruct((N_HEADS, 8, T), F32)], name=name,
        compiler_params=_cparams(("arbitrary", "arbitrary")))(q, k, v)


def _attn_bwd(name, q, k, v, do, lse, delta):
    T = q.shape[0]
    tb = min(ATT_BLOCK, T)
    nb = T // tb
    nh = ATT_HEADS
    hs = [slice(h * HEAD_PAD, (h + 1) * HEAD_PAD) for h in range(nh)]

    def body(q_ref, k_ref, v_ref, do_ref, lse_ref, dl_ref, dq_ref, dk_ref, dv_ref, dq_acc, dk_acc, dv_acc):
        kb = pl.program_id(1)

        @pl.when(kb == 0)
        def _():
            dq_acc[...] = jnp.zeros(dq_acc.shape, F32)


        def q_block(q0, masked):
            for h in range(nh):
                qh = q_ref[pl.ds(q0, tb), hs[h]]
                doh = do_ref[pl.ds(q0, tb), hs[h]]
                kh = k_ref[:, hs[h]]
                s = _dot_nt(kh, qh) * ATT_SCALE
                if masked:
                    s = jnp.where(_chunk_mask_t(tb, tb), s, NEG)
                p = jnp.exp(s - lse_ref[h, 0:1, pl.ds(q0, tb)])
                ds = (p * (_dot_nt(v_ref[:, hs[h]], doh) - dl_ref[h, 0:1, pl.ds(q0, tb)]) * ATT_SCALE).astype(MM)
                if masked:
                    dv_acc[:, hs[h]] = _dot(p, doh)
                    dk_acc[:, hs[h]] = _dot(ds, qh)
                else:
                    dv_acc[:, hs[h]] += _dot(p, doh)
                    dk_acc[:, hs[h]] += _dot(ds, qh)
                dq_acc[pl.ds(q0, tb), hs[h]] += _dot_tn(ds, kh)

        q_block(pl.multiple_of(kb * tb, tb), True)

        def rest(qb, c_):
            q_block(pl.multiple_of(qb * tb, tb), False)
            return c_

        lax.fori_loop(kb + 1, nb, rest, 0)
        dk_ref[...] = dk_acc[...].astype(MM)
        dv_ref[...] = dv_acc[...].astype(MM)

        @pl.when(kb == nb - 1)
        def _():
            dq_ref[...] = dq_acc[...].astype(MM)

    W = nh * HEAD_PAD
    blk = pl.BlockSpec((tb, W), lambda g, i: (i, g))
    res = pl.BlockSpec((T, W), lambda g, i: (0, g))
    rows = pl.BlockSpec((nh, 8, T), lambda g, i: (g, 0, 0))
    return pl.pallas_call(
        body, grid=(N_HEADS // nh, nb), in_specs=[res, blk, blk, res, rows, rows], out_specs=[res, blk, blk],
        out_shape=[jax.ShapeDtypeStruct(q.shape, MM)] * 3,
        scratch_shapes=[pltpu.VMEM((T, W), F32), pltpu.VMEM((tb, W), F32), pltpu.VMEM((tb, W), F32)],
        name=name, compiler_params=_cparams(("arbitrary", "arbitrary")))(q, k, v, do, lse, delta)


def _mla_proj_bwd(name, dq, dk, dv, C, S1, S2, cq_raw, ckv_raw, x, dxo, w_uq, w_ukv_k, w_ukv_v, w_dq, w_dkv, qg, kvg, g0, sc):
    T, D = x.shape
    HP = N_HEADS * HEAD_PAD

    def body(i, n, rr, cc, oo, aa, ss):
        Cv, S1v, S2v = rr[3][...], rr[4][...], rr[5][...]
        dq_pre = _unrope(rr[0][...].astype(F32), Cv, S1v, S2v).astype(MM)
        oo[0][...] = dq_pre
        dcq = _dot_nt(dq_pre, cc[0][...])
        dcq_raw, dqg = _rms_bwd(rr[6][...], cc[5][...], dcq)
        aa[0][...] += dqg
        dcq_raw = dcq_raw.astype(MM)
        oo[1][...] = dcq_raw
        dkv = rr[1][...]
        dkr = dkv[:, :HEAD_PAD].astype(F32)
        for hh in range(1, N_HEADS):
            dkr = dkr + dkv[:, hh * HEAD_PAD:(hh + 1) * HEAD_PAD].astype(F32)
        lane = lax.broadcasted_iota(jnp.int32, dkr.shape, 1)
        dkr = jnp.where((lane >= QK_NOPE) & (lane < QK_NOPE + QK_ROPE), _unrope(dkr, Cv, S1v, S2v), 0.0)
        dckv = _dot_nt(dkv, cc[1][...]) + _dot_nt(rr[2][...], cc[2][...])
        dckv_raw, dkvg = _rms_bwd(rr[7][...], cc[6][...], dckv)
        aa[1][...] += dkvg
        dckv_all = jnp.concatenate([dckv_raw, dkr], axis=1).astype(MM)
        oo[2][...] = dckv_all
        dh = _dot_nt(dcq_raw, cc[3][...]) + _dot_nt(dckv_all, cc[4][...])
        dx, dsh, dsc, dg0 = _prenorm_bwd(rr[8][...], cc[7][...], cc[8][...], dh)
        oo[3][...] = rr[9][...] + dx
        aa[2][...] += dsh
        aa[3][...] += dsc
        aa[4][...] += dg0

    return _rows(name, body, T, MLA_BWD_ROWS,
                 [(dq, 'cur'), (dk, 'cur'), (dv, 'cur'), (C, 'cur'), (S1, 'cur'), (S2, 'cur'), (cq_raw, 'cur'), (ckv_raw, 'cur'),
                  (x, 'cur'), (dxo, 'cur')],
                 [w_uq, w_ukv_k, w_ukv_v, w_dq, w_dkv, qg, kvg, g0, sc],
                 [(HP, MM), (Q_LORA, MM), (KV_LORA + HEAD_PAD, MM), (D, F32)],
                 accs=[(1, Q_LORA), (1, KV_LORA), (1, D), (1, D), (1, D)])


HALO = 32


def _windows(ext, tm, first):
    rolled = {0: ext}
    out = []
    for j in range(CONV_W):
        r = (first + j) % 8
        if r not in rolled:
            rolled[r] = pltpu.roll(ext, ext.shape[0] - r, 0)
        out.append(rolled[r][first + j - r:first + j - r + tm])
    return out


def _conv_glu(name, x, g0, sc, sh, w_pw1, b_pw1):
    T, D = x.shape

    def body(i, n, rr, cc, oo, aa, ss):
        hv = (_rms(rr[0][...], cc[2][...]) * (1.0 + cc[3][...]) + cc[4][...]).astype(MM)
        oo[2][...] = hv
        a = _dot(hv, cc[0][...]) + cc[1][...]
        oo[0][...] = a
        oo[1][...] = a[:, :D] * _sigmoid(a[:, D:])

    return _rows(name, body, T, 512, [(x, 'cur')], [w_pw1, b_pw1, g0, sc, sh], [(2 * D, F32), (D, F32), (D, MM)])


def _layernorm_parts(uc):
    xc = uc - jnp.mean(uc, axis=-1, keepdims=True)
    r = lax.rsqrt(jnp.mean(xc * xc, axis=-1, keepdims=True) + EPS)
    return xc * r, r


def _conv_dw(name, u, w_dw, b_dw, ln_g, ln_b, w_pw2, b_pw2, x, g1, gt):
    T, D = u.shape
    tm = min(256, T)

    def body(i, n, rr, cc, oo, aa, ss):
        ext = jnp.concatenate([jnp.where(i > 0, rr[1][tm - HALO:tm, :], 0.0), rr[0][...]], axis=0)
        uc = jnp.zeros((tm, D), F32) + cc[1][...]
        for kk, win in enumerate(_windows(ext, tm, HALO - (CONV_W - 1))):
            uc = uc + win * cc[0][kk:kk + 1, :]
        xh, _ = _layernorm_parts(uc)
        ln = xh * cc[2][...] + cc[3][...]
        z = (ln * _sigmoid(ln)).astype(MM)
        y = _dot(z, cc[4][...]) + cc[5][...]
        oo[0][...] = uc
        oo[1][...] = z
        oo[2][...] = y
        oo[3][...] = rr[2][...] + cc[7][...] * _rms(y, cc[6][...])

    return _rows(name, body, T, tm, [(u, 'cur'), (u, 'prev'), (x, 'cur')], [w_dw, b_dw, ln_g, ln_b, w_pw2, b_pw2, g1, gt],
                 [(D, F32), (D, MM), (D, F32), (D, F32)])


def _conv_bwd1(name, dxo, y, g1, gt, uc, w_pw2, ln_g, ln_b):
    T, D = uc.shape

    def body(i, n, rr, cc, oo, aa, ss):
        dy, dg1, dgt = _post_bwd_math(rr[0][...], rr[1][...], cc[3][...], cc[4][...])
        aa[3][...] += dg1
        aa[4][...] += dgt
        aa[5][...] += jnp.sum(dy, axis=0, keepdims=True)
        dy = dy.astype(MM)
        oo[1][...] = dy
        dz = _dot_nt(dy, cc[0][...])
        xh, r = _layernorm_parts(rr[2][...])
        g = cc[1][...]
        ln = xh * g + cc[2][...]
        sg = _sigmoid(ln)
        dln = dz * (sg * (1.0 + ln * (1.0 - sg)))
        aa[0][...] += jnp.sum(dln * xh, axis=0, keepdims=True)
        aa[1][...] += jnp.sum(dln, axis=0, keepdims=True)
        dxh = dln * g
        duc = r * (dxh - jnp.mean(dxh, axis=-1, keepdims=True) - xh * jnp.mean(dxh * xh, axis=-1, keepdims=True))
        aa[2][...] += jnp.sum(duc, axis=0, keepdims=True)
        oo[0][...] = duc

    return _rows(name, body, T, 512, [(dxo, 'cur'), (y, 'cur'), (uc, 'cur')], [w_pw2, ln_g, ln_b, g1, gt], [(D, F32), (D, MM)],
                 accs=[(1, D)] * 6)


def _conv_bwd2(name, duc, u, a, x, dxo, w_dw, w_pw1, g0, sc):
    T, D = u.shape
    tm = min(256, T)

    def body(i, n, rr, cc, oo, aa, ss):
        dcur = rr[0][...]
        extd = jnp.concatenate([dcur, jnp.where(i < n - 1, rr[1][0:HALO, :], 0.0)], axis=0)
        extu = jnp.concatenate([jnp.where(i > 0, rr[3][tm - HALO:tm, :], 0.0), rr[2][...]], axis=0)
        wd = _windows(extd, tm, 0)
        wu = _windows(extu, tm, HALO - (CONV_W - 1))
        du = jnp.zeros((tm, D), F32)
        for kk in range(CONV_W):
            du = du + wd[CONV_W - 1 - kk] * cc[0][kk:kk + 1, :]
            aa[0][kk:kk + 1, :] += jnp.sum(dcur * wu[kk], axis=0, keepdims=True)
        av = rr[4][...]
        a1, sg = av[:, :D], _sigmoid(av[:, D:])
        da = jnp.concatenate([du * sg, du * a1 * (sg * (1.0 - sg))], axis=1)
        aa[1][...] += jnp.sum(da, axis=0, keepdims=True)
        da = da.astype(MM)
        oo[0][...] = da
        dx, dsh, dsc, dg0 = _prenorm_bwd(rr[5][...], cc[2][...], cc[3][...], _dot_nt(da, cc[1][...]))
        oo[1][...] = rr[6][...] + dx
        aa[2][...] += dsh
        aa[3][...] += dsc
        aa[4][...] += dg0

    return _rows(name, body, T, tm,
                 [(duc, 'cur'), (duc, 'next'), (u, 'cur'), (u, 'prev'), (a, 'cur'), (x, 'cur'), (dxo, 'cur')],
                 [w_dw, w_pw1, g0, sc], [(2 * D, MM), (D, F32)],
                 accs=[(32, D), (1, 2 * D), (1, D), (1, D), (1, D)])


PHALO = 16


def _pool_fwd(name, h, w, b, scale, x, g1, gt):
    T, D = h.shape
    G = len(POOL_WINDOWS)
    Cg = D // G
    tm = min(512, T)

    def body(i, n, rr, cc, oo, aa, ss):
        ext = ss[0]
        ext[0:PHALO, :] = jnp.where(i > 0, rr[1][tm - PHALO:tm, :], 0.0)
        ext[PHALO:PHALO + tm, :] = rr[0][...]
        t_glob = i * tm + lax.broadcasted_iota(jnp.int32, (tm, 1), 0)
        ps, ys = [], []
        for g, win in enumerate(POOL_WINDOWS):
            cols = slice(g * Cg, (g + 1) * Cg)
            s = ext[pl.ds(PHALO, tm), cols]
            for j in range(1, win):
                s = s + ext[pl.ds(PHALO - j, tm), cols]
            cnt = jnp.minimum(t_glob + 1, win).astype(F32)
            p = (s / cnt - ext[pl.ds(PHALO, tm), cols]).astype(MM)
            ps.append(p)
            ys.append(_dot(p, cc[0][g]) + cc[1][:, cols])
        ypre = jnp.concatenate(ys, axis=1)
        y = ypre * cc[2][...]
        oo[0][...] = jnp.concatenate(ps, axis=1)
        oo[1][...] = ypre
        oo[2][...] = y
        oo[3][...] = rr[2][...] + cc[4][...] * _rms(y, cc[3][...])

    return _rows(name, body, T, tm, [(h, 'cur'), (h, 'prev'), (x, 'cur')], [w, b, scale, g1, gt],
                 [(D, MM), (D, F32), (D, F32), (D, F32)], scratch=[pltpu.VMEM((tm + PHALO, D), F32)])


def _pool_bwd1(name, dxo, y, g1, gt, ypre, scale, w):
    T, D = ypre.shape
    G = len(POOL_WINDOWS)
    Cg = D // G

    def body(i, n, rr, cc, oo, aa, ss):
        dyv, dg1, dgt = _post_bwd_math(rr[0][...], rr[1][...], cc[2][...], cc[3][...])
        aa[2][...] += dg1
        aa[3][...] += dgt
        aa[0][...] += jnp.sum(dyv * rr[2][...], axis=0, keepdims=True)
        dypre = dyv * cc[0][...]
        aa[1][...] += jnp.sum(dypre, axis=0, keepdims=True)
        dypre = dypre.astype(MM)
        oo[1][...] = dypre
        oo[0][...] = jnp.concatenate([_dot_nt(dypre[:, g * Cg:(g + 1) * Cg], cc[1][g]) for g in range(G)], axis=1)

    return _rows(name, body, T, 512, [(dxo, 'cur'), (y, 'cur'), (ypre, 'cur')], [scale, w, g1, gt], [(D, F32), (D, MM)],
                 accs=[(1, D)] * 4)


def _pool_bwd2(name, dp, x, dxo, g0, sc):
    T, D = x.shape
    G = len(POOL_WINDOWS)
    Cg = D // G
    tm = min(512, T)

    def body(i, n, rr, cc, oo, aa, ss):
        ext = ss[0]
        t_glob = i * tm + lax.broadcasted_iota(jnp.int32, (tm, 1), 0)
        dcur = rr[0][...]
        dhs = []
        for g, win in enumerate(POOL_WINDOWS):
            cols = slice(g * Cg, (g + 1) * Cg)
            cnt = jnp.minimum(t_glob + 1, win).astype(F32)
            ext[0:tm, cols] = dcur[:, cols] / cnt
            ext[tm:tm + PHALO, cols] = jnp.where(i < n - 1, rr[1][0:PHALO, cols] * (1.0 / win), 0.0)
        for g, win in enumerate(POOL_WINDOWS):
            cols = slice(g * Cg, (g + 1) * Cg)
            s = ext[pl.ds(0, tm), cols]
            for j in range(1, win):
                s = s + ext[pl.ds(j, tm), cols]
            dhs.append(s - dcur[:, cols])
        dx, dsh, dsc, dg0 = _prenorm_bwd(rr[2][...], cc[0][...], cc[1][...], jnp.concatenate(dhs, axis=1))
        oo[0][...] = rr[3][...] + dx
        aa[0][...] += dsh
        aa[1][...] += dsc
        aa[2][...] += dg0

    return _rows(name, body, T, tm, [(dp, 'cur'), (dp, 'next'), (x, 'cur'), (dxo, 'cur')], [g0, sc], [(D, F32)],
                 accs=[(1, D)] * 3, scratch=[pltpu.VMEM((tm + PHALO, D), F32)])


def _loss_head(x, tgt):
    T, D = x.shape

    def body(i, n, rr, cc, oo, aa, ss):
        err = rr[0][...] - rr[1][...]
        oo[0][...] = err * (1.0 / D)
        aa[0][...] += jnp.sum(err * err, axis=0, keepdims=True)

        @pl.when(i == n - 1)
        def _():
            aa[1][...] = jnp.broadcast_to(jnp.sum(aa[0][...], axis=1, keepdims=True) * (0.5 / D), (1, 128))

    dx, _, loss_row = _rows("loss_head", body, T, 512, [(x, 'cur'), (tgt, 'cur')], [], [(D, F32)], accs=[(1, D), (1, 128)])
    return dx, loss_row


def _adamw(name, w, g, m, v, after=None):
    shape = w.shape
    C = shape[-1]
    R = w.size // C
    w2, g2, m2, v2 = (t.reshape(R, C) for t in (w, g, m, v))
    br = R
    if R * C * 4 > (1 << 20):
        br = 8
        while br * 2 * C * 4 <= (1 << 20) and R % (br * 2) == 0:
            br *= 2
    b1c = 1.0 - ADAM_B1 ** ADAM_STEP
    b2c = 1.0 - ADAM_B2 ** ADAM_STEP

    def body(w_ref, g_ref, m_ref, v_ref, *rest):
        d_ref, mo_ref, vo_ref = rest[-3:]
        gv = g_ref[...]
        mn = ADAM_B1 * m_ref[...] + (1.0 - ADAM_B1) * gv
        vn = ADAM_B2 * v_ref[...] + (1.0 - ADAM_B2) * (gv * gv)
        d_ref[...] = -ADAM_LR * ((mn / b1c) / (jnp.sqrt(vn / b2c) + ADAM_EPS) + ADAM_WD * w_ref[...])
        mo_ref[...] = mn
        vo_ref[...] = vn

    spec = pl.BlockSpec((br, C), lambda r: (r, 0))
    extra = [] if after is None else [after]
    outs = pl.pallas_call(body, grid=(R // br,), in_specs=[spec] * 4 + [pl.BlockSpec(memory_space=pl.ANY)] * len(extra), out_specs=[spec] * 3,
                          out_shape=[jax.ShapeDtypeStruct((R, C), F32)] * 3, name=name,
                          compiler_params=_cparams(("arbitrary",)))(w2, g2, m2, v2, *extra)
    return tuple(t.reshape(shape) for t in outs)


def _layer_shards(g, ax):
    s = g.shape
    r = g.reshape(s[:ax] + (N_DEV, s[ax] // N_DEV) + s[ax + 1:])
    return (jnp.moveaxis(r, ax, 0) if ax else r).reshape(N_DEV, -1)


def _unshard(g, ax):
    r = jnp.moveaxis(g, 0, ax)
    s = r.shape
    return r.reshape(s[:ax] + (s[ax] * s[ax + 1],) + s[ax + 2:])


def _pack(parts, dtype, row_mult):
    lead = parts[0].shape[:-1]
    flat = jnp.concatenate([p.astype(dtype) for p in parts], axis=-1)
    n = flat.shape[-1]
    per = row_mult * 1024
    tot = -(-n // per) * per
    flat = jnp.pad(flat, [(0, 0)] * len(lead) + [(0, tot - n)])
    return flat.reshape(lead + (tot // 1024, 1024))


def _pad_heads(w, lo, hi):
    K = w.shape[0]
    r = w.reshape(K, N_HEADS, -1)[:, :, lo:hi]
    return jnp.pad(r, ((0, 0), (0, 0), (0, HEAD_PAD - (hi - lo)))).reshape(K, N_HEADS * HEAD_PAD)


def kernel(x, c, positions, ada_w, ada_b, norm_g, mla_w_dq, mla_q_norm_g, mla_w_uq, mla_w_dkv, mla_kv_norm_g, mla_w_ukv, mla_w_o, conv_w_pw1, conv_b_pw1, conv_w_dw, conv_b_dw, conv_ln_g, conv_ln_b, conv_w_pw2, conv_b_pw2, pool_w, pool_b, pool_scale, ffn_w1, ffn_w2, loss_target, m_ada_w, m_ada_b, m_norm_g, m_mla_w_dq, m_mla_q_norm_g, m_mla_w_uq, m_mla_w_dkv, m_mla_kv_norm_g, m_mla_w_ukv, m_mla_w_o, m_conv_w_pw1, m_conv_b_pw1, m_conv_w_dw, m_conv_b_dw, m_conv_ln_g, m_conv_ln_b, m_conv_w_pw2, m_conv_b_pw2, m_pool_w, m_pool_b, m_pool_scale, m_ffn_w1, m_ffn_w2, v_ada_w, v_ada_b, v_norm_g, v_mla_w_dq, v_mla_q_norm_g, v_mla_w_uq, v_mla_w_dkv, v_mla_kv_norm_g, v_mla_w_ukv, v_mla_w_o, v_conv_w_pw1, v_conv_b_pw1, v_conv_w_dw, v_conv_b_dw, v_conv_ln_g, v_conv_ln_b, v_conv_w_pw2, v_conv_b_pw2, v_pool_w, v_pool_b, v_pool_scale, v_ffn_w1, v_ffn_w2):
    args = dict(locals())
    W = {n: args[n] for n, _ in WEIGHTS}
    M1 = {n: args['m_' + n] for n, _ in WEIGHTS}
    V2 = {n: args['v_' + n] for n, _ in WEIGHTS}
    D = D_MODEL
    T = x.shape[1]
    L = ffn_w1.shape[0]
    xi, yi, ci = _place()
    me = 4 * xi + 2 * yi + ci
    n_ada = ada_w.shape[2]

    small_sizes = [W[n].size for n in SMALL]
    small_in = _pack([c.reshape(-1)] + [W[n].reshape(-1) for n in SMALL], F32, 8)
    small_all = _ag_small("ag_small_params", small_in).reshape(N_DEV, -1)
    c_all = small_all[:, :D]
    Ws = {}
    off = D
    for n, sz in zip(SMALL, small_sizes):
        Ws[n] = _unshard(small_all[:, off:off + sz].reshape((N_DEV,) + W[n].shape), SHARD_AXIS[n])
        off += sz
    c16 = jnp.pad(c_all, ((0, 16 - N_DEV), (0, 0)))

    ada_b_cols = lax.dynamic_slice_in_dim(ada_b, me * n_ada, n_ada, axis=1).reshape(L, 1, n_ada)
    mod_part = _mod_part(c16, ada_w, ada_b_cols)[:, :N_DEV]
    mod_all = _ag_small("ag_mod", mod_part.reshape(L * N_DEV, n_ada)).reshape(N_DEV, L, N_DEV, n_ada)
    mod_mine = lax.dynamic_index_in_dim(mod_all, me, axis=2, keepdims=False)
    mod = jnp.transpose(mod_mine, (1, 0, 2)).reshape(L, 6, 1, D)

    mla_names = [n for n in BIG if n.startswith('mla')]
    first_items = [(n, W[n][0]) for n in mla_names]
    later_items = [(n, W[n][1:]) for n in mla_names] + [(n, W[n]) for n in BIG if not n.startswith(('mla', 'ffn'))]
    first_all, = _ag_big("ag_weights", [_pack([a.reshape(-1) for _, a in first_items], MM, 32)])
    wf = [ffn_w1.astype(MM), ffn_w2.astype(MM), _pack([a.reshape(-1) for _, a in later_items], MM, 32)]
    wf, first_all, mod = lax.optimization_barrier((wf, first_all, mod))
    wf_land = [lax.dynamic_update_slice(lax.empty((N_DEV,) + w.shape, MM), w[None], (me,) + (0,) * w.ndim) for w in wf]
    ag_sems, wf_thru, wf_land, ag_token = _copies_start("ag_ffn_start", wf, wf_land, FIRST_LEVEL_PEERS, False)

    def unpack(g, items, dropped):
        flat, out, off = g.reshape(N_DEV, -1), {}, 0
        for n, a in items:
            out[n] = _unshard(flat[:, off:off + a.size].reshape((N_DEV,) + a.shape), SHARD_AXIS[n] - dropped)
            off += a.size
        return out

    n_mla = mla_w_dq.shape[0]
    w_dq, w_uq_p, w_ukv_k, w_ukv_v, w_dkv_p, w_o_p = ([None] * n_mla for _ in range(6))

    def set_mla(j, w):
        w_dq[j] = w['mla_w_dq']
        w_uq_p[j] = _pad_heads(w['mla_w_uq'], 0, QK_NOPE + QK_ROPE)
        w_ukv_k[j] = _pad_heads(w['mla_w_ukv'], 0, QK_NOPE)
        w_ukv_v[j] = _pad_heads(w['mla_w_ukv'], QK_NOPE, QK_NOPE + V_HEAD)
        w_dkv_p[j] = jnp.pad(jnp.concatenate([w['mla_w_dkv'][:, :KV_LORA], jnp.zeros((D, QK_NOPE), MM), w['mla_w_dkv'][:, KV_LORA:]], axis=1),
                             ((0, 0), (0, HEAD_PAD - QK_NOPE - QK_ROPE)))
        w_o_p[j] = jnp.pad(w['mla_w_o'].reshape(N_HEADS, V_HEAD, D), ((0, 0), (0, HEAD_PAD - V_HEAD), (0, 0))).reshape(N_HEADS * HEAD_PAD, D)

    set_mla(0, unpack(first_all, first_items, 1))
    w_dw32 = jnp.pad(Ws['conv_w_dw'], ((0, 0), (0, 32 - CONV_W), (0, 0)))
    row = lambda t: t.reshape(1, -1)

    half = QK_ROPE // 2
    inv_freq = ROPE_THETA ** (-jnp.arange(0, QK_ROPE, 2, dtype=F32) / QK_ROPE)
    invf = jnp.zeros((1, HEAD_PAD), F32).at[0, QK_NOPE:QK_NOPE + half].set(inv_freq).at[0, QK_NOPE + half:QK_NOPE + QK_ROPE].set(inv_freq)
    rC, rS1, rS2 = _rope_tables(positions.reshape(T, 1).astype(F32), invf)

    xs = x.reshape(T, D)
    saved = []
    for i in range(L):
        kind, j = i % 3, i // 3
        sh_m, sc_m, gt_m, sh_f, sc_f, gt_f = (mod[i, r] for r in range(6))
        g = [row(Ws['norm_g'][i, r]) for r in range(4)]
        st = dict(x0=xs)
        if i == 0:
            sc_m = sc_m + ag_token[0:1, 0:1]
        if kind == 0:
            cq_raw, cq, ckv_raw, ckv, q, k, v, h = _mla_proj(f"mla_proj{i}", xs, g[0], sc_m, sh_m, rC, rS1, rS2, w_dq[j],
                                                             row(Ws['mla_q_norm_g'][j]), w_uq_p[j], w_dkv_p[j],
                                                             row(Ws['mla_kv_norm_g'][j]), w_ukv_k[j], w_ukv_v[j])
            o, lse = _attn_fwd(f"attn_fwd{i}", q, k, v)
            y, xs = _mm_post(f"mla_out{i}", o, w_o_p[j], None, xs, g[1], gt_m)
            st.update(h=h, cq_raw=cq_raw, cq=cq, ckv_raw=ckv_raw, ckv=ckv, q=q, k=k, v=v, o=o, lse=lse, y=y)
        elif kind == 1:
            a, u, h = _conv_glu(f"conv_glu{i}", xs, g[0], sc_m, sh_m, w_pw1[j], row(W['conv_b_pw1'][j]))
            uc, z, y, xs = _conv_dw(f"conv_dw{i}", u, w_dw32[j], row(W['conv_b_dw'][j]), row(W['conv_ln_g'][j]), row(W['conv_ln_b'][j]),
                                    w_pw2[j], row(W['conv_b_pw2'][j]), xs, g[1], gt_m)
            st.update(h=h, a=a, u=u, uc=uc, z=z, y=y)
        else:
            h = _prenorm(f"prenorm_m{i}", xs, g[0], sc_m, sh_m, F32)
            p, ypre, y, xs = _pool_fwd(f"pool_fwd{i}", h, w_pool[j], row(Ws['pool_b'][j]), row(Ws['pool_scale'][j]), xs, g[1], gt_m)
            st.update(p=p, ypre=ypre, y=y)
        st['x1'] = xs
        if i == 0:
            wg = _copies_wait("ag_ffn_wait", ag_sems, wf_thru, wf_land, xs, FIRST_LEVEL_PEERS, False)
            w1g, w2g, later_all = _ag_forward("ag_ffn_forward", wg)
            later = unpack(later_all, later_items, 0)
            for jj in range(1, n_mla):
                set_mla(jj, {n: later[n][jj - 1] for n in mla_names})
            w_pw1, w_pw2, w_pool = later['conv_w_pw1'], later['conv_w_pw2'], later['pool_w']
        hf, af, yf, xs = _ffn_fwd(f"ffn_fwd{i}", i, xs, g[2], sc_f, sh_f, w1g, w2g, g[3], gt_f)
        st.update(hf=hf, af=af, yf=yf)
        saved.append(st)

    dx, loss_row = _loss_head(xs, loss_target.reshape(T, D))

    G = {}
    dmod = [None] * L
    dnorm = [None] * L
    rs_pending = None
    ffn_red = [lax.empty(ffn_w1.shape, F32), lax.empty(ffn_w2.shape, F32)]
    for i in reversed(range(L)):
        kind, j = i % 3, i // 3
        sh_m, sc_m, gt_m, sh_f, sc_f, gt_f = (mod[i, r] for r in range(6))
        g = [row(Ws['norm_g'][i, r]) for r in range(4)]
        st = saved[i]
        dy, da, dx, dg3, dgt_f, dsh_f, dsc_f, dg2 = _ffn_bwd(f"ffn_bwd{i}", i, st['yf'], g[3], gt_f, st['af'], w1g, w2g, st['x1'], dx, g[2], sc_f)
        wire1, own1 = _mm_tn_wire(f"ffn_dw1_{i}", st['hf'], da, me, False, False)
        wire2, own2 = _mm_tn_wire(f"ffn_dw2_{i}", st['af'], dy, me, True, True)
        if rs_pending is not None:
            ffn_red = _rs_finish(rs_pending, wire2, me, ffn_red)
        wires = [wire1, wire2]
        rs_sems, wires_thru, rs_lands, rs_token = _copies_start(f"rs_start{i}", wires, [lax.empty(w.shape, MM) for w in wires], ALL_PEERS, True)
        rs_pending = (i, rs_sems, wires_thru, rs_lands, [own1, own2])
        gt_m = gt_m + rs_token[0:1, 0:1]
        if kind == 0:
            dy, do, delta, dg1, dgt_m = _post_bwd_nt(f"mla_do{i}", dx, st['y'], g[1], gt_m, w_o_p[j], st['o'])
            dq, dk, dv = _attn_bwd(f"attn_bwd{i}", st['q'], st['k'], st['v'], do, st['lse'], delta)
            dq_pre, dcq_raw, dckv_all, dx, dqg, dkvg, dsh_m, dsc_m, dg0 = _mla_proj_bwd(
                f"mla_proj_bwd{i}", dq, dk, dv, rC, rS1, rS2, st['cq_raw'], st['ckv_raw'], st['x0'], dx, w_uq_p[j], w_ukv_k[j], w_ukv_v[j],
                w_dq[j], w_dkv_p[j], row(Ws['mla_q_norm_g'][j]), row(Ws['mla_kv_norm_g'][j]), g[0], sc_m)
            dwo = _mm_tn(f"mla_dwo{i}", st['o'], dy)
            dwuq = _mm_tn(f"mla_dwuq{i}", st['cq'], dq_pre)
            dwk = _mm_tn(f"mla_dwukvk{i}", st['ckv'], dk)
            dwv = _mm_tn(f"mla_dwukvv{i}", st['ckv'], dv)
            dwdq = _mm_tn(f"mla_dwdq{i}", st['h'], dcq_raw)
            dwdkv = _mm_tn(f"mla_dwdkv{i}", st['h'], dckv_all)
            G.setdefault('mla_w_o', [None] * n_mla)[j] = dwo.reshape(N_HEADS, HEAD_PAD, D)[:, :V_HEAD].reshape(N_HEADS * V_HEAD, D)
            G.setdefault('mla_w_uq', [None] * n_mla)[j] = dwuq.reshape(Q_LORA, N_HEADS, HEAD_PAD)[:, :, :QK_NOPE + QK_ROPE].reshape(Q_LORA, -1)
            G.setdefault('mla_w_ukv', [None] * n_mla)[j] = jnp.concatenate(
                [dwk.reshape(KV_LORA, N_HEADS, HEAD_PAD)[:, :, :QK_NOPE], dwv.reshape(KV_LORA, N_HEADS, HEAD_PAD)[:, :, :V_HEAD]], axis=2).reshape(KV_LORA, -1)
            G.setdefault('mla_w_dq', [None] * n_mla)[j] = dwdq
            G.setdefault('mla_w_dkv', [None] * n_mla)[j] = jnp.concatenate([dwdkv[:, :KV_LORA], dwdkv[:, KV_LORA + QK_NOPE:KV_LORA + QK_NOPE + QK_ROPE]], axis=1)
            G.setdefault('mla_q_norm_g', [None] * n_mla)[j] = dqg[0]
            G.setdefault('mla_kv_norm_g', [None] * n_mla)[j] = dkvg[0]
        elif kind == 1:
            duc, dy, dlng, dlnb, dbdw, dg1, dgt_m, dysum = _conv_bwd1(f"conv_bwd1_{i}", dx, st['y'], g[1], gt_m, st['uc'], w_pw2[j],
                                                                      row(W['conv_ln_g'][j]), row(W['conv_ln_b'][j]))
            da, dx, dwdw, dbpw1, dsh_m, dsc_m, dg0 = _conv_bwd2(f"conv_bwd2_{i}", duc, st['u'], st['a'], st['x0'], dx, w_dw32[j], w_pw1[j], g[0], sc_m)
            G['conv_w_pw2'] = [_mm_tn(f"conv_dwpw2_{i}", st['z'], dy)]
            G['conv_w_pw1'] = [_mm_tn(f"conv_dwpw1_{i}", st['h'], da)]
            G['conv_w_dw'] = [dwdw[:CONV_W]]
            G['conv_b_pw1'], G['conv_b_dw'], G['conv_ln_g'], G['conv_ln_b'], G['conv_b_pw2'] = [dbpw1[0]], [dbdw[0]], [dlng[0]], [dlnb[0]], [dysum[0]]
        else:
            dp, dypre, dscale, dpb, dg1, dgt_m = _pool_bwd1(f"pool_bwd1_{i}", dx, st['y'], g[1], gt_m, st['ypre'], row(Ws['pool_scale'][j]), w_pool[j])
            dx, dsh_m, dsc_m, dg0 = _pool_bwd2(f"pool_bwd2_{i}", dp, st['x0'], dx, g[0], sc_m)
            G['pool_w'] = [_mm_tn(f"pool_dw{i}", st['p'], dypre, diag=len(POOL_WINDOWS))]
            G['pool_b'] = [dpb.reshape(len(POOL_WINDOWS), -1)]
            G['pool_scale'] = [dscale[0]]
        dmod[i] = jnp.concatenate([dsh_m, dsc_m, dgt_m, dsh_f, dsc_f, dgt_f], axis=1)
        dnorm[i] = jnp.concatenate([dg0, dg1, dg2, dg3], axis=0)
    G['norm_g'] = dnorm
    grad_x = dx.reshape(x.shape)

    rs_names = [n for n, ax in WEIGHTS if ax is not None and n != 'ada_w' and not n.startswith('ffn')]
    pieces = [(n, _layer_shards(g, SHARD_AXIS[n] - 1)) for n in rs_names for g in G[n]]
    big = [(n, p) for n, p in pieces if p.shape[1] % (8 * 1024) == 0]
    small = [(n, p) for n, p in pieces if p.shape[1] % (8 * 1024) != 0]
    packed = jnp.concatenate([p.reshape(N_DEV, -1, 1024) for _, p in big] + [_pack([p for _, p in small], F32, 8)], axis=1)
    ffn_red = _rs_finish(rs_pending, dx, me, ffn_red)
    my_chip = 2 * xi + yi
    p4 = packed.reshape((4, 2) + packed.shape[1:])
    pair_recv, = _rs_pair("rs_pair", [p4])
    chip_wire, chip_own = _pair_sum("rs_pair_sum", p4, pair_recv, ci, my_chip)

    dmod_mine = jnp.concatenate(dmod, axis=1).reshape(-1)
    fin_in = _pack([dmod_mine] + [G[n][0].reshape(-1) for n in REPL] + [loss_row.reshape(-1)], F32, 8)
    fin_all = _ag_small("ag_final", fin_in)
    chip_wire, fin_all = lax.optimization_barrier((chip_wire, fin_all))
    chip_sems, chip_thru, chip_land, chip_token = _copies_start("rs_chips_start", [chip_wire], [lax.empty(chip_wire.shape, MM)],
                                                                CHIP_PEERS, 'chip')
    grads = {'ffn_w1': ffn_red[0], 'ffn_w2': ffn_red[1]}
    fin_sum = _sum_devices("final_sum", fin_all).reshape(-1)
    nm = L * 6 * D
    grads['ada_b'] = fin_sum[:nm].reshape(L, 6 * D)
    off = nm
    for n in REPL:
        grads[n] = fin_sum[off:off + W[n].size].reshape(W[n].shape)
        off += W[n].size
    loss = fin_sum[off]
    dmod_all = fin_all.reshape(N_DEV, -1)[:, :nm].reshape(N_DEV, L, 6 * D)
    dmod_cols = lax.dynamic_slice_in_dim(dmod_all, me * n_ada, n_ada, axis=2)
    dmod16 = jnp.pad(jnp.transpose(dmod_cols, (1, 0, 2)), ((0, 0), (0, 16 - N_DEV), (0, 0)))
    grads['ada_w'] = _ada_w_grad(c16, dmod16)
    deltas, new_m, new_v = {}, {}, {}
    done = chip_token
    for n in ['ffn_w1', 'ffn_w2', 'ada_w', 'ada_b'] + REPL:
        deltas[n], new_m[n], new_v[n] = _adamw("adamw_" + n, W[n], grads[n], M1[n], V2[n], after=done)
        done = deltas[n]
    chip_recv, = _copies_wait("rs_chips_wait", chip_sems, chip_thru, chip_land, done, CHIP_PEERS, 'chip')
    red = _chip_sum("rs_chip_sum", chip_own, chip_recv, my_chip)
    got = {}
    row0 = 0
    for n, p in big:
        rows = p.shape[1] // 1024
        got.setdefault(n, []).append(red[row0:row0 + rows])
        row0 += rows
    tail = red[row0:].reshape(-1)
    off = 0
    for n, p in small:
        got.setdefault(n, []).append(tail[off:off + p.shape[1]])
        off += p.shape[1]
    for n in rs_names:
        grads[n] = jnp.stack([g_.reshape(W[n].shape[1:]) for g_ in got[n]], axis=0)

    for n in rs_names:
        deltas[n], new_m[n], new_v[n] = _adamw("adamw_" + n, W[n], grads[n], M1[n], V2[n])
    names = [n for n, _ in WEIGHTS]
    return (loss, grad_x, *[grads[n] for n in names], *[deltas[n] for n in names], *[new_m[n] for n in names],
            *[new_v[n] for n in names])
```
